```python
import jax
import jax.numpy as jnp
from jax import lax
import numpy as np

D_MODEL = 1024
BATCH = 8
SEQ = 4096
DEPTH = 2

N_META = 16
EPS = 1e-6
GATE_CLAMP = 1.0 - 1e-6
CONV_DIM = D_MODEL // 2
CONV_K = 31
MLA_HEADS = D_MODEL // 128
Q_RANK = D_MODEL // 4
KV_RANK = D_MODEL // 8
NOPE_DIM = 64
ROPE_DIM = 32
V_DIM = 64
QK_DIM = NOPE_DIM + ROPE_DIM
ROPE_BASE = 10000.0
Q_BLOCK = 128
HGRN_HEADS = D_MODEL // 256
HGRN_DK = 128
HGRN_DV = (D_MODEL // 2) // HGRN_HEADS
HGRN_CHUNK = 64
D_FF = 4 * D_MODEL
N_BRANCH = 3

SPLIT_SIZES = (
    2 * CONV_DIM,
    Q_RANK,
    KV_RANK,
    ROPE_DIM,
    HGRN_HEADS * HGRN_DK,
    HGRN_HEADS * HGRN_DK,
    HGRN_HEADS * HGRN_DV,
    HGRN_HEADS * HGRN_DV,
    N_BRANCH * D_MODEL,
)
SPLIT_POINTS = tuple(int(s) for s in np.cumsum(SPLIT_SIZES)[:-1])
N_IN = int(sum(SPLIT_SIZES))

kernel_name = 'hybrid_conv_mla_hgrn2_block'


def rms_norm(x, g):
    xf = x.astype(jnp.float32)
    y = xf * lax.rsqrt(jnp.mean(xf * xf, axis=-1, keepdims=True) + EPS)
    return (y * g.astype(jnp.float32)).astype(x.dtype)


def layer_norm(x, g, b):
    xf = x.astype(jnp.float32)
    mu = jnp.mean(xf, axis=-1, keepdims=True)
    xc = xf - mu
    y = xc * lax.rsqrt(jnp.mean(xc * xc, axis=-1, keepdims=True) + EPS)
    return (y * g.astype(jnp.float32) + b.astype(jnp.float32)).astype(x.dtype)


def apply_rope(x, cos, sin):
    half = ROPE_DIM // 2
    xf = x.astype(jnp.float32)
    x1, x2 = xf[..., :half], xf[..., half:]
    out = jnp.concatenate([x1 * cos - x2 * sin, x1 * sin + x2 * cos], axis=-1)
    return out.astype(x.dtype)


def conv_module(u, conv_w, conv_b, ln_g, ln_b, w_proj):
    a, gt = jnp.split(u, 2, axis=-1)
    h = a * jax.nn.sigmoid(gt)
    h = lax.conv_general_dilated(
        h, conv_w[:, None, :].astype(h.dtype), window_strides=(1,), padding=[(CONV_K - 1, 0)],
        dimension_numbers=('NWC', 'WIO', 'NWC'), feature_group_count=CONV_DIM) + conv_b
    h = jax.nn.silu(layer_norm(h, ln_g, ln_b))
    return h @ w_proj


def causal_block_attention(q, k, v):
    B, L, H, Dq = q.shape
    n_blk = -(-L // Q_BLOCK)
    Lp = n_blk * Q_BLOCK
    qp = jnp.pad(q, ((0, 0), (0, Lp - L), (0, 0), (0, 0)))
    qb = qp.reshape(B, n_blk, Q_BLOCK, H, Dq).transpose(1, 0, 2, 3, 4)
    starts = jnp.arange(n_blk, dtype=jnp.int32) * Q_BLOCK
    k_pos = jnp.arange(L, dtype=jnp.int32)
    scale = Dq ** -0.5

    def one_block(args):
        q_blk, start = args
        s = jnp.einsum('bqhd,bkhd->bhqk', q_blk, k).astype(jnp.float32) * scale
        q_pos = start + jnp.arange(Q_BLOCK, dtype=jnp.int32)
        mask = k_pos[None, :] <= q_pos[:, None]
        s = jnp.where(mask, s, -1e30)
        p = jax.nn.softmax(s, axis=-1).astype(v.dtype)
        return jnp.einsum('bhqk,bkhd->bqhd', p, v)

    ob = lax.map(one_block, (qb, starts))
    return ob.transpose(1, 0, 2, 3, 4).reshape(B, Lp, H, V_DIM)[:, :L]


def mla(c_q, c_kv, k_rope, cos, sin, q_a_g, w_uq, kv_a_g, w_ukv, q_norm_g, k_norm_g, w_proj):
    B, L = c_q.shape[:2]
    q = (rms_norm(c_q, q_a_g) @ w_uq).reshape(B, L, MLA_HEADS, QK_DIM)
    kv = (rms_norm(c_kv, kv_a_g) @ w_ukv).reshape(B, L, MLA_HEADS, NOPE_DIM + V_DIM)
    k_nope, v = kv[..., :NOPE_DIM], kv[..., NOPE_DIM:]
    k_r = jnp.broadcast_to(k_rope[:, :, None, :], (B, L, MLA_HEADS, ROPE_DIM))
    k = jnp.concatenate([k_nope, k_r], axis=-1)
    q = rms_norm(q, q_norm_g)
    k = rms_norm(k, k_norm_g)
    q = jnp.concatenate([q[..., :NOPE_DIM], apply_rope(q[..., NOPE_DIM:], cos, sin)], axis=-1)
    k = jnp.concatenate([k[..., :NOPE_DIM], apply_rope(k[..., NOPE_DIM:], cos, sin)], axis=-1)
    o = causal_block_attention(q, k, v)
    return o.reshape(B, L, MLA_HEADS * V_DIM) @ w_proj


def hgrn2(q, f_raw, i, g, lb, norm_g, w_proj):
    B, L = q.shape[:2]
    f32 = jnp.float32
    lbf = lb.astype(f32)
    fr = f_raw.astype(f32)
    k = (1.0 - lbf) * jax.nn.sigmoid(-fr)
    log_f = jnp.log1p(-jnp.minimum(k, GATE_CLAMP))
    v = jax.nn.silu(i.astype(f32))
    qf = q.astype(f32)
    pad_front = (-N_META) % HGRN_CHUNK
    pad_back = (-(pad_front + L)) % HGRN_CHUNK
    pads = ((0, 0), (pad_front, pad_back), (0, 0))
    qf, k, v, log_f = [jnp.pad(a, pads) for a in (qf, k, v, log_f)]
    Lp = L + pad_front + pad_back
    nc = Lp // HGRN_CHUNK

    def to_chunks(a, d):
        return a.reshape(B, nc, HGRN_CHUNK, HGRN_HEADS, d).transpose(1, 0, 3, 2, 4)

    qc, kc, lfc = to_chunks(qf, HGRN_DK), to_chunks(k, HGRN_DK), to_chunks(log_f, HGRN_DK)
    vc = to_chunks(v, HGRN_DV)
    causal = jnp.tril(jnp.ones((HGRN_CHUNK, HGRN_CHUNK), dtype=bool))[:, :, None]

    def chunk_step(S, inp):
        q_c, k_c, v_c, lf_c = inp
        b = jnp.cumsum(lf_c, axis=2)
        o_inter = jnp.einsum('bhtk,bhkv->bhtv', q_c * jnp.exp(b), S)
        diff = b[:, :, :, None, :] - b[:, :, None, :, :]
        decay = jnp.where(causal, jnp.exp(jnp.where(causal, diff, 0.0)), 0.0)
        A = jnp.einsum('bhtk,bhtsk,bhsk->bhts', q_c, decay, k_c)
        o_intra = jnp.einsum('bhts,bhsv->bhtv', A, v_c)
        b_last = b[:, :, -1:, :]
        S_new = jnp.exp(b_last[:, :, 0, :])[..., None] * S + jnp.einsum(
            'bhsk,bhsv->bhkv', k_c * jnp.exp(b_last - b), v_c)
        return S_new, o_inter + o_intra

    S0 = jnp.zeros((B, HGRN_HEADS, HGRN_DK, HGRN_DV), f32)
    _, oc = lax.scan(chunk_step, S0, (qc, kc, vc, lfc))
    o = oc.transpose(1, 0, 3, 2, 4).reshape(B, Lp, HGRN_HEADS, HGRN_DV)[:, pad_front:pad_front + L]
    o = o * lax.rsqrt(jnp.mean(o * o, axis=-1, keepdims=True) + EPS)
    o = o.reshape(B, L, HGRN_HEADS * HGRN_DV) * norm_g.astype(f32)
    o = (o * jax.nn.silu(g.astype(f32))).astype(q.dtype)
    return o @ w_proj


def _fwd_setup_inputs(seed: int = 0) -> dict:
    key = jax.random.key(seed)
    ks = iter(jax.random.split(key, 32))

    def nrm(shape, scale):
        return jax.random.normal(next(ks), shape, jnp.float32) * scale

    def gain(shape):
        return 1.0 + 0.1 * nrm(shape, 1.0)

    D = D_MODEL
    return {
        'x': nrm((BATCH, SEQ, D), 1.0),
        'meta': nrm((N_META, D), 1.0),
        'norm1_g': gain((DEPTH, D)),
        'w_in': nrm((DEPTH, D, N_IN), D ** -0.5),
        'conv_w': nrm((DEPTH, CONV_K, CONV_DIM), CONV_K ** -0.5),
        'conv_b': nrm((DEPTH, CONV_DIM), 0.02),
        'conv_ln_g': gain((DEPTH, CONV_DIM)),
        'conv_ln_b': nrm((DEPTH, CONV_DIM), 0.02),
        'w_conv_out': nrm((DEPTH, CONV_DIM, D), CONV_DIM ** -0.5),
        'q_a_norm_g': gain((DEPTH, Q_RANK)),
        'w_uq': nrm((DEPTH, Q_RANK, MLA_HEADS * QK_DIM), Q_RANK ** -0.5),
        'kv_a_norm_g': gain((DEPTH, KV_RANK)),
        'w_ukv': nrm((DEPTH, KV_RANK, MLA_HEADS * (NOPE_DIM + V_DIM)), KV_RANK ** -0.5),
        'q_norm_g': gain((DEPTH, QK_DIM)),
        'k_norm_g': gain((DEPTH, QK_DIM)),
        'w_attn_out': nrm((DEPTH, MLA_HEADS * V_DIM, D), (MLA_HEADS * V_DIM) ** -0.5),
        'hgrn_lb_logits': nrm((DEPTH, HGRN_HEADS * HGRN_DK), 1.0),
        'hgrn_norm_g': gain((DEPTH, HGRN_HEADS * HGRN_DV)),
        'w_hgrn_out': nrm((DEPTH, HGRN_HEADS * HGRN_DV, D), (HGRN_HEADS * HGRN_DV) ** -0.5),
        'w_out': nrm((DEPTH, D, D), D ** -0.5),
        'norm2_g': gain((DEPTH, D)),
        'w_ff1': nrm((DEPTH, D, D_FF), D ** -0.5),
        'w_ff2': nrm((DEPTH, D_FF, D), D_FF ** -0.5),
    }


def _fwd_reference(x, meta, norm1_g, w_in, conv_w, conv_b, conv_ln_g, conv_ln_b, w_conv_out,
              q_a_norm_g, w_uq, kv_a_norm_g, w_ukv, q_norm_g, k_norm_g, w_attn_out,
              hgrn_lb_logits, hgrn_norm_g, w_hgrn_out, w_out, norm2_g, w_ff1, w_ff2):
    B = x.shape[0]
    D = D_MODEL
    x = jnp.concatenate([jnp.broadcast_to(meta[None].astype(x.dtype), (B, N_META, D)), x], axis=1)
    L = x.shape[1]
    half = ROPE_DIM // 2
    pos = jnp.arange(L, dtype=jnp.float32)
    inv_freq = ROPE_BASE ** (-jnp.arange(half, dtype=jnp.float32) / half)
    ang = pos[:, None] * inv_freq[None, :]
    cos = jnp.cos(ang)[None, :, None, :]
    sin = jnp.sin(ang)[None, :, None, :]
    p_lb = jax.nn.softmax(hgrn_lb_logits.astype(jnp.float32), axis=0)
    lower_bounds = jnp.cumsum(p_lb, axis=0) - p_lb[0:1]

    for l in range(DEPTH):
        h = rms_norm(x, norm1_g[l])
        u = h @ w_in[l]
        (u_conv, c_q, c_kv, k_rope, hq, hf, hi, hg, u_gate) = jnp.split(u, SPLIT_POINTS, axis=-1)
        y_a = conv_module(u_conv, conv_w[l], conv_b[l], conv_ln_g[l], conv_ln_b[l], w_conv_out[l])
        y_b = mla(c_q, c_kv, k_rope, cos, sin, q_a_norm_g[l], w_uq[l], kv_a_norm_g[l], w_ukv[l],
                  q_norm_g[l], k_norm_g[l], w_attn_out[l])
        y_c = hgrn2(hq, hf, hi, hg, lower_bounds[l], hgrn_norm_g[l], w_hgrn_out[l])
        gates = jax.nn.sigmoid(u_gate).reshape(B, L, N_BRANCH, D)
        mix = gates[:, :, 0] * y_a + gates[:, :, 1] * y_b + gates[:, :, 2] * y_c
        x = x + mix @ w_out[l]
        h2 = rms_norm(x, norm2_g[l])
        x = x + jnp.square(jax.nn.relu(h2 @ w_ff1[l])) @ w_ff2[l]

    return x[:, N_META:]


import jax as _jax
import jax.numpy as _jnp

TWIN_FORMAT = 'train_step'
FWD_PARAMS = ['x', 'meta', 'norm1_g', 'w_in', 'conv_w', 'conv_b', 'conv_ln_g', 'conv_ln_b', 'w_conv_out', 'q_a_norm_g', 'w_uq', 'kv_a_norm_g', 'w_ukv', 'q_norm_g', 'k_norm_g', 'w_attn_out', 'hgrn_lb_logits', 'hgrn_norm_g', 'w_hgrn_out', 'w_out', 'norm2_g', 'w_ff1', 'w_ff2']
TWIN_WEIGHTS = ['meta', 'norm1_g', 'w_in', 'conv_w', 'conv_b', 'conv_ln_g', 'conv_ln_b', 'w_conv_out', 'q_a_norm_g', 'w_uq', 'kv_a_norm_g', 'w_ukv', 'q_norm_g', 'k_norm_g', 'w_attn_out', 'hgrn_lb_logits', 'hgrn_norm_g', 'w_hgrn_out', 'w_out', 'norm2_g', 'w_ff1', 'w_ff2']
TWIN_DIFF_INPUT = 'x'
TWIN_INPUTS = ['x', 'meta', 'norm1_g', 'w_in', 'conv_w', 'conv_b', 'conv_ln_g', 'conv_ln_b', 'w_conv_out', 'q_a_norm_g', 'w_uq', 'kv_a_norm_g', 'w_ukv', 'q_norm_g', 'k_norm_g', 'w_attn_out', 'hgrn_lb_logits', 'hgrn_norm_g', 'w_hgrn_out', 'w_out', 'norm2_g', 'w_ff1', 'w_ff2', 'loss_target', 'm_meta', 'm_norm1_g', 'm_w_in', 'm_conv_w', 'm_conv_b', 'm_conv_ln_g', 'm_conv_ln_b', 'm_w_conv_out', 'm_q_a_norm_g', 'm_w_uq', 'm_kv_a_norm_g', 'm_w_ukv', 'm_q_norm_g', 'm_k_norm_g', 'm_w_attn_out', 'm_hgrn_lb_logits', 'm_hgrn_norm_g', 'm_w_hgrn_out', 'm_w_out', 'm_norm2_g', 'm_w_ff1', 'm_w_ff2', 'v_meta', 'v_norm1_g', 'v_w_in', 'v_conv_w', 'v_conv_b', 'v_conv_ln_g', 'v_conv_ln_b', 'v_w_conv_out', 'v_q_a_norm_g', 'v_w_uq', 'v_kv_a_norm_g', 'v_w_ukv', 'v_q_norm_g', 'v_k_norm_g', 'v_w_attn_out', 'v_hgrn_lb_logits', 'v_hgrn_norm_g', 'v_w_hgrn_out', 'v_w_out', 'v_norm2_g', 'v_w_ff1', 'v_w_ff2']
TWIN_OUTPUTS = ['loss', 'grad_x', 'grad_meta', 'grad_norm1_g', 'grad_w_in', 'grad_conv_w', 'grad_conv_b', 'grad_conv_ln_g', 'grad_conv_ln_b', 'grad_w_conv_out', 'grad_q_a_norm_g', 'grad_w_uq', 'grad_kv_a_norm_g', 'grad_w_ukv', 'grad_q_norm_g', 'grad_k_norm_g', 'grad_w_attn_out', 'grad_hgrn_lb_logits', 'grad_hgrn_norm_g', 'grad_w_hgrn_out', 'grad_w_out', 'grad_norm2_g', 'grad_w_ff1', 'grad_w_ff2', 'delta_meta', 'delta_norm1_g', 'delta_w_in', 'delta_conv_w', 'delta_conv_b', 'delta_conv_ln_g', 'delta_conv_ln_b', 'delta_w_conv_out', 'delta_q_a_norm_g', 'delta_w_uq', 'delta_kv_a_norm_g', 'delta_w_ukv', 'delta_q_norm_g', 'delta_k_norm_g', 'delta_w_attn_out', 'delta_hgrn_lb_logits', 'delta_hgrn_norm_g', 'delta_w_hgrn_out', 'delta_w_out', 'delta_norm2_g', 'delta_w_ff1', 'delta_w_ff2', 'new_m_meta', 'new_m_norm1_g', 'new_m_w_in', 'new_m_conv_w', 'new_m_conv_b', 'new_m_conv_ln_g', 'new_m_conv_ln_b', 'new_m_w_conv_out', 'new_m_q_a_norm_g', 'new_m_w_uq', 'new_m_kv_a_norm_g', 'new_m_w_ukv', 'new_m_q_norm_g', 'new_m_k_norm_g', 'new_m_w_attn_out', 'new_m_hgrn_lb_logits', 'new_m_hgrn_norm_g', 'new_m_w_hgrn_out', 'new_m_w_out', 'new_m_norm2_g', 'new_m_w_ff1', 'new_m_w_ff2', 'new_v_meta', 'new_v_norm1_g', 'new_v_w_in', 'new_v_conv_w', 'new_v_conv_b', 'new_v_conv_ln_g', 'new_v_conv_ln_b', 'new_v_w_conv_out', 'new_v_q_a_norm_g', 'new_v_w_uq', 'new_v_kv_a_norm_g', 'new_v_w_ukv', 'new_v_q_norm_g', 'new_v_k_norm_g', 'new_v_w_attn_out', 'new_v_hgrn_lb_logits', 'new_v_hgrn_norm_g', 'new_v_w_hgrn_out', 'new_v_w_out', 'new_v_norm2_g', 'new_v_w_ff1', 'new_v_w_ff2']
TWIN_LEAF_KINDS = {'loss': 'loss', 'grad_x': 'grad_x', 'grad_meta': 'grad_w', 'grad_norm1_g': 'grad_w', 'grad_w_in': 'grad_w', 'grad_conv_w': 'grad_w', 'grad_conv_b': 'grad_w', 'grad_conv_ln_g': 'grad_w', 'grad_conv_ln_b': 'grad_w', 'grad_w_conv_out': 'grad_w', 'grad_q_a_norm_g': 'grad_w', 'grad_w_uq': 'grad_w', 'grad_kv_a_norm_g': 'grad_w', 'grad_w_ukv': 'grad_w', 'grad_q_norm_g': 'grad_w', 'grad_k_norm_g': 'grad_w', 'grad_w_attn_out': 'grad_w', 'grad_hgrn_lb_logits': 'grad_w', 'grad_hgrn_norm_g': 'grad_w', 'grad_w_hgrn_out': 'grad_w', 'grad_w_out': 'grad_w', 'grad_norm2_g': 'grad_w', 'grad_w_ff1': 'grad_w', 'grad_w_ff2': 'grad_w', 'delta_meta': 'delta_w', 'delta_norm1_g': 'delta_w', 'delta_w_in': 'delta_w', 'delta_conv_w': 'delta_w', 'delta_conv_b': 'delta_w', 'delta_conv_ln_g': 'delta_w', 'delta_conv_ln_b': 'delta_w', 'delta_w_conv_out': 'delta_w', 'delta_q_a_norm_g': 'delta_w', 'delta_w_uq': 'delta_w', 'delta_kv_a_norm_g': 'delta_w', 'delta_w_ukv': 'delta_w', 'delta_q_norm_g': 'delta_w', 'delta_k_norm_g': 'delta_w', 'delta_w_attn_out': 'delta_w', 'delta_hgrn_lb_logits': 'delta_w', 'delta_hgrn_norm_g': 'delta_w', 'delta_w_hgrn_out': 'delta_w', 'delta_w_out': 'delta_w', 'delta_norm2_g': 'delta_w', 'delta_w_ff1': 'delta_w', 'delta_w_ff2': 'delta_w', 'new_m_meta': 'new_m', 'new_m_norm1_g': 'new_m', 'new_m_w_in': 'new_m', 'new_m_conv_w': 'new_m', 'new_m_conv_b': 'new_m', 'new_m_conv_ln_g': 'new_m', 'new_m_conv_ln_b': 'new_m', 'new_m_w_conv_out': 'new_m', 'new_m_q_a_norm_g': 'new_m', 'new_m_w_uq': 'new_m', 'new_m_kv_a_norm_g': 'new_m', 'new_m_w_ukv': 'new_m', 'new_m_q_norm_g': 'new_m', 'new_m_k_norm_g': 'new_m', 'new_m_w_attn_out': 'new_m', 'new_m_hgrn_lb_logits': 'new_m', 'new_m_hgrn_norm_g': 'new_m', 'new_m_w_hgrn_out': 'new_m', 'new_m_w_out': 'new_m', 'new_m_norm2_g': 'new_m', 'new_m_w_ff1': 'new_m', 'new_m_w_ff2': 'new_m', 'new_v_meta': 'new_v', 'new_v_norm1_g': 'new_v', 'new_v_w_in': 'new_v', 'new_v_conv_w': 'new_v', 'new_v_conv_b': 'new_v', 'new_v_conv_ln_g': 'new_v', 'new_v_conv_ln_b': 'new_v', 'new_v_w_conv_out': 'new_v', 'new_v_q_a_norm_g': 'new_v', 'new_v_w_uq': 'new_v', 'new_v_kv_a_norm_g': 'new_v', 'new_v_w_ukv': 'new_v', 'new_v_q_norm_g': 'new_v', 'new_v_k_norm_g': 'new_v', 'new_v_w_attn_out': 'new_v', 'new_v_hgrn_lb_logits': 'new_v', 'new_v_hgrn_norm_g': 'new_v', 'new_v_w_hgrn_out': 'new_v', 'new_v_w_out': 'new_v', 'new_v_norm2_g': 'new_v', 'new_v_w_ff1': 'new_v', 'new_v_w_ff2': 'new_v'}


def _forward(args):
    return _fwd_reference(*[args[k] for k in FWD_PARAMS])


def _output_shape():
    out = _jax.eval_shape(lambda: _forward(_fwd_setup_inputs(0)))
    return out.shape, out.dtype

N_MICROBATCH = 1
ADAM_LR = 0.001
ADAM_B1 = 0.9
ADAM_B2 = 0.999
ADAM_EPS = 1e-08
ADAM_WD = 0.01
ADAM_STEP = 10
PER_EXAMPLE_BATCH_AXIS = {'x': 0, 'loss_target': 0}
SHARED_INPUTS = []
_WEIGHT_DTYPES = {'meta': _jnp.float32, 'norm1_g': _jnp.float32, 'w_in': _jnp.float32, 'conv_w': _jnp.float32, 'conv_b': _jnp.float32, 'conv_ln_g': _jnp.float32, 'conv_ln_b': _jnp.float32, 'w_conv_out': _jnp.float32, 'q_a_norm_g': _jnp.float32, 'w_uq': _jnp.float32, 'kv_a_norm_g': _jnp.float32, 'w_ukv': _jnp.float32, 'q_norm_g': _jnp.float32, 'k_norm_g': _jnp.float32, 'w_attn_out': _jnp.float32, 'hgrn_lb_logits': _jnp.float32, 'hgrn_norm_g': _jnp.float32, 'w_hgrn_out': _jnp.float32, 'w_out': _jnp.float32, 'norm2_g': _jnp.float32, 'w_ff1': _jnp.float32, 'w_ff2': _jnp.float32}
MOMENT_SCALE = {'meta': 8.897228e-02, 'norm1_g': 5.984249e+00, 'w_in': 1.474821e+00, 'conv_w': 2.503272e+00, 'conv_b': 2.637568e+01, 'conv_ln_g': 1.327922e+01, 'conv_ln_b': 1.552374e+01, 'w_conv_out': 4.617453e+00, 'q_a_norm_g': 2.728206e-01, 'w_uq': 1.840652e-01, 'kv_a_norm_g': 1.104831e+01, 'w_ukv': 3.925495e+00, 'q_norm_g': 8.341818e-01, 'k_norm_g': 8.763603e-01, 'w_attn_out': 3.564525e+00, 'hgrn_lb_logits': 1.299549e-01, 'hgrn_norm_g': 8.171237e+00, 'w_hgrn_out': 4.565319e-01, 'w_out': 5.734626e+00, 'norm2_g': 9.994856e+01, 'w_ff1': 4.271599e+00, 'w_ff2': 1.696134e+01}


def _to_microbatches(a, axis):
    t = _jnp.moveaxis(a, axis, 0)
    t = t.reshape((N_MICROBATCH, t.shape[0] // N_MICROBATCH) + t.shape[1:])
    return _jnp.moveaxis(t, 1, axis + 1)


def setup_inputs(seed: int = 0) -> dict:
    inp = _fwd_setup_inputs(seed)
    key = _jax.random.fold_in(_jax.random.key(seed), 7919)
    shape, _ = _output_shape()
    out = dict(inp)
    out["loss_target"] = _jax.random.normal(_jax.random.fold_in(key, 0), shape, _jnp.float32)
    for i, name in enumerate(TWIN_WEIGHTS):
        w = inp[name].astype(_jnp.float32)
        if MOMENT_SCALE is None:
            s = _jnp.sqrt(_jnp.mean(_jnp.square(w)) + 1e-30)
        else:
            s = MOMENT_SCALE[name]
        km, kv = _jax.random.split(_jax.random.fold_in(key, i + 1))
        out[name] = w
        out["m_" + name] = s * _jax.random.normal(km, w.shape, _jnp.float32)
        out["v_" + name] = (s * s) * _jax.random.uniform(kv, w.shape, _jnp.float32, 0.5, 1.5)
    if N_MICROBATCH > 1:
        for name, axis in PER_EXAMPLE_BATCH_AXIS.items():
            out[name] = _to_microbatches(out[name], axis)
    return {'x': out['x'], 'meta': out['meta'], 'norm1_g': out['norm1_g'], 'w_in': out['w_in'], 'conv_w': out['conv_w'], 'conv_b': out['conv_b'], 'conv_ln_g': out['conv_ln_g'], 'conv_ln_b': out['conv_ln_b'], 'w_conv_out': out['w_conv_out'], 'q_a_norm_g': out['q_a_norm_g'], 'w_uq': out['w_uq'], 'kv_a_norm_g': out['kv_a_norm_g'], 'w_ukv': out['w_ukv'], 'q_norm_g': out['q_norm_g'], 'k_norm_g': out['k_norm_g'], 'w_attn_out': out['w_attn_out'], 'hgrn_lb_logits': out['hgrn_lb_logits'], 'hgrn_norm_g': out['hgrn_norm_g'], 'w_hgrn_out': out['w_hgrn_out'], 'w_out': out['w_out'], 'norm2_g': out['norm2_g'], 'w_ff1': out['w_ff1'], 'w_ff2': out['w_ff2'], 'loss_target': out['loss_target'], 'm_meta': out['m_meta'], 'm_norm1_g': out['m_norm1_g'], 'm_w_in': out['m_w_in'], 'm_conv_w': out['m_conv_w'], 'm_conv_b': out['m_conv_b'], 'm_conv_ln_g': out['m_conv_ln_g'], 'm_conv_ln_b': out['m_conv_ln_b'], 'm_w_conv_out': out['m_w_conv_out'], 'm_q_a_norm_g': out['m_q_a_norm_g'], 'm_w_uq': out['m_w_uq'], 'm_kv_a_norm_g': out['m_kv_a_norm_g'], 'm_w_ukv': out['m_w_ukv'], 'm_q_norm_g': out['m_q_norm_g'], 'm_k_norm_g': out['m_k_norm_g'], 'm_w_attn_out': out['m_w_attn_out'], 'm_hgrn_lb_logits': out['m_hgrn_lb_logits'], 'm_hgrn_norm_g': out['m_hgrn_norm_g'], 'm_w_hgrn_out': out['m_w_hgrn_out'], 'm_w_out': out['m_w_out'], 'm_norm2_g': out['m_norm2_g'], 'm_w_ff1': out['m_w_ff1'], 'm_w_ff2': out['m_w_ff2'], 'v_meta': out['v_meta'], 'v_norm1_g': out['v_norm1_g'], 'v_w_in': out['v_w_in'], 'v_conv_w': out['v_conv_w'], 'v_conv_b': out['v_conv_b'], 'v_conv_ln_g': out['v_conv_ln_g'], 'v_conv_ln_b': out['v_conv_ln_b'], 'v_w_conv_out': out['v_w_conv_out'], 'v_q_a_norm_g': out['v_q_a_norm_g'], 'v_w_uq': out['v_w_uq'], 'v_kv_a_norm_g': out['v_kv_a_norm_g'], 'v_w_ukv': out['v_w_ukv'], 'v_q_norm_g': out['v_q_norm_g'], 'v_k_norm_g': out['v_k_norm_g'], 'v_w_attn_out': out['v_w_attn_out'], 'v_hgrn_lb_logits': out['v_hgrn_lb_logits'], 'v_hgrn_norm_g': out['v_hgrn_norm_g'], 'v_w_hgrn_out': out['v_w_hgrn_out'], 'v_w_out': out['v_w_out'], 'v_norm2_g': out['v_norm2_g'], 'v_w_ff1': out['v_w_ff1'], 'v_w_ff2': out['v_w_ff2']}


def _loss(weights, diff, rest, loss_target):
    with _jax.named_scope("forward"):
        args = {**rest, TWIN_DIFF_INPUT: diff, **{k: w.astype(_WEIGHT_DTYPES[k]) for k, w in weights.items()}}
        y = _forward(args)
    with _jax.named_scope("loss_head"):
        err = _jnp.square(y.astype(_jnp.float32) - loss_target)
        return 0.5 * _jnp.sum(_jnp.mean(err, axis=-1)) if err.ndim else 0.5 * err


def _adamw(w, g, m, v):
    m = ADAM_B1 * m + (1.0 - ADAM_B1) * g
    v = ADAM_B2 * v + (1.0 - ADAM_B2) * _jnp.square(g)
    m_hat = m / (1.0 - ADAM_B1 ** ADAM_STEP)
    v_hat = v / (1.0 - ADAM_B2 ** ADAM_STEP)
    delta = -ADAM_LR * (m_hat / (_jnp.sqrt(v_hat) + ADAM_EPS) + ADAM_WD * w)
    return delta, m, v


def reference(x, meta, norm1_g, w_in, conv_w, conv_b, conv_ln_g, conv_ln_b, w_conv_out, q_a_norm_g, w_uq, kv_a_norm_g, w_ukv, q_norm_g, k_norm_g, w_attn_out, hgrn_lb_logits, hgrn_norm_g, w_hgrn_out, w_out, norm2_g, w_ff1, w_ff2, loss_target, m_meta, m_norm1_g, m_w_in, m_conv_w, m_conv_b, m_conv_ln_g, m_conv_ln_b, m_w_conv_out, m_q_a_norm_g, m_w_uq, m_kv_a_norm_g, m_w_ukv, m_q_norm_g, m_k_norm_g, m_w_attn_out, m_hgrn_lb_logits, m_hgrn_norm_g, m_w_hgrn_out, m_w_out, m_norm2_g, m_w_ff1, m_w_ff2, v_meta, v_norm1_g, v_w_in, v_conv_w, v_conv_b, v_conv_ln_g, v_conv_ln_b, v_w_conv_out, v_q_a_norm_g, v_w_uq, v_kv_a_norm_g, v_w_ukv, v_q_norm_g, v_k_norm_g, v_w_attn_out, v_hgrn_lb_logits, v_hgrn_norm_g, v_w_hgrn_out, v_w_out, v_norm2_g, v_w_ff1, v_w_ff2):
    given = dict(x=x, meta=meta, norm1_g=norm1_g, w_in=w_in, conv_w=conv_w, conv_b=conv_b, conv_ln_g=conv_ln_g, conv_ln_b=conv_ln_b, w_conv_out=w_conv_out, q_a_norm_g=q_a_norm_g, w_uq=w_uq, kv_a_norm_g=kv_a_norm_g, w_ukv=w_ukv, q_norm_g=q_norm_g, k_norm_g=k_norm_g, w_attn_out=w_attn_out, hgrn_lb_logits=hgrn_lb_logits, hgrn_norm_g=hgrn_norm_g, w_hgrn_out=w_hgrn_out, w_out=w_out, norm2_g=norm2_g, w_ff1=w_ff1, w_ff2=w_ff2, loss_target=loss_target, m_meta=m_meta, m_norm1_g=m_norm1_g, m_w_in=m_w_in, m_conv_w=m_conv_w, m_conv_b=m_conv_b, m_conv_ln_g=m_conv_ln_g, m_conv_ln_b=m_conv_ln_b, m_w_conv_out=m_w_conv_out, m_q_a_norm_g=m_q_a_norm_g, m_w_uq=m_w_uq, m_kv_a_norm_g=m_kv_a_norm_g, m_w_ukv=m_w_ukv, m_q_norm_g=m_q_norm_g, m_k_norm_g=m_k_norm_g, m_w_attn_out=m_w_attn_out, m_hgrn_lb_logits=m_hgrn_lb_logits, m_hgrn_norm_g=m_hgrn_norm_g, m_w_hgrn_out=m_w_hgrn_out, m_w_out=m_w_out, m_norm2_g=m_norm2_g, m_w_ff1=m_w_ff1, m_w_ff2=m_w_ff2, v_meta=v_meta, v_norm1_g=v_norm1_g, v_w_in=v_w_in, v_conv_w=v_conv_w, v_conv_b=v_conv_b, v_conv_ln_g=v_conv_ln_g, v_conv_ln_b=v_conv_ln_b, v_w_conv_out=v_w_conv_out, v_q_a_norm_g=v_q_a_norm_g, v_w_uq=v_w_uq, v_kv_a_norm_g=v_kv_a_norm_g, v_w_ukv=v_w_ukv, v_q_norm_g=v_q_norm_g, v_k_norm_g=v_k_norm_g, v_w_attn_out=v_w_attn_out, v_hgrn_lb_logits=v_hgrn_lb_logits, v_hgrn_norm_g=v_hgrn_norm_g, v_w_hgrn_out=v_w_hgrn_out, v_w_out=v_w_out, v_norm2_g=v_norm2_g, v_w_ff1=v_w_ff1, v_w_ff2=v_w_ff2)
    weights = {n: given[n] for n in TWIN_WEIGHTS}
    shared = {n: given[n] for n in SHARED_INPUTS}
    per_example = {n: given[n] for n in ['x']}
    grad_fn = _jax.value_and_grad(_loss, argnums=(0, 1))

    def one_microbatch(ex, loss_target):
        ex = dict(ex)
        diff = ex.pop(TWIN_DIFF_INPUT)
        return grad_fn(weights, diff, {**shared, **ex}, loss_target)

    if N_MICROBATCH == 1:
        loss, (grad_w, grad_x) = one_microbatch(per_example, given["loss_target"])
    else:
        def body(carry, xs):
            loss_sum, grad_sum = carry
            l_k, (gw_k, gx_k) = one_microbatch(xs[0], xs[1])
            with _jax.named_scope("update"):
                return (loss_sum + l_k, _jax.tree.map(_jnp.add, grad_sum, gw_k)), gx_k

        init = (_jnp.zeros((), _jnp.float32), _jax.tree.map(_jnp.zeros_like, weights))
        (loss, grad_w), grad_x = _jax.lax.scan(body, init, (per_example, given["loss_target"]))
    with _jax.named_scope("update"):
        delta_w, new_m, new_v = {}, {}, {}
        for n in TWIN_WEIGHTS:
            delta_w[n], new_m[n], new_v[n] = _adamw(weights[n], grad_w[n], given["m_" + n], given["v_" + n])
    return (loss, grad_x, *[grad_w[n] for n in TWIN_WEIGHTS], *[delta_w[n] for n in TWIN_WEIGHTS],
            *[new_m[n] for n in TWIN_WEIGHTS], *[new_v[n] for n in TWIN_WEIGHTS])
```

```python
import functools

import numpy as np
import jax
import jax.numpy as jnp
from jax import lax
from jax.experimental import pallas as pl
from jax.experimental.pallas import tpu as pltpu

F32 = jnp.float32
BF16 = jnp.bfloat16

D_MODEL = 1024
DEPTH = 2
N_META = 16
PAD_FRONT = 112
ROW0 = PAD_FRONT + N_META
EPS = 1e-6
GATE_CLAMP = 1.0 - 1e-6
CONV_DIM = 512
CONV_K = 31
HEADS = 8
Q_RANK = 256
KV_RANK = 128
NOPE = 64
ROPE = 32
V_DIM = 64
QK_DIM = NOPE + ROPE
HEAD_W = 128
ROPE_BASE = 10000.0
HG_HEADS = 4
HG_DK = 128
HG_DV = 128
HG_CHUNK = 64
D_FF = 4096
N_IN = 6560
C_CONV_A, C_CONV_G, C_GATE, C_CQ, C_CKV, C_KR, C_HQ, C_HF, C_HI, C_HG = (
    0, 512, 1024, 4096, 4352, 4480, 4608, 5120, 5632, 6144)
N_IN_P = 6656
O_CQ, O_KR, O_HQ, O_GATE = 1024, 1408, 1440, 3488
KR_LANE = NOPE

ADAM_LR = 0.001
ADAM_B1 = 0.9
ADAM_B2 = 0.999
ADAM_EPS = 1e-08
ADAM_WD = 0.01
ADAM_STEP = 10

N_DEV = 8
VMEM_LIMIT = 56 * 1024 * 1024
MESH = pl.DeviceIdType.MESH


def _pick(n, cands):
    for c in cands:
        if n % c == 0:
            return c
    raise ValueError(f"no tile for {n}")


def _cparams(sem, **kw):
    return pltpu.CompilerParams(dimension_semantics=sem, vmem_limit_bytes=VMEM_LIMIT, **kw)


def _mm(a, b, *, ta=False, tb=False, out_dtype=F32, res=None, name):
    M, K = (a.shape[1], a.shape[0]) if ta else a.shape
    N = b.shape[0] if tb else b.shape[1]
    assert (b.shape[1] if tb else b.shape[0]) == K, (a.shape, b.shape, ta, tb)
    tm = _pick(M, (1056, 1024, 512, 384, 256, 128, 96))
    tn = _pick(N, (512, 384, 256, 128))
    tk = _pick(K, (1056, 1024, 512, 384, 256, 128, 96))
    nk = K // tk
    dims = (((0 if ta else 1,), (1 if tb else 0,)), ((), ()))

    def body(*refs):
        a_ref, b_ref = refs[0], refs[1]
        r_ref = refs[2] if res is not None else None
        o_ref = refs[3] if res is not None else refs[2]
        acc = refs[-1] if nk > 1 else None
        k = pl.program_id(2)
        p = lax.dot_general(a_ref[...].astype(BF16), b_ref[...].astype(BF16), dims,
                            preferred_element_type=F32)

        def finish(total):
            if r_ref is not None:
                total = total + r_ref[...].astype(F32)
            o_ref[...] = total.astype(o_ref.dtype)

        if nk == 1:
            finish(p)
        else:
            @pl.when(k == 0)
            def _():
                acc[...] = p

            @pl.when(k > 0)
            def _():
                acc[...] += p

            @pl.when(k == nk - 1)
            def _():
                finish(acc[...])

    a_spec = pl.BlockSpec((tk, tm), lambda i, j, k: (k, i)) if ta else pl.BlockSpec((tm, tk), lambda i, j, k: (i, k))
    b_spec = pl.BlockSpec((tn, tk), lambda i, j, k: (j, k)) if tb else pl.BlockSpec((tk, tn), lambda i, j, k: (k, j))
    o_spec = pl.BlockSpec((tm, tn), lambda i, j, k: (i, j))
    in_specs = [a_spec, b_spec] + ([o_spec] if res is not None else [])
    args = (a, b) + ((res,) if res is not None else ())
    return pl.pallas_call(
        body, name=name, grid=(M // tm, N // tn, nk), in_specs=in_specs, out_specs=o_spec,
        out_shape=jax.ShapeDtypeStruct((M, N), out_dtype),
        scratch_shapes=[pltpu.VMEM((tm, tn), F32)] if nk > 1 else [],
        compiler_params=_cparams(("parallel", "parallel", "arbitrary")),
    )(*args)


class Row:
    def __init__(self, arr, width=None, col=0, piece=None):
        self.arr = arr
        self.width = arr.shape[1] if width is None else width
        assert col % self.width == 0
        self.blk = col // self.width
        self.piece = self.width if piece is None else piece

    def spec(self, tm):
        blk = self.blk
        return pl.BlockSpec((tm, self.width), lambda i: (i, blk))


def _split(v, piece):
    w = v.shape[-1]
    if piece == w:
        return v
    return [v[:, j * piece:(j + 1) * piece] for j in range(w // piece)]


def _store(ref, val, dtype=None):
    if isinstance(val, (list, tuple)):
        piece = val[0].shape[-1]
        for j, p in enumerate(val):
            ref[:, j * piece:(j + 1) * piece] = p.astype(ref.dtype)
    else:
        ref[...] = val.astype(ref.dtype)


def _row_tile(T):
    return _pick(T, (384, 352, 192, 128))


def _param2d(p):
    return p.reshape(1, -1).astype(F32)


def _rowwise(fn, T, rows, params, outs, name):
    tm = _row_tile(T)
    nr, npar = len(rows), len(params)
    par = [(_param2d(p), piece) for p, piece in params]

    def body(*refs):
        rid = pl.program_id(0) * tm + lax.broadcasted_iota(jnp.int32, (tm, 1), 0)
        rv = [_split(refs[n][...].astype(F32), rows[n].piece) for n in range(nr)]
        pv = [_split(refs[nr + n][...], par[n][1]) for n in range(npar)]
        res = fn(rid, rv, pv)
        for n, val in enumerate(res):
            _store(refs[nr + npar + n], val)

    return pl.pallas_call(
        body, name=name, grid=(T // tm,),
        in_specs=[r.spec(tm) for r in rows] + [pl.BlockSpec(p.shape, lambda i: (0, 0)) for p, _ in par],
        out_specs=[pl.BlockSpec((tm, w), lambda i: (i, 0)) for w, _ in outs],
        out_shape=[jax.ShapeDtypeStruct((T, w), dt) for w, dt in outs],
        compiler_params=_cparams(("parallel",)),
    )(*[r.arr for r in rows], *[p for p, _ in par])


def _rowwise_bwd(fn, T, rows, params, cts, drow, name, add=None):
    tm = _row_tile(T)
    nr, npar, nct = len(rows), len(params), len(cts)
    par = [(_param2d(p), piece) for p, piece in params]
    didx = sorted(drow)
    has_add = add is not None

    def body(*refs):
        i = pl.program_id(0)
        rid = i * tm + lax.broadcasted_iota(jnp.int32, (tm, 1), 0)
        rv = [_split(refs[n][...].astype(F32), rows[n].piece) for n in range(nr)]
        pv = [_split(refs[nr + n][...], par[n][1]) for n in range(npar)]
        cv = [_split(refs[nr + npar + n][...].astype(F32), cts[n].piece) for n in range(nct)]
        base = nr + npar + nct + (1 if has_add else 0)
        d_refs = refs[base:base + len(didx)]
        p_refs = refs[base + len(didx):]

        def g(dvals, pvals):
            full = list(rv)
            for n, v in zip(didx, dvals):
                full[n] = v
            return fn(rid, full, pvals)

        _, vjp = jax.vjp(g, [rv[n] for n in didx], pv)
        d_rows, d_pars = vjp(cv)
        for slot, n in enumerate(didx):
            val = d_rows[slot]
            if has_add and add[0] == n:
                assert not isinstance(val, (list, tuple))
                val = val + refs[nr + npar + nct][...].astype(F32)
            _store(d_refs[slot], val)

        @pl.when(i == 0)
        def _():
            for r in p_refs:
                r[...] = jnp.zeros_like(r)

        for r, val in zip(p_refs, d_pars):
            if isinstance(val, (list, tuple)):
                piece = val[0].shape[-1]
                for j, p in enumerate(val):
                    r[:, j * piece:(j + 1) * piece] += p
            else:
                r[...] += val

    in_specs = ([r.spec(tm) for r in rows] + [pl.BlockSpec(p.shape, lambda i: (0, 0)) for p, _ in par]
                + [c.spec(tm) for c in cts])
    args = [r.arr for r in rows] + [p for p, _ in par] + [c.arr for c in cts]
    if has_add:
        in_specs.append(pl.BlockSpec((tm, rows[add[0]].width), lambda i: (i, 0)))
        args.append(add[1])
    out_specs = ([pl.BlockSpec((tm, rows[n].width), lambda i: (i, 0)) for n in didx]
                 + [pl.BlockSpec(p.shape, lambda i: (0, 0)) for p, _ in par])
    out_shape = ([jax.ShapeDtypeStruct((T, rows[n].width), drow[n]) for n in didx]
                 + [jax.ShapeDtypeStruct(p.shape, F32) for p, _ in par])
    res = pl.pallas_call(
        body, name=name, grid=(T // tm,), in_specs=in_specs, out_specs=out_specs, out_shape=out_shape,
        compiler_params=_cparams(("arbitrary",)),
    )(*args)
    return list(res[:len(didx)]), list(res[len(didx):])


def _f_rms(rid, rv, pv):
    x, g = rv[0], pv[0]
    return [x * lax.rsqrt(jnp.mean(x * x, axis=-1, keepdims=True) + EPS) * g]


def _f_glu(rid, rv, pv):
    a, gt = rv
    return [a * jax.nn.sigmoid(gt) * (rid >= PAD_FRONT).astype(F32)]


def _f_lnsilu(rid, rv, pv):
    x = rv[0]
    g, b = pv
    mu = jnp.mean(x, axis=-1, keepdims=True)
    xc = x - mu
    y = xc * lax.rsqrt(jnp.mean(xc * xc, axis=-1, keepdims=True) + EPS) * g + b
    return [y * jax.nn.sigmoid(y)]


@functools.partial(jax.custom_vjp, nondiff_argnums=(1,))
def _lane_roll(x, shift):
    return pltpu.roll(x, shift, 1)


def _lane_roll_fwd(x, shift):
    return pltpu.roll(x, shift, 1), None


def _lane_roll_bwd(shift, _, g):
    return (pltpu.roll(g, (HEAD_W - shift) % HEAD_W, 1),)


_lane_roll.defvjp(_lane_roll_fwd, _lane_roll_bwd)


def _head_norm_rope(xh, g, c, s1, s2):
    y = xh * lax.rsqrt(jnp.sum(xh * xh, axis=-1, keepdims=True) * (1.0 / QK_DIM) + EPS) * g
    half = ROPE // 2
    return y * c + _lane_roll(y, HEAD_W - half) * s1 + _lane_roll(y, half) * s2


def _f_qrope(rid, rv, pv):
    q, c, s1, s2 = rv
    return [[_head_norm_rope(qh, pv[0], c, s1, s2) for qh in q]]


def _f_krope(rid, rv, pv):
    k, kr, c, s1, s2 = rv
    return [[_head_norm_rope(kh + kr, pv[0], c, s1, s2) for kh in k]]


def _f_hgrn_prep(rid, rv, pv):
    hf, hi = rv
    m = (rid >= PAD_FRONT).astype(F32)
    kk = (1.0 - pv[0]) * jax.nn.sigmoid(-hf) * m
    lf = jnp.log1p(-jnp.minimum(kk, GATE_CLAMP))
    vv = hi * jax.nn.sigmoid(hi) * m
    return [kk, lf, vv]


def _f_hgrn_out(rid, rv, pv):
    o, hg = rv
    ng = pv[0]
    out = []
    for oh, gh, nh in zip(o, hg, ng):
        y = oh * lax.rsqrt(jnp.mean(oh * oh, axis=-1, keepdims=True) + EPS) * nh
        out.append(y * (gh * jax.nn.sigmoid(gh)))
    return [out]


def _f_mix(rid, rv, pv):
    g0, g1, g2, ya, yb, yc = rv
    return [jax.nn.sigmoid(g0) * ya + jax.nn.sigmoid(g1) * yb + jax.nn.sigmoid(g2) * yc]


def _f_relu2(rid, rv, pv):
    return [jnp.square(jax.nn.relu(rv[0]))]


def _loss_head(x2, tgt, T):
    tm = _row_tile(T)

    def body(x_ref, t_ref, dx_ref, l_ref):
        i = pl.program_id(0)
        rid = i * tm + lax.broadcasted_iota(jnp.int32, (tm, 1), 0)
        diff = (x_ref[...] - t_ref[...]) * (rid >= ROW0).astype(F32)
        dx_ref[...] = diff * (1.0 / D_MODEL)

        @pl.when(i == 0)
        def _():
            l_ref[...] = jnp.zeros_like(l_ref)

        l_ref[...] += jnp.sum(diff * diff, axis=0, keepdims=True)

    spec = pl.BlockSpec((tm, D_MODEL), lambda i: (i, 0))
    return pl.pallas_call(
        body, name="loss_head", grid=(T // tm,), in_specs=[spec, spec],
        out_specs=[spec, pl.BlockSpec((1, D_MODEL), lambda i: (0, 0))],
        out_shape=[jax.ShapeDtypeStruct((T, D_MODEL), F32), jax.ShapeDtypeStruct((1, D_MODEL), F32)],
        compiler_params=_cparams(("arbitrary",)),
    )(x2, tgt)


HALO = 32


def _conv_tile(T):
    return _pick(T, (384, 128))


def _conv_fwd(h, w, b, T, name):
    tr = _conv_tile(T)
    ratio = tr // HALO
    wp = jnp.zeros((HALO, CONV_DIM), F32).at[:CONV_K].set(w)

    def body(m_ref, h_ref, w_ref, b_ref, o_ref, win):
        i = pl.program_id(0)
        win[0:HALO, :] = h_ref[...] * (i > 0).astype(F32)
        win[HALO:, :] = m_ref[...]
        acc = jnp.broadcast_to(b_ref[...], (tr, CONV_DIM))
        for k in range(CONV_K):
            acc = acc + w_ref[k:k + 1, :] * win[pl.ds(HALO - (CONV_K - 1) + k, tr), :]
        o_ref[...] = acc

    return pl.pallas_call(
        body, name=name, grid=(T // tr,),
        in_specs=[pl.BlockSpec((tr, CONV_DIM), lambda i: (i, 0)),
                  pl.BlockSpec((HALO, CONV_DIM), lambda i: (jnp.maximum(i * ratio - 1, 0), 0)),
                  pl.BlockSpec((HALO, CONV_DIM), lambda i: (0, 0)),
                  pl.BlockSpec((1, CONV_DIM), lambda i: (0, 0))],
        out_specs=pl.BlockSpec((tr, CONV_DIM), lambda i: (i, 0)),
        out_shape=jax.ShapeDtypeStruct((T, CONV_DIM), F32),
        scratch_shapes=[pltpu.VMEM((tr + HALO, CONV_DIM), F32)],
        compiler_params=_cparams(("parallel",)),
    )(h, h, wp, _param2d(b))


def _conv_bwd(h, w, dy, T, name):
    tr = _conv_tile(T)
    ratio = tr // HALO
    n_t = T // tr
    last_halo = T // HALO - 1
    wp = jnp.zeros((HALO, CONV_DIM), F32).at[:CONV_K].set(w)

    def body(hm_ref, hh_ref, dm_ref, dh_ref, w_ref, dx_ref, dw_ref, db_ref, hwin, dwin):
        i = pl.program_id(0)
        hwin[0:HALO, :] = hh_ref[...] * (i > 0).astype(F32)
        hwin[HALO:, :] = hm_ref[...]
        dwin[0:tr, :] = dm_ref[...]
        dwin[tr:, :] = dh_ref[...] * (i < n_t - 1).astype(F32)

        @pl.when(i == 0)
        def _():
            dw_ref[...] = jnp.zeros_like(dw_ref)
            db_ref[...] = jnp.zeros_like(db_ref)

        dy_m = dm_ref[...]
        db_ref[...] += jnp.sum(dy_m, axis=0, keepdims=True)
        acc = jnp.zeros((tr, CONV_DIM), F32)
        for k in range(CONV_K):
            acc = acc + w_ref[k:k + 1, :] * dwin[pl.ds(CONV_K - 1 - k, tr), :]
            dw_ref[k:k + 1, :] += jnp.sum(dy_m * hwin[pl.ds(HALO - (CONV_K - 1) + k, tr), :], axis=0, keepdims=True)
        dx_ref[...] = acc

    main = pl.BlockSpec((tr, CONV_DIM), lambda i: (i, 0))
    return pl.pallas_call(
        body, name=name, grid=(n_t,),
        in_specs=[main,
                  pl.BlockSpec((HALO, CONV_DIM), lambda i: (jnp.maximum(i * ratio - 1, 0), 0)),
                  main,
                  pl.BlockSpec((HALO, CONV_DIM), lambda i: (jnp.minimum((i + 1) * ratio, last_halo), 0)),
                  pl.BlockSpec((HALO, CONV_DIM), lambda i: (0, 0))],
        out_specs=[main, pl.BlockSpec((HALO, CONV_DIM), lambda i: (0, 0)), pl.BlockSpec((1, CONV_DIM), lambda i: (0, 0))],
        out_shape=[jax.ShapeDtypeStruct((T, CONV_DIM), F32), jax.ShapeDtypeStruct((HALO, CONV_DIM), F32),
                   jax.ShapeDtypeStruct((1, CONV_DIM), F32)],
        scratch_shapes=[pltpu.VMEM((tr + HALO, CONV_DIM), F32), pltpu.VMEM((tr + HALO, CONV_DIM), F32)],
        compiler_params=_cparams(("arbitrary",)),
    )(h, h, dy, dy, wp)


BLK = 128
NEG = -1e30
ATT_SCALE = QK_DIM ** -0.5
_NT = (((1,), (1,)), ((), ()))
_TN = (((0,), (0,)), ((), ()))


def _att_mask(i, j):
    qpos = i * BLK + lax.broadcasted_iota(jnp.int32, (BLK, BLK), 0)
    kpos = j * BLK + lax.broadcasted_iota(jnp.int32, (BLK, BLK), 1)
    return (kpos <= qpos) & (kpos >= PAD_FRONT)


def _attn_fwd(q, k, v, T, name):
    nq = T // BLK

    def body(q_ref, k_ref, v_ref, o_ref, lse_ref):
        i = pl.program_id(1)
        qb = q_ref[...].astype(BF16)

        def step(j, carry):
            m, l, acc = carry
            r0 = pl.multiple_of(j * BLK, BLK)
            kb = k_ref[pl.ds(r0, BLK), :].astype(BF16)
            vb = v_ref[pl.ds(r0, BLK), :].astype(BF16)
            s = lax.dot_general(qb, kb, _NT, preferred_element_type=F32) * ATT_SCALE
            s = jnp.where(_att_mask(i, j), s, NEG)
            m_new = jnp.maximum(m, jnp.max(s, axis=-1, keepdims=True))
            p = jnp.exp(s - m_new)
            alpha = jnp.exp(m - m_new)
            l = alpha * l + jnp.sum(p, axis=-1, keepdims=True)
            acc = alpha * acc + jnp.dot(p.astype(BF16), vb, preferred_element_type=F32)
            return m_new, l, acc

        init = (jnp.full((BLK, 1), NEG, F32), jnp.zeros((BLK, 1), F32), jnp.zeros((BLK, HEAD_W), F32))
        m, l, acc = lax.fori_loop(0, i + 1, step, init)
        o_ref[...] = (acc / l).astype(o_ref.dtype)
        lse_ref[0] = m + jnp.log(l)

    full = lambda off: pl.BlockSpec((T, HEAD_W), lambda h, i: (0, h + off))
    return pl.pallas_call(
        body, name=name, grid=(HEADS, nq),
        in_specs=[pl.BlockSpec((BLK, HEAD_W), lambda h, i: (i, h)), full(0), full(0)],
        out_specs=[pl.BlockSpec((BLK, HEAD_W), lambda h, i: (i, h)),
                   pl.BlockSpec((1, BLK, 1), lambda h, i: (h, i, 0))],
        out_shape=[jax.ShapeDtypeStruct((T, HEADS * HEAD_W), BF16), jax.ShapeDtypeStruct((HEADS, T, 1), F32)],
        compiler_params=_cparams(("parallel", "arbitrary")),
    )(q, k, v)


def _attn_bwd(q, k, v, o, lse, do, T, name):
    nq = T // BLK

    def body(q_ref, k_ref, v_ref, o_ref, lse_ref, do_ref, dq_ref, dk_ref, dv_ref, delta):
        j = pl.program_id(1)

        @pl.when(j == 0)
        def _():
            dq_ref[...] = jnp.zeros_like(dq_ref)

            def dstep(i, c):
                r0 = pl.multiple_of(i * BLK, BLK)
                delta[pl.ds(r0, BLK), :] = jnp.sum(
                    do_ref[pl.ds(r0, BLK), :].astype(F32) * o_ref[pl.ds(r0, BLK), :].astype(F32), axis=-1, keepdims=True)
                return c

            lax.fori_loop(0, nq, dstep, 0)

        kb = k_ref[...].astype(BF16)
        vb = v_ref[...].astype(BF16)

        def step(i, carry):
            dk, dv = carry
            r0 = pl.multiple_of(i * BLK, BLK)
            qb = q_ref[pl.ds(r0, BLK), :].astype(BF16)
            dob = do_ref[pl.ds(r0, BLK), :].astype(BF16)
            s = lax.dot_general(qb, kb, _NT, preferred_element_type=F32) * ATT_SCALE
            p = jnp.where(_att_mask(i, j), jnp.exp(s - lse_ref[0, pl.ds(r0, BLK), :]), 0.0)
            dv = dv + lax.dot_general(p.astype(BF16), dob, _TN, preferred_element_type=F32)
            dp = lax.dot_general(dob, vb, _NT, preferred_element_type=F32)
            ds = (p * (dp - delta[pl.ds(r0, BLK), :]) * ATT_SCALE).astype(BF16)
            dk = dk + lax.dot_general(ds, qb, _TN, preferred_element_type=F32)
            dq_ref[pl.ds(r0, BLK), :] += jnp.dot(ds, kb, preferred_element_type=F32)
            return dk, dv

        zero = jnp.zeros((BLK, HEAD_W), F32)
        dk, dv = lax.fori_loop(j, nq, step, (zero, zero))
        dk_ref[...] = dk
        dv_ref[...] = dv

    full = pl.BlockSpec((T, HEAD_W), lambda h, j: (0, h))
    blk = pl.BlockSpec((BLK, HEAD_W), lambda h, j: (j, h))
    return pl.pallas_call(
        body, name=name, grid=(HEADS, nq),
        in_specs=[full, blk, blk, full, pl.BlockSpec((1, T, 1), lambda h, j: (h, 0, 0)), full],
        out_specs=[full, blk, blk],
        out_shape=[jax.ShapeDtypeStruct((T, HEADS * HEAD_W), F32)] * 3,
        scratch_shapes=[pltpu.VMEM((T, 1), F32)],
        compiler_params=_cparams(("parallel", "arbitrary")),
    )(q, k, v, o, lse, do)


HG_NB = 6
C = HG_CHUNK
_HI = lax.Precision.HIGHEST


def _tri(lower):
    r = lax.broadcasted_iota(jnp.int32, (C, C), 0)
    c = lax.broadcasted_iota(jnp.int32, (C, C), 1)
    return ((c <= r) if lower else (c >= r)).astype(F32)


def _hg_intra_fwd(q, k, v, b):
    o = jnp.zeros((C, HG_DV), F32)
    rows = lax.broadcasted_iota(jnp.int32, (C, 1), 0)
    for s in range(C):
        lo = (s // 8) * 8
        e = jnp.exp(jnp.minimum(b[lo:] - b[s:s + 1], 0.0))
        a = jnp.sum(q[lo:] * k[s:s + 1] * e, axis=-1, keepdims=True)
        a = jnp.where(rows[lo:] >= s, a, 0.0)
        upd = a * v[s:s + 1]
        o = o + (upd if lo == 0 else jnp.concatenate([jnp.zeros((lo, HG_DV), F32), upd], axis=0))
    return o


def _hgrn_fwd(u, kk, lf, vv, T, name):
    nb = _pick(T // C, (HG_NB, 3, 2, 1))
    rows = nb * C
    qblk = C_HQ // HG_DK

    def body(q_ref, k_ref, lf_ref, v_ref, o_ref, st_ref, st):
        @pl.when(pl.program_id(1) == 0)
        def _():
            st[...] = jnp.zeros_like(st)

        lower = _tri(True)
        for n in range(nb):
            sl = slice(n * C, (n + 1) * C)
            q, k, v = q_ref[sl, :], k_ref[sl, :], v_ref[sl, :]
            b = jnp.dot(lower, lf_ref[sl, :], precision=_HI, preferred_element_type=F32)
            s_t = st[...]
            st_ref[0, n] = s_t
            qe = (q * jnp.exp(b)).astype(BF16)
            o = lax.dot_general(qe, s_t.astype(BF16), _NT, preferred_element_type=F32)
            o_ref[sl, :] = o + _hg_intra_fwd(q, k, v, b)
            bl = b[C - 1:C, :]
            kd = (k * jnp.exp(bl - b)).astype(BF16)
            st[...] = s_t * jnp.exp(bl) + lax.dot_general(v.astype(BF16), kd, _TN, preferred_element_type=F32)

    col = lambda off: pl.BlockSpec((rows, HG_DK), lambda h, c: (c, h + off))
    return pl.pallas_call(
        body, name=name, grid=(HG_HEADS, T // rows),
        in_specs=[col(qblk), col(0), col(0), col(0)],
        out_specs=[col(0), pl.BlockSpec((1, nb, HG_DV, HG_DK), lambda h, c: (h, c, 0, 0))],
        out_shape=[jax.ShapeDtypeStruct((T, HG_HEADS * HG_DV), F32),
                   jax.ShapeDtypeStruct((HG_HEADS, T // C, HG_DV, HG_DK), F32)],
        scratch_shapes=[pltpu.VMEM((HG_DV, HG_DK), F32)],
        compiler_params=_cparams(("parallel", "arbitrary")),
    )(u, kk, lf, vv)


def _hgrn_bwd(u, kk, lf, vv, states, do, T, name):
    nb = _pick(T // C, (HG_NB, 3, 2, 1))
    rows = nb * C
    n_steps = T // rows
    qblk = C_HQ // HG_DK

    def body(q_ref, k_ref, lf_ref, v_ref, st_ref, do_ref, dq_ref, dk_ref, dlf_ref, dv_ref, dst, dk_s, dv_s):
        @pl.when(pl.program_id(1) == 0)
        def _():
            dst[...] = jnp.zeros_like(dst)

        lower, upper = _tri(True), _tri(False)
        rid = lax.broadcasted_iota(jnp.int32, (C, 1), 0)
        for n in reversed(range(nb)):
            sl = slice(n * C, (n + 1) * C)
            q, k, v, do = q_ref[sl, :], k_ref[sl, :], v_ref[sl, :], do_ref[sl, :]
            b = jnp.dot(lower, lf_ref[sl, :], precision=_HI, preferred_element_type=F32)
            s_t = st_ref[0, n]
            d_new = dst[...]
            eb = jnp.exp(b)
            bl = b[C - 1:C, :]
            ebl = jnp.exp(bl)
            dec = jnp.exp(bl - b)
            qe = q * eb
            kd = k * dec
            do_b = do.astype(BF16)
            dqe = jnp.dot(do_b, s_t.astype(BF16), preferred_element_type=F32)
            dkd = jnp.dot(v.astype(BF16), d_new.astype(BF16), preferred_element_type=F32)
            dv = lax.dot_general(kd.astype(BF16), d_new.astype(BF16), _NT, preferred_element_type=F32)
            dbl = ebl * jnp.sum(d_new * s_t, axis=0, keepdims=True) + jnp.sum(dkd * kd, axis=0, keepdims=True)
            dst[...] = d_new * ebl + lax.dot_general(do_b, qe.astype(BF16), _TN, preferred_element_type=F32)
            dq = dqe * eb
            dk = dkd * dec
            for s in range(C):
                lo = (s // 8) * 8
                e = jnp.exp(jnp.minimum(b[lo:] - b[s:s + 1], 0.0))
                e = jnp.where(rid[lo:] >= s, e, 0.0)
                a = jnp.sum(q[lo:] * k[s:s + 1] * e, axis=-1, keepdims=True)
                da = jnp.sum(do[lo:] * v[s:s + 1], axis=-1, keepdims=True)
                g = da * e
                upd = g * k[s:s + 1]
                dq = dq + (upd if lo == 0 else jnp.concatenate([jnp.zeros((lo, HG_DK), F32), upd], axis=0))
                dk_s[s:s + 1, :] = jnp.sum(g * q[lo:], axis=0, keepdims=True)
                dv_s[s:s + 1, :] = jnp.sum(a * do[lo:], axis=0, keepdims=True)
            dk = dk + dk_s[...]
            dv = dv + dv_s[...]
            db = q * dq - k * dk
            db = db + jnp.where(rid == C - 1, dbl, 0.0)
            dq_ref[sl, :] = dq
            dk_ref[sl, :] = dk
            dv_ref[sl, :] = dv
            dlf_ref[sl, :] = jnp.dot(upper, db, precision=_HI, preferred_element_type=F32)

    rev = lambda off: pl.BlockSpec((rows, HG_DK), lambda h, c: (n_steps - 1 - c, h + off))
    return pl.pallas_call(
        body, name=name, grid=(HG_HEADS, n_steps),
        in_specs=[rev(qblk), rev(0), rev(0), rev(0),
                  pl.BlockSpec((1, nb, HG_DV, HG_DK), lambda h, c: (h, n_steps - 1 - c, 0, 0)), rev(0)],
        out_specs=[rev(0)] * 4,
        out_shape=[jax.ShapeDtypeStruct((T, HG_HEADS * HG_DK), F32)] * 4,
        scratch_shapes=[pltpu.VMEM((HG_DV, HG_DK), F32), pltpu.VMEM((C, HG_DK), F32), pltpu.VMEM((C, HG_DV), F32)],
        compiler_params=_cparams(("parallel", "arbitrary")),
    )(u, kk, lf, vv, states, do)


def _rope_tables(T):
    half = ROPE // 2
    pos = (np.arange(T, dtype=np.float32) - PAD_FRONT).astype(np.float32)
    inv_freq = (ROPE_BASE ** (-np.arange(half, dtype=np.float32) / half)).astype(np.float32)
    ang = pos[:, None] * inv_freq[None, :]
    cos, sin = jnp.cos(jnp.asarray(ang)), jnp.sin(jnp.asarray(ang))
    z = jnp.zeros((T, HEAD_W), F32)
    c = z.at[:, :NOPE].set(1.0).at[:, NOPE:NOPE + half].set(cos).at[:, NOPE + half:NOPE + ROPE].set(cos)
    s1 = z.at[:, NOPE:NOPE + half].set(-sin)
    s2 = z.at[:, NOPE + half:NOPE + ROPE].set(sin)
    return c, s1, s2


def _layer_fwd(x, w, tabs, T, l):
    c, s1, s2 = tabs
    n = lambda s: f"l{l}_{s}"
    sv = {"x": x}
    h = _rowwise(_f_rms, T, [Row(x)], [(w["norm1_g"], D_MODEL)], [(D_MODEL, BF16)], n("norm1"))[0]
    u = _mm(h, w["w_in"], name=n("in_proj"))
    sv.update(h=h, u=u)
    hglu = _rowwise(_f_glu, T, [Row(u, 512, C_CONV_A), Row(u, 512, C_CONV_G)], [], [(CONV_DIM, F32)], n("glu"))[0]
    cv = _conv_fwd(hglu, w["conv_w"], w["conv_b"], T, n("conv"))
    hc = _rowwise(_f_lnsilu, T, [Row(cv)], [(w["conv_ln_g"], CONV_DIM), (w["conv_ln_b"], CONV_DIM)],
                  [(CONV_DIM, BF16)], n("conv_ln"))[0]
    y_a = _mm(hc, w["w_conv_out"], name=n("conv_out"))
    sv.update(hglu=hglu, cv=cv, hc=hc, y_a=y_a)
    cqn = _rowwise(_f_rms, T, [Row(u, Q_RANK, C_CQ)], [(w["q_a_norm_g"], Q_RANK)], [(Q_RANK, BF16)], n("q_a_norm"))[0]
    ckvn = _rowwise(_f_rms, T, [Row(u, KV_RANK, C_CKV)], [(w["kv_a_norm_g"], KV_RANK)], [(KV_RANK, BF16)], n("kv_a_norm"))[0]
    q_raw = _mm(cqn, w["w_uq"], name=n("uq"))
    k_raw = _mm(ckvn, w["w_uk"], name=n("uk"))
    v = _mm(ckvn, w["w_uv"], out_dtype=BF16, name=n("uv"))
    tab_rows = [Row(c), Row(s1), Row(s2)]
    q = _rowwise(_f_qrope, T, [Row(q_raw, piece=HEAD_W)] + tab_rows, [(w["q_norm_g"], HEAD_W)],
                 [(HEADS * HEAD_W, BF16)], n("q_rope"))[0]
    k = _rowwise(_f_krope, T, [Row(k_raw, piece=HEAD_W), Row(u, HEAD_W, C_KR)] + tab_rows, [(w["k_norm_g"], HEAD_W)],
                 [(HEADS * HEAD_W, BF16)], n("k_rope"))[0]
    o, lse = _attn_fwd(q, k, v, T, n("attn"))
    y_b = _mm(o, w["w_attn_out"], name=n("attn_out"))
    sv.update(cqn=cqn, ckvn=ckvn, q_raw=q_raw, k_raw=k_raw, v=v, q=q, k=k, o=o, lse=lse, y_b=y_b)
    kk, lf, vv = _rowwise(_f_hgrn_prep, T, [Row(u, 512, C_HF), Row(u, 512, C_HI)], [(w["lb"], 512)],
                          [(512, F32)] * 3, n("hgrn_prep"))
    o_h, states = _hgrn_fwd(u, kk, lf, vv, T, n("hgrn"))
    oh = _rowwise(_f_hgrn_out, T, [Row(o_h, piece=HG_DV), Row(u, 512, C_HG, piece=HG_DV)], [(w["hgrn_norm_g"], HG_DV)],
                  [(512, BF16)], n("hgrn_out_norm"))[0]
    y_c = _mm(oh, w["w_hgrn_out"], name=n("hgrn_out"))
    sv.update(kk=kk, lf=lf, vv=vv, o_h=o_h, states=states, oh=oh, y_c=y_c)
    gate_rows = [Row(u, D_MODEL, C_GATE + g * D_MODEL) for g in range(3)]
    mix = _rowwise(_f_mix, T, gate_rows + [Row(y_a), Row(y_b), Row(y_c)], [], [(D_MODEL, BF16)], n("mix"))[0]
    x1 = _mm(mix, w["w_out"], res=x, name=n("out_proj"))
    h2 = _rowwise(_f_rms, T, [Row(x1)], [(w["norm2_g"], D_MODEL)], [(D_MODEL, BF16)], n("norm2"))[0]
    f = _mm(h2, w["w_ff1"], name=n("ff1"))
    r = _rowwise(_f_relu2, T, [Row(f)], [], [(D_FF, BF16)], n("relu2"))[0]
    x2 = _mm(r, w["w_ff2"], res=x1, name=n("ff2"))
    sv.update(mix=mix, x1=x1, h2=h2, f=f, r=r)
    return x2, sv


def _layer_bwd(dx2, w, sv, tabs, T, l):
    c, s1, s2 = tabs
    n = lambda s: f"l{l}_b_{s}"
    u = sv["u"]
    g = {}
    dx2_b = dx2
    g["w_ff2"] = _mm(sv["r"], dx2_b, ta=True, name=n("dw_ff2"))
    dr = _mm(dx2_b, w["w_ff2"], tb=True, name=n("d_r"))
    (df,), _ = _rowwise_bwd(_f_relu2, T, [Row(sv["f"])], [], [Row(dr)], {0: BF16}, n("relu2"))
    g["w_ff1"] = _mm(sv["h2"], df, ta=True, name=n("dw_ff1"))
    dh2 = _mm(df, w["w_ff1"], tb=True, name=n("d_h2"))
    (dx1,), (g["norm2_g"],) = _rowwise_bwd(_f_rms, T, [Row(sv["x1"])], [(w["norm2_g"], D_MODEL)], [Row(dh2)],
                                           {0: F32}, n("norm2"), add=(0, dx2))
    g["w_out"] = _mm(sv["mix"], dx1, ta=True, name=n("dw_out"))
    dmix = _mm(dx1, w["w_out"], tb=True, name=n("d_mix"))
    gate_rows = [Row(u, D_MODEL, C_GATE + i * D_MODEL) for i in range(3)]
    (dg0, dg1, dg2, dy_a, dy_b, dy_c), _ = _rowwise_bwd(
        _f_mix, T, gate_rows + [Row(sv["y_a"]), Row(sv["y_b"]), Row(sv["y_c"])], [], [Row(dmix)],
        {0: BF16, 1: BF16, 2: BF16, 3: BF16, 4: BF16, 5: BF16}, n("mix"))
    g["w_hgrn_out"] = _mm(sv["oh"], dy_c, ta=True, name=n("dw_hgrn_out"))
    doh = _mm(dy_c, w["w_hgrn_out"], tb=True, name=n("d_oh"))
    (do_h, dhg), (g["hgrn_norm_g"],) = _rowwise_bwd(
        _f_hgrn_out, T, [Row(sv["o_h"], piece=HG_DV), Row(u, 512, C_HG, piece=HG_DV)], [(w["hgrn_norm_g"], HG_DV)],
        [Row(doh, piece=HG_DV)], {0: F32, 1: BF16}, n("hgrn_out_norm"))
    dhq, dkk, dlf, dvv = _hgrn_bwd(u, sv["kk"], sv["lf"], sv["vv"], sv["states"], do_h, T, n("hgrn"))
    (dhf, dhi), (g["lb"],) = _rowwise_bwd(
        _f_hgrn_prep, T, [Row(u, 512, C_HF), Row(u, 512, C_HI)], [(w["lb"], 512)],
        [Row(dkk), Row(dlf), Row(dvv)], {0: BF16, 1: BF16}, n("hgrn_prep"))
    g["w_attn_out"] = _mm(sv["o"], dy_b, ta=True, name=n("dw_attn_out"))
    do = _mm(dy_b, w["w_attn_out"], tb=True, out_dtype=BF16, name=n("d_o"))
    dq, dk, dv = _attn_bwd(sv["q"], sv["k"], sv["v"], sv["o"], sv["lse"], do, T, n("attn"))
    tab_rows = [Row(c), Row(s1), Row(s2)]
    (dq_raw,), (g["q_norm_g"],) = _rowwise_bwd(
        _f_qrope, T, [Row(sv["q_raw"], piece=HEAD_W)] + tab_rows, [(w["q_norm_g"], HEAD_W)],
        [Row(dq, piece=HEAD_W)], {0: BF16}, n("q_rope"))
    (dk_raw, dkr), (g["k_norm_g"],) = _rowwise_bwd(
        _f_krope, T, [Row(sv["k_raw"], piece=HEAD_W), Row(u, HEAD_W, C_KR)] + tab_rows, [(w["k_norm_g"], HEAD_W)],
        [Row(dk, piece=HEAD_W)], {0: BF16, 1: BF16}, n("k_rope"))
    g["w_uq"] = _mm(sv["cqn"], dq_raw, ta=True, name=n("dw_uq"))
    g["w_uk"] = _mm(sv["ckvn"], dk_raw, ta=True, name=n("dw_uk"))
    g["w_uv"] = _mm(sv["ckvn"], dv, ta=True, name=n("dw_uv"))
    dcqn = _mm(dq_raw, w["w_uq"], tb=True, name=n("d_cqn"))
    dckvn = _mm(dk_raw, w["w_uk"], tb=True, name=n("d_ckvn_k"))
    dckvn = _mm(dv, w["w_uv"], tb=True, res=dckvn, name=n("d_ckvn_v"))
    (dcq,), (g["q_a_norm_g"],) = _rowwise_bwd(_f_rms, T, [Row(u, Q_RANK, C_CQ)], [(w["q_a_norm_g"], Q_RANK)],
                                              [Row(dcqn)], {0: BF16}, n("q_a_norm"))
    (dckv,), (g["kv_a_norm_g"],) = _rowwise_bwd(_f_rms, T, [Row(u, KV_RANK, C_CKV)], [(w["kv_a_norm_g"], KV_RANK)],
                                                [Row(dckvn)], {0: BF16}, n("kv_a_norm"))
    g["w_conv_out"] = _mm(sv["hc"], dy_a, ta=True, name=n("dw_conv_out"))
    dhc = _mm(dy_a, w["w_conv_out"], tb=True, name=n("d_hc"))
    (dcv,), (g["conv_ln_g"], g["conv_ln_b"]) = _rowwise_bwd(
        _f_lnsilu, T, [Row(sv["cv"])], [(w["conv_ln_g"], CONV_DIM), (w["conv_ln_b"], CONV_DIM)], [Row(dhc)],
        {0: F32}, n("conv_ln"))
    dhglu, dconv_w, g["conv_b"] = _conv_bwd(sv["hglu"], w["conv_w"], dcv, T, n("conv"))
    g["conv_w"] = dconv_w[:CONV_K]
    (dua, dug), _ = _rowwise_bwd(_f_glu, T, [Row(u, 512, C_CONV_A), Row(u, 512, C_CONV_G)], [], [Row(dhglu)],
                                 {0: BF16, 1: BF16}, n("glu"))
    du = jnp.concatenate([dua, dug, dg0, dg1, dg2, dcq, dckv, dkr, dhq.astype(BF16), dhf, dhi, dhg], axis=1)
    g["w_in"] = _mm(sv["h"], du, ta=True, name=n("dw_in"))
    dh = _mm(du, w["w_in"], tb=True, name=n("d_h"))
    (dx,), (g["norm1_g"],) = _rowwise_bwd(_f_rms, T, [Row(sv["x"])], [(w["norm1_g"], D_MODEL)], [Row(dh)],
                                          {0: F32}, n("norm1"), add=(0, dx1))
    return dx, g


def _pad_w_in(w_in):
    z = lambda k: jnp.zeros((w_in.shape[0], k), w_in.dtype)
    return jnp.concatenate([w_in[:, :O_CQ], w_in[:, O_GATE:], w_in[:, O_CQ:O_KR], z(KR_LANE), w_in[:, O_KR:O_HQ],
                            z(HEAD_W - KR_LANE - ROPE), w_in[:, O_HQ:O_GATE]], axis=1)


def _unpad_w_in(g):
    return jnp.concatenate([g[:, :C_GATE], g[:, C_CQ:C_KR], g[:, C_KR + KR_LANE:C_KR + KR_LANE + ROPE],
                            g[:, C_HQ:], g[:, C_GATE:C_CQ]], axis=1)


def _pad_heads(wm, per_head, lo, hi):
    lead = wm.shape[:-1]
    wh = wm.reshape(lead + (HEADS, per_head))[..., lo:hi]
    pad = [(0, 0)] * len(lead) + [(0, 0), (0, HEAD_W - (hi - lo))]
    return jnp.pad(wh, pad).reshape(lead + (HEADS * HEAD_W,))


def _unpad_heads(gm, width):
    lead = gm.shape[:-1]
    return gm.reshape(lead + (HEADS, HEAD_W))[..., :width]


def _layer_weights(full, lbs, l):
    w = {}
    w["norm1_g"] = full["norm1_g"][l]
    w["w_in"] = _pad_w_in(full["w_in"][l])
    w["conv_w"] = full["conv_w"][l]
    w["conv_b"] = full["conv_b"][l]
    w["conv_ln_g"] = full["conv_ln_g"][l]
    w["conv_ln_b"] = full["conv_ln_b"][l]
    w["w_conv_out"] = full["w_conv_out"][l]
    w["q_a_norm_g"] = full["q_a_norm_g"][l]
    w["w_uq"] = _pad_heads(full["w_uq"][l], QK_DIM, 0, QK_DIM)
    w["kv_a_norm_g"] = full["kv_a_norm_g"][l]
    w["w_uk"] = _pad_heads(full["w_ukv"][l], NOPE + V_DIM, 0, NOPE)
    w["w_uv"] = _pad_heads(full["w_ukv"][l], NOPE + V_DIM, NOPE, NOPE + V_DIM)
    w["q_norm_g"] = jnp.pad(full["q_norm_g"][l], (0, HEAD_W - QK_DIM))
    w["k_norm_g"] = jnp.pad(full["k_norm_g"][l], (0, HEAD_W - QK_DIM))
    wa = full["w_attn_out"][l].reshape(HEADS, V_DIM, D_MODEL)
    w["w_attn_out"] = jnp.pad(wa, ((0, 0), (0, HEAD_W - V_DIM), (0, 0))).reshape(HEADS * HEAD_W, D_MODEL)
    w["lb"] = lbs[l]
    w["hgrn_norm_g"] = full["hgrn_norm_g"][l]
    w["w_hgrn_out"] = full["w_hgrn_out"][l]
    w["w_out"] = full["w_out"][l]
    w["norm2_g"] = full["norm2_g"][l]
    w["w_ff1"] = full["w_ff1"][l]
    w["w_ff2"] = full["w_ff2"][l]
    return w


def _layer_grads_to_original(g):
    o = {}
    for name in ("w_conv_out", "w_hgrn_out", "w_out", "w_ff1", "w_ff2", "conv_w"):
        o[name] = g[name]
    for name in ("norm1_g", "conv_b", "conv_ln_g", "conv_ln_b", "q_a_norm_g", "kv_a_norm_g", "hgrn_norm_g", "norm2_g", "lb"):
        o[name] = g[name].reshape(-1)
    o["w_in"] = _unpad_w_in(g["w_in"])
    o["w_uq"] = _unpad_heads(g["w_uq"], QK_DIM).reshape(Q_RANK, HEADS * QK_DIM)
    guk = _unpad_heads(g["w_uk"], NOPE)
    guv = _unpad_heads(g["w_uv"], V_DIM)
    o["w_ukv"] = jnp.concatenate([guk, guv], axis=-1).reshape(KV_RANK, HEADS * (NOPE + V_DIM))
    o["q_norm_g"] = g["q_norm_g"].reshape(-1)[:QK_DIM]
    o["k_norm_g"] = g["k_norm_g"].reshape(-1)[:QK_DIM]
    o["w_attn_out"] = g["w_attn_out"].reshape(HEADS, HEAD_W, D_MODEL)[:, :V_DIM].reshape(HEADS * V_DIM, D_MODEL)
    return o


def _lower_bounds(logits):
    p = jax.nn.softmax(logits.astype(F32), axis=0)
    return jnp.cumsum(p, axis=0) - p[0:1]


def _local_step(x, target, full):
    seq = x.shape[0]
    T = ROW0 + seq
    assert T % 128 == 0
    tabs = _rope_tables(T)
    lbs, lb_vjp = jax.vjp(_lower_bounds, full["hgrn_lb_logits"])
    xp = jnp.concatenate([jnp.zeros((PAD_FRONT, D_MODEL), F32), full["meta"].astype(F32), x], axis=0)
    tp = jnp.concatenate([jnp.zeros((ROW0, D_MODEL), F32), target], axis=0)
    ws, svs = [], []
    for l in range(DEPTH):
        w = _layer_weights(full, lbs, l)
        xp, sv = _layer_fwd(xp, w, tabs, T, l)
        ws.append(w)
        svs.append(sv)
    dx, sq = _loss_head(xp, tp, T)
    loss = 0.5 * jnp.sum(sq) * (1.0 / D_MODEL)
    per_layer = [None] * DEPTH
    for l in reversed(range(DEPTH)):
        dx, g = _layer_bwd(dx, ws[l], svs[l], tabs, T, l)
        per_layer[l] = _layer_grads_to_original(g)
    grads = {k: jnp.stack([per_layer[l][k] for l in range(DEPTH)]) for k in per_layer[0]}
    grads["hgrn_lb_logits"] = lb_vjp(grads.pop("lb"))[0]
    grads["meta"] = dx[PAD_FRONT:ROW0]
    return loss, dx[ROW0:], grads


def _mesh_pos():
    return lax.axis_index("x"), lax.axis_index("y"), lax.axis_index("c")


def _all_gather(pack, name):
    R, W = pack.shape

    def body(x_ref, out_ref, send_sems, recv_sems, local_sem):
        x, y, c = _mesh_pos()
        me, sibling = (x, y, c), (x, y, 1 - c)
        chips = [(1 - x, y), (x, 1 - y), (1 - x, 1 - y)]

        def slot(px, py, pc):
            return out_ref.at[4 * px + 2 * py + pc]

        def copy(k, block, to, src=None):
            return pltpu.make_async_remote_copy(
                src_ref=slot(*block) if src is None else src, dst_ref=slot(*block),
                send_sem=send_sems.at[k], recv_sem=recv_sems.at[k], device_id=to, device_id_type=MESH)

        mine = pltpu.make_async_copy(x_ref, slot(*me), local_sem)
        mine.start()
        first = [copy(0, me, sibling, src=x_ref)]
        first += [copy(1 + j, me, (*chip, c), src=x_ref) for j, chip in enumerate(chips)]
        for cp in first:
            cp.start()
        passed = [copy(4 + j, (*chip, c), sibling) for j, chip in enumerate(chips)]
        for j, chip in enumerate(chips):
            copy(1 + j, (*chip, c), me).wait_recv()
            passed[j].start()
        copy(0, sibling, me).wait_recv()
        for j, chip in enumerate(chips):
            copy(4 + j, (*chip, 1 - c), me).wait_recv()
        for cp in first + passed:
            cp.wait_send()
        mine.wait()

    return pl.pallas_call(
        body, name=name, out_shape=jax.ShapeDtypeStruct((N_DEV, R, W), pack.dtype),
        in_specs=[pl.BlockSpec(memory_space=pl.ANY)], out_specs=pl.BlockSpec(memory_space=pl.ANY),
        scratch_shapes=[pltpu.SemaphoreType.DMA((7,)), pltpu.SemaphoreType.DMA((7,)), pltpu.SemaphoreType.DMA(())],
    )(pack)


def _exchange(send, name):
    def body(s_ref, r_ref, send_sems, recv_sems, local_sem):
        x, y, c = _mesh_pos()
        me = 4 * x + 2 * y + c
        local = pltpu.make_async_copy(s_ref.at[me], r_ref.at[me], local_sem)
        local.start()
        sends, recvs = [], []
        for rel in range(1, N_DEV):
            px = 1 - x if rel & 4 else x
            py = 1 - y if rel & 2 else y
            pc = 1 - c if rel & 1 else c
            p = 4 * px + 2 * py + pc
            sends.append(pltpu.make_async_remote_copy(
                src_ref=s_ref.at[p], dst_ref=r_ref.at[me], send_sem=send_sems.at[rel - 1],
                recv_sem=recv_sems.at[rel - 1], device_id=(px, py, pc), device_id_type=MESH))
            recvs.append(pltpu.make_async_remote_copy(
                src_ref=s_ref.at[me], dst_ref=r_ref.at[p], send_sem=send_sems.at[rel - 1],
                recv_sem=recv_sems.at[rel - 1], device_id=(px, py, pc), device_id_type=MESH))
        for cp in sends:
            cp.start()
        for cp in recvs:
            cp.wait_recv()
        for cp in sends:
            cp.wait_send()
        local.wait()

    return pl.pallas_call(
        body, name=name, out_shape=jax.ShapeDtypeStruct(send.shape, send.dtype),
        in_specs=[pl.BlockSpec(memory_space=pl.ANY)], out_specs=pl.BlockSpec(memory_space=pl.ANY),
        scratch_shapes=[pltpu.SemaphoreType.DMA((7,)), pltpu.SemaphoreType.DMA((7,)), pltpu.SemaphoreType.DMA(())],
    )(send)


def _sum_parts(parts, name):
    P, R, W = parts.shape

    def body(p_ref, o_ref):
        g = p_ref[0].astype(F32)
        for i in range(1, P):
            g = g + p_ref[i].astype(F32)
        o_ref[...] = g

    return pl.pallas_call(body, name=name, out_shape=jax.ShapeDtypeStruct((R, W), F32))(parts)


def _adamw(parts, w, m, v, name):
    P, R, W = parts.shape
    tr = _pick(R, (368, 192, 64, 16, 8))

    def body(p_ref, w_ref, m_ref, v_ref, g_ref, d_ref, nm_ref, nv_ref):
        g = p_ref[0].astype(F32)
        for i in range(1, P):
            g = g + p_ref[i].astype(F32)
        m_new = ADAM_B1 * m_ref[...] + (1.0 - ADAM_B1) * g
        v_new = ADAM_B2 * v_ref[...] + (1.0 - ADAM_B2) * jnp.square(g)
        m_hat = m_new / (1.0 - ADAM_B1 ** ADAM_STEP)
        v_hat = v_new / (1.0 - ADAM_B2 ** ADAM_STEP)
        g_ref[...] = g
        d_ref[...] = -ADAM_LR * (m_hat / (jnp.sqrt(v_hat) + ADAM_EPS) + ADAM_WD * w_ref[...])
        nm_ref[...] = m_new
        nv_ref[...] = v_new

    spec = pl.BlockSpec((tr, W), lambda i: (i, 0))
    return pl.pallas_call(
        body, name=name, grid=(R // tr,),
        in_specs=[pl.BlockSpec((P, tr, W), lambda i: (0, i, 0)), spec, spec, spec], out_specs=[spec] * 4,
        out_shape=[jax.ShapeDtypeStruct((R, W), F32)] * 4,
        compiler_params=_cparams(("parallel",)),
    )(parts, w, m, v)


PACK_W = 1024
BIG = (("w_in", (DEPTH, D_MODEL, N_IN // N_DEV), 2), ("w_conv_out", (DEPTH, CONV_DIM, D_MODEL // N_DEV), 2),
       ("w_uq", (DEPTH, Q_RANK, HEADS * QK_DIM // N_DEV), 2), ("w_ukv", (DEPTH, KV_RANK, HEADS * (NOPE + V_DIM) // N_DEV), 2),
       ("w_attn_out", (DEPTH, HEADS * V_DIM, D_MODEL // N_DEV), 2), ("w_hgrn_out", (DEPTH, 512, D_MODEL // N_DEV), 2),
       ("w_out", (DEPTH, D_MODEL // N_DEV, D_MODEL), 1), ("w_ff1", (DEPTH, D_MODEL, D_FF // N_DEV), 2),
       ("w_ff2", (DEPTH, D_FF // N_DEV, D_MODEL), 1))
SMALL_SHARDED = (("meta", (N_META, D_MODEL // N_DEV), 1), ("conv_w", (DEPTH, CONV_K, CONV_DIM // N_DEV), 2))
REPLICATED = (("norm1_g", (DEPTH, D_MODEL)), ("conv_b", (DEPTH, CONV_DIM)), ("conv_ln_g", (DEPTH, CONV_DIM)),
              ("conv_ln_b", (DEPTH, CONV_DIM)), ("q_a_norm_g", (DEPTH, Q_RANK)), ("kv_a_norm_g", (DEPTH, KV_RANK)),
              ("q_norm_g", (DEPTH, QK_DIM)), ("k_norm_g", (DEPTH, QK_DIM)), ("hgrn_lb_logits", (DEPTH, 512)),
              ("hgrn_norm_g", (DEPTH, 512)), ("norm2_g", (DEPTH, D_MODEL)))
WEIGHT_ORDER = ("meta", "norm1_g", "w_in", "conv_w", "conv_b", "conv_ln_g", "conv_ln_b", "w_conv_out", "q_a_norm_g", "w_uq",
                "kv_a_norm_g", "w_ukv", "q_norm_g", "k_norm_g", "w_attn_out", "hgrn_lb_logits", "hgrn_norm_g", "w_hgrn_out",
                "w_out", "norm2_g", "w_ff1", "w_ff2")


def _rows_for(n_elems, mult):
    rows = -(-n_elems // PACK_W)
    return -(-rows // mult) * mult


def _pack(arrays, dtype, mult, lead=()):
    nl = len(lead)
    flat = jnp.concatenate([a.reshape(lead + (-1,)).astype(dtype) for a in arrays], axis=nl)
    rows = _rows_for(flat.shape[nl], mult)
    flat = jnp.pad(flat, [(0, 0)] * nl + [(0, rows * PACK_W - flat.shape[nl])])
    return flat.reshape(lead + (rows, PACK_W))


def _unpack(pack, shapes, lead=()):
    nl = len(lead)
    flat = pack.reshape(lead + (-1,))
    out, off = [], 0
    for shp in shapes:
        n = int(np.prod(shp))
        out.append(lax.slice_in_dim(flat, off, off + n, axis=nl).reshape(lead + tuple(shp)))
        off += n
    return out


def _join_shards(g, axis):
    g = jnp.moveaxis(g, 0, axis)
    shp = g.shape
    return g.reshape(shp[:axis] + (shp[axis] * shp[axis + 1],) + shp[axis + 2:])


def _cut_shards(a, axis):
    shp = a.shape
    a = a.reshape(shp[:axis] + (N_DEV, shp[axis] // N_DEV) + shp[axis + 1:])
    return jnp.moveaxis(a, axis, 0)


def kernel(x, meta, norm1_g, w_in, conv_w, conv_b, conv_ln_g, conv_ln_b, w_conv_out, q_a_norm_g, w_uq, kv_a_norm_g, w_ukv, q_norm_g, k_norm_g, w_attn_out, hgrn_lb_logits, hgrn_norm_g, w_hgrn_out, w_out, norm2_g, w_ff1, w_ff2, loss_target, m_meta, m_norm1_g, m_w_in, m_conv_w, m_conv_b, m_conv_ln_g, m_conv_ln_b, m_w_conv_out, m_q_a_norm_g, m_w_uq, m_kv_a_norm_g, m_w_ukv, m_q_norm_g, m_k_norm_g, m_w_attn_out, m_hgrn_lb_logits, m_hgrn_norm_g, m_w_hgrn_out, m_w_out, m_norm2_g, m_w_ff1, m_w_ff2, v_meta, v_norm1_g, v_w_in, v_conv_w, v_conv_b, v_conv_ln_g, v_conv_ln_b, v_w_conv_out, v_q_a_norm_g, v_w_uq, v_kv_a_norm_g, v_w_ukv, v_q_norm_g, v_k_norm_g, v_w_attn_out, v_hgrn_lb_logits, v_hgrn_norm_g, v_w_hgrn_out, v_w_out, v_norm2_g, v_w_ff1, v_w_ff2):
    args = dict(locals())
    wts = {n: args[n] for n in WEIGHT_ORDER}
    mom = {n: args["m_" + n] for n in WEIGHT_ORDER}
    var = {n: args["v_" + n] for n in WEIGHT_ORDER}
    xi, yi, ci = _mesh_pos()
    me = 4 * xi + 2 * yi + ci

    big = _all_gather(_pack([wts[n] for n, _, _ in BIG], BF16, 16), "gather_weights")
    small = _all_gather(_pack([wts[n] for n, _, _ in SMALL_SHARDED], F32, 8), "gather_small")
    full = {n: wts[n] for n, _ in REPLICATED}
    for (n, _, axis), g in zip(BIG, _unpack(big, [s for _, s, _ in BIG], (N_DEV,))):
        full[n] = _join_shards(g, axis)
    for (n, _, axis), g in zip(SMALL_SHARDED, _unpack(small, [s for _, s, _ in SMALL_SHARDED], (N_DEV,))):
        full[n] = _join_shards(g, axis)

    loss, grad_x, grads = _local_step(x[0], loss_target[0], full)
    loss = lax.psum(loss, ("x", "y", "c"))

    send = _pack([_cut_shards(grads[n], axis) for n, _, axis in BIG], BF16, 16, (N_DEV,))
    recv = _exchange(send, "scatter_grads")
    pk = lambda d: _pack([d[n] for n, _, _ in BIG], F32, 16)
    big_out = _adamw(recv, pk(wts), pk(mom), pk(var), "adamw_matrices")
    out = {}
    for kind, pack in zip(("grad_", "delta_", "new_m_", "new_v_"), big_out):
        for (n, _, _), a in zip(BIG, _unpack(pack, [s for _, s, _ in BIG])):
            out[kind + n] = a

    small_names = [n for n, _ in REPLICATED] + [n for n, _, _ in SMALL_SHARDED]
    part = _pack([grads[n] for n in small_names], F32, 8)
    total = _sum_parts(_all_gather(part, "gather_small_grads"), "sum_small_grads")
    tot = dict(zip(small_names, _unpack(total, [grads[n].shape for n in small_names])))
    mine = {n: tot[n] for n, _ in REPLICATED}
    for n, shp, axis in SMALL_SHARDED:
        mine[n] = lax.dynamic_slice_in_dim(tot[n], me * shp[axis], shp[axis], axis=axis)
    pk = lambda d: _pack([d[n] for n in small_names], F32, 8)
    small_out = _adamw(pk(mine)[None], pk(wts), pk(mom), pk(var), "adamw_vectors")
    for kind, pack in zip(("grad_", "delta_", "new_m_", "new_v_"), small_out):
        for n, a in zip(small_names, _unpack(pack, [wts[n].shape for n in small_names])):
            out[kind + n] = a

    res = [loss, grad_x[None]]
    for kind in ("grad_", "delta_", "new_m_", "new_v_"):
        res += [out[kind + n] for n in WEIGHT_ORDER]
    return tuple(res)
```

```python
import functools

import numpy as np
import jax
import jax.numpy as jnp
from jax import lax
from jax.experimental import pallas as pl
from jax.experimental.pallas import tpu as pltpu

F32 = jnp.float32
BF16 = jnp.bfloat16

D_MODEL = 1024
DEPTH = 2
N_META = 16
PAD_FRONT = 112
ROW0 = PAD_FRONT + N_META
EPS = 1e-6
GATE_CLAMP = 1.0 - 1e-6
CONV_DIM = 512
CONV_K = 31
HEADS = 8
Q_RANK = 256
KV_RANK = 128
NOPE = 64
ROPE = 32
V_DIM = 64
QK_DIM = NOPE + ROPE
HEAD_W = 128
ROPE_BASE = 10000.0
HG_HEADS = 4
HG_DK = 128
HG_DV = 128
HG_CHUNK = 64
D_FF = 4096
N_IN = 6560
C_CONV_A, C_CONV_G, C_GATE, C_CQ, C_CKV, C_KR, C_HQ, C_HF, C_HI, C_HG = (
    0, 512, 1024, 4096, 4352, 4480, 4608, 5120, 5632, 6144)
N_IN_P = 6656
O_CQ, O_KR, O_HQ, O_GATE = 1024, 1408, 1440, 3488
KR_LANE = NOPE

ADAM_LR = 0.001
ADAM_B1 = 0.9
ADAM_B2 = 0.999
ADAM_EPS = 1e-08
ADAM_WD = 0.01
ADAM_STEP = 10

N_DEV = 8
VMEM_LIMIT = 56 * 1024 * 1024
MESH = pl.DeviceIdType.MESH


def _pick(n, cands):
    for c in cands:
        if n % c == 0:
            return c
    raise ValueError(f"no tile for {n}")


def _cparams(sem, **kw):
    return pltpu.CompilerParams(dimension_semantics=sem, vmem_limit_bytes=VMEM_LIMIT, **kw)


def _mm(a, b, *, ta=False, tb=False, out_dtype=F32, res=None, name):
    M, K = (a.shape[1], a.shape[0]) if ta else a.shape
    N = b.shape[0] if tb else b.shape[1]
    assert (b.shape[1] if tb else b.shape[0]) == K, (a.shape, b.shape, ta, tb)
    tm = _pick(M, (1056, 1024, 512, 384, 256, 128, 96))
    tn = _pick(N, (512, 384, 256, 128))
    tk = _pick(K, (1056, 1024, 512, 384, 256, 128, 96))
    nk = K // tk
    dims = (((0 if ta else 1,), (1 if tb else 0,)), ((), ()))

    def body(*refs):
        a_ref, b_ref = refs[0], refs[1]
        r_ref = refs[2] if res is not None else None
        o_ref = refs[3] if res is not None else refs[2]
        acc = refs[-1] if nk > 1 else None
        k = pl.program_id(2)
        p = lax.dot_general(a_ref[...].astype(BF16), b_ref[...].astype(BF16), dims,
                            preferred_element_type=F32)

        def finish(total):
            if r_ref is not None:
                total = total + r_ref[...].astype(F32)
            o_ref[...] = total.astype(o_ref.dtype)

        if nk == 1:
            finish(p)
        else:
            @pl.when(k == 0)
            def _():
                acc[...] = p

            @pl.when(k > 0)
            def _():
                acc[...] += p

            @pl.when(k == nk - 1)
            def _():
                finish(acc[...])

    a_spec = pl.BlockSpec((tk, tm), lambda i, j, k: (k, i)) if ta else pl.BlockSpec((tm, tk), lambda i, j, k: (i, k))
    b_spec = pl.BlockSpec((tn, tk), lambda i, j, k: (j, k)) if tb else pl.BlockSpec((tk, tn), lambda i, j, k: (k, j))
    o_spec = pl.BlockSpec((tm, tn), lambda i, j, k: (i, j))
    in_specs = [a_spec, b_spec] + ([o_spec] if res is not None else [])
    args = (a, b) + ((res,) if res is not None else ())
    return pl.pallas_call(
        body, name=name, grid=(M // tm, N // tn, nk), in_specs=in_specs, out_specs=o_spec,
        out_shape=jax.ShapeDtypeStruct((M, N), out_dtype),
        scratch_shapes=[pltpu.VMEM((tm, tn), F32)] if nk > 1 else [],
        compiler_params=_cparams(("parallel", "parallel", "arbitrary")),
    )(*args)


class Row:
    def __init__(self, arr, width=None, col=0, piece=None):
        self.arr = arr
        self.width = arr.shape[1] if width is None else width
        assert col % self.width == 0
        self.blk = col // self.width
        self.piece = self.width if piece is None else piece

    def spec(self, tm):
        blk = self.blk
        return pl.BlockSpec((tm, self.width), lambda i: (i, blk))


def _split(v, piece):
    w = v.shape[-1]
    if piece == w:
        return v
    return [v[:, j * piece:(j + 1) * piece] for j in range(w // piece)]


def _store(ref, val, dtype=None):
    if isinstance(val, (list, tuple)):
        piece = val[0].shape[-1]
        for j, p in enumerate(val):
            ref[:, j * piece:(j + 1) * piece] = p.astype(ref.dtype)
    else:
        ref[...] = val.astype(ref.dtype)


def _row_tile(T):
    return _pick(T, (384, 352, 192, 128))


def _param2d(p):
    return p.reshape(1, -1).astype(F32)


def _rowwise(fn, T, rows, params, outs, name):
    tm = _row_tile(T)
    nr, npar = len(rows), len(params)
    par = [(_param2d(p), piece) for p, piece in params]

    def body(*refs):
        rid = pl.program_id(0) * tm + lax.broadcasted_iota(jnp.int32, (tm, 1), 0)
        rv = [_split(refs[n][...].astype(F32), rows[n].piece) for n in range(nr)]
        pv = [_split(refs[nr + n][...], par[n][1]) for n in range(npar)]
        res = fn(rid, rv, pv)
        for n, val in enumerate(res):
            _store(refs[nr + npar + n], val)

    return pl.pallas_call(
        body, name=name, grid=(T // tm,),
        in_specs=[r.spec(tm) for r in rows] + [pl.BlockSpec(p.shape, lambda i: (0, 0)) for p, _ in par],
        out_specs=[pl.BlockSpec((tm, w), lambda i: (i, 0)) for w, _ in outs],
        out_shape=[jax.ShapeDtypeStruct((T, w), dt) for w, dt in outs],
        compiler_params=_cparams(("parallel",)),
    )(*[r.arr for r in rows], *[p for p, _ in par])


def _rowwise_bwd(fn, T, rows, params, cts, drow, name, add=None):
    tm = _row_tile(T)
    nr, npar, nct = len(rows), len(params), len(cts)
    par = [(_param2d(p), piece) for p, piece in params]
    didx = sorted(drow)
    has_add = add is not None

    def body(*refs):
        i = pl.program_id(0)
        rid = i * tm + lax.broadcasted_iota(jnp.int32, (tm, 1), 0)
        rv = [_split(refs[n][...].astype(F32), rows[n].piece) for n in range(nr)]
        pv = [_split(refs[nr + n][...], par[n][1]) for n in range(npar)]
        cv = [_split(refs[nr + npar + n][...].astype(F32), cts[n].piece) for n in range(nct)]
        base = nr + npar + nct + (1 if has_add else 0)
        d_refs = refs[base:base + len(didx)]
        p_refs = refs[base + len(didx):]

        def g(dvals, pvals):
            full = list(rv)
            for n, v in zip(didx, dvals):
                full[n] = v
            return fn(rid, full, pvals)

        _, vjp = jax.vjp(g, [rv[n] for n in didx], pv)
        d_rows, d_pars = vjp(cv)
        for slot, n in enumerate(didx):
            val = d_rows[slot]
            if has_add and add[0] == n:
                assert not isinstance(val, (list, tuple))
                val = val + refs[nr + npar + nct][...].astype(F32)
            _store(d_refs[slot], val)

        @pl.when(i == 0)
        def _():
            for r in p_refs:
                r[...] = jnp.zeros_like(r)

        for r, val in zip(p_refs, d_pars):
            if isinstance(val, (list, tuple)):
                piece = val[0].shape[-1]
                for j, p in enumerate(val):
                    r[:, j * piece:(j + 1) * piece] += p
            else:
                r[...] += val

    in_specs = ([r.spec(tm) for r in rows] + [pl.BlockSpec(p.shape, lambda i: (0, 0)) for p, _ in par]
                + [c.spec(tm) for c in cts])
    args = [r.arr for r in rows] + [p for p, _ in par] + [c.arr for c in cts]
    if has_add:
        in_specs.append(pl.BlockSpec((tm, rows[add[0]].width), lambda i: (i, 0)))
        args.append(add[1])
    out_specs = ([pl.BlockSpec((tm, rows[n].width), lambda i: (i, 0)) for n in didx]
                 + [pl.BlockSpec(p.shape, lambda i: (0, 0)) for p, _ in par])
    out_shape = ([jax.ShapeDtypeStruct((T, rows[n].width), drow[n]) for n in didx]
                 + [jax.ShapeDtypeStruct(p.shape, F32) for p, _ in par])
    res = pl.pallas_call(
        body, name=name, grid=(T // tm,), in_specs=in_specs, out_specs=out_specs, out_shape=out_shape,
        compiler_params=_cparams(("arbitrary",)),
    )(*args)
    return list(res[:len(didx)]), list(res[len(didx):])


def _f_rms(rid, rv, pv):
    x, g = rv[0], pv[0]
    return [x * lax.rsqrt(jnp.mean(x * x, axis=-1, keepdims=True) + EPS) * g]


def _f_glu(rid, rv, pv):
    a, gt = rv
    return [a * jax.nn.sigmoid(gt) * (rid >= PAD_FRONT).astype(F32)]


def _f_lnsilu(rid, rv, pv):
    x = rv[0]
    g, b = pv
    mu = jnp.mean(x, axis=-1, keepdims=True)
    xc = x - mu
    y = xc * lax.rsqrt(jnp.mean(xc * xc, axis=-1, keepdims=True) + EPS) * g + b
    return [y * jax.nn.sigmoid(y)]


@functools.partial(jax.custom_vjp, nondiff_argnums=(1,))
def _lane_roll(x, shift):
    return pltpu.roll(x, shift, 1)


def _lane_roll_fwd(x, shift):
    return pltpu.roll(x, shift, 1), None


def _lane_roll_bwd(shift, _, g):
    return (pltpu.roll(g, (HEAD_W - shift) % HEAD_W, 1),)


_lane_roll.defvjp(_lane_roll_fwd, _lane_roll_bwd)


def _head_norm_rope(xh, g, c, s1, s2):
    y = xh * lax.rsqrt(jnp.sum(xh * xh, axis=-1, keepdims=True) * (1.0 / QK_DIM) + EPS) * g
    half = ROPE // 2
    return y * c + _lane_roll(y, HEAD_W - half) * s1 + _lane_roll(y, half) * s2


def _f_qrope(rid, rv, pv):
    q, c, s1, s2 = rv
    return [[_head_norm_rope(qh, pv[0], c, s1, s2) for qh in q]]


def _f_krope(rid, rv, pv):
    k, kr, c, s1, s2 = rv
    return [[_head_norm_rope(kh + kr, pv[0], c, s1, s2) for kh in k]]


def _f_hgrn_prep(rid, rv, pv):
    hf, hi = rv
    m = (rid >= PAD_FRONT).astype(F32)
    kk = (1.0 - pv[0]) * jax.nn.sigmoid(-hf) * m
    lf = jnp.log1p(-jnp.minimum(kk, GATE_CLAMP))
    vv = hi * jax.nn.sigmoid(hi) * m
    return [kk, lf, vv]


def _f_hgrn_out(rid, rv, pv):
    o, hg = rv
    ng = pv[0]
    out = []
    for oh, gh, nh in zip(o, hg, ng):
        y = oh * lax.rsqrt(jnp.mean(oh * oh, axis=-1, keepdims=True) + EPS) * nh
        out.append(y * (gh * jax.nn.sigmoid(gh)))
    return [out]


def _f_mix(rid, rv, pv):
    g0, g1, g2, ya, yb, yc = rv
    return [jax.nn.sigmoid(g0) * ya + jax.nn.sigmoid(g1) * yb + jax.nn.sigmoid(g2) * yc]


def _f_relu2(rid, rv, pv):
    return [jnp.square(jax.nn.relu(rv[0]))]


def _loss_head(x2, tgt, T):
    tm = _row_tile(T)

    def body(x_ref, t_ref, dx_ref, l_ref):
        i = pl.program_id(0)
        rid = i * tm + lax.broadcasted_iota(jnp.int32, (tm, 1), 0)
        diff = (x_ref[...] - t_ref[...]) * (rid >= ROW0).astype(F32)
        dx_ref[...] = diff * (1.0 / D_MODEL)

        @pl.when(i == 0)
        def _():
            l_ref[...] = jnp.zeros_like(l_ref)

        l_ref[...] += jnp.sum(diff * diff, axis=0, keepdims=True)

    spec = pl.BlockSpec((tm, D_MODEL), lambda i: (i, 0))
    return pl.pallas_call(
        body, name="loss_head", grid=(T // tm,), in_specs=[spec, spec],
        out_specs=[spec, pl.BlockSpec((1, D_MODEL), lambda i: (0, 0))],
        out_shape=[jax.ShapeDtypeStruct((T, D_MODEL), F32), jax.ShapeDtypeStruct((1, D_MODEL), F32)],
        compiler_params=_cparams(("arbitrary",)),
    )(x2, tgt)


HALO = 32


def _conv_tile(T):
    return _pick(T, (384, 128))


def _conv_fwd(h, w, b, T, name):
    tr = _conv_tile(T)
    ratio = tr // HALO
    wp = jnp.zeros((HALO, CONV_DIM), F32).at[:CONV_K].set(w)

    def body(m_ref, h_ref, w_ref, b_ref, o_ref, win):
        i = pl.program_id(0)
        win[0:HALO, :] = h_ref[...] * (i > 0).astype(F32)
        win[HALO:, :] = m_ref[...]
        acc = jnp.broadcast_to(b_ref[...], (tr, CONV_DIM))
        for k in range(CONV_K):
            acc = acc + w_ref[k:k + 1, :] * win[pl.ds(HALO - (CONV_K - 1) + k, tr), :]
        o_ref[...] = acc

    return pl.pallas_call(
        body, name=name, grid=(T // tr,),
        in_specs=[pl.BlockSpec((tr, CONV_DIM), lambda i: (i, 0)),
                  pl.BlockSpec((HALO, CONV_DIM), lambda i: (jnp.maximum(i * ratio - 1, 0), 0)),
                  pl.BlockSpec((HALO, CONV_DIM), lambda i: (0, 0)),
                  pl.BlockSpec((1, CONV_DIM), lambda i: (0, 0))],
        out_specs=pl.BlockSpec((tr, CONV_DIM), lambda i: (i, 0)),
        out_shape=jax.ShapeDtypeStruct((T, CONV_DIM), F32),
        scratch_shapes=[pltpu.VMEM((tr + HALO, CONV_DIM), F32)],
        compiler_params=_cparams(("parallel",)),
    )(h, h, wp, _param2d(b))


def _conv_bwd(h, w, dy, T, name):
    tr = _conv_tile(T)
    ratio = tr // HALO
    n_t = T // tr
    last_halo = T // HALO - 1
    wp = jnp.zeros((HALO, CONV_DIM), F32).at[:CONV_K].set(w)

    def body(hm_ref, hh_ref, dm_ref, dh_ref, w_ref, dx_ref, dw_ref, db_ref, hwin, dwin):
        i = pl.program_id(0)
        hwin[0:HALO, :] = hh_ref[...] * (i > 0).astype(F32)
        hwin[HALO:, :] = hm_ref[...]
        dwin[0:tr, :] = dm_ref[...]
        dwin[tr:, :] = dh_ref[...] * (i < n_t - 1).astype(F32)

        @pl.when(i == 0)
        def _():
            dw_ref[...] = jnp.zeros_like(dw_ref)
            db_ref[...] = jnp.zeros_like(db_ref)

        dy_m = dm_ref[...]
        db_ref[...] += jnp.sum(dy_m, axis=0, keepdims=True)
        acc = jnp.zeros((tr, CONV_DIM), F32)
        for k in range(CONV_K):
            acc = acc + w_ref[k:k + 1, :] * dwin[pl.ds(CONV_K - 1 - k, tr), :]
            dw_ref[k:k + 1, :] += jnp.sum(dy_m * hwin[pl.ds(HALO - (CONV_K - 1) + k, tr), :], axis=0, keepdims=True)
        dx_ref[...] = acc

    main = pl.BlockSpec((tr, CONV_DIM), lambda i: (i, 0))
    return pl.pallas_call(
        body, name=name, grid=(n_t,),
        in_specs=[main,
                  pl.BlockSpec((HALO, CONV_DIM), lambda i: (jnp.maximum(i * ratio - 1, 0), 0)),
                  main,
                  pl.BlockSpec((HALO, CONV_DIM), lambda i: (jnp.minimum((i + 1) * ratio, last_halo), 0)),
                  pl.BlockSpec((HALO, CONV_DIM), lambda i: (0, 0))],
        out_specs=[main, pl.BlockSpec((HALO, CONV_DIM), lambda i: (0, 0)), pl.BlockSpec((1, CONV_DIM), lambda i: (0, 0))],
        out_shape=[jax.ShapeDtypeStruct((T, CONV_DIM), F32), jax.ShapeDtypeStruct((HALO, CONV_DIM), F32),
                   jax.ShapeDtypeStruct((1, CONV_DIM), F32)],
        scratch_shapes=[pltpu.VMEM((tr + HALO, CONV_DIM), F32), pltpu.VMEM((tr + HALO, CONV_DIM), F32)],
        compiler_params=_cparams(("arbitrary",)),
    )(h, h, dy, dy, wp)


BLK = 128
NEG = -1e30
ATT_SCALE = QK_DIM ** -0.5
_NT = (((1,), (1,)), ((), ()))
_TN = (((0,), (0,)), ((), ()))


def _att_blk(T):
    return _pick(T, (384, 128))


def _att_mask(i, j, blk):
    qpos = i * blk + lax.broadcasted_iota(jnp.int32, (blk, blk), 0)
    kpos = j * blk + lax.broadcasted_iota(jnp.int32, (blk, blk), 1)
    return (kpos <= qpos) & (kpos >= PAD_FRONT)


def _attn_fwd(q, k, v, T, name):
    blk = _att_blk(T)
    nq = T // blk

    def body(q_ref, k_ref, v_ref, o_ref, lse_ref):
        i = pl.program_id(1)
        qb = q_ref[...].astype(BF16)

        def step(j, carry, masked):
            m, l, acc = carry
            r0 = pl.multiple_of(j * blk, blk)
            kb = k_ref[pl.ds(r0, blk), :].astype(BF16)
            vb = v_ref[pl.ds(r0, blk), :].astype(BF16)
            s = lax.dot_general(qb, kb, _NT, preferred_element_type=F32) * ATT_SCALE
            if masked:
                s = jnp.where(_att_mask(i, j, blk), s, NEG)
            m_new = jnp.maximum(m, jnp.max(s, axis=-1, keepdims=True))
            p = jnp.exp(s - m_new)
            alpha = jnp.exp(m - m_new)
            l = alpha * l + jnp.sum(p, axis=-1, keepdims=True)
            acc = alpha * acc + jnp.dot(p.astype(BF16), vb, preferred_element_type=F32)
            return m_new, l, acc

        init = (jnp.full((blk, 1), NEG, F32), jnp.zeros((blk, 1), F32), jnp.zeros((blk, HEAD_W), F32))
        carry = step(i, init, True)
        carry = lax.fori_loop(0, jnp.minimum(i, 1), lambda j, c: step(j, c, True), carry)
        m, l, acc = lax.fori_loop(1, i, lambda j, c: step(j, c, False), carry)
        o_ref[...] = (acc / l).astype(o_ref.dtype)
        lse_ref[0] = m + jnp.log(l)

    full = pl.BlockSpec((T, HEAD_W), lambda h, i: (0, h))
    return pl.pallas_call(
        body, name=name, grid=(HEADS, nq),
        in_specs=[pl.BlockSpec((blk, HEAD_W), lambda h, i: (i, h)), full, full],
        out_specs=[pl.BlockSpec((blk, HEAD_W), lambda h, i: (i, h)),
                   pl.BlockSpec((1, blk, 1), lambda h, i: (h, i, 0))],
        out_shape=[jax.ShapeDtypeStruct((T, HEADS * HEAD_W), BF16), jax.ShapeDtypeStruct((HEADS, T, 1), F32)],
        compiler_params=_cparams(("parallel", "arbitrary")),
    )(q, k, v)


def _attn_bwd(q, k, v, o, lse, do, T, name):
    blk = _att_blk(T)
    nq = T // blk

    def body(q_ref, k_ref, v_ref, o_ref, lse_ref, do_ref, dq_ref, dk_ref, dv_ref, delta):
        j = pl.program_id(1)

        @pl.when(j == 0)
        def _():
            dq_ref[...] = jnp.zeros_like(dq_ref)

            def dstep(i, c):
                r0 = pl.multiple_of(i * blk, blk)
                delta[pl.ds(r0, blk), :] = jnp.sum(
                    do_ref[pl.ds(r0, blk), :].astype(F32) * o_ref[pl.ds(r0, blk), :].astype(F32), axis=-1, keepdims=True)
                return c

            lax.fori_loop(0, nq, dstep, 0)

        kb = k_ref[...].astype(BF16)
        vb = v_ref[...].astype(BF16)
        dk_ref[...] = jnp.zeros_like(dk_ref)
        dv_ref[...] = jnp.zeros_like(dv_ref)

        def step(i, masked):
            r0 = pl.multiple_of(i * blk, blk)
            qb = q_ref[pl.ds(r0, blk), :].astype(BF16)
            dob = do_ref[pl.ds(r0, blk), :].astype(BF16)
            s = lax.dot_general(qb, kb, _NT, preferred_element_type=F32) * ATT_SCALE
            p = jnp.exp(s - lse_ref[0, pl.ds(r0, blk), :])
            if masked:
                p = jnp.where(_att_mask(i, j, blk), p, 0.0)
            dv_ref[...] += lax.dot_general(p.astype(BF16), dob, _TN, preferred_element_type=F32)
            dp = lax.dot_general(dob, vb, _NT, preferred_element_type=F32)
            ds = (p * (dp - delta[pl.ds(r0, blk), :]) * ATT_SCALE).astype(BF16)
            dk_ref[...] += lax.dot_general(ds, qb, _TN, preferred_element_type=F32)
            dq_ref[pl.ds(r0, blk), :] += jnp.dot(ds, kb, preferred_element_type=F32)

        def loop(lo, masked):
            def it(i, c):
                step(i, masked)
                return c
            lax.fori_loop(lo, nq, it, 0)

        @pl.when(j == 0)
        def _():
            loop(0, True)

        @pl.when(j > 0)
        def _():
            step(j, True)
            loop(j + 1, False)

    full = pl.BlockSpec((T, HEAD_W), lambda h, j: (0, h))
    kblk = pl.BlockSpec((blk, HEAD_W), lambda h, j: (j, h))
    return pl.pallas_call(
        body, name=name, grid=(HEADS, nq),
        in_specs=[full, kblk, kblk, full, pl.BlockSpec((1, T, 1), lambda h, j: (h, 0, 0)), full],
        out_specs=[full, kblk, kblk],
        out_shape=[jax.ShapeDtypeStruct((T, HEADS * HEAD_W), F32)] * 3,
        scratch_shapes=[pltpu.VMEM((T, 1), F32)],
        compiler_params=_cparams(("parallel", "arbitrary")),
    )(q, k, v, o, lse, do)


HG_NB = 6
C = HG_CHUNK
_HI = lax.Precision.HIGHEST


def _tri(lower):
    r = lax.broadcasted_iota(jnp.int32, (C, C), 0)
    c = lax.broadcasted_iota(jnp.int32, (C, C), 1)
    return ((c <= r) if lower else (c >= r)).astype(F32)


def _hg_intra_fwd(q, k, v, b):
    o = jnp.zeros((C, HG_DV), F32)
    rows = lax.broadcasted_iota(jnp.int32, (C, 1), 0)
    for s in range(C):
        lo = (s // 8) * 8
        e = jnp.exp(jnp.minimum(b[lo:] - b[s:s + 1], 0.0))
        a = jnp.sum(q[lo:] * k[s:s + 1] * e, axis=-1, keepdims=True)
        a = jnp.where(rows[lo:] >= s, a, 0.0)
        upd = a * v[s:s + 1]
        o = o + (upd if lo == 0 else jnp.concatenate([jnp.zeros((lo, HG_DV), F32), upd], axis=0))
    return o


def _hgrn_fwd(u, kk, lf, vv, T, name):
    nb = _pick(T // C, (HG_NB, 3, 2, 1))
    rows = nb * C
    qblk = C_HQ // HG_DK

    def body(q_ref, k_ref, lf_ref, v_ref, o_ref, st_ref, st):
        @pl.when(pl.program_id(1) == 0)
        def _():
            st[...] = jnp.zeros_like(st)

        lower = _tri(True)
        for n in range(nb):
            sl = slice(n * C, (n + 1) * C)
            q, k, v = q_ref[sl, :], k_ref[sl, :], v_ref[sl, :]
            b = jnp.dot(lower, lf_ref[sl, :], precision=_HI, preferred_element_type=F32)
            s_t = st[...]
            st_ref[0, n] = s_t
            qe = (q * jnp.exp(b)).astype(BF16)
            o = lax.dot_general(qe, s_t.astype(BF16), _NT, preferred_element_type=F32)
            o_ref[sl, :] = o + _hg_intra_fwd(q, k, v, b)
            bl = b[C - 1:C, :]
            kd = (k * jnp.exp(bl - b)).astype(BF16)
            st[...] = s_t * jnp.exp(bl) + lax.dot_general(v.astype(BF16), kd, _TN, preferred_element_type=F32)

    col = lambda off: pl.BlockSpec((rows, HG_DK), lambda h, c: (c, h + off))
    return pl.pallas_call(
        body, name=name, grid=(HG_HEADS, T // rows),
        in_specs=[col(qblk), col(0), col(0), col(0)],
        out_specs=[col(0), pl.BlockSpec((1, nb, HG_DV, HG_DK), lambda h, c: (h, c, 0, 0))],
        out_shape=[jax.ShapeDtypeStruct((T, HG_HEADS * HG_DV), F32),
                   jax.ShapeDtypeStruct((HG_HEADS, T // C, HG_DV, HG_DK), F32)],
        scratch_shapes=[pltpu.VMEM((HG_DV, HG_DK), F32)],
        compiler_params=_cparams(("parallel", "arbitrary")),
    )(u, kk, lf, vv)


def _hgrn_bwd(u, kk, lf, vv, states, do, T, name):
    nb = _pick(T // C, (HG_NB, 3, 2, 1))
    rows = nb * C
    n_steps = T // rows
    qblk = C_HQ // HG_DK

    def body(q_ref, k_ref, lf_ref, v_ref, st_ref, do_ref, dq_ref, dk_ref, dlf_ref, dv_ref, dst, dk_s, dv_s):
        @pl.when(pl.program_id(1) == 0)
        def _():
            dst[...] = jnp.zeros_like(dst)

        lower, upper = _tri(True), _tri(False)
        rid = lax.broadcasted_iota(jnp.int32, (C, 1), 0)
        for n in reversed(range(nb)):
            sl = slice(n * C, (n + 1) * C)
            q, k, v, do = q_ref[sl, :], k_ref[sl, :], v_ref[sl, :], do_ref[sl, :]
            b = jnp.dot(lower, lf_ref[sl, :], precision=_HI, preferred_element_type=F32)
            s_t = st_ref[0, n]
            d_new = dst[...]
            eb = jnp.exp(b)
            bl = b[C - 1:C, :]
            ebl = jnp.exp(bl)
            dec = jnp.exp(bl - b)
            qe = q * eb
            kd = k * dec
            do_b = do.astype(BF16)
            dqe = jnp.dot(do_b, s_t.astype(BF16), preferred_element_type=F32)
            dkd = jnp.dot(v.astype(BF16), d_new.astype(BF16), preferred_element_type=F32)
            dv = lax.dot_general(kd.astype(BF16), d_new.astype(BF16), _NT, preferred_element_type=F32)
            dbl = ebl * jnp.sum(d_new * s_t, axis=0, keepdims=True) + jnp.sum(dkd * kd, axis=0, keepdims=True)
            dst[...] = d_new * ebl + lax.dot_general(do_b, qe.astype(BF16), _TN, preferred_element_type=F32)
            dq = dqe * eb
            dk = dkd * dec
            for s in range(C):
                lo = (s // 8) * 8
                e = jnp.exp(jnp.minimum(b[lo:] - b[s:s + 1], 0.0))
                e = jnp.where(rid[lo:] >= s, e, 0.0)
                a = jnp.sum(q[lo:] * k[s:s + 1] * e, axis=-1, keepdims=True)
                da = jnp.sum(do[lo:] * v[s:s + 1], axis=-1, keepdims=True)
                g = da * e
                upd = g * k[s:s + 1]
                dq = dq + (upd if lo == 0 else jnp.concatenate([jnp.zeros((lo, HG_DK), F32), upd], axis=0))
                dk_s[s:s + 1, :] = jnp.sum(g * q[lo:], axis=0, keepdims=True)
                dv_s[s:s + 1, :] = jnp.sum(a * do[lo:], axis=0, keepdims=True)
            dk = dk + dk_s[...]
            dv = dv + dv_s[...]
            db = q * dq - k * dk
            db = db + jnp.where(rid == C - 1, dbl, 0.0)
            dq_ref[sl, :] = dq
            dk_ref[sl, :] = dk
            dv_ref[sl, :] = dv
            dlf_ref[sl, :] = jnp.dot(upper, db, precision=_HI, preferred_element_type=F32)

    rev = lambda off: pl.BlockSpec((rows, HG_DK), lambda h, c: (n_steps - 1 - c, h + off))
    return pl.pallas_call(
        body, name=name, grid=(HG_HEADS, n_steps),
        in_specs=[rev(qblk), rev(0), rev(0), rev(0),
                  pl.BlockSpec((1, nb, HG_DV, HG_DK), lambda h, c: (h, n_steps - 1 - c, 0, 0)), rev(0)],
        out_specs=[rev(0)] * 4,
        out_shape=[jax.ShapeDtypeStruct((T, HG_HEADS * HG_DK), F32)] * 4,
        scratch_shapes=[pltpu.VMEM((HG_DV, HG_DK), F32), pltpu.VMEM((C, HG_DK), F32), pltpu.VMEM((C, HG_DV), F32)],
        compiler_params=_cparams(("parallel", "arbitrary")),
    )(u, kk, lf, vv, states, do)


def _rope_tables(T):
    half = ROPE // 2
    pos = (np.arange(T, dtype=np.float32) - PAD_FRONT).astype(np.float32)
    inv_freq = (ROPE_BASE ** (-np.arange(half, dtype=np.float32) / half)).astype(np.float32)
    ang = pos[:, None] * inv_freq[None, :]
    cos, sin = jnp.cos(jnp.asarray(ang)), jnp.sin(jnp.asarray(ang))
    z = jnp.zeros((T, HEAD_W), F32)
    c = z.at[:, :NOPE].set(1.0).at[:, NOPE:NOPE + half].set(cos).at[:, NOPE + half:NOPE + ROPE].set(cos)
    s1 = z.at[:, NOPE:NOPE + half].set(-sin)
    s2 = z.at[:, NOPE + half:NOPE + ROPE].set(sin)
    return c, s1, s2


def _layer_fwd(x, w, tabs, T, l):
    c, s1, s2 = tabs
    n = lambda s: f"l{l}_{s}"
    sv = {"x": x}
    h = _rowwise(_f_rms, T, [Row(x)], [(w["norm1_g"], D_MODEL)], [(D_MODEL, BF16)], n("norm1"))[0]
    u = _mm(h, w["w_in"], name=n("in_proj"))
    sv.update(h=h, u=u)
    hglu = _rowwise(_f_glu, T, [Row(u, 512, C_CONV_A), Row(u, 512, C_CONV_G)], [], [(CONV_DIM, F32)], n("glu"))[0]
    cv = _conv_fwd(hglu, w["conv_w"], w["conv_b"], T, n("conv"))
    hc = _rowwise(_f_lnsilu, T, [Row(cv)], [(w["conv_ln_g"], CONV_DIM), (w["conv_ln_b"], CONV_DIM)],
                  [(CONV_DIM, BF16)], n("conv_ln"))[0]
    y_a = _mm(hc, w["w_conv_out"], name=n("conv_out"))
    sv.update(hglu=hglu, cv=cv, hc=hc, y_a=y_a)
    cqn = _rowwise(_f_rms, T, [Row(u, Q_RANK, C_CQ)], [(w["q_a_norm_g"], Q_RANK)], [(Q_RANK, BF16)], n("q_a_norm"))[0]
    ckvn = _rowwise(_f_rms, T, [Row(u, KV_RANK, C_CKV)], [(w["kv_a_norm_g"], KV_RANK)], [(KV_RANK, BF16)], n("kv_a_norm"))[0]
    q_raw = _mm(cqn, w["w_uq"], name=n("uq"))
    k_raw = _mm(ckvn, w["w_uk"], name=n("uk"))
    v = _mm(ckvn, w["w_uv"], out_dtype=BF16, name=n("uv"))
    tab_rows = [Row(c), Row(s1), Row(s2)]
    q = _rowwise(_f_qrope, T, [Row(q_raw, piece=HEAD_W)] + tab_rows, [(w["q_norm_g"], HEAD_W)],
                 [(HEADS * HEAD_W, BF16)], n("q_rope"))[0]
    k = _rowwise(_f_krope, T, [Row(k_raw, piece=HEAD_W), Row(u, HEAD_W, C_KR)] + tab_rows, [(w["k_norm_g"], HEAD_W)],
                 [(HEADS * HEAD_W, BF16)], n("k_rope"))[0]
    o, lse = _attn_fwd(q, k, v, T, n("attn"))
    y_b = _mm(o, w["w_attn_out"], name=n("attn_out"))
    sv.update(cqn=cqn, ckvn=ckvn, q_raw=q_raw, k_raw=k_raw, v=v, q=q, k=k, o=o, lse=lse, y_b=y_b)
    kk, lf, vv = _rowwise(_f_hgrn_prep, T, [Row(u, 512, C_HF), Row(u, 512, C_HI)], [(w["lb"], 512)],
                          [(512, F32)] * 3, n("hgrn_prep"))
    o_h, states = _hgrn_fwd(u, kk, lf, vv, T, n("hgrn"))
    oh = _rowwise(_f_hgrn_out, T, [Row(o_h, piece=HG_DV), Row(u, 512, C_HG, piece=HG_DV)], [(w["hgrn_norm_g"], HG_DV)],
                  [(512, BF16)], n("hgrn_out_norm"))[0]
    y_c = _mm(oh, w["w_hgrn_out"], name=n("hgrn_out"))
    sv.update(kk=kk, lf=lf, vv=vv, o_h=o_h, states=states, oh=oh, y_c=y_c)
    gate_rows = [Row(u, D_MODEL, C_GATE + g * D_MODEL) for g in range(3)]
    mix = _rowwise(_f_mix, T, gate_rows + [Row(y_a), Row(y_b), Row(y_c)], [], [(D_MODEL, BF16)], n("mix"))[0]
    x1 = _mm(mix, w["w_out"], res=x, name=n("out_proj"))
    h2 = _rowwise(_f_rms, T, [Row(x1)], [(w["norm2_g"], D_MODEL)], [(D_MODEL, BF16)], n("norm2"))[0]
    f = _mm(h2, w["w_ff1"], name=n("ff1"))
    r = _rowwise(_f_relu2, T, [Row(f)], [], [(D_FF, BF16)], n("relu2"))[0]
    x2 = _mm(r, w["w_ff2"], res=x1, name=n("ff2"))
    sv.update(mix=mix, x1=x1, h2=h2, f=f, r=r)
    return x2, sv


def _layer_bwd(dx2, w, sv, tabs, T, l):
    c, s1, s2 = tabs
    n = lambda s: f"l{l}_b_{s}"
    u = sv["u"]
    g = {}
    dx2_b = dx2
    g["w_ff2"] = _mm(sv["r"], dx2_b, ta=True, name=n("dw_ff2"))
    dr = _mm(dx2_b, w["w_ff2"], tb=True, name=n("d_r"))
    (df,), _ = _rowwise_bwd(_f_relu2, T, [Row(sv["f"])], [], [Row(dr)], {0: BF16}, n("relu2"))
    g["w_ff1"] = _mm(sv["h2"], df, ta=True, name=n("dw_ff1"))
    dh2 = _mm(df, w["w_ff1"], tb=True, name=n("d_h2"))
    (dx1,), (g["norm2_g"],) = _rowwise_bwd(_f_rms, T, [Row(sv["x1"])], [(w["norm2_g"], D_MODEL)], [Row(dh2)],
                                           {0: F32}, n("norm2"), add=(0, dx2))
    g["w_out"] = _mm(sv["mix"], dx1, ta=True, name=n("dw_out"))
    dmix = _mm(dx1, w["w_out"], tb=True, name=n("d_mix"))
    gate_rows = [Row(u, D_MODEL, C_GATE + i * D_MODEL) for i in range(3)]
    (dg0, dg1, dg2, dy_a, dy_b, dy_c), _ = _rowwise_bwd(
        _f_mix, T, gate_rows + [Row(sv["y_a"]), Row(sv["y_b"]), Row(sv["y_c"])], [], [Row(dmix)],
        {0: BF16, 1: BF16, 2: BF16, 3: BF16, 4: BF16, 5: BF16}, n("mix"))
    g["w_hgrn_out"] = _mm(sv["oh"], dy_c, ta=True, name=n("dw_hgrn_out"))
    doh = _mm(dy_c, w["w_hgrn_out"], tb=True, name=n("d_oh"))
    (do_h, dhg), (g["hgrn_norm_g"],) = _rowwise_bwd(
        _f_hgrn_out, T, [Row(sv["o_h"], piece=HG_DV), Row(u, 512, C_HG, piece=HG_DV)], [(w["hgrn_norm_g"], HG_DV)],
        [Row(doh, piece=HG_DV)], {0: F32, 1: BF16}, n("hgrn_out_norm"))
    dhq, dkk, dlf, dvv = _hgrn_bwd(u, sv["kk"], sv["lf"], sv["vv"], sv["states"], do_h, T, n("hgrn"))
    (dhf, dhi), (g["lb"],) = _rowwise_bwd(
        _f_hgrn_prep, T, [Row(u, 512, C_HF), Row(u, 512, C_HI)], [(w["lb"], 512)],
        [Row(dkk), Row(dlf), Row(dvv)], {0: BF16, 1: BF16}, n("hgrn_prep"))
    g["w_attn_out"] = _mm(sv["o"], dy_b, ta=True, name=n("dw_attn_out"))
    do = _mm(dy_b, w["w_attn_out"], tb=True, out_dtype=BF16, name=n("d_o"))
    dq, dk, dv = _attn_bwd(sv["q"], sv["k"], sv["v"], sv["o"], sv["lse"], do, T, n("attn"))
    tab_rows = [Row(c), Row(s1), Row(s2)]
    (dq_raw,), (g["q_norm_g"],) = _rowwise_bwd(
        _f_qrope, T, [Row(sv["q_raw"], piece=HEAD_W)] + tab_rows, [(w["q_norm_g"], HEAD_W)],
        [Row(dq, piece=HEAD_W)], {0: BF16}, n("q_rope"))
    (dk_raw, dkr), (g["k_norm_g"],) = _rowwise_bwd(
        _f_krope, T, [Row(sv["k_raw"], piece=HEAD_W), Row(u, HEAD_W, C_KR)] + tab_rows, [(w["k_norm_g"], HEAD_W)],
        [Row(dk, piece=HEAD_W)], {0: BF16, 1: BF16}, n("k_rope"))
    g["w_uq"] = _mm(sv["cqn"], dq_raw, ta=True, name=n("dw_uq"))
    g["w_uk"] = _mm(sv["ckvn"], dk_raw, ta=True, name=n("dw_uk"))
    g["w_uv"] = _mm(sv["ckvn"], dv, ta=True, name=n("dw_uv"))
    dcqn = _mm(dq_raw, w["w_uq"], tb=True, name=n("d_cqn"))
    dckvn = _mm(dk_raw, w["w_uk"], tb=True, name=n("d_ckvn_k"))
    dckvn = _mm(dv, w["w_uv"], tb=True, res=dckvn, name=n("d_ckvn_v"))
    (dcq,), (g["q_a_norm_g"],) = _rowwise_bwd(_f_rms, T, [Row(u, Q_RANK, C_CQ)], [(w["q_a_norm_g"], Q_RANK)],
                                              [Row(dcqn)], {0: BF16}, n("q_a_norm"))
    (dckv,), (g["kv_a_norm_g"],) = _rowwise_bwd(_f_rms, T, [Row(u, KV_RANK, C_CKV)], [(w["kv_a_norm_g"], KV_RANK)],
                                                [Row(dckvn)], {0: BF16}, n("kv_a_norm"))
    g["w_conv_out"] = _mm(sv["hc"], dy_a, ta=True, name=n("dw_conv_out"))
    dhc = _mm(dy_a, w["w_conv_out"], tb=True, name=n("d_hc"))
    (dcv,), (g["conv_ln_g"], g["conv_ln_b"]) = _rowwise_bwd(
        _f_lnsilu, T, [Row(sv["cv"])], [(w["conv_ln_g"], CONV_DIM), (w["conv_ln_b"], CONV_DIM)], [Row(dhc)],
        {0: F32}, n("conv_ln"))
    dhglu, dconv_w, g["conv_b"] = _conv_bwd(sv["hglu"], w["conv_w"], dcv, T, n("conv"))
    g["conv_w"] = dconv_w[:CONV_K]
    (dua, dug), _ = _rowwise_bwd(_f_glu, T, [Row(u, 512, C_CONV_A), Row(u, 512, C_CONV_G)], [], [Row(dhglu)],
                                 {0: BF16, 1: BF16}, n("glu"))
    du = jnp.concatenate([dua, dug, dg0, dg1, dg2, dcq, dckv, dkr, dhq.astype(BF16), dhf, dhi, dhg], axis=1)
    g["w_in"] = _mm(sv["h"], du, ta=True, name=n("dw_in"))
    dh = _mm(du, w["w_in"], tb=True, name=n("d_h"))
    (dx,), (g["norm1_g"],) = _rowwise_bwd(_f_rms, T, [Row(sv["x"])], [(w["norm1_g"], D_MODEL)], [Row(dh)],
                                          {0: F32}, n("norm1"), add=(0, dx1))
    return dx, g


def _pad_w_in(w_in):
    z = lambda k: jnp.zeros((w_in.shape[0], k), w_in.dtype)
    return jnp.concatenate([w_in[:, :O_CQ], w_in[:, O_GATE:], w_in[:, O_CQ:O_KR], z(KR_LANE), w_in[:, O_KR:O_HQ],
                            z(HEAD_W - KR_LANE - ROPE), w_in[:, O_HQ:O_GATE]], axis=1)


def _unpad_w_in(g):
    return jnp.concatenate([g[:, :C_GATE], g[:, C_CQ:C_KR], g[:, C_KR + KR_LANE:C_KR + KR_LANE + ROPE],
                            g[:, C_HQ:], g[:, C_GATE:C_CQ]], axis=1)


def _pad_heads(wm, per_head, lo, hi):
    lead = wm.shape[:-1]
    wh = wm.reshape(lead + (HEADS, per_head))[..., lo:hi]
    pad = [(0, 0)] * len(lead) + [(0, 0), (0, HEAD_W - (hi - lo))]
    return jnp.pad(wh, pad).reshape(lead + (HEADS * HEAD_W,))


def _unpad_heads(gm, width):
    lead = gm.shape[:-1]
    return gm.reshape(lead + (HEADS, HEAD_W))[..., :width]


def _layer_weights(full, lbs, l):
    w = {}
    w["norm1_g"] = full["norm1_g"][l]
    w["w_in"] = _pad_w_in(full["w_in"][l])
    w["conv_w"] = full["conv_w"][l]
    w["conv_b"] = full["conv_b"][l]
    w["conv_ln_g"] = full["conv_ln_g"][l]
    w["conv_ln_b"] = full["conv_ln_b"][l]
    w["w_conv_out"] = full["w_conv_out"][l]
    w["q_a_norm_g"] = full["q_a_norm_g"][l]
    w["w_uq"] = _pad_heads(full["w_uq"][l], QK_DIM, 0, QK_DIM)
    w["kv_a_norm_g"] = full["kv_a_norm_g"][l]
    w["w_uk"] = _pad_heads(full["w_ukv"][l], NOPE + V_DIM, 0, NOPE)
    w["w_uv"] = _pad_heads(full["w_ukv"][l], NOPE + V_DIM, NOPE, NOPE + V_DIM)
    w["q_norm_g"] = jnp.pad(full["q_norm_g"][l], (0, HEAD_W - QK_DIM))
    w["k_norm_g"] = jnp.pad(full["k_norm_g"][l], (0, HEAD_W - QK_DIM))
    wa = full["w_attn_out"][l].reshape(HEADS, V_DIM, D_MODEL)
    w["w_attn_out"] = jnp.pad(wa, ((0, 0), (0, HEAD_W - V_DIM), (0, 0))).reshape(HEADS * HEAD_W, D_MODEL)
    w["lb"] = lbs[l]
    w["hgrn_norm_g"] = full["hgrn_norm_g"][l]
    w["w_hgrn_out"] = full["w_hgrn_out"][l]
    w["w_out"] = full["w_out"][l]
    w["norm2_g"] = full["norm2_g"][l]
    w["w_ff1"] = full["w_ff1"][l]
    w["w_ff2"] = full["w_ff2"][l]
    return w


def _layer_grads_to_original(g):
    o = {}
    for name in ("w_conv_out", "w_hgrn_out", "w_out", "w_ff1", "w_ff2", "conv_w"):
        o[name] = g[name]
    for name in ("norm1_g", "conv_b", "conv_ln_g", "conv_ln_b", "q_a_norm_g", "kv_a_norm_g", "hgrn_norm_g", "norm2_g", "lb"):
        o[name] = g[name].reshape(-1)
    o["w_in"] = _unpad_w_in(g["w_in"])
    o["w_uq"] = _unpad_heads(g["w_uq"], QK_DIM).reshape(Q_RANK, HEADS * QK_DIM)
    guk = _unpad_heads(g["w_uk"], NOPE)
    guv = _unpad_heads(g["w_uv"], V_DIM)
    o["w_ukv"] = jnp.concatenate([guk, guv], axis=-1).reshape(KV_RANK, HEADS * (NOPE + V_DIM))
    o["q_norm_g"] = g["q_norm_g"].reshape(-1)[:QK_DIM]
    o["k_norm_g"] = g["k_norm_g"].reshape(-1)[:QK_DIM]
    o["w_attn_out"] = g["w_attn_out"].reshape(HEADS, HEAD_W, D_MODEL)[:, :V_DIM].reshape(HEADS * V_DIM, D_MODEL)
    return o


def _lower_bounds(logits):
    p = jax.nn.softmax(logits.astype(F32), axis=0)
    return jnp.cumsum(p, axis=0) - p[0:1]


def _local_step(x, target, full):
    seq = x.shape[0]
    T = ROW0 + seq
    assert T % 128 == 0
    tabs = _rope_tables(T)
    lbs, lb_vjp = jax.vjp(_lower_bounds, full["hgrn_lb_logits"])
    xp = jnp.concatenate([jnp.zeros((PAD_FRONT, D_MODEL), F32), full["meta"].astype(F32), x], axis=0)
    tp = jnp.concatenate([jnp.zeros((ROW0, D_MODEL), F32), target], axis=0)
    ws, svs = [], []
    for l in range(DEPTH):
        w = _layer_weights(full, lbs, l)
        xp, sv = _layer_fwd(xp, w, tabs, T, l)
        ws.append(w)
        svs.append(sv)
    dx, sq = _loss_head(xp, tp, T)
    loss = 0.5 * jnp.sum(sq) * (1.0 / D_MODEL)
    per_layer = [None] * DEPTH
    for l in reversed(range(DEPTH)):
        dx, g = _layer_bwd(dx, ws[l], svs[l], tabs, T, l)
        per_layer[l] = _layer_grads_to_original(g)
    grads = {k: jnp.stack([per_layer[l][k] for l in range(DEPTH)]) for k in per_layer[0]}
    grads["hgrn_lb_logits"] = lb_vjp(grads.pop("lb"))[0]
    grads["meta"] = dx[PAD_FRONT:ROW0]
    return loss, dx[ROW0:], grads


def _mesh_pos():
    return lax.axis_index("x"), lax.axis_index("y"), lax.axis_index("c")


N_COPY = N_DEV - 1


def _all_gather(arrs, name):
    n = len(arrs)

    def body(*refs):
        x_refs, out_refs = refs[:n], refs[n:2 * n]
        send_sems, recv_sems, local_sems = refs[2 * n:]
        x, y, c = _mesh_pos()
        me, sibling = (x, y, c), (x, y, 1 - c)
        chips = [(1 - x, y), (x, 1 - y), (1 - x, 1 - y)]

        def slot(a, px, py, pc):
            return out_refs[a].at[4 * px + 2 * py + pc]

        def copy(a, k, block, to, own=False):
            return pltpu.make_async_remote_copy(
                src_ref=x_refs[a] if own else slot(a, *block), dst_ref=slot(a, *block),
                send_sem=send_sems.at[a * N_COPY + k], recv_sem=recv_sems.at[a * N_COPY + k],
                device_id=to, device_id_type=MESH)

        mine = [pltpu.make_async_copy(x_refs[a], slot(a, *me), local_sems.at[a]) for a in range(n)]
        for cp in mine:
            cp.start()
        first = []
        for a in range(n):
            first.append(copy(a, 0, me, sibling, own=True))
            first += [copy(a, 1 + j, me, (*chip, c), own=True) for j, chip in enumerate(chips)]
        for cp in first:
            cp.start()
        passed = []
        for j, chip in enumerate(chips):
            for a in range(n):
                copy(a, 1 + j, (*chip, c), me).wait_recv()
                cp = copy(a, 4 + j, (*chip, c), sibling)
                cp.start()
                passed.append(cp)
        for a in range(n):
            copy(a, 0, sibling, me).wait_recv()
            for j, chip in enumerate(chips):
                copy(a, 4 + j, (*chip, 1 - c), me).wait_recv()
        for cp in first + passed:
            cp.wait_send()
        for cp in mine:
            cp.wait()

    anyspec = pl.BlockSpec(memory_space=pl.ANY)
    return pl.pallas_call(
        body, name=name, out_shape=[jax.ShapeDtypeStruct((N_DEV,) + a.shape, a.dtype) for a in arrs],
        in_specs=[anyspec] * n, out_specs=[anyspec] * n,
        scratch_shapes=[pltpu.SemaphoreType.DMA((n * N_COPY,)), pltpu.SemaphoreType.DMA((n * N_COPY,)),
                        pltpu.SemaphoreType.DMA((n,))],
    )(*arrs)


def _exchange(arrs, name):
    n = len(arrs)

    def body(*refs):
        s_refs, r_refs = refs[:n], refs[n:2 * n]
        send_sems, recv_sems, local_sems = refs[2 * n:]
        x, y, c = _mesh_pos()
        me = 4 * x + 2 * y + c
        local = [pltpu.make_async_copy(s_refs[a].at[me], r_refs[a].at[me], local_sems.at[a]) for a in range(n)]
        for cp in local:
            cp.start()
        sends, recvs = [], []
        for rel in range(1, N_DEV):
            px = 1 - x if rel & 4 else x
            py = 1 - y if rel & 2 else y
            pc = 1 - c if rel & 1 else c
            p = 4 * px + 2 * py + pc
            for a in range(n):
                k = a * N_COPY + rel - 1
                sends.append(pltpu.make_async_remote_copy(
                    src_ref=s_refs[a].at[p], dst_ref=r_refs[a].at[me], send_sem=send_sems.at[k],
                    recv_sem=recv_sems.at[k], device_id=(px, py, pc), device_id_type=MESH))
                recvs.append(pltpu.make_async_remote_copy(
                    src_ref=s_refs[a].at[me], dst_ref=r_refs[a].at[p], send_sem=send_sems.at[k],
                    recv_sem=recv_sems.at[k], device_id=(px, py, pc), device_id_type=MESH))
        for cp in sends:
            cp.start()
        for cp in recvs:
            cp.wait_recv()
        for cp in sends:
            cp.wait_send()
        for cp in local:
            cp.wait()

    anyspec = pl.BlockSpec(memory_space=pl.ANY)
    return pl.pallas_call(
        body, name=name, out_shape=[jax.ShapeDtypeStruct(a.shape, a.dtype) for a in arrs],
        in_specs=[anyspec] * n, out_specs=[anyspec] * n,
        scratch_shapes=[pltpu.SemaphoreType.DMA((n * N_COPY,)), pltpu.SemaphoreType.DMA((n * N_COPY,)),
                        pltpu.SemaphoreType.DMA((n,))],
    )(*arrs)


def _sum_parts(parts, name):
    P, R, W = parts.shape

    def body(p_ref, o_ref):
        g = p_ref[0].astype(F32)
        for i in range(1, P):
            g = g + p_ref[i].astype(F32)
        o_ref[...] = g

    return pl.pallas_call(body, name=name, out_shape=jax.ShapeDtypeStruct((R, W), F32))(parts)


def _adamw_body(p_ref, w_ref, m_ref, v_ref, g_ref, d_ref, nm_ref, nv_ref):
    g = p_ref[0].astype(F32)
    for i in range(1, p_ref.shape[0]):
        g = g + p_ref[i].astype(F32)
    m_new = ADAM_B1 * m_ref[...] + (1.0 - ADAM_B1) * g
    v_new = ADAM_B2 * v_ref[...] + (1.0 - ADAM_B2) * jnp.square(g)
    m_hat = m_new / (1.0 - ADAM_B1 ** ADAM_STEP)
    v_hat = v_new / (1.0 - ADAM_B2 ** ADAM_STEP)
    g_ref[...] = g
    d_ref[...] = -ADAM_LR * (m_hat / (jnp.sqrt(v_hat) + ADAM_EPS) + ADAM_WD * w_ref[...])
    nm_ref[...] = m_new
    nv_ref[...] = v_new


def _adamw(parts, w, m, v, name):
    P, R, W = parts.shape
    tr = _pick(R, (368, 192, 64, 16, 8))
    spec = pl.BlockSpec((tr, W), lambda i: (i, 0))
    return pl.pallas_call(
        functools.partial(_adamw_body), name=name, grid=(R // tr,),
        in_specs=[pl.BlockSpec((P, tr, W), lambda i: (0, i, 0)), spec, spec, spec], out_specs=[spec] * 4,
        out_shape=[jax.ShapeDtypeStruct((R, W), F32)] * 4,
        compiler_params=_cparams(("parallel",)),
    )(parts, w, m, v)


def _adamw3(parts, w, m, v, name):
    P, A, B, C_ = parts.shape
    tb = _pick(B, (256, 128))
    spec = pl.BlockSpec((1, tb, C_), lambda a, i: (a, i, 0))
    return pl.pallas_call(
        functools.partial(_adamw_body), name=name, grid=(A, B // tb),
        in_specs=[pl.BlockSpec((P, 1, tb, C_), lambda a, i: (0, a, i, 0)), spec, spec, spec], out_specs=[spec] * 4,
        out_shape=[jax.ShapeDtypeStruct((A, B, C_), F32)] * 4,
        compiler_params=_cparams(("parallel", "parallel")),
    )(parts, w, m, v)


PACK_W = 1024
BIG = (("w_in", (DEPTH, D_MODEL, N_IN // N_DEV), 2), ("w_conv_out", (DEPTH, CONV_DIM, D_MODEL // N_DEV), 2),
       ("w_uq", (DEPTH, Q_RANK, HEADS * QK_DIM // N_DEV), 2), ("w_ukv", (DEPTH, KV_RANK, HEADS * (NOPE + V_DIM) // N_DEV), 2),
       ("w_attn_out", (DEPTH, HEADS * V_DIM, D_MODEL // N_DEV), 2), ("w_hgrn_out", (DEPTH, 512, D_MODEL // N_DEV), 2),
       ("w_out", (DEPTH, D_MODEL // N_DEV, D_MODEL), 1), ("w_ff1", (DEPTH, D_MODEL, D_FF // N_DEV), 2),
       ("w_ff2", (DEPTH, D_FF // N_DEV, D_MODEL), 1))
SMALL_SHARDED = (("meta", (N_META, D_MODEL // N_DEV), 1), ("conv_w", (DEPTH, CONV_K, CONV_DIM // N_DEV), 2))
REPLICATED = (("norm1_g", (DEPTH, D_MODEL)), ("conv_b", (DEPTH, CONV_DIM)), ("conv_ln_g", (DEPTH, CONV_DIM)),
              ("conv_ln_b", (DEPTH, CONV_DIM)), ("q_a_norm_g", (DEPTH, Q_RANK)), ("kv_a_norm_g", (DEPTH, KV_RANK)),
              ("q_norm_g", (DEPTH, QK_DIM)), ("k_norm_g", (DEPTH, QK_DIM)), ("hgrn_lb_logits", (DEPTH, 512)),
              ("hgrn_norm_g", (DEPTH, 512)), ("norm2_g", (DEPTH, D_MODEL)))
WEIGHT_ORDER = ("meta", "norm1_g", "w_in", "conv_w", "conv_b", "conv_ln_g", "conv_ln_b", "w_conv_out", "q_a_norm_g", "w_uq",
                "kv_a_norm_g", "w_ukv", "q_norm_g", "k_norm_g", "w_attn_out", "hgrn_lb_logits", "hgrn_norm_g", "w_hgrn_out",
                "w_out", "norm2_g", "w_ff1", "w_ff2")


def _rows_for(n_elems, mult):
    rows = -(-n_elems // PACK_W)
    return -(-rows // mult) * mult


def _pack(arrays, dtype, mult, lead=()):
    nl = len(lead)
    flat = jnp.concatenate([a.reshape(lead + (-1,)).astype(dtype) for a in arrays], axis=nl)
    rows = _rows_for(flat.shape[nl], mult)
    flat = jnp.pad(flat, [(0, 0)] * nl + [(0, rows * PACK_W - flat.shape[nl])])
    return flat.reshape(lead + (rows, PACK_W))


def _unpack(pack, shapes, lead=()):
    nl = len(lead)
    flat = pack.reshape(lead + (-1,))
    out, off = [], 0
    for shp in shapes:
        n = int(np.prod(shp))
        out.append(lax.slice_in_dim(flat, off, off + n, axis=nl).reshape(lead + tuple(shp)))
        off += n
    return out


def _join_shards(g, axis):
    g = jnp.moveaxis(g, 0, axis)
    shp = g.shape
    return g.reshape(shp[:axis] + (shp[axis] * shp[axis + 1],) + shp[axis + 2:])


def _cut_shards(a, axis):
    shp = a.shape
    a = a.reshape(shp[:axis] + (N_DEV, shp[axis] // N_DEV) + shp[axis + 1:])
    return jnp.moveaxis(a, axis, 0)


def kernel(x, meta, norm1_g, w_in, conv_w, conv_b, conv_ln_g, conv_ln_b, w_conv_out, q_a_norm_g, w_uq, kv_a_norm_g, w_ukv, q_norm_g, k_norm_g, w_attn_out, hgrn_lb_logits, hgrn_norm_g, w_hgrn_out, w_out, norm2_g, w_ff1, w_ff2, loss_target, m_meta, m_norm1_g, m_w_in, m_conv_w, m_conv_b, m_conv_ln_g, m_conv_ln_b, m_w_conv_out, m_q_a_norm_g, m_w_uq, m_kv_a_norm_g, m_w_ukv, m_q_norm_g, m_k_norm_g, m_w_attn_out, m_hgrn_lb_logits, m_hgrn_norm_g, m_w_hgrn_out, m_w_out, m_norm2_g, m_w_ff1, m_w_ff2, v_meta, v_norm1_g, v_w_in, v_conv_w, v_conv_b, v_conv_ln_g, v_conv_ln_b, v_w_conv_out, v_q_a_norm_g, v_w_uq, v_kv_a_norm_g, v_w_ukv, v_q_norm_g, v_k_norm_g, v_w_attn_out, v_hgrn_lb_logits, v_hgrn_norm_g, v_w_hgrn_out, v_w_out, v_norm2_g, v_w_ff1, v_w_ff2):
    args = dict(locals())
    wts = {n: args[n] for n in WEIGHT_ORDER}
    mom = {n: args["m_" + n] for n in WEIGHT_ORDER}
    var = {n: args["v_" + n] for n in WEIGHT_ORDER}
    xi, yi, ci = _mesh_pos()
    me = 4 * xi + 2 * yi + ci

    gathered = _all_gather([wts[n].astype(BF16) for n, _, _ in BIG]
                           + [_pack([wts[n] for n, _, _ in SMALL_SHARDED], F32, 8)], "gather_weights")
    full = {n: wts[n] for n, _ in REPLICATED}
    for (n, _, axis), g in zip(BIG, gathered):
        full[n] = _join_shards(g, axis)
    for (n, _, axis), g in zip(SMALL_SHARDED, _unpack(gathered[-1], [s for _, s, _ in SMALL_SHARDED], (N_DEV,))):
        full[n] = _join_shards(g, axis)

    loss, grad_x, grads = _local_step(x[0], loss_target[0], full)
    loss = lax.psum(loss, ("x", "y", "c"))

    recv = _exchange([_cut_shards(grads[n], axis).astype(BF16) for n, _, axis in BIG], "scatter_grads")
    out = {}
    for (n, _, _), r in zip(BIG, recv):
        res4 = _adamw3(r, wts[n], mom[n], var[n], "adamw_" + n)
        for kind, a in zip(("grad_", "delta_", "new_m_", "new_v_"), res4):
            out[kind + n] = a

    small_names = [n for n, _ in REPLICATED] + [n for n, _, _ in SMALL_SHARDED]
    part = _pack([grads[n] for n in small_names], F32, 8)
    total = _sum_parts(_all_gather([part], "gather_small_grads")[0], "sum_small_grads")
    tot = dict(zip(small_names, _unpack(total, [grads[n].shape for n in small_names])))
    mine = {n: tot[n] for n, _ in REPLICATED}
    for n, shp, axis in SMALL_SHARDED:
        mine[n] = lax.dynamic_slice_in_dim(tot[n], me * shp[axis], shp[axis], axis=axis)
    pk = lambda d: _pack([d[n] for n in small_names], F32, 8)
    small_out = _adamw(pk(mine)[None], pk(wts), pk(mom), pk(var), "adamw_vectors")
    for kind, pack in zip(("grad_", "delta_", "new_m_", "new_v_"), small_out):
        for n, a in zip(small_names, _unpack(pack, [wts[n].shape for n in small_names])):
            out[kind + n] = a

    res = [loss, grad_x[None]]
    for kind in ("grad_", "delta_", "new_m_", "new_v_"):
        res += [out[kind + n] for n in WEIGHT_ORDER]
    return tuple(res)
```

```python
import functools

import numpy as np
import jax
import jax.numpy as jnp
from jax import lax
from jax.experimental import pallas as pl
from jax.experimental.pallas import tpu as pltpu

F32 = jnp.float32
BF16 = jnp.bfloat16

D_MODEL = 1024
DEPTH = 2
N_META = 16
PAD_FRONT = 112
ROW0 = PAD_FRONT + N_META
EPS = 1e-6
GATE_CLAMP = 1.0 - 1e-6
CONV_DIM = 512
CONV_K = 31
HEADS = 8
Q_RANK = 256
KV_RANK = 128
NOPE = 64
ROPE = 32
V_DIM = 64
QK_DIM = NOPE + ROPE
HEAD_W = 128
ROPE_BASE = 10000.0
HG_HEADS = 4
HG_DK = 128
HG_DV = 128
HG_CHUNK = 64
D_FF = 4096
N_IN = 6560
C_CONV_A, C_CONV_G, C_GATE, C_CQ, C_CKV, C_KR, C_HQ, C_HF, C_HI, C_HG = (
    0, 512, 1024, 4096, 4352, 4480, 4608, 5120, 5632, 6144)
N_IN_P = 6656
O_CQ, O_KR, O_HQ, O_GATE = 1024, 1408, 1440, 3488
KR_LANE = NOPE

ADAM_LR = 0.001
ADAM_B1 = 0.9
ADAM_B2 = 0.999
ADAM_EPS = 1e-08
ADAM_WD = 0.01
ADAM_STEP = 10

N_DEV = 8
VMEM_LIMIT = 56 * 1024 * 1024
MESH = pl.DeviceIdType.MESH


def _pick(n, cands):
    for c in cands:
        if n % c == 0:
            return c
    raise ValueError(f"no tile for {n}")


def _cparams(sem, **kw):
    return pltpu.CompilerParams(dimension_semantics=sem, vmem_limit_bytes=VMEM_LIMIT, **kw)


def _relu2(v):
    return jnp.square(jnp.maximum(v, 0.0))


def _mm(a, b, *, ta=False, tb=False, out_dtype=F32, res=None, a_fn=None, epi=None, name):
    M, K = (a.shape[1], a.shape[0]) if ta else a.shape
    N = b.shape[0] if tb else b.shape[1]
    assert (b.shape[1] if tb else b.shape[0]) == K, (a.shape, b.shape, ta, tb)
    tm = _pick(M, (1056, 1024, 512, 384, 256, 128, 96))
    tn = _pick(N, (1664, 1024, 512, 384, 256, 128))
    tk = _pick(K, (1664, 1056, 1024, 512, 384, 256, 128, 96))
    nk = K // tk
    dims = (((0 if ta else 1,), (1 if tb else 0,)), ((), ()))
    extras = ([res] if res is not None else []) + ([epi[0]] if epi is not None else [])

    def body(*refs):
        a_ref, b_ref = refs[0], refs[1]
        r_ref = refs[2] if res is not None else None
        e_ref = refs[2 + (res is not None)] if epi is not None else None
        o_ref = refs[2 + len(extras)]
        acc = refs[-1] if nk > 1 else None
        k = pl.program_id(2)
        av = a_ref[...]
        if a_fn is not None:
            av = a_fn(av.astype(F32))
        p = lax.dot_general(av.astype(BF16), b_ref[...].astype(BF16), dims, preferred_element_type=F32)

        def finish(total):
            if e_ref is not None:
                total = epi[1](total, e_ref[...].astype(F32))
            if r_ref is not None:
                total = total + r_ref[...].astype(F32)
            o_ref[...] = total.astype(o_ref.dtype)

        if nk == 1:
            finish(p)
        else:
            @pl.when(k == 0)
            def _():
                acc[...] = p

            @pl.when(k > 0)
            def _():
                acc[...] += p

            @pl.when(k == nk - 1)
            def _():
                finish(acc[...])

    a_spec = pl.BlockSpec((tk, tm), lambda i, j, k: (k, i)) if ta else pl.BlockSpec((tm, tk), lambda i, j, k: (i, k))
    b_spec = pl.BlockSpec((tn, tk), lambda i, j, k: (j, k)) if tb else pl.BlockSpec((tk, tn), lambda i, j, k: (k, j))
    o_spec = pl.BlockSpec((tm, tn), lambda i, j, k: (i, j))
    in_specs = [a_spec, b_spec] + [o_spec] * len(extras)
    args = (a, b) + tuple(extras)
    return pl.pallas_call(
        body, name=name, grid=(M // tm, N // tn, nk), in_specs=in_specs, out_specs=o_spec,
        out_shape=jax.ShapeDtypeStruct((M, N), out_dtype),
        scratch_shapes=[pltpu.VMEM((tm, tn), F32)] if nk > 1 else [],
        compiler_params=_cparams(("parallel", "parallel", "arbitrary")),
    )(*args)


class Row:
    def __init__(self, arr, width=None, col=0, piece=None):
        self.arr = arr
        self.width = arr.shape[1] if width is None else width
        assert col % self.width == 0
        self.blk = col // self.width
        self.piece = self.width if piece is None else piece

    def spec(self, tm):
        blk = self.blk
        return pl.BlockSpec((tm, self.width), lambda i: (i, blk))


def _split(v, piece):
    w = v.shape[-1]
    if piece == w:
        return v
    return [v[:, j * piece:(j + 1) * piece] for j in range(w // piece)]


def _store(ref, val, dtype=None):
    if isinstance(val, (list, tuple)):
        piece = val[0].shape[-1]
        for j, p in enumerate(val):
            ref[:, j * piece:(j + 1) * piece] = p.astype(ref.dtype)
    else:
        ref[...] = val.astype(ref.dtype)


def _row_tile(T):
    return _pick(T, (384, 352, 192, 128))


def _param2d(p):
    return p.reshape(1, -1).astype(F32)


def _rowwise(fn, T, rows, params, outs, name):
    tm = _row_tile(T)
    nr, npar = len(rows), len(params)
    par = [(_param2d(p), piece) for p, piece in params]

    def body(*refs):
        rid = pl.program_id(0) * tm + lax.broadcasted_iota(jnp.int32, (tm, 1), 0)
        rv = [_split(refs[n][...].astype(F32), rows[n].piece) for n in range(nr)]
        pv = [_split(refs[nr + n][...], par[n][1]) for n in range(npar)]
        res = fn(rid, rv, pv)
        for n, val in enumerate(res):
            _store(refs[nr + npar + n], val)

    return pl.pallas_call(
        body, name=name, grid=(T // tm,),
        in_specs=[r.spec(tm) for r in rows] + [pl.BlockSpec(p.shape, lambda i: (0, 0)) for p, _ in par],
        out_specs=[pl.BlockSpec((tm, w), lambda i: (i, 0)) for w, _ in outs],
        out_shape=[jax.ShapeDtypeStruct((T, w), dt) for w, dt in outs],
        compiler_params=_cparams(("parallel",)),
    )(*[r.arr for r in rows], *[p for p, _ in par])


def _rowwise_bwd(fn, T, rows, params, cts, drow, name, add=None):
    tm = _row_tile(T)
    nr, npar, nct = len(rows), len(params), len(cts)
    par = [(_param2d(p), piece) for p, piece in params]
    didx = sorted(drow)
    has_add = add is not None

    def body(*refs):
        i = pl.program_id(0)
        rid = i * tm + lax.broadcasted_iota(jnp.int32, (tm, 1), 0)
        rv = [_split(refs[n][...].astype(F32), rows[n].piece) for n in range(nr)]
        pv = [_split(refs[nr + n][...], par[n][1]) for n in range(npar)]
        cv = [_split(refs[nr + npar + n][...].astype(F32), cts[n].piece) for n in range(nct)]
        base = nr + npar + nct + (1 if has_add else 0)
        d_refs = refs[base:base + len(didx)]
        p_refs = refs[base + len(didx):]

        def g(dvals, pvals):
            full = list(rv)
            for n, v in zip(didx, dvals):
                full[n] = v
            return fn(rid, full, pvals)

        _, vjp = jax.vjp(g, [rv[n] for n in didx], pv)
        d_rows, d_pars = vjp(cv)
        for slot, n in enumerate(didx):
            val = d_rows[slot]
            if has_add and add[0] == n:
                assert not isinstance(val, (list, tuple))
                val = val + refs[nr + npar + nct][...].astype(F32)
            _store(d_refs[slot], val)

        @pl.when(i == 0)
        def _():
            for r in p_refs:
                r[...] = jnp.zeros_like(r)

        for r, val in zip(p_refs, d_pars):
            if isinstance(val, (list, tuple)):
                piece = val[0].shape[-1]
                for j, p in enumerate(val):
                    r[:, j * piece:(j + 1) * piece] += p
            else:
                r[...] += val

    in_specs = ([r.spec(tm) for r in rows] + [pl.BlockSpec(p.shape, lambda i: (0, 0)) for p, _ in par]
                + [c.spec(tm) for c in cts])
    args = [r.arr for r in rows] + [p for p, _ in par] + [c.arr for c in cts]
    if has_add:
        in_specs.append(pl.BlockSpec((tm, rows[add[0]].width), lambda i: (i, 0)))
        args.append(add[1])
    out_specs = ([pl.BlockSpec((tm, rows[n].width), lambda i: (i, 0)) for n in didx]
                 + [pl.BlockSpec(p.shape, lambda i: (0, 0)) for p, _ in par])
    out_shape = ([jax.ShapeDtypeStruct((T, rows[n].width), drow[n]) for n in didx]
                 + [jax.ShapeDtypeStruct(p.shape, F32) for p, _ in par])
    res = pl.pallas_call(
        body, name=name, grid=(T // tm,), in_specs=in_specs, out_specs=out_specs, out_shape=out_shape,
        compiler_params=_cparams(("arbitrary",)),
    )(*args)
    return list(res[:len(didx)]), list(res[len(didx):])


def _f_rms(rid, rv, pv):
    x, g = rv[0], pv[0]
    return [x * lax.rsqrt(jnp.mean(x * x, axis=-1, keepdims=True) + EPS) * g]


def _f_glu(rid, rv, pv):
    a, gt = rv
    return [a * jax.nn.sigmoid(gt) * (rid >= PAD_FRONT).astype(F32)]


def _f_lnsilu(rid, rv, pv):
    x = rv[0]
    g, b = pv
    mu = jnp.mean(x, axis=-1, keepdims=True)
    xc = x - mu
    y = xc * lax.rsqrt(jnp.mean(xc * xc, axis=-1, keepdims=True) + EPS) * g + b
    return [y * jax.nn.sigmoid(y)]


@functools.partial(jax.custom_vjp, nondiff_argnums=(1,))
def _lane_roll(x, shift):
    return pltpu.roll(x, shift, 1)


def _lane_roll_fwd(x, shift):
    return pltpu.roll(x, shift, 1), None


def _lane_roll_bwd(shift, _, g):
    return (pltpu.roll(g, (HEAD_W - shift) % HEAD_W, 1),)


_lane_roll.defvjp(_lane_roll_fwd, _lane_roll_bwd)


def _head_norm_rope(xh, g, c, s1, s2):
    y = xh * lax.rsqrt(jnp.sum(xh * xh, axis=-1, keepdims=True) * (1.0 / QK_DIM) + EPS) * g
    half = ROPE // 2
    return y * c + _lane_roll(y, HEAD_W - half) * s1 + _lane_roll(y, half) * s2


def _f_qrope(rid, rv, pv):
    q, c, s1, s2 = rv
    return [[_head_norm_rope(qh, pv[0], c, s1, s2) * ATT_SCALE for qh in q]]


def _f_krope(rid, rv, pv):
    k, kr, c, s1, s2 = rv
    return [[_head_norm_rope(kh + kr, pv[0], c, s1, s2) for kh in k]]


def _f_hgrn_prep(rid, rv, pv):
    hf, hi = rv
    m = (rid >= PAD_FRONT).astype(F32)
    kk = (1.0 - pv[0]) * jax.nn.sigmoid(-hf) * m
    lf = jnp.log1p(-jnp.minimum(kk, GATE_CLAMP))
    vv = hi * jax.nn.sigmoid(hi) * m
    return [kk, lf, vv]


def _f_hgrn_out(rid, rv, pv):
    o, hg = rv
    ng = pv[0]
    out = []
    for oh, gh, nh in zip(o, hg, ng):
        y = oh * lax.rsqrt(jnp.mean(oh * oh, axis=-1, keepdims=True) + EPS) * nh
        out.append(y * (gh * jax.nn.sigmoid(gh)))
    return [out]


def _f_mix(rid, rv, pv):
    g0, g1, g2, ya, yb, yc = rv
    return [jax.nn.sigmoid(g0) * ya + jax.nn.sigmoid(g1) * yb + jax.nn.sigmoid(g2) * yc]


def _f_relu2(rid, rv, pv):
    return [jnp.square(jax.nn.relu(rv[0]))]


def _loss_head(x2, tgt, T):
    tm = _row_tile(T)

    def body(x_ref, t_ref, dx_ref, l_ref):
        i = pl.program_id(0)
        rid = i * tm + lax.broadcasted_iota(jnp.int32, (tm, 1), 0)
        diff = (x_ref[...] - t_ref[...]) * (rid >= ROW0).astype(F32)
        dx_ref[...] = diff * (1.0 / D_MODEL)

        @pl.when(i == 0)
        def _():
            l_ref[...] = jnp.zeros_like(l_ref)

        l_ref[...] += jnp.sum(diff * diff, axis=0, keepdims=True)

    spec = pl.BlockSpec((tm, D_MODEL), lambda i: (i, 0))
    return pl.pallas_call(
        body, name="loss_head", grid=(T // tm,), in_specs=[spec, spec],
        out_specs=[spec, pl.BlockSpec((1, D_MODEL), lambda i: (0, 0))],
        out_shape=[jax.ShapeDtypeStruct((T, D_MODEL), F32), jax.ShapeDtypeStruct((1, D_MODEL), F32)],
        compiler_params=_cparams(("arbitrary",)),
    )(x2, tgt)


HALO = 32


def _conv_tile(T):
    return _pick(T, (384, 128))


def _conv_fwd(h, w, b, T, name):
    tr = _conv_tile(T)
    ratio = tr // HALO
    wp = jnp.zeros((HALO, CONV_DIM), F32).at[:CONV_K].set(w)

    def body(m_ref, h_ref, w_ref, b_ref, o_ref, win):
        i = pl.program_id(0)
        win[0:HALO, :] = h_ref[...] * (i > 0).astype(F32)
        win[HALO:, :] = m_ref[...]
        acc = jnp.broadcast_to(b_ref[...], (tr, CONV_DIM))
        for k in range(CONV_K):
            acc = acc + w_ref[k:k + 1, :] * win[pl.ds(HALO - (CONV_K - 1) + k, tr), :]
        o_ref[...] = acc

    return pl.pallas_call(
        body, name=name, grid=(T // tr,),
        in_specs=[pl.BlockSpec((tr, CONV_DIM), lambda i: (i, 0)),
                  pl.BlockSpec((HALO, CONV_DIM), lambda i: (jnp.maximum(i * ratio - 1, 0), 0)),
                  pl.BlockSpec((HALO, CONV_DIM), lambda i: (0, 0)),
                  pl.BlockSpec((1, CONV_DIM), lambda i: (0, 0))],
        out_specs=pl.BlockSpec((tr, CONV_DIM), lambda i: (i, 0)),
        out_shape=jax.ShapeDtypeStruct((T, CONV_DIM), F32),
        scratch_shapes=[pltpu.VMEM((tr + HALO, CONV_DIM), F32)],
        compiler_params=_cparams(("parallel",)),
    )(h, h, wp, _param2d(b))


def _conv_bwd(h, w, dy, T, name):
    tr = _conv_tile(T)
    ratio = tr // HALO
    n_t = T // tr
    last_halo = T // HALO - 1
    wp = jnp.zeros((HALO, CONV_DIM), F32).at[:CONV_K].set(w)

    def body(hm_ref, hh_ref, dm_ref, dh_ref, w_ref, dx_ref, dw_ref, db_ref, hwin, dwin):
        i = pl.program_id(0)
        hwin[0:HALO, :] = hh_ref[...] * (i > 0).astype(F32)
        hwin[HALO:, :] = hm_ref[...]
        dwin[0:tr, :] = dm_ref[...]
        dwin[tr:, :] = dh_ref[...] * (i < n_t - 1).astype(F32)

        @pl.when(i == 0)
        def _():
            dw_ref[...] = jnp.zeros_like(dw_ref)
            db_ref[...] = jnp.zeros_like(db_ref)

        dy_m = dm_ref[...]
        db_ref[...] += jnp.sum(dy_m, axis=0, keepdims=True)
        acc = jnp.zeros((tr, CONV_DIM), F32)
        for k in range(CONV_K):
            acc = acc + w_ref[k:k + 1, :] * dwin[pl.ds(CONV_K - 1 - k, tr), :]
            dw_ref[k:k + 1, :] += jnp.sum(dy_m * hwin[pl.ds(HALO - (CONV_K - 1) + k, tr), :], axis=0, keepdims=True)
        dx_ref[...] = acc

    main = pl.BlockSpec((tr, CONV_DIM), lambda i: (i, 0))
    return pl.pallas_call(
        body, name=name, grid=(n_t,),
        in_specs=[main,
                  pl.BlockSpec((HALO, CONV_DIM), lambda i: (jnp.maximum(i * ratio - 1, 0), 0)),
                  main,
                  pl.BlockSpec((HALO, CONV_DIM), lambda i: (jnp.minimum((i + 1) * ratio, last_halo), 0)),
                  pl.BlockSpec((HALO, CONV_DIM), lambda i: (0, 0))],
        out_specs=[main, pl.BlockSpec((HALO, CONV_DIM), lambda i: (0, 0)), pl.BlockSpec((1, CONV_DIM), lambda i: (0, 0))],
        out_shape=[jax.ShapeDtypeStruct((T, CONV_DIM), F32), jax.ShapeDtypeStruct((HALO, CONV_DIM), F32),
                   jax.ShapeDtypeStruct((1, CONV_DIM), F32)],
        scratch_shapes=[pltpu.VMEM((tr + HALO, CONV_DIM), F32), pltpu.VMEM((tr + HALO, CONV_DIM), F32)],
        compiler_params=_cparams(("arbitrary",)),
    )(h, h, dy, dy, wp)


BLK = 128
NEG = -1e30
ATT_SCALE = QK_DIM ** -0.5
_NT = (((1,), (1,)), ((), ()))
_TN = (((0,), (0,)), ((), ()))


def _att_blk(T):
    return _pick(T, (384, 128))


def _att_mask(i, j, blk):
    qpos = i * blk + lax.broadcasted_iota(jnp.int32, (blk, blk), 0)
    kpos = j * blk + lax.broadcasted_iota(jnp.int32, (blk, blk), 1)
    return (kpos <= qpos) & (kpos >= PAD_FRONT)


def _attn_fwd(q, k, v, T, name):
    blk = _att_blk(T)
    nq = T // blk

    def body(q_ref, k_ref, v_ref, o_ref, lse_ref):
        i = pl.program_id(1)
        qb = q_ref[...].astype(BF16)

        def step(j, carry, masked):
            m, l, acc = carry
            r0 = pl.multiple_of(j * blk, blk)
            kb = k_ref[pl.ds(r0, blk), :].astype(BF16)
            vb = v_ref[pl.ds(r0, blk), :].astype(BF16)
            s = lax.dot_general(qb, kb, _NT, preferred_element_type=F32)
            if masked:
                s = jnp.where(_att_mask(i, j, blk), s, NEG)
            m_new = jnp.maximum(m, jnp.max(s, axis=-1, keepdims=True))
            p = jnp.exp(s - m_new)
            alpha = jnp.exp(m - m_new)
            l = alpha * l + jnp.sum(p, axis=-1, keepdims=True)
            acc = alpha * acc + jnp.dot(p.astype(BF16), vb, preferred_element_type=F32)
            return m_new, l, acc

        init = (jnp.full((blk, 1), NEG, F32), jnp.zeros((blk, 1), F32), jnp.zeros((blk, HEAD_W), F32))
        carry = step(i, init, True)
        carry = lax.fori_loop(0, jnp.minimum(i, 1), lambda j, c: step(j, c, True), carry)
        m, l, acc = lax.fori_loop(1, i, lambda j, c: step(j, c, False), carry)
        o_ref[...] = (acc / l).astype(o_ref.dtype)
        lse_ref[0] = m + jnp.log(l)

    full = pl.BlockSpec((T, HEAD_W), lambda h, i: (0, h))
    return pl.pallas_call(
        body, name=name, grid=(HEADS, nq),
        in_specs=[pl.BlockSpec((blk, HEAD_W), lambda h, i: (i, h)), full, full],
        out_specs=[pl.BlockSpec((blk, HEAD_W), lambda h, i: (i, h)),
                   pl.BlockSpec((1, blk, 1), lambda h, i: (h, i, 0))],
        out_shape=[jax.ShapeDtypeStruct((T, HEADS * HEAD_W), BF16), jax.ShapeDtypeStruct((HEADS, T, 1), F32)],
        compiler_params=_cparams(("parallel", "arbitrary")),
    )(q, k, v)


def _attn_bwd(q, k, v, o, lse, do, T, name):
    blk = _att_blk(T)
    nq = T // blk

    def body(q_ref, k_ref, v_ref, o_ref, lse_ref, do_ref, dq_ref, dk_ref, dv_ref, delta, dk_acc, dv_acc):
        j = pl.program_id(1)

        @pl.when(j == 0)
        def _():
            dq_ref[...] = jnp.zeros_like(dq_ref)

            def dstep(i, c):
                r0 = pl.multiple_of(i * blk, blk)
                delta[pl.ds(r0, blk), :] = jnp.sum(
                    do_ref[pl.ds(r0, blk), :].astype(F32) * o_ref[pl.ds(r0, blk), :].astype(F32), axis=-1, keepdims=True)
                return c

            lax.fori_loop(0, nq, dstep, 0)

        kb = k_ref[...].astype(BF16)
        vb = v_ref[...].astype(BF16)
        dk_acc[...] = jnp.zeros_like(dk_acc)
        dv_acc[...] = jnp.zeros_like(dv_acc)

        def step(i, masked):
            r0 = pl.multiple_of(i * blk, blk)
            qb = q_ref[pl.ds(r0, blk), :].astype(BF16)
            dob = do_ref[pl.ds(r0, blk), :].astype(BF16)
            s = lax.dot_general(qb, kb, _NT, preferred_element_type=F32)
            p = jnp.exp(s - lse_ref[0, pl.ds(r0, blk), :])
            if masked:
                p = jnp.where(_att_mask(i, j, blk), p, 0.0)
            dv_acc[...] += lax.dot_general(p.astype(BF16), dob, _TN, preferred_element_type=F32)
            dp = lax.dot_general(dob, vb, _NT, preferred_element_type=F32)
            ds = (p * (dp - delta[pl.ds(r0, blk), :])).astype(BF16)
            dk_acc[...] += lax.dot_general(ds, qb, _TN, preferred_element_type=F32)
            dq_ref[pl.ds(r0, blk), :] += jnp.dot(ds, kb, preferred_element_type=F32)

        def loop(lo, masked):
            def it(i, c):
                step(i, masked)
                return c
            lax.fori_loop(lo, nq, it, 0)

        @pl.when(j == 0)
        def _():
            loop(0, True)

        @pl.when(j > 0)
        def _():
            step(j, True)
            loop(j + 1, False)

        dk_ref[...] = dk_acc[...].astype(dk_ref.dtype)
        dv_ref[...] = dv_acc[...].astype(dv_ref.dtype)

    full = pl.BlockSpec((T, HEAD_W), lambda h, j: (0, h))
    kblk = pl.BlockSpec((blk, HEAD_W), lambda h, j: (j, h))
    wide = (T, HEADS * HEAD_W)
    return pl.pallas_call(
        body, name=name, grid=(HEADS, nq),
        in_specs=[full, kblk, kblk, full, pl.BlockSpec((1, T, 1), lambda h, j: (h, 0, 0)), full],
        out_specs=[full, kblk, kblk],
        out_shape=[jax.ShapeDtypeStruct(wide, F32), jax.ShapeDtypeStruct(wide, BF16), jax.ShapeDtypeStruct(wide, BF16)],
        scratch_shapes=[pltpu.VMEM((T, 1), F32), pltpu.VMEM((blk, HEAD_W), F32), pltpu.VMEM((blk, HEAD_W), F32)],
        compiler_params=_cparams(("parallel", "arbitrary")),
    )(q, k, v, o, lse, do)


HG_NB = 6
C = HG_CHUNK
_HI = lax.Precision.HIGHEST


def _tri(lower):
    r = lax.broadcasted_iota(jnp.int32, (C, C), 0)
    c = lax.broadcasted_iota(jnp.int32, (C, C), 1)
    return ((c <= r) if lower else (c >= r)).astype(F32)


def _hg_intra_fwd(q, k, v, b):
    o = jnp.zeros((C, HG_DV), F32)
    rows = lax.broadcasted_iota(jnp.int32, (C, 1), 0)
    for s in range(C):
        lo = (s // 8) * 8
        e = jnp.exp(jnp.minimum(b[lo:] - b[s:s + 1], 0.0))
        a = jnp.sum(q[lo:] * k[s:s + 1] * e, axis=-1, keepdims=True)
        a = jnp.where(rows[lo:] >= s, a, 0.0)
        upd = a * v[s:s + 1]
        o = o + (upd if lo == 0 else jnp.concatenate([jnp.zeros((lo, HG_DV), F32), upd], axis=0))
    return o


def _hgrn_fwd(u, kk, lf, vv, T, name):
    nb = _pick(T // C, (HG_NB, 3, 2, 1))
    rows = nb * C
    qblk = C_HQ // HG_DK

    def body(q_ref, k_ref, lf_ref, v_ref, o_ref, st_ref, st):
        @pl.when(pl.program_id(1) == 0)
        def _():
            st[...] = jnp.zeros_like(st)

        lower = _tri(True)
        for n in range(nb):
            sl = slice(n * C, (n + 1) * C)
            q, k, v = q_ref[sl, :].astype(F32), k_ref[sl, :], v_ref[sl, :]
            b = jnp.dot(lower, lf_ref[sl, :], precision=_HI, preferred_element_type=F32)
            s_t = st[...]
            st_ref[0, n] = s_t
            qe = (q * jnp.exp(b)).astype(BF16)
            o = lax.dot_general(qe, s_t.astype(BF16), _NT, preferred_element_type=F32)
            o_ref[sl, :] = o + _hg_intra_fwd(q, k, v, b)
            bl = b[C - 1:C, :]
            kd = (k * jnp.exp(bl - b)).astype(BF16)
            st[...] = s_t * jnp.exp(bl) + lax.dot_general(v.astype(BF16), kd, _TN, preferred_element_type=F32)

    col = lambda off: pl.BlockSpec((rows, HG_DK), lambda h, c: (c, h + off))
    return pl.pallas_call(
        body, name=name, grid=(HG_HEADS, T // rows),
        in_specs=[col(qblk), col(0), col(0), col(0)],
        out_specs=[col(0), pl.BlockSpec((1, nb, HG_DV, HG_DK), lambda h, c: (h, c, 0, 0))],
        out_shape=[jax.ShapeDtypeStruct((T, HG_HEADS * HG_DV), F32),
                   jax.ShapeDtypeStruct((HG_HEADS, T // C, HG_DV, HG_DK), F32)],
        scratch_shapes=[pltpu.VMEM((HG_DV, HG_DK), F32)],
        compiler_params=_cparams(("parallel", "arbitrary")),
    )(u, kk, lf, vv)


def _hgrn_bwd(u, kk, lf, vv, states, do, T, name):
    nb = _pick(T // C, (HG_NB, 3, 2, 1))
    rows = nb * C
    n_steps = T // rows
    qblk = C_HQ // HG_DK

    def body(q_ref, k_ref, lf_ref, v_ref, st_ref, do_ref, dq_ref, dk_ref, dlf_ref, dv_ref, dst, dk_s, dv_s):
        @pl.when(pl.program_id(1) == 0)
        def _():
            dst[...] = jnp.zeros_like(dst)

        lower, upper = _tri(True), _tri(False)
        rid = lax.broadcasted_iota(jnp.int32, (C, 1), 0)
        for n in reversed(range(nb)):
            sl = slice(n * C, (n + 1) * C)
            q, k, v, do = q_ref[sl, :].astype(F32), k_ref[sl, :], v_ref[sl, :], do_ref[sl, :]
            b = jnp.dot(lower, lf_ref[sl, :], precision=_HI, preferred_element_type=F32)
            s_t = st_ref[0, n]
            d_new = dst[...]
            eb = jnp.exp(b)
            bl = b[C - 1:C, :]
            ebl = jnp.exp(bl)
            dec = jnp.exp(bl - b)
            qe = q * eb
            kd = k * dec
            do_b = do.astype(BF16)
            dqe = jnp.dot(do_b, s_t.astype(BF16), preferred_element_type=F32)
            dkd = jnp.dot(v.astype(BF16), d_new.astype(BF16), preferred_element_type=F32)
            dv = lax.dot_general(kd.astype(BF16), d_new.astype(BF16), _NT, preferred_element_type=F32)
            dbl = ebl * jnp.sum(d_new * s_t, axis=0, keepdims=True) + jnp.sum(dkd * kd, axis=0, keepdims=True)
            dst[...] = d_new * ebl + lax.dot_general(do_b, qe.astype(BF16), _TN, preferred_element_type=F32)
            dq = dqe * eb
            dk = dkd * dec
            for s in range(C):
                lo = (s // 8) * 8
                e = jnp.exp(jnp.minimum(b[lo:] - b[s:s + 1], 0.0))
                e = jnp.where(rid[lo:] >= s, e, 0.0)
                a = jnp.sum(q[lo:] * k[s:s + 1] * e, axis=-1, keepdims=True)
                da = jnp.sum(do[lo:] * v[s:s + 1], axis=-1, keepdims=True)
                g = da * e
                upd = g * k[s:s + 1]
                dq = dq + (upd if lo == 0 else jnp.concatenate([jnp.zeros((lo, HG_DK), F32), upd], axis=0))
                dk_s[s:s + 1, :] = jnp.sum(g * q[lo:], axis=0, keepdims=True)
                dv_s[s:s + 1, :] = jnp.sum(a * do[lo:], axis=0, keepdims=True)
            dk = dk + dk_s[...]
            dv = dv + dv_s[...]
            db = q * dq - k * dk
            db = db + jnp.where(rid == C - 1, dbl, 0.0)
            dq_ref[sl, :] = dq
            dk_ref[sl, :] = dk
            dv_ref[sl, :] = dv
            dlf_ref[sl, :] = jnp.dot(upper, db, precision=_HI, preferred_element_type=F32)

    rev = lambda off: pl.BlockSpec((rows, HG_DK), lambda h, c: (n_steps - 1 - c, h + off))
    return pl.pallas_call(
        body, name=name, grid=(HG_HEADS, n_steps),
        in_specs=[rev(qblk), rev(0), rev(0), rev(0),
                  pl.BlockSpec((1, nb, HG_DV, HG_DK), lambda h, c: (h, n_steps - 1 - c, 0, 0)), rev(0)],
        out_specs=[rev(0)] * 4,
        out_shape=[jax.ShapeDtypeStruct((T, HG_HEADS * HG_DK), F32)] * 4,
        scratch_shapes=[pltpu.VMEM((HG_DV, HG_DK), F32), pltpu.VMEM((C, HG_DK), F32), pltpu.VMEM((C, HG_DV), F32)],
        compiler_params=_cparams(("parallel", "arbitrary")),
    )(u, kk, lf, vv, states, do)


def _rope_tables(T):
    half = ROPE // 2
    pos = (np.arange(T, dtype=np.float32) - PAD_FRONT).astype(np.float32)
    inv_freq = (ROPE_BASE ** (-np.arange(half, dtype=np.float32) / half)).astype(np.float32)
    ang = pos[:, None] * inv_freq[None, :]
    cos, sin = jnp.cos(jnp.asarray(ang)), jnp.sin(jnp.asarray(ang))
    z = jnp.zeros((T, HEAD_W), F32)
    c = z.at[:, :NOPE].set(1.0).at[:, NOPE:NOPE + half].set(cos).at[:, NOPE + half:NOPE + ROPE].set(cos)
    s1 = z.at[:, NOPE:NOPE + half].set(-sin)
    s2 = z.at[:, NOPE + half:NOPE + ROPE].set(sin)
    return c, s1, s2


def _layer_fwd(x, w, tabs, T, l):
    c, s1, s2 = tabs
    n = lambda s: f"l{l}_{s}"
    sv = {"x": x}
    h = _rowwise(_f_rms, T, [Row(x)], [(w["norm1_g"], D_MODEL)], [(D_MODEL, BF16)], n("norm1"))[0]
    u = _mm(h, w["w_in"], out_dtype=BF16, name=n("in_proj"))
    sv.update(h=h, u=u)
    hglu = _rowwise(_f_glu, T, [Row(u, 512, C_CONV_A), Row(u, 512, C_CONV_G)], [], [(CONV_DIM, F32)], n("glu"))[0]
    cv = _conv_fwd(hglu, w["conv_w"], w["conv_b"], T, n("conv"))
    hc = _rowwise(_f_lnsilu, T, [Row(cv)], [(w["conv_ln_g"], CONV_DIM), (w["conv_ln_b"], CONV_DIM)],
                  [(CONV_DIM, BF16)], n("conv_ln"))[0]
    y_a = _mm(hc, w["w_conv_out"], out_dtype=BF16, name=n("conv_out"))
    sv.update(hglu=hglu, cv=cv, hc=hc, y_a=y_a)
    cqn = _rowwise(_f_rms, T, [Row(u, Q_RANK, C_CQ)], [(w["q_a_norm_g"], Q_RANK)], [(Q_RANK, BF16)], n("q_a_norm"))[0]
    ckvn = _rowwise(_f_rms, T, [Row(u, KV_RANK, C_CKV)], [(w["kv_a_norm_g"], KV_RANK)], [(KV_RANK, BF16)], n("kv_a_norm"))[0]
    q_raw = _mm(cqn, w["w_uq"], out_dtype=BF16, name=n("uq"))
    k_raw = _mm(ckvn, w["w_uk"], out_dtype=BF16, name=n("uk"))
    v = _mm(ckvn, w["w_uv"], out_dtype=BF16, name=n("uv"))
    tab_rows = [Row(c), Row(s1), Row(s2)]
    q = _rowwise(_f_qrope, T, [Row(q_raw, piece=HEAD_W)] + tab_rows, [(w["q_norm_g"], HEAD_W)],
                 [(HEADS * HEAD_W, BF16)], n("q_rope"))[0]
    k = _rowwise(_f_krope, T, [Row(k_raw, piece=HEAD_W), Row(u, HEAD_W, C_KR)] + tab_rows, [(w["k_norm_g"], HEAD_W)],
                 [(HEADS * HEAD_W, BF16)], n("k_rope"))[0]
    o, lse = _attn_fwd(q, k, v, T, n("attn"))
    y_b = _mm(o, w["w_attn_out"], out_dtype=BF16, name=n("attn_out"))
    sv.update(cqn=cqn, ckvn=ckvn, q_raw=q_raw, k_raw=k_raw, v=v, q=q, k=k, o=o, lse=lse, y_b=y_b)
    kk, lf, vv = _rowwise(_f_hgrn_prep, T, [Row(u, 512, C_HF), Row(u, 512, C_HI)], [(w["lb"], 512)],
                          [(512, F32)] * 3, n("hgrn_prep"))
    o_h, states = _hgrn_fwd(u, kk, lf, vv, T, n("hgrn"))
    oh = _rowwise(_f_hgrn_out, T, [Row(o_h, piece=HG_DV), Row(u, 512, C_HG, piece=HG_DV)], [(w["hgrn_norm_g"], HG_DV)],
                  [(512, BF16)], n("hgrn_out_norm"))[0]
    y_c = _mm(oh, w["w_hgrn_out"], out_dtype=BF16, name=n("hgrn_out"))
    sv.update(kk=kk, lf=lf, vv=vv, o_h=o_h, states=states, oh=oh, y_c=y_c)
    gate_rows = [Row(u, D_MODEL, C_GATE + g * D_MODEL) for g in range(3)]
    mix = _rowwise(_f_mix, T, gate_rows + [Row(y_a), Row(y_b), Row(y_c)], [], [(D_MODEL, BF16)], n("mix"))[0]
    x1 = _mm(mix, w["w_out"], res=x, name=n("out_proj"))
    h2 = _rowwise(_f_rms, T, [Row(x1)], [(w["norm2_g"], D_MODEL)], [(D_MODEL, BF16)], n("norm2"))[0]
    f = _mm(h2, w["w_ff1"], out_dtype=BF16, name=n("ff1"))
    x2 = _mm(f, w["w_ff2"], res=x1, a_fn=_relu2, name=n("ff2"))
    sv.update(mix=mix, x1=x1, h2=h2, f=f)
    return x2, sv


def _layer_bwd(dx2, w, sv, tabs, T, l):
    c, s1, s2 = tabs
    n = lambda s: f"l{l}_b_{s}"
    u = sv["u"]
    g = {}
    g["w_ff2"] = _mm(sv["f"], dx2, ta=True, a_fn=_relu2, name=n("dw_ff2"))
    df = _mm(dx2, w["w_ff2"], tb=True, out_dtype=BF16, name=n("d_f"),
             epi=(sv["f"], lambda d, fv: d * (2.0 * jnp.maximum(fv, 0.0))))
    g["w_ff1"] = _mm(sv["h2"], df, ta=True, name=n("dw_ff1"))
    dh2 = _mm(df, w["w_ff1"], tb=True, name=n("d_h2"))
    (dx1,), (g["norm2_g"],) = _rowwise_bwd(_f_rms, T, [Row(sv["x1"])], [(w["norm2_g"], D_MODEL)], [Row(dh2)],
                                           {0: F32}, n("norm2"), add=(0, dx2))
    g["w_out"] = _mm(sv["mix"], dx1, ta=True, name=n("dw_out"))
    dmix = _mm(dx1, w["w_out"], tb=True, out_dtype=BF16, name=n("d_mix"))
    gate_rows = [Row(u, D_MODEL, C_GATE + i * D_MODEL) for i in range(3)]
    (dg0, dg1, dg2, dy_a, dy_b, dy_c), _ = _rowwise_bwd(
        _f_mix, T, gate_rows + [Row(sv["y_a"]), Row(sv["y_b"]), Row(sv["y_c"])], [], [Row(dmix)],
        {0: BF16, 1: BF16, 2: BF16, 3: BF16, 4: BF16, 5: BF16}, n("mix"))
    g["w_hgrn_out"] = _mm(sv["oh"], dy_c, ta=True, name=n("dw_hgrn_out"))
    doh = _mm(dy_c, w["w_hgrn_out"], tb=True, out_dtype=BF16, name=n("d_oh"))
    (do_h, dhg), (g["hgrn_norm_g"],) = _rowwise_bwd(
        _f_hgrn_out, T, [Row(sv["o_h"], piece=HG_DV), Row(u, 512, C_HG, piece=HG_DV)], [(w["hgrn_norm_g"], HG_DV)],
        [Row(doh, piece=HG_DV)], {0: F32, 1: BF16}, n("hgrn_out_norm"))
    dhq, dkk, dlf, dvv = _hgrn_bwd(u, sv["kk"], sv["lf"], sv["vv"], sv["states"], do_h, T, n("hgrn"))
    (dhf, dhi), (g["lb"],) = _rowwise_bwd(
        _f_hgrn_prep, T, [Row(u, 512, C_HF), Row(u, 512, C_HI)], [(w["lb"], 512)],
        [Row(dkk), Row(dlf), Row(dvv)], {0: BF16, 1: BF16}, n("hgrn_prep"))
    g["w_attn_out"] = _mm(sv["o"], dy_b, ta=True, name=n("dw_attn_out"))
    do = _mm(dy_b, w["w_attn_out"], tb=True, out_dtype=BF16, name=n("d_o"))
    dq, dk, dv = _attn_bwd(sv["q"], sv["k"], sv["v"], sv["o"], sv["lse"], do, T, n("attn"))
    tab_rows = [Row(c), Row(s1), Row(s2)]
    (dq_raw,), (g["q_norm_g"],) = _rowwise_bwd(
        _f_qrope, T, [Row(sv["q_raw"], piece=HEAD_W)] + tab_rows, [(w["q_norm_g"], HEAD_W)],
        [Row(dq, piece=HEAD_W)], {0: BF16}, n("q_rope"))
    (dk_raw, dkr), (g["k_norm_g"],) = _rowwise_bwd(
        _f_krope, T, [Row(sv["k_raw"], piece=HEAD_W), Row(u, HEAD_W, C_KR)] + tab_rows, [(w["k_norm_g"], HEAD_W)],
        [Row(dk, piece=HEAD_W)], {0: BF16, 1: BF16}, n("k_rope"))
    g["w_uq"] = _mm(sv["cqn"], dq_raw, ta=True, name=n("dw_uq"))
    g["w_uk"] = _mm(sv["ckvn"], dk_raw, ta=True, name=n("dw_uk"))
    g["w_uv"] = _mm(sv["ckvn"], dv, ta=True, name=n("dw_uv"))
    dcqn = _mm(dq_raw, w["w_uq"], tb=True, out_dtype=BF16, name=n("d_cqn"))
    dckvn = _mm(dk_raw, w["w_uk"], tb=True, name=n("d_ckvn_k"))
    dckvn = _mm(dv, w["w_uv"], tb=True, res=dckvn, out_dtype=BF16, name=n("d_ckvn_v"))
    (dcq,), (g["q_a_norm_g"],) = _rowwise_bwd(_f_rms, T, [Row(u, Q_RANK, C_CQ)], [(w["q_a_norm_g"], Q_RANK)],
                                              [Row(dcqn)], {0: BF16}, n("q_a_norm"))
    (dckv,), (g["kv_a_norm_g"],) = _rowwise_bwd(_f_rms, T, [Row(u, KV_RANK, C_CKV)], [(w["kv_a_norm_g"], KV_RANK)],
                                                [Row(dckvn)], {0: BF16}, n("kv_a_norm"))
    g["w_conv_out"] = _mm(sv["hc"], dy_a, ta=True, name=n("dw_conv_out"))
    dhc = _mm(dy_a, w["w_conv_out"], tb=True, out_dtype=BF16, name=n("d_hc"))
    (dcv,), (g["conv_ln_g"], g["conv_ln_b"]) = _rowwise_bwd(
        _f_lnsilu, T, [Row(sv["cv"])], [(w["conv_ln_g"], CONV_DIM), (w["conv_ln_b"], CONV_DIM)], [Row(dhc)],
        {0: F32}, n("conv_ln"))
    dhglu, dconv_w, g["conv_b"] = _conv_bwd(sv["hglu"], w["conv_w"], dcv, T, n("conv"))
    g["conv_w"] = dconv_w[:CONV_K]
    (dua, dug), _ = _rowwise_bwd(_f_glu, T, [Row(u, 512, C_CONV_A), Row(u, 512, C_CONV_G)], [], [Row(dhglu)],
                                 {0: BF16, 1: BF16}, n("glu"))
    du = jnp.concatenate([dua, dug, dg0, dg1, dg2, dcq, dckv, dkr, dhq.astype(BF16), dhf, dhi, dhg], axis=1)
    g["w_in"] = _mm(sv["h"], du, ta=True, name=n("dw_in"))
    dh = _mm(du, w["w_in"], tb=True, name=n("d_h"))
    (dx,), (g["norm1_g"],) = _rowwise_bwd(_f_rms, T, [Row(sv["x"])], [(w["norm1_g"], D_MODEL)], [Row(dh)],
                                          {0: F32}, n("norm1"), add=(0, dx1))
    return dx, g


def _pad_w_in(w_in):
    z = lambda k: jnp.zeros((w_in.shape[0], k), w_in.dtype)
    return jnp.concatenate([w_in[:, :O_CQ], w_in[:, O_GATE:], w_in[:, O_CQ:O_KR], z(KR_LANE), w_in[:, O_KR:O_HQ],
                            z(HEAD_W - KR_LANE - ROPE), w_in[:, O_HQ:O_GATE]], axis=1)


def _unpad_w_in(g):
    return jnp.concatenate([g[:, :C_GATE], g[:, C_CQ:C_KR], g[:, C_KR + KR_LANE:C_KR + KR_LANE + ROPE],
                            g[:, C_HQ:], g[:, C_GATE:C_CQ]], axis=1)


def _pad_heads(wm, per_head, lo, hi):
    lead = wm.shape[:-1]
    wh = wm.reshape(lead + (HEADS, per_head))[..., lo:hi]
    pad = [(0, 0)] * len(lead) + [(0, 0), (0, HEAD_W - (hi - lo))]
    return jnp.pad(wh, pad).reshape(lead + (HEADS * HEAD_W,))


def _unpad_heads(gm, width):
    lead = gm.shape[:-1]
    return gm.reshape(lead + (HEADS, HEAD_W))[..., :width]


def _layer_weights(full, lb):
    w = {}
    w["norm1_g"] = full["norm1_g"]
    w["w_in"] = _pad_w_in(full["w_in"])
    w["conv_w"] = full["conv_w"]
    w["conv_b"] = full["conv_b"]
    w["conv_ln_g"] = full["conv_ln_g"]
    w["conv_ln_b"] = full["conv_ln_b"]
    w["w_conv_out"] = full["w_conv_out"]
    w["q_a_norm_g"] = full["q_a_norm_g"]
    w["w_uq"] = _pad_heads(full["w_uq"], QK_DIM, 0, QK_DIM)
    w["kv_a_norm_g"] = full["kv_a_norm_g"]
    w["w_uk"] = _pad_heads(full["w_ukv"], NOPE + V_DIM, 0, NOPE)
    w["w_uv"] = _pad_heads(full["w_ukv"], NOPE + V_DIM, NOPE, NOPE + V_DIM)
    w["q_norm_g"] = jnp.pad(full["q_norm_g"], (0, HEAD_W - QK_DIM))
    w["k_norm_g"] = jnp.pad(full["k_norm_g"], (0, HEAD_W - QK_DIM))
    wa = full["w_attn_out"].reshape(HEADS, V_DIM, D_MODEL)
    w["w_attn_out"] = jnp.pad(wa, ((0, 0), (0, HEAD_W - V_DIM), (0, 0))).reshape(HEADS * HEAD_W, D_MODEL)
    w["lb"] = lb
    w["hgrn_norm_g"] = full["hgrn_norm_g"]
    w["w_hgrn_out"] = full["w_hgrn_out"]
    w["w_out"] = full["w_out"]
    w["norm2_g"] = full["norm2_g"]
    w["w_ff1"] = full["w_ff1"]
    w["w_ff2"] = full["w_ff2"]
    return w


def _layer_grads_to_original(g):
    o = {}
    for name in ("w_conv_out", "w_hgrn_out", "w_out", "w_ff1", "w_ff2", "conv_w"):
        o[name] = g[name]
    for name in ("norm1_g", "conv_b", "conv_ln_g", "conv_ln_b", "q_a_norm_g", "kv_a_norm_g", "hgrn_norm_g", "norm2_g", "lb"):
        o[name] = g[name].reshape(-1)
    o["w_in"] = _unpad_w_in(g["w_in"])
    o["w_uq"] = _unpad_heads(g["w_uq"], QK_DIM).reshape(Q_RANK, HEADS * QK_DIM)
    guk = _unpad_heads(g["w_uk"], NOPE)
    guv = _unpad_heads(g["w_uv"], V_DIM)
    o["w_ukv"] = jnp.concatenate([guk, guv], axis=-1).reshape(KV_RANK, HEADS * (NOPE + V_DIM))
    o["q_norm_g"] = g["q_norm_g"].reshape(-1)[:QK_DIM]
    o["k_norm_g"] = g["k_norm_g"].reshape(-1)[:QK_DIM]
    o["w_attn_out"] = g["w_attn_out"].reshape(HEADS, HEAD_W, D_MODEL)[:, :V_DIM].reshape(HEADS * V_DIM, D_MODEL)
    return o


def _lower_bounds(logits):
    p = jax.nn.softmax(logits.astype(F32), axis=0)
    return jnp.cumsum(p, axis=0) - p[0:1]


def _run_step(x, target, meta, lb_logits, layer_weights, layer_done):
    seq = x.shape[0]
    T = ROW0 + seq
    assert T % 128 == 0
    tabs = _rope_tables(T)
    lbs, lb_vjp = jax.vjp(_lower_bounds, lb_logits)
    xp = jnp.concatenate([jnp.zeros((PAD_FRONT, D_MODEL), F32), meta.astype(F32), x], axis=0)
    tp = jnp.concatenate([jnp.zeros((ROW0, D_MODEL), F32), target], axis=0)
    ws, svs = [], []
    for l in range(DEPTH):
        full, xp = layer_weights(l, xp)
        w = _layer_weights(full, lbs[l])
        xp, sv = _layer_fwd(xp, w, tabs, T, l)
        ws.append(w)
        svs.append(sv)
    dx, sq = _loss_head(xp, tp, T)
    loss = 0.5 * jnp.sum(sq) * (1.0 / D_MODEL)
    dlb = [None] * DEPTH
    for l in reversed(range(DEPTH)):
        dx, g = _layer_bwd(dx, ws[l], svs[l], tabs, T, l)
        g = _layer_grads_to_original(g)
        dlb[l] = g.pop("lb")
        dx = layer_done(l, g, dx)
    return loss, dx[ROW0:], dx[PAD_FRONT:ROW0], lb_vjp(jnp.stack(dlb))[0]


def _local_step(x, target, full):
    per_layer = [None] * DEPTH

    def done(l, g, dx):
        per_layer[l] = g
        return dx

    loss, gx, gmeta, glb = _run_step(
        x, target, full["meta"], full["hgrn_lb_logits"],
        lambda l, xp: ({k: v[l] for k, v in full.items() if k != "meta"}, xp), done)
    grads = {k: jnp.stack([per_layer[l][k] for l in range(DEPTH)]) for k in per_layer[0]}
    grads["hgrn_lb_logits"] = glb
    grads["meta"] = gmeta
    return loss, gx, grads


def _mesh_pos():
    return lax.axis_index("x"), lax.axis_index("y"), lax.axis_index("c")


N_COPY = N_DEV - 1


def _all_gather(arrs, name):
    n = len(arrs)

    def body(*refs):
        x_refs, out_refs = refs[:n], refs[n:2 * n]
        send_sems, recv_sems, local_sems = refs[2 * n:]
        x, y, c = _mesh_pos()
        me, sibling = (x, y, c), (x, y, 1 - c)
        chips = [(1 - x, y), (x, 1 - y), (1 - x, 1 - y)]

        def slot(a, px, py, pc):
            return out_refs[a].at[4 * px + 2 * py + pc]

        def copy(a, k, block, to, own=False):
            return pltpu.make_async_remote_copy(
                src_ref=x_refs[a] if own else slot(a, *block), dst_ref=slot(a, *block),
                send_sem=send_sems.at[a * N_COPY + k], recv_sem=recv_sems.at[a * N_COPY + k],
                device_id=to, device_id_type=MESH)

        mine = [pltpu.make_async_copy(x_refs[a], slot(a, *me), local_sems.at[a]) for a in range(n)]
        for cp in mine:
            cp.start()
        first = []
        for a in range(n):
            first.append(copy(a, 0, me, sibling, own=True))
            first += [copy(a, 1 + j, me, (*chip, c), own=True) for j, chip in enumerate(chips)]
        for cp in first:
            cp.start()
        passed = []
        for j, chip in enumerate(chips):
            for a in range(n):
                copy(a, 1 + j, (*chip, c), me).wait_recv()
                cp = copy(a, 4 + j, (*chip, c), sibling)
                cp.start()
                passed.append(cp)
        for a in range(n):
            copy(a, 0, sibling, me).wait_recv()
            for j, chip in enumerate(chips):
                copy(a, 4 + j, (*chip, 1 - c), me).wait_recv()
        for cp in first + passed:
            cp.wait_send()
        for cp in mine:
            cp.wait()

    anyspec = pl.BlockSpec(memory_space=pl.ANY)
    return pl.pallas_call(
        body, name=name, out_shape=[jax.ShapeDtypeStruct((N_DEV,) + a.shape, a.dtype) for a in arrs],
        in_specs=[anyspec] * n, out_specs=[anyspec] * n,
        scratch_shapes=[pltpu.SemaphoreType.DMA((n * N_COPY,)), pltpu.SemaphoreType.DMA((n * N_COPY,)),
                        pltpu.SemaphoreType.DMA((n,))],
    )(*arrs)


def _exchange(arrs, name):
    n = len(arrs)

    def body(*refs):
        s_refs, r_refs = refs[:n], refs[n:2 * n]
        send_sems, recv_sems, local_sems = refs[2 * n:]
        x, y, c = _mesh_pos()
        me = 4 * x + 2 * y + c
        local = [pltpu.make_async_copy(s_refs[a].at[me], r_refs[a].at[me], local_sems.at[a]) for a in range(n)]
        for cp in local:
            cp.start()
        sends, recvs = [], []
        for rel in range(1, N_DEV):
            px = 1 - x if rel & 4 else x
            py = 1 - y if rel & 2 else y
            pc = 1 - c if rel & 1 else c
            p = 4 * px + 2 * py + pc
            for a in range(n):
                k = a * N_COPY + rel - 1
                sends.append(pltpu.make_async_remote_copy(
                    src_ref=s_refs[a].at[p], dst_ref=r_refs[a].at[me], send_sem=send_sems.at[k],
                    recv_sem=recv_sems.at[k], device_id=(px, py, pc), device_id_type=MESH))
                recvs.append(pltpu.make_async_remote_copy(
                    src_ref=s_refs[a].at[me], dst_ref=r_refs[a].at[p], send_sem=send_sems.at[k],
                    recv_sem=recv_sems.at[k], device_id=(px, py, pc), device_id_type=MESH))
        for cp in sends:
            cp.start()
        for cp in recvs:
            cp.wait_recv()
        for cp in sends:
            cp.wait_send()
        for cp in local:
            cp.wait()

    anyspec = pl.BlockSpec(memory_space=pl.ANY)
    return pl.pallas_call(
        body, name=name, out_shape=[jax.ShapeDtypeStruct(a.shape, a.dtype) for a in arrs],
        in_specs=[anyspec] * n, out_specs=[anyspec] * n,
        scratch_shapes=[pltpu.SemaphoreType.DMA((n * N_COPY,)), pltpu.SemaphoreType.DMA((n * N_COPY,)),
                        pltpu.SemaphoreType.DMA((n,))],
    )(*arrs)


_HBM = pl.BlockSpec(memory_space=pltpu.HBM)
_SEM = pl.BlockSpec(memory_space=pltpu.SEMAPHORE)
_EFFECT = pltpu.SideEffectType.DATAFLOW_SIDE_EFFECTING


def _peers(x, y, c):
    out = []
    for rel in range(1, N_DEV):
        px = 1 - x if rel & 4 else x
        py = 1 - y if rel & 2 else y
        pc = 1 - c if rel & 1 else c
        out.append((rel, (px, py, pc), 4 * px + 2 * py + pc))
    return out


def _split_copies(src_refs, land_refs, send_sems, recv_sems, gather):
    x, y, c = _mesh_pos()
    me = 4 * x + 2 * y + c
    out = []
    for a, (src, land) in enumerate(zip(src_refs, land_refs)):
        for rel, peer, p in _peers(x, y, c):
            k = a * N_COPY + rel - 1
            mk = lambda s, d: pltpu.make_async_remote_copy(
                src_ref=s, dst_ref=d, send_sem=send_sems.at[k], recv_sem=recv_sems.at[k],
                device_id=peer, device_id_type=MESH)
            mine = src if gather else src.at[p]
            out.append((mk(mine, land.at[me]), mk(mine, land.at[p])))
    return out


def _copy_start(srcs, gather, name, collective_id):
    n = len(srcs)
    lands = [lax.empty(((N_DEV,) + s.shape) if gather else s.shape, s.dtype) for s in srcs]

    def body(*refs):
        src_refs, land_refs = refs[:n], refs[n:2 * n]
        send_sems, recv_sems = refs[2 * n], refs[2 * n + 1]
        token = refs[-1]
        x, y, c = _mesh_pos()
        barrier = pltpu.get_barrier_semaphore()
        for _, peer, _ in _peers(x, y, c):
            pl.semaphore_signal(barrier, inc=1, device_id=peer, device_id_type=MESH)
        pl.semaphore_wait(barrier, N_COPY)
        for out_copy, _ in _split_copies(src_refs, land_refs, send_sems, recv_sems, gather):
            out_copy.start()
        token[...] = jnp.zeros_like(token)

    hbm = lambda a: pltpu.HBM(a.shape, a.dtype)
    res = pl.pallas_call(
        body, name=name,
        out_shape=(pltpu.SemaphoreType.DMA((n * N_COPY,)), pltpu.SemaphoreType.DMA((n * N_COPY,)),
                   *[hbm(s) for s in srcs], *[hbm(z) for z in lands], jax.ShapeDtypeStruct((8, 128), F32)),
        in_specs=[_HBM] * (2 * n), out_specs=(_SEM, _SEM, *([_HBM] * (2 * n)), pl.BlockSpec(memory_space=pltpu.VMEM)),
        input_output_aliases={i: 2 + i for i in range(2 * n)},
        compiler_params=pltpu.CompilerParams(has_side_effects=_EFFECT, collective_id=collective_id),
    )(*[pltpu.with_memory_space_constraint(s, pltpu.HBM) for s in srcs],
      *[pltpu.with_memory_space_constraint(z, pltpu.HBM) for z in lands])
    return res[0], res[1], list(res[2:2 + n]), list(res[2 + n:2 + 2 * n]), res[-1]


def _copy_wait(send_sems, recv_sems, srcs, lands, after, gather, name):
    n = len(srcs)

    def body(*refs):
        src_refs, land_refs = refs[:n], refs[n:2 * n]
        s_sems, r_sems = refs[2 * n], refs[2 * n + 1]
        for out_copy, in_copy in _split_copies(src_refs, land_refs, s_sems, r_sems, gather):
            out_copy.wait_send()
            in_copy.wait_recv()

    hbm = lambda a: pltpu.HBM(a.shape, a.dtype)
    res = pl.pallas_call(
        body, name=name, out_shape=(*[hbm(s) for s in srcs], *[hbm(z) for z in lands]),
        in_specs=[_HBM] * (2 * n) + [_SEM, _SEM, pl.BlockSpec(memory_space=pl.ANY)], out_specs=tuple([_HBM] * (2 * n)),
        input_output_aliases={i: i for i in range(2 * n)},
        compiler_params=pltpu.CompilerParams(has_side_effects=_EFFECT),
    )(*srcs, *lands, send_sems, recv_sems, after)
    return list(res[:n]), list(res[n:])


def _sum_parts(parts, name):
    P, R, W = parts.shape

    def body(p_ref, o_ref):
        g = p_ref[0].astype(F32)
        for i in range(1, P):
            g = g + p_ref[i].astype(F32)
        o_ref[...] = g

    return pl.pallas_call(body, name=name, out_shape=jax.ShapeDtypeStruct((R, W), F32))(parts)


def _adamw_body(p_ref, w_ref, m_ref, v_ref, g_ref, d_ref, nm_ref, nv_ref):
    g = p_ref[0].astype(F32)
    for i in range(1, p_ref.shape[0]):
        g = g + p_ref[i].astype(F32)
    m_new = ADAM_B1 * m_ref[...] + (1.0 - ADAM_B1) * g
    v_new = ADAM_B2 * v_ref[...] + (1.0 - ADAM_B2) * jnp.square(g)
    m_hat = m_new / (1.0 - ADAM_B1 ** ADAM_STEP)
    v_hat = v_new / (1.0 - ADAM_B2 ** ADAM_STEP)
    g_ref[...] = g
    d_ref[...] = -ADAM_LR * (m_hat / (jnp.sqrt(v_hat) + ADAM_EPS) + ADAM_WD * w_ref[...])
    nm_ref[...] = m_new
    nv_ref[...] = v_new


def _adamw(parts, w, m, v, name):
    P, R, W = parts.shape
    tr = _pick(R, (368, 192, 64, 16, 8))
    spec = pl.BlockSpec((tr, W), lambda i: (i, 0))
    return pl.pallas_call(
        functools.partial(_adamw_body), name=name, grid=(R // tr,),
        in_specs=[pl.BlockSpec((P, tr, W), lambda i: (0, i, 0)), spec, spec, spec], out_specs=[spec] * 4,
        out_shape=[jax.ShapeDtypeStruct((R, W), F32)] * 4,
        compiler_params=_cparams(("parallel",)),
    )(parts, w, m, v)


def _adamw_layers(parts, w, m, v, name):
    P, B, C_ = parts[0].shape
    tb = _pick(B, (256, 128))
    nb = B // tb

    def body(*refs):
        p_refs, rest = refs[:DEPTH], refs[DEPTH:]
        a = pl.program_id(0)
        for l in range(DEPTH):
            @pl.when(a == l)
            def _():
                _adamw_body(p_refs[l], *[r.at[0] for r in rest])

    spec = pl.BlockSpec((1, tb, C_), lambda a, i: (a, i, 0))

    def part_spec(l):
        return pl.BlockSpec((P, tb, C_), lambda a, i: (0, jnp.where(a == l, i, jnp.where(a < l, 0, nb - 1)), 0))

    return pl.pallas_call(
        body, name=name, grid=(DEPTH, nb),
        in_specs=[part_spec(l) for l in range(DEPTH)] + [spec, spec, spec], out_specs=[spec] * 4,
        out_shape=[jax.ShapeDtypeStruct((DEPTH, B, C_), F32)] * 4,
        compiler_params=_cparams(("arbitrary", "arbitrary")),
    )(*parts, w, m, v)


PACK_W = 1024
BIG = (("w_in", (DEPTH, D_MODEL, N_IN // N_DEV), 2), ("w_conv_out", (DEPTH, CONV_DIM, D_MODEL // N_DEV), 2),
       ("w_uq", (DEPTH, Q_RANK, HEADS * QK_DIM // N_DEV), 2), ("w_ukv", (DEPTH, KV_RANK, HEADS * (NOPE + V_DIM) // N_DEV), 2),
       ("w_attn_out", (DEPTH, HEADS * V_DIM, D_MODEL // N_DEV), 2), ("w_hgrn_out", (DEPTH, 512, D_MODEL // N_DEV), 2),
       ("w_out", (DEPTH, D_MODEL // N_DEV, D_MODEL), 1), ("w_ff1", (DEPTH, D_MODEL, D_FF // N_DEV), 2),
       ("w_ff2", (DEPTH, D_FF // N_DEV, D_MODEL), 1))
SMALL_SHARDED = (("meta", (N_META, D_MODEL // N_DEV), 1), ("conv_w", (DEPTH, CONV_K, CONV_DIM // N_DEV), 2))
REPLICATED = (("norm1_g", (DEPTH, D_MODEL)), ("conv_b", (DEPTH, CONV_DIM)), ("conv_ln_g", (DEPTH, CONV_DIM)),
              ("conv_ln_b", (DEPTH, CONV_DIM)), ("q_a_norm_g", (DEPTH, Q_RANK)), ("kv_a_norm_g", (DEPTH, KV_RANK)),
              ("q_norm_g", (DEPTH, QK_DIM)), ("k_norm_g", (DEPTH, QK_DIM)), ("hgrn_lb_logits", (DEPTH, 512)),
              ("hgrn_norm_g", (DEPTH, 512)), ("norm2_g", (DEPTH, D_MODEL)))
WEIGHT_ORDER = ("meta", "norm1_g", "w_in", "conv_w", "conv_b", "conv_ln_g", "conv_ln_b", "w_conv_out", "q_a_norm_g", "w_uq",
                "kv_a_norm_g", "w_ukv", "q_norm_g", "k_norm_g", "w_attn_out", "hgrn_lb_logits", "hgrn_norm_g", "w_hgrn_out",
                "w_out", "norm2_g", "w_ff1", "w_ff2")


def _rows_for(n_elems, mult):
    rows = -(-n_elems // PACK_W)
    return -(-rows // mult) * mult


def _pack(arrays, dtype, mult, lead=()):
    nl = len(lead)
    flat = jnp.concatenate([a.reshape(lead + (-1,)).astype(dtype) for a in arrays], axis=nl)
    rows = _rows_for(flat.shape[nl], mult)
    flat = jnp.pad(flat, [(0, 0)] * nl + [(0, rows * PACK_W - flat.shape[nl])])
    return flat.reshape(lead + (rows, PACK_W))


def _unpack(pack, shapes, lead=()):
    nl = len(lead)
    flat = pack.reshape(lead + (-1,))
    out, off = [], 0
    for shp in shapes:
        n = int(np.prod(shp))
        out.append(lax.slice_in_dim(flat, off, off + n, axis=nl).reshape(lead + tuple(shp)))
        off += n
    return out


def _join_shards(g, axis):
    g = jnp.moveaxis(g, 0, axis)
    shp = g.shape
    return g.reshape(shp[:axis] + (shp[axis] * shp[axis + 1],) + shp[axis + 2:])


def _cut_shards(a, axis):
    shp = a.shape
    a = a.reshape(shp[:axis] + (N_DEV, shp[axis] // N_DEV) + shp[axis + 1:])
    return jnp.moveaxis(a, axis, 0)


def kernel(x, meta, norm1_g, w_in, conv_w, conv_b, conv_ln_g, conv_ln_b, w_conv_out, q_a_norm_g, w_uq, kv_a_norm_g, w_ukv, q_norm_g, k_norm_g, w_attn_out, hgrn_lb_logits, hgrn_norm_g, w_hgrn_out, w_out, norm2_g, w_ff1, w_ff2, loss_target, m_meta, m_norm1_g, m_w_in, m_conv_w, m_conv_b, m_conv_ln_g, m_conv_ln_b, m_w_conv_out, m_q_a_norm_g, m_w_uq, m_kv_a_norm_g, m_w_ukv, m_q_norm_g, m_k_norm_g, m_w_attn_out, m_hgrn_lb_logits, m_hgrn_norm_g, m_w_hgrn_out, m_w_out, m_norm2_g, m_w_ff1, m_w_ff2, v_meta, v_norm1_g, v_w_in, v_conv_w, v_conv_b, v_conv_ln_g, v_conv_ln_b, v_w_conv_out, v_q_a_norm_g, v_w_uq, v_kv_a_norm_g, v_w_ukv, v_q_norm_g, v_k_norm_g, v_w_attn_out, v_hgrn_lb_logits, v_hgrn_norm_g, v_w_hgrn_out, v_w_out, v_norm2_g, v_w_ff1, v_w_ff2):
    args = dict(locals())
    wts = {n: args[n] for n in WEIGHT_ORDER}
    mom = {n: args["m_" + n] for n in WEIGHT_ORDER}
    var = {n: args["v_" + n] for n in WEIGHT_ORDER}
    xi, yi, ci = _mesh_pos()
    me = 4 * xi + 2 * yi + ci

    shard = lambda l: [wts[n][l].astype(BF16) for n, _, _ in BIG]
    gathered = _all_gather(shard(0) + [_pack([wts[n] for n, _, _ in SMALL_SHARDED], F32, 8)], "gather_layer0")
    small = dict(zip([n for n, _, _ in SMALL_SHARDED],
                     [_join_shards(g, axis) for (_, _, axis), g in
                      zip(SMALL_SHARDED, _unpack(gathered[-1], [s for _, s, _ in SMALL_SHARDED], (N_DEV,)))]))
    pending = _copy_start(shard(1), True, "gather_layer1_start", 5)

    def layer_weights(l, xp):
        if l == 0:
            mats = gathered[:-1]
            xp = xp + pending[4][0, 0]
        else:
            own, lands = _copy_wait(pending[0], pending[1], pending[2], pending[3], xp, True, "gather_layer1_wait")
            mats = [lax.dynamic_update_index_in_dim(z, s, me, 0) for z, s in zip(lands, own)]
        full = {n: wts[n][l] for n, _ in REPLICATED}
        full["conv_w"] = small["conv_w"][l]
        for (n, _, axis), g in zip(BIG, mats):
            full[n] = _join_shards(g, axis - 1)
        return full, xp

    layer_grads = [None] * DEPTH
    flight = []

    def layer_done(l, g, dx):
        layer_grads[l] = g
        if l == 1:
            send = [_cut_shards(g[n], axis - 1).astype(BF16) for n, _, axis in BIG]
            flight.append(_copy_start(send, False, "scatter_layer1_start", 6))
            dx = dx + flight[0][4][0, 0]
        return dx

    loss, grad_x, g_meta, g_lb = _run_step(x[0], loss_target[0], small["meta"], wts["hgrn_lb_logits"],
                                           layer_weights, layer_done)
    loss = lax.psum(loss, ("x", "y", "c"))
    recv0 = _exchange([_cut_shards(layer_grads[0][n], axis - 1).astype(BF16) for n, _, axis in BIG], "scatter_layer0")
    s_sems, r_sems, sent, lands, _ = flight[0]
    sent, lands = _copy_wait(s_sems, r_sems, sent, lands, grad_x, False, "scatter_layer1_wait")
    recv1 = [lax.dynamic_update_index_in_dim(z, lax.dynamic_index_in_dim(s, me, 0, keepdims=False), me, 0)
             for z, s in zip(lands, sent)]

    out = {}
    for (n, _, _), r0, r1 in zip(BIG, recv0, recv1):
        res4 = _adamw_layers([r0, r1], wts[n], mom[n], var[n], "adamw_" + n)
        for kind, a in zip(("grad_", "delta_", "new_m_", "new_v_"), res4):
            out[kind + n] = a

    big_names = {n for n, _, _ in BIG}
    grads = {k: jnp.stack([layer_grads[l][k] for l in range(DEPTH)]) for k in layer_grads[0] if k not in big_names}
    grads["hgrn_lb_logits"] = g_lb
    grads["meta"] = g_meta
    small_names = [n for n, _ in REPLICATED] + [n for n, _, _ in SMALL_SHARDED]
    part = _pack([grads[n] for n in small_names], F32, 8)
    total = _sum_parts(_all_gather([part], "gather_small_grads")[0], "sum_small_grads")
    tot = dict(zip(small_names, _unpack(total, [grads[n].shape for n in small_names])))
    mine = {n: tot[n] for n, _ in REPLICATED}
    for n, shp, axis in SMALL_SHARDED:
        mine[n] = lax.dynamic_slice_in_dim(tot[n], me * shp[axis], shp[axis], axis=axis)
    pk = lambda d: _pack([d[n] for n in small_names], F32, 8)
    small_out = _adamw(pk(mine)[None], pk(wts), pk(mom), pk(var), "adamw_vectors")
    for kind, pack in zip(("grad_", "delta_", "new_m_", "new_v_"), small_out):
        for n, a in zip(small_names, _unpack(pack, [wts[n].shape for n in small_names])):
            out[kind + n] = a

    res = [loss, grad_x[None]]
    for kind in ("grad_", "delta_", "new_m_", "new_v_"):
        res += [out[kind + n] for n in WEIGHT_ORDER]
    return tuple(res)
```

```python
import functools

import numpy as np
import jax
import jax.numpy as jnp
from jax import lax
from jax.experimental import pallas as pl
from jax.experimental.pallas import tpu as pltpu

F32 = jnp.float32
BF16 = jnp.bfloat16

D_MODEL = 1024
DEPTH = 2
N_META = 16
PAD_FRONT = 112
ROW0 = PAD_FRONT + N_META
EPS = 1e-6
GATE_CLAMP = 1.0 - 1e-6
CONV_DIM = 512
CONV_K = 31
HEADS = 8
Q_RANK = 256
KV_RANK = 128
NOPE = 64
ROPE = 32
V_DIM = 64
QK_DIM = NOPE + ROPE
HEAD_W = 128
ROPE_BASE = 10000.0
HG_HEADS = 4
HG_DK = 128
HG_DV = 128
HG_CHUNK = 64
D_FF = 4096
N_IN = 6560
C_CONV_A, C_CONV_G, C_GATE, C_CQ, C_CKV, C_KR, C_HQ, C_HF, C_HI, C_HG = (
    0, 512, 1024, 4096, 4352, 4480, 4608, 5120, 5632, 6144)
N_IN_P = 6656
O_CQ, O_KR, O_HQ, O_GATE = 1024, 1408, 1440, 3488
KR_LANE = NOPE

ADAM_LR = 0.001
ADAM_B1 = 0.9
ADAM_B2 = 0.999
ADAM_EPS = 1e-08
ADAM_WD = 0.01
ADAM_STEP = 10

N_DEV = 8
VMEM_LIMIT = 56 * 1024 * 1024
MESH = pl.DeviceIdType.MESH


def _pick(n, cands):
    for c in cands:
        if n % c == 0:
            return c
    raise ValueError(f"no tile for {n}")


def _cparams(sem, **kw):
    return pltpu.CompilerParams(dimension_semantics=sem, vmem_limit_bytes=VMEM_LIMIT, **kw)


def _relu2(v):
    return jnp.square(jnp.maximum(v, 0.0))


def _mm(a, b, *, ta=False, tb=False, out_dtype=F32, res=None, a_fn=None, epi=None, name):
    M, K = (a.shape[1], a.shape[0]) if ta else a.shape
    N = b.shape[0] if tb else b.shape[1]
    assert (b.shape[1] if tb else b.shape[0]) == K, (a.shape, b.shape, ta, tb)
    tm = _pick(M, (1056, 1024, 512, 384, 256, 128, 96))
    tn = _pick(N, (1664, 1024, 512, 384, 256, 128))
    tk = _pick(K, (1664, 1056, 1024, 512, 384, 256, 128, 96))
    nk = K // tk
    dims = (((0 if ta else 1,), (1 if tb else 0,)), ((), ()))
    extras = ([res] if res is not None else []) + ([epi[0]] if epi is not None else [])

    def body(*refs):
        a_ref, b_ref = refs[0], refs[1]
        r_ref = refs[2] if res is not None else None
        e_ref = refs[2 + (res is not None)] if epi is not None else None
        o_ref = refs[2 + len(extras)]
        acc = refs[-1] if nk > 1 else None
        k = pl.program_id(2)
        av = a_ref[...]
        if a_fn is not None:
            av = a_fn(av.astype(F32))
        p = lax.dot_general(av.astype(BF16), b_ref[...].astype(BF16), dims, preferred_element_type=F32)

        def finish(total):
            if e_ref is not None:
                total = epi[1](total, e_ref[...].astype(F32))
            if r_ref is not None:
                total = total + r_ref[...].astype(F32)
            o_ref[...] = total.astype(o_ref.dtype)

        if nk == 1:
            finish(p)
        else:
            @pl.when(k == 0)
            def _():
                acc[...] = p

            @pl.when(k > 0)
            def _():
                acc[...] += p

            @pl.when(k == nk - 1)
            def _():
                finish(acc[...])

    a_spec = pl.BlockSpec((tk, tm), lambda i, j, k: (k, i)) if ta else pl.BlockSpec((tm, tk), lambda i, j, k: (i, k))
    b_spec = pl.BlockSpec((tn, tk), lambda i, j, k: (j, k)) if tb else pl.BlockSpec((tk, tn), lambda i, j, k: (k, j))
    o_spec = pl.BlockSpec((tm, tn), lambda i, j, k: (i, j))
    in_specs = [a_spec, b_spec] + [o_spec] * len(extras)
    args = (a, b) + tuple(extras)
    return pl.pallas_call(
        body, name=name, grid=(M // tm, N // tn, nk), in_specs=in_specs, out_specs=o_spec,
        out_shape=jax.ShapeDtypeStruct((M, N), out_dtype),
        scratch_shapes=[pltpu.VMEM((tm, tn), F32)] if nk > 1 else [],
        compiler_params=_cparams(("parallel", "parallel", "arbitrary")),
    )(*args)


class Row:
    def __init__(self, arr, width=None, col=0, piece=None):
        self.arr = arr
        self.width = arr.shape[1] if width is None else width
        assert col % self.width == 0
        self.blk = col // self.width
        self.piece = self.width if piece is None else piece

    def spec(self, tm):
        blk = self.blk
        return pl.BlockSpec((tm, self.width), lambda i: (i, blk))


def _split(v, piece):
    w = v.shape[-1]
    if piece == w:
        return v
    return [v[:, j * piece:(j + 1) * piece] for j in range(w // piece)]


def _store(ref, val, dtype=None):
    if isinstance(val, (list, tuple)):
        piece = val[0].shape[-1]
        for j, p in enumerate(val):
            ref[:, j * piece:(j + 1) * piece] = p.astype(ref.dtype)
    else:
        ref[...] = val.astype(ref.dtype)


def _row_tile(T):
    return _pick(T, (384, 352, 192, 128))


def _param2d(p):
    return p.reshape(1, -1).astype(F32)


def _rowwise(fn, T, rows, params, outs, name):
    tm = _row_tile(T)
    nr, npar = len(rows), len(params)
    par = [(_param2d(p), piece) for p, piece in params]

    def body(*refs):
        rid = pl.program_id(0) * tm + lax.broadcasted_iota(jnp.int32, (tm, 1), 0)
        rv = [_split(refs[n][...].astype(F32), rows[n].piece) for n in range(nr)]
        pv = [_split(refs[nr + n][...], par[n][1]) for n in range(npar)]
        res = fn(rid, rv, pv)
        for n, val in enumerate(res):
            _store(refs[nr + npar + n], val)

    return pl.pallas_call(
        body, name=name, grid=(T // tm,),
        in_specs=[r.spec(tm) for r in rows] + [pl.BlockSpec(p.shape, lambda i: (0, 0)) for p, _ in par],
        out_specs=[pl.BlockSpec((tm, w), lambda i: (i, 0)) for w, _ in outs],
        out_shape=[jax.ShapeDtypeStruct((T, w), dt) for w, dt in outs],
        compiler_params=_cparams(("parallel",)),
    )(*[r.arr for r in rows], *[p for p, _ in par])


def _rowwise_bwd(fn, T, rows, params, cts, drow, name, add=None):
    tm = _row_tile(T)
    nr, npar, nct = len(rows), len(params), len(cts)
    par = [(_param2d(p), piece) for p, piece in params]
    didx = sorted(drow)
    has_add = add is not None

    def body(*refs):
        i = pl.program_id(0)
        rid = i * tm + lax.broadcasted_iota(jnp.int32, (tm, 1), 0)
        rv = [_split(refs[n][...].astype(F32), rows[n].piece) for n in range(nr)]
        pv = [_split(refs[nr + n][...], par[n][1]) for n in range(npar)]
        cv = [_split(refs[nr + npar + n][...].astype(F32), cts[n].piece) for n in range(nct)]
        base = nr + npar + nct + (1 if has_add else 0)
        d_refs = refs[base:base + len(didx)]
        p_refs = refs[base + len(didx):]

        def g(dvals, pvals):
            full = list(rv)
            for n, v in zip(didx, dvals):
                full[n] = v
            return fn(rid, full, pvals)

        _, vjp = jax.vjp(g, [rv[n] for n in didx], pv)
        d_rows, d_pars = vjp(cv)
        for slot, n in enumerate(didx):
            val = d_rows[slot]
            if has_add and add[0] == n:
                assert not isinstance(val, (list, tuple))
                val = val + refs[nr + npar + nct][...].astype(F32)
            _store(d_refs[slot], val)

        @pl.when(i == 0)
        def _():
            for r in p_refs:
                r[...] = jnp.zeros_like(r)

        for r, val in zip(p_refs, d_pars):
            if isinstance(val, (list, tuple)):
                piece = val[0].shape[-1]
                for j, p in enumerate(val):
                    r[:, j * piece:(j + 1) * piece] += p
            else:
                r[...] += val

    in_specs = ([r.spec(tm) for r in rows] + [pl.BlockSpec(p.shape, lambda i: (0, 0)) for p, _ in par]
                + [c.spec(tm) for c in cts])
    args = [r.arr for r in rows] + [p for p, _ in par] + [c.arr for c in cts]
    if has_add:
        in_specs.append(pl.BlockSpec((tm, rows[add[0]].width), lambda i: (i, 0)))
        args.append(add[1])
    out_specs = ([pl.BlockSpec((tm, rows[n].width), lambda i: (i, 0)) for n in didx]
                 + [pl.BlockSpec(p.shape, lambda i: (0, 0)) for p, _ in par])
    out_shape = ([jax.ShapeDtypeStruct((T, rows[n].width), drow[n]) for n in didx]
                 + [jax.ShapeDtypeStruct(p.shape, F32) for p, _ in par])
    res = pl.pallas_call(
        body, name=name, grid=(T // tm,), in_specs=in_specs, out_specs=out_specs, out_shape=out_shape,
        compiler_params=_cparams(("arbitrary",)),
    )(*args)
    return list(res[:len(didx)]), list(res[len(didx):])


def _f_rms(rid, rv, pv):
    x, g = rv[0], pv[0]
    return [x * lax.rsqrt(jnp.mean(x * x, axis=-1, keepdims=True) + EPS) * g]


def _f_glu(rid, rv, pv):
    a, gt = rv
    return [a * jax.nn.sigmoid(gt) * (rid >= PAD_FRONT).astype(F32)]


def _f_lnsilu(rid, rv, pv):
    x = rv[0]
    g, b = pv
    mu = jnp.mean(x, axis=-1, keepdims=True)
    xc = x - mu
    y = xc * lax.rsqrt(jnp.mean(xc * xc, axis=-1, keepdims=True) + EPS) * g + b
    return [y * jax.nn.sigmoid(y)]


@functools.partial(jax.custom_vjp, nondiff_argnums=(1,))
def _lane_roll(x, shift):
    return pltpu.roll(x, shift, 1)


def _lane_roll_fwd(x, shift):
    return pltpu.roll(x, shift, 1), None


def _lane_roll_bwd(shift, _, g):
    return (pltpu.roll(g, (HEAD_W - shift) % HEAD_W, 1),)


_lane_roll.defvjp(_lane_roll_fwd, _lane_roll_bwd)


def _head_norm_rope(xh, g, c, s1, s2):
    y = xh * lax.rsqrt(jnp.sum(xh * xh, axis=-1, keepdims=True) * (1.0 / QK_DIM) + EPS) * g
    half = ROPE // 2
    return y * c + _lane_roll(y, HEAD_W - half) * s1 + _lane_roll(y, half) * s2


def _f_qrope(rid, rv, pv):
    q, c, s1, s2 = rv
    return [[_head_norm_rope(qh, pv[0], c, s1, s2) * ATT_SCALE for qh in q]]


def _f_krope(rid, rv, pv):
    k, kr, c, s1, s2 = rv
    return [[_head_norm_rope(kh + kr, pv[0], c, s1, s2) for kh in k]]


def _f_hgrn_prep(rid, rv, pv):
    hf, hi = rv
    m = (rid >= PAD_FRONT).astype(F32)
    kk = (1.0 - pv[0]) * jax.nn.sigmoid(-hf) * m
    lf = jnp.log1p(-jnp.minimum(kk, GATE_CLAMP))
    vv = hi * jax.nn.sigmoid(hi) * m
    return [kk, lf, vv]


def _f_hgrn_out(rid, rv, pv):
    o, hg = rv
    ng = pv[0]
    out = []
    for oh, gh, nh in zip(o, hg, ng):
        y = oh * lax.rsqrt(jnp.mean(oh * oh, axis=-1, keepdims=True) + EPS) * nh
        out.append(y * (gh * jax.nn.sigmoid(gh)))
    return [out]


def _f_mix(rid, rv, pv):
    g0, g1, g2, ya, yb, yc = rv
    return [jax.nn.sigmoid(g0) * ya + jax.nn.sigmoid(g1) * yb + jax.nn.sigmoid(g2) * yc]


def _f_relu2(rid, rv, pv):
    return [jnp.square(jax.nn.relu(rv[0]))]


def _loss_head(x2, tgt, T):
    tm = _row_tile(T)

    def body(x_ref, t_ref, dx_ref, l_ref):
        i = pl.program_id(0)
        rid = i * tm + lax.broadcasted_iota(jnp.int32, (tm, 1), 0)
        diff = (x_ref[...] - t_ref[...]) * (rid >= ROW0).astype(F32)
        dx_ref[...] = diff * (1.0 / D_MODEL)

        @pl.when(i == 0)
        def _():
            l_ref[...] = jnp.zeros_like(l_ref)

        l_ref[...] += jnp.sum(diff * diff, axis=0, keepdims=True)

    spec = pl.BlockSpec((tm, D_MODEL), lambda i: (i, 0))
    return pl.pallas_call(
        body, name="loss_head", grid=(T // tm,), in_specs=[spec, spec],
        out_specs=[spec, pl.BlockSpec((1, D_MODEL), lambda i: (0, 0))],
        out_shape=[jax.ShapeDtypeStruct((T, D_MODEL), F32), jax.ShapeDtypeStruct((1, D_MODEL), F32)],
        compiler_params=_cparams(("arbitrary",)),
    )(x2, tgt)


HALO = 32


def _conv_tile(T):
    return _pick(T, (384, 128))


def _conv_fwd(h, w, b, T, name):
    tr = _conv_tile(T)
    ratio = tr // HALO
    wp = jnp.zeros((HALO, CONV_DIM), F32).at[:CONV_K].set(w)

    def body(m_ref, h_ref, w_ref, b_ref, o_ref, win):
        i = pl.program_id(0)
        win[0:HALO, :] = h_ref[...] * (i > 0).astype(F32)
        win[HALO:, :] = m_ref[...]
        acc = jnp.broadcast_to(b_ref[...], (tr, CONV_DIM))
        for k in range(CONV_K):
            acc = acc + w_ref[k:k + 1, :] * win[pl.ds(HALO - (CONV_K - 1) + k, tr), :]
        o_ref[...] = acc

    return pl.pallas_call(
        body, name=name, grid=(T // tr,),
        in_specs=[pl.BlockSpec((tr, CONV_DIM), lambda i: (i, 0)),
                  pl.BlockSpec((HALO, CONV_DIM), lambda i: (jnp.maximum(i * ratio - 1, 0), 0)),
                  pl.BlockSpec((HALO, CONV_DIM), lambda i: (0, 0)),
                  pl.BlockSpec((1, CONV_DIM), lambda i: (0, 0))],
        out_specs=pl.BlockSpec((tr, CONV_DIM), lambda i: (i, 0)),
        out_shape=jax.ShapeDtypeStruct((T, CONV_DIM), F32),
        scratch_shapes=[pltpu.VMEM((tr + HALO, CONV_DIM), F32)],
        compiler_params=_cparams(("parallel",)),
    )(h, h, wp, _param2d(b))


def _conv_bwd(h, w, dy, T, name):
    tr = _conv_tile(T)
    ratio = tr // HALO
    n_t = T // tr
    last_halo = T // HALO - 1
    wp = jnp.zeros((HALO, CONV_DIM), F32).at[:CONV_K].set(w)

    def body(hm_ref, hh_ref, dm_ref, dh_ref, w_ref, dx_ref, dw_ref, db_ref, hwin, dwin):
        i = pl.program_id(0)
        hwin[0:HALO, :] = hh_ref[...] * (i > 0).astype(F32)
        hwin[HALO:, :] = hm_ref[...]
        dwin[0:tr, :] = dm_ref[...]
        dwin[tr:, :] = dh_ref[...] * (i < n_t - 1).astype(F32)

        @pl.when(i == 0)
        def _():
            dw_ref[...] = jnp.zeros_like(dw_ref)
            db_ref[...] = jnp.zeros_like(db_ref)

        dy_m = dm_ref[...]
        db_ref[...] += jnp.sum(dy_m, axis=0, keepdims=True)
        acc = jnp.zeros((tr, CONV_DIM), F32)
        for k in range(CONV_K):
            acc = acc + w_ref[k:k + 1, :] * dwin[pl.ds(CONV_K - 1 - k, tr), :]
            dw_ref[k:k + 1, :] += jnp.sum(dy_m * hwin[pl.ds(HALO - (CONV_K - 1) + k, tr), :], axis=0, keepdims=True)
        dx_ref[...] = acc

    main = pl.BlockSpec((tr, CONV_DIM), lambda i: (i, 0))
    return pl.pallas_call(
        body, name=name, grid=(n_t,),
        in_specs=[main,
                  pl.BlockSpec((HALO, CONV_DIM), lambda i: (jnp.maximum(i * ratio - 1, 0), 0)),
                  main,
                  pl.BlockSpec((HALO, CONV_DIM), lambda i: (jnp.minimum((i + 1) * ratio, last_halo), 0)),
                  pl.BlockSpec((HALO, CONV_DIM), lambda i: (0, 0))],
        out_specs=[main, pl.BlockSpec((HALO, CONV_DIM), lambda i: (0, 0)), pl.BlockSpec((1, CONV_DIM), lambda i: (0, 0))],
        out_shape=[jax.ShapeDtypeStruct((T, CONV_DIM), F32), jax.ShapeDtypeStruct((HALO, CONV_DIM), F32),
                   jax.ShapeDtypeStruct((1, CONV_DIM), F32)],
        scratch_shapes=[pltpu.VMEM((tr + HALO, CONV_DIM), F32), pltpu.VMEM((tr + HALO, CONV_DIM), F32)],
        compiler_params=_cparams(("arbitrary",)),
    )(h, h, dy, dy, wp)


NEG = -1e30
ATT_SCALE = QK_DIM ** -0.5
_NT = (((1,), (1,)), ((), ()))
_TN = (((0,), (0,)), ((), ()))


def _att_blk(T):
    return _pick(T, (384, 128))


def _att_mask(i, j, blk):
    kpos = j * blk + lax.broadcasted_iota(jnp.int32, (blk, blk), 0)
    qpos = i * blk + lax.broadcasted_iota(jnp.int32, (blk, blk), 1)
    return (kpos <= qpos) & (kpos >= PAD_FRONT)


def _t32(a):
    return a.astype(F32).T.astype(BF16)


def _attn_fwd(q, k, v, T, name):
    blk = _att_blk(T)
    nq = T // blk

    def body(q_ref, k_ref, v_ref, o_ref, lse_ref, vt):
        i = pl.program_id(1)

        @pl.when(i == 0)
        def _():
            def tr(j, c):
                vt[j] = _t32(v_ref[pl.ds(pl.multiple_of(j * blk, blk), blk), :])
                return c

            lax.fori_loop(0, nq, tr, 0)

        qb = q_ref[...]

        def step(js, carry, masked):
            m, l, acc = carry
            ss = []
            for j in js:
                kb = k_ref[pl.ds(pl.multiple_of(j * blk, blk), blk), :]
                s = lax.dot_general(kb, qb, _NT, preferred_element_type=F32)
                ss.append(jnp.where(_att_mask(i, j, blk), s, NEG) if masked else s)
            m_new = m
            for s in ss:
                m_new = jnp.maximum(m_new, jnp.max(s, axis=0, keepdims=True))
            alpha = jnp.exp(m - m_new)
            l = alpha * l
            acc = alpha * acc
            for j, s in zip(js, ss):
                p = jnp.exp(s - m_new)
                l = l + jnp.sum(p, axis=0, keepdims=True)
                acc = acc + jnp.dot(vt[j], p.astype(BF16), preferred_element_type=F32)
            return m_new, l, acc

        init = (jnp.full((1, blk), NEG, F32), jnp.zeros((1, blk), F32), jnp.zeros((HEAD_W, blk), F32))
        carry = step([i], init, True)
        carry = lax.fori_loop(0, jnp.minimum(i, 1), lambda j, c: step([j], c, True), carry)
        n_free = jnp.maximum(i - 1, 0)
        carry = lax.fori_loop(0, n_free // 2, lambda t, c: step([1 + 2 * t, 2 + 2 * t], c, False), carry)
        m, l, acc = lax.fori_loop(0, n_free % 2, lambda t, c: step([i - 1], c, False), carry)
        o_ref[...] = (acc / l).T.astype(o_ref.dtype)
        lse_ref[0, 0] = m + jnp.log(l)

    full = pl.BlockSpec((T, HEAD_W), lambda h, i: (0, h))
    return pl.pallas_call(
        body, name=name, grid=(HEADS, nq),
        in_specs=[pl.BlockSpec((blk, HEAD_W), lambda h, i: (i, h)), full, full],
        out_specs=[pl.BlockSpec((blk, HEAD_W), lambda h, i: (i, h)),
                   pl.BlockSpec((1, 1, 1, blk), lambda h, i: (h, i, 0, 0))],
        out_shape=[jax.ShapeDtypeStruct((T, HEADS * HEAD_W), BF16), jax.ShapeDtypeStruct((HEADS, nq, 1, blk), F32)],
        scratch_shapes=[pltpu.VMEM((nq, HEAD_W, blk), BF16)],
        compiler_params=_cparams(("parallel", "arbitrary")),
    )(q, k, v)


def _attn_bwd(q, k, v, o, lse, do, T, name):
    blk = _att_blk(T)
    nq = T // blk

    def body(q_ref, k_ref, v_ref, o_ref, lse_ref, do_ref, dq_ref, dk_ref, dv_ref, delta, dqt, dk_acc, dv_acc):
        j = pl.program_id(1)

        @pl.when(j == 0)
        def _():
            dqt[...] = jnp.zeros_like(dqt)

            def dstep(i, c):
                r0 = pl.multiple_of(i * blk, blk)
                prod = do_ref[pl.ds(r0, blk), :].astype(F32) * o_ref[pl.ds(r0, blk), :].astype(F32)
                delta[i] = jnp.sum(prod.T, axis=0, keepdims=True)
                return c

            lax.fori_loop(0, nq, dstep, 0)

        kb = k_ref[...]
        vb = v_ref[...]
        kbt = _t32(kb)
        dk_acc[...] = jnp.zeros_like(dk_acc)
        dv_acc[...] = jnp.zeros_like(dv_acc)

        def step(qs, masked):
            dvs, dks = [], []
            for i in qs:
                r0 = pl.multiple_of(i * blk, blk)
                qb = q_ref[pl.ds(r0, blk), :]
                dob = do_ref[pl.ds(r0, blk), :]
                s = lax.dot_general(kb, qb, _NT, preferred_element_type=F32)
                p = jnp.exp(s - lse_ref[0, i])
                if masked:
                    p = jnp.where(_att_mask(i, j, blk), p, 0.0)
                dvs.append(jnp.dot(p.astype(BF16), dob, preferred_element_type=F32))
                dp = lax.dot_general(vb, dob, _NT, preferred_element_type=F32)
                ds = (p * (dp - delta[i])).astype(BF16)
                dks.append(jnp.dot(ds, qb, preferred_element_type=F32))
                dqt[i] += jnp.dot(kbt, ds, preferred_element_type=F32)
            dv_acc[...] += functools.reduce(jnp.add, dvs)
            dk_acc[...] += functools.reduce(jnp.add, dks)

        def loop(lo, masked):
            n = nq - lo

            def pair(t, c):
                step([lo + 2 * t, lo + 2 * t + 1], masked)
                return c

            def last(t, c):
                step([nq - 1], masked)
                return c

            lax.fori_loop(0, n // 2, pair, 0)
            lax.fori_loop(0, n % 2, last, 0)

        @pl.when(j == 0)
        def _():
            loop(0, True)

        @pl.when(j > 0)
        def _():
            step([j], True)
            loop(j + 1, False)

        dk_ref[...] = dk_acc[...].astype(dk_ref.dtype)
        dv_ref[...] = dv_acc[...].astype(dv_ref.dtype)

        @pl.when(j == nq - 1)
        def _():
            def wstep(i, c):
                dq_ref[pl.ds(pl.multiple_of(i * blk, blk), blk), :] = dqt[i].T
                return c

            lax.fori_loop(0, nq, wstep, 0)

    full = pl.BlockSpec((T, HEAD_W), lambda h, j: (0, h))
    kblk = pl.BlockSpec((blk, HEAD_W), lambda h, j: (j, h))
    wide = (T, HEADS * HEAD_W)
    return pl.pallas_call(
        body, name=name, grid=(HEADS, nq),
        in_specs=[full, kblk, kblk, full, pl.BlockSpec((1, nq, 1, blk), lambda h, j: (h, 0, 0, 0)), full],
        out_specs=[full, kblk, kblk],
        out_shape=[jax.ShapeDtypeStruct(wide, F32), jax.ShapeDtypeStruct(wide, BF16), jax.ShapeDtypeStruct(wide, BF16)],
        scratch_shapes=[pltpu.VMEM((nq, 1, blk), F32), pltpu.VMEM((nq, HEAD_W, blk), F32),
                        pltpu.VMEM((blk, HEAD_W), F32), pltpu.VMEM((blk, HEAD_W), F32)],
        compiler_params=_cparams(("parallel", "arbitrary")),
    )(q, k, v, o, lse, do)


HG_NB = 6
C = HG_CHUNK
_HI = lax.Precision.HIGHEST


def _tri(lower):
    r = lax.broadcasted_iota(jnp.int32, (C, C), 0)
    c = lax.broadcasted_iota(jnp.int32, (C, C), 1)
    return ((c <= r) if lower else (c >= r)).astype(F32)


def _hg_intra_fwd(q, k, v, b):
    o = jnp.zeros((C, HG_DV), F32)
    rows = lax.broadcasted_iota(jnp.int32, (C, 1), 0)
    for s in range(C):
        lo = (s // 8) * 8
        e = jnp.exp(jnp.minimum(b[lo:] - b[s:s + 1], 0.0))
        a = jnp.sum(q[lo:] * k[s:s + 1] * e, axis=-1, keepdims=True)
        a = jnp.where(rows[lo:] >= s, a, 0.0)
        upd = a * v[s:s + 1]
        o = o + (upd if lo == 0 else jnp.concatenate([jnp.zeros((lo, HG_DV), F32), upd], axis=0))
    return o


def _hgrn_fwd(u, kk, lf, vv, T, name):
    nb = _pick(T // C, (HG_NB, 3, 2, 1))
    rows = nb * C
    qblk = C_HQ // HG_DK

    def body(q_ref, k_ref, lf_ref, v_ref, o_ref, st_ref, st):
        @pl.when(pl.program_id(1) == 0)
        def _():
            st[...] = jnp.zeros_like(st)

        lower = _tri(True)
        for n in range(nb):
            sl = slice(n * C, (n + 1) * C)
            q, k, v = q_ref[sl, :].astype(F32), k_ref[sl, :], v_ref[sl, :]
            b = jnp.dot(lower, lf_ref[sl, :], precision=_HI, preferred_element_type=F32)
            s_t = st[...]
            st_ref[0, n] = s_t
            qe = (q * jnp.exp(b)).astype(BF16)
            o = lax.dot_general(qe, s_t.astype(BF16), _NT, preferred_element_type=F32)
            o_ref[sl, :] = o + _hg_intra_fwd(q, k, v, b)
            bl = b[C - 1:C, :]
            kd = (k * jnp.exp(bl - b)).astype(BF16)
            st[...] = s_t * jnp.exp(bl) + lax.dot_general(v.astype(BF16), kd, _TN, preferred_element_type=F32)

    col = lambda off: pl.BlockSpec((rows, HG_DK), lambda h, c: (c, h + off))
    return pl.pallas_call(
        body, name=name, grid=(HG_HEADS, T // rows),
        in_specs=[col(qblk), col(0), col(0), col(0)],
        out_specs=[col(0), pl.BlockSpec((1, nb, HG_DV, HG_DK), lambda h, c: (h, c, 0, 0))],
        out_shape=[jax.ShapeDtypeStruct((T, HG_HEADS * HG_DV), F32),
                   jax.ShapeDtypeStruct((HG_HEADS, T // C, HG_DV, HG_DK), F32)],
        scratch_shapes=[pltpu.VMEM((HG_DV, HG_DK), F32)],
        compiler_params=_cparams(("parallel", "arbitrary")),
    )(u, kk, lf, vv)


def _hgrn_bwd(u, kk, lf, vv, states, do, T, name):
    nb = _pick(T // C, (HG_NB, 3, 2, 1))
    rows = nb * C
    n_steps = T // rows
    qblk = C_HQ // HG_DK

    def body(q_ref, k_ref, lf_ref, v_ref, st_ref, do_ref, dq_ref, dk_ref, dlf_ref, dv_ref, dst, dk_s, dv_s):
        @pl.when(pl.program_id(1) == 0)
        def _():
            dst[...] = jnp.zeros_like(dst)

        lower, upper = _tri(True), _tri(False)
        rid = lax.broadcasted_iota(jnp.int32, (C, 1), 0)
        for n in reversed(range(nb)):
            sl = slice(n * C, (n + 1) * C)
            q, k, v, do = q_ref[sl, :].astype(F32), k_ref[sl, :], v_ref[sl, :], do_ref[sl, :]
            b = jnp.dot(lower, lf_ref[sl, :], precision=_HI, preferred_element_type=F32)
            s_t = st_ref[0, n]
            d_new = dst[...]
            eb = jnp.exp(b)
            bl = b[C - 1:C, :]
            ebl = jnp.exp(bl)
            dec = jnp.exp(bl - b)
            qe = q * eb
            kd = k * dec
            do_b = do.astype(BF16)
            dqe = jnp.dot(do_b, s_t.astype(BF16), preferred_element_type=F32)
            dkd = jnp.dot(v.astype(BF16), d_new.astype(BF16), preferred_element_type=F32)
            dv = lax.dot_general(kd.astype(BF16), d_new.astype(BF16), _NT, preferred_element_type=F32)
            dbl = ebl * jnp.sum(d_new * s_t, axis=0, keepdims=True) + jnp.sum(dkd * kd, axis=0, keepdims=True)
            dst[...] = d_new * ebl + lax.dot_general(do_b, qe.astype(BF16), _TN, preferred_element_type=F32)
            dq = dqe * eb
            dk = dkd * dec
            for s in range(C):
                lo = (s // 8) * 8
                e = jnp.exp(jnp.minimum(b[lo:] - b[s:s + 1], 0.0))
                e = jnp.where(rid[lo:] >= s, e, 0.0)
                a = jnp.sum(q[lo:] * k[s:s + 1] * e, axis=-1, keepdims=True)
                da = jnp.sum(do[lo:] * v[s:s + 1], axis=-1, keepdims=True)
                g = da * e
                upd = g * k[s:s + 1]
                dq = dq + (upd if lo == 0 else jnp.concatenate([jnp.zeros((lo, HG_DK), F32), upd], axis=0))
                dk_s[s:s + 1, :] = jnp.sum(g * q[lo:], axis=0, keepdims=True)
                dv_s[s:s + 1, :] = jnp.sum(a * do[lo:], axis=0, keepdims=True)
            dk = dk + dk_s[...]
            dv = dv + dv_s[...]
            db = q * dq - k * dk
            db = db + jnp.where(rid == C - 1, dbl, 0.0)
            dq_ref[sl, :] = dq
            dk_ref[sl, :] = dk
            dv_ref[sl, :] = dv
            dlf_ref[sl, :] = jnp.dot(upper, db, precision=_HI, preferred_element_type=F32)

    rev = lambda off: pl.BlockSpec((rows, HG_DK), lambda h, c: (n_steps - 1 - c, h + off))
    return pl.pallas_call(
        body, name=name, grid=(HG_HEADS, n_steps),
        in_specs=[rev(qblk), rev(0), rev(0), rev(0),
                  pl.BlockSpec((1, nb, HG_DV, HG_DK), lambda h, c: (h, n_steps - 1 - c, 0, 0)), rev(0)],
        out_specs=[rev(0)] * 4,
        out_shape=[jax.ShapeDtypeStruct((T, HG_HEADS * HG_DK), F32)] * 4,
        scratch_shapes=[pltpu.VMEM((HG_DV, HG_DK), F32), pltpu.VMEM((C, HG_DK), F32), pltpu.VMEM((C, HG_DV), F32)],
        compiler_params=_cparams(("parallel", "arbitrary")),
    )(u, kk, lf, vv, states, do)


def _rope_tables(T):
    half = ROPE // 2
    inv_freq = (ROPE_BASE ** (-np.arange(half, dtype=np.float32) / half)).astype(np.float32)
    row = lambda lo, hi, val: np.concatenate([np.zeros(lo, np.float32), np.asarray(val, np.float32) * np.ones(hi - lo, np.float32),
                                              np.zeros(HEAD_W - hi, np.float32)])[None, :]
    freq = row(NOPE, NOPE + half, inv_freq) + row(NOPE + half, NOPE + ROPE, inv_freq)
    pos = lax.broadcasted_iota(jnp.int32, (T, HEAD_W), 0).astype(F32) - float(PAD_FRONT)
    ang = pos * freq
    cos, sin = jnp.cos(ang), jnp.sin(ang)
    c = cos * row(NOPE, NOPE + ROPE, 1.0) + row(0, NOPE, 1.0)
    s1 = sin * row(NOPE, NOPE + half, -1.0)
    s2 = sin * row(NOPE + half, NOPE + ROPE, 1.0)
    return c, s1, s2


def _layer_fwd(x, w, tabs, T, l):
    c, s1, s2 = tabs
    n = lambda s: f"l{l}_{s}"
    sv = {"x": x}
    h = _rowwise(_f_rms, T, [Row(x)], [(w["norm1_g"], D_MODEL)], [(D_MODEL, BF16)], n("norm1"))[0]
    u = _mm(h, w["w_in"], out_dtype=BF16, name=n("in_proj"))
    sv.update(h=h, u=u)
    hglu = _rowwise(_f_glu, T, [Row(u, 512, C_CONV_A), Row(u, 512, C_CONV_G)], [], [(CONV_DIM, F32)], n("glu"))[0]
    cv = _conv_fwd(hglu, w["conv_w"], w["conv_b"], T, n("conv"))
    hc = _rowwise(_f_lnsilu, T, [Row(cv)], [(w["conv_ln_g"], CONV_DIM), (w["conv_ln_b"], CONV_DIM)],
                  [(CONV_DIM, BF16)], n("conv_ln"))[0]
    y_a = _mm(hc, w["w_conv_out"], out_dtype=BF16, name=n("conv_out"))
    sv.update(hglu=hglu, cv=cv, hc=hc, y_a=y_a)
    cqn = _rowwise(_f_rms, T, [Row(u, Q_RANK, C_CQ)], [(w["q_a_norm_g"], Q_RANK)], [(Q_RANK, BF16)], n("q_a_norm"))[0]
    ckvn = _rowwise(_f_rms, T, [Row(u, KV_RANK, C_CKV)], [(w["kv_a_norm_g"], KV_RANK)], [(KV_RANK, BF16)], n("kv_a_norm"))[0]
    q_raw = _mm(cqn, w["w_uq"], out_dtype=BF16, name=n("uq"))
    k_raw = _mm(ckvn, w["w_uk"], out_dtype=BF16, name=n("uk"))
    v = _mm(ckvn, w["w_uv"], out_dtype=BF16, name=n("uv"))
    tab_rows = [Row(c), Row(s1), Row(s2)]
    q = _rowwise(_f_qrope, T, [Row(q_raw, piece=HEAD_W)] + tab_rows, [(w["q_norm_g"], HEAD_W)],
                 [(HEADS * HEAD_W, BF16)], n("q_rope"))[0]
    k = _rowwise(_f_krope, T, [Row(k_raw, piece=HEAD_W), Row(u, HEAD_W, C_KR)] + tab_rows, [(w["k_norm_g"], HEAD_W)],
                 [(HEADS * HEAD_W, BF16)], n("k_rope"))[0]
    o, lse = _attn_fwd(q, k, v, T, n("attn"))
    y_b = _mm(o, w["w_attn_out"], out_dtype=BF16, name=n("attn_out"))
    sv.update(cqn=cqn, ckvn=ckvn, q_raw=q_raw, k_raw=k_raw, v=v, q=q, k=k, o=o, lse=lse, y_b=y_b)
    kk, lf, vv = _rowwise(_f_hgrn_prep, T, [Row(u, 512, C_HF), Row(u, 512, C_HI)], [(w["lb"], 512)],
                          [(512, F32)] * 3, n("hgrn_prep"))
    o_h, states = _hgrn_fwd(u, kk, lf, vv, T, n("hgrn"))
    oh = _rowwise(_f_hgrn_out, T, [Row(o_h, piece=HG_DV), Row(u, 512, C_HG, piece=HG_DV)], [(w["hgrn_norm_g"], HG_DV)],
                  [(512, BF16)], n("hgrn_out_norm"))[0]
    y_c = _mm(oh, w["w_hgrn_out"], out_dtype=BF16, name=n("hgrn_out"))
    sv.update(kk=kk, lf=lf, vv=vv, o_h=o_h, states=states, oh=oh, y_c=y_c)
    gate_rows = [Row(u, D_MODEL, C_GATE + g * D_MODEL) for g in range(3)]
    mix = _rowwise(_f_mix, T, gate_rows + [Row(y_a), Row(y_b), Row(y_c)], [], [(D_MODEL, BF16)], n("mix"))[0]
    x1 = _mm(mix, w["w_out"], res=x, name=n("out_proj"))
    h2 = _rowwise(_f_rms, T, [Row(x1)], [(w["norm2_g"], D_MODEL)], [(D_MODEL, BF16)], n("norm2"))[0]
    f = _mm(h2, w["w_ff1"], out_dtype=BF16, name=n("ff1"))
    x2 = _mm(f, w["w_ff2"], res=x1, a_fn=_relu2, name=n("ff2"))
    sv.update(mix=mix, x1=x1, h2=h2, f=f)
    return x2, sv


def _layer_bwd(dx2, w, sv, tabs, T, l, mid=None):
    c, s1, s2 = tabs
    n = lambda s: f"l{l}_b_{s}"
    u = sv["u"]
    g = {}
    g["w_ff2"] = _mm(sv["f"], dx2, ta=True, a_fn=_relu2, name=n("dw_ff2"))
    df = _mm(dx2, w["w_ff2"], tb=True, out_dtype=BF16, name=n("d_f"),
             epi=(sv["f"], lambda d, fv: d * (2.0 * jnp.maximum(fv, 0.0))))
    g["w_ff1"] = _mm(sv["h2"], df, ta=True, name=n("dw_ff1"))
    dh2 = _mm(df, w["w_ff1"], tb=True, name=n("d_h2"))
    (dx1,), (g["norm2_g"],) = _rowwise_bwd(_f_rms, T, [Row(sv["x1"])], [(w["norm2_g"], D_MODEL)], [Row(dh2)],
                                           {0: F32}, n("norm2"), add=(0, dx2))
    g["w_out"] = _mm(sv["mix"], dx1, ta=True, name=n("dw_out"))
    if mid is not None:
        dx1 = mid(g, dx1)
    dmix = _mm(dx1, w["w_out"], tb=True, out_dtype=BF16, name=n("d_mix"))
    gate_rows = [Row(u, D_MODEL, C_GATE + i * D_MODEL) for i in range(3)]
    (dg0, dg1, dg2, dy_a, dy_b, dy_c), _ = _rowwise_bwd(
        _f_mix, T, gate_rows + [Row(sv["y_a"]), Row(sv["y_b"]), Row(sv["y_c"])], [], [Row(dmix)],
        {0: BF16, 1: BF16, 2: BF16, 3: BF16, 4: BF16, 5: BF16}, n("mix"))
    g["w_hgrn_out"] = _mm(sv["oh"], dy_c, ta=True, name=n("dw_hgrn_out"))
    doh = _mm(dy_c, w["w_hgrn_out"], tb=True, out_dtype=BF16, name=n("d_oh"))
    (do_h, dhg), (g["hgrn_norm_g"],) = _rowwise_bwd(
        _f_hgrn_out, T, [Row(sv["o_h"], piece=HG_DV), Row(u, 512, C_HG, piece=HG_DV)], [(w["hgrn_norm_g"], HG_DV)],
        [Row(doh, piece=HG_DV)], {0: F32, 1: BF16}, n("hgrn_out_norm"))
    dhq, dkk, dlf, dvv = _hgrn_bwd(u, sv["kk"], sv["lf"], sv["vv"], sv["states"], do_h, T, n("hgrn"))
    (dhf, dhi), (g["lb"],) = _rowwise_bwd(
        _f_hgrn_prep, T, [Row(u, 512, C_HF), Row(u, 512, C_HI)], [(w["lb"], 512)],
        [Row(dkk), Row(dlf), Row(dvv)], {0: BF16, 1: BF16}, n("hgrn_prep"))
    g["w_attn_out"] = _mm(sv["o"], dy_b, ta=True, name=n("dw_attn_out"))
    do = _mm(dy_b, w["w_attn_out"], tb=True, out_dtype=BF16, name=n("d_o"))
    dq, dk, dv = _attn_bwd(sv["q"], sv["k"], sv["v"], sv["o"], sv["lse"], do, T, n("attn"))
    tab_rows = [Row(c), Row(s1), Row(s2)]
    (dq_raw,), (g["q_norm_g"],) = _rowwise_bwd(
        _f_qrope, T, [Row(sv["q_raw"], piece=HEAD_W)] + tab_rows, [(w["q_norm_g"], HEAD_W)],
        [Row(dq, piece=HEAD_W)], {0: BF16}, n("q_rope"))
    (dk_raw, dkr), (g["k_norm_g"],) = _rowwise_bwd(
        _f_krope, T, [Row(sv["k_raw"], piece=HEAD_W), Row(u, HEAD_W, C_KR)] + tab_rows, [(w["k_norm_g"], HEAD_W)],
        [Row(dk, piece=HEAD_W)], {0: BF16, 1: BF16}, n("k_rope"))
    g["w_uq"] = _mm(sv["cqn"], dq_raw, ta=True, name=n("dw_uq"))
    g["w_uk"] = _mm(sv["ckvn"], dk_raw, ta=True, name=n("dw_uk"))
    g["w_uv"] = _mm(sv["ckvn"], dv, ta=True, name=n("dw_uv"))
    dcqn = _mm(dq_raw, w["w_uq"], tb=True, out_dtype=BF16, name=n("d_cqn"))
    dckvn = _mm(dk_raw, w["w_uk"], tb=True, name=n("d_ckvn_k"))
    dckvn = _mm(dv, w["w_uv"], tb=True, res=dckvn, out_dtype=BF16, name=n("d_ckvn_v"))
    (dcq,), (g["q_a_norm_g"],) = _rowwise_bwd(_f_rms, T, [Row(u, Q_RANK, C_CQ)], [(w["q_a_norm_g"], Q_RANK)],
                                              [Row(dcqn)], {0: BF16}, n("q_a_norm"))
    (dckv,), (g["kv_a_norm_g"],) = _rowwise_bwd(_f_rms, T, [Row(u, KV_RANK, C_CKV)], [(w["kv_a_norm_g"], KV_RANK)],
                                                [Row(dckvn)], {0: BF16}, n("kv_a_norm"))
    g["w_conv_out"] = _mm(sv["hc"], dy_a, ta=True, name=n("dw_conv_out"))
    dhc = _mm(dy_a, w["w_conv_out"], tb=True, out_dtype=BF16, name=n("d_hc"))
    (dcv,), (g["conv_ln_g"], g["conv_ln_b"]) = _rowwise_bwd(
        _f_lnsilu, T, [Row(sv["cv"])], [(w["conv_ln_g"], CONV_DIM), (w["conv_ln_b"], CONV_DIM)], [Row(dhc)],
        {0: F32}, n("conv_ln"))
    dhglu, dconv_w, g["conv_b"] = _conv_bwd(sv["hglu"], w["conv_w"], dcv, T, n("conv"))
    g["conv_w"] = dconv_w[:CONV_K]
    (dua, dug), _ = _rowwise_bwd(_f_glu, T, [Row(u, 512, C_CONV_A), Row(u, 512, C_CONV_G)], [], [Row(dhglu)],
                                 {0: BF16, 1: BF16}, n("glu"))
    du = jnp.concatenate([dua, dug, dg0, dg1, dg2, dcq, dckv, dkr, dhq.astype(BF16), dhf, dhi, dhg], axis=1)
    g["w_in"] = _mm(sv["h"], du, ta=True, name=n("dw_in"))
    dh = _mm(du, w["w_in"], tb=True, name=n("d_h"))
    (dx,), (g["norm1_g"],) = _rowwise_bwd(_f_rms, T, [Row(sv["x"])], [(w["norm1_g"], D_MODEL)], [Row(dh)],
                                          {0: F32}, n("norm1"), add=(0, dx1))
    return dx, g


def _pad_w_in(w_in):
    z = lambda k: jnp.zeros((w_in.shape[0], k), w_in.dtype)
    return jnp.concatenate([w_in[:, :O_CQ], w_in[:, O_GATE:], w_in[:, O_CQ:O_KR], z(KR_LANE), w_in[:, O_KR:O_HQ],
                            z(HEAD_W - KR_LANE - ROPE), w_in[:, O_HQ:O_GATE]], axis=1)


def _unpad_w_in(g):
    return jnp.concatenate([g[:, :C_GATE], g[:, C_CQ:C_KR], g[:, C_KR + KR_LANE:C_KR + KR_LANE + ROPE],
                            g[:, C_HQ:], g[:, C_GATE:C_CQ]], axis=1)


def _pad_heads(wm, per_head, lo, hi):
    lead = wm.shape[:-1]
    wh = wm.reshape(lead + (HEADS, per_head))[..., lo:hi]
    pad = [(0, 0)] * len(lead) + [(0, 0), (0, HEAD_W - (hi - lo))]
    return jnp.pad(wh, pad).reshape(lead + (HEADS * HEAD_W,))


def _unpad_heads(gm, width):
    lead = gm.shape[:-1]
    return gm.reshape(lead + (HEADS, HEAD_W))[..., :width]


def _layer_weights(full, lb):
    w = {}
    w["norm1_g"] = full["norm1_g"]
    w["w_in"] = _pad_w_in(full["w_in"])
    w["conv_w"] = full["conv_w"]
    w["conv_b"] = full["conv_b"]
    w["conv_ln_g"] = full["conv_ln_g"]
    w["conv_ln_b"] = full["conv_ln_b"]
    w["w_conv_out"] = full["w_conv_out"]
    w["q_a_norm_g"] = full["q_a_norm_g"]
    w["w_uq"] = _pad_heads(full["w_uq"], QK_DIM, 0, QK_DIM)
    w["kv_a_norm_g"] = full["kv_a_norm_g"]
    w["w_uk"] = _pad_heads(full["w_ukv"], NOPE + V_DIM, 0, NOPE)
    w["w_uv"] = _pad_heads(full["w_ukv"], NOPE + V_DIM, NOPE, NOPE + V_DIM)
    w["q_norm_g"] = jnp.pad(full["q_norm_g"], (0, HEAD_W - QK_DIM))
    w["k_norm_g"] = jnp.pad(full["k_norm_g"], (0, HEAD_W - QK_DIM))
    wa = full["w_attn_out"].reshape(HEADS, V_DIM, D_MODEL)
    w["w_attn_out"] = jnp.pad(wa, ((0, 0), (0, HEAD_W - V_DIM), (0, 0))).reshape(HEADS * HEAD_W, D_MODEL)
    w["lb"] = lb
    w["hgrn_norm_g"] = full["hgrn_norm_g"]
    w["w_hgrn_out"] = full["w_hgrn_out"]
    w["w_out"] = full["w_out"]
    w["norm2_g"] = full["norm2_g"]
    w["w_ff1"] = full["w_ff1"]
    w["w_ff2"] = full["w_ff2"]
    return w


def _layer_grads_to_original(g):
    o = {}
    for name in ("w_conv_out", "w_hgrn_out", "w_out", "w_ff1", "w_ff2", "conv_w"):
        o[name] = g[name]
    for name in ("norm1_g", "conv_b", "conv_ln_g", "conv_ln_b", "q_a_norm_g", "kv_a_norm_g", "hgrn_norm_g", "norm2_g", "lb"):
        o[name] = g[name].reshape(-1)
    o["w_in"] = _unpad_w_in(g["w_in"])
    o["w_uq"] = _unpad_heads(g["w_uq"], QK_DIM).reshape(Q_RANK, HEADS * QK_DIM)
    guk = _unpad_heads(g["w_uk"], NOPE)
    guv = _unpad_heads(g["w_uv"], V_DIM)
    o["w_ukv"] = jnp.concatenate([guk, guv], axis=-1).reshape(KV_RANK, HEADS * (NOPE + V_DIM))
    o["q_norm_g"] = g["q_norm_g"].reshape(-1)[:QK_DIM]
    o["k_norm_g"] = g["k_norm_g"].reshape(-1)[:QK_DIM]
    o["w_attn_out"] = g["w_attn_out"].reshape(HEADS, HEAD_W, D_MODEL)[:, :V_DIM].reshape(HEADS * V_DIM, D_MODEL)
    return o


def _lower_bounds(logits):
    p = jax.nn.softmax(logits.astype(F32), axis=0)
    return jnp.cumsum(p, axis=0) - p[0:1]


def _run_step(x, target, meta, lb_logits, layer_weights, layer_done, layer_mid=None):
    seq = x.shape[0]
    T = ROW0 + seq
    assert T % 128 == 0
    tabs = _rope_tables(T)
    lbs, lb_vjp = jax.vjp(_lower_bounds, lb_logits)
    xp = jnp.concatenate([jnp.zeros((PAD_FRONT, D_MODEL), F32), meta.astype(F32), x], axis=0)
    tp = jnp.concatenate([jnp.zeros((ROW0, D_MODEL), F32), target], axis=0)
    ws, svs = [], []
    for l in range(DEPTH):
        full, xp = layer_weights(l, xp)
        w = _layer_weights(full, lbs[l])
        xp, sv = _layer_fwd(xp, w, tabs, T, l)
        ws.append(w)
        svs.append(sv)
    dx, sq = _loss_head(xp, tp, T)
    loss = 0.5 * jnp.sum(sq) * (1.0 / D_MODEL)
    dlb = [None] * DEPTH
    for l in reversed(range(DEPTH)):
        mid = None if layer_mid is None else functools.partial(layer_mid, l)
        dx, g = _layer_bwd(dx, ws[l], svs[l], tabs, T, l, mid)
        g = _layer_grads_to_original(g)
        dlb[l] = g.pop("lb")
        dx = layer_done(l, g, dx)
    return loss, dx[ROW0:], dx[PAD_FRONT:ROW0], lb_vjp(jnp.stack(dlb))[0]


def _local_step(x, target, full):
    per_layer = [None] * DEPTH

    def done(l, g, dx):
        per_layer[l] = g
        return dx

    loss, gx, gmeta, glb = _run_step(
        x, target, full["meta"], full["hgrn_lb_logits"],
        lambda l, xp: ({k: v[l] for k, v in full.items() if k != "meta"}, xp), done)
    grads = {k: jnp.stack([per_layer[l][k] for l in range(DEPTH)]) for k in per_layer[0]}
    grads["hgrn_lb_logits"] = glb
    grads["meta"] = gmeta
    return loss, gx, grads


def _mesh_pos():
    return lax.axis_index("x"), lax.axis_index("y"), lax.axis_index("c")


N_COPY = N_DEV - 1


def _all_gather(arrs, name):
    n = len(arrs)

    def body(*refs):
        x_refs, out_refs = refs[:n], refs[n:2 * n]
        send_sems, recv_sems, local_sems = refs[2 * n:]
        x, y, c = _mesh_pos()
        me, sibling = (x, y, c), (x, y, 1 - c)
        chips = [(1 - x, y), (x, 1 - y), (1 - x, 1 - y)]

        def slot(a, px, py, pc):
            return out_refs[a].at[4 * px + 2 * py + pc]

        def copy(a, k, block, to, own=False):
            return pltpu.make_async_remote_copy(
                src_ref=x_refs[a] if own else slot(a, *block), dst_ref=slot(a, *block),
                send_sem=send_sems.at[a * N_COPY + k], recv_sem=recv_sems.at[a * N_COPY + k],
                device_id=to, device_id_type=MESH)

        mine = [pltpu.make_async_copy(x_refs[a], slot(a, *me), local_sems.at[a]) for a in range(n)]
        for cp in mine:
            cp.start()
        first = []
        for a in range(n):
            first.append(copy(a, 0, me, sibling, own=True))
            first += [copy(a, 1 + j, me, (*chip, c), own=True) for j, chip in enumerate(chips)]
        for cp in first:
            cp.start()
        passed = []
        for j, chip in enumerate(chips):
            for a in range(n):
                copy(a, 1 + j, (*chip, c), me).wait_recv()
                cp = copy(a, 4 + j, (*chip, c), sibling)
                cp.start()
                passed.append(cp)
        for a in range(n):
            copy(a, 0, sibling, me).wait_recv()
            for j, chip in enumerate(chips):
                copy(a, 4 + j, (*chip, 1 - c), me).wait_recv()
        for cp in first + passed:
            cp.wait_send()
        for cp in mine:
            cp.wait()

    anyspec = pl.BlockSpec(memory_space=pl.ANY)
    return pl.pallas_call(
        body, name=name, out_shape=[jax.ShapeDtypeStruct((N_DEV,) + a.shape, a.dtype) for a in arrs],
        in_specs=[anyspec] * n, out_specs=[anyspec] * n,
        scratch_shapes=[pltpu.SemaphoreType.DMA((n * N_COPY,)), pltpu.SemaphoreType.DMA((n * N_COPY,)),
                        pltpu.SemaphoreType.DMA((n,))],
    )(*arrs)


def _exchange(arrs, name):
    n = len(arrs)

    def body(*refs):
        s_refs, r_refs = refs[:n], refs[n:2 * n]
        send_sems, recv_sems, local_sems = refs[2 * n:]
        x, y, c = _mesh_pos()
        me = 4 * x + 2 * y + c
        local = [pltpu.make_async_copy(s_refs[a].at[me], r_refs[a].at[me], local_sems.at[a]) for a in range(n)]
        for cp in local:
            cp.start()
        sends, recvs = [], []
        for rel in range(1, N_DEV):
            px = 1 - x if rel & 4 else x
            py = 1 - y if rel & 2 else y
            pc = 1 - c if rel & 1 else c
            p = 4 * px + 2 * py + pc
            for a in range(n):
                k = a * N_COPY + rel - 1
                sends.append(pltpu.make_async_remote_copy(
                    src_ref=s_refs[a].at[p], dst_ref=r_refs[a].at[me], send_sem=send_sems.at[k],
                    recv_sem=recv_sems.at[k], device_id=(px, py, pc), device_id_type=MESH))
                recvs.append(pltpu.make_async_remote_copy(
                    src_ref=s_refs[a].at[me], dst_ref=r_refs[a].at[p], send_sem=send_sems.at[k],
                    recv_sem=recv_sems.at[k], device_id=(px, py, pc), device_id_type=MESH))
        for cp in sends:
            cp.start()
        for cp in recvs:
            cp.wait_recv()
        for cp in sends:
            cp.wait_send()
        for cp in local:
            cp.wait()

    anyspec = pl.BlockSpec(memory_space=pl.ANY)
    return pl.pallas_call(
        body, name=name, out_shape=[jax.ShapeDtypeStruct(a.shape, a.dtype) for a in arrs],
        in_specs=[anyspec] * n, out_specs=[anyspec] * n,
        scratch_shapes=[pltpu.SemaphoreType.DMA((n * N_COPY,)), pltpu.SemaphoreType.DMA((n * N_COPY,)),
                        pltpu.SemaphoreType.DMA((n,))],
    )(*arrs)


_HBM = pl.BlockSpec(memory_space=pltpu.HBM)
_SEM = pl.BlockSpec(memory_space=pltpu.SEMAPHORE)
_EFFECT = pltpu.SideEffectType.DATAFLOW_SIDE_EFFECTING


def _peers(x, y, c):
    out = []
    for rel in range(1, N_DEV):
        px = 1 - x if rel & 4 else x
        py = 1 - y if rel & 2 else y
        pc = 1 - c if rel & 1 else c
        out.append((rel, (px, py, pc), 4 * px + 2 * py + pc))
    return out


def _split_copies(src_refs, land_refs, send_sems, recv_sems, gather):
    x, y, c = _mesh_pos()
    me = 4 * x + 2 * y + c
    out = []
    for a, (src, land) in enumerate(zip(src_refs, land_refs)):
        for rel, peer, p in _peers(x, y, c):
            k = a * N_COPY + rel - 1
            mk = lambda s, d: pltpu.make_async_remote_copy(
                src_ref=s, dst_ref=d, send_sem=send_sems.at[k], recv_sem=recv_sems.at[k],
                device_id=peer, device_id_type=MESH)
            mine = src if gather else src.at[p]
            out.append((mk(mine, land.at[me]), mk(mine, land.at[p])))
    return out


def _copy_start(srcs, gather, name, collective_id):
    n = len(srcs)
    lands = [lax.empty(((N_DEV,) + s.shape) if gather else s.shape, s.dtype) for s in srcs]

    def body(*refs):
        src_refs, land_refs = refs[:n], refs[n:2 * n]
        send_sems, recv_sems = refs[2 * n], refs[2 * n + 1]
        token = refs[-1]
        x, y, c = _mesh_pos()
        barrier = pltpu.get_barrier_semaphore()
        for _, peer, _ in _peers(x, y, c):
            pl.semaphore_signal(barrier, inc=1, device_id=peer, device_id_type=MESH)
        pl.semaphore_wait(barrier, N_COPY)
        for out_copy, _ in _split_copies(src_refs, land_refs, send_sems, recv_sems, gather):
            out_copy.start()
        token[...] = jnp.zeros_like(token)

    hbm = lambda a: pltpu.HBM(a.shape, a.dtype)
    res = pl.pallas_call(
        body, name=name,
        out_shape=(pltpu.SemaphoreType.DMA((n * N_COPY,)), pltpu.SemaphoreType.DMA((n * N_COPY,)),
                   *[hbm(s) for s in srcs], *[hbm(z) for z in lands], jax.ShapeDtypeStruct((8, 128), F32)),
        in_specs=[_HBM] * (2 * n), out_specs=(_SEM, _SEM, *([_HBM] * (2 * n)), pl.BlockSpec(memory_space=pltpu.VMEM)),
        input_output_aliases={i: 2 + i for i in range(2 * n)},
        compiler_params=pltpu.CompilerParams(has_side_effects=_EFFECT, collective_id=collective_id),
    )(*[pltpu.with_memory_space_constraint(s, pltpu.HBM) for s in srcs],
      *[pltpu.with_memory_space_constraint(z, pltpu.HBM) for z in lands])
    return res[0], res[1], list(res[2:2 + n]), list(res[2 + n:2 + 2 * n]), res[-1]


def _copy_wait(send_sems, recv_sems, srcs, lands, after, gather, name):
    n = len(srcs)

    def body(*refs):
        src_refs, land_refs = refs[:n], refs[n:2 * n]
        s_sems, r_sems = refs[2 * n], refs[2 * n + 1]
        for out_copy, in_copy in _split_copies(src_refs, land_refs, s_sems, r_sems, gather):
            out_copy.wait_send()
            in_copy.wait_recv()

    hbm = lambda a: pltpu.HBM(a.shape, a.dtype)
    res = pl.pallas_call(
        body, name=name, out_shape=(*[hbm(s) for s in srcs], *[hbm(z) for z in lands]),
        in_specs=[_HBM] * (2 * n) + [_SEM, _SEM, pl.BlockSpec(memory_space=pl.ANY)], out_specs=tuple([_HBM] * (2 * n)),
        input_output_aliases={i: i for i in range(2 * n)},
        compiler_params=pltpu.CompilerParams(has_side_effects=_EFFECT),
    )(*srcs, *lands, send_sems, recv_sems, after)
    return list(res[:n]), list(res[n:])


def _sum_parts(parts, name):
    P, R, W = parts.shape

    def body(p_ref, o_ref):
        g = p_ref[0].astype(F32)
        for i in range(1, P):
            g = g + p_ref[i].astype(F32)
        o_ref[...] = g

    return pl.pallas_call(body, name=name, out_shape=jax.ShapeDtypeStruct((R, W), F32))(parts)


def _adamw_body(p_ref, w_ref, m_ref, v_ref, g_ref, d_ref, nm_ref, nv_ref):
    g = p_ref[0].astype(F32)
    for i in range(1, p_ref.shape[0]):
        g = g + p_ref[i].astype(F32)
    m_new = ADAM_B1 * m_ref[...] + (1.0 - ADAM_B1) * g
    v_new = ADAM_B2 * v_ref[...] + (1.0 - ADAM_B2) * jnp.square(g)
    m_hat = m_new / (1.0 - ADAM_B1 ** ADAM_STEP)
    v_hat = v_new / (1.0 - ADAM_B2 ** ADAM_STEP)
    g_ref[...] = g
    d_ref[...] = -ADAM_LR * (m_hat / (jnp.sqrt(v_hat) + ADAM_EPS) + ADAM_WD * w_ref[...])
    nm_ref[...] = m_new
    nv_ref[...] = v_new


def _adamw(parts, w, m, v, name):
    P, R, W = parts.shape
    tr = _pick(R, (368, 192, 64, 16, 8))
    spec = pl.BlockSpec((tr, W), lambda i: (i, 0))
    return pl.pallas_call(
        functools.partial(_adamw_body), name=name, grid=(R // tr,),
        in_specs=[pl.BlockSpec((P, tr, W), lambda i: (0, i, 0)), spec, spec, spec], out_specs=[spec] * 4,
        out_shape=[jax.ShapeDtypeStruct((R, W), F32)] * 4,
        compiler_params=_cparams(("parallel",)),
    )(parts, w, m, v)


def _adamw_layers(parts, w, m, v, name):
    P, B, C_ = parts[0].shape
    tb = _pick(B, (256, 128))
    nb = B // tb

    def body(*refs):
        p_refs, rest = refs[:DEPTH], refs[DEPTH:]
        a = pl.program_id(0)
        for l in range(DEPTH):
            @pl.when(a == l)
            def _():
                _adamw_body(p_refs[l], *[r.at[0] for r in rest])

    spec = pl.BlockSpec((1, tb, C_), lambda a, i: (a, i, 0))

    def part_spec(l):
        return pl.BlockSpec((P, tb, C_), lambda a, i: (0, jnp.where(a == l, i, jnp.where(a < l, 0, nb - 1)), 0))

    return pl.pallas_call(
        body, name=name, grid=(DEPTH, nb),
        in_specs=[part_spec(l) for l in range(DEPTH)] + [spec, spec, spec], out_specs=[spec] * 4,
        out_shape=[jax.ShapeDtypeStruct((DEPTH, B, C_), F32)] * 4,
        compiler_params=_cparams(("arbitrary", "arbitrary")),
    )(*parts, w, m, v)


PACK_W = 1024
BIG = (("w_in", (DEPTH, D_MODEL, N_IN // N_DEV), 2), ("w_conv_out", (DEPTH, CONV_DIM, D_MODEL // N_DEV), 2),
       ("w_uq", (DEPTH, Q_RANK, HEADS * QK_DIM // N_DEV), 2), ("w_ukv", (DEPTH, KV_RANK, HEADS * (NOPE + V_DIM) // N_DEV), 2),
       ("w_attn_out", (DEPTH, HEADS * V_DIM, D_MODEL // N_DEV), 2), ("w_hgrn_out", (DEPTH, 512, D_MODEL // N_DEV), 2),
       ("w_out", (DEPTH, D_MODEL // N_DEV, D_MODEL), 1), ("w_ff1", (DEPTH, D_MODEL, D_FF // N_DEV), 2),
       ("w_ff2", (DEPTH, D_FF // N_DEV, D_MODEL), 1))
SMALL_SHARDED = (("meta", (N_META, D_MODEL // N_DEV), 1), ("conv_w", (DEPTH, CONV_K, CONV_DIM // N_DEV), 2))
REPLICATED = (("norm1_g", (DEPTH, D_MODEL)), ("conv_b", (DEPTH, CONV_DIM)), ("conv_ln_g", (DEPTH, CONV_DIM)),
              ("conv_ln_b", (DEPTH, CONV_DIM)), ("q_a_norm_g", (DEPTH, Q_RANK)), ("kv_a_norm_g", (DEPTH, KV_RANK)),
              ("q_norm_g", (DEPTH, QK_DIM)), ("k_norm_g", (DEPTH, QK_DIM)), ("hgrn_lb_logits", (DEPTH, 512)),
              ("hgrn_norm_g", (DEPTH, 512)), ("norm2_g", (DEPTH, D_MODEL)))
WEIGHT_ORDER = ("meta", "norm1_g", "w_in", "conv_w", "conv_b", "conv_ln_g", "conv_ln_b", "w_conv_out", "q_a_norm_g", "w_uq",
                "kv_a_norm_g", "w_ukv", "q_norm_g", "k_norm_g", "w_attn_out", "hgrn_lb_logits", "hgrn_norm_g", "w_hgrn_out",
                "w_out", "norm2_g", "w_ff1", "w_ff2")


def _rows_for(n_elems, mult):
    rows = -(-n_elems // PACK_W)
    return -(-rows // mult) * mult


def _pack(arrays, dtype, mult, lead=()):
    nl = len(lead)
    flat = jnp.concatenate([a.reshape(lead + (-1,)).astype(dtype) for a in arrays], axis=nl)
    rows = _rows_for(flat.shape[nl], mult)
    flat = jnp.pad(flat, [(0, 0)] * nl + [(0, rows * PACK_W - flat.shape[nl])])
    return flat.reshape(lead + (rows, PACK_W))


def _unpack(pack, shapes, lead=()):
    nl = len(lead)
    flat = pack.reshape(lead + (-1,))
    out, off = [], 0
    for shp in shapes:
        n = int(np.prod(shp))
        out.append(lax.slice_in_dim(flat, off, off + n, axis=nl).reshape(lead + tuple(shp)))
        off += n
    return out


def _join_shards(g, axis):
    g = jnp.moveaxis(g, 0, axis)
    shp = g.shape
    return g.reshape(shp[:axis] + (shp[axis] * shp[axis + 1],) + shp[axis + 2:])


def _cut_shards(a, axis):
    shp = a.shape
    a = a.reshape(shp[:axis] + (N_DEV, shp[axis] // N_DEV) + shp[axis + 1:])
    return jnp.moveaxis(a, axis, 0)


def kernel(x, meta, norm1_g, w_in, conv_w, conv_b, conv_ln_g, conv_ln_b, w_conv_out, q_a_norm_g, w_uq, kv_a_norm_g, w_ukv, q_norm_g, k_norm_g, w_attn_out, hgrn_lb_logits, hgrn_norm_g, w_hgrn_out, w_out, norm2_g, w_ff1, w_ff2, loss_target, m_meta, m_norm1_g, m_w_in, m_conv_w, m_conv_b, m_conv_ln_g, m_conv_ln_b, m_w_conv_out, m_q_a_norm_g, m_w_uq, m_kv_a_norm_g, m_w_ukv, m_q_norm_g, m_k_norm_g, m_w_attn_out, m_hgrn_lb_logits, m_hgrn_norm_g, m_w_hgrn_out, m_w_out, m_norm2_g, m_w_ff1, m_w_ff2, v_meta, v_norm1_g, v_w_in, v_conv_w, v_conv_b, v_conv_ln_g, v_conv_ln_b, v_w_conv_out, v_q_a_norm_g, v_w_uq, v_kv_a_norm_g, v_w_ukv, v_q_norm_g, v_k_norm_g, v_w_attn_out, v_hgrn_lb_logits, v_hgrn_norm_g, v_w_hgrn_out, v_w_out, v_norm2_g, v_w_ff1, v_w_ff2):
    args = dict(locals())
    wts = {n: args[n] for n in WEIGHT_ORDER}
    mom = {n: args["m_" + n] for n in WEIGHT_ORDER}
    var = {n: args["v_" + n] for n in WEIGHT_ORDER}
    xi, yi, ci = _mesh_pos()
    me = 4 * xi + 2 * yi + ci

    shard = lambda l: [wts[n][l].astype(BF16) for n, _, _ in BIG]
    gathered = _all_gather(shard(0) + [_pack([wts[n] for n, _, _ in SMALL_SHARDED], F32, 8)], "gather_layer0")
    small = dict(zip([n for n, _, _ in SMALL_SHARDED],
                     [_join_shards(g, axis) for (_, _, axis), g in
                      zip(SMALL_SHARDED, _unpack(gathered[-1], [s for _, s, _ in SMALL_SHARDED], (N_DEV,)))]))
    pending = _copy_start(shard(1), True, "gather_layer1_start", 5)

    def layer_weights(l, xp):
        if l == 0:
            mats = gathered[:-1]
            xp = xp + pending[4][0, 0]
        else:
            own, lands = _copy_wait(pending[0], pending[1], pending[2], pending[3], xp, True, "gather_layer1_wait")
            mats = [lax.dynamic_update_index_in_dim(z, s, me, 0) for z, s in zip(lands, own)]
        full = {n: wts[n][l] for n, _ in REPLICATED}
        full["conv_w"] = small["conv_w"][l]
        for (n, _, axis), g in zip(BIG, mats):
            full[n] = _join_shards(g, axis - 1)
        return full, xp

    big_names = [n for n, _, _ in BIG]
    early = [n for n in big_names if n in ("w_out", "w_ff1", "w_ff2")]
    late = [n for n in big_names if n not in early]
    cut = lambda g, names: [_cut_shards(g[n], axis - 1).astype(BF16) for n, _, axis in BIG if n in names]
    layer_grads = [None] * DEPTH
    flight = {}

    def layer_mid(l, g, dx1):
        if l == 0:
            flight["l0_early"] = _copy_start(cut(g, early), False, "scatter_layer0_early_start", 7)
            dx1 = dx1 + flight["l0_early"][4][0, 0]
        return dx1

    def layer_done(l, g, dx):
        layer_grads[l] = g
        key, names, cid = ("l1", big_names, 6) if l == 1 else ("l0_late", late, 8)
        flight[key] = _copy_start(cut(g, names), False, f"scatter_{key}_start", cid)
        return dx + flight[key][4][0, 0]

    loss, grad_x, g_meta, g_lb = _run_step(x[0], loss_target[0], small["meta"], wts["hgrn_lb_logits"],
                                           layer_weights, layer_done, layer_mid)
    loss = lax.psum(loss, ("x", "y", "c"))

    def arrive(key, names, after):
        s_sems, r_sems, sent, lands, _ = flight[key]
        sent, lands = _copy_wait(s_sems, r_sems, sent, lands, after, False, f"scatter_{key}_wait")
        return {n: lax.dynamic_update_index_in_dim(z, lax.dynamic_index_in_dim(s, me, 0, keepdims=False), me, 0)
                for n, z, s in zip(names, lands, sent)}

    out = {}

    def update(names, recv0, recv1):
        for n in names:
            res4 = _adamw_layers([recv0[n], recv1[n]], wts[n], mom[n], var[n], "adamw_" + n)
            for kind, a in zip(("grad_", "delta_", "new_m_", "new_v_"), res4):
                out[kind + n] = a

    recv1 = arrive("l1", big_names, grad_x)
    recv0 = arrive("l0_early", early, grad_x)
    update(early, recv0, recv1)
    recv0 = arrive("l0_late", late, out["grad_" + early[-1]])
    update(late, recv0, recv1)

    big_names = {n for n, _, _ in BIG}
    grads = {k: jnp.stack([layer_grads[l][k] for l in range(DEPTH)]) for k in layer_grads[0] if k not in big_names}
    grads["hgrn_lb_logits"] = g_lb
    grads["meta"] = g_meta
    small_names = [n for n, _ in REPLICATED] + [n for n, _, _ in SMALL_SHARDED]
    part = _pack([grads[n] for n in small_names], F32, 8)
    total = _sum_parts(_all_gather([part], "gather_small_grads")[0], "sum_small_grads")
    tot = dict(zip(small_names, _unpack(total, [grads[n].shape for n in small_names])))
    mine = {n: tot[n] for n, _ in REPLICATED}
    for n, shp, axis in SMALL_SHARDED:
        mine[n] = lax.dynamic_slice_in_dim(tot[n], me * shp[axis], shp[axis], axis=axis)
    pk = lambda d: _pack([d[n] for n in small_names], F32, 8)
    small_out = _adamw(pk(mine)[None], pk(wts), pk(mom), pk(var), "adamw_vectors")
    for kind, pack in zip(("grad_", "delta_", "new_m_", "new_v_"), small_out):
        for n, a in zip(small_names, _unpack(pack, [wts[n].shape for n in small_names])):
            out[kind + n] = a

    res = [loss, grad_x[None]]
    for kind in ("grad_", "delta_", "new_m_", "new_v_"):
        res += [out[kind + n] for n in WEIGHT_ORDER]
    return tuple(res)
```

```python
import functools

import numpy as np
import jax
import jax.numpy as jnp
from jax import lax
from jax.experimental import pallas as pl
from jax.experimental.pallas import tpu as pltpu

F32 = jnp.float32
BF16 = jnp.bfloat16

D_MODEL = 1024
DEPTH = 2
N_META = 16
PAD_FRONT = 112
ROW0 = PAD_FRONT + N_META
EPS = 1e-6
GATE_CLAMP = 1.0 - 1e-6
CONV_DIM = 512
CONV_K = 31
HEADS = 8
Q_RANK = 256
KV_RANK = 128
NOPE = 64
ROPE = 32
V_DIM = 64
QK_DIM = NOPE + ROPE
HEAD_W = 128
ROPE_BASE = 10000.0
HG_HEADS = 4
HG_DK = 128
HG_DV = 128
HG_CHUNK = 64
D_FF = 4096
N_IN = 6560
C_CONV_A, C_CONV_G, C_GATE, C_CQ, C_CKV, C_KR, C_HQ, C_HF, C_HI, C_HG = (
    0, 512, 1024, 4096, 4352, 4480, 4608, 5120, 5632, 6144)
N_IN_P = 6656
O_CQ, O_KR, O_HQ, O_GATE = 1024, 1408, 1440, 3488
KR_LANE = NOPE

ADAM_LR = 0.001
ADAM_B1 = 0.9
ADAM_B2 = 0.999
ADAM_EPS = 1e-08
ADAM_WD = 0.01
ADAM_STEP = 10

N_DEV = 8
VMEM_LIMIT = 56 * 1024 * 1024
MESH = pl.DeviceIdType.MESH


def _pick(n, cands):
    for c in cands:
        if n % c == 0:
            return c
    raise ValueError(f"no tile for {n}")


def _cparams(sem, **kw):
    return pltpu.CompilerParams(dimension_semantics=sem, vmem_limit_bytes=VMEM_LIMIT, **kw)


def _relu2(v):
    return jnp.square(jnp.maximum(v, 0.0))


def _mm(a, b, *, ta=False, tb=False, out_dtype=F32, res=None, a_fn=None, epi=None, name):
    M, K = (a.shape[1], a.shape[0]) if ta else a.shape
    N = b.shape[0] if tb else b.shape[1]
    assert (b.shape[1] if tb else b.shape[0]) == K, (a.shape, b.shape, ta, tb)
    tm = _pick(M, (1056, 1024, 512, 384, 256, 128, 96))
    tn = _pick(N, (1664, 1024, 512, 384, 256, 128))
    tk = _pick(K, (1664, 1056, 1024, 512, 384, 256, 128, 96))
    nk = K // tk
    dims = (((0 if ta else 1,), (1 if tb else 0,)), ((), ()))
    extras = ([res] if res is not None else []) + ([epi[0]] if epi is not None else [])

    def body(*refs):
        a_ref, b_ref = refs[0], refs[1]
        r_ref = refs[2] if res is not None else None
        e_ref = refs[2 + (res is not None)] if epi is not None else None
        o_ref = refs[2 + len(extras)]
        acc = refs[-1] if nk > 1 else None
        k = pl.program_id(2)
        av = a_ref[...]
        if a_fn is not None:
            av = a_fn(av.astype(F32))
        p = lax.dot_general(av.astype(BF16), b_ref[...].astype(BF16), dims, preferred_element_type=F32)

        def finish(total):
            if e_ref is not None:
                total = epi[1](total, e_ref[...].astype(F32))
            if r_ref is not None:
                total = total + r_ref[...].astype(F32)
            o_ref[...] = total.astype(o_ref.dtype)

        if nk == 1:
            finish(p)
        else:
            @pl.when(k == 0)
            def _():
                acc[...] = p

            @pl.when(k > 0)
            def _():
                acc[...] += p

            @pl.when(k == nk - 1)
            def _():
                finish(acc[...])

    a_spec = pl.BlockSpec((tk, tm), lambda i, j, k: (k, i)) if ta else pl.BlockSpec((tm, tk), lambda i, j, k: (i, k))
    b_spec = pl.BlockSpec((tn, tk), lambda i, j, k: (j, k)) if tb else pl.BlockSpec((tk, tn), lambda i, j, k: (k, j))
    o_spec = pl.BlockSpec((tm, tn), lambda i, j, k: (i, j))
    in_specs = [a_spec, b_spec] + [o_spec] * len(extras)
    args = (a, b) + tuple(extras)
    return pl.pallas_call(
        body, name=name, grid=(M // tm, N // tn, nk), in_specs=in_specs, out_specs=o_spec,
        out_shape=jax.ShapeDtypeStruct((M, N), out_dtype),
        scratch_shapes=[pltpu.VMEM((tm, tn), F32)] if nk > 1 else [],
        compiler_params=_cparams(("parallel", "parallel", "arbitrary")),
    )(*args)


class Row:
    def __init__(self, arr, width=None, col=0, piece=None):
        self.arr = arr
        self.width = arr.shape[1] if width is None else width
        assert col % self.width == 0
        self.blk = col // self.width
        self.piece = self.width if piece is None else piece

    def spec(self, tm):
        blk = self.blk
        return pl.BlockSpec((tm, self.width), lambda i: (i, blk))


def _split(v, piece):
    w = v.shape[-1]
    if piece == w:
        return v
    return [v[:, j * piece:(j + 1) * piece] for j in range(w // piece)]


def _store(ref, val, dtype=None):
    if isinstance(val, (list, tuple)):
        piece = val[0].shape[-1]
        for j, p in enumerate(val):
            ref[:, j * piece:(j + 1) * piece] = p.astype(ref.dtype)
    else:
        ref[...] = val.astype(ref.dtype)


def _row_tile(T):
    return _pick(T, (384, 352, 192, 128))


def _param2d(p):
    return p.reshape(1, -1).astype(F32)


def _rowwise(fn, T, rows, params, outs, name):
    tm = _row_tile(T)
    nr, npar = len(rows), len(params)
    par = [(_param2d(p), piece) for p, piece in params]

    def body(*refs):
        rid = pl.program_id(0) * tm + lax.broadcasted_iota(jnp.int32, (tm, 1), 0)
        rv = [_split(refs[n][...].astype(F32), rows[n].piece) for n in range(nr)]
        pv = [_split(refs[nr + n][...], par[n][1]) for n in range(npar)]
        res = fn(rid, rv, pv)
        for n, val in enumerate(res):
            _store(refs[nr + npar + n], val)

    return pl.pallas_call(
        body, name=name, grid=(T // tm,),
        in_specs=[r.spec(tm) for r in rows] + [pl.BlockSpec(p.shape, lambda i: (0, 0)) for p, _ in par],
        out_specs=[pl.BlockSpec((tm, w), lambda i: (i, 0)) for w, _ in outs],
        out_shape=[jax.ShapeDtypeStruct((T, w), dt) for w, dt in outs],
        compiler_params=_cparams(("parallel",)),
    )(*[r.arr for r in rows], *[p for p, _ in par])


def _rowwise_bwd(fn, T, rows, params, cts, drow, name, add=None):
    tm = _row_tile(T)
    nr, npar, nct = len(rows), len(params), len(cts)
    par = [(_param2d(p), piece) for p, piece in params]
    didx = sorted(drow)
    has_add = add is not None

    def body(*refs):
        i = pl.program_id(0)
        rid = i * tm + lax.broadcasted_iota(jnp.int32, (tm, 1), 0)
        rv = [_split(refs[n][...].astype(F32), rows[n].piece) for n in range(nr)]
        pv = [_split(refs[nr + n][...], par[n][1]) for n in range(npar)]
        cv = [_split(refs[nr + npar + n][...].astype(F32), cts[n].piece) for n in range(nct)]
        base = nr + npar + nct + (1 if has_add else 0)
        d_refs = refs[base:base + len(didx)]
        p_refs = refs[base + len(didx):]

        def g(dvals, pvals):
            full = list(rv)
            for n, v in zip(didx, dvals):
                full[n] = v
            return fn(rid, full, pvals)

        _, vjp = jax.vjp(g, [rv[n] for n in didx], pv)
        d_rows, d_pars = vjp(cv)
        for slot, n in enumerate(didx):
            val = d_rows[slot]
            if has_add and add[0] == n:
                assert not isinstance(val, (list, tuple))
                val = val + refs[nr + npar + nct][...].astype(F32)
            _store(d_refs[slot], val)

        @pl.when(i == 0)
        def _():
            for r in p_refs:
                r[...] = jnp.zeros_like(r)

        for r, val in zip(p_refs, d_pars):
            if isinstance(val, (list, tuple)):
                piece = val[0].shape[-1]
                for j, p in enumerate(val):
                    r[:, j * piece:(j + 1) * piece] += p
            else:
                r[...] += val

    in_specs = ([r.spec(tm) for r in rows] + [pl.BlockSpec(p.shape, lambda i: (0, 0)) for p, _ in par]
                + [c.spec(tm) for c in cts])
    args = [r.arr for r in rows] + [p for p, _ in par] + [c.arr for c in cts]
    if has_add:
        in_specs.append(pl.BlockSpec((tm, rows[add[0]].width), lambda i: (i, 0)))
        args.append(add[1])
    out_specs = ([pl.BlockSpec((tm, rows[n].width), lambda i: (i, 0)) for n in didx]
                 + [pl.BlockSpec(p.shape, lambda i: (0, 0)) for p, _ in par])
    out_shape = ([jax.ShapeDtypeStruct((T, rows[n].width), drow[n]) for n in didx]
                 + [jax.ShapeDtypeStruct(p.shape, F32) for p, _ in par])
    res = pl.pallas_call(
        body, name=name, grid=(T // tm,), in_specs=in_specs, out_specs=out_specs, out_shape=out_shape,
        compiler_params=_cparams(("arbitrary",)),
    )(*args)
    return list(res[:len(didx)]), list(res[len(didx):])


def _f_rms(rid, rv, pv):
    x, g = rv[0], pv[0]
    return [x * lax.rsqrt(jnp.mean(x * x, axis=-1, keepdims=True) + EPS) * g]


def _f_glu(rid, rv, pv):
    a, gt = rv
    return [a * jax.nn.sigmoid(gt) * (rid >= PAD_FRONT).astype(F32)]


def _f_lnsilu(rid, rv, pv):
    x = rv[0]
    g, b = pv
    mu = jnp.mean(x, axis=-1, keepdims=True)
    xc = x - mu
    y = xc * lax.rsqrt(jnp.mean(xc * xc, axis=-1, keepdims=True) + EPS) * g + b
    return [y * jax.nn.sigmoid(y)]


@functools.partial(jax.custom_vjp, nondiff_argnums=(1,))
def _lane_roll(x, shift):
    return pltpu.roll(x, shift, 1)


def _lane_roll_fwd(x, shift):
    return pltpu.roll(x, shift, 1), None


def _lane_roll_bwd(shift, _, g):
    return (pltpu.roll(g, (HEAD_W - shift) % HEAD_W, 1),)


_lane_roll.defvjp(_lane_roll_fwd, _lane_roll_bwd)


def _head_norm_rope(xh, g, c, s1, s2):
    y = xh * lax.rsqrt(jnp.sum(xh * xh, axis=-1, keepdims=True) * (1.0 / QK_DIM) + EPS) * g
    half = ROPE // 2
    return y * c + _lane_roll(y, HEAD_W - half) * s1 + _lane_roll(y, half) * s2


def _f_qrope(rid, rv, pv):
    q, c, s1, s2 = rv
    return [[_head_norm_rope(qh, pv[0], c, s1, s2) * ATT_SCALE for qh in q]]


def _f_krope(rid, rv, pv):
    k, kr, c, s1, s2 = rv
    return [[_head_norm_rope(kh + kr, pv[0], c, s1, s2) for kh in k]]


def _f_hgrn_prep(rid, rv, pv):
    hf, hi = rv
    m = (rid >= PAD_FRONT).astype(F32)
    kk = (1.0 - pv[0]) * jax.nn.sigmoid(-hf) * m
    lf = jnp.log1p(-jnp.minimum(kk, GATE_CLAMP))
    vv = hi * jax.nn.sigmoid(hi) * m
    return [kk, lf, vv]


def _f_hgrn_out(rid, rv, pv):
    o, hg = rv
    ng = pv[0]
    out = []
    for oh, gh, nh in zip(o, hg, ng):
        y = oh * lax.rsqrt(jnp.mean(oh * oh, axis=-1, keepdims=True) + EPS) * nh
        out.append(y * (gh * jax.nn.sigmoid(gh)))
    return [out]


def _f_mix(rid, rv, pv):
    g0, g1, g2, ya, yb, yc = rv
    return [jax.nn.sigmoid(g0) * ya + jax.nn.sigmoid(g1) * yb + jax.nn.sigmoid(g2) * yc]


def _f_relu2(rid, rv, pv):
    return [jnp.square(jax.nn.relu(rv[0]))]


def _loss_head(x2, tgt, T):
    tm = _row_tile(T)

    def body(x_ref, t_ref, dx_ref, l_ref):
        i = pl.program_id(0)
        rid = i * tm + lax.broadcasted_iota(jnp.int32, (tm, 1), 0)
        diff = (x_ref[...] - t_ref[...]) * (rid >= ROW0).astype(F32)
        dx_ref[...] = diff * (1.0 / D_MODEL)

        @pl.when(i == 0)
        def _():
            l_ref[...] = jnp.zeros_like(l_ref)

        l_ref[...] += jnp.sum(diff * diff, axis=0, keepdims=True)

    spec = pl.BlockSpec((tm, D_MODEL), lambda i: (i, 0))
    return pl.pallas_call(
        body, name="loss_head", grid=(T // tm,), in_specs=[spec, spec],
        out_specs=[spec, pl.BlockSpec((1, D_MODEL), lambda i: (0, 0))],
        out_shape=[jax.ShapeDtypeStruct((T, D_MODEL), F32), jax.ShapeDtypeStruct((1, D_MODEL), F32)],
        compiler_params=_cparams(("arbitrary",)),
    )(x2, tgt)


HALO = 32


def _conv_tile(T):
    return _pick(T, (384, 128))


def _conv_fwd(h, w, b, T, name):
    tr = _conv_tile(T)
    ratio = tr // HALO
    wp = jnp.zeros((HALO, CONV_DIM), F32).at[:CONV_K].set(w)

    def body(m_ref, h_ref, w_ref, b_ref, o_ref, win):
        i = pl.program_id(0)
        win[0:HALO, :] = h_ref[...] * (i > 0).astype(F32)
        win[HALO:, :] = m_ref[...]
        acc = jnp.broadcast_to(b_ref[...], (tr, CONV_DIM))
        for k in range(CONV_K):
            acc = acc + w_ref[k:k + 1, :] * win[pl.ds(HALO - (CONV_K - 1) + k, tr), :]
        o_ref[...] = acc

    return pl.pallas_call(
        body, name=name, grid=(T // tr,),
        in_specs=[pl.BlockSpec((tr, CONV_DIM), lambda i: (i, 0)),
                  pl.BlockSpec((HALO, CONV_DIM), lambda i: (jnp.maximum(i * ratio - 1, 0), 0)),
                  pl.BlockSpec((HALO, CONV_DIM), lambda i: (0, 0)),
                  pl.BlockSpec((1, CONV_DIM), lambda i: (0, 0))],
        out_specs=pl.BlockSpec((tr, CONV_DIM), lambda i: (i, 0)),
        out_shape=jax.ShapeDtypeStruct((T, CONV_DIM), F32),
        scratch_shapes=[pltpu.VMEM((tr + HALO, CONV_DIM), F32)],
        compiler_params=_cparams(("parallel",)),
    )(h, h, wp, _param2d(b))


def _conv_bwd(h, w, dy, T, name):
    tr = _conv_tile(T)
    ratio = tr // HALO
    n_t = T // tr
    last_halo = T // HALO - 1
    wp = jnp.zeros((HALO, CONV_DIM), F32).at[:CONV_K].set(w)

    def body(hm_ref, hh_ref, dm_ref, dh_ref, w_ref, dx_ref, dw_ref, db_ref, hwin, dwin):
        i = pl.program_id(0)
        hwin[0:HALO, :] = hh_ref[...] * (i > 0).astype(F32)
        hwin[HALO:, :] = hm_ref[...]
        dwin[0:tr, :] = dm_ref[...]
        dwin[tr:, :] = dh_ref[...] * (i < n_t - 1).astype(F32)

        @pl.when(i == 0)
        def _():
            dw_ref[...] = jnp.zeros_like(dw_ref)
            db_ref[...] = jnp.zeros_like(db_ref)

        dy_m = dm_ref[...]
        db_ref[...] += jnp.sum(dy_m, axis=0, keepdims=True)
        acc = jnp.zeros((tr, CONV_DIM), F32)
        for k in range(CONV_K):
            acc = acc + w_ref[k:k + 1, :] * dwin[pl.ds(CONV_K - 1 - k, tr), :]
            dw_ref[k:k + 1, :] += jnp.sum(dy_m * hwin[pl.ds(HALO - (CONV_K - 1) + k, tr), :], axis=0, keepdims=True)
        dx_ref[...] = acc

    main = pl.BlockSpec((tr, CONV_DIM), lambda i: (i, 0))
    return pl.pallas_call(
        body, name=name, grid=(n_t,),
        in_specs=[main,
                  pl.BlockSpec((HALO, CONV_DIM), lambda i: (jnp.maximum(i * ratio - 1, 0), 0)),
                  main,
                  pl.BlockSpec((HALO, CONV_DIM), lambda i: (jnp.minimum((i + 1) * ratio, last_halo), 0)),
                  pl.BlockSpec((HALO, CONV_DIM), lambda i: (0, 0))],
        out_specs=[main, pl.BlockSpec((HALO, CONV_DIM), lambda i: (0, 0)), pl.BlockSpec((1, CONV_DIM), lambda i: (0, 0))],
        out_shape=[jax.ShapeDtypeStruct((T, CONV_DIM), F32), jax.ShapeDtypeStruct((HALO, CONV_DIM), F32),
                   jax.ShapeDtypeStruct((1, CONV_DIM), F32)],
        scratch_shapes=[pltpu.VMEM((tr + HALO, CONV_DIM), F32), pltpu.VMEM((tr + HALO, CONV_DIM), F32)],
        compiler_params=_cparams(("arbitrary",)),
    )(h, h, dy, dy, wp)


NEG = -1e30
ATT_SCALE = QK_DIM ** -0.5
_NT = (((1,), (1,)), ((), ()))
_TN = (((0,), (0,)), ((), ()))


def _att_blk(T):
    return _pick(T, (384, 128))


def _att_mask(i, j, blk):
    kpos = j * blk + lax.broadcasted_iota(jnp.int32, (blk, blk), 0)
    qpos = i * blk + lax.broadcasted_iota(jnp.int32, (blk, blk), 1)
    return (kpos <= qpos) & (kpos >= PAD_FRONT)


def _t32(a):
    return a.astype(F32).T.astype(BF16)


def _attn_fwd(q, k, v, T, name):
    blk = _att_blk(T)
    nq = T // blk

    def body(q_ref, k_ref, v_ref, o_ref, lse_ref, vt):
        i = pl.program_id(1)

        @pl.when(i == 0)
        def _():
            def tr(j, c):
                vt[j] = _t32(v_ref[pl.ds(pl.multiple_of(j * blk, blk), blk), :])
                return c

            lax.fori_loop(0, nq, tr, 0)

        qb = q_ref[...]

        def step(js, carry, masked):
            m, l, acc = carry
            ss = []
            for j in js:
                kb = k_ref[pl.ds(pl.multiple_of(j * blk, blk), blk), :]
                s = lax.dot_general(kb, qb, _NT, preferred_element_type=F32)
                ss.append(jnp.where(_att_mask(i, j, blk), s, NEG) if masked else s)
            m_new = m
            for s in ss:
                m_new = jnp.maximum(m_new, jnp.max(s, axis=0, keepdims=True))
            alpha = jnp.exp(m - m_new)
            l = alpha * l
            acc = alpha * acc
            for j, s in zip(js, ss):
                p = jnp.exp(s - m_new)
                l = l + jnp.sum(p, axis=0, keepdims=True)
                acc = acc + jnp.dot(vt[j], p.astype(BF16), preferred_element_type=F32)
            return m_new, l, acc

        init = (jnp.full((1, blk), NEG, F32), jnp.zeros((1, blk), F32), jnp.zeros((HEAD_W, blk), F32))
        carry = step([i], init, True)
        carry = lax.fori_loop(0, jnp.minimum(i, 1), lambda j, c: step([j], c, True), carry)
        n_free = jnp.maximum(i - 1, 0)
        carry = lax.fori_loop(0, n_free // 2, lambda t, c: step([1 + 2 * t, 2 + 2 * t], c, False), carry)
        m, l, acc = lax.fori_loop(0, n_free % 2, lambda t, c: step([i - 1], c, False), carry)
        o_ref[...] = (acc / l).T.astype(o_ref.dtype)
        lse_ref[0, 0] = m + jnp.log(l)

    full = pl.BlockSpec((T, HEAD_W), lambda h, i: (0, h))
    return pl.pallas_call(
        body, name=name, grid=(HEADS, nq),
        in_specs=[pl.BlockSpec((blk, HEAD_W), lambda h, i: (i, h)), full, full],
        out_specs=[pl.BlockSpec((blk, HEAD_W), lambda h, i: (i, h)),
                   pl.BlockSpec((1, 1, 1, blk), lambda h, i: (h, i, 0, 0))],
        out_shape=[jax.ShapeDtypeStruct((T, HEADS * HEAD_W), BF16), jax.ShapeDtypeStruct((HEADS, nq, 1, blk), F32)],
        scratch_shapes=[pltpu.VMEM((nq, HEAD_W, blk), BF16)],
        compiler_params=_cparams(("parallel", "arbitrary")),
    )(q, k, v)


def _attn_bwd(q, k, v, o, lse, do, T, name):
    blk = _att_blk(T)
    nq = T // blk

    def body(q_ref, k_ref, v_ref, o_ref, lse_ref, do_ref, dq_ref, dk_ref, dv_ref, delta, dqt, dk_acc, dv_acc):
        j = pl.program_id(1)

        @pl.when(j == 0)
        def _():
            dqt[...] = jnp.zeros_like(dqt)

            def dstep(i, c):
                r0 = pl.multiple_of(i * blk, blk)
                prod = do_ref[pl.ds(r0, blk), :].astype(F32) * o_ref[pl.ds(r0, blk), :].astype(F32)
                delta[i] = jnp.sum(prod.T, axis=0, keepdims=True)
                return c

            lax.fori_loop(0, nq, dstep, 0)

        kb = k_ref[...]
        vb = v_ref[...]
        kbt = _t32(kb)
        dk_acc[...] = jnp.zeros_like(dk_acc)
        dv_acc[...] = jnp.zeros_like(dv_acc)

        def step(qs, masked):
            dvs, dks = [], []
            for i in qs:
                r0 = pl.multiple_of(i * blk, blk)
                qb = q_ref[pl.ds(r0, blk), :]
                dob = do_ref[pl.ds(r0, blk), :]
                s = lax.dot_general(kb, qb, _NT, preferred_element_type=F32)
                p = jnp.exp(s - lse_ref[0, i])
                if masked:
                    p = jnp.where(_att_mask(i, j, blk), p, 0.0)
                dvs.append(jnp.dot(p.astype(BF16), dob, preferred_element_type=F32))
                dp = lax.dot_general(vb, dob, _NT, preferred_element_type=F32)
                ds = (p * (dp - delta[i])).astype(BF16)
                dks.append(jnp.dot(ds, qb, preferred_element_type=F32))
                dqt[i] += jnp.dot(kbt, ds, preferred_element_type=F32)
            dv_acc[...] += functools.reduce(jnp.add, dvs)
            dk_acc[...] += functools.reduce(jnp.add, dks)

        def loop(lo, masked):
            n = nq - lo

            def pair(t, c):
                step([lo + 2 * t, lo + 2 * t + 1], masked)
                return c

            def last(t, c):
                step([nq - 1], masked)
                return c

            lax.fori_loop(0, n // 2, pair, 0)
            lax.fori_loop(0, n % 2, last, 0)

        @pl.when(j == 0)
        def _():
            loop(0, True)

        @pl.when(j > 0)
        def _():
            step([j], True)
            loop(j + 1, False)

        dk_ref[...] = dk_acc[...].astype(dk_ref.dtype)
        dv_ref[...] = dv_acc[...].astype(dv_ref.dtype)

        @pl.when(j == nq - 1)
        def _():
            def wstep(i, c):
                dq_ref[pl.ds(pl.multiple_of(i * blk, blk), blk), :] = dqt[i].T
                return c

            lax.fori_loop(0, nq, wstep, 0)

    full = pl.BlockSpec((T, HEAD_W), lambda h, j: (0, h))
    kblk = pl.BlockSpec((blk, HEAD_W), lambda h, j: (j, h))
    wide = (T, HEADS * HEAD_W)
    return pl.pallas_call(
        body, name=name, grid=(HEADS, nq),
        in_specs=[full, kblk, kblk, full, pl.BlockSpec((1, nq, 1, blk), lambda h, j: (h, 0, 0, 0)), full],
        out_specs=[full, kblk, kblk],
        out_shape=[jax.ShapeDtypeStruct(wide, F32), jax.ShapeDtypeStruct(wide, BF16), jax.ShapeDtypeStruct(wide, BF16)],
        scratch_shapes=[pltpu.VMEM((nq, 1, blk), F32), pltpu.VMEM((nq, HEAD_W, blk), F32),
                        pltpu.VMEM((blk, HEAD_W), F32), pltpu.VMEM((blk, HEAD_W), F32)],
        compiler_params=_cparams(("parallel", "arbitrary")),
    )(q, k, v, o, lse, do)


HG_NB = 6
C = HG_CHUNK
_HI = lax.Precision.HIGHEST


def _tri(lower):
    r = lax.broadcasted_iota(jnp.int32, (C, C), 0)
    c = lax.broadcasted_iota(jnp.int32, (C, C), 1)
    return ((c <= r) if lower else (c >= r)).astype(F32)


HG_SUB = 8
N_SUB = C // HG_SUB


def _hg_split_decay(b, I, rid):
    lo = I * HG_SUB
    r = b[lo:lo + 1]
    eq = jnp.exp(b[lo:lo + HG_SUB] - r)
    ek = jnp.where(rid < lo, jnp.exp(jnp.minimum(r - b, 0.0)), 0.0)
    return eq, ek


def _hg_intra_fwd(q, k, v, b):
    rid = lax.broadcasted_iota(jnp.int32, (C, 1), 0)
    tid = lax.broadcasted_iota(jnp.int32, (HG_SUB, 1), 0)
    a_rows = [jnp.zeros((HG_SUB, C), F32)]
    blocks = []
    for I in range(N_SUB):
        lo = I * HG_SUB
        q_i, b_i = q[lo:lo + HG_SUB], b[lo:lo + HG_SUB]
        if I > 0:
            eq, ek = _hg_split_decay(b, I, rid)
            a_rows.append(lax.dot_general((q_i * eq).astype(BF16), (k * ek).astype(BF16), _NT,
                                          preferred_element_type=F32))
        o_i = jnp.zeros((HG_SUB, HG_DV), F32)
        for s in range(HG_SUB):
            r = lo + s
            e = jnp.exp(jnp.minimum(b_i - b[r:r + 1], 0.0))
            a = jnp.sum(q_i * k[r:r + 1] * e, axis=-1, keepdims=True)
            o_i = o_i + jnp.where(tid >= s, a, 0.0) * v[r:r + 1]
        blocks.append(o_i)
    a_off = jnp.concatenate(a_rows, axis=0).astype(BF16)
    return jnp.dot(a_off, v.astype(BF16), preferred_element_type=F32) + jnp.concatenate(blocks, axis=0)


def _hg_intra_bwd(q, k, v, b, do, dk_s, dv_s):
    rid = lax.broadcasted_iota(jnp.int32, (C, 1), 0)
    tid = lax.broadcasted_iota(jnp.int32, (HG_SUB, 1), 0)
    da_all = lax.dot_general(do.astype(BF16), v.astype(BF16), _NT, preferred_element_type=F32)
    a_rows = [jnp.zeros((HG_SUB, C), F32)]
    dq_blocks = []
    dk = jnp.zeros((C, HG_DK), F32)
    for I in range(N_SUB):
        lo = I * HG_SUB
        q_i, b_i, do_i = q[lo:lo + HG_SUB], b[lo:lo + HG_SUB], do[lo:lo + HG_SUB]
        dq_i = jnp.zeros((HG_SUB, HG_DK), F32)
        if I > 0:
            eq, ek = _hg_split_decay(b, I, rid)
            qs, ks = (q_i * eq).astype(BF16), (k * ek).astype(BF16)
            a_rows.append(lax.dot_general(qs, ks, _NT, preferred_element_type=F32))
            da = da_all[lo:lo + HG_SUB].astype(BF16)
            dq_i = jnp.dot(da, ks, preferred_element_type=F32) * eq
            dk = dk + lax.dot_general(da, qs, _TN, preferred_element_type=F32) * ek
        for s in range(HG_SUB):
            r = lo + s
            e = jnp.where(tid >= s, jnp.exp(jnp.minimum(b_i - b[r:r + 1], 0.0)), 0.0)
            a = jnp.sum(q_i * k[r:r + 1] * e, axis=-1, keepdims=True)
            g = jnp.sum(do_i * v[r:r + 1], axis=-1, keepdims=True) * e
            dq_i = dq_i + g * k[r:r + 1]
            dk_s[r:r + 1, :] = jnp.sum(g * q_i, axis=0, keepdims=True)
            dv_s[r:r + 1, :] = jnp.sum(a * do_i, axis=0, keepdims=True)
        dq_blocks.append(dq_i)
    a_off = jnp.concatenate(a_rows, axis=0).astype(BF16)
    dv = lax.dot_general(a_off, do.astype(BF16), _TN, preferred_element_type=F32)
    return jnp.concatenate(dq_blocks, axis=0), dk + dk_s[...], dv + dv_s[...]


def _hgrn_fwd(u, kk, lf, vv, T, name):
    nb = _pick(T // C, (HG_NB, 3, 2, 1))
    rows = nb * C
    qblk = C_HQ // HG_DK

    def body(q_ref, k_ref, lf_ref, v_ref, o_ref, st_ref, st):
        @pl.when(pl.program_id(1) == 0)
        def _():
            st[...] = jnp.zeros_like(st)

        lower = _tri(True)
        for n in range(nb):
            sl = slice(n * C, (n + 1) * C)
            q, k, v = q_ref[sl, :].astype(F32), k_ref[sl, :], v_ref[sl, :]
            b = jnp.dot(lower, lf_ref[sl, :], precision=_HI, preferred_element_type=F32)
            s_t = st[...]
            st_ref[0, n] = s_t
            qe = (q * jnp.exp(b)).astype(BF16)
            o = lax.dot_general(qe, s_t.astype(BF16), _NT, preferred_element_type=F32)
            o_ref[sl, :] = o + _hg_intra_fwd(q, k, v, b)
            bl = b[C - 1:C, :]
            kd = (k * jnp.exp(bl - b)).astype(BF16)
            st[...] = s_t * jnp.exp(bl) + lax.dot_general(v.astype(BF16), kd, _TN, preferred_element_type=F32)

    col = lambda off: pl.BlockSpec((rows, HG_DK), lambda h, c: (c, h + off))
    return pl.pallas_call(
        body, name=name, grid=(HG_HEADS, T // rows),
        in_specs=[col(qblk), col(0), col(0), col(0)],
        out_specs=[col(0), pl.BlockSpec((1, nb, HG_DV, HG_DK), lambda h, c: (h, c, 0, 0))],
        out_shape=[jax.ShapeDtypeStruct((T, HG_HEADS * HG_DV), F32),
                   jax.ShapeDtypeStruct((HG_HEADS, T // C, HG_DV, HG_DK), F32)],
        scratch_shapes=[pltpu.VMEM((HG_DV, HG_DK), F32)],
        compiler_params=_cparams(("parallel", "arbitrary")),
    )(u, kk, lf, vv)


def _hgrn_bwd(u, kk, lf, vv, states, do, T, name):
    nb = _pick(T // C, (HG_NB, 3, 2, 1))
    rows = nb * C
    n_steps = T // rows
    qblk = C_HQ // HG_DK

    def body(q_ref, k_ref, lf_ref, v_ref, st_ref, do_ref, dq_ref, dk_ref, dlf_ref, dv_ref, dst, dk_s, dv_s):
        @pl.when(pl.program_id(1) == 0)
        def _():
            dst[...] = jnp.zeros_like(dst)

        lower, upper = _tri(True), _tri(False)
        rid = lax.broadcasted_iota(jnp.int32, (C, 1), 0)
        for n in reversed(range(nb)):
            sl = slice(n * C, (n + 1) * C)
            q, k, v, do = q_ref[sl, :].astype(F32), k_ref[sl, :], v_ref[sl, :], do_ref[sl, :]
            b = jnp.dot(lower, lf_ref[sl, :], precision=_HI, preferred_element_type=F32)
            s_t = st_ref[0, n]
            d_new = dst[...]
            eb = jnp.exp(b)
            bl = b[C - 1:C, :]
            ebl = jnp.exp(bl)
            dec = jnp.exp(bl - b)
            qe = q * eb
            kd = k * dec
            do_b = do.astype(BF16)
            dqe = jnp.dot(do_b, s_t.astype(BF16), preferred_element_type=F32)
            dkd = jnp.dot(v.astype(BF16), d_new.astype(BF16), preferred_element_type=F32)
            dv = lax.dot_general(kd.astype(BF16), d_new.astype(BF16), _NT, preferred_element_type=F32)
            dbl = ebl * jnp.sum(d_new * s_t, axis=0, keepdims=True) + jnp.sum(dkd * kd, axis=0, keepdims=True)
            dst[...] = d_new * ebl + lax.dot_general(do_b, qe.astype(BF16), _TN, preferred_element_type=F32)
            dq_in, dk_in, dv_in = _hg_intra_bwd(q, k, v, b, do, dk_s, dv_s)
            dq = dqe * eb + dq_in
            dk = dkd * dec + dk_in
            dv = dv + dv_in
            db = q * dq - k * dk
            db = db + jnp.where(rid == C - 1, dbl, 0.0)
            dq_ref[sl, :] = dq
            dk_ref[sl, :] = dk
            dv_ref[sl, :] = dv
            dlf_ref[sl, :] = jnp.dot(upper, db, precision=_HI, preferred_element_type=F32)

    rev = lambda off: pl.BlockSpec((rows, HG_DK), lambda h, c: (n_steps - 1 - c, h + off))
    return pl.pallas_call(
        body, name=name, grid=(HG_HEADS, n_steps),
        in_specs=[rev(qblk), rev(0), rev(0), rev(0),
                  pl.BlockSpec((1, nb, HG_DV, HG_DK), lambda h, c: (h, n_steps - 1 - c, 0, 0)), rev(0)],
        out_specs=[rev(0)] * 4,
        out_shape=[jax.ShapeDtypeStruct((T, HG_HEADS * HG_DK), F32)] * 4,
        scratch_shapes=[pltpu.VMEM((HG_DV, HG_DK), F32), pltpu.VMEM((C, HG_DK), F32), pltpu.VMEM((C, HG_DV), F32)],
        compiler_params=_cparams(("parallel", "arbitrary")),
    )(u, kk, lf, vv, states, do)


def _rope_tables(T):
    half = ROPE // 2
    inv_freq = (ROPE_BASE ** (-np.arange(half, dtype=np.float32) / half)).astype(np.float32)
    row = lambda lo, hi, val: np.concatenate([np.zeros(lo, np.float32), np.asarray(val, np.float32) * np.ones(hi - lo, np.float32),
                                              np.zeros(HEAD_W - hi, np.float32)])[None, :]
    freq = row(NOPE, NOPE + half, inv_freq) + row(NOPE + half, NOPE + ROPE, inv_freq)
    pos = lax.broadcasted_iota(jnp.int32, (T, HEAD_W), 0).astype(F32) - float(PAD_FRONT)
    ang = pos * freq
    cos, sin = jnp.cos(ang), jnp.sin(ang)
    c = cos * row(NOPE, NOPE + ROPE, 1.0) + row(0, NOPE, 1.0)
    s1 = sin * row(NOPE, NOPE + half, -1.0)
    s2 = sin * row(NOPE + half, NOPE + ROPE, 1.0)
    return c, s1, s2


def _layer_fwd(x, w, tabs, T, l):
    c, s1, s2 = tabs
    n = lambda s: f"l{l}_{s}"
    sv = {"x": x}
    h = _rowwise(_f_rms, T, [Row(x)], [(w["norm1_g"], D_MODEL)], [(D_MODEL, BF16)], n("norm1"))[0]
    u = _mm(h, w["w_in"], out_dtype=BF16, name=n("in_proj"))
    sv.update(h=h, u=u)
    hglu = _rowwise(_f_glu, T, [Row(u, 512, C_CONV_A), Row(u, 512, C_CONV_G)], [], [(CONV_DIM, F32)], n("glu"))[0]
    cv = _conv_fwd(hglu, w["conv_w"], w["conv_b"], T, n("conv"))
    hc = _rowwise(_f_lnsilu, T, [Row(cv)], [(w["conv_ln_g"], CONV_DIM), (w["conv_ln_b"], CONV_DIM)],
                  [(CONV_DIM, BF16)], n("conv_ln"))[0]
    y_a = _mm(hc, w["w_conv_out"], out_dtype=BF16, name=n("conv_out"))
    sv.update(hglu=hglu, cv=cv, hc=hc, y_a=y_a)
    cqn = _rowwise(_f_rms, T, [Row(u, Q_RANK, C_CQ)], [(w["q_a_norm_g"], Q_RANK)], [(Q_RANK, BF16)], n("q_a_norm"))[0]
    ckvn = _rowwise(_f_rms, T, [Row(u, KV_RANK, C_CKV)], [(w["kv_a_norm_g"], KV_RANK)], [(KV_RANK, BF16)], n("kv_a_norm"))[0]
    q_raw = _mm(cqn, w["w_uq"], out_dtype=BF16, name=n("uq"))
    k_raw = _mm(ckvn, w["w_uk"], out_dtype=BF16, name=n("uk"))
    v = _mm(ckvn, w["w_uv"], out_dtype=BF16, name=n("uv"))
    tab_rows = [Row(c), Row(s1), Row(s2)]
    q = _rowwise(_f_qrope, T, [Row(q_raw, piece=HEAD_W)] + tab_rows, [(w["q_norm_g"], HEAD_W)],
                 [(HEADS * HEAD_W, BF16)], n("q_rope"))[0]
    k = _rowwise(_f_krope, T, [Row(k_raw, piece=HEAD_W), Row(u, HEAD_W, C_KR)] + tab_rows, [(w["k_norm_g"], HEAD_W)],
                 [(HEADS * HEAD_W, BF16)], n("k_rope"))[0]
    o, lse = _attn_fwd(q, k, v, T, n("attn"))
    y_b = _mm(o, w["w_attn_out"], out_dtype=BF16, name=n("attn_out"))
    sv.update(cqn=cqn, ckvn=ckvn, q_raw=q_raw, k_raw=k_raw, v=v, q=q, k=k, o=o, lse=lse, y_b=y_b)
    kk, lf, vv = _rowwise(_f_hgrn_prep, T, [Row(u, 512, C_HF), Row(u, 512, C_HI)], [(w["lb"], 512)],
                          [(512, F32)] * 3, n("hgrn_prep"))
    o_h, states = _hgrn_fwd(u, kk, lf, vv, T, n("hgrn"))
    oh = _rowwise(_f_hgrn_out, T, [Row(o_h, piece=HG_DV), Row(u, 512, C_HG, piece=HG_DV)], [(w["hgrn_norm_g"], HG_DV)],
                  [(512, BF16)], n("hgrn_out_norm"))[0]
    y_c = _mm(oh, w["w_hgrn_out"], out_dtype=BF16, name=n("hgrn_out"))
    sv.update(kk=kk, lf=lf, vv=vv, o_h=o_h, states=states, oh=oh, y_c=y_c)
    gate_rows = [Row(u, D_MODEL, C_GATE + g * D_MODEL) for g in range(3)]
    mix = _rowwise(_f_mix, T, gate_rows + [Row(y_a), Row(y_b), Row(y_c)], [], [(D_MODEL, BF16)], n("mix"))[0]
    x1 = _mm(mix, w["w_out"], res=x, name=n("out_proj"))
    h2 = _rowwise(_f_rms, T, [Row(x1)], [(w["norm2_g"], D_MODEL)], [(D_MODEL, BF16)], n("norm2"))[0]
    f = _mm(h2, w["w_ff1"], out_dtype=BF16, name=n("ff1"))
    x2 = _mm(f, w["w_ff2"], res=x1, a_fn=_relu2, name=n("ff2"))
    sv.update(mix=mix, x1=x1, h2=h2, f=f)
    return x2, sv


def _layer_bwd(dx2, w, sv, tabs, T, l, mid=None):
    c, s1, s2 = tabs
    n = lambda s: f"l{l}_b_{s}"
    u = sv["u"]
    g = {}
    g["w_ff2"] = _mm(sv["f"], dx2, ta=True, a_fn=_relu2, name=n("dw_ff2"))
    df = _mm(dx2, w["w_ff2"], tb=True, out_dtype=BF16, name=n("d_f"),
             epi=(sv["f"], lambda d, fv: d * (2.0 * jnp.maximum(fv, 0.0))))
    g["w_ff1"] = _mm(sv["h2"], df, ta=True, name=n("dw_ff1"))
    dh2 = _mm(df, w["w_ff1"], tb=True, name=n("d_h2"))
    (dx1,), (g["norm2_g"],) = _rowwise_bwd(_f_rms, T, [Row(sv["x1"])], [(w["norm2_g"], D_MODEL)], [Row(dh2)],
                                           {0: F32}, n("norm2"), add=(0, dx2))
    g["w_out"] = _mm(sv["mix"], dx1, ta=True, name=n("dw_out"))
    if mid is not None:
        dx1 = mid(g, dx1)
    dmix = _mm(dx1, w["w_out"], tb=True, out_dtype=BF16, name=n("d_mix"))
    gate_rows = [Row(u, D_MODEL, C_GATE + i * D_MODEL) for i in range(3)]
    (dg0, dg1, dg2, dy_a, dy_b, dy_c), _ = _rowwise_bwd(
        _f_mix, T, gate_rows + [Row(sv["y_a"]), Row(sv["y_b"]), Row(sv["y_c"])], [], [Row(dmix)],
        {0: BF16, 1: BF16, 2: BF16, 3: BF16, 4: BF16, 5: BF16}, n("mix"))
    g["w_hgrn_out"] = _mm(sv["oh"], dy_c, ta=True, name=n("dw_hgrn_out"))
    doh = _mm(dy_c, w["w_hgrn_out"], tb=True, out_dtype=BF16, name=n("d_oh"))
    (do_h, dhg), (g["hgrn_norm_g"],) = _rowwise_bwd(
        _f_hgrn_out, T, [Row(sv["o_h"], piece=HG_DV), Row(u, 512, C_HG, piece=HG_DV)], [(w["hgrn_norm_g"], HG_DV)],
        [Row(doh, piece=HG_DV)], {0: F32, 1: BF16}, n("hgrn_out_norm"))
    dhq, dkk, dlf, dvv = _hgrn_bwd(u, sv["kk"], sv["lf"], sv["vv"], sv["states"], do_h, T, n("hgrn"))
    (dhf, dhi), (g["lb"],) = _rowwise_bwd(
        _f_hgrn_prep, T, [Row(u, 512, C_HF), Row(u, 512, C_HI)], [(w["lb"], 512)],
        [Row(dkk), Row(dlf), Row(dvv)], {0: BF16, 1: BF16}, n("hgrn_prep"))
    g["w_attn_out"] = _mm(sv["o"], dy_b, ta=True, name=n("dw_attn_out"))
    do = _mm(dy_b, w["w_attn_out"], tb=True, out_dtype=BF16, name=n("d_o"))
    dq, dk, dv = _attn_bwd(sv["q"], sv["k"], sv["v"], sv["o"], sv["lse"], do, T, n("attn"))
    tab_rows = [Row(c), Row(s1), Row(s2)]
    (dq_raw,), (g["q_norm_g"],) = _rowwise_bwd(
        _f_qrope, T, [Row(sv["q_raw"], piece=HEAD_W)] + tab_rows, [(w["q_norm_g"], HEAD_W)],
        [Row(dq, piece=HEAD_W)], {0: BF16}, n("q_rope"))
    (dk_raw, dkr), (g["k_norm_g"],) = _rowwise_bwd(
        _f_krope, T, [Row(sv["k_raw"], piece=HEAD_W), Row(u, HEAD_W, C_KR)] + tab_rows, [(w["k_norm_g"], HEAD_W)],
        [Row(dk, piece=HEAD_W)], {0: BF16, 1: BF16}, n("k_rope"))
    g["w_uq"] = _mm(sv["cqn"], dq_raw, ta=True, name=n("dw_uq"))
    g["w_uk"] = _mm(sv["ckvn"], dk_raw, ta=True, name=n("dw_uk"))
    g["w_uv"] = _mm(sv["ckvn"], dv, ta=True, name=n("dw_uv"))
    dcqn = _mm(dq_raw, w["w_uq"], tb=True, out_dtype=BF16, name=n("d_cqn"))
    dckvn = _mm(dk_raw, w["w_uk"], tb=True, name=n("d_ckvn_k"))
    dckvn = _mm(dv, w["w_uv"], tb=True, res=dckvn, out_dtype=BF16, name=n("d_ckvn_v"))
    (dcq,), (g["q_a_norm_g"],) = _rowwise_bwd(_f_rms, T, [Row(u, Q_RANK, C_CQ)], [(w["q_a_norm_g"], Q_RANK)],
                                              [Row(dcqn)], {0: BF16}, n("q_a_norm"))
    (dckv,), (g["kv_a_norm_g"],) = _rowwise_bwd(_f_rms, T, [Row(u, KV_RANK, C_CKV)], [(w["kv_a_norm_g"], KV_RANK)],
                                                [Row(dckvn)], {0: BF16}, n("kv_a_norm"))
    g["w_conv_out"] = _mm(sv["hc"], dy_a, ta=True, name=n("dw_conv_out"))
    dhc = _mm(dy_a, w["w_conv_out"], tb=True, out_dtype=BF16, name=n("d_hc"))
    (dcv,), (g["conv_ln_g"], g["conv_ln_b"]) = _rowwise_bwd(
        _f_lnsilu, T, [Row(sv["cv"])], [(w["conv_ln_g"], CONV_DIM), (w["conv_ln_b"], CONV_DIM)], [Row(dhc)],
        {0: F32}, n("conv_ln"))
    dhglu, dconv_w, g["conv_b"] = _conv_bwd(sv["hglu"], w["conv_w"], dcv, T, n("conv"))
    g["conv_w"] = dconv_w[:CONV_K]
    (dua, dug), _ = _rowwise_bwd(_f_glu, T, [Row(u, 512, C_CONV_A), Row(u, 512, C_CONV_G)], [], [Row(dhglu)],
                                 {0: BF16, 1: BF16}, n("glu"))
    du = jnp.concatenate([dua, dug, dg0, dg1, dg2, dcq, dckv, dkr, dhq.astype(BF16), dhf, dhi, dhg], axis=1)
    g["w_in"] = _mm(sv["h"], du, ta=True, name=n("dw_in"))
    dh = _mm(du, w["w_in"], tb=True, name=n("d_h"))
    (dx,), (g["norm1_g"],) = _rowwise_bwd(_f_rms, T, [Row(sv["x"])], [(w["norm1_g"], D_MODEL)], [Row(dh)],
                                          {0: F32}, n("norm1"), add=(0, dx1))
    return dx, g


def _pad_w_in(w_in):
    z = lambda k: jnp.zeros((w_in.shape[0], k), w_in.dtype)
    return jnp.concatenate([w_in[:, :O_CQ], w_in[:, O_GATE:], w_in[:, O_CQ:O_KR], z(KR_LANE), w_in[:, O_KR:O_HQ],
                            z(HEAD_W - KR_LANE - ROPE), w_in[:, O_HQ:O_GATE]], axis=1)


def _unpad_w_in(g):
    return jnp.concatenate([g[:, :C_GATE], g[:, C_CQ:C_KR], g[:, C_KR + KR_LANE:C_KR + KR_LANE + ROPE],
                            g[:, C_HQ:], g[:, C_GATE:C_CQ]], axis=1)


def _pad_heads(wm, per_head, lo, hi):
    lead = wm.shape[:-1]
    wh = wm.reshape(lead + (HEADS, per_head))[..., lo:hi]
    pad = [(0, 0)] * len(lead) + [(0, 0), (0, HEAD_W - (hi - lo))]
    return jnp.pad(wh, pad).reshape(lead + (HEADS * HEAD_W,))


def _unpad_heads(gm, width):
    lead = gm.shape[:-1]
    return gm.reshape(lead + (HEADS, HEAD_W))[..., :width]


def _layer_weights(full, lb):
    w = {}
    w["norm1_g"] = full["norm1_g"]
    w["w_in"] = _pad_w_in(full["w_in"])
    w["conv_w"] = full["conv_w"]
    w["conv_b"] = full["conv_b"]
    w["conv_ln_g"] = full["conv_ln_g"]
    w["conv_ln_b"] = full["conv_ln_b"]
    w["w_conv_out"] = full["w_conv_out"]
    w["q_a_norm_g"] = full["q_a_norm_g"]
    w["w_uq"] = _pad_heads(full["w_uq"], QK_DIM, 0, QK_DIM)
    w["kv_a_norm_g"] = full["kv_a_norm_g"]
    w["w_uk"] = _pad_heads(full["w_ukv"], NOPE + V_DIM, 0, NOPE)
    w["w_uv"] = _pad_heads(full["w_ukv"], NOPE + V_DIM, NOPE, NOPE + V_DIM)
    w["q_norm_g"] = jnp.pad(full["q_norm_g"], (0, HEAD_W - QK_DIM))
    w["k_norm_g"] = jnp.pad(full["k_norm_g"], (0, HEAD_W - QK_DIM))
    wa = full["w_attn_out"].reshape(HEADS, V_DIM, D_MODEL)
    w["w_attn_out"] = jnp.pad(wa, ((0, 0), (0, HEAD_W - V_DIM), (0, 0))).reshape(HEADS * HEAD_W, D_MODEL)
    w["lb"] = lb
    w["hgrn_norm_g"] = full["hgrn_norm_g"]
    w["w_hgrn_out"] = full["w_hgrn_out"]
    w["w_out"] = full["w_out"]
    w["norm2_g"] = full["norm2_g"]
    w["w_ff1"] = full["w_ff1"]
    w["w_ff2"] = full["w_ff2"]
    return w


def _layer_grads_to_original(g):
    o = {}
    for name in ("w_conv_out", "w_hgrn_out", "w_out", "w_ff1", "w_ff2", "conv_w"):
        o[name] = g[name]
    for name in ("norm1_g", "conv_b", "conv_ln_g", "conv_ln_b", "q_a_norm_g", "kv_a_norm_g", "hgrn_norm_g", "norm2_g", "lb"):
        o[name] = g[name].reshape(-1)
    o["w_in"] = _unpad_w_in(g["w_in"])
    o["w_uq"] = _unpad_heads(g["w_uq"], QK_DIM).reshape(Q_RANK, HEADS * QK_DIM)
    guk = _unpad_heads(g["w_uk"], NOPE)
    guv = _unpad_heads(g["w_uv"], V_DIM)
    o["w_ukv"] = jnp.concatenate([guk, guv], axis=-1).reshape(KV_RANK, HEADS * (NOPE + V_DIM))
    o["q_norm_g"] = g["q_norm_g"].reshape(-1)[:QK_DIM]
    o["k_norm_g"] = g["k_norm_g"].reshape(-1)[:QK_DIM]
    o["w_attn_out"] = g["w_attn_out"].reshape(HEADS, HEAD_W, D_MODEL)[:, :V_DIM].reshape(HEADS * V_DIM, D_MODEL)
    return o


def _lower_bounds(logits):
    p = jax.nn.softmax(logits.astype(F32), axis=0)
    return jnp.cumsum(p, axis=0) - p[0:1]


def _run_step(x, target, meta, lb_logits, layer_weights, layer_done, layer_mid=None):
    seq = x.shape[0]
    T = ROW0 + seq
    assert T % 128 == 0
    tabs = _rope_tables(T)
    lbs, lb_vjp = jax.vjp(_lower_bounds, lb_logits)
    xp = jnp.concatenate([jnp.zeros((PAD_FRONT, D_MODEL), F32), meta.astype(F32), x], axis=0)
    tp = jnp.concatenate([jnp.zeros((ROW0, D_MODEL), F32), target], axis=0)
    ws, svs = [], []
    for l in range(DEPTH):
        full, xp = layer_weights(l, xp)
        w = _layer_weights(full, lbs[l])
        xp, sv = _layer_fwd(xp, w, tabs, T, l)
        ws.append(w)
        svs.append(sv)
    dx, sq = _loss_head(xp, tp, T)
    loss = 0.5 * jnp.sum(sq) * (1.0 / D_MODEL)
    dlb = [None] * DEPTH
    for l in reversed(range(DEPTH)):
        mid = None if layer_mid is None else functools.partial(layer_mid, l)
        dx, g = _layer_bwd(dx, ws[l], svs[l], tabs, T, l, mid)
        g = _layer_grads_to_original(g)
        dlb[l] = g.pop("lb")
        dx = layer_done(l, g, dx)
    return loss, dx[ROW0:], dx[PAD_FRONT:ROW0], lb_vjp(jnp.stack(dlb))[0]


def _local_step(x, target, full):
    per_layer = [None] * DEPTH

    def done(l, g, dx):
        per_layer[l] = g
        return dx

    loss, gx, gmeta, glb = _run_step(
        x, target, full["meta"], full["hgrn_lb_logits"],
        lambda l, xp: ({k: v[l] for k, v in full.items() if k != "meta"}, xp), done)
    grads = {k: jnp.stack([per_layer[l][k] for l in range(DEPTH)]) for k in per_layer[0]}
    grads["hgrn_lb_logits"] = glb
    grads["meta"] = gmeta
    return loss, gx, grads


def _mesh_pos():
    return lax.axis_index("x"), lax.axis_index("y"), lax.axis_index("c")


N_COPY = N_DEV - 1


def _all_gather(arrs, name):
    n = len(arrs)

    def body(*refs):
        x_refs, out_refs = refs[:n], refs[n:2 * n]
        send_sems, recv_sems, local_sems = refs[2 * n:]
        x, y, c = _mesh_pos()
        me, sibling = (x, y, c), (x, y, 1 - c)
        chips = [(1 - x, y), (x, 1 - y), (1 - x, 1 - y)]

        def slot(a, px, py, pc):
            return out_refs[a].at[4 * px + 2 * py + pc]

        def copy(a, k, block, to, own=False):
            return pltpu.make_async_remote_copy(
                src_ref=x_refs[a] if own else slot(a, *block), dst_ref=slot(a, *block),
                send_sem=send_sems.at[a * N_COPY + k], recv_sem=recv_sems.at[a * N_COPY + k],
                device_id=to, device_id_type=MESH)

        mine = [pltpu.make_async_copy(x_refs[a], slot(a, *me), local_sems.at[a]) for a in range(n)]
        for cp in mine:
            cp.start()
        first = []
        for a in range(n):
            first.append(copy(a, 0, me, sibling, own=True))
            first += [copy(a, 1 + j, me, (*chip, c), own=True) for j, chip in enumerate(chips)]
        for cp in first:
            cp.start()
        passed = []
        for j, chip in enumerate(chips):
            for a in range(n):
                copy(a, 1 + j, (*chip, c), me).wait_recv()
                cp = copy(a, 4 + j, (*chip, c), sibling)
                cp.start()
                passed.append(cp)
        for a in range(n):
            copy(a, 0, sibling, me).wait_recv()
            for j, chip in enumerate(chips):
                copy(a, 4 + j, (*chip, 1 - c), me).wait_recv()
        for cp in first + passed:
            cp.wait_send()
        for cp in mine:
            cp.wait()

    anyspec = pl.BlockSpec(memory_space=pl.ANY)
    return pl.pallas_call(
        body, name=name, out_shape=[jax.ShapeDtypeStruct((N_DEV,) + a.shape, a.dtype) for a in arrs],
        in_specs=[anyspec] * n, out_specs=[anyspec] * n,
        scratch_shapes=[pltpu.SemaphoreType.DMA((n * N_COPY,)), pltpu.SemaphoreType.DMA((n * N_COPY,)),
                        pltpu.SemaphoreType.DMA((n,))],
    )(*arrs)


def _exchange(arrs, name):
    n = len(arrs)

    def body(*refs):
        s_refs, r_refs = refs[:n], refs[n:2 * n]
        send_sems, recv_sems, local_sems = refs[2 * n:]
        x, y, c = _mesh_pos()
        me = 4 * x + 2 * y + c
        local = [pltpu.make_async_copy(s_refs[a].at[me], r_refs[a].at[me], local_sems.at[a]) for a in range(n)]
        for cp in local:
            cp.start()
        sends, recvs = [], []
        for rel in range(1, N_DEV):
            px = 1 - x if rel & 4 else x
            py = 1 - y if rel & 2 else y
            pc = 1 - c if rel & 1 else c
            p = 4 * px + 2 * py + pc
            for a in range(n):
                k = a * N_COPY + rel - 1
                sends.append(pltpu.make_async_remote_copy(
                    src_ref=s_refs[a].at[p], dst_ref=r_refs[a].at[me], send_sem=send_sems.at[k],
                    recv_sem=recv_sems.at[k], device_id=(px, py, pc), device_id_type=MESH))
                recvs.append(pltpu.make_async_remote_copy(
                    src_ref=s_refs[a].at[me], dst_ref=r_refs[a].at[p], send_sem=send_sems.at[k],
                    recv_sem=recv_sems.at[k], device_id=(px, py, pc), device_id_type=MESH))
        for cp in sends:
            cp.start()
        for cp in recvs:
            cp.wait_recv()
        for cp in sends:
            cp.wait_send()
        for cp in local:
            cp.wait()

    anyspec = pl.BlockSpec(memory_space=pl.ANY)
    return pl.pallas_call(
        body, name=name, out_shape=[jax.ShapeDtypeStruct(a.shape, a.dtype) for a in arrs],
        in_specs=[anyspec] * n, out_specs=[anyspec] * n,
        scratch_shapes=[pltpu.SemaphoreType.DMA((n * N_COPY,)), pltpu.SemaphoreType.DMA((n * N_COPY,)),
                        pltpu.SemaphoreType.DMA((n,))],
    )(*arrs)


_HBM = pl.BlockSpec(memory_space=pltpu.HBM)
_SEM = pl.BlockSpec(memory_space=pltpu.SEMAPHORE)
_EFFECT = pltpu.SideEffectType.DATAFLOW_SIDE_EFFECTING


def _peers(x, y, c):
    out = []
    for rel in range(1, N_DEV):
        px = 1 - x if rel & 4 else x
        py = 1 - y if rel & 2 else y
        pc = 1 - c if rel & 1 else c
        out.append((rel, (px, py, pc), 4 * px + 2 * py + pc))
    return out


def _split_copies(src_refs, land_refs, send_sems, recv_sems, gather):
    x, y, c = _mesh_pos()
    me = 4 * x + 2 * y + c
    out = []
    for a, (src, land) in enumerate(zip(src_refs, land_refs)):
        for rel, peer, p in _peers(x, y, c):
            k = a * N_COPY + rel - 1
            mk = lambda s, d: pltpu.make_async_remote_copy(
                src_ref=s, dst_ref=d, send_sem=send_sems.at[k], recv_sem=recv_sems.at[k],
                device_id=peer, device_id_type=MESH)
            mine = src if gather else src.at[p]
            out.append((mk(mine, land.at[me]), mk(mine, land.at[p])))
    return out


def _copy_start(srcs, gather, name, collective_id):
    n = len(srcs)
    lands = [lax.empty(((N_DEV,) + s.shape) if gather else s.shape, s.dtype) for s in srcs]

    def body(*refs):
        src_refs, land_refs = refs[:n], refs[n:2 * n]
        send_sems, recv_sems = refs[2 * n], refs[2 * n + 1]
        token = refs[-1]
        x, y, c = _mesh_pos()
        barrier = pltpu.get_barrier_semaphore()
        for _, peer, _ in _peers(x, y, c):
            pl.semaphore_signal(barrier, inc=1, device_id=peer, device_id_type=MESH)
        pl.semaphore_wait(barrier, N_COPY)
        for out_copy, _ in _split_copies(src_refs, land_refs, send_sems, recv_sems, gather):
            out_copy.start()
        token[...] = jnp.zeros_like(token)

    hbm = lambda a: pltpu.HBM(a.shape, a.dtype)
    res = pl.pallas_call(
        body, name=name,
        out_shape=(pltpu.SemaphoreType.DMA((n * N_COPY,)), pltpu.SemaphoreType.DMA((n * N_COPY,)),
                   *[hbm(s) for s in srcs], *[hbm(z) for z in lands], jax.ShapeDtypeStruct((8, 128), F32)),
        in_specs=[_HBM] * (2 * n), out_specs=(_SEM, _SEM, *([_HBM] * (2 * n)), pl.BlockSpec(memory_space=pltpu.VMEM)),
        input_output_aliases={i: 2 + i for i in range(2 * n)},
        compiler_params=pltpu.CompilerParams(has_side_effects=_EFFECT, collective_id=collective_id),
    )(*[pltpu.with_memory_space_constraint(s, pltpu.HBM) for s in srcs],
      *[pltpu.with_memory_space_constraint(z, pltpu.HBM) for z in lands])
    return res[0], res[1], list(res[2:2 + n]), list(res[2 + n:2 + 2 * n]), res[-1]


def _copy_wait(send_sems, recv_sems, srcs, lands, after, gather, name):
    n = len(srcs)

    def body(*refs):
        src_refs, land_refs = refs[:n], refs[n:2 * n]
        s_sems, r_sems = refs[2 * n], refs[2 * n + 1]
        for out_copy, in_copy in _split_copies(src_refs, land_refs, s_sems, r_sems, gather):
            out_copy.wait_send()
            in_copy.wait_recv()

    hbm = lambda a: pltpu.HBM(a.shape, a.dtype)
    res = pl.pallas_call(
        body, name=name, out_shape=(*[hbm(s) for s in srcs], *[hbm(z) for z in lands]),
        in_specs=[_HBM] * (2 * n) + [_SEM, _SEM, pl.BlockSpec(memory_space=pl.ANY)], out_specs=tuple([_HBM] * (2 * n)),
        input_output_aliases={i: i for i in range(2 * n)},
        compiler_params=pltpu.CompilerParams(has_side_effects=_EFFECT),
    )(*srcs, *lands, send_sems, recv_sems, after)
    return list(res[:n]), list(res[n:])


def _sum_parts(parts, name):
    P, R, W = parts.shape

    def body(p_ref, o_ref):
        g = p_ref[0].astype(F32)
        for i in range(1, P):
            g = g + p_ref[i].astype(F32)
        o_ref[...] = g

    return pl.pallas_call(body, name=name, out_shape=jax.ShapeDtypeStruct((R, W), F32))(parts)


def _adamw_body(p_ref, w_ref, m_ref, v_ref, g_ref, d_ref, nm_ref, nv_ref):
    g = p_ref[0].astype(F32)
    for i in range(1, p_ref.shape[0]):
        g = g + p_ref[i].astype(F32)
    m_new = ADAM_B1 * m_ref[...] + (1.0 - ADAM_B1) * g
    v_new = ADAM_B2 * v_ref[...] + (1.0 - ADAM_B2) * jnp.square(g)
    m_hat = m_new / (1.0 - ADAM_B1 ** ADAM_STEP)
    v_hat = v_new / (1.0 - ADAM_B2 ** ADAM_STEP)
    g_ref[...] = g
    d_ref[...] = -ADAM_LR * (m_hat / (jnp.sqrt(v_hat) + ADAM_EPS) + ADAM_WD * w_ref[...])
    nm_ref[...] = m_new
    nv_ref[...] = v_new


def _adamw(parts, w, m, v, name):
    P, R, W = parts.shape
    tr = _pick(R, (368, 192, 64, 16, 8))
    spec = pl.BlockSpec((tr, W), lambda i: (i, 0))
    return pl.pallas_call(
        functools.partial(_adamw_body), name=name, grid=(R // tr,),
        in_specs=[pl.BlockSpec((P, tr, W), lambda i: (0, i, 0)), spec, spec, spec], out_specs=[spec] * 4,
        out_shape=[jax.ShapeDtypeStruct((R, W), F32)] * 4,
        compiler_params=_cparams(("parallel",)),
    )(parts, w, m, v)


def _adamw_layers(parts, w, m, v, name):
    P, B, C_ = parts[0].shape
    tb = _pick(B, (256, 128))
    nb = B // tb

    def body(*refs):
        p_refs, rest = refs[:DEPTH], refs[DEPTH:]
        a = pl.program_id(0)
        for l in range(DEPTH):
            @pl.when(a == l)
            def _():
                _adamw_body(p_refs[l], *[r.at[0] for r in rest])

    spec = pl.BlockSpec((1, tb, C_), lambda a, i: (a, i, 0))

    def part_spec(l):
        return pl.BlockSpec((P, tb, C_), lambda a, i: (0, jnp.where(a == l, i, jnp.where(a < l, 0, nb - 1)), 0))

    return pl.pallas_call(
        body, name=name, grid=(DEPTH, nb),
        in_specs=[part_spec(l) for l in range(DEPTH)] + [spec, spec, spec], out_specs=[spec] * 4,
        out_shape=[jax.ShapeDtypeStruct((DEPTH, B, C_), F32)] * 4,
        compiler_params=_cparams(("arbitrary", "arbitrary")),
    )(*parts, w, m, v)


PACK_W = 1024
BIG = (("w_in", (DEPTH, D_MODEL, N_IN // N_DEV), 2), ("w_conv_out", (DEPTH, CONV_DIM, D_MODEL // N_DEV), 2),
       ("w_uq", (DEPTH, Q_RANK, HEADS * QK_DIM // N_DEV), 2), ("w_ukv", (DEPTH, KV_RANK, HEADS * (NOPE + V_DIM) // N_DEV), 2),
       ("w_attn_out", (DEPTH, HEADS * V_DIM, D_MODEL // N_DEV), 2), ("w_hgrn_out", (DEPTH, 512, D_MODEL // N_DEV), 2),
       ("w_out", (DEPTH, D_MODEL // N_DEV, D_MODEL), 1), ("w_ff1", (DEPTH, D_MODEL, D_FF // N_DEV), 2),
       ("w_ff2", (DEPTH, D_FF // N_DEV, D_MODEL), 1))
SMALL_SHARDED = (("meta", (N_META, D_MODEL // N_DEV), 1), ("conv_w", (DEPTH, CONV_K, CONV_DIM // N_DEV), 2))
REPLICATED = (("norm1_g", (DEPTH, D_MODEL)), ("conv_b", (DEPTH, CONV_DIM)), ("conv_ln_g", (DEPTH, CONV_DIM)),
              ("conv_ln_b", (DEPTH, CONV_DIM)), ("q_a_norm_g", (DEPTH, Q_RANK)), ("kv_a_norm_g", (DEPTH, KV_RANK)),
              ("q_norm_g", (DEPTH, QK_DIM)), ("k_norm_g", (DEPTH, QK_DIM)), ("hgrn_lb_logits", (DEPTH, 512)),
              ("hgrn_norm_g", (DEPTH, 512)), ("norm2_g", (DEPTH, D_MODEL)))
WEIGHT_ORDER = ("meta", "norm1_g", "w_in", "conv_w", "conv_b", "conv_ln_g", "conv_ln_b", "w_conv_out", "q_a_norm_g", "w_uq",
                "kv_a_norm_g", "w_ukv", "q_norm_g", "k_norm_g", "w_attn_out", "hgrn_lb_logits", "hgrn_norm_g", "w_hgrn_out",
                "w_out", "norm2_g", "w_ff1", "w_ff2")


def _rows_for(n_elems, mult):
    rows = -(-n_elems // PACK_W)
    return -(-rows // mult) * mult


def _pack(arrays, dtype, mult, lead=()):
    nl = len(lead)
    flat = jnp.concatenate([a.reshape(lead + (-1,)).astype(dtype) for a in arrays], axis=nl)
    rows = _rows_for(flat.shape[nl], mult)
    flat = jnp.pad(flat, [(0, 0)] * nl + [(0, rows * PACK_W - flat.shape[nl])])
    return flat.reshape(lead + (rows, PACK_W))


def _unpack(pack, shapes, lead=()):
    nl = len(lead)
    flat = pack.reshape(lead + (-1,))
    out, off = [], 0
    for shp in shapes:
        n = int(np.prod(shp))
        out.append(lax.slice_in_dim(flat, off, off + n, axis=nl).reshape(lead + tuple(shp)))
        off += n
    return out


def _join_shards(g, axis):
    g = jnp.moveaxis(g, 0, axis)
    shp = g.shape
    return g.reshape(shp[:axis] + (shp[axis] * shp[axis + 1],) + shp[axis + 2:])


def _cut_shards(a, axis):
    shp = a.shape
    a = a.reshape(shp[:axis] + (N_DEV, shp[axis] // N_DEV) + shp[axis + 1:])
    return jnp.moveaxis(a, axis, 0)


def kernel(x, meta, norm1_g, w_in, conv_w, conv_b, conv_ln_g, conv_ln_b, w_conv_out, q_a_norm_g, w_uq, kv_a_norm_g, w_ukv, q_norm_g, k_norm_g, w_attn_out, hgrn_lb_logits, hgrn_norm_g, w_hgrn_out, w_out, norm2_g, w_ff1, w_ff2, loss_target, m_meta, m_norm1_g, m_w_in, m_conv_w, m_conv_b, m_conv_ln_g, m_conv_ln_b, m_w_conv_out, m_q_a_norm_g, m_w_uq, m_kv_a_norm_g, m_w_ukv, m_q_norm_g, m_k_norm_g, m_w_attn_out, m_hgrn_lb_logits, m_hgrn_norm_g, m_w_hgrn_out, m_w_out, m_norm2_g, m_w_ff1, m_w_ff2, v_meta, v_norm1_g, v_w_in, v_conv_w, v_conv_b, v_conv_ln_g, v_conv_ln_b, v_w_conv_out, v_q_a_norm_g, v_w_uq, v_kv_a_norm_g, v_w_ukv, v_q_norm_g, v_k_norm_g, v_w_attn_out, v_hgrn_lb_logits, v_hgrn_norm_g, v_w_hgrn_out, v_w_out, v_norm2_g, v_w_ff1, v_w_ff2):
    args = dict(locals())
    wts = {n: args[n] for n in WEIGHT_ORDER}
    mom = {n: args["m_" + n] for n in WEIGHT_ORDER}
    var = {n: args["v_" + n] for n in WEIGHT_ORDER}
    xi, yi, ci = _mesh_pos()
    me = 4 * xi + 2 * yi + ci

    shard = lambda l: [wts[n][l].astype(BF16) for n, _, _ in BIG]
    gathered = _all_gather(shard(0) + [_pack([wts[n] for n, _, _ in SMALL_SHARDED], F32, 8)], "gather_layer0")
    small = dict(zip([n for n, _, _ in SMALL_SHARDED],
                     [_join_shards(g, axis) for (_, _, axis), g in
                      zip(SMALL_SHARDED, _unpack(gathered[-1], [s for _, s, _ in SMALL_SHARDED], (N_DEV,)))]))
    pending = _copy_start(shard(1), True, "gather_layer1_start", 5)

    def layer_weights(l, xp):
        if l == 0:
            mats = gathered[:-1]
            xp = xp + pending[4][0, 0]
        else:
            own, lands = _copy_wait(pending[0], pending[1], pending[2], pending[3], xp, True, "gather_layer1_wait")
            mats = [lax.dynamic_update_index_in_dim(z, s, me, 0) for z, s in zip(lands, own)]
        full = {n: wts[n][l] for n, _ in REPLICATED}
        full["conv_w"] = small["conv_w"][l]
        for (n, _, axis), g in zip(BIG, mats):
            full[n] = _join_shards(g, axis - 1)
        return full, xp

    big_names = [n for n, _, _ in BIG]
    early = [n for n in big_names if n in ("w_out", "w_ff1", "w_ff2")]
    late = [n for n in big_names if n not in early]
    cut = lambda g, names: [_cut_shards(g[n], axis - 1).astype(BF16) for n, _, axis in BIG if n in names]
    layer_grads = [None] * DEPTH
    flight = {}

    def layer_mid(l, g, dx1):
        if l == 0:
            flight["l0_early"] = _copy_start(cut(g, early), False, "scatter_layer0_early_start", 7)
            dx1 = dx1 + flight["l0_early"][4][0, 0]
        return dx1

    def layer_done(l, g, dx):
        layer_grads[l] = g
        if l == 1:
            flight["l1"] = _copy_start(cut(g, big_names), False, "scatter_l1_start", 6)
            dx = dx + flight["l1"][4][0, 0]
        return dx

    loss, grad_x, g_meta, g_lb = _run_step(x[0], loss_target[0], small["meta"], wts["hgrn_lb_logits"],
                                           layer_weights, layer_done, layer_mid)

    grads = {k: jnp.stack([layer_grads[l][k] for l in range(DEPTH)]) for k in layer_grads[0] if k not in big_names}
    grads["hgrn_lb_logits"] = g_lb
    grads["meta"] = g_meta
    small_names = [n for n, _ in REPLICATED] + [n for n, _, _ in SMALL_SHARDED]
    part = _pack([grads[n] for n in small_names] + [loss.reshape(1)], F32, 8)
    flight["small"] = _copy_start([part], True, "gather_small_grads_start", 9)
    flight["l0_late"] = _copy_start(cut(layer_grads[0], late), False, "scatter_l0_late_start", 8)
    started = flight["l0_late"][4]

    def arrive(key, names, after):
        s_sems, r_sems, sent, lands, _ = flight[key]
        sent, lands = _copy_wait(s_sems, r_sems, sent, lands, after, False, f"scatter_{key}_wait")
        return {n: lax.dynamic_update_index_in_dim(z, lax.dynamic_index_in_dim(s, me, 0, keepdims=False), me, 0)
                for n, z, s in zip(names, lands, sent)}

    out = {}

    def update(names, recv0, recv1):
        for n in names:
            res4 = _adamw_layers([recv0[n], recv1[n]], wts[n], mom[n], var[n], "adamw_" + n)
            for kind, a in zip(("grad_", "delta_", "new_m_", "new_v_"), res4):
                out[kind + n] = a

    recv1 = arrive("l1", big_names, started)
    recv0 = arrive("l0_early", early, started)
    update(early, recv0, recv1)

    s_sems, r_sems, sent, lands, _ = flight["small"]
    sent, lands = _copy_wait(s_sems, r_sems, sent, lands, out["grad_" + early[-1]], True, "gather_small_grads_wait")
    total = _sum_parts(lax.dynamic_update_index_in_dim(lands[0], sent[0], me, 0), "sum_small_grads")
    sizes = [grads[n].shape for n in small_names] + [(1,)]
    tot = dict(zip(small_names + ["loss"], _unpack(total, sizes)))
    loss = tot.pop("loss").reshape(())
    mine = {n: tot[n] for n, _ in REPLICATED}
    for n, shp, axis in SMALL_SHARDED:
        mine[n] = lax.dynamic_slice_in_dim(tot[n], me * shp[axis], shp[axis], axis=axis)
    pk = lambda d: _pack([d[n] for n in small_names], F32, 8)
    small_out = _adamw(pk(mine)[None], pk(wts), pk(mom), pk(var), "adamw_vectors")
    for kind, pack in zip(("grad_", "delta_", "new_m_", "new_v_"), small_out):
        for n, a in zip(small_names, _unpack(pack, [wts[n].shape for n in small_names])):
            out[kind + n] = a

    recv0 = arrive("l0_late", late, small_out[0])
    update(late, recv0, recv1)

    res = [loss, grad_x[None]]
    for kind in ("grad_", "delta_", "new_m_", "new_v_"):
        res += [out[kind + n] for n in WEIGHT_ORDER]
    return tuple(res)
```

```python
import functools

import numpy as np
import jax
import jax.numpy as jnp
from jax import lax
from jax.experimental import pallas as pl
from jax.experimental.pallas import tpu as pltpu

F32 = jnp.float32
BF16 = jnp.bfloat16

D_MODEL = 1024
DEPTH = 2
N_META = 16
PAD_FRONT = 112
ROW0 = PAD_FRONT + N_META
EPS = 1e-6
GATE_CLAMP = 1.0 - 1e-6
CONV_DIM = 512
CONV_K = 31
HEADS = 8
Q_RANK = 256
KV_RANK = 128
NOPE = 64
ROPE = 32
V_DIM = 64
QK_DIM = NOPE + ROPE
HEAD_W = 128
ROPE_BASE = 10000.0
HG_HEADS = 4
HG_DK = 128
HG_DV = 128
HG_CHUNK = 64
D_FF = 4096
N_IN = 6560
C_CONV_A, C_CONV_G, C_GATE, C_CQ, C_CKV, C_KR, C_HQ, C_HF, C_HI, C_HG = (
    0, 512, 1024, 4096, 4352, 4480, 4608, 5120, 5632, 6144)
N_IN_P = 6656
O_CQ, O_KR, O_HQ, O_GATE = 1024, 1408, 1440, 3488
KR_LANE = NOPE

ADAM_LR = 0.001
ADAM_B1 = 0.9
ADAM_B2 = 0.999
ADAM_EPS = 1e-08
ADAM_WD = 0.01
ADAM_STEP = 10

N_DEV = 8
VMEM_LIMIT = 56 * 1024 * 1024
MESH = pl.DeviceIdType.MESH


def _pick(n, cands):
    for c in cands:
        if n % c == 0:
            return c
    raise ValueError(f"no tile for {n}")


def _cparams(sem, **kw):
    return pltpu.CompilerParams(dimension_semantics=sem, vmem_limit_bytes=VMEM_LIMIT, **kw)


def _relu2(v):
    return jnp.square(jnp.maximum(v, 0.0))


def _mm(a, b, *, ta=False, tb=False, out_dtype=F32, res=None, a_fn=None, epi=None, name):
    M, K = (a.shape[1], a.shape[0]) if ta else a.shape
    N = b.shape[0] if tb else b.shape[1]
    assert (b.shape[1] if tb else b.shape[0]) == K, (a.shape, b.shape, ta, tb)
    tm = _pick(M, (1056, 1024, 512, 384, 256, 128, 96))
    tn = _pick(N, (1664, 1024, 512, 384, 256, 128))
    tk = _pick(K, (1664, 1056, 1024, 512, 384, 256, 128, 96))
    nk = K // tk
    dims = (((0 if ta else 1,), (1 if tb else 0,)), ((), ()))
    extras = ([res] if res is not None else []) + ([epi[0]] if epi is not None else [])

    def body(*refs):
        a_ref, b_ref = refs[0], refs[1]
        r_ref = refs[2] if res is not None else None
        e_ref = refs[2 + (res is not None)] if epi is not None else None
        o_ref = refs[2 + len(extras)]
        acc = refs[-1] if nk > 1 else None
        k = pl.program_id(2)
        av = a_ref[...]
        if a_fn is not None:
            av = a_fn(av.astype(F32))
        p = lax.dot_general(av.astype(BF16), b_ref[...].astype(BF16), dims, preferred_element_type=F32)

        def finish(total):
            if e_ref is not None:
                total = epi[1](total, e_ref[...].astype(F32))
            if r_ref is not None:
                total = total + r_ref[...].astype(F32)
            o_ref[...] = total.astype(o_ref.dtype)

        if nk == 1:
            finish(p)
        else:
            @pl.when(k == 0)
            def _():
                acc[...] = p

            @pl.when(k > 0)
            def _():
                acc[...] += p

            @pl.when(k == nk - 1)
            def _():
                finish(acc[...])

    a_spec = pl.BlockSpec((tk, tm), lambda i, j, k: (k, i)) if ta else pl.BlockSpec((tm, tk), lambda i, j, k: (i, k))
    b_spec = pl.BlockSpec((tn, tk), lambda i, j, k: (j, k)) if tb else pl.BlockSpec((tk, tn), lambda i, j, k: (k, j))
    o_spec = pl.BlockSpec((tm, tn), lambda i, j, k: (i, j))
    in_specs = [a_spec, b_spec] + [o_spec] * len(extras)
    args = (a, b) + tuple(extras)
    return pl.pallas_call(
        body, name=name, grid=(M // tm, N // tn, nk), in_specs=in_specs, out_specs=o_spec,
        out_shape=jax.ShapeDtypeStruct((M, N), out_dtype),
        scratch_shapes=[pltpu.VMEM((tm, tn), F32)] if nk > 1 else [],
        compiler_params=_cparams(("parallel", "parallel", "arbitrary")),
    )(*args)


class Row:
    def __init__(self, arr, width=None, col=0, piece=None):
        self.arr = arr
        self.width = arr.shape[1] if width is None else width
        assert col % self.width == 0
        self.blk = col // self.width
        self.piece = self.width if piece is None else piece

    def spec(self, tm):
        blk = self.blk
        return pl.BlockSpec((tm, self.width), lambda i: (i, blk))


def _split(v, piece):
    w = v.shape[-1]
    if piece == w:
        return v
    return [v[:, j * piece:(j + 1) * piece] for j in range(w // piece)]


def _store(ref, val, dtype=None):
    if isinstance(val, (list, tuple)):
        piece = val[0].shape[-1]
        for j, p in enumerate(val):
            ref[:, j * piece:(j + 1) * piece] = p.astype(ref.dtype)
    else:
        ref[...] = val.astype(ref.dtype)


def _row_tile(T):
    return _pick(T, (384, 352, 192, 128))


def _param2d(p):
    return p.reshape(1, -1).astype(F32)


def _rowwise(fn, T, rows, params, outs, name):
    tm = _row_tile(T)
    nr, npar = len(rows), len(params)
    par = [(_param2d(p), piece) for p, piece in params]

    def body(*refs):
        rid = pl.program_id(0) * tm + lax.broadcasted_iota(jnp.int32, (tm, 1), 0)
        rv = [_split(refs[n][...].astype(F32), rows[n].piece) for n in range(nr)]
        pv = [_split(refs[nr + n][...], par[n][1]) for n in range(npar)]
        res = fn(rid, rv, pv)
        for n, val in enumerate(res):
            _store(refs[nr + npar + n], val)

    return pl.pallas_call(
        body, name=name, grid=(T // tm,),
        in_specs=[r.spec(tm) for r in rows] + [pl.BlockSpec(p.shape, lambda i: (0, 0)) for p, _ in par],
        out_specs=[pl.BlockSpec((tm, w), lambda i: (i, 0)) for w, _ in outs],
        out_shape=[jax.ShapeDtypeStruct((T, w), dt) for w, dt in outs],
        compiler_params=_cparams(("parallel",)),
    )(*[r.arr for r in rows], *[p for p, _ in par])


def _rowwise_bwd(fn, T, rows, params, cts, drow, name, add=None):
    tm = _row_tile(T)
    nr, npar, nct = len(rows), len(params), len(cts)
    par = [(_param2d(p), piece) for p, piece in params]
    didx = sorted(drow)
    has_add = add is not None

    def body(*refs):
        i = pl.program_id(0)
        rid = i * tm + lax.broadcasted_iota(jnp.int32, (tm, 1), 0)
        rv = [_split(refs[n][...].astype(F32), rows[n].piece) for n in range(nr)]
        pv = [_split(refs[nr + n][...], par[n][1]) for n in range(npar)]
        cv = [_split(refs[nr + npar + n][...].astype(F32), cts[n].piece) for n in range(nct)]
        base = nr + npar + nct + (1 if has_add else 0)
        d_refs = refs[base:base + len(didx)]
        p_refs = refs[base + len(didx):]

        def g(dvals, pvals):
            full = list(rv)
            for n, v in zip(didx, dvals):
                full[n] = v
            return fn(rid, full, pvals)

        _, vjp = jax.vjp(g, [rv[n] for n in didx], pv)
        d_rows, d_pars = vjp(cv)
        for slot, n in enumerate(didx):
            val = d_rows[slot]
            if has_add and add[0] == n:
                assert not isinstance(val, (list, tuple))
                val = val + refs[nr + npar + nct][...].astype(F32)
            _store(d_refs[slot], val)

        @pl.when(i == 0)
        def _():
            for r in p_refs:
                r[...] = jnp.zeros_like(r)

        for r, val in zip(p_refs, d_pars):
            if isinstance(val, (list, tuple)):
                piece = val[0].shape[-1]
                for j, p in enumerate(val):
                    r[:, j * piece:(j + 1) * piece] += p
            else:
                r[...] += val

    in_specs = ([r.spec(tm) for r in rows] + [pl.BlockSpec(p.shape, lambda i: (0, 0)) for p, _ in par]
                + [c.spec(tm) for c in cts])
    args = [r.arr for r in rows] + [p for p, _ in par] + [c.arr for c in cts]
    if has_add:
        in_specs.append(pl.BlockSpec((tm, rows[add[0]].width), lambda i: (i, 0)))
        args.append(add[1])
    out_specs = ([pl.BlockSpec((tm, rows[n].width), lambda i: (i, 0)) for n in didx]
                 + [pl.BlockSpec(p.shape, lambda i: (0, 0)) for p, _ in par])
    out_shape = ([jax.ShapeDtypeStruct((T, rows[n].width), drow[n]) for n in didx]
                 + [jax.ShapeDtypeStruct(p.shape, F32) for p, _ in par])
    res = pl.pallas_call(
        body, name=name, grid=(T // tm,), in_specs=in_specs, out_specs=out_specs, out_shape=out_shape,
        compiler_params=_cparams(("arbitrary",)),
    )(*args)
    return list(res[:len(didx)]), list(res[len(didx):])


def _f_rms(rid, rv, pv):
    x, g = rv[0], pv[0]
    return [x * lax.rsqrt(jnp.mean(x * x, axis=-1, keepdims=True) + EPS) * g]


def _f_glu(rid, rv, pv):
    a, gt = rv
    return [a * jax.nn.sigmoid(gt) * (rid >= PAD_FRONT).astype(F32)]


def _f_lnsilu(rid, rv, pv):
    x = rv[0]
    g, b = pv
    mu = jnp.mean(x, axis=-1, keepdims=True)
    xc = x - mu
    y = xc * lax.rsqrt(jnp.mean(xc * xc, axis=-1, keepdims=True) + EPS) * g + b
    return [y * jax.nn.sigmoid(y)]


@functools.partial(jax.custom_vjp, nondiff_argnums=(1,))
def _lane_roll(x, shift):
    return pltpu.roll(x, shift, 1)


def _lane_roll_fwd(x, shift):
    return pltpu.roll(x, shift, 1), None


def _lane_roll_bwd(shift, _, g):
    return (pltpu.roll(g, (HEAD_W - shift) % HEAD_W, 1),)


_lane_roll.defvjp(_lane_roll_fwd, _lane_roll_bwd)


def _head_norm_rope(xh, g, c, s1, s2):
    y = xh * lax.rsqrt(jnp.sum(xh * xh, axis=-1, keepdims=True) * (1.0 / QK_DIM) + EPS) * g
    half = ROPE // 2
    return y * c + _lane_roll(y, HEAD_W - half) * s1 + _lane_roll(y, half) * s2


def _f_qrope(rid, rv, pv):
    q, c, s1, s2 = rv
    return [[_head_norm_rope(qh, pv[0], c, s1, s2) * ATT_SCALE for qh in q]]


def _f_krope(rid, rv, pv):
    k, kr, c, s1, s2 = rv
    return [[_head_norm_rope(kh + kr, pv[0], c, s1, s2) for kh in k]]


def _f_hgrn_prep(rid, rv, pv):
    hf, hi = rv
    m = (rid >= PAD_FRONT).astype(F32)
    kk = (1.0 - pv[0]) * jax.nn.sigmoid(-hf) * m
    lf = jnp.log1p(-jnp.minimum(kk, GATE_CLAMP))
    vv = hi * jax.nn.sigmoid(hi) * m
    return [kk, lf, vv]


def _f_hgrn_out(rid, rv, pv):
    o, hg = rv
    ng = pv[0]
    out = []
    for oh, gh, nh in zip(o, hg, ng):
        y = oh * lax.rsqrt(jnp.mean(oh * oh, axis=-1, keepdims=True) + EPS) * nh
        out.append(y * (gh * jax.nn.sigmoid(gh)))
    return [out]


def _f_mix(rid, rv, pv):
    g0, g1, g2, ya, yb, yc = rv
    return [jax.nn.sigmoid(g0) * ya + jax.nn.sigmoid(g1) * yb + jax.nn.sigmoid(g2) * yc]


def _f_relu2(rid, rv, pv):
    return [jnp.square(jax.nn.relu(rv[0]))]


def _loss_head(x2, tgt, T):
    tm = _row_tile(T)

    def body(x_ref, t_ref, dx_ref, l_ref):
        i = pl.program_id(0)
        rid = i * tm + lax.broadcasted_iota(jnp.int32, (tm, 1), 0)
        diff = (x_ref[...] - t_ref[...]) * (rid >= ROW0).astype(F32)
        dx_ref[...] = diff * (1.0 / D_MODEL)

        @pl.when(i == 0)
        def _():
            l_ref[...] = jnp.zeros_like(l_ref)

        l_ref[...] += jnp.sum(diff * diff, axis=0, keepdims=True)

    spec = pl.BlockSpec((tm, D_MODEL), lambda i: (i, 0))
    return pl.pallas_call(
        body, name="loss_head", grid=(T // tm,), in_specs=[spec, spec],
        out_specs=[spec, pl.BlockSpec((1, D_MODEL), lambda i: (0, 0))],
        out_shape=[jax.ShapeDtypeStruct((T, D_MODEL), F32), jax.ShapeDtypeStruct((1, D_MODEL), F32)],
        compiler_params=_cparams(("arbitrary",)),
    )(x2, tgt)


HALO = 32


CONV_ROWS = 64


def _conv_lanes():
    return [slice(c, c + 128) for c in range(0, CONV_DIM, 128)]


def _conv_tile(T):
    return _pick(T, (384, 128))


def _conv_fwd(h, w, b, T, name):
    tr = _conv_tile(T)
    ratio = tr // HALO
    wp = jnp.zeros((HALO, CONV_DIM), F32).at[:CONV_K].set(w)

    def body(m_ref, h_ref, w_ref, b_ref, o_ref, win):
        i = pl.program_id(0)
        win[0:HALO, :] = h_ref[...] * (i > 0).astype(F32)
        win[HALO:, :] = m_ref[...]
        for cs in _conv_lanes():
            wv, bv = w_ref[:, cs], b_ref[:, cs]
            for r0 in range(0, tr, CONV_ROWS):
                acc = jnp.broadcast_to(bv, (CONV_ROWS, 128))
                for k in range(CONV_K):
                    acc = acc + wv[k:k + 1] * win[pl.ds(HALO - (CONV_K - 1) + k + r0, CONV_ROWS), cs]
                o_ref[r0:r0 + CONV_ROWS, cs] = acc

    return pl.pallas_call(
        body, name=name, grid=(T // tr,),
        in_specs=[pl.BlockSpec((tr, CONV_DIM), lambda i: (i, 0)),
                  pl.BlockSpec((HALO, CONV_DIM), lambda i: (jnp.maximum(i * ratio - 1, 0), 0)),
                  pl.BlockSpec((HALO, CONV_DIM), lambda i: (0, 0)),
                  pl.BlockSpec((1, CONV_DIM), lambda i: (0, 0))],
        out_specs=pl.BlockSpec((tr, CONV_DIM), lambda i: (i, 0)),
        out_shape=jax.ShapeDtypeStruct((T, CONV_DIM), F32),
        scratch_shapes=[pltpu.VMEM((tr + HALO, CONV_DIM), F32)],
        compiler_params=_cparams(("parallel",)),
    )(h, h, wp, _param2d(b))


def _conv_bwd(h, w, dy, T, name):
    tr = _conv_tile(T)
    ratio = tr // HALO
    n_t = T // tr
    last_halo = T // HALO - 1
    wp = jnp.zeros((HALO, CONV_DIM), F32).at[:CONV_K].set(w)

    def body(hm_ref, hh_ref, dm_ref, dh_ref, w_ref, dx_ref, dw_ref, db_ref, hwin, dwin):
        i = pl.program_id(0)
        hwin[0:HALO, :] = hh_ref[...] * (i > 0).astype(F32)
        hwin[HALO:, :] = hm_ref[...]
        dwin[0:tr, :] = dm_ref[...]
        dwin[tr:, :] = dh_ref[...] * (i < n_t - 1).astype(F32)

        @pl.when(i == 0)
        def _():
            dw_ref[...] = jnp.zeros_like(dw_ref)
            db_ref[...] = jnp.zeros_like(db_ref)

        db_ref[...] += jnp.sum(dm_ref[...], axis=0, keepdims=True)
        fold = lambda a: functools.reduce(jnp.add, [a[r:r + 8] for r in range(0, CONV_ROWS, 8)])
        for cs in _conv_lanes():
            wv = w_ref[:, cs]
            dws = [jnp.zeros((8, 128), F32) for _ in range(CONV_K)]
            for r0 in range(0, tr, CONV_ROWS):
                acc = jnp.zeros((CONV_ROWS, 128), F32)
                for k in range(CONV_K):
                    acc = acc + wv[k:k + 1] * dwin[pl.ds(CONV_K - 1 - k + r0, CONV_ROWS), cs]
                dx_ref[r0:r0 + CONV_ROWS, cs] = acc
                dy_t = dm_ref[r0:r0 + CONV_ROWS, cs]
                for k in range(CONV_K):
                    dws[k] = dws[k] + fold(dy_t * hwin[pl.ds(HALO - (CONV_K - 1) + k + r0, CONV_ROWS), cs])
            for k in range(CONV_K):
                dw_ref[k:k + 1, cs] += jnp.sum(dws[k], axis=0, keepdims=True)

    main = pl.BlockSpec((tr, CONV_DIM), lambda i: (i, 0))
    return pl.pallas_call(
        body, name=name, grid=(n_t,),
        in_specs=[main,
                  pl.BlockSpec((HALO, CONV_DIM), lambda i: (jnp.maximum(i * ratio - 1, 0), 0)),
                  main,
                  pl.BlockSpec((HALO, CONV_DIM), lambda i: (jnp.minimum((i + 1) * ratio, last_halo), 0)),
                  pl.BlockSpec((HALO, CONV_DIM), lambda i: (0, 0))],
        out_specs=[main, pl.BlockSpec((HALO, CONV_DIM), lambda i: (0, 0)), pl.BlockSpec((1, CONV_DIM), lambda i: (0, 0))],
        out_shape=[jax.ShapeDtypeStruct((T, CONV_DIM), F32), jax.ShapeDtypeStruct((HALO, CONV_DIM), F32),
                   jax.ShapeDtypeStruct((1, CONV_DIM), F32)],
        scratch_shapes=[pltpu.VMEM((tr + HALO, CONV_DIM), F32), pltpu.VMEM((tr + HALO, CONV_DIM), F32)],
        compiler_params=_cparams(("arbitrary",)),
    )(h, h, dy, dy, wp)


NEG = -1e30
ATT_SCALE = QK_DIM ** -0.5
_NT = (((1,), (1,)), ((), ()))
_TN = (((0,), (0,)), ((), ()))


def _att_blk(T):
    return _pick(T, (384, 128))


def _att_mask(i, j, blk):
    kpos = j * blk + lax.broadcasted_iota(jnp.int32, (blk, blk), 0)
    qpos = i * blk + lax.broadcasted_iota(jnp.int32, (blk, blk), 1)
    return (kpos <= qpos) & (kpos >= PAD_FRONT)


def _t32(a):
    return a.astype(F32).T.astype(BF16)


def _attn_fwd(q, k, v, T, name):
    blk = _att_blk(T)
    nq = T // blk

    def body(q_ref, k_ref, v_ref, o_ref, lse_ref, vt):
        i = pl.program_id(1)

        @pl.when(i == 0)
        def _():
            def tr(j, c):
                vt[j] = _t32(v_ref[pl.ds(pl.multiple_of(j * blk, blk), blk), :])
                return c

            lax.fori_loop(0, nq, tr, 0)

        qb = q_ref[...]

        def step(js, carry, masked):
            m, l, acc = carry
            ss = []
            for j in js:
                kb = k_ref[pl.ds(pl.multiple_of(j * blk, blk), blk), :]
                s = lax.dot_general(kb, qb, _NT, preferred_element_type=F32)
                ss.append(jnp.where(_att_mask(i, j, blk), s, NEG) if masked else s)
            m_new = m
            for s in ss:
                m_new = jnp.maximum(m_new, jnp.max(s, axis=0, keepdims=True))
            alpha = jnp.exp(m - m_new)
            l = alpha * l
            acc = alpha * acc
            for j, s in zip(js, ss):
                p = jnp.exp(s - m_new)
                l = l + jnp.sum(p, axis=0, keepdims=True)
                acc = acc + jnp.dot(vt[j], p.astype(BF16), preferred_element_type=F32)
            return m_new, l, acc

        init = (jnp.full((1, blk), NEG, F32), jnp.zeros((1, blk), F32), jnp.zeros((HEAD_W, blk), F32))
        later = jnp.minimum(i, 1)
        carry = lax.fori_loop(0, 1 - later, lambda t, c: step([i], c, True), init)
        carry = lax.fori_loop(0, later, lambda t, c: step([i, 0], c, True), carry)
        n_free = jnp.maximum(i - 1, 0)
        carry = lax.fori_loop(0, n_free // 2, lambda t, c: step([1 + 2 * t, 2 + 2 * t], c, False), carry)
        m, l, acc = lax.fori_loop(0, n_free % 2, lambda t, c: step([i - 1], c, False), carry)
        o_ref[...] = (acc / l).T.astype(o_ref.dtype)
        lse_ref[0, 0] = m + jnp.log(l)

    full = pl.BlockSpec((T, HEAD_W), lambda h, i: (0, h))
    return pl.pallas_call(
        body, name=name, grid=(HEADS, nq),
        in_specs=[pl.BlockSpec((blk, HEAD_W), lambda h, i: (i, h)), full, full],
        out_specs=[pl.BlockSpec((blk, HEAD_W), lambda h, i: (i, h)),
                   pl.BlockSpec((1, 1, 1, blk), lambda h, i: (h, i, 0, 0))],
        out_shape=[jax.ShapeDtypeStruct((T, HEADS * HEAD_W), BF16), jax.ShapeDtypeStruct((HEADS, nq, 1, blk), F32)],
        scratch_shapes=[pltpu.VMEM((nq, HEAD_W, blk), BF16)],
        compiler_params=_cparams(("parallel", "arbitrary")),
    )(q, k, v)


def _attn_bwd(q, k, v, o, lse, do, T, name):
    blk = _att_blk(T)
    nq = T // blk

    def body(q_ref, k_ref, v_ref, o_ref, lse_ref, do_ref, dq_ref, dk_ref, dv_ref, delta, dqt, dk_acc, dv_acc):
        j = pl.program_id(1)

        @pl.when(j == 0)
        def _():
            dqt[...] = jnp.zeros_like(dqt)

            def dstep(i, c):
                r0 = pl.multiple_of(i * blk, blk)
                prod = do_ref[pl.ds(r0, blk), :].astype(F32) * o_ref[pl.ds(r0, blk), :].astype(F32)
                delta[i] = jnp.sum(prod.T, axis=0, keepdims=True)
                return c

            lax.fori_loop(0, nq, dstep, 0)

        kb = k_ref[...]
        vb = v_ref[...]
        kbt = _t32(kb)
        dk_acc[...] = jnp.zeros_like(dk_acc)
        dv_acc[...] = jnp.zeros_like(dv_acc)

        def step(qs, masked):
            dvs, dks = [], []
            for i in qs:
                r0 = pl.multiple_of(i * blk, blk)
                qb = q_ref[pl.ds(r0, blk), :]
                dob = do_ref[pl.ds(r0, blk), :]
                s = lax.dot_general(kb, qb, _NT, preferred_element_type=F32)
                p = jnp.exp(s - lse_ref[0, i])
                if masked:
                    p = jnp.where(_att_mask(i, j, blk), p, 0.0)
                dvs.append(jnp.dot(p.astype(BF16), dob, preferred_element_type=F32))
                dp = lax.dot_general(vb, dob, _NT, preferred_element_type=F32)
                ds = (p * (dp - delta[i])).astype(BF16)
                dks.append(jnp.dot(ds, qb, preferred_element_type=F32))
                dqt[i] += jnp.dot(kbt, ds, preferred_element_type=F32)
            dv_acc[...] += functools.reduce(jnp.add, dvs)
            dk_acc[...] += functools.reduce(jnp.add, dks)

        def loop(lo, masked):
            n = nq - lo

            def pair(t, c):
                step([lo + 2 * t, lo + 2 * t + 1], masked)
                return c

            def last(t, c):
                step([nq - 1], masked)
                return c

            lax.fori_loop(0, n // 2, pair, 0)
            lax.fori_loop(0, n % 2, last, 0)

        @pl.when(j == 0)
        def _():
            loop(0, True)

        @pl.when(j > 0)
        def _():
            step([j], True)
            loop(j + 1, False)

        dk_ref[...] = dk_acc[...].astype(dk_ref.dtype)
        dv_ref[...] = dv_acc[...].astype(dv_ref.dtype)

        @pl.when(j == nq - 1)
        def _():
            def wstep(i, c):
                dq_ref[pl.ds(pl.multiple_of(i * blk, blk), blk), :] = dqt[i].T
                return c

            lax.fori_loop(0, nq, wstep, 0)

    full = pl.BlockSpec((T, HEAD_W), lambda h, j: (0, h))
    kblk = pl.BlockSpec((blk, HEAD_W), lambda h, j: (j, h))
    wide = (T, HEADS * HEAD_W)
    return pl.pallas_call(
        body, name=name, grid=(HEADS, nq),
        in_specs=[full, kblk, kblk, full, pl.BlockSpec((1, nq, 1, blk), lambda h, j: (h, 0, 0, 0)), full],
        out_specs=[full, kblk, kblk],
        out_shape=[jax.ShapeDtypeStruct(wide, F32), jax.ShapeDtypeStruct(wide, BF16), jax.ShapeDtypeStruct(wide, BF16)],
        scratch_shapes=[pltpu.VMEM((nq, 1, blk), F32), pltpu.VMEM((nq, HEAD_W, blk), F32),
                        pltpu.VMEM((blk, HEAD_W), F32), pltpu.VMEM((blk, HEAD_W), F32)],
        compiler_params=_cparams(("parallel", "arbitrary")),
    )(q, k, v, o, lse, do)


HG_NB = 6
C = HG_CHUNK
_HI = lax.Precision.HIGHEST


def _tri(lower):
    r = lax.broadcasted_iota(jnp.int32, (C, C), 0)
    c = lax.broadcasted_iota(jnp.int32, (C, C), 1)
    return ((c <= r) if lower else (c >= r)).astype(F32)


HG_SUB = 8
N_SUB = C // HG_SUB


def _hg_split_decay(b, I, rid):
    lo = I * HG_SUB
    r = b[lo:lo + 1]
    eq = jnp.exp(b[lo:lo + HG_SUB] - r)
    ek = jnp.where(rid < lo, jnp.exp(jnp.minimum(r - b, 0.0)), 0.0)
    return eq, ek


def _hg_intra_fwd(q, k, v, b):
    rid = lax.broadcasted_iota(jnp.int32, (C, 1), 0)
    tid = lax.broadcasted_iota(jnp.int32, (HG_SUB, 1), 0)
    a_rows = [jnp.zeros((HG_SUB, C), F32)]
    blocks = []
    for I in range(N_SUB):
        lo = I * HG_SUB
        q_i, b_i = q[lo:lo + HG_SUB], b[lo:lo + HG_SUB]
        if I > 0:
            eq, ek = _hg_split_decay(b, I, rid)
            a_rows.append(lax.dot_general((q_i * eq).astype(BF16), (k * ek).astype(BF16), _NT,
                                          preferred_element_type=F32))
        o_i = jnp.zeros((HG_SUB, HG_DV), F32)
        for s in range(HG_SUB):
            r = lo + s
            e = jnp.exp(jnp.minimum(b_i - b[r:r + 1], 0.0))
            a = jnp.sum(q_i * k[r:r + 1] * e, axis=-1, keepdims=True)
            o_i = o_i + jnp.where(tid >= s, a, 0.0) * v[r:r + 1]
        blocks.append(o_i)
    a_off = jnp.concatenate(a_rows, axis=0).astype(BF16)
    return jnp.dot(a_off, v.astype(BF16), preferred_element_type=F32) + jnp.concatenate(blocks, axis=0)


def _hg_intra_bwd(q, k, v, b, do, dk_s, dv_s):
    rid = lax.broadcasted_iota(jnp.int32, (C, 1), 0)
    tid = lax.broadcasted_iota(jnp.int32, (HG_SUB, 1), 0)
    da_all = lax.dot_general(do.astype(BF16), v.astype(BF16), _NT, preferred_element_type=F32)
    a_rows = [jnp.zeros((HG_SUB, C), F32)]
    dq_blocks = []
    dk = jnp.zeros((C, HG_DK), F32)
    for I in range(N_SUB):
        lo = I * HG_SUB
        q_i, b_i, do_i = q[lo:lo + HG_SUB], b[lo:lo + HG_SUB], do[lo:lo + HG_SUB]
        dq_i = jnp.zeros((HG_SUB, HG_DK), F32)
        if I > 0:
            eq, ek = _hg_split_decay(b, I, rid)
            qs, ks = (q_i * eq).astype(BF16), (k * ek).astype(BF16)
            a_rows.append(lax.dot_general(qs, ks, _NT, preferred_element_type=F32))
            da = da_all[lo:lo + HG_SUB].astype(BF16)
            dq_i = jnp.dot(da, ks, preferred_element_type=F32) * eq
            dk = dk + lax.dot_general(da, qs, _TN, preferred_element_type=F32) * ek
        for s in range(HG_SUB):
            r = lo + s
            e = jnp.where(tid >= s, jnp.exp(jnp.minimum(b_i - b[r:r + 1], 0.0)), 0.0)
            a = jnp.sum(q_i * k[r:r + 1] * e, axis=-1, keepdims=True)
            g = jnp.sum(do_i * v[r:r + 1], axis=-1, keepdims=True) * e
            dq_i = dq_i + g * k[r:r + 1]
            dk_s[r:r + 1, :] = jnp.sum(g * q_i, axis=0, keepdims=True)
            dv_s[r:r + 1, :] = jnp.sum(a * do_i, axis=0, keepdims=True)
        dq_blocks.append(dq_i)
    a_off = jnp.concatenate(a_rows, axis=0).astype(BF16)
    dv = lax.dot_general(a_off, do.astype(BF16), _TN, preferred_element_type=F32)
    return jnp.concatenate(dq_blocks, axis=0), dk + dk_s[...], dv + dv_s[...]


def _hgrn_fwd(u, kk, lf, vv, T, name):
    nb = _pick(T // C, (HG_NB, 3, 2, 1))
    rows = nb * C
    qblk = C_HQ // HG_DK

    def body(q_ref, k_ref, lf_ref, v_ref, o_ref, st_ref, st):
        @pl.when(pl.program_id(1) == 0)
        def _():
            st[...] = jnp.zeros_like(st)

        lower = _tri(True)
        for n in range(nb):
            sl = slice(n * C, (n + 1) * C)
            q, k, v = q_ref[sl, :].astype(F32), k_ref[sl, :], v_ref[sl, :]
            b = jnp.dot(lower, lf_ref[sl, :], precision=_HI, preferred_element_type=F32)
            s_t = st[...]
            st_ref[0, n] = s_t
            qe = (q * jnp.exp(b)).astype(BF16)
            o = lax.dot_general(qe, s_t.astype(BF16), _NT, preferred_element_type=F32)
            o_ref[sl, :] = o + _hg_intra_fwd(q, k, v, b)
            bl = b[C - 1:C, :]
            kd = (k * jnp.exp(bl - b)).astype(BF16)
            st[...] = s_t * jnp.exp(bl) + lax.dot_general(v.astype(BF16), kd, _TN, preferred_element_type=F32)

    col = lambda off: pl.BlockSpec((rows, HG_DK), lambda h, c: (c, h + off))
    return pl.pallas_call(
        body, name=name, grid=(HG_HEADS, T // rows),
        in_specs=[col(qblk), col(0), col(0), col(0)],
        out_specs=[col(0), pl.BlockSpec((1, nb, HG_DV, HG_DK), lambda h, c: (h, c, 0, 0))],
        out_shape=[jax.ShapeDtypeStruct((T, HG_HEADS * HG_DV), F32),
                   jax.ShapeDtypeStruct((HG_HEADS, T // C, HG_DV, HG_DK), F32)],
        scratch_shapes=[pltpu.VMEM((HG_DV, HG_DK), F32)],
        compiler_params=_cparams(("parallel", "arbitrary")),
    )(u, kk, lf, vv)


def _hgrn_bwd(u, kk, lf, vv, states, do, T, name):
    nb = _pick(T // C, (HG_NB, 3, 2, 1))
    rows = nb * C
    n_steps = T // rows
    qblk = C_HQ // HG_DK

    def body(q_ref, k_ref, lf_ref, v_ref, st_ref, do_ref, dq_ref, dk_ref, dlf_ref, dv_ref, dst, dk_s, dv_s):
        @pl.when(pl.program_id(1) == 0)
        def _():
            dst[...] = jnp.zeros_like(dst)

        lower, upper = _tri(True), _tri(False)
        rid = lax.broadcasted_iota(jnp.int32, (C, 1), 0)
        for n in reversed(range(nb)):
            sl = slice(n * C, (n + 1) * C)
            q, k, v, do = q_ref[sl, :].astype(F32), k_ref[sl, :], v_ref[sl, :], do_ref[sl, :]
            b = jnp.dot(lower, lf_ref[sl, :], precision=_HI, preferred_element_type=F32)
            s_t = st_ref[0, n]
            d_new = dst[...]
            eb = jnp.exp(b)
            bl = b[C - 1:C, :]
            ebl = jnp.exp(bl)
            dec = jnp.exp(bl - b)
            qe = q * eb
            kd = k * dec
            do_b = do.astype(BF16)
            dqe = jnp.dot(do_b, s_t.astype(BF16), preferred_element_type=F32)
            dkd = jnp.dot(v.astype(BF16), d_new.astype(BF16), preferred_element_type=F32)
            dv = lax.dot_general(kd.astype(BF16), d_new.astype(BF16), _NT, preferred_element_type=F32)
            dbl = ebl * jnp.sum(d_new * s_t, axis=0, keepdims=True) + jnp.sum(dkd * kd, axis=0, keepdims=True)
            dst[...] = d_new * ebl + lax.dot_general(do_b, qe.astype(BF16), _TN, preferred_element_type=F32)
            dq_in, dk_in, dv_in = _hg_intra_bwd(q, k, v, b, do, dk_s, dv_s)
            dq = dqe * eb + dq_in
            dk = dkd * dec + dk_in
            dv = dv + dv_in
            db = q * dq - k * dk
            db = db + jnp.where(rid == C - 1, dbl, 0.0)
            dq_ref[sl, :] = dq
            dk_ref[sl, :] = dk
            dv_ref[sl, :] = dv
            dlf_ref[sl, :] = jnp.dot(upper, db, precision=_HI, preferred_element_type=F32)

    rev = lambda off: pl.BlockSpec((rows, HG_DK), lambda h, c: (n_steps - 1 - c, h + off))
    return pl.pallas_call(
        body, name=name, grid=(HG_HEADS, n_steps),
        in_specs=[rev(qblk), rev(0), rev(0), rev(0),
                  pl.BlockSpec((1, nb, HG_DV, HG_DK), lambda h, c: (h, n_steps - 1 - c, 0, 0)), rev(0)],
        out_specs=[rev(0)] * 4,
        out_shape=[jax.ShapeDtypeStruct((T, HG_HEADS * HG_DK), F32)] * 4,
        scratch_shapes=[pltpu.VMEM((HG_DV, HG_DK), F32), pltpu.VMEM((C, HG_DK), F32), pltpu.VMEM((C, HG_DV), F32)],
        compiler_params=_cparams(("parallel", "arbitrary")),
    )(u, kk, lf, vv, states, do)


def _rope_tables(T):
    half = ROPE // 2
    inv_freq = (ROPE_BASE ** (-np.arange(half, dtype=np.float32) / half)).astype(np.float32)
    row = lambda lo, hi, val: np.concatenate([np.zeros(lo, np.float32), np.asarray(val, np.float32) * np.ones(hi - lo, np.float32),
                                              np.zeros(HEAD_W - hi, np.float32)])[None, :]
    freq = row(NOPE, NOPE + half, inv_freq) + row(NOPE + half, NOPE + ROPE, inv_freq)
    pos = lax.broadcasted_iota(jnp.int32, (T, HEAD_W), 0).astype(F32) - float(PAD_FRONT)
    ang = pos * freq
    cos, sin = jnp.cos(ang), jnp.sin(ang)
    c = cos * row(NOPE, NOPE + ROPE, 1.0) + row(0, NOPE, 1.0)
    s1 = sin * row(NOPE, NOPE + half, -1.0)
    s2 = sin * row(NOPE + half, NOPE + ROPE, 1.0)
    return c, s1, s2


def _layer_fwd(x, w, tabs, T, l):
    c, s1, s2 = tabs
    n = lambda s: f"l{l}_{s}"
    sv = {"x": x}
    h = _rowwise(_f_rms, T, [Row(x)], [(w["norm1_g"], D_MODEL)], [(D_MODEL, BF16)], n("norm1"))[0]
    u = _mm(h, w["w_in"], out_dtype=BF16, name=n("in_proj"))
    sv.update(h=h, u=u)
    hglu = _rowwise(_f_glu, T, [Row(u, 512, C_CONV_A), Row(u, 512, C_CONV_G)], [], [(CONV_DIM, F32)], n("glu"))[0]
    cv = _conv_fwd(hglu, w["conv_w"], w["conv_b"], T, n("conv"))
    hc = _rowwise(_f_lnsilu, T, [Row(cv)], [(w["conv_ln_g"], CONV_DIM), (w["conv_ln_b"], CONV_DIM)],
                  [(CONV_DIM, BF16)], n("conv_ln"))[0]
    y_a = _mm(hc, w["w_conv_out"], out_dtype=BF16, name=n("conv_out"))
    sv.update(hglu=hglu, cv=cv, hc=hc, y_a=y_a)
    cqn = _rowwise(_f_rms, T, [Row(u, Q_RANK, C_CQ)], [(w["q_a_norm_g"], Q_RANK)], [(Q_RANK, BF16)], n("q_a_norm"))[0]
    ckvn = _rowwise(_f_rms, T, [Row(u, KV_RANK, C_CKV)], [(w["kv_a_norm_g"], KV_RANK)], [(KV_RANK, BF16)], n("kv_a_norm"))[0]
    q_raw = _mm(cqn, w["w_uq"], out_dtype=BF16, name=n("uq"))
    k_raw = _mm(ckvn, w["w_uk"], out_dtype=BF16, name=n("uk"))
    v = _mm(ckvn, w["w_uv"], out_dtype=BF16, name=n("uv"))
    tab_rows = [Row(c), Row(s1), Row(s2)]
    q = _rowwise(_f_qrope, T, [Row(q_raw, piece=HEAD_W)] + tab_rows, [(w["q_norm_g"], HEAD_W)],
                 [(HEADS * HEAD_W, BF16)], n("q_rope"))[0]
    k = _rowwise(_f_krope, T, [Row(k_raw, piece=HEAD_W), Row(u, HEAD_W, C_KR)] + tab_rows, [(w["k_norm_g"], HEAD_W)],
                 [(HEADS * HEAD_W, BF16)], n("k_rope"))[0]
    o, lse = _attn_fwd(q, k, v, T, n("attn"))
    y_b = _mm(o, w["w_attn_out"], out_dtype=BF16, name=n("attn_out"))
    sv.update(cqn=cqn, ckvn=ckvn, q_raw=q_raw, k_raw=k_raw, v=v, q=q, k=k, o=o, lse=lse, y_b=y_b)
    kk, lf, vv = _rowwise(_f_hgrn_prep, T, [Row(u, 512, C_HF), Row(u, 512, C_HI)], [(w["lb"], 512)],
                          [(512, F32)] * 3, n("hgrn_prep"))
    o_h, states = _hgrn_fwd(u, kk, lf, vv, T, n("hgrn"))
    oh = _rowwise(_f_hgrn_out, T, [Row(o_h, piece=HG_DV), Row(u, 512, C_HG, piece=HG_DV)], [(w["hgrn_norm_g"], HG_DV)],
                  [(512, BF16)], n("hgrn_out_norm"))[0]
    y_c = _mm(oh, w["w_hgrn_out"], out_dtype=BF16, name=n("hgrn_out"))
    sv.update(kk=kk, lf=lf, vv=vv, o_h=o_h, states=states, oh=oh, y_c=y_c)
    gate_rows = [Row(u, D_MODEL, C_GATE + g * D_MODEL) for g in range(3)]
    mix = _rowwise(_f_mix, T, gate_rows + [Row(y_a), Row(y_b), Row(y_c)], [], [(D_MODEL, BF16)], n("mix"))[0]
    x1 = _mm(mix, w["w_out"], res=x, name=n("out_proj"))
    h2 = _rowwise(_f_rms, T, [Row(x1)], [(w["norm2_g"], D_MODEL)], [(D_MODEL, BF16)], n("norm2"))[0]
    f = _mm(h2, w["w_ff1"], out_dtype=BF16, name=n("ff1"))
    x2 = _mm(f, w["w_ff2"], res=x1, a_fn=_relu2, name=n("ff2"))
    sv.update(mix=mix, x1=x1, h2=h2, f=f)
    return x2, sv


def _layer_bwd(dx2, w, sv, tabs, T, l, mid=None, matrices=None):
    c, s1, s2 = tabs
    n = lambda s: f"l{l}_b_{s}"
    u = sv["u"]
    g = {}
    g["w_ff2"] = _mm(sv["f"], dx2, ta=True, a_fn=_relu2, out_dtype=BF16, name=n("dw_ff2"))
    df = _mm(dx2, w["w_ff2"], tb=True, out_dtype=BF16, name=n("d_f"),
             epi=(sv["f"], lambda d, fv: d * (2.0 * jnp.maximum(fv, 0.0))))
    g["w_ff1"] = _mm(sv["h2"], df, ta=True, out_dtype=BF16, name=n("dw_ff1"))
    dh2 = _mm(df, w["w_ff1"], tb=True, name=n("d_h2"))
    (dx1,), (g["norm2_g"],) = _rowwise_bwd(_f_rms, T, [Row(sv["x1"])], [(w["norm2_g"], D_MODEL)], [Row(dh2)],
                                           {0: F32}, n("norm2"), add=(0, dx2))
    g["w_out"] = _mm(sv["mix"], dx1, ta=True, out_dtype=BF16, name=n("dw_out"))
    w_out = w["w_out"] if mid is None else mid(g, w["w_out"])
    dmix = _mm(dx1, w_out, tb=True, out_dtype=BF16, name=n("d_mix"))
    gate_rows = [Row(u, D_MODEL, C_GATE + i * D_MODEL) for i in range(3)]
    (dg0, dg1, dg2, dy_a, dy_b, dy_c), _ = _rowwise_bwd(
        _f_mix, T, gate_rows + [Row(sv["y_a"]), Row(sv["y_b"]), Row(sv["y_c"])], [], [Row(dmix)],
        {0: BF16, 1: BF16, 2: BF16, 3: BF16, 4: BF16, 5: BF16}, n("mix"))
    g["w_hgrn_out"] = _mm(sv["oh"], dy_c, ta=True, out_dtype=BF16, name=n("dw_hgrn_out"))
    doh = _mm(dy_c, w["w_hgrn_out"], tb=True, out_dtype=BF16, name=n("d_oh"))
    (do_h, dhg), (g["hgrn_norm_g"],) = _rowwise_bwd(
        _f_hgrn_out, T, [Row(sv["o_h"], piece=HG_DV), Row(u, 512, C_HG, piece=HG_DV)], [(w["hgrn_norm_g"], HG_DV)],
        [Row(doh, piece=HG_DV)], {0: F32, 1: BF16}, n("hgrn_out_norm"))
    dhq, dkk, dlf, dvv = _hgrn_bwd(u, sv["kk"], sv["lf"], sv["vv"], sv["states"], do_h, T, n("hgrn"))
    (dhf, dhi), (g["lb"],) = _rowwise_bwd(
        _f_hgrn_prep, T, [Row(u, 512, C_HF), Row(u, 512, C_HI)], [(w["lb"], 512)],
        [Row(dkk), Row(dlf), Row(dvv)], {0: BF16, 1: BF16}, n("hgrn_prep"))
    g["w_attn_out"] = _mm(sv["o"], dy_b, ta=True, out_dtype=BF16, name=n("dw_attn_out"))
    do = _mm(dy_b, w["w_attn_out"], tb=True, out_dtype=BF16, name=n("d_o"))
    dq, dk, dv = _attn_bwd(sv["q"], sv["k"], sv["v"], sv["o"], sv["lse"], do, T, n("attn"))
    tab_rows = [Row(c), Row(s1), Row(s2)]
    (dq_raw,), (g["q_norm_g"],) = _rowwise_bwd(
        _f_qrope, T, [Row(sv["q_raw"], piece=HEAD_W)] + tab_rows, [(w["q_norm_g"], HEAD_W)],
        [Row(dq, piece=HEAD_W)], {0: BF16}, n("q_rope"))
    (dk_raw, dkr), (g["k_norm_g"],) = _rowwise_bwd(
        _f_krope, T, [Row(sv["k_raw"], piece=HEAD_W), Row(u, HEAD_W, C_KR)] + tab_rows, [(w["k_norm_g"], HEAD_W)],
        [Row(dk, piece=HEAD_W)], {0: BF16, 1: BF16}, n("k_rope"))
    g["w_uq"] = _mm(sv["cqn"], dq_raw, ta=True, out_dtype=BF16, name=n("dw_uq"))
    g["w_uk"] = _mm(sv["ckvn"], dk_raw, ta=True, out_dtype=BF16, name=n("dw_uk"))
    g["w_uv"] = _mm(sv["ckvn"], dv, ta=True, out_dtype=BF16, name=n("dw_uv"))
    dcqn = _mm(dq_raw, w["w_uq"], tb=True, out_dtype=BF16, name=n("d_cqn"))
    dckvn = _mm(dk_raw, w["w_uk"], tb=True, name=n("d_ckvn_k"))
    dckvn = _mm(dv, w["w_uv"], tb=True, res=dckvn, out_dtype=BF16, name=n("d_ckvn_v"))
    (dcq,), (g["q_a_norm_g"],) = _rowwise_bwd(_f_rms, T, [Row(u, Q_RANK, C_CQ)], [(w["q_a_norm_g"], Q_RANK)],
                                              [Row(dcqn)], {0: BF16}, n("q_a_norm"))
    (dckv,), (g["kv_a_norm_g"],) = _rowwise_bwd(_f_rms, T, [Row(u, KV_RANK, C_CKV)], [(w["kv_a_norm_g"], KV_RANK)],
                                                [Row(dckvn)], {0: BF16}, n("kv_a_norm"))
    g["w_conv_out"] = _mm(sv["hc"], dy_a, ta=True, out_dtype=BF16, name=n("dw_conv_out"))
    dhc = _mm(dy_a, w["w_conv_out"], tb=True, out_dtype=BF16, name=n("d_hc"))
    (dcv,), (g["conv_ln_g"], g["conv_ln_b"]) = _rowwise_bwd(
        _f_lnsilu, T, [Row(sv["cv"])], [(w["conv_ln_g"], CONV_DIM), (w["conv_ln_b"], CONV_DIM)], [Row(dhc)],
        {0: F32}, n("conv_ln"))
    dhglu, dconv_w, g["conv_b"] = _conv_bwd(sv["hglu"], w["conv_w"], dcv, T, n("conv"))
    g["conv_w"] = dconv_w[:CONV_K]
    (dua, dug), _ = _rowwise_bwd(_f_glu, T, [Row(u, 512, C_CONV_A), Row(u, 512, C_CONV_G)], [], [Row(dhglu)],
                                 {0: BF16, 1: BF16}, n("glu"))
    du = jnp.concatenate([dua, dug, dg0, dg1, dg2, dcq, dckv, dkr, dhq.astype(BF16), dhf, dhi, dhg], axis=1)
    g["w_in"] = _mm(sv["h"], du, ta=True, out_dtype=BF16, name=n("dw_in"))
    norm_g = w["norm1_g"] if matrices is None else matrices(g, w["norm1_g"])
    dh = _mm(du, w["w_in"], tb=True, name=n("d_h"))
    (dx,), (g["norm1_g"],) = _rowwise_bwd(_f_rms, T, [Row(sv["x"])], [(norm_g, D_MODEL)], [Row(dh)],
                                          {0: F32}, n("norm1"), add=(0, dx1))
    return dx, g


def _pad_w_in(w_in):
    z = lambda k: jnp.zeros((w_in.shape[0], k), w_in.dtype)
    return jnp.concatenate([w_in[:, :O_CQ], w_in[:, O_GATE:], w_in[:, O_CQ:O_KR], z(KR_LANE), w_in[:, O_KR:O_HQ],
                            z(HEAD_W - KR_LANE - ROPE), w_in[:, O_HQ:O_GATE]], axis=1)


def _unpad_w_in(g):
    return jnp.concatenate([g[:, :C_GATE], g[:, C_CQ:C_KR], g[:, C_KR + KR_LANE:C_KR + KR_LANE + ROPE],
                            g[:, C_HQ:], g[:, C_GATE:C_CQ]], axis=1)


def _pad_heads(wm, per_head, lo, hi):
    lead = wm.shape[:-1]
    wh = wm.reshape(lead + (HEADS, per_head))[..., lo:hi]
    pad = [(0, 0)] * len(lead) + [(0, 0), (0, HEAD_W - (hi - lo))]
    return jnp.pad(wh, pad).reshape(lead + (HEADS * HEAD_W,))


def _unpad_heads(gm, width):
    lead = gm.shape[:-1]
    return gm.reshape(lead + (HEADS, HEAD_W))[..., :width]


def _layer_weights(full, lb):
    w = {}
    w["norm1_g"] = full["norm1_g"]
    w["w_in"] = _pad_w_in(full["w_in"])
    w["conv_w"] = full["conv_w"]
    w["conv_b"] = full["conv_b"]
    w["conv_ln_g"] = full["conv_ln_g"]
    w["conv_ln_b"] = full["conv_ln_b"]
    w["w_conv_out"] = full["w_conv_out"]
    w["q_a_norm_g"] = full["q_a_norm_g"]
    w["w_uq"] = _pad_heads(full["w_uq"], QK_DIM, 0, QK_DIM)
    w["kv_a_norm_g"] = full["kv_a_norm_g"]
    w["w_uk"] = _pad_heads(full["w_ukv"], NOPE + V_DIM, 0, NOPE)
    w["w_uv"] = _pad_heads(full["w_ukv"], NOPE + V_DIM, NOPE, NOPE + V_DIM)
    w["q_norm_g"] = jnp.pad(full["q_norm_g"], (0, HEAD_W - QK_DIM))
    w["k_norm_g"] = jnp.pad(full["k_norm_g"], (0, HEAD_W - QK_DIM))
    wa = full["w_attn_out"].reshape(HEADS, V_DIM, D_MODEL)
    w["w_attn_out"] = jnp.pad(wa, ((0, 0), (0, HEAD_W - V_DIM), (0, 0))).reshape(HEADS * HEAD_W, D_MODEL)
    w["lb"] = lb
    w["hgrn_norm_g"] = full["hgrn_norm_g"]
    w["w_hgrn_out"] = full["w_hgrn_out"]
    w["w_out"] = full["w_out"]
    w["norm2_g"] = full["norm2_g"]
    w["w_ff1"] = full["w_ff1"]
    w["w_ff2"] = full["w_ff2"]
    return w


def _matrix_grads_to_original(g):
    o = {name: g[name] for name in ("w_conv_out", "w_hgrn_out", "w_out", "w_ff1", "w_ff2")}
    o["w_in"] = _unpad_w_in(g["w_in"])
    o["w_uq"] = _unpad_heads(g["w_uq"], QK_DIM).reshape(Q_RANK, HEADS * QK_DIM)
    guk = _unpad_heads(g["w_uk"], NOPE)
    guv = _unpad_heads(g["w_uv"], V_DIM)
    o["w_ukv"] = jnp.concatenate([guk, guv], axis=-1).reshape(KV_RANK, HEADS * (NOPE + V_DIM))
    o["w_attn_out"] = g["w_attn_out"].reshape(HEADS, HEAD_W, D_MODEL)[:, :V_DIM].reshape(HEADS * V_DIM, D_MODEL)
    return o


def _vector_grads_to_original(g):
    o = {"conv_w": g["conv_w"]}
    for name in ("norm1_g", "conv_b", "conv_ln_g", "conv_ln_b", "q_a_norm_g", "kv_a_norm_g", "hgrn_norm_g", "norm2_g", "lb"):
        o[name] = g[name].reshape(-1)
    o["q_norm_g"] = g["q_norm_g"].reshape(-1)[:QK_DIM]
    o["k_norm_g"] = g["k_norm_g"].reshape(-1)[:QK_DIM]
    return o


def _lower_bounds(logits):
    p = jax.nn.softmax(logits.astype(F32), axis=0)
    return jnp.cumsum(p, axis=0) - p[0:1]


def _run_step(x, target, meta, lb_logits, layer_weights, layer_done, layer_mid=None, layer_matrices=None):
    seq = x.shape[0]
    T = ROW0 + seq
    assert T % 128 == 0
    tabs = _rope_tables(T)
    lbs, lb_vjp = jax.vjp(_lower_bounds, lb_logits)
    xp = jnp.concatenate([jnp.zeros((PAD_FRONT, D_MODEL), F32), meta.astype(F32), x], axis=0)
    tp = jnp.concatenate([jnp.zeros((ROW0, D_MODEL), F32), target], axis=0)
    ws, svs = [], []
    for l in range(DEPTH):
        full, xp = layer_weights(l, xp)
        w = _layer_weights(full, lbs[l])
        xp, sv = _layer_fwd(xp, w, tabs, T, l)
        ws.append(w)
        svs.append(sv)
    dx, sq = _loss_head(xp, tp, T)
    loss = 0.5 * jnp.sum(sq) * (1.0 / D_MODEL)
    dlb = [None] * DEPTH
    for l in reversed(range(DEPTH)):
        mid = None if layer_mid is None else functools.partial(layer_mid, l)
        mats = {}

        def matrices(g, norm_g, l=l, mats=mats):
            mats.update(_matrix_grads_to_original(g))
            return norm_g if layer_matrices is None else layer_matrices(l, mats, norm_g)

        dx, g = _layer_bwd(dx, ws[l], svs[l], tabs, T, l, mid, matrices)
        g = {**_vector_grads_to_original(g), **mats}
        dlb[l] = g.pop("lb")
        dx = layer_done(l, g, dx)
    return loss, dx[ROW0:], dx[PAD_FRONT:ROW0], lb_vjp(jnp.stack(dlb))[0]


def _local_step(x, target, full):
    per_layer = [None] * DEPTH

    def done(l, g, dx):
        per_layer[l] = g
        return dx

    loss, gx, gmeta, glb = _run_step(
        x, target, full["meta"], full["hgrn_lb_logits"],
        lambda l, xp: ({k: v[l] for k, v in full.items() if k != "meta"}, xp), done)
    grads = {k: jnp.stack([per_layer[l][k] for l in range(DEPTH)]) for k in per_layer[0]}
    grads["hgrn_lb_logits"] = glb
    grads["meta"] = gmeta
    return loss, gx, grads


def _mesh_pos():
    return lax.axis_index("x"), lax.axis_index("y"), lax.axis_index("c")


N_COPY = N_DEV - 1


def _all_gather(arrs, name):
    n = len(arrs)

    def body(*refs):
        x_refs, out_refs = refs[:n], refs[n:2 * n]
        send_sems, recv_sems, local_sems = refs[2 * n:]
        x, y, c = _mesh_pos()
        me, sibling = (x, y, c), (x, y, 1 - c)
        chips = [(1 - x, y), (x, 1 - y), (1 - x, 1 - y)]

        def slot(a, px, py, pc):
            return out_refs[a].at[4 * px + 2 * py + pc]

        def copy(a, k, block, to, own=False):
            return pltpu.make_async_remote_copy(
                src_ref=x_refs[a] if own else slot(a, *block), dst_ref=slot(a, *block),
                send_sem=send_sems.at[a * N_COPY + k], recv_sem=recv_sems.at[a * N_COPY + k],
                device_id=to, device_id_type=MESH)

        mine = [pltpu.make_async_copy(x_refs[a], slot(a, *me), local_sems.at[a]) for a in range(n)]
        for cp in mine:
            cp.start()
        first = []
        for a in range(n):
            first.append(copy(a, 0, me, sibling, own=True))
            first += [copy(a, 1 + j, me, (*chip, c), own=True) for j, chip in enumerate(chips)]
        for cp in first:
            cp.start()
        passed = []
        for j, chip in enumerate(chips):
            for a in range(n):
                copy(a, 1 + j, (*chip, c), me).wait_recv()
                cp = copy(a, 4 + j, (*chip, c), sibling)
                cp.start()
                passed.append(cp)
        for a in range(n):
            copy(a, 0, sibling, me).wait_recv()
            for j, chip in enumerate(chips):
                copy(a, 4 + j, (*chip, 1 - c), me).wait_recv()
        for cp in first + passed:
            cp.wait_send()
        for cp in mine:
            cp.wait()

    anyspec = pl.BlockSpec(memory_space=pl.ANY)
    return pl.pallas_call(
        body, name=name, out_shape=[jax.ShapeDtypeStruct((N_DEV,) + a.shape, a.dtype) for a in arrs],
        in_specs=[anyspec] * n, out_specs=[anyspec] * n,
        scratch_shapes=[pltpu.SemaphoreType.DMA((n * N_COPY,)), pltpu.SemaphoreType.DMA((n * N_COPY,)),
                        pltpu.SemaphoreType.DMA((n,))],
    )(*arrs)


def _exchange(arrs, name):
    n = len(arrs)

    def body(*refs):
        s_refs, r_refs = refs[:n], refs[n:2 * n]
        send_sems, recv_sems, local_sems = refs[2 * n:]
        x, y, c = _mesh_pos()
        me = 4 * x + 2 * y + c
        local = [pltpu.make_async_copy(s_refs[a].at[me], r_refs[a].at[me], local_sems.at[a]) for a in range(n)]
        for cp in local:
            cp.start()
        sends, recvs = [], []
        for rel in range(1, N_DEV):
            px = 1 - x if rel & 4 else x
            py = 1 - y if rel & 2 else y
            pc = 1 - c if rel & 1 else c
            p = 4 * px + 2 * py + pc
            for a in range(n):
                k = a * N_COPY + rel - 1
                sends.append(pltpu.make_async_remote_copy(
                    src_ref=s_refs[a].at[p], dst_ref=r_refs[a].at[me], send_sem=send_sems.at[k],
                    recv_sem=recv_sems.at[k], device_id=(px, py, pc), device_id_type=MESH))
                recvs.append(pltpu.make_async_remote_copy(
                    src_ref=s_refs[a].at[me], dst_ref=r_refs[a].at[p], send_sem=send_sems.at[k],
                    recv_sem=recv_sems.at[k], device_id=(px, py, pc), device_id_type=MESH))
        for cp in sends:
            cp.start()
        for cp in recvs:
            cp.wait_recv()
        for cp in sends:
            cp.wait_send()
        for cp in local:
            cp.wait()

    anyspec = pl.BlockSpec(memory_space=pl.ANY)
    return pl.pallas_call(
        body, name=name, out_shape=[jax.ShapeDtypeStruct(a.shape, a.dtype) for a in arrs],
        in_specs=[anyspec] * n, out_specs=[anyspec] * n,
        scratch_shapes=[pltpu.SemaphoreType.DMA((n * N_COPY,)), pltpu.SemaphoreType.DMA((n * N_COPY,)),
                        pltpu.SemaphoreType.DMA((n,))],
    )(*arrs)


_HBM = pl.BlockSpec(memory_space=pltpu.HBM)
_SEM = pl.BlockSpec(memory_space=pltpu.SEMAPHORE)
_EFFECT = pltpu.SideEffectType.DATAFLOW_SIDE_EFFECTING


def _peers(x, y, c):
    out = []
    for rel in range(1, N_DEV):
        px = 1 - x if rel & 4 else x
        py = 1 - y if rel & 2 else y
        pc = 1 - c if rel & 1 else c
        out.append((rel, (px, py, pc), 4 * px + 2 * py + pc))
    return out


def _split_copies(src_refs, land_refs, send_sems, recv_sems, gather):
    x, y, c = _mesh_pos()
    me = 4 * x + 2 * y + c
    out = []
    for a, (src, land) in enumerate(zip(src_refs, land_refs)):
        for rel, peer, p in _peers(x, y, c):
            k = a * N_COPY + rel - 1
            mk = lambda s, d: pltpu.make_async_remote_copy(
                src_ref=s, dst_ref=d, send_sem=send_sems.at[k], recv_sem=recv_sems.at[k],
                device_id=peer, device_id_type=MESH)
            mine = src if gather else src.at[p]
            out.append((mk(mine, land.at[me]), mk(mine, land.at[p])))
    return out


def _copy_start(srcs, gather, name, collective_id):
    n = len(srcs)
    lands = [lax.empty(((N_DEV,) + s.shape) if gather else s.shape, s.dtype) for s in srcs]

    def body(*refs):
        src_refs, land_refs = refs[:n], refs[n:2 * n]
        send_sems, recv_sems = refs[2 * n], refs[2 * n + 1]
        token = refs[-1]
        x, y, c = _mesh_pos()
        barrier = pltpu.get_barrier_semaphore()
        for _, peer, _ in _peers(x, y, c):
            pl.semaphore_signal(barrier, inc=1, device_id=peer, device_id_type=MESH)
        pl.semaphore_wait(barrier, N_COPY)
        for out_copy, _ in _split_copies(src_refs, land_refs, send_sems, recv_sems, gather):
            out_copy.start()
        token[...] = jnp.zeros_like(token)

    hbm = lambda a: pltpu.HBM(a.shape, a.dtype)
    res = pl.pallas_call(
        body, name=name,
        out_shape=(pltpu.SemaphoreType.DMA((n * N_COPY,)), pltpu.SemaphoreType.DMA((n * N_COPY,)),
                   *[hbm(s) for s in srcs], *[hbm(z) for z in lands], jax.ShapeDtypeStruct((8, 128), F32)),
        in_specs=[_HBM] * (2 * n), out_specs=(_SEM, _SEM, *([_HBM] * (2 * n)), pl.BlockSpec(memory_space=pltpu.VMEM)),
        input_output_aliases={i: 2 + i for i in range(2 * n)},
        compiler_params=pltpu.CompilerParams(has_side_effects=_EFFECT, collective_id=collective_id),
    )(*[pltpu.with_memory_space_constraint(s, pltpu.HBM) for s in srcs],
      *[pltpu.with_memory_space_constraint(z, pltpu.HBM) for z in lands])
    return res[0], res[1], list(res[2:2 + n]), list(res[2 + n:2 + 2 * n]), res[-1]


def _copy_wait(send_sems, recv_sems, srcs, lands, after, gather, name):
    n = len(srcs)

    def body(*refs):
        src_refs, land_refs = refs[:n], refs[n:2 * n]
        s_sems, r_sems = refs[2 * n], refs[2 * n + 1]
        for out_copy, in_copy in _split_copies(src_refs, land_refs, s_sems, r_sems, gather):
            out_copy.wait_send()
            in_copy.wait_recv()

    hbm = lambda a: pltpu.HBM(a.shape, a.dtype)
    res = pl.pallas_call(
        body, name=name, out_shape=(*[hbm(s) for s in srcs], *[hbm(z) for z in lands]),
        in_specs=[_HBM] * (2 * n) + [_SEM, _SEM, pl.BlockSpec(memory_space=pl.ANY)], out_specs=tuple([_HBM] * (2 * n)),
        input_output_aliases={i: i for i in range(2 * n)},
        compiler_params=pltpu.CompilerParams(has_side_effects=_EFFECT),
    )(*srcs, *lands, send_sems, recv_sems, after)
    return list(res[:n]), list(res[n:])


def _sum_parts(parts, name):
    P, R, W = parts.shape

    def body(p_ref, o_ref):
        g = p_ref[0].astype(F32)
        for i in range(1, P):
            g = g + p_ref[i].astype(F32)
        o_ref[...] = g

    return pl.pallas_call(body, name=name, out_shape=jax.ShapeDtypeStruct((R, W), F32))(parts)


def _adamw_body(p_ref, w_ref, m_ref, v_ref, g_ref, d_ref, nm_ref, nv_ref):
    g = p_ref[0].astype(F32)
    for i in range(1, p_ref.shape[0]):
        g = g + p_ref[i].astype(F32)
    m_new = ADAM_B1 * m_ref[...] + (1.0 - ADAM_B1) * g
    v_new = ADAM_B2 * v_ref[...] + (1.0 - ADAM_B2) * jnp.square(g)
    m_hat = m_new / (1.0 - ADAM_B1 ** ADAM_STEP)
    v_hat = v_new / (1.0 - ADAM_B2 ** ADAM_STEP)
    g_ref[...] = g
    d_ref[...] = -ADAM_LR * (m_hat / (jnp.sqrt(v_hat) + ADAM_EPS) + ADAM_WD * w_ref[...])
    nm_ref[...] = m_new
    nv_ref[...] = v_new


def _adamw(parts, w, m, v, name):
    P, R, W = parts.shape
    tr = _pick(R, (368, 192, 64, 16, 8))
    spec = pl.BlockSpec((tr, W), lambda i: (i, 0))
    return pl.pallas_call(
        functools.partial(_adamw_body), name=name, grid=(R // tr,),
        in_specs=[pl.BlockSpec((P, tr, W), lambda i: (0, i, 0)), spec, spec, spec], out_specs=[spec] * 4,
        out_shape=[jax.ShapeDtypeStruct((R, W), F32)] * 4,
        compiler_params=_cparams(("parallel",)),
    )(parts, w, m, v)


def _adamw_layers(parts, w, m, v, name):
    P, B, C_ = parts[0].shape
    tb = _pick(B, (256, 128))
    nb = B // tb

    def body(*refs):
        p_refs, rest = refs[:DEPTH], refs[DEPTH:]
        a = pl.program_id(0)
        for l in range(DEPTH):
            @pl.when(a == l)
            def _():
                _adamw_body(p_refs[l], *[r.at[0] for r in rest])

    spec = pl.BlockSpec((1, tb, C_), lambda a, i: (a, i, 0))

    def part_spec(l):
        return pl.BlockSpec((P, tb, C_), lambda a, i: (0, jnp.where(a == l, i, jnp.where(a < l, 0, nb - 1)), 0))

    return pl.pallas_call(
        body, name=name, grid=(DEPTH, nb),
        in_specs=[part_spec(l) for l in range(DEPTH)] + [spec, spec, spec], out_specs=[spec] * 4,
        out_shape=[jax.ShapeDtypeStruct((DEPTH, B, C_), F32)] * 4,
        compiler_params=_cparams(("arbitrary", "arbitrary")),
    )(*parts, w, m, v)


PACK_W = 1024
BIG = (("w_in", (DEPTH, D_MODEL, N_IN // N_DEV), 2), ("w_conv_out", (DEPTH, CONV_DIM, D_MODEL // N_DEV), 2),
       ("w_uq", (DEPTH, Q_RANK, HEADS * QK_DIM // N_DEV), 2), ("w_ukv", (DEPTH, KV_RANK, HEADS * (NOPE + V_DIM) // N_DEV), 2),
       ("w_attn_out", (DEPTH, HEADS * V_DIM, D_MODEL // N_DEV), 2), ("w_hgrn_out", (DEPTH, 512, D_MODEL // N_DEV), 2),
       ("w_out", (DEPTH, D_MODEL // N_DEV, D_MODEL), 1), ("w_ff1", (DEPTH, D_MODEL, D_FF // N_DEV), 2),
       ("w_ff2", (DEPTH, D_FF // N_DEV, D_MODEL), 1))
SMALL_SHARDED = (("meta", (N_META, D_MODEL // N_DEV), 1), ("conv_w", (DEPTH, CONV_K, CONV_DIM // N_DEV), 2))
REPLICATED = (("norm1_g", (DEPTH, D_MODEL)), ("conv_b", (DEPTH, CONV_DIM)), ("conv_ln_g", (DEPTH, CONV_DIM)),
              ("conv_ln_b", (DEPTH, CONV_DIM)), ("q_a_norm_g", (DEPTH, Q_RANK)), ("kv_a_norm_g", (DEPTH, KV_RANK)),
              ("q_norm_g", (DEPTH, QK_DIM)), ("k_norm_g", (DEPTH, QK_DIM)), ("hgrn_lb_logits", (DEPTH, 512)),
              ("hgrn_norm_g", (DEPTH, 512)), ("norm2_g", (DEPTH, D_MODEL)))
WEIGHT_ORDER = ("meta", "norm1_g", "w_in", "conv_w", "conv_b", "conv_ln_g", "conv_ln_b", "w_conv_out", "q_a_norm_g", "w_uq",
                "kv_a_norm_g", "w_ukv", "q_norm_g", "k_norm_g", "w_attn_out", "hgrn_lb_logits", "hgrn_norm_g", "w_hgrn_out",
                "w_out", "norm2_g", "w_ff1", "w_ff2")


def _rows_for(n_elems, mult):
    rows = -(-n_elems // PACK_W)
    return -(-rows // mult) * mult


def _pack(arrays, dtype, mult, lead=()):
    nl = len(lead)
    flat = jnp.concatenate([a.reshape(lead + (-1,)).astype(dtype) for a in arrays], axis=nl)
    rows = _rows_for(flat.shape[nl], mult)
    flat = jnp.pad(flat, [(0, 0)] * nl + [(0, rows * PACK_W - flat.shape[nl])])
    return flat.reshape(lead + (rows, PACK_W))


def _unpack(pack, shapes, lead=()):
    nl = len(lead)
    flat = pack.reshape(lead + (-1,))
    out, off = [], 0
    for shp in shapes:
        n = int(np.prod(shp))
        out.append(lax.slice_in_dim(flat, off, off + n, axis=nl).reshape(lead + tuple(shp)))
        off += n
    return out


def _join_shards(g, axis):
    g = jnp.moveaxis(g, 0, axis)
    shp = g.shape
    return g.reshape(shp[:axis] + (shp[axis] * shp[axis + 1],) + shp[axis + 2:])


def _cut_shards(a, axis):
    shp = a.shape
    a = a.reshape(shp[:axis] + (N_DEV, shp[axis] // N_DEV) + shp[axis + 1:])
    return jnp.moveaxis(a, axis, 0)


def kernel(x, meta, norm1_g, w_in, conv_w, conv_b, conv_ln_g, conv_ln_b, w_conv_out, q_a_norm_g, w_uq, kv_a_norm_g, w_ukv, q_norm_g, k_norm_g, w_attn_out, hgrn_lb_logits, hgrn_norm_g, w_hgrn_out, w_out, norm2_g, w_ff1, w_ff2, loss_target, m_meta, m_norm1_g, m_w_in, m_conv_w, m_conv_b, m_conv_ln_g, m_conv_ln_b, m_w_conv_out, m_q_a_norm_g, m_w_uq, m_kv_a_norm_g, m_w_ukv, m_q_norm_g, m_k_norm_g, m_w_attn_out, m_hgrn_lb_logits, m_hgrn_norm_g, m_w_hgrn_out, m_w_out, m_norm2_g, m_w_ff1, m_w_ff2, v_meta, v_norm1_g, v_w_in, v_conv_w, v_conv_b, v_conv_ln_g, v_conv_ln_b, v_w_conv_out, v_q_a_norm_g, v_w_uq, v_kv_a_norm_g, v_w_ukv, v_q_norm_g, v_k_norm_g, v_w_attn_out, v_hgrn_lb_logits, v_hgrn_norm_g, v_w_hgrn_out, v_w_out, v_norm2_g, v_w_ff1, v_w_ff2):
    args = dict(locals())
    wts = {n: args[n] for n in WEIGHT_ORDER}
    mom = {n: args["m_" + n] for n in WEIGHT_ORDER}
    var = {n: args["v_" + n] for n in WEIGHT_ORDER}
    xi, yi, ci = _mesh_pos()
    me = 4 * xi + 2 * yi + ci

    shard = lambda l: [wts[n][l].astype(BF16) for n, _, _ in BIG]
    gathered = _all_gather(shard(0) + [_pack([wts[n] for n, _, _ in SMALL_SHARDED], F32, 8)], "gather_layer0")
    small = dict(zip([n for n, _, _ in SMALL_SHARDED],
                     [_join_shards(g, axis) for (_, _, axis), g in
                      zip(SMALL_SHARDED, _unpack(gathered[-1], [s for _, s, _ in SMALL_SHARDED], (N_DEV,)))]))
    pending = _copy_start(shard(1), True, "gather_layer1_start", 5)

    def layer_weights(l, xp):
        full = {n: wts[n][l] for n, _ in REPLICATED}
        if l == 0:
            mats = gathered[:-1]
            full["norm1_g"] = full["norm1_g"] + pending[4][0, 0]
        else:
            own, lands = _copy_wait(pending[0], pending[1], pending[2], pending[3], xp, True, "gather_layer1_wait")
            mats = [lax.dynamic_update_index_in_dim(z, s, me, 0) for z, s in zip(lands, own)]
        full["conv_w"] = small["conv_w"][l]
        for (n, _, axis), g in zip(BIG, mats):
            full[n] = _join_shards(g, axis - 1)
        return full, xp

    big_names = [n for n, _, _ in BIG]
    early = [n for n in big_names if n in ("w_out", "w_ff1", "w_ff2")]
    late = [n for n in big_names if n not in early]
    cut = lambda g, names: [_cut_shards(g[n], axis - 1).astype(BF16) for n, _, axis in BIG if n in names]
    layer_grads = [None] * DEPTH
    flight = {}

    def layer_mid(l, g, w_out):
        if l == 0:
            flight["l0_early"] = _copy_start(cut(g, early), False, "scatter_layer0_early_start", 7)
            w_out = w_out + flight["l0_early"][4][0, 0].astype(w_out.dtype)
        return w_out

    def layer_done(l, g, dx):
        layer_grads[l] = g
        if l == 1:
            flight["l1"] = _copy_start(cut(g, big_names), False, "scatter_l1_start", 6)
            dx = dx + flight["l1"][4][0, 0]
        return dx

    def layer_matrices(l, mats, norm_g):
        if l == 0:
            flight["l0_late"] = _copy_start(cut(mats, late), False, "scatter_l0_late_start", 8)
            norm_g = norm_g + flight["l0_late"][4][0, 0]
        return norm_g

    loss, grad_x, g_meta, g_lb = _run_step(x[0], loss_target[0], small["meta"], wts["hgrn_lb_logits"],
                                           layer_weights, layer_done, layer_mid, layer_matrices)

    grads = {k: jnp.stack([layer_grads[l][k] for l in range(DEPTH)]) for k in layer_grads[0] if k not in big_names}
    grads["hgrn_lb_logits"] = g_lb
    grads["meta"] = g_meta
    small_names = [n for n, _ in REPLICATED] + [n for n, _, _ in SMALL_SHARDED]
    part = _pack([grads[n] for n in small_names] + [loss.reshape(1)], F32, 8)
    flight["small"] = _copy_start([part], True, "gather_small_grads_start", 9)
    started = flight["small"][4]

    def arrive(key, names, after):
        s_sems, r_sems, sent, lands, _ = flight[key]
        sent, lands = _copy_wait(s_sems, r_sems, sent, lands, after, False, f"scatter_{key}_wait")
        return {n: lax.dynamic_update_index_in_dim(z, lax.dynamic_index_in_dim(s, me, 0, keepdims=False), me, 0)
                for n, z, s in zip(names, lands, sent)}

    out = {}

    def update(names, recv0, recv1):
        for n in names:
            res4 = _adamw_layers([recv0[n], recv1[n]], wts[n], mom[n], var[n], "adamw_" + n)
            for kind, a in zip(("grad_", "delta_", "new_m_", "new_v_"), res4):
                out[kind + n] = a

    recv1 = arrive("l1", big_names, started)
    recv0 = arrive("l0_early", early, started)
    update(early, recv0, recv1)

    s_sems, r_sems, sent, lands, _ = flight["small"]
    sent, lands = _copy_wait(s_sems, r_sems, sent, lands, out["grad_" + early[-1]], True, "gather_small_grads_wait")
    total = _sum_parts(lax.dynamic_update_index_in_dim(lands[0], sent[0], me, 0), "sum_small_grads")
    sizes = [grads[n].shape for n in small_names] + [(1,)]
    tot = dict(zip(small_names + ["loss"], _unpack(total, sizes)))
    loss = tot.pop("loss").reshape(())
    mine = {n: tot[n] for n, _ in REPLICATED}
    for n, shp, axis in SMALL_SHARDED:
        mine[n] = lax.dynamic_slice_in_dim(tot[n], me * shp[axis], shp[axis], axis=axis)
    pk = lambda d: _pack([d[n] for n in small_names], F32, 8)
    small_out = _adamw(pk(mine)[None], pk(wts), pk(mom), pk(var), "adamw_vectors")
    for kind, pack in zip(("grad_", "delta_", "new_m_", "new_v_"), small_out):
        for n, a in zip(small_names, _unpack(pack, [wts[n].shape for n in small_names])):
            out[kind + n] = a

    recv0 = arrive("l0_late", late, small_out[0])
    update(late, recv0, recv1)

    res = [loss, grad_x[None]]
    for kind in ("grad_", "delta_", "new_m_", "new_v_"):
        res += [out[kind + n] for n in WEIGHT_ORDER]
    return tuple(res)
```

```python
import functools

import numpy as np
import jax
import jax.numpy as jnp
from jax import lax
from jax.experimental import pallas as pl
from jax.experimental.pallas import tpu as pltpu

F32 = jnp.float32
BF16 = jnp.bfloat16

D_MODEL = 1024
DEPTH = 2
N_META = 16
PAD_FRONT = 112
ROW0 = PAD_FRONT + N_META
EPS = 1e-6
GATE_CLAMP = 1.0 - 1e-6
CONV_DIM = 512
CONV_K = 31
HEADS = 8
Q_RANK = 256
KV_RANK = 128
NOPE = 64
ROPE = 32
V_DIM = 64
QK_DIM = NOPE + ROPE
HEAD_W = 128
ROPE_BASE = 10000.0
HG_HEADS = 4
HG_DK = 128
HG_DV = 128
HG_CHUNK = 64
D_FF = 4096
N_IN = 6560
C_CONV_A, C_CONV_G, C_GATE, C_CQ, C_CKV, C_KR, C_HQ, C_HF, C_HI, C_HG = (
    0, 512, 1024, 4096, 4352, 4480, 4608, 5120, 5632, 6144)
N_IN_P = 6656
O_CQ, O_KR, O_HQ, O_GATE = 1024, 1408, 1440, 3488
KR_LANE = NOPE

ADAM_LR = 0.001
ADAM_B1 = 0.9
ADAM_B2 = 0.999
ADAM_EPS = 1e-08
ADAM_WD = 0.01
ADAM_STEP = 10

N_DEV = 8
VMEM_LIMIT = 56 * 1024 * 1024
MESH = pl.DeviceIdType.MESH


def _pick(n, cands):
    for c in cands:
        if n % c == 0:
            return c
    raise ValueError(f"no tile for {n}")


def _cparams(sem, **kw):
    return pltpu.CompilerParams(dimension_semantics=sem, vmem_limit_bytes=VMEM_LIMIT, **kw)


def _relu2(v):
    return jnp.square(jnp.maximum(v, 0.0))


def _mm(a, b, *, ta=False, tb=False, out_dtype=F32, res=None, a_fn=None, epi=None, name):
    M, K = (a.shape[1], a.shape[0]) if ta else a.shape
    N = b.shape[0] if tb else b.shape[1]
    assert (b.shape[1] if tb else b.shape[0]) == K, (a.shape, b.shape, ta, tb)
    tm = _pick(M, (1056, 1024, 512, 384, 256, 128, 96))
    tn = _pick(N, (1664, 1024, 512, 384, 256, 128))
    tk = _pick(K, (1664, 1056, 1024, 512, 384, 256, 128, 96))
    nk = K // tk
    dims = (((0 if ta else 1,), (1 if tb else 0,)), ((), ()))
    extras = ([res] if res is not None else []) + ([epi[0]] if epi is not None else [])

    def body(*refs):
        a_ref, b_ref = refs[0], refs[1]
        r_ref = refs[2] if res is not None else None
        e_ref = refs[2 + (res is not None)] if epi is not None else None
        o_ref = refs[2 + len(extras)]
        acc = refs[-1] if nk > 1 else None
        k = pl.program_id(2)
        av = a_ref[...]
        if a_fn is not None:
            av = a_fn(av.astype(F32))
        p = lax.dot_general(av.astype(BF16), b_ref[...].astype(BF16), dims, preferred_element_type=F32)

        def finish(total):
            if e_ref is not None:
                total = epi[1](total, e_ref[...].astype(F32))
            if r_ref is not None:
                total = total + r_ref[...].astype(F32)
            o_ref[...] = total.astype(o_ref.dtype)

        if nk == 1:
            finish(p)
        else:
            @pl.when(k == 0)
            def _():
                acc[...] = p

            @pl.when(k > 0)
            def _():
                acc[...] += p

            @pl.when(k == nk - 1)
            def _():
                finish(acc[...])

    a_spec = pl.BlockSpec((tk, tm), lambda i, j, k: (k, i)) if ta else pl.BlockSpec((tm, tk), lambda i, j, k: (i, k))
    b_spec = pl.BlockSpec((tn, tk), lambda i, j, k: (j, k)) if tb else pl.BlockSpec((tk, tn), lambda i, j, k: (k, j))
    o_spec = pl.BlockSpec((tm, tn), lambda i, j, k: (i, j))
    in_specs = [a_spec, b_spec] + [o_spec] * len(extras)
    args = (a, b) + tuple(extras)
    return pl.pallas_call(
        body, name=name, grid=(M // tm, N // tn, nk), in_specs=in_specs, out_specs=o_spec,
        out_shape=jax.ShapeDtypeStruct((M, N), out_dtype),
        scratch_shapes=[pltpu.VMEM((tm, tn), F32)] if nk > 1 else [],
        compiler_params=_cparams(("parallel", "parallel", "arbitrary")),
    )(*args)


class Row:
    def __init__(self, arr, width=None, col=0, piece=None):
        self.arr = arr
        self.width = arr.shape[1] if width is None else width
        assert col % self.width == 0
        self.blk = col // self.width
        self.piece = self.width if piece is None else piece

    def spec(self, tm):
        blk = self.blk
        return pl.BlockSpec((tm, self.width), lambda i: (i, blk))


def _split(v, piece):
    w = v.shape[-1]
    if piece == w:
        return v
    return [v[:, j * piece:(j + 1) * piece] for j in range(w // piece)]


def _store(ref, val, dtype=None):
    if isinstance(val, (list, tuple)):
        piece = val[0].shape[-1]
        for j, p in enumerate(val):
            ref[:, j * piece:(j + 1) * piece] = p.astype(ref.dtype)
    else:
        ref[...] = val.astype(ref.dtype)


def _row_tile(T):
    return _pick(T, (384, 352, 192, 128))


def _param2d(p):
    return p.reshape(1, -1).astype(F32)


def _rowwise(fn, T, rows, params, outs, name):
    tm = _row_tile(T)
    nr, npar = len(rows), len(params)
    par = [(_param2d(p), piece) for p, piece in params]

    def body(*refs):
        rid = pl.program_id(0) * tm + lax.broadcasted_iota(jnp.int32, (tm, 1), 0)
        rv = [_split(refs[n][...].astype(F32), rows[n].piece) for n in range(nr)]
        pv = [_split(refs[nr + n][...], par[n][1]) for n in range(npar)]
        res = fn(rid, rv, pv)
        for n, val in enumerate(res):
            _store(refs[nr + npar + n], val)

    return pl.pallas_call(
        body, name=name, grid=(T // tm,),
        in_specs=[r.spec(tm) for r in rows] + [pl.BlockSpec(p.shape, lambda i: (0, 0)) for p, _ in par],
        out_specs=[pl.BlockSpec((tm, w), lambda i: (i, 0)) for w, _ in outs],
        out_shape=[jax.ShapeDtypeStruct((T, w), dt) for w, dt in outs],
        compiler_params=_cparams(("parallel",)),
    )(*[r.arr for r in rows], *[p for p, _ in par])


def _rowwise_bwd(fn, T, rows, params, cts, drow, name, add=None):
    tm = _row_tile(T)
    nr, npar, nct = len(rows), len(params), len(cts)
    par = [(_param2d(p), piece) for p, piece in params]
    didx = sorted(drow)
    has_add = add is not None

    def body(*refs):
        i = pl.program_id(0)
        rid = i * tm + lax.broadcasted_iota(jnp.int32, (tm, 1), 0)
        rv = [_split(refs[n][...].astype(F32), rows[n].piece) for n in range(nr)]
        pv = [_split(refs[nr + n][...], par[n][1]) for n in range(npar)]
        cv = [_split(refs[nr + npar + n][...].astype(F32), cts[n].piece) for n in range(nct)]
        base = nr + npar + nct + (1 if has_add else 0)
        d_refs = refs[base:base + len(didx)]
        p_refs = refs[base + len(didx):]

        def g(dvals, pvals):
            full = list(rv)
            for n, v in zip(didx, dvals):
                full[n] = v
            return fn(rid, full, pvals)

        _, vjp = jax.vjp(g, [rv[n] for n in didx], pv)
        d_rows, d_pars = vjp(cv)
        for slot, n in enumerate(didx):
            val = d_rows[slot]
            if has_add and add[0] == n:
                assert not isinstance(val, (list, tuple))
                val = val + refs[nr + npar + nct][...].astype(F32)
            _store(d_refs[slot], val)

        @pl.when(i == 0)
        def _():
            for r in p_refs:
                r[...] = jnp.zeros_like(r)

        for r, val in zip(p_refs, d_pars):
            if isinstance(val, (list, tuple)):
                piece = val[0].shape[-1]
                for j, p in enumerate(val):
                    r[:, j * piece:(j + 1) * piece] += p
            else:
                r[...] += val

    in_specs = ([r.spec(tm) for r in rows] + [pl.BlockSpec(p.shape, lambda i: (0, 0)) for p, _ in par]
                + [c.spec(tm) for c in cts])
    args = [r.arr for r in rows] + [p for p, _ in par] + [c.arr for c in cts]
    if has_add:
        in_specs.append(pl.BlockSpec((tm, rows[add[0]].width), lambda i: (i, 0)))
        args.append(add[1])
    out_specs = ([pl.BlockSpec((tm, rows[n].width), lambda i: (i, 0)) for n in didx]
                 + [pl.BlockSpec(p.shape, lambda i: (0, 0)) for p, _ in par])
    out_shape = ([jax.ShapeDtypeStruct((T, rows[n].width), drow[n]) for n in didx]
                 + [jax.ShapeDtypeStruct(p.shape, F32) for p, _ in par])
    res = pl.pallas_call(
        body, name=name, grid=(T // tm,), in_specs=in_specs, out_specs=out_specs, out_shape=out_shape,
        compiler_params=_cparams(("arbitrary",)),
    )(*args)
    return list(res[:len(didx)]), list(res[len(didx):])


def _f_rms(rid, rv, pv):
    x, g = rv[0], pv[0]
    return [x * lax.rsqrt(jnp.mean(x * x, axis=-1, keepdims=True) + EPS) * g]


def _f_glu(rid, rv, pv):
    a, gt = rv
    return [a * jax.nn.sigmoid(gt) * (rid >= PAD_FRONT).astype(F32)]


def _f_lnsilu(rid, rv, pv):
    x = rv[0]
    g, b = pv
    mu = jnp.mean(x, axis=-1, keepdims=True)
    xc = x - mu
    y = xc * lax.rsqrt(jnp.mean(xc * xc, axis=-1, keepdims=True) + EPS) * g + b
    return [y * jax.nn.sigmoid(y)]


@functools.partial(jax.custom_vjp, nondiff_argnums=(1,))
def _lane_roll(x, shift):
    return pltpu.roll(x, shift, 1)


def _lane_roll_fwd(x, shift):
    return pltpu.roll(x, shift, 1), None


def _lane_roll_bwd(shift, _, g):
    return (pltpu.roll(g, (HEAD_W - shift) % HEAD_W, 1),)


_lane_roll.defvjp(_lane_roll_fwd, _lane_roll_bwd)


def _head_norm_rope(xh, g, c, s1, s2):
    y = xh * lax.rsqrt(jnp.sum(xh * xh, axis=-1, keepdims=True) * (1.0 / QK_DIM) + EPS) * g
    half = ROPE // 2
    return y * c + _lane_roll(y, HEAD_W - half) * s1 + _lane_roll(y, half) * s2


def _f_qrope(rid, rv, pv):
    q, c, s1, s2 = rv
    return [[_head_norm_rope(qh, pv[0], c, s1, s2) * ATT_SCALE for qh in q]]


def _f_krope(rid, rv, pv):
    k, kr, c, s1, s2 = rv
    return [[_head_norm_rope(kh + kr, pv[0], c, s1, s2) for kh in k]]


def _f_hgrn_prep(rid, rv, pv):
    hf, hi = rv
    m = (rid >= PAD_FRONT).astype(F32)
    kk = (1.0 - pv[0]) * jax.nn.sigmoid(-hf) * m
    lf = jnp.log1p(-jnp.minimum(kk, GATE_CLAMP))
    vv = hi * jax.nn.sigmoid(hi) * m
    return [kk, lf, vv]


def _f_hgrn_out(rid, rv, pv):
    o, hg = rv
    ng = pv[0]
    out = []
    for oh, gh, nh in zip(o, hg, ng):
        y = oh * lax.rsqrt(jnp.mean(oh * oh, axis=-1, keepdims=True) + EPS) * nh
        out.append(y * (gh * jax.nn.sigmoid(gh)))
    return [out]


def _f_mix(rid, rv, pv):
    g0, g1, g2, ya, yb, yc = rv
    return [jax.nn.sigmoid(g0) * ya + jax.nn.sigmoid(g1) * yb + jax.nn.sigmoid(g2) * yc]


def _f_relu2(rid, rv, pv):
    return [jnp.square(jax.nn.relu(rv[0]))]


def _loss_head(x2, tgt, T):
    tm = _row_tile(T)

    def body(x_ref, t_ref, dx_ref, l_ref):
        i = pl.program_id(0)
        rid = i * tm + lax.broadcasted_iota(jnp.int32, (tm, 1), 0)
        diff = (x_ref[...] - t_ref[...]) * (rid >= ROW0).astype(F32)
        dx_ref[...] = diff * (1.0 / D_MODEL)

        @pl.when(i == 0)
        def _():
            l_ref[...] = jnp.zeros_like(l_ref)

        l_ref[...] += jnp.sum(diff * diff, axis=0, keepdims=True)

    spec = pl.BlockSpec((tm, D_MODEL), lambda i: (i, 0))
    return pl.pallas_call(
        body, name="loss_head", grid=(T // tm,), in_specs=[spec, spec],
        out_specs=[spec, pl.BlockSpec((1, D_MODEL), lambda i: (0, 0))],
        out_shape=[jax.ShapeDtypeStruct((T, D_MODEL), F32), jax.ShapeDtypeStruct((1, D_MODEL), F32)],
        compiler_params=_cparams(("arbitrary",)),
    )(x2, tgt)


HALO = 32


CONV_ROWS = 64


def _conv_lanes():
    return [slice(c, c + 128) for c in range(0, CONV_DIM, 128)]


def _conv_tile(T):
    return _pick(T, (384, 128))


def _conv_fwd(h, w, b, T, name):
    tr = _conv_tile(T)
    ratio = tr // HALO
    wp = jnp.zeros((HALO, CONV_DIM), F32).at[:CONV_K].set(w)

    def body(m_ref, h_ref, w_ref, b_ref, o_ref, win):
        i = pl.program_id(0)
        win[0:HALO, :] = h_ref[...] * (i > 0).astype(F32)
        win[HALO:, :] = m_ref[...]
        for cs in _conv_lanes():
            wv, bv = w_ref[:, cs], b_ref[:, cs]
            for r0 in range(0, tr, CONV_ROWS):
                acc = jnp.broadcast_to(bv, (CONV_ROWS, 128))
                for k in range(CONV_K):
                    acc = acc + wv[k:k + 1] * win[pl.ds(HALO - (CONV_K - 1) + k + r0, CONV_ROWS), cs]
                o_ref[r0:r0 + CONV_ROWS, cs] = acc

    return pl.pallas_call(
        body, name=name, grid=(T // tr,),
        in_specs=[pl.BlockSpec((tr, CONV_DIM), lambda i: (i, 0)),
                  pl.BlockSpec((HALO, CONV_DIM), lambda i: (jnp.maximum(i * ratio - 1, 0), 0)),
                  pl.BlockSpec((HALO, CONV_DIM), lambda i: (0, 0)),
                  pl.BlockSpec((1, CONV_DIM), lambda i: (0, 0))],
        out_specs=pl.BlockSpec((tr, CONV_DIM), lambda i: (i, 0)),
        out_shape=jax.ShapeDtypeStruct((T, CONV_DIM), F32),
        scratch_shapes=[pltpu.VMEM((tr + HALO, CONV_DIM), F32)],
        compiler_params=_cparams(("parallel",)),
    )(h, h, wp, _param2d(b))


def _conv_bwd(h, w, dy, T, name):
    tr = _conv_tile(T)
    ratio = tr // HALO
    n_t = T // tr
    last_halo = T // HALO - 1
    wp = jnp.zeros((HALO, CONV_DIM), F32).at[:CONV_K].set(w)

    def body(hm_ref, hh_ref, dm_ref, dh_ref, w_ref, dx_ref, dw_ref, db_ref, hwin, dwin):
        i = pl.program_id(0)
        hwin[0:HALO, :] = hh_ref[...] * (i > 0).astype(F32)
        hwin[HALO:, :] = hm_ref[...]
        dwin[0:tr, :] = dm_ref[...]
        dwin[tr:, :] = dh_ref[...] * (i < n_t - 1).astype(F32)

        @pl.when(i == 0)
        def _():
            dw_ref[...] = jnp.zeros_like(dw_ref)
            db_ref[...] = jnp.zeros_like(db_ref)

        db_ref[...] += jnp.sum(dm_ref[...], axis=0, keepdims=True)
        fold = lambda a: functools.reduce(jnp.add, [a[r:r + 8] for r in range(0, CONV_ROWS, 8)])
        for cs in _conv_lanes():
            wv = w_ref[:, cs]
            dws = [jnp.zeros((8, 128), F32) for _ in range(CONV_K)]
            for r0 in range(0, tr, CONV_ROWS):
                acc = jnp.zeros((CONV_ROWS, 128), F32)
                for k in range(CONV_K):
                    acc = acc + wv[k:k + 1] * dwin[pl.ds(CONV_K - 1 - k + r0, CONV_ROWS), cs]
                dx_ref[r0:r0 + CONV_ROWS, cs] = acc
                dy_t = dm_ref[r0:r0 + CONV_ROWS, cs]
                for k in range(CONV_K):
                    dws[k] = dws[k] + fold(dy_t * hwin[pl.ds(HALO - (CONV_K - 1) + k + r0, CONV_ROWS), cs])
            for k in range(CONV_K):
                dw_ref[k:k + 1, cs] += jnp.sum(dws[k], axis=0, keepdims=True)

    main = pl.BlockSpec((tr, CONV_DIM), lambda i: (i, 0))
    return pl.pallas_call(
        body, name=name, grid=(n_t,),
        in_specs=[main,
                  pl.BlockSpec((HALO, CONV_DIM), lambda i: (jnp.maximum(i * ratio - 1, 0), 0)),
                  main,
                  pl.BlockSpec((HALO, CONV_DIM), lambda i: (jnp.minimum((i + 1) * ratio, last_halo), 0)),
                  pl.BlockSpec((HALO, CONV_DIM), lambda i: (0, 0))],
        out_specs=[main, pl.BlockSpec((HALO, CONV_DIM), lambda i: (0, 0)), pl.BlockSpec((1, CONV_DIM), lambda i: (0, 0))],
        out_shape=[jax.ShapeDtypeStruct((T, CONV_DIM), F32), jax.ShapeDtypeStruct((HALO, CONV_DIM), F32),
                   jax.ShapeDtypeStruct((1, CONV_DIM), F32)],
        scratch_shapes=[pltpu.VMEM((tr + HALO, CONV_DIM), F32), pltpu.VMEM((tr + HALO, CONV_DIM), F32)],
        compiler_params=_cparams(("arbitrary",)),
    )(h, h, dy, dy, wp)


NEG = -1e30
ATT_SCALE = QK_DIM ** -0.5
_NT = (((1,), (1,)), ((), ()))
_TN = (((0,), (0,)), ((), ()))


def _att_blk(T):
    return _pick(T, (384, 128))


def _att_mask(i, j, blk):
    kpos = j * blk + lax.broadcasted_iota(jnp.int32, (blk, blk), 0)
    qpos = i * blk + lax.broadcasted_iota(jnp.int32, (blk, blk), 1)
    return (kpos <= qpos) & (kpos >= PAD_FRONT)


def _t32(a):
    return a.astype(F32).T.astype(BF16)


def _attn_fwd(q, k, v, T, name):
    blk = _att_blk(T)
    nq = T // blk

    def body(q_ref, k_ref, v_ref, o_ref, lse_ref, vt):
        i = pl.program_id(1)

        @pl.when(i == 0)
        def _():
            def tr(j, c):
                vt[j] = _t32(v_ref[pl.ds(pl.multiple_of(j * blk, blk), blk), :])
                return c

            lax.fori_loop(0, nq, tr, 0)

        qb = q_ref[...]

        def step(js, carry, masked):
            m, l, acc = carry
            ss = []
            for j in js:
                kb = k_ref[pl.ds(pl.multiple_of(j * blk, blk), blk), :]
                s = lax.dot_general(kb, qb, _NT, preferred_element_type=F32)
                ss.append(jnp.where(_att_mask(i, j, blk), s, NEG) if masked else s)
            m_new = m
            for s in ss:
                m_new = jnp.maximum(m_new, jnp.max(s, axis=0, keepdims=True))
            alpha = jnp.exp(m - m_new)
            l = alpha * l
            acc = alpha * acc
            for j, s in zip(js, ss):
                p = jnp.exp(s - m_new)
                l = l + jnp.sum(p, axis=0, keepdims=True)
                acc = acc + jnp.dot(vt[j], p.astype(BF16), preferred_element_type=F32)
            return m_new, l, acc

        init = (jnp.full((1, blk), NEG, F32), jnp.zeros((1, blk), F32), jnp.zeros((HEAD_W, blk), F32))
        later = jnp.minimum(i, 1)
        carry = lax.fori_loop(0, 1 - later, lambda t, c: step([i], c, True), init)
        carry = lax.fori_loop(0, later, lambda t, c: step([i, 0], c, True), carry)
        n_free = jnp.maximum(i - 1, 0)
        carry = lax.fori_loop(0, n_free // 2, lambda t, c: step([1 + 2 * t, 2 + 2 * t], c, False), carry)
        m, l, acc = lax.fori_loop(0, n_free % 2, lambda t, c: step([i - 1], c, False), carry)
        o_ref[...] = (acc / l).T.astype(o_ref.dtype)
        lse_ref[0, 0] = m + jnp.log(l)

    full = pl.BlockSpec((T, HEAD_W), lambda h, i: (0, h))
    return pl.pallas_call(
        body, name=name, grid=(HEADS, nq),
        in_specs=[pl.BlockSpec((blk, HEAD_W), lambda h, i: (i, h)), full, full],
        out_specs=[pl.BlockSpec((blk, HEAD_W), lambda h, i: (i, h)),
                   pl.BlockSpec((1, 1, 1, blk), lambda h, i: (h, i, 0, 0))],
        out_shape=[jax.ShapeDtypeStruct((T, HEADS * HEAD_W), BF16), jax.ShapeDtypeStruct((HEADS, nq, 1, blk), F32)],
        scratch_shapes=[pltpu.VMEM((nq, HEAD_W, blk), BF16)],
        compiler_params=_cparams(("parallel", "arbitrary")),
    )(q, k, v)


def _attn_bwd(q, k, v, o, lse, do, T, name):
    blk = _att_blk(T)
    nq = T // blk

    def body(q_ref, k_ref, v_ref, o_ref, lse_ref, do_ref, dq_ref, dk_ref, dv_ref, delta, dqt, dk_acc, dv_acc):
        j = pl.program_id(1)

        @pl.when(j == 0)
        def _():
            dqt[...] = jnp.zeros_like(dqt)

            def dstep(i, c):
                r0 = pl.multiple_of(i * blk, blk)
                prod = do_ref[pl.ds(r0, blk), :].astype(F32) * o_ref[pl.ds(r0, blk), :].astype(F32)
                delta[i] = jnp.sum(prod.T, axis=0, keepdims=True)
                return c

            lax.fori_loop(0, nq, dstep, 0)

        kb = k_ref[...]
        vb = v_ref[...]
        kbt = _t32(kb)
        dk_acc[...] = jnp.zeros_like(dk_acc)
        dv_acc[...] = jnp.zeros_like(dv_acc)

        def step(qs, masked):
            dvs, dks = [], []
            for i in qs:
                r0 = pl.multiple_of(i * blk, blk)
                qb = q_ref[pl.ds(r0, blk), :]
                dob = do_ref[pl.ds(r0, blk), :]
                s = lax.dot_general(kb, qb, _NT, preferred_element_type=F32)
                p = jnp.exp(s - lse_ref[0, i])
                if masked:
                    p = jnp.where(_att_mask(i, j, blk), p, 0.0)
                dvs.append(jnp.dot(p.astype(BF16), dob, preferred_element_type=F32))
                dp = lax.dot_general(vb, dob, _NT, preferred_element_type=F32)
                ds = (p * (dp - delta[i])).astype(BF16)
                dks.append(jnp.dot(ds, qb, preferred_element_type=F32))
                dqt[i] += jnp.dot(kbt, ds, preferred_element_type=F32)
            dv_acc[...] += functools.reduce(jnp.add, dvs)
            dk_acc[...] += functools.reduce(jnp.add, dks)

        def loop(lo, masked):
            n = nq - lo

            def pair(t, c):
                step([lo + 2 * t, lo + 2 * t + 1], masked)
                return c

            def last(t, c):
                step([nq - 1], masked)
                return c

            lax.fori_loop(0, n // 2, pair, 0)
            lax.fori_loop(0, n % 2, last, 0)

        @pl.when(j == 0)
        def _():
            loop(0, True)

        @pl.when(j > 0)
        def _():
            step([j], True)
            loop(j + 1, False)

        dk_ref[...] = dk_acc[...].astype(dk_ref.dtype)
        dv_ref[...] = dv_acc[...].astype(dv_ref.dtype)

        @pl.when(j == nq - 1)
        def _():
            def wstep(i, c):
                dq_ref[pl.ds(pl.multiple_of(i * blk, blk), blk), :] = dqt[i].T
                return c

            lax.fori_loop(0, nq, wstep, 0)

    full = pl.BlockSpec((T, HEAD_W), lambda h, j: (0, h))
    kblk = pl.BlockSpec((blk, HEAD_W), lambda h, j: (j, h))
    wide = (T, HEADS * HEAD_W)
    return pl.pallas_call(
        body, name=name, grid=(HEADS, nq),
        in_specs=[full, kblk, kblk, full, pl.BlockSpec((1, nq, 1, blk), lambda h, j: (h, 0, 0, 0)), full],
        out_specs=[full, kblk, kblk],
        out_shape=[jax.ShapeDtypeStruct(wide, F32), jax.ShapeDtypeStruct(wide, BF16), jax.ShapeDtypeStruct(wide, BF16)],
        scratch_shapes=[pltpu.VMEM((nq, 1, blk), F32), pltpu.VMEM((nq, HEAD_W, blk), F32),
                        pltpu.VMEM((blk, HEAD_W), F32), pltpu.VMEM((blk, HEAD_W), F32)],
        compiler_params=_cparams(("parallel", "arbitrary")),
    )(q, k, v, o, lse, do)


HG_NB = 6
C = HG_CHUNK
_HI = lax.Precision.HIGHEST


def _tri(lower):
    r = lax.broadcasted_iota(jnp.int32, (C, C), 0)
    c = lax.broadcasted_iota(jnp.int32, (C, C), 1)
    return ((c <= r) if lower else (c >= r)).astype(F32)


HG_SUB = 8
N_SUB = C // HG_SUB


def _hg_split_decay(b, I, rid):
    lo = I * HG_SUB
    r = b[lo:lo + 1]
    eq = jnp.exp(b[lo:lo + HG_SUB] - r)
    ek = jnp.where(rid < lo, jnp.exp(jnp.minimum(r - b, 0.0)), 0.0)
    return eq, ek


def _hg_intra_fwd(q, k, v, b):
    rid = lax.broadcasted_iota(jnp.int32, (C, 1), 0)
    tid = lax.broadcasted_iota(jnp.int32, (HG_SUB, 1), 0)
    a_rows = [jnp.zeros((HG_SUB, C), F32)]
    blocks = []
    for I in range(N_SUB):
        lo = I * HG_SUB
        q_i, b_i = q[lo:lo + HG_SUB], b[lo:lo + HG_SUB]
        if I > 0:
            eq, ek = _hg_split_decay(b, I, rid)
            a_rows.append(lax.dot_general((q_i * eq).astype(BF16), (k * ek).astype(BF16), _NT,
                                          preferred_element_type=F32))
        o_i = jnp.zeros((HG_SUB, HG_DV), F32)
        for s in range(HG_SUB):
            r = lo + s
            e = jnp.exp(jnp.minimum(b_i - b[r:r + 1], 0.0))
            a = jnp.sum(q_i * k[r:r + 1] * e, axis=-1, keepdims=True)
            o_i = o_i + jnp.where(tid >= s, a, 0.0) * v[r:r + 1]
        blocks.append(o_i)
    a_off = jnp.concatenate(a_rows, axis=0).astype(BF16)
    return jnp.dot(a_off, v.astype(BF16), preferred_element_type=F32) + jnp.concatenate(blocks, axis=0)


def _hg_intra_bwd(q, k, v, b, do, dk_s, dv_s):
    rid = lax.broadcasted_iota(jnp.int32, (C, 1), 0)
    tid = lax.broadcasted_iota(jnp.int32, (HG_SUB, 1), 0)
    da_all = lax.dot_general(do.astype(BF16), v.astype(BF16), _NT, preferred_element_type=F32)
    a_rows = [jnp.zeros((HG_SUB, C), F32)]
    dq_blocks = []
    dk = jnp.zeros((C, HG_DK), F32)
    for I in range(N_SUB):
        lo = I * HG_SUB
        q_i, b_i, do_i = q[lo:lo + HG_SUB], b[lo:lo + HG_SUB], do[lo:lo + HG_SUB]
        dq_i = jnp.zeros((HG_SUB, HG_DK), F32)
        if I > 0:
            eq, ek = _hg_split_decay(b, I, rid)
            qs, ks = (q_i * eq).astype(BF16), (k * ek).astype(BF16)
            a_rows.append(lax.dot_general(qs, ks, _NT, preferred_element_type=F32))
            da = da_all[lo:lo + HG_SUB].astype(BF16)
            dq_i = jnp.dot(da, ks, preferred_element_type=F32) * eq
            dk = dk + lax.dot_general(da, qs, _TN, preferred_element_type=F32) * ek
        for s in range(HG_SUB):
            r = lo + s
            e = jnp.where(tid >= s, jnp.exp(jnp.minimum(b_i - b[r:r + 1], 0.0)), 0.0)
            a = jnp.sum(q_i * k[r:r + 1] * e, axis=-1, keepdims=True)
            g = jnp.sum(do_i * v[r:r + 1], axis=-1, keepdims=True) * e
            dq_i = dq_i + g * k[r:r + 1]
            dk_s[r:r + 1, :] = jnp.sum(g * q_i, axis=0, keepdims=True)
            dv_s[r:r + 1, :] = jnp.sum(a * do_i, axis=0, keepdims=True)
        dq_blocks.append(dq_i)
    a_off = jnp.concatenate(a_rows, axis=0).astype(BF16)
    dv = lax.dot_general(a_off, do.astype(BF16), _TN, preferred_element_type=F32)
    return jnp.concatenate(dq_blocks, axis=0), dk + dk_s[...], dv + dv_s[...]


def _hgrn_fwd(u, kk, lf, vv, T, name):
    nb = _pick(T // C, (HG_NB, 3, 2, 1))
    rows = nb * C
    qblk = C_HQ // HG_DK

    def body(q_ref, k_ref, lf_ref, v_ref, o_ref, st_ref, st):
        @pl.when(pl.program_id(1) == 0)
        def _():
            st[...] = jnp.zeros_like(st)

        lower = _tri(True)
        for n in range(nb):
            sl = slice(n * C, (n + 1) * C)
            q, k, v = q_ref[sl, :].astype(F32), k_ref[sl, :], v_ref[sl, :]
            b = jnp.dot(lower, lf_ref[sl, :], precision=_HI, preferred_element_type=F32)
            s_t = st[...]
            st_ref[0, n] = s_t
            qe = (q * jnp.exp(b)).astype(BF16)
            o = lax.dot_general(qe, s_t.astype(BF16), _NT, preferred_element_type=F32)
            o_ref[sl, :] = o + _hg_intra_fwd(q, k, v, b)
            bl = b[C - 1:C, :]
            kd = (k * jnp.exp(bl - b)).astype(BF16)
            st[...] = s_t * jnp.exp(bl) + lax.dot_general(v.astype(BF16), kd, _TN, preferred_element_type=F32)

    col = lambda off: pl.BlockSpec((rows, HG_DK), lambda h, c: (c, h + off))
    return pl.pallas_call(
        body, name=name, grid=(HG_HEADS, T // rows),
        in_specs=[col(qblk), col(0), col(0), col(0)],
        out_specs=[col(0), pl.BlockSpec((1, nb, HG_DV, HG_DK), lambda h, c: (h, c, 0, 0))],
        out_shape=[jax.ShapeDtypeStruct((T, HG_HEADS * HG_DV), F32),
                   jax.ShapeDtypeStruct((HG_HEADS, T // C, HG_DV, HG_DK), F32)],
        scratch_shapes=[pltpu.VMEM((HG_DV, HG_DK), F32)],
        compiler_params=_cparams(("parallel", "arbitrary")),
    )(u, kk, lf, vv)


def _hgrn_bwd(u, kk, lf, vv, states, do, T, name):
    nb = _pick(T // C, (HG_NB, 3, 2, 1))
    rows = nb * C
    n_steps = T // rows
    qblk = C_HQ // HG_DK

    def body(q_ref, k_ref, lf_ref, v_ref, st_ref, do_ref, dq_ref, dk_ref, dlf_ref, dv_ref, dst, dk_s, dv_s):
        @pl.when(pl.program_id(1) == 0)
        def _():
            dst[...] = jnp.zeros_like(dst)

        lower, upper = _tri(True), _tri(False)
        rid = lax.broadcasted_iota(jnp.int32, (C, 1), 0)
        for n in reversed(range(nb)):
            sl = slice(n * C, (n + 1) * C)
            q, k, v, do = q_ref[sl, :].astype(F32), k_ref[sl, :], v_ref[sl, :], do_ref[sl, :]
            b = jnp.dot(lower, lf_ref[sl, :], precision=_HI, preferred_element_type=F32)
            s_t = st_ref[0, n]
            d_new = dst[...]
            eb = jnp.exp(b)
            bl = b[C - 1:C, :]
            ebl = jnp.exp(bl)
            dec = jnp.exp(bl - b)
            qe = q * eb
            kd = k * dec
            do_b = do.astype(BF16)
            dqe = jnp.dot(do_b, s_t.astype(BF16), preferred_element_type=F32)
            dkd = jnp.dot(v.astype(BF16), d_new.astype(BF16), preferred_element_type=F32)
            dv = lax.dot_general(kd.astype(BF16), d_new.astype(BF16), _NT, preferred_element_type=F32)
            dbl = ebl * jnp.sum(d_new * s_t, axis=0, keepdims=True) + jnp.sum(dkd * kd, axis=0, keepdims=True)
            dst[...] = d_new * ebl + lax.dot_general(do_b, qe.astype(BF16), _TN, preferred_element_type=F32)
            dq_in, dk_in, dv_in = _hg_intra_bwd(q, k, v, b, do, dk_s, dv_s)
            dq = dqe * eb + dq_in
            dk = dkd * dec + dk_in
            dv = dv + dv_in
            db = q * dq - k * dk
            db = db + jnp.where(rid == C - 1, dbl, 0.0)
            dq_ref[sl, :] = dq
            dk_ref[sl, :] = dk
            dv_ref[sl, :] = dv
            dlf_ref[sl, :] = jnp.dot(upper, db, precision=_HI, preferred_element_type=F32)

    rev = lambda off: pl.BlockSpec((rows, HG_DK), lambda h, c: (n_steps - 1 - c, h + off))
    return pl.pallas_call(
        body, name=name, grid=(HG_HEADS, n_steps),
        in_specs=[rev(qblk), rev(0), rev(0), rev(0),
                  pl.BlockSpec((1, nb, HG_DV, HG_DK), lambda h, c: (h, n_steps - 1 - c, 0, 0)), rev(0)],
        out_specs=[rev(0)] * 4,
        out_shape=[jax.ShapeDtypeStruct((T, HG_HEADS * HG_DK), F32)] * 4,
        scratch_shapes=[pltpu.VMEM((HG_DV, HG_DK), F32), pltpu.VMEM((C, HG_DK), F32), pltpu.VMEM((C, HG_DV), F32)],
        compiler_params=_cparams(("parallel", "arbitrary")),
    )(u, kk, lf, vv, states, do)


def _rope_tables(T):
    half = ROPE // 2
    inv_freq = (ROPE_BASE ** (-np.arange(half, dtype=np.float32) / half)).astype(np.float32)
    row = lambda lo, hi, val: np.concatenate([np.zeros(lo, np.float32), np.asarray(val, np.float32) * np.ones(hi - lo, np.float32),
                                              np.zeros(HEAD_W - hi, np.float32)])[None, :]
    freq = row(NOPE, NOPE + half, inv_freq) + row(NOPE + half, NOPE + ROPE, inv_freq)
    pos = lax.broadcasted_iota(jnp.int32, (T, HEAD_W), 0).astype(F32) - float(PAD_FRONT)
    ang = pos * freq
    cos, sin = jnp.cos(ang), jnp.sin(ang)
    c = cos * row(NOPE, NOPE + ROPE, 1.0) + row(0, NOPE, 1.0)
    s1 = sin * row(NOPE, NOPE + half, -1.0)
    s2 = sin * row(NOPE + half, NOPE + ROPE, 1.0)
    return c, s1, s2


def _layer_fwd(x, w, tabs, T, l):
    c, s1, s2 = tabs
    n = lambda s: f"l{l}_{s}"
    sv = {"x": x}
    h = _rowwise(_f_rms, T, [Row(x)], [(w["norm1_g"], D_MODEL)], [(D_MODEL, BF16)], n("norm1"))[0]
    u = _mm(h, w["w_in"], out_dtype=BF16, name=n("in_proj"))
    sv.update(h=h, u=u)
    hglu = _rowwise(_f_glu, T, [Row(u, 512, C_CONV_A), Row(u, 512, C_CONV_G)], [], [(CONV_DIM, F32)], n("glu"))[0]
    cv = _conv_fwd(hglu, w["conv_w"], w["conv_b"], T, n("conv"))
    hc = _rowwise(_f_lnsilu, T, [Row(cv)], [(w["conv_ln_g"], CONV_DIM), (w["conv_ln_b"], CONV_DIM)],
                  [(CONV_DIM, BF16)], n("conv_ln"))[0]
    y_a = _mm(hc, w["w_conv_out"], out_dtype=BF16, name=n("conv_out"))
    sv.update(hglu=hglu, cv=cv, hc=hc, y_a=y_a)
    cqn = _rowwise(_f_rms, T, [Row(u, Q_RANK, C_CQ)], [(w["q_a_norm_g"], Q_RANK)], [(Q_RANK, BF16)], n("q_a_norm"))[0]
    ckvn = _rowwise(_f_rms, T, [Row(u, KV_RANK, C_CKV)], [(w["kv_a_norm_g"], KV_RANK)], [(KV_RANK, BF16)], n("kv_a_norm"))[0]
    q_raw = _mm(cqn, w["w_uq"], out_dtype=BF16, name=n("uq"))
    k_raw = _mm(ckvn, w["w_uk"], out_dtype=BF16, name=n("uk"))
    v = _mm(ckvn, w["w_uv"], out_dtype=BF16, name=n("uv"))
    tab_rows = [Row(c), Row(s1), Row(s2)]
    q = _rowwise(_f_qrope, T, [Row(q_raw, piece=HEAD_W)] + tab_rows, [(w["q_norm_g"], HEAD_W)],
                 [(HEADS * HEAD_W, BF16)], n("q_rope"))[0]
    k = _rowwise(_f_krope, T, [Row(k_raw, piece=HEAD_W), Row(u, HEAD_W, C_KR)] + tab_rows, [(w["k_norm_g"], HEAD_W)],
                 [(HEADS * HEAD_W, BF16)], n("k_rope"))[0]
    o, lse = _attn_fwd(q, k, v, T, n("attn"))
    y_b = _mm(o, w["w_attn_out"], out_dtype=BF16, name=n("attn_out"))
    sv.update(cqn=cqn, ckvn=ckvn, q_raw=q_raw, k_raw=k_raw, v=v, q=q, k=k, o=o, lse=lse, y_b=y_b)
    kk, lf, vv = _rowwise(_f_hgrn_prep, T, [Row(u, 512, C_HF), Row(u, 512, C_HI)], [(w["lb"], 512)],
                          [(512, F32)] * 3, n("hgrn_prep"))
    o_h, states = _hgrn_fwd(u, kk, lf, vv, T, n("hgrn"))
    oh = _rowwise(_f_hgrn_out, T, [Row(o_h, piece=HG_DV), Row(u, 512, C_HG, piece=HG_DV)], [(w["hgrn_norm_g"], HG_DV)],
                  [(512, BF16)], n("hgrn_out_norm"))[0]
    y_c = _mm(oh, w["w_hgrn_out"], out_dtype=BF16, name=n("hgrn_out"))
    sv.update(kk=kk, lf=lf, vv=vv, o_h=o_h, states=states, oh=oh, y_c=y_c)
    gate_rows = [Row(u, D_MODEL, C_GATE + g * D_MODEL) for g in range(3)]
    mix = _rowwise(_f_mix, T, gate_rows + [Row(y_a), Row(y_b), Row(y_c)], [], [(D_MODEL, BF16)], n("mix"))[0]
    x1 = _mm(mix, w["w_out"], res=x, name=n("out_proj"))
    h2 = _rowwise(_f_rms, T, [Row(x1)], [(w["norm2_g"], D_MODEL)], [(D_MODEL, BF16)], n("norm2"))[0]
    f = _mm(h2, w["w_ff1"], out_dtype=BF16, name=n("ff1"))
    x2 = _mm(f, w["w_ff2"], res=x1, a_fn=_relu2, name=n("ff2"))
    sv.update(mix=mix, x1=x1, h2=h2, f=f)
    return x2, sv


def _layer_bwd(dx2, w, sv, tabs, T, l, mid=None, matrices=None):
    c, s1, s2 = tabs
    n = lambda s: f"l{l}_b_{s}"
    u = sv["u"]
    g = {}
    g["w_ff2"] = _mm(sv["f"], dx2, ta=True, a_fn=_relu2, out_dtype=BF16, name=n("dw_ff2"))
    df = _mm(dx2, w["w_ff2"], tb=True, out_dtype=BF16, name=n("d_f"),
             epi=(sv["f"], lambda d, fv: d * (2.0 * jnp.maximum(fv, 0.0))))
    g["w_ff1"] = _mm(sv["h2"], df, ta=True, out_dtype=BF16, name=n("dw_ff1"))
    dh2 = _mm(df, w["w_ff1"], tb=True, name=n("d_h2"))
    (dx1,), (g["norm2_g"],) = _rowwise_bwd(_f_rms, T, [Row(sv["x1"])], [(w["norm2_g"], D_MODEL)], [Row(dh2)],
                                           {0: F32}, n("norm2"), add=(0, dx2))
    g["w_out"] = _mm(sv["mix"], dx1, ta=True, out_dtype=BF16, name=n("dw_out"))
    w_out = w["w_out"] if mid is None else mid(g, w["w_out"])
    dmix = _mm(dx1, w_out, tb=True, out_dtype=BF16, name=n("d_mix"))
    gate_rows = [Row(u, D_MODEL, C_GATE + i * D_MODEL) for i in range(3)]
    (dg0, dg1, dg2, dy_a, dy_b, dy_c), _ = _rowwise_bwd(
        _f_mix, T, gate_rows + [Row(sv["y_a"]), Row(sv["y_b"]), Row(sv["y_c"])], [], [Row(dmix)],
        {0: BF16, 1: BF16, 2: BF16, 3: BF16, 4: BF16, 5: BF16}, n("mix"))
    g["w_hgrn_out"] = _mm(sv["oh"], dy_c, ta=True, out_dtype=BF16, name=n("dw_hgrn_out"))
    doh = _mm(dy_c, w["w_hgrn_out"], tb=True, out_dtype=BF16, name=n("d_oh"))
    (do_h, dhg), (g["hgrn_norm_g"],) = _rowwise_bwd(
        _f_hgrn_out, T, [Row(sv["o_h"], piece=HG_DV), Row(u, 512, C_HG, piece=HG_DV)], [(w["hgrn_norm_g"], HG_DV)],
        [Row(doh, piece=HG_DV)], {0: F32, 1: BF16}, n("hgrn_out_norm"))
    dhq, dkk, dlf, dvv = _hgrn_bwd(u, sv["kk"], sv["lf"], sv["vv"], sv["states"], do_h, T, n("hgrn"))
    (dhf, dhi), (g["lb"],) = _rowwise_bwd(
        _f_hgrn_prep, T, [Row(u, 512, C_HF), Row(u, 512, C_HI)], [(w["lb"], 512)],
        [Row(dkk), Row(dlf), Row(dvv)], {0: BF16, 1: BF16}, n("hgrn_prep"))
    g["w_attn_out"] = _mm(sv["o"], dy_b, ta=True, out_dtype=BF16, name=n("dw_attn_out"))
    do = _mm(dy_b, w["w_attn_out"], tb=True, out_dtype=BF16, name=n("d_o"))
    dq, dk, dv = _attn_bwd(sv["q"], sv["k"], sv["v"], sv["o"], sv["lse"], do, T, n("attn"))
    tab_rows = [Row(c), Row(s1), Row(s2)]
    (dq_raw,), (g["q_norm_g"],) = _rowwise_bwd(
        _f_qrope, T, [Row(sv["q_raw"], piece=HEAD_W)] + tab_rows, [(w["q_norm_g"], HEAD_W)],
        [Row(dq, piece=HEAD_W)], {0: BF16}, n("q_rope"))
    (dk_raw, dkr), (g["k_norm_g"],) = _rowwise_bwd(
        _f_krope, T, [Row(sv["k_raw"], piece=HEAD_W), Row(u, HEAD_W, C_KR)] + tab_rows, [(w["k_norm_g"], HEAD_W)],
        [Row(dk, piece=HEAD_W)], {0: BF16, 1: BF16}, n("k_rope"))
    g["w_uq"] = _mm(sv["cqn"], dq_raw, ta=True, out_dtype=BF16, name=n("dw_uq"))
    g["w_uk"] = _mm(sv["ckvn"], dk_raw, ta=True, out_dtype=BF16, name=n("dw_uk"))
    g["w_uv"] = _mm(sv["ckvn"], dv, ta=True, out_dtype=BF16, name=n("dw_uv"))
    dcqn = _mm(dq_raw, w["w_uq"], tb=True, out_dtype=BF16, name=n("d_cqn"))
    dckvn = _mm(dk_raw, w["w_uk"], tb=True, name=n("d_ckvn_k"))
    dckvn = _mm(dv, w["w_uv"], tb=True, res=dckvn, out_dtype=BF16, name=n("d_ckvn_v"))
    (dcq,), (g["q_a_norm_g"],) = _rowwise_bwd(_f_rms, T, [Row(u, Q_RANK, C_CQ)], [(w["q_a_norm_g"], Q_RANK)],
                                              [Row(dcqn)], {0: BF16}, n("q_a_norm"))
    (dckv,), (g["kv_a_norm_g"],) = _rowwise_bwd(_f_rms, T, [Row(u, KV_RANK, C_CKV)], [(w["kv_a_norm_g"], KV_RANK)],
                                                [Row(dckvn)], {0: BF16}, n("kv_a_norm"))
    g["w_conv_out"] = _mm(sv["hc"], dy_a, ta=True, out_dtype=BF16, name=n("dw_conv_out"))
    dhc = _mm(dy_a, w["w_conv_out"], tb=True, out_dtype=BF16, name=n("d_hc"))
    (dcv,), (g["conv_ln_g"], g["conv_ln_b"]) = _rowwise_bwd(
        _f_lnsilu, T, [Row(sv["cv"])], [(w["conv_ln_g"], CONV_DIM), (w["conv_ln_b"], CONV_DIM)], [Row(dhc)],
        {0: F32}, n("conv_ln"))
    dhglu, dconv_w, g["conv_b"] = _conv_bwd(sv["hglu"], w["conv_w"], dcv, T, n("conv"))
    g["conv_w"] = dconv_w[:CONV_K]
    (dua, dug), _ = _rowwise_bwd(_f_glu, T, [Row(u, 512, C_CONV_A), Row(u, 512, C_CONV_G)], [], [Row(dhglu)],
                                 {0: BF16, 1: BF16}, n("glu"))
    du = jnp.concatenate([dua, dug, dg0, dg1, dg2, dcq, dckv, dkr, dhq.astype(BF16), dhf, dhi, dhg], axis=1)
    small = ("w_uq", "w_uk", "w_uv", "w_attn_out", "w_hgrn_out", "w_conv_out")
    du, *done = lax.optimization_barrier((du, *[g[k] for k in small]))
    g.update(zip(small, done))
    g["w_in"] = _mm(sv["h"], du, ta=True, out_dtype=BF16, name=n("dw_in"))
    norm_g = w["norm1_g"] if matrices is None else matrices(g, w["norm1_g"])
    du, norm_g = lax.optimization_barrier((du, norm_g))
    dh = _mm(du, w["w_in"], tb=True, name=n("d_h"))
    (dx,), (g["norm1_g"],) = _rowwise_bwd(_f_rms, T, [Row(sv["x"])], [(norm_g, D_MODEL)], [Row(dh)],
                                          {0: F32}, n("norm1"), add=(0, dx1))
    return dx, g


def _pad_w_in(w_in):
    z = lambda k: jnp.zeros((w_in.shape[0], k), w_in.dtype)
    return jnp.concatenate([w_in[:, :O_CQ], w_in[:, O_GATE:], w_in[:, O_CQ:O_KR], z(KR_LANE), w_in[:, O_KR:O_HQ],
                            z(HEAD_W - KR_LANE - ROPE), w_in[:, O_HQ:O_GATE]], axis=1)


def _unpad_w_in(g):
    return jnp.concatenate([g[:, :C_GATE], g[:, C_CQ:C_KR], g[:, C_KR + KR_LANE:C_KR + KR_LANE + ROPE],
                            g[:, C_HQ:], g[:, C_GATE:C_CQ]], axis=1)


_W_IN_RUNS = ((0, 0, O_CQ), (O_CQ, C_CQ, O_KR - O_CQ), (O_KR, C_KR + KR_LANE, ROPE), (O_HQ, C_HQ, O_GATE - O_HQ),
              (O_GATE, C_GATE, N_IN - O_GATE))


def _w_in_from_shards(g8):
    per = N_IN // N_DEV
    pieces, at = [], 0
    for o0, p0, n in sorted(_W_IN_RUNS, key=lambda r: r[1]):
        if p0 > at:
            pieces.append(jnp.zeros((g8.shape[1], p0 - at), g8.dtype))
        for j in range(o0 // per, (o0 + n - 1) // per + 1):
            lo, hi = max(o0, j * per), min(o0 + n, (j + 1) * per)
            pieces.append(g8[j][:, lo - j * per:hi - j * per])
        at = p0 + n
    if at < N_IN_P:
        pieces.append(jnp.zeros((g8.shape[1], N_IN_P - at), g8.dtype))
    return jnp.concatenate(pieces, axis=1)


def _w_in_grad_shards(g):
    per = N_IN // N_DEV
    shards = []
    for j in range(N_DEV):
        lo, hi = j * per, (j + 1) * per
        pieces = [g[:, p0 + max(lo, o0) - o0:p0 + min(hi, o0 + n) - o0]
                  for o0, p0, n in _W_IN_RUNS if max(lo, o0) < min(hi, o0 + n)]
        shards.append(jnp.concatenate(pieces, axis=1) if len(pieces) > 1 else pieces[0])
    return jnp.stack(shards)


def _pad_heads(wm, per_head, lo, hi):
    lead = wm.shape[:-1]
    wh = wm.reshape(lead + (HEADS, per_head))[..., lo:hi]
    pad = [(0, 0)] * len(lead) + [(0, 0), (0, HEAD_W - (hi - lo))]
    return jnp.pad(wh, pad).reshape(lead + (HEADS * HEAD_W,))


def _unpad_heads(gm, width):
    lead = gm.shape[:-1]
    return gm.reshape(lead + (HEADS, HEAD_W))[..., :width]


def _layer_weights(full, lb):
    w = {}
    w["norm1_g"] = full["norm1_g"]
    w["w_in"] = full["w_in_padded"] if "w_in_padded" in full else _pad_w_in(full["w_in"])
    w["conv_w"] = full["conv_w"]
    w["conv_b"] = full["conv_b"]
    w["conv_ln_g"] = full["conv_ln_g"]
    w["conv_ln_b"] = full["conv_ln_b"]
    w["w_conv_out"] = full["w_conv_out"]
    w["q_a_norm_g"] = full["q_a_norm_g"]
    w["w_uq"] = _pad_heads(full["w_uq"], QK_DIM, 0, QK_DIM)
    w["kv_a_norm_g"] = full["kv_a_norm_g"]
    w["w_uk"] = _pad_heads(full["w_ukv"], NOPE + V_DIM, 0, NOPE)
    w["w_uv"] = _pad_heads(full["w_ukv"], NOPE + V_DIM, NOPE, NOPE + V_DIM)
    w["q_norm_g"] = jnp.pad(full["q_norm_g"], (0, HEAD_W - QK_DIM))
    w["k_norm_g"] = jnp.pad(full["k_norm_g"], (0, HEAD_W - QK_DIM))
    wa = full["w_attn_out"].reshape(HEADS, V_DIM, D_MODEL)
    w["w_attn_out"] = jnp.pad(wa, ((0, 0), (0, HEAD_W - V_DIM), (0, 0))).reshape(HEADS * HEAD_W, D_MODEL)
    w["lb"] = lb
    w["hgrn_norm_g"] = full["hgrn_norm_g"]
    w["w_hgrn_out"] = full["w_hgrn_out"]
    w["w_out"] = full["w_out"]
    w["norm2_g"] = full["norm2_g"]
    w["w_ff1"] = full["w_ff1"]
    w["w_ff2"] = full["w_ff2"]
    return w


def _matrix_grads_to_original(g):
    o = {name: g[name] for name in ("w_conv_out", "w_hgrn_out", "w_out", "w_ff1", "w_ff2")}
    o["w_in"] = _unpad_w_in(g["w_in"])
    o["w_in_shards"] = _w_in_grad_shards(g["w_in"])
    o["w_uq"] = _unpad_heads(g["w_uq"], QK_DIM).reshape(Q_RANK, HEADS * QK_DIM)
    guk = _unpad_heads(g["w_uk"], NOPE)
    guv = _unpad_heads(g["w_uv"], V_DIM)
    o["w_ukv"] = jnp.concatenate([guk, guv], axis=-1).reshape(KV_RANK, HEADS * (NOPE + V_DIM))
    o["w_attn_out"] = g["w_attn_out"].reshape(HEADS, HEAD_W, D_MODEL)[:, :V_DIM].reshape(HEADS * V_DIM, D_MODEL)
    return o


def _vector_grads_to_original(g):
    o = {"conv_w": g["conv_w"]}
    for name in ("norm1_g", "conv_b", "conv_ln_g", "conv_ln_b", "q_a_norm_g", "kv_a_norm_g", "hgrn_norm_g", "norm2_g", "lb"):
        o[name] = g[name].reshape(-1)
    o["q_norm_g"] = g["q_norm_g"].reshape(-1)[:QK_DIM]
    o["k_norm_g"] = g["k_norm_g"].reshape(-1)[:QK_DIM]
    return o


def _lower_bounds(logits):
    p = jax.nn.softmax(logits.astype(F32), axis=0)
    return jnp.cumsum(p, axis=0) - p[0:1]


def _run_step(x, target, meta, lb_logits, layer_weights, layer_done, layer_mid=None, layer_matrices=None):
    seq = x.shape[0]
    T = ROW0 + seq
    assert T % 128 == 0
    tabs = _rope_tables(T)
    lbs, lb_vjp = jax.vjp(_lower_bounds, lb_logits)
    xp = jnp.concatenate([jnp.zeros((PAD_FRONT, D_MODEL), F32), meta.astype(F32), x], axis=0)
    tp = jnp.concatenate([jnp.zeros((ROW0, D_MODEL), F32), target], axis=0)
    ws, svs = [], []
    for l in range(DEPTH):
        full, xp = layer_weights(l, xp)
        w = _layer_weights(full, lbs[l])
        xp, sv = _layer_fwd(xp, w, tabs, T, l)
        ws.append(w)
        svs.append(sv)
    dx, sq = _loss_head(xp, tp, T)
    loss = 0.5 * jnp.sum(sq) * (1.0 / D_MODEL)
    dlb = [None] * DEPTH
    for l in reversed(range(DEPTH)):
        mid = None if layer_mid is None else functools.partial(layer_mid, l)
        mats = {}

        def matrices(g, norm_g, l=l, mats=mats):
            mats.update(_matrix_grads_to_original(g))
            return norm_g if layer_matrices is None else layer_matrices(l, mats, norm_g)

        dx, g = _layer_bwd(dx, ws[l], svs[l], tabs, T, l, mid, matrices)
        g = {**_vector_grads_to_original(g), **mats}
        dlb[l] = g.pop("lb")
        dx = layer_done(l, g, dx)
    return loss, dx[ROW0:], dx[PAD_FRONT:ROW0], lb_vjp(jnp.stack(dlb))[0]


def _local_step(x, target, full):
    per_layer = [None] * DEPTH

    def done(l, g, dx):
        per_layer[l] = g
        return dx

    loss, gx, gmeta, glb = _run_step(
        x, target, full["meta"], full["hgrn_lb_logits"],
        lambda l, xp: ({k: v[l] for k, v in full.items() if k != "meta"}, xp), done)
    grads = {k: jnp.stack([per_layer[l][k] for l in range(DEPTH)]) for k in per_layer[0]}
    grads["hgrn_lb_logits"] = glb
    grads["meta"] = gmeta
    return loss, gx, grads


def _mesh_pos():
    return lax.axis_index("x"), lax.axis_index("y"), lax.axis_index("c")


N_COPY = N_DEV - 1


def _all_gather(arrs, name):
    n = len(arrs)

    def body(*refs):
        x_refs, out_refs = refs[:n], refs[n:2 * n]
        send_sems, recv_sems, local_sems = refs[2 * n:]
        x, y, c = _mesh_pos()
        me, sibling = (x, y, c), (x, y, 1 - c)
        chips = [(1 - x, y), (x, 1 - y), (1 - x, 1 - y)]

        def slot(a, px, py, pc):
            return out_refs[a].at[4 * px + 2 * py + pc]

        def copy(a, k, block, to, own=False):
            return pltpu.make_async_remote_copy(
                src_ref=x_refs[a] if own else slot(a, *block), dst_ref=slot(a, *block),
                send_sem=send_sems.at[a * N_COPY + k], recv_sem=recv_sems.at[a * N_COPY + k],
                device_id=to, device_id_type=MESH)

        mine = [pltpu.make_async_copy(x_refs[a], slot(a, *me), local_sems.at[a]) for a in range(n)]
        for cp in mine:
            cp.start()
        first = []
        for a in range(n):
            first.append(copy(a, 0, me, sibling, own=True))
            first += [copy(a, 1 + j, me, (*chip, c), own=True) for j, chip in enumerate(chips)]
        for cp in first:
            cp.start()
        passed = []
        for j, chip in enumerate(chips):
            for a in range(n):
                copy(a, 1 + j, (*chip, c), me).wait_recv()
                cp = copy(a, 4 + j, (*chip, c), sibling)
                cp.start()
                passed.append(cp)
        for a in range(n):
            copy(a, 0, sibling, me).wait_recv()
            for j, chip in enumerate(chips):
                copy(a, 4 + j, (*chip, 1 - c), me).wait_recv()
        for cp in first + passed:
            cp.wait_send()
        for cp in mine:
            cp.wait()

    anyspec = pl.BlockSpec(memory_space=pl.ANY)
    return pl.pallas_call(
        body, name=name, out_shape=[jax.ShapeDtypeStruct((N_DEV,) + a.shape, a.dtype) for a in arrs],
        in_specs=[anyspec] * n, out_specs=[anyspec] * n,
        scratch_shapes=[pltpu.SemaphoreType.DMA((n * N_COPY,)), pltpu.SemaphoreType.DMA((n * N_COPY,)),
                        pltpu.SemaphoreType.DMA((n,))],
    )(*arrs)


def _exchange(arrs, name):
    n = len(arrs)

    def body(*refs):
        s_refs, r_refs = refs[:n], refs[n:2 * n]
        send_sems, recv_sems, local_sems = refs[2 * n:]
        x, y, c = _mesh_pos()
        me = 4 * x + 2 * y + c
        local = [pltpu.make_async_copy(s_refs[a].at[me], r_refs[a].at[me], local_sems.at[a]) for a in range(n)]
        for cp in local:
            cp.start()
        sends, recvs = [], []
        for rel in range(1, N_DEV):
            px = 1 - x if rel & 4 else x
            py = 1 - y if rel & 2 else y
            pc = 1 - c if rel & 1 else c
            p = 4 * px + 2 * py + pc
            for a in range(n):
                k = a * N_COPY + rel - 1
                sends.append(pltpu.make_async_remote_copy(
                    src_ref=s_refs[a].at[p], dst_ref=r_refs[a].at[me], send_sem=send_sems.at[k],
                    recv_sem=recv_sems.at[k], device_id=(px, py, pc), device_id_type=MESH))
                recvs.append(pltpu.make_async_remote_copy(
                    src_ref=s_refs[a].at[me], dst_ref=r_refs[a].at[p], send_sem=send_sems.at[k],
                    recv_sem=recv_sems.at[k], device_id=(px, py, pc), device_id_type=MESH))
        for cp in sends:
            cp.start()
        for cp in recvs:
            cp.wait_recv()
        for cp in sends:
            cp.wait_send()
        for cp in local:
            cp.wait()

    anyspec = pl.BlockSpec(memory_space=pl.ANY)
    return pl.pallas_call(
        body, name=name, out_shape=[jax.ShapeDtypeStruct(a.shape, a.dtype) for a in arrs],
        in_specs=[anyspec] * n, out_specs=[anyspec] * n,
        scratch_shapes=[pltpu.SemaphoreType.DMA((n * N_COPY,)), pltpu.SemaphoreType.DMA((n * N_COPY,)),
                        pltpu.SemaphoreType.DMA((n,))],
    )(*arrs)


_HBM = pl.BlockSpec(memory_space=pltpu.HBM)
_SEM = pl.BlockSpec(memory_space=pltpu.SEMAPHORE)
_EFFECT = pltpu.SideEffectType.DATAFLOW_SIDE_EFFECTING


def _peers(x, y, c):
    out = []
    for rel in range(1, N_DEV):
        px = 1 - x if rel & 4 else x
        py = 1 - y if rel & 2 else y
        pc = 1 - c if rel & 1 else c
        out.append((rel, (px, py, pc), 4 * px + 2 * py + pc))
    return out


def _split_copies(src_refs, land_refs, send_sems, recv_sems, gather):
    x, y, c = _mesh_pos()
    me = 4 * x + 2 * y + c
    out = []
    for a, (src, land) in enumerate(zip(src_refs, land_refs)):
        for rel, peer, p in _peers(x, y, c):
            k = a * N_COPY + rel - 1
            mk = lambda s, d: pltpu.make_async_remote_copy(
                src_ref=s, dst_ref=d, send_sem=send_sems.at[k], recv_sem=recv_sems.at[k],
                device_id=peer, device_id_type=MESH)
            mine = src if gather else src.at[p]
            out.append((mk(mine, land.at[me]), mk(mine, land.at[p])))
    return out


def _copy_start(srcs, gather, name, collective_id):
    n = len(srcs)
    lands = [lax.empty(((N_DEV,) + s.shape) if gather else s.shape, s.dtype) for s in srcs]

    def body(*refs):
        src_refs, land_refs = refs[:n], refs[n:2 * n]
        send_sems, recv_sems = refs[2 * n], refs[2 * n + 1]
        token = refs[-1]
        x, y, c = _mesh_pos()
        barrier = pltpu.get_barrier_semaphore()
        for _, peer, _ in _peers(x, y, c):
            pl.semaphore_signal(barrier, inc=1, device_id=peer, device_id_type=MESH)
        pl.semaphore_wait(barrier, N_COPY)
        for out_copy, _ in _split_copies(src_refs, land_refs, send_sems, recv_sems, gather):
            out_copy.start()
        token[...] = jnp.zeros_like(token)

    hbm = lambda a: pltpu.HBM(a.shape, a.dtype)
    res = pl.pallas_call(
        body, name=name,
        out_shape=(pltpu.SemaphoreType.DMA((n * N_COPY,)), pltpu.SemaphoreType.DMA((n * N_COPY,)),
                   *[hbm(s) for s in srcs], *[hbm(z) for z in lands], jax.ShapeDtypeStruct((8, 128), F32)),
        in_specs=[_HBM] * (2 * n), out_specs=(_SEM, _SEM, *([_HBM] * (2 * n)), pl.BlockSpec(memory_space=pltpu.VMEM)),
        input_output_aliases={i: 2 + i for i in range(2 * n)},
        compiler_params=pltpu.CompilerParams(has_side_effects=_EFFECT, collective_id=collective_id),
    )(*[pltpu.with_memory_space_constraint(s, pltpu.HBM) for s in srcs],
      *[pltpu.with_memory_space_constraint(z, pltpu.HBM) for z in lands])
    return res[0], res[1], list(res[2:2 + n]), list(res[2 + n:2 + 2 * n]), res[-1]


def _copy_wait(send_sems, recv_sems, srcs, lands, after, gather, name):
    n = len(srcs)

    def body(*refs):
        src_refs, land_refs = refs[:n], refs[n:2 * n]
        s_sems, r_sems = refs[2 * n], refs[2 * n + 1]
        for out_copy, in_copy in _split_copies(src_refs, land_refs, s_sems, r_sems, gather):
            out_copy.wait_send()
            in_copy.wait_recv()

    hbm = lambda a: pltpu.HBM(a.shape, a.dtype)
    res = pl.pallas_call(
        body, name=name, out_shape=(*[hbm(s) for s in srcs], *[hbm(z) for z in lands]),
        in_specs=[_HBM] * (2 * n) + [_SEM, _SEM, pl.BlockSpec(memory_space=pl.ANY)], out_specs=tuple([_HBM] * (2 * n)),
        input_output_aliases={i: i for i in range(2 * n)},
        compiler_params=pltpu.CompilerParams(has_side_effects=_EFFECT),
    )(*srcs, *lands, send_sems, recv_sems, after)
    return list(res[:n]), list(res[n:])


def _sum_parts(parts, name):
    P, R, W = parts.shape

    def body(p_ref, o_ref):
        g = p_ref[0].astype(F32)
        for i in range(1, P):
            g = g + p_ref[i].astype(F32)
        o_ref[...] = g

    return pl.pallas_call(body, name=name, out_shape=jax.ShapeDtypeStruct((R, W), F32))(parts)


def _adamw_body(p_ref, w_ref, m_ref, v_ref, g_ref, d_ref, nm_ref, nv_ref):
    g = p_ref[0].astype(F32)
    for i in range(1, p_ref.shape[0]):
        g = g + p_ref[i].astype(F32)
    m_new = ADAM_B1 * m_ref[...] + (1.0 - ADAM_B1) * g
    v_new = ADAM_B2 * v_ref[...] + (1.0 - ADAM_B2) * jnp.square(g)
    m_hat = m_new / (1.0 - ADAM_B1 ** ADAM_STEP)
    v_hat = v_new / (1.0 - ADAM_B2 ** ADAM_STEP)
    g_ref[...] = g
    d_ref[...] = -ADAM_LR * (m_hat / (jnp.sqrt(v_hat) + ADAM_EPS) + ADAM_WD * w_ref[...])
    nm_ref[...] = m_new
    nv_ref[...] = v_new


def _adamw(parts, w, m, v, name):
    P, R, W = parts.shape
    tr = _pick(R, (368, 192, 64, 16, 8))
    spec = pl.BlockSpec((tr, W), lambda i: (i, 0))
    return pl.pallas_call(
        functools.partial(_adamw_body), name=name, grid=(R // tr,),
        in_specs=[pl.BlockSpec((P, tr, W), lambda i: (0, i, 0)), spec, spec, spec], out_specs=[spec] * 4,
        out_shape=[jax.ShapeDtypeStruct((R, W), F32)] * 4,
        compiler_params=_cparams(("parallel",)),
    )(parts, w, m, v)


def _adamw_layers(parts, w, m, v, name):
    P, B, C_ = parts[0].shape
    tb = _pick(B, (256, 128))
    nb = B // tb

    def body(*refs):
        p_refs, rest = refs[:DEPTH], refs[DEPTH:]
        a = pl.program_id(0)
        for l in range(DEPTH):
            @pl.when(a == l)
            def _():
                _adamw_body(p_refs[l], *[r.at[0] for r in rest])

    spec = pl.BlockSpec((1, tb, C_), lambda a, i: (a, i, 0))

    def part_spec(l):
        return pl.BlockSpec((P, tb, C_), lambda a, i: (0, jnp.where(a == l, i, jnp.where(a < l, 0, nb - 1)), 0))

    return pl.pallas_call(
        body, name=name, grid=(DEPTH, nb),
        in_specs=[part_spec(l) for l in range(DEPTH)] + [spec, spec, spec], out_specs=[spec] * 4,
        out_shape=[jax.ShapeDtypeStruct((DEPTH, B, C_), F32)] * 4,
        compiler_params=_cparams(("arbitrary", "arbitrary")),
    )(*parts, w, m, v)


PACK_W = 1024
BIG = (("w_in", (DEPTH, D_MODEL, N_IN // N_DEV), 2), ("w_conv_out", (DEPTH, CONV_DIM, D_MODEL // N_DEV), 2),
       ("w_uq", (DEPTH, Q_RANK, HEADS * QK_DIM // N_DEV), 2), ("w_ukv", (DEPTH, KV_RANK, HEADS * (NOPE + V_DIM) // N_DEV), 2),
       ("w_attn_out", (DEPTH, HEADS * V_DIM, D_MODEL // N_DEV), 2), ("w_hgrn_out", (DEPTH, 512, D_MODEL // N_DEV), 2),
       ("w_out", (DEPTH, D_MODEL // N_DEV, D_MODEL), 1), ("w_ff1", (DEPTH, D_MODEL, D_FF // N_DEV), 2),
       ("w_ff2", (DEPTH, D_FF // N_DEV, D_MODEL), 1))
SMALL_SHARDED = (("meta", (N_META, D_MODEL // N_DEV), 1), ("conv_w", (DEPTH, CONV_K, CONV_DIM // N_DEV), 2))
REPLICATED = (("norm1_g", (DEPTH, D_MODEL)), ("conv_b", (DEPTH, CONV_DIM)), ("conv_ln_g", (DEPTH, CONV_DIM)),
              ("conv_ln_b", (DEPTH, CONV_DIM)), ("q_a_norm_g", (DEPTH, Q_RANK)), ("kv_a_norm_g", (DEPTH, KV_RANK)),
              ("q_norm_g", (DEPTH, QK_DIM)), ("k_norm_g", (DEPTH, QK_DIM)), ("hgrn_lb_logits", (DEPTH, 512)),
              ("hgrn_norm_g", (DEPTH, 512)), ("norm2_g", (DEPTH, D_MODEL)))
WEIGHT_ORDER = ("meta", "norm1_g", "w_in", "conv_w", "conv_b", "conv_ln_g", "conv_ln_b", "w_conv_out", "q_a_norm_g", "w_uq",
                "kv_a_norm_g", "w_ukv", "q_norm_g", "k_norm_g", "w_attn_out", "hgrn_lb_logits", "hgrn_norm_g", "w_hgrn_out",
                "w_out", "norm2_g", "w_ff1", "w_ff2")


def _rows_for(n_elems, mult):
    rows = -(-n_elems // PACK_W)
    return -(-rows // mult) * mult


def _pack(arrays, dtype, mult, lead=()):
    nl = len(lead)
    flat = jnp.concatenate([a.reshape(lead + (-1,)).astype(dtype) for a in arrays], axis=nl)
    rows = _rows_for(flat.shape[nl], mult)
    flat = jnp.pad(flat, [(0, 0)] * nl + [(0, rows * PACK_W - flat.shape[nl])])
    return flat.reshape(lead + (rows, PACK_W))


def _unpack(pack, shapes, lead=()):
    nl = len(lead)
    flat = pack.reshape(lead + (-1,))
    out, off = [], 0
    for shp in shapes:
        n = int(np.prod(shp))
        out.append(lax.slice_in_dim(flat, off, off + n, axis=nl).reshape(lead + tuple(shp)))
        off += n
    return out


def _join_shards(g, axis):
    g = jnp.moveaxis(g, 0, axis)
    shp = g.shape
    return g.reshape(shp[:axis] + (shp[axis] * shp[axis + 1],) + shp[axis + 2:])


def _cut_shards(a, axis):
    shp = a.shape
    a = a.reshape(shp[:axis] + (N_DEV, shp[axis] // N_DEV) + shp[axis + 1:])
    return jnp.moveaxis(a, axis, 0)


def kernel(x, meta, norm1_g, w_in, conv_w, conv_b, conv_ln_g, conv_ln_b, w_conv_out, q_a_norm_g, w_uq, kv_a_norm_g, w_ukv, q_norm_g, k_norm_g, w_attn_out, hgrn_lb_logits, hgrn_norm_g, w_hgrn_out, w_out, norm2_g, w_ff1, w_ff2, loss_target, m_meta, m_norm1_g, m_w_in, m_conv_w, m_conv_b, m_conv_ln_g, m_conv_ln_b, m_w_conv_out, m_q_a_norm_g, m_w_uq, m_kv_a_norm_g, m_w_ukv, m_q_norm_g, m_k_norm_g, m_w_attn_out, m_hgrn_lb_logits, m_hgrn_norm_g, m_w_hgrn_out, m_w_out, m_norm2_g, m_w_ff1, m_w_ff2, v_meta, v_norm1_g, v_w_in, v_conv_w, v_conv_b, v_conv_ln_g, v_conv_ln_b, v_w_conv_out, v_q_a_norm_g, v_w_uq, v_kv_a_norm_g, v_w_ukv, v_q_norm_g, v_k_norm_g, v_w_attn_out, v_hgrn_lb_logits, v_hgrn_norm_g, v_w_hgrn_out, v_w_out, v_norm2_g, v_w_ff1, v_w_ff2):
    args = dict(locals())
    wts = {n: args[n] for n in WEIGHT_ORDER}
    mom = {n: args["m_" + n] for n in WEIGHT_ORDER}
    var = {n: args["v_" + n] for n in WEIGHT_ORDER}
    xi, yi, ci = _mesh_pos()
    me = 4 * xi + 2 * yi + ci

    shard = lambda l: [wts[n][l].astype(BF16) for n, _, _ in BIG]
    gathered = _all_gather(shard(0) + [_pack([wts[n] for n, _, _ in SMALL_SHARDED], F32, 8)], "gather_layer0")
    small = dict(zip([n for n, _, _ in SMALL_SHARDED],
                     [_join_shards(g, axis) for (_, _, axis), g in
                      zip(SMALL_SHARDED, _unpack(gathered[-1], [s for _, s, _ in SMALL_SHARDED], (N_DEV,)))]))
    pending = _copy_start(shard(1), True, "gather_layer1_start", 5)

    def layer_weights(l, xp):
        full = {n: wts[n][l] for n, _ in REPLICATED}
        if l == 0:
            mats = gathered[:-1]
            full["norm1_g"] = full["norm1_g"] + pending[4][0, 0]
        else:
            own, lands = _copy_wait(pending[0], pending[1], pending[2], pending[3], xp, True, "gather_layer1_wait")
            mats = [lax.dynamic_update_index_in_dim(z, s, me, 0) for z, s in zip(lands, own)]
        full["conv_w"] = small["conv_w"][l]
        for (n, _, axis), g in zip(BIG, mats):
            if n == "w_in":
                full["w_in_padded"] = _w_in_from_shards(g)
            else:
                full[n] = _join_shards(g, axis - 1)
        return full, xp

    big_names = [n for n, _, _ in BIG]
    early = [n for n in big_names if n in ("w_out", "w_ff1", "w_ff2")]
    late = [n for n in big_names if n not in early]
    cut = lambda g, names: [(g[n + "_shards"] if n + "_shards" in g else _cut_shards(g[n], axis - 1)).astype(BF16)
                            for n, _, axis in BIG if n in names]
    layer_grads = [None] * DEPTH
    flight = {}

    def layer_mid(l, g, w_out):
        if l == 0:
            flight["l0_early"] = _copy_start(cut(g, early), False, "scatter_layer0_early_start", 7)
            w_out = w_out + flight["l0_early"][4][0, 0].astype(w_out.dtype)
        return w_out

    def layer_done(l, g, dx):
        layer_grads[l] = g
        if l == 1:
            flight["l1"] = _copy_start(cut(g, big_names), False, "scatter_l1_start", 6)
            dx = dx + flight["l1"][4][0, 0]
        return dx

    def layer_matrices(l, mats, norm_g):
        if l == 0:
            flight["l0_late"] = _copy_start(cut(mats, late), False, "scatter_l0_late_start", 8)
            norm_g = norm_g + flight["l0_late"][4][0, 0]
        return norm_g

    loss, grad_x, g_meta, g_lb = _run_step(x[0], loss_target[0], small["meta"], wts["hgrn_lb_logits"],
                                           layer_weights, layer_done, layer_mid, layer_matrices)

    grads = {k: jnp.stack([layer_grads[l][k] for l in range(DEPTH)]) for k in layer_grads[0]
             if k not in big_names and not k.endswith("_shards")}
    grads["hgrn_lb_logits"] = g_lb
    grads["meta"] = g_meta
    small_names = [n for n, _ in REPLICATED] + [n for n, _, _ in SMALL_SHARDED]
    part = _pack([grads[n] for n in small_names] + [loss.reshape(1)], F32, 8)
    flight["small"] = _copy_start([part], True, "gather_small_grads_start", 9)
    started = flight["small"][4]

    def arrive(key, names, after):
        s_sems, r_sems, sent, lands, _ = flight[key]
        sent, lands = _copy_wait(s_sems, r_sems, sent, lands, after, False, f"scatter_{key}_wait")
        return {n: lax.dynamic_update_index_in_dim(z, lax.dynamic_index_in_dim(s, me, 0, keepdims=False), me, 0)
                for n, z, s in zip(names, lands, sent)}

    out = {}

    def update(names, recv0, recv1):
        for n in names:
            res4 = _adamw_layers([recv0[n], recv1[n]], wts[n], mom[n], var[n], "adamw_" + n)
            for kind, a in zip(("grad_", "delta_", "new_m_", "new_v_"), res4):
                out[kind + n] = a

    recv1 = arrive("l1", big_names, started)
    recv0 = arrive("l0_early", early, started)
    update(early, recv0, recv1)

    s_sems, r_sems, sent, lands, _ = flight["small"]
    updated = lax.optimization_barrier(tuple(out["grad_" + n] for n in early))
    sent, lands = _copy_wait(s_sems, r_sems, sent, lands, updated[0], True, "gather_small_grads_wait")
    total = _sum_parts(lax.dynamic_update_index_in_dim(lands[0], sent[0], me, 0), "sum_small_grads")
    sizes = [grads[n].shape for n in small_names] + [(1,)]
    tot = dict(zip(small_names + ["loss"], _unpack(total, sizes)))
    loss = tot.pop("loss").reshape(())
    mine = {n: tot[n] for n, _ in REPLICATED}
    for n, shp, axis in SMALL_SHARDED:
        mine[n] = lax.dynamic_slice_in_dim(tot[n], me * shp[axis], shp[axis], axis=axis)
    pk = lambda d: _pack([d[n] for n in small_names], F32, 8)
    small_out = _adamw(pk(mine)[None], pk(wts), pk(mom), pk(var), "adamw_vectors")
    for kind, pack in zip(("grad_", "delta_", "new_m_", "new_v_"), small_out):
        for n, a in zip(small_names, _unpack(pack, [wts[n].shape for n in small_names])):
            out[kind + n] = a

    recv0 = arrive("l0_late", late, small_out[0])
    update(late, recv0, recv1)

    res = [loss, grad_x[None]]
    for kind in ("grad_", "delta_", "new_m_", "new_v_"):
        res += [out[kind + n] for n in WEIGHT_ORDER]
    return tuple(res)
```

```python
import functools

import numpy as np
import jax
import jax.numpy as jnp
from jax import lax
from jax.experimental import pallas as pl
from jax.experimental.pallas import tpu as pltpu

F32 = jnp.float32
BF16 = jnp.bfloat16

D_MODEL = 1024
DEPTH = 2
N_META = 16
PAD_FRONT = 112
ROW0 = PAD_FRONT + N_META
EPS = 1e-6
GATE_CLAMP = 1.0 - 1e-6
CONV_DIM = 512
CONV_K = 31
HEADS = 8
Q_RANK = 256
KV_RANK = 128
NOPE = 64
ROPE = 32
V_DIM = 64
QK_DIM = NOPE + ROPE
HEAD_W = 128
ROPE_BASE = 10000.0
HG_HEADS = 4
HG_DK = 128
HG_DV = 128
HG_CHUNK = 64
D_FF = 4096
N_IN = 6560
C_CONV_A, C_CONV_G, C_GATE, C_CQ, C_CKV, C_KR, C_HQ, C_HF, C_HI, C_HG = (
    0, 512, 1024, 4096, 4352, 4480, 4608, 5120, 5632, 6144)
N_IN_P = 6656
O_CQ, O_KR, O_HQ, O_GATE = 1024, 1408, 1440, 3488
KR_LANE = NOPE

ADAM_LR = 0.001
ADAM_B1 = 0.9
ADAM_B2 = 0.999
ADAM_EPS = 1e-08
ADAM_WD = 0.01
ADAM_STEP = 10

N_DEV = 8
VMEM_LIMIT = 56 * 1024 * 1024
MESH = pl.DeviceIdType.MESH


def _pick(n, cands):
    for c in cands:
        if n % c == 0:
            return c
    raise ValueError(f"no tile for {n}")


def _cparams(sem, **kw):
    return pltpu.CompilerParams(dimension_semantics=sem, vmem_limit_bytes=VMEM_LIMIT, **kw)


def _relu2(v):
    return jnp.square(jnp.maximum(v, 0.0))


def _mm(a, b, *, ta=False, tb=False, out_dtype=F32, res=None, a_fn=None, epi=None, name):
    M, K = (a.shape[1], a.shape[0]) if ta else a.shape
    N = b.shape[0] if tb else b.shape[1]
    assert (b.shape[1] if tb else b.shape[0]) == K, (a.shape, b.shape, ta, tb)
    tm = _pick(M, (1056, 1024, 512, 384, 256, 128, 96))
    tn = _pick(N, (1664, 1024, 512, 384, 256, 128))
    tk = _pick(K, (1664, 1056, 1024, 512, 384, 256, 128, 96))
    nk = K // tk
    dims = (((0 if ta else 1,), (1 if tb else 0,)), ((), ()))
    extras = ([res] if res is not None else []) + ([epi[0]] if epi is not None else [])

    def body(*refs):
        a_ref, b_ref = refs[0], refs[1]
        r_ref = refs[2] if res is not None else None
        e_ref = refs[2 + (res is not None)] if epi is not None else None
        o_ref = refs[2 + len(extras)]
        acc = refs[-1] if nk > 1 else None
        k = pl.program_id(2)
        av = a_ref[...]
        if a_fn is not None:
            av = a_fn(av.astype(F32))
        p = lax.dot_general(av.astype(BF16), b_ref[...].astype(BF16), dims, preferred_element_type=F32)

        def finish(total):
            if e_ref is not None:
                total = epi[1](total, e_ref[...].astype(F32))
            if r_ref is not None:
                total = total + r_ref[...].astype(F32)
            o_ref[...] = total.astype(o_ref.dtype)

        if nk == 1:
            finish(p)
        else:
            @pl.when(k == 0)
            def _():
                acc[...] = p

            @pl.when(k > 0)
            def _():
                acc[...] += p

            @pl.when(k == nk - 1)
            def _():
                finish(acc[...])

    a_spec = pl.BlockSpec((tk, tm), lambda i, j, k: (k, i)) if ta else pl.BlockSpec((tm, tk), lambda i, j, k: (i, k))
    b_spec = pl.BlockSpec((tn, tk), lambda i, j, k: (j, k)) if tb else pl.BlockSpec((tk, tn), lambda i, j, k: (k, j))
    o_spec = pl.BlockSpec((tm, tn), lambda i, j, k: (i, j))
    in_specs = [a_spec, b_spec] + [o_spec] * len(extras)
    args = (a, b) + tuple(extras)
    return pl.pallas_call(
        body, name=name, grid=(M // tm, N // tn, nk), in_specs=in_specs, out_specs=o_spec,
        out_shape=jax.ShapeDtypeStruct((M, N), out_dtype),
        scratch_shapes=[pltpu.VMEM((tm, tn), F32)] if nk > 1 else [],
        compiler_params=_cparams(("parallel", "parallel", "arbitrary")),
    )(*args)


class Row:
    def __init__(self, arr, width=None, col=0, piece=None):
        self.arr = arr
        self.width = arr.shape[1] if width is None else width
        assert col % self.width == 0
        self.blk = col // self.width
        self.piece = self.width if piece is None else piece

    def spec(self, tm):
        blk = self.blk
        return pl.BlockSpec((tm, self.width), lambda i: (i, blk))


def _split(v, piece):
    w = v.shape[-1]
    if piece == w:
        return v
    return [v[:, j * piece:(j + 1) * piece] for j in range(w // piece)]


def _store(ref, val, dtype=None):
    if isinstance(val, (list, tuple)):
        piece = val[0].shape[-1]
        for j, p in enumerate(val):
            ref[:, j * piece:(j + 1) * piece] = p.astype(ref.dtype)
    else:
        ref[...] = val.astype(ref.dtype)


def _row_tile(T):
    return _pick(T, (384, 352, 192, 128))


def _param2d(p):
    return p.reshape(1, -1).astype(F32)


def _rowwise(fn, T, rows, params, outs, name):
    tm = _row_tile(T)
    nr, npar = len(rows), len(params)
    par = [(_param2d(p), piece) for p, piece in params]

    def body(*refs):
        rid = pl.program_id(0) * tm + lax.broadcasted_iota(jnp.int32, (tm, 1), 0)
        rv = [_split(refs[n][...].astype(F32), rows[n].piece) for n in range(nr)]
        pv = [_split(refs[nr + n][...], par[n][1]) for n in range(npar)]
        res = fn(rid, rv, pv)
        for n, val in enumerate(res):
            _store(refs[nr + npar + n], val)

    return pl.pallas_call(
        body, name=name, grid=(T // tm,),
        in_specs=[r.spec(tm) for r in rows] + [pl.BlockSpec(p.shape, lambda i: (0, 0)) for p, _ in par],
        out_specs=[pl.BlockSpec((tm, w), lambda i: (i, 0)) for w, _ in outs],
        out_shape=[jax.ShapeDtypeStruct((T, w), dt) for w, dt in outs],
        compiler_params=_cparams(("parallel",)),
    )(*[r.arr for r in rows], *[p for p, _ in par])


def _rowwise_bwd(fn, T, rows, params, cts, drow, name, add=None):
    tm = _row_tile(T)
    nr, npar, nct = len(rows), len(params), len(cts)
    par = [(_param2d(p), piece) for p, piece in params]
    didx = sorted(drow)
    has_add = add is not None

    def body(*refs):
        i = pl.program_id(0)
        rid = i * tm + lax.broadcasted_iota(jnp.int32, (tm, 1), 0)
        rv = [_split(refs[n][...].astype(F32), rows[n].piece) for n in range(nr)]
        pv = [_split(refs[nr + n][...], par[n][1]) for n in range(npar)]
        cv = [_split(refs[nr + npar + n][...].astype(F32), cts[n].piece) for n in range(nct)]
        base = nr + npar + nct + (1 if has_add else 0)
        d_refs = refs[base:base + len(didx)]
        p_refs = refs[base + len(didx):]

        def g(dvals, pvals):
            full = list(rv)
            for n, v in zip(didx, dvals):
                full[n] = v
            return fn(rid, full, pvals)

        _, vjp = jax.vjp(g, [rv[n] for n in didx], pv)
        d_rows, d_pars = vjp(cv)
        for slot, n in enumerate(didx):
            val = d_rows[slot]
            if has_add and add[0] == n:
                assert not isinstance(val, (list, tuple))
                val = val + refs[nr + npar + nct][...].astype(F32)
            _store(d_refs[slot], val)

        @pl.when(i == 0)
        def _():
            for r in p_refs:
                r[...] = jnp.zeros_like(r)

        for r, val in zip(p_refs, d_pars):
            if isinstance(val, (list, tuple)):
                piece = val[0].shape[-1]
                for j, p in enumerate(val):
                    r[:, j * piece:(j + 1) * piece] += p
            else:
                r[...] += val

    in_specs = ([r.spec(tm) for r in rows] + [pl.BlockSpec(p.shape, lambda i: (0, 0)) for p, _ in par]
                + [c.spec(tm) for c in cts])
    args = [r.arr for r in rows] + [p for p, _ in par] + [c.arr for c in cts]
    if has_add:
        in_specs.append(pl.BlockSpec((tm, rows[add[0]].width), lambda i: (i, 0)))
        args.append(add[1])
    out_specs = ([pl.BlockSpec((tm, rows[n].width), lambda i: (i, 0)) for n in didx]
                 + [pl.BlockSpec(p.shape, lambda i: (0, 0)) for p, _ in par])
    out_shape = ([jax.ShapeDtypeStruct((T, rows[n].width), drow[n]) for n in didx]
                 + [jax.ShapeDtypeStruct(p.shape, F32) for p, _ in par])
    res = pl.pallas_call(
        body, name=name, grid=(T // tm,), in_specs=in_specs, out_specs=out_specs, out_shape=out_shape,
        compiler_params=_cparams(("arbitrary",)),
    )(*args)
    return list(res[:len(didx)]), list(res[len(didx):])


def _f_rms(rid, rv, pv):
    x, g = rv[0], pv[0]
    return [x * lax.rsqrt(jnp.mean(x * x, axis=-1, keepdims=True) + EPS) * g]


def _f_glu(rid, rv, pv):
    a, gt = rv
    return [a * jax.nn.sigmoid(gt) * (rid >= PAD_FRONT).astype(F32)]


def _f_lnsilu(rid, rv, pv):
    x = rv[0]
    g, b = pv
    mu = jnp.mean(x, axis=-1, keepdims=True)
    xc = x - mu
    y = xc * lax.rsqrt(jnp.mean(xc * xc, axis=-1, keepdims=True) + EPS) * g + b
    return [y * jax.nn.sigmoid(y)]


@functools.partial(jax.custom_vjp, nondiff_argnums=(1,))
def _lane_roll(x, shift):
    return pltpu.roll(x, shift, 1)


def _lane_roll_fwd(x, shift):
    return pltpu.roll(x, shift, 1), None


def _lane_roll_bwd(shift, _, g):
    return (pltpu.roll(g, (HEAD_W - shift) % HEAD_W, 1),)


_lane_roll.defvjp(_lane_roll_fwd, _lane_roll_bwd)


def _head_norm_rope(xh, g, c, s1, s2):
    y = xh * lax.rsqrt(jnp.sum(xh * xh, axis=-1, keepdims=True) * (1.0 / QK_DIM) + EPS) * g
    half = ROPE // 2
    return y * c + _lane_roll(y, HEAD_W - half) * s1 + _lane_roll(y, half) * s2


def _f_qrope(rid, rv, pv):
    q, c, s1, s2 = rv
    return [[_head_norm_rope(qh, pv[0], c, s1, s2) * ATT_SCALE for qh in q]]


def _f_krope(rid, rv, pv):
    k, kr, c, s1, s2 = rv
    return [[_head_norm_rope(kh + kr, pv[0], c, s1, s2) for kh in k]]


def _f_hgrn_prep(rid, rv, pv):
    hf, hi = rv
    m = (rid >= PAD_FRONT).astype(F32)
    kk = (1.0 - pv[0]) * jax.nn.sigmoid(-hf) * m
    lf = jnp.log1p(-jnp.minimum(kk, GATE_CLAMP))
    vv = hi * jax.nn.sigmoid(hi) * m
    return [kk, lf, vv]


def _f_hgrn_out(rid, rv, pv):
    o, hg = rv
    ng = pv[0]
    out = []
    for oh, gh, nh in zip(o, hg, ng):
        y = oh * lax.rsqrt(jnp.mean(oh * oh, axis=-1, keepdims=True) + EPS) * nh
        out.append(y * (gh * jax.nn.sigmoid(gh)))
    return [out]


def _f_mix(rid, rv, pv):
    g0, g1, g2, ya, yb, yc = rv
    return [jax.nn.sigmoid(g0) * ya + jax.nn.sigmoid(g1) * yb + jax.nn.sigmoid(g2) * yc]


def _f_relu2(rid, rv, pv):
    return [jnp.square(jax.nn.relu(rv[0]))]


def _loss_head(x2, tgt, T):
    tm = _row_tile(T)

    def body(x_ref, t_ref, dx_ref, l_ref):
        i = pl.program_id(0)
        rid = i * tm + lax.broadcasted_iota(jnp.int32, (tm, 1), 0)
        diff = (x_ref[...] - t_ref[...]) * (rid >= ROW0).astype(F32)
        dx_ref[...] = diff * (1.0 / D_MODEL)

        @pl.when(i == 0)
        def _():
            l_ref[...] = jnp.zeros_like(l_ref)

        l_ref[...] += jnp.sum(diff * diff, axis=0, keepdims=True)

    spec = pl.BlockSpec((tm, D_MODEL), lambda i: (i, 0))
    return pl.pallas_call(
        body, name="loss_head", grid=(T // tm,), in_specs=[spec, spec],
        out_specs=[spec, pl.BlockSpec((1, D_MODEL), lambda i: (0, 0))],
        out_shape=[jax.ShapeDtypeStruct((T, D_MODEL), F32), jax.ShapeDtypeStruct((1, D_MODEL), F32)],
        compiler_params=_cparams(("arbitrary",)),
    )(x2, tgt)


HALO = 32


CONV_ROWS = 64


def _conv_lanes():
    return [slice(c, c + 128) for c in range(0, CONV_DIM, 128)]


def _conv_tile(T):
    return _pick(T, (384, 128))


def _conv_fwd(h, w, b, T, name):
    tr = _conv_tile(T)
    ratio = tr // HALO
    wp = jnp.zeros((HALO, CONV_DIM), F32).at[:CONV_K].set(w)

    def body(m_ref, h_ref, w_ref, b_ref, o_ref, win):
        i = pl.program_id(0)
        win[0:HALO, :] = h_ref[...] * (i > 0).astype(F32)
        win[HALO:, :] = m_ref[...]
        for cs in _conv_lanes():
            wv, bv = w_ref[:, cs], b_ref[:, cs]
            for r0 in range(0, tr, CONV_ROWS):
                acc = jnp.broadcast_to(bv, (CONV_ROWS, 128))
                for k in range(CONV_K):
                    acc = acc + wv[k:k + 1] * win[pl.ds(HALO - (CONV_K - 1) + k + r0, CONV_ROWS), cs]
                o_ref[r0:r0 + CONV_ROWS, cs] = acc

    return pl.pallas_call(
        body, name=name, grid=(T // tr,),
        in_specs=[pl.BlockSpec((tr, CONV_DIM), lambda i: (i, 0)),
                  pl.BlockSpec((HALO, CONV_DIM), lambda i: (jnp.maximum(i * ratio - 1, 0), 0)),
                  pl.BlockSpec((HALO, CONV_DIM), lambda i: (0, 0)),
                  pl.BlockSpec((1, CONV_DIM), lambda i: (0, 0))],
        out_specs=pl.BlockSpec((tr, CONV_DIM), lambda i: (i, 0)),
        out_shape=jax.ShapeDtypeStruct((T, CONV_DIM), F32),
        scratch_shapes=[pltpu.VMEM((tr + HALO, CONV_DIM), F32)],
        compiler_params=_cparams(("parallel",)),
    )(h, h, wp, _param2d(b))


def _conv_bwd(h, w, dy, T, name):
    tr = _conv_tile(T)
    ratio = tr // HALO
    n_t = T // tr
    last_halo = T // HALO - 1
    wp = jnp.zeros((HALO, CONV_DIM), F32).at[:CONV_K].set(w)

    def body(hm_ref, hh_ref, dm_ref, dh_ref, w_ref, dx_ref, dw_ref, db_ref, hwin, dwin):
        i = pl.program_id(0)
        hwin[0:HALO, :] = hh_ref[...] * (i > 0).astype(F32)
        hwin[HALO:, :] = hm_ref[...]
        dwin[0:tr, :] = dm_ref[...]
        dwin[tr:, :] = dh_ref[...] * (i < n_t - 1).astype(F32)

        @pl.when(i == 0)
        def _():
            dw_ref[...] = jnp.zeros_like(dw_ref)
            db_ref[...] = jnp.zeros_like(db_ref)

        db_ref[...] += jnp.sum(dm_ref[...], axis=0, keepdims=True)
        fold = lambda a: functools.reduce(jnp.add, [a[r:r + 8] for r in range(0, CONV_ROWS, 8)])
        for cs in _conv_lanes():
            wv = w_ref[:, cs]
            dws = [jnp.zeros((8, 128), F32) for _ in range(CONV_K)]
            for r0 in range(0, tr, CONV_ROWS):
                acc = jnp.zeros((CONV_ROWS, 128), F32)
                for k in range(CONV_K):
                    acc = acc + wv[k:k + 1] * dwin[pl.ds(CONV_K - 1 - k + r0, CONV_ROWS), cs]
                dx_ref[r0:r0 + CONV_ROWS, cs] = acc
                dy_t = dm_ref[r0:r0 + CONV_ROWS, cs]
                for k in range(CONV_K):
                    dws[k] = dws[k] + fold(dy_t * hwin[pl.ds(HALO - (CONV_K - 1) + k + r0, CONV_ROWS), cs])
            for k in range(CONV_K):
                dw_ref[k:k + 1, cs] += jnp.sum(dws[k], axis=0, keepdims=True)

    main = pl.BlockSpec((tr, CONV_DIM), lambda i: (i, 0))
    return pl.pallas_call(
        body, name=name, grid=(n_t,),
        in_specs=[main,
                  pl.BlockSpec((HALO, CONV_DIM), lambda i: (jnp.maximum(i * ratio - 1, 0), 0)),
                  main,
                  pl.BlockSpec((HALO, CONV_DIM), lambda i: (jnp.minimum((i + 1) * ratio, last_halo), 0)),
                  pl.BlockSpec((HALO, CONV_DIM), lambda i: (0, 0))],
        out_specs=[main, pl.BlockSpec((HALO, CONV_DIM), lambda i: (0, 0)), pl.BlockSpec((1, CONV_DIM), lambda i: (0, 0))],
        out_shape=[jax.ShapeDtypeStruct((T, CONV_DIM), F32), jax.ShapeDtypeStruct((HALO, CONV_DIM), F32),
                   jax.ShapeDtypeStruct((1, CONV_DIM), F32)],
        scratch_shapes=[pltpu.VMEM((tr + HALO, CONV_DIM), F32), pltpu.VMEM((tr + HALO, CONV_DIM), F32)],
        compiler_params=_cparams(("arbitrary",)),
    )(h, h, dy, dy, wp)


NEG = -1e30
ATT_SCALE = QK_DIM ** -0.5
_NT = (((1,), (1,)), ((), ()))
_TN = (((0,), (0,)), ((), ()))


def _att_blk(T):
    return _pick(T, (384, 128))


def _att_mask(i, j, blk):
    kpos = j * blk + lax.broadcasted_iota(jnp.int32, (blk, blk), 0)
    qpos = i * blk + lax.broadcasted_iota(jnp.int32, (blk, blk), 1)
    return (kpos <= qpos) & (kpos >= PAD_FRONT)


def _t32(a):
    return a.astype(F32).T.astype(BF16)


def _attn_fwd(q, k, v, T, name):
    blk = _att_blk(T)
    nq = T // blk

    def body(q_ref, k_ref, v_ref, o_ref, lse_ref, vt):
        i = pl.program_id(1)

        @pl.when(i == 0)
        def _():
            def tr(j, c):
                vt[j] = _t32(v_ref[pl.ds(pl.multiple_of(j * blk, blk), blk), :])
                return c

            lax.fori_loop(0, nq, tr, 0)

        qb = q_ref[...]

        def step(js, carry, masked):
            m, l, acc = carry
            ss = []
            for j in js:
                kb = k_ref[pl.ds(pl.multiple_of(j * blk, blk), blk), :]
                s = lax.dot_general(kb, qb, _NT, preferred_element_type=F32)
                ss.append(jnp.where(_att_mask(i, j, blk), s, NEG) if masked else s)
            m_new = m
            for s in ss:
                m_new = jnp.maximum(m_new, jnp.max(s, axis=0, keepdims=True))
            alpha = jnp.exp(m - m_new)
            l = alpha * l
            acc = alpha * acc
            for j, s in zip(js, ss):
                p = jnp.exp(s - m_new)
                l = l + jnp.sum(p, axis=0, keepdims=True)
                acc = acc + jnp.dot(vt[j], p.astype(BF16), preferred_element_type=F32)
            return m_new, l, acc

        init = (jnp.full((1, blk), NEG, F32), jnp.zeros((1, blk), F32), jnp.zeros((HEAD_W, blk), F32))
        later = jnp.minimum(i, 1)
        carry = lax.fori_loop(0, 1 - later, lambda t, c: step([i], c, True), init)
        carry = lax.fori_loop(0, later, lambda t, c: step([i, 0], c, True), carry)
        n_free = jnp.maximum(i - 1, 0)
        n4 = n_free // 4
        carry = lax.fori_loop(0, n4, lambda t, c: step([1 + 4 * t + d for d in range(4)], c, False), carry)
        rest = n_free - 4 * n4
        carry = lax.fori_loop(0, rest // 2, lambda t, c: step([i - rest, i - rest + 1], c, False), carry)
        m, l, acc = lax.fori_loop(0, rest % 2, lambda t, c: step([i - 1], c, False), carry)
        o_ref[...] = (acc / l).T.astype(o_ref.dtype)
        lse_ref[0, 0] = m + jnp.log(l)

    full = pl.BlockSpec((T, HEAD_W), lambda h, i: (0, h))
    return pl.pallas_call(
        body, name=name, grid=(HEADS, nq),
        in_specs=[pl.BlockSpec((blk, HEAD_W), lambda h, i: (i, h)), full, full],
        out_specs=[pl.BlockSpec((blk, HEAD_W), lambda h, i: (i, h)),
                   pl.BlockSpec((1, 1, 1, blk), lambda h, i: (h, i, 0, 0))],
        out_shape=[jax.ShapeDtypeStruct((T, HEADS * HEAD_W), BF16), jax.ShapeDtypeStruct((HEADS, nq, 1, blk), F32)],
        scratch_shapes=[pltpu.VMEM((nq, HEAD_W, blk), BF16)],
        compiler_params=_cparams(("parallel", "arbitrary")),
    )(q, k, v)


def _attn_bwd(q, k, v, o, lse, do, T, name):
    blk = _att_blk(T)
    nq = T // blk

    def body(q_ref, k_ref, v_ref, o_ref, lse_ref, do_ref, dq_ref, dk_ref, dv_ref, delta, dqt, dk_acc, dv_acc):
        j = pl.program_id(1)

        @pl.when(j == 0)
        def _():
            dqt[...] = jnp.zeros_like(dqt)

            def dstep(i, c):
                r0 = pl.multiple_of(i * blk, blk)
                prod = do_ref[pl.ds(r0, blk), :].astype(F32) * o_ref[pl.ds(r0, blk), :].astype(F32)
                delta[i] = jnp.sum(prod.T, axis=0, keepdims=True)
                return c

            lax.fori_loop(0, nq, dstep, 0)

        kb = k_ref[...]
        vb = v_ref[...]
        kbt = _t32(kb)
        dk_acc[...] = jnp.zeros_like(dk_acc)
        dv_acc[...] = jnp.zeros_like(dv_acc)

        def step(qs, masked):
            dvs, dks = [], []
            for i in qs:
                r0 = pl.multiple_of(i * blk, blk)
                qb = q_ref[pl.ds(r0, blk), :]
                dob = do_ref[pl.ds(r0, blk), :]
                s = lax.dot_general(kb, qb, _NT, preferred_element_type=F32)
                p = jnp.exp(s - lse_ref[0, i])
                if masked:
                    p = jnp.where(_att_mask(i, j, blk), p, 0.0)
                dvs.append(jnp.dot(p.astype(BF16), dob, preferred_element_type=F32))
                dp = lax.dot_general(vb, dob, _NT, preferred_element_type=F32)
                ds = (p * (dp - delta[i])).astype(BF16)
                dks.append(jnp.dot(ds, qb, preferred_element_type=F32))
                dqt[i] += jnp.dot(kbt, ds, preferred_element_type=F32)
            dv_acc[...] += functools.reduce(jnp.add, dvs)
            dk_acc[...] += functools.reduce(jnp.add, dks)

        def loop(lo, masked):
            n = nq - lo

            def pair(t, c):
                step([lo + 2 * t, lo + 2 * t + 1], masked)
                return c

            def last(t, c):
                step([nq - 1], masked)
                return c

            lax.fori_loop(0, n // 2, pair, 0)
            lax.fori_loop(0, n % 2, last, 0)

        @pl.when(j == 0)
        def _():
            loop(0, True)

        @pl.when(j > 0)
        def _():
            step([j], True)
            loop(j + 1, False)

        dk_ref[...] = dk_acc[...].astype(dk_ref.dtype)
        dv_ref[...] = dv_acc[...].astype(dv_ref.dtype)

        @pl.when(j == nq - 1)
        def _():
            def wstep(i, c):
                dq_ref[pl.ds(pl.multiple_of(i * blk, blk), blk), :] = dqt[i].T
                return c

            lax.fori_loop(0, nq, wstep, 0)

    full = pl.BlockSpec((T, HEAD_W), lambda h, j: (0, h))
    kblk = pl.BlockSpec((blk, HEAD_W), lambda h, j: (j, h))
    wide = (T, HEADS * HEAD_W)
    return pl.pallas_call(
        body, name=name, grid=(HEADS, nq),
        in_specs=[full, kblk, kblk, full, pl.BlockSpec((1, nq, 1, blk), lambda h, j: (h, 0, 0, 0)), full],
        out_specs=[full, kblk, kblk],
        out_shape=[jax.ShapeDtypeStruct(wide, F32), jax.ShapeDtypeStruct(wide, BF16), jax.ShapeDtypeStruct(wide, BF16)],
        scratch_shapes=[pltpu.VMEM((nq, 1, blk), F32), pltpu.VMEM((nq, HEAD_W, blk), F32),
                        pltpu.VMEM((blk, HEAD_W), F32), pltpu.VMEM((blk, HEAD_W), F32)],
        compiler_params=_cparams(("parallel", "arbitrary")),
    )(q, k, v, o, lse, do)


HG_NB = 6
C = HG_CHUNK
_HI = lax.Precision.HIGHEST


def _tri(lower):
    r = lax.broadcasted_iota(jnp.int32, (C, C), 0)
    c = lax.broadcasted_iota(jnp.int32, (C, C), 1)
    return ((c <= r) if lower else (c >= r)).astype(F32)


HG_SUB = 8
N_SUB = C // HG_SUB


def _hg_split_decay(b, I, rid):
    lo = I * HG_SUB
    r = b[lo:lo + 1]
    eq = jnp.exp(b[lo:lo + HG_SUB] - r)
    ek = jnp.where(rid < lo, jnp.exp(jnp.minimum(r - b, 0.0)), 0.0)
    return eq, ek


def _hg_intra_fwd(q, k, v, b):
    rid = lax.broadcasted_iota(jnp.int32, (C, 1), 0)
    tid = lax.broadcasted_iota(jnp.int32, (HG_SUB, 1), 0)
    a_rows = [jnp.zeros((HG_SUB, C), F32)]
    blocks = []
    for I in range(N_SUB):
        lo = I * HG_SUB
        q_i, b_i = q[lo:lo + HG_SUB], b[lo:lo + HG_SUB]
        if I > 0:
            eq, ek = _hg_split_decay(b, I, rid)
            a_rows.append(lax.dot_general((q_i * eq).astype(BF16), (k * ek).astype(BF16), _NT,
                                          preferred_element_type=F32))
        o_i = jnp.zeros((HG_SUB, HG_DV), F32)
        for s in range(HG_SUB):
            r = lo + s
            e = jnp.exp(jnp.minimum(b_i - b[r:r + 1], 0.0))
            a = jnp.sum(q_i * k[r:r + 1] * e, axis=-1, keepdims=True)
            o_i = o_i + jnp.where(tid >= s, a, 0.0) * v[r:r + 1]
        blocks.append(o_i)
    a_off = jnp.concatenate(a_rows, axis=0).astype(BF16)
    return jnp.dot(a_off, v.astype(BF16), preferred_element_type=F32) + jnp.concatenate(blocks, axis=0)


def _hg_intra_bwd(q, k, v, b, do, dk_s, dv_s):
    rid = lax.broadcasted_iota(jnp.int32, (C, 1), 0)
    tid = lax.broadcasted_iota(jnp.int32, (HG_SUB, 1), 0)
    da_all = lax.dot_general(do.astype(BF16), v.astype(BF16), _NT, preferred_element_type=F32)
    a_rows = [jnp.zeros((HG_SUB, C), F32)]
    dq_blocks = []
    dk = jnp.zeros((C, HG_DK), F32)
    for I in range(N_SUB):
        lo = I * HG_SUB
        q_i, b_i, do_i = q[lo:lo + HG_SUB], b[lo:lo + HG_SUB], do[lo:lo + HG_SUB]
        dq_i = jnp.zeros((HG_SUB, HG_DK), F32)
        if I > 0:
            eq, ek = _hg_split_decay(b, I, rid)
            qs, ks = (q_i * eq).astype(BF16), (k * ek).astype(BF16)
            a_rows.append(lax.dot_general(qs, ks, _NT, preferred_element_type=F32))
            da = da_all[lo:lo + HG_SUB].astype(BF16)
            dq_i = jnp.dot(da, ks, preferred_element_type=F32) * eq
            dk = dk + lax.dot_general(da, qs, _TN, preferred_element_type=F32) * ek
        for s in range(HG_SUB):
            r = lo + s
            e = jnp.where(tid >= s, jnp.exp(jnp.minimum(b_i - b[r:r + 1], 0.0)), 0.0)
            a = jnp.sum(q_i * k[r:r + 1] * e, axis=-1, keepdims=True)
            g = jnp.sum(do_i * v[r:r + 1], axis=-1, keepdims=True) * e
            dq_i = dq_i + g * k[r:r + 1]
            dk_s[r:r + 1, :] = jnp.sum(g * q_i, axis=0, keepdims=True)
            dv_s[r:r + 1, :] = jnp.sum(a * do_i, axis=0, keepdims=True)
        dq_blocks.append(dq_i)
    a_off = jnp.concatenate(a_rows, axis=0).astype(BF16)
    dv = lax.dot_general(a_off, do.astype(BF16), _TN, preferred_element_type=F32)
    return jnp.concatenate(dq_blocks, axis=0), dk + dk_s[...], dv + dv_s[...]


def _hgrn_fwd(u, kk, lf, vv, T, name):
    nb = _pick(T // C, (HG_NB, 3, 2, 1))
    rows = nb * C
    qblk = C_HQ // HG_DK

    def body(q_ref, k_ref, lf_ref, v_ref, o_ref, st_ref, st):
        @pl.when(pl.program_id(1) == 0)
        def _():
            st[...] = jnp.zeros_like(st)

        lower = _tri(True)
        for n in range(nb):
            sl = slice(n * C, (n + 1) * C)
            q, k, v = q_ref[sl, :].astype(F32), k_ref[sl, :], v_ref[sl, :]
            b = jnp.dot(lower, lf_ref[sl, :], precision=_HI, preferred_element_type=F32)
            s_t = st[...]
            st_ref[0, n] = s_t
            qe = (q * jnp.exp(b)).astype(BF16)
            o = lax.dot_general(qe, s_t.astype(BF16), _NT, preferred_element_type=F32)
            o_ref[sl, :] = o + _hg_intra_fwd(q, k, v, b)
            bl = b[C - 1:C, :]
            kd = (k * jnp.exp(bl - b)).astype(BF16)
            st[...] = s_t * jnp.exp(bl) + lax.dot_general(v.astype(BF16), kd, _TN, preferred_element_type=F32)

    col = lambda off: pl.BlockSpec((rows, HG_DK), lambda h, c: (c, h + off))
    return pl.pallas_call(
        body, name=name, grid=(HG_HEADS, T // rows),
        in_specs=[col(qblk), col(0), col(0), col(0)],
        out_specs=[col(0), pl.BlockSpec((1, nb, HG_DV, HG_DK), lambda h, c: (h, c, 0, 0))],
        out_shape=[jax.ShapeDtypeStruct((T, HG_HEADS * HG_DV), F32),
                   jax.ShapeDtypeStruct((HG_HEADS, T // C, HG_DV, HG_DK), F32)],
        scratch_shapes=[pltpu.VMEM((HG_DV, HG_DK), F32)],
        compiler_params=_cparams(("parallel", "arbitrary")),
    )(u, kk, lf, vv)


def _hgrn_bwd(u, kk, lf, vv, states, do, T, name):
    nb = _pick(T // C, (HG_NB, 3, 2, 1))
    rows = nb * C
    n_steps = T // rows
    qblk = C_HQ // HG_DK

    def body(q_ref, k_ref, lf_ref, v_ref, st_ref, do_ref, dq_ref, dk_ref, dlf_ref, dv_ref, dst, dk_s, dv_s):
        @pl.when(pl.program_id(1) == 0)
        def _():
            dst[...] = jnp.zeros_like(dst)

        lower, upper = _tri(True), _tri(False)
        rid = lax.broadcasted_iota(jnp.int32, (C, 1), 0)
        for n in reversed(range(nb)):
            sl = slice(n * C, (n + 1) * C)
            q, k, v, do = q_ref[sl, :].astype(F32), k_ref[sl, :], v_ref[sl, :], do_ref[sl, :]
            b = jnp.dot(lower, lf_ref[sl, :], precision=_HI, preferred_element_type=F32)
            s_t = st_ref[0, n]
            d_new = dst[...]
            eb = jnp.exp(b)
            bl = b[C - 1:C, :]
            ebl = jnp.exp(bl)
            dec = jnp.exp(bl - b)
            qe = q * eb
            kd = k * dec
            do_b = do.astype(BF16)
            dqe = jnp.dot(do_b, s_t.astype(BF16), preferred_element_type=F32)
            dkd = jnp.dot(v.astype(BF16), d_new.astype(BF16), preferred_element_type=F32)
            dv = lax.dot_general(kd.astype(BF16), d_new.astype(BF16), _NT, preferred_element_type=F32)
            dbl = ebl * jnp.sum(d_new * s_t, axis=0, keepdims=True) + jnp.sum(dkd * kd, axis=0, keepdims=True)
            dst[...] = d_new * ebl + lax.dot_general(do_b, qe.astype(BF16), _TN, preferred_element_type=F32)
            dq_in, dk_in, dv_in = _hg_intra_bwd(q, k, v, b, do, dk_s, dv_s)
            dq = dqe * eb + dq_in
            dk = dkd * dec + dk_in
            dv = dv + dv_in
            db = q * dq - k * dk
            db = db + jnp.where(rid == C - 1, dbl, 0.0)
            dq_ref[sl, :] = dq
            dk_ref[sl, :] = dk
            dv_ref[sl, :] = dv
            dlf_ref[sl, :] = jnp.dot(upper, db, precision=_HI, preferred_element_type=F32)

    rev = lambda off: pl.BlockSpec((rows, HG_DK), lambda h, c: (n_steps - 1 - c, h + off))
    return pl.pallas_call(
        body, name=name, grid=(HG_HEADS, n_steps),
        in_specs=[rev(qblk), rev(0), rev(0), rev(0),
                  pl.BlockSpec((1, nb, HG_DV, HG_DK), lambda h, c: (h, n_steps - 1 - c, 0, 0)), rev(0)],
        out_specs=[rev(0)] * 4,
        out_shape=[jax.ShapeDtypeStruct((T, HG_HEADS * HG_DK), F32)] * 4,
        scratch_shapes=[pltpu.VMEM((HG_DV, HG_DK), F32), pltpu.VMEM((C, HG_DK), F32), pltpu.VMEM((C, HG_DV), F32)],
        compiler_params=_cparams(("parallel", "arbitrary")),
    )(u, kk, lf, vv, states, do)


def _rope_tables(T):
    half = ROPE // 2
    inv_freq = (ROPE_BASE ** (-np.arange(half, dtype=np.float32) / half)).astype(np.float32)
    row = lambda lo, hi, val: np.concatenate([np.zeros(lo, np.float32), np.asarray(val, np.float32) * np.ones(hi - lo, np.float32),
                                              np.zeros(HEAD_W - hi, np.float32)])[None, :]
    freq = row(NOPE, NOPE + half, inv_freq) + row(NOPE + half, NOPE + ROPE, inv_freq)
    pos = lax.broadcasted_iota(jnp.int32, (T, HEAD_W), 0).astype(F32) - float(PAD_FRONT)
    ang = pos * freq
    cos, sin = jnp.cos(ang), jnp.sin(ang)
    c = cos * row(NOPE, NOPE + ROPE, 1.0) + row(0, NOPE, 1.0)
    s1 = sin * row(NOPE, NOPE + half, -1.0)
    s2 = sin * row(NOPE + half, NOPE + ROPE, 1.0)
    return c, s1, s2


def _layer_fwd(x, w, tabs, T, l):
    c, s1, s2 = tabs
    n = lambda s: f"l{l}_{s}"
    sv = {"x": x}
    h = _rowwise(_f_rms, T, [Row(x)], [(w["norm1_g"], D_MODEL)], [(D_MODEL, BF16)], n("norm1"))[0]
    u = _mm(h, w["w_in"], out_dtype=BF16, name=n("in_proj"))
    sv.update(h=h, u=u)
    hglu = _rowwise(_f_glu, T, [Row(u, 512, C_CONV_A), Row(u, 512, C_CONV_G)], [], [(CONV_DIM, F32)], n("glu"))[0]
    cv = _conv_fwd(hglu, w["conv_w"], w["conv_b"], T, n("conv"))
    hc = _rowwise(_f_lnsilu, T, [Row(cv)], [(w["conv_ln_g"], CONV_DIM), (w["conv_ln_b"], CONV_DIM)],
                  [(CONV_DIM, BF16)], n("conv_ln"))[0]
    y_a = _mm(hc, w["w_conv_out"], out_dtype=BF16, name=n("conv_out"))
    sv.update(hglu=hglu, cv=cv, hc=hc, y_a=y_a)
    cqn = _rowwise(_f_rms, T, [Row(u, Q_RANK, C_CQ)], [(w["q_a_norm_g"], Q_RANK)], [(Q_RANK, BF16)], n("q_a_norm"))[0]
    ckvn = _rowwise(_f_rms, T, [Row(u, KV_RANK, C_CKV)], [(w["kv_a_norm_g"], KV_RANK)], [(KV_RANK, BF16)], n("kv_a_norm"))[0]
    q_raw = _mm(cqn, w["w_uq"], out_dtype=BF16, name=n("uq"))
    k_raw = _mm(ckvn, w["w_uk"], out_dtype=BF16, name=n("uk"))
    v = _mm(ckvn, w["w_uv"], out_dtype=BF16, name=n("uv"))
    tab_rows = [Row(c), Row(s1), Row(s2)]
    q = _rowwise(_f_qrope, T, [Row(q_raw, piece=HEAD_W)] + tab_rows, [(w["q_norm_g"], HEAD_W)],
                 [(HEADS * HEAD_W, BF16)], n("q_rope"))[0]
    k = _rowwise(_f_krope, T, [Row(k_raw, piece=HEAD_W), Row(u, HEAD_W, C_KR)] + tab_rows, [(w["k_norm_g"], HEAD_W)],
                 [(HEADS * HEAD_W, BF16)], n("k_rope"))[0]
    o, lse = _attn_fwd(q, k, v, T, n("attn"))
    y_b = _mm(o, w["w_attn_out"], out_dtype=BF16, name=n("attn_out"))
    sv.update(cqn=cqn, ckvn=ckvn, q_raw=q_raw, k_raw=k_raw, v=v, q=q, k=k, o=o, lse=lse, y_b=y_b)
    kk, lf, vv = _rowwise(_f_hgrn_prep, T, [Row(u, 512, C_HF), Row(u, 512, C_HI)], [(w["lb"], 512)],
                          [(512, F32)] * 3, n("hgrn_prep"))
    o_h, states = _hgrn_fwd(u, kk, lf, vv, T, n("hgrn"))
    oh = _rowwise(_f_hgrn_out, T, [Row(o_h, piece=HG_DV), Row(u, 512, C_HG, piece=HG_DV)], [(w["hgrn_norm_g"], HG_DV)],
                  [(512, BF16)], n("hgrn_out_norm"))[0]
    y_c = _mm(oh, w["w_hgrn_out"], out_dtype=BF16, name=n("hgrn_out"))
    sv.update(kk=kk, lf=lf, vv=vv, o_h=o_h, states=states, oh=oh, y_c=y_c)
    gate_rows = [Row(u, D_MODEL, C_GATE + g * D_MODEL) for g in range(3)]
    mix = _rowwise(_f_mix, T, gate_rows + [Row(y_a), Row(y_b), Row(y_c)], [], [(D_MODEL, BF16)], n("mix"))[0]
    x1 = _mm(mix, w["w_out"], res=x, name=n("out_proj"))
    h2 = _rowwise(_f_rms, T, [Row(x1)], [(w["norm2_g"], D_MODEL)], [(D_MODEL, BF16)], n("norm2"))[0]
    f = _mm(h2, w["w_ff1"], out_dtype=BF16, name=n("ff1"))
    x2 = _mm(f, w["w_ff2"], res=x1, a_fn=_relu2, name=n("ff2"))
    sv.update(mix=mix, x1=x1, h2=h2, f=f)
    return x2, sv


def _layer_bwd(dx2, w, sv, tabs, T, l, mid=None, matrices=None):
    c, s1, s2 = tabs
    n = lambda s: f"l{l}_b_{s}"
    u = sv["u"]
    g = {}
    g["w_ff2"] = _mm(sv["f"], dx2, ta=True, a_fn=_relu2, out_dtype=BF16, name=n("dw_ff2"))
    df = _mm(dx2, w["w_ff2"], tb=True, out_dtype=BF16, name=n("d_f"),
             epi=(sv["f"], lambda d, fv: d * (2.0 * jnp.maximum(fv, 0.0))))
    g["w_ff1"] = _mm(sv["h2"], df, ta=True, out_dtype=BF16, name=n("dw_ff1"))
    dh2 = _mm(df, w["w_ff1"], tb=True, name=n("d_h2"))
    (dx1,), (g["norm2_g"],) = _rowwise_bwd(_f_rms, T, [Row(sv["x1"])], [(w["norm2_g"], D_MODEL)], [Row(dh2)],
                                           {0: F32}, n("norm2"), add=(0, dx2))
    g["w_out"] = _mm(sv["mix"], dx1, ta=True, out_dtype=BF16, name=n("dw_out"))
    w_out = w["w_out"] if mid is None else mid(g, w["w_out"])
    dmix = _mm(dx1, w_out, tb=True, out_dtype=BF16, name=n("d_mix"))
    gate_rows = [Row(u, D_MODEL, C_GATE + i * D_MODEL) for i in range(3)]
    (dg0, dg1, dg2, dy_a, dy_b, dy_c), _ = _rowwise_bwd(
        _f_mix, T, gate_rows + [Row(sv["y_a"]), Row(sv["y_b"]), Row(sv["y_c"])], [], [Row(dmix)],
        {0: BF16, 1: BF16, 2: BF16, 3: BF16, 4: BF16, 5: BF16}, n("mix"))
    g["w_hgrn_out"] = _mm(sv["oh"], dy_c, ta=True, out_dtype=BF16, name=n("dw_hgrn_out"))
    doh = _mm(dy_c, w["w_hgrn_out"], tb=True, out_dtype=BF16, name=n("d_oh"))
    (do_h, dhg), (g["hgrn_norm_g"],) = _rowwise_bwd(
        _f_hgrn_out, T, [Row(sv["o_h"], piece=HG_DV), Row(u, 512, C_HG, piece=HG_DV)], [(w["hgrn_norm_g"], HG_DV)],
        [Row(doh, piece=HG_DV)], {0: F32, 1: BF16}, n("hgrn_out_norm"))
    dhq, dkk, dlf, dvv = _hgrn_bwd(u, sv["kk"], sv["lf"], sv["vv"], sv["states"], do_h, T, n("hgrn"))
    (dhf, dhi), (g["lb"],) = _rowwise_bwd(
        _f_hgrn_prep, T, [Row(u, 512, C_HF), Row(u, 512, C_HI)], [(w["lb"], 512)],
        [Row(dkk), Row(dlf), Row(dvv)], {0: BF16, 1: BF16}, n("hgrn_prep"))
    g["w_attn_out"] = _mm(sv["o"], dy_b, ta=True, out_dtype=BF16, name=n("dw_attn_out"))
    do = _mm(dy_b, w["w_attn_out"], tb=True, out_dtype=BF16, name=n("d_o"))
    dq, dk, dv = _attn_bwd(sv["q"], sv["k"], sv["v"], sv["o"], sv["lse"], do, T, n("attn"))
    tab_rows = [Row(c), Row(s1), Row(s2)]
    (dq_raw,), (g["q_norm_g"],) = _rowwise_bwd(
        _f_qrope, T, [Row(sv["q_raw"], piece=HEAD_W)] + tab_rows, [(w["q_norm_g"], HEAD_W)],
        [Row(dq, piece=HEAD_W)], {0: BF16}, n("q_rope"))
    (dk_raw, dkr), (g["k_norm_g"],) = _rowwise_bwd(
        _f_krope, T, [Row(sv["k_raw"], piece=HEAD_W), Row(u, HEAD_W, C_KR)] + tab_rows, [(w["k_norm_g"], HEAD_W)],
        [Row(dk, piece=HEAD_W)], {0: BF16, 1: BF16}, n("k_rope"))
    g["w_uq"] = _mm(sv["cqn"], dq_raw, ta=True, out_dtype=BF16, name=n("dw_uq"))
    g["w_uk"] = _mm(sv["ckvn"], dk_raw, ta=True, out_dtype=BF16, name=n("dw_uk"))
    g["w_uv"] = _mm(sv["ckvn"], dv, ta=True, out_dtype=BF16, name=n("dw_uv"))
    dcqn = _mm(dq_raw, w["w_uq"], tb=True, out_dtype=BF16, name=n("d_cqn"))
    dckvn = _mm(dk_raw, w["w_uk"], tb=True, name=n("d_ckvn_k"))
    dckvn = _mm(dv, w["w_uv"], tb=True, res=dckvn, out_dtype=BF16, name=n("d_ckvn_v"))
    (dcq,), (g["q_a_norm_g"],) = _rowwise_bwd(_f_rms, T, [Row(u, Q_RANK, C_CQ)], [(w["q_a_norm_g"], Q_RANK)],
                                              [Row(dcqn)], {0: BF16}, n("q_a_norm"))
    (dckv,), (g["kv_a_norm_g"],) = _rowwise_bwd(_f_rms, T, [Row(u, KV_RANK, C_CKV)], [(w["kv_a_norm_g"], KV_RANK)],
                                                [Row(dckvn)], {0: BF16}, n("kv_a_norm"))
    g["w_conv_out"] = _mm(sv["hc"], dy_a, ta=True, out_dtype=BF16, name=n("dw_conv_out"))
    dhc = _mm(dy_a, w["w_conv_out"], tb=True, out_dtype=BF16, name=n("d_hc"))
    (dcv,), (g["conv_ln_g"], g["conv_ln_b"]) = _rowwise_bwd(
        _f_lnsilu, T, [Row(sv["cv"])], [(w["conv_ln_g"], CONV_DIM), (w["conv_ln_b"], CONV_DIM)], [Row(dhc)],
        {0: F32}, n("conv_ln"))
    dhglu, dconv_w, g["conv_b"] = _conv_bwd(sv["hglu"], w["conv_w"], dcv, T, n("conv"))
    g["conv_w"] = dconv_w[:CONV_K]
    (dua, dug), _ = _rowwise_bwd(_f_glu, T, [Row(u, 512, C_CONV_A), Row(u, 512, C_CONV_G)], [], [Row(dhglu)],
                                 {0: BF16, 1: BF16}, n("glu"))
    du = jnp.concatenate([dua, dug, dg0, dg1, dg2, dcq, dckv, dkr, dhq.astype(BF16), dhf, dhi, dhg], axis=1)
    small = ("w_uq", "w_uk", "w_uv", "w_attn_out", "w_hgrn_out", "w_conv_out")
    du, *done = lax.optimization_barrier((du, *[g[k] for k in small]))
    g.update(zip(small, done))
    g["w_in"] = _mm(sv["h"], du, ta=True, out_dtype=BF16, name=n("dw_in"))
    norm_g = w["norm1_g"] if matrices is None else matrices(g, w["norm1_g"])
    du, norm_g = lax.optimization_barrier((du, norm_g))
    dh = _mm(du, w["w_in"], tb=True, name=n("d_h"))
    (dx,), (g["norm1_g"],) = _rowwise_bwd(_f_rms, T, [Row(sv["x"])], [(norm_g, D_MODEL)], [Row(dh)],
                                          {0: F32}, n("norm1"), add=(0, dx1))
    return dx, g


def _pad_w_in(w_in):
    z = lambda k: jnp.zeros((w_in.shape[0], k), w_in.dtype)
    return jnp.concatenate([w_in[:, :O_CQ], w_in[:, O_GATE:], w_in[:, O_CQ:O_KR], z(KR_LANE), w_in[:, O_KR:O_HQ],
                            z(HEAD_W - KR_LANE - ROPE), w_in[:, O_HQ:O_GATE]], axis=1)


def _unpad_w_in(g):
    return jnp.concatenate([g[:, :C_GATE], g[:, C_CQ:C_KR], g[:, C_KR + KR_LANE:C_KR + KR_LANE + ROPE],
                            g[:, C_HQ:], g[:, C_GATE:C_CQ]], axis=1)


_W_IN_RUNS = ((0, 0, O_CQ), (O_CQ, C_CQ, O_KR - O_CQ), (O_KR, C_KR + KR_LANE, ROPE), (O_HQ, C_HQ, O_GATE - O_HQ),
              (O_GATE, C_GATE, N_IN - O_GATE))


def _w_in_from_shards(g8):
    per = N_IN // N_DEV
    pieces, at = [], 0
    for o0, p0, n in sorted(_W_IN_RUNS, key=lambda r: r[1]):
        if p0 > at:
            pieces.append(jnp.zeros((g8.shape[1], p0 - at), g8.dtype))
        for j in range(o0 // per, (o0 + n - 1) // per + 1):
            lo, hi = max(o0, j * per), min(o0 + n, (j + 1) * per)
            pieces.append(g8[j][:, lo - j * per:hi - j * per])
        at = p0 + n
    if at < N_IN_P:
        pieces.append(jnp.zeros((g8.shape[1], N_IN_P - at), g8.dtype))
    return jnp.concatenate(pieces, axis=1)


def _w_in_grad_shards(g):
    per = N_IN // N_DEV
    shards = []
    for j in range(N_DEV):
        lo, hi = j * per, (j + 1) * per
        pieces = [g[:, p0 + max(lo, o0) - o0:p0 + min(hi, o0 + n) - o0]
                  for o0, p0, n in _W_IN_RUNS if max(lo, o0) < min(hi, o0 + n)]
        shards.append(jnp.concatenate(pieces, axis=1) if len(pieces) > 1 else pieces[0])
    return jnp.stack(shards)


def _pad_heads(wm, per_head, lo, hi):
    lead = wm.shape[:-1]
    wh = wm.reshape(lead + (HEADS, per_head))[..., lo:hi]
    pad = [(0, 0)] * len(lead) + [(0, 0), (0, HEAD_W - (hi - lo))]
    return jnp.pad(wh, pad).reshape(lead + (HEADS * HEAD_W,))


def _unpad_heads(gm, width):
    lead = gm.shape[:-1]
    return gm.reshape(lead + (HEADS, HEAD_W))[..., :width]


def _layer_weights(full, lb):
    w = {}
    w["norm1_g"] = full["norm1_g"]
    w["w_in"] = full["w_in_padded"] if "w_in_padded" in full else _pad_w_in(full["w_in"])
    w["conv_w"] = full["conv_w"]
    w["conv_b"] = full["conv_b"]
    w["conv_ln_g"] = full["conv_ln_g"]
    w["conv_ln_b"] = full["conv_ln_b"]
    w["w_conv_out"] = full["w_conv_out"]
    w["q_a_norm_g"] = full["q_a_norm_g"]
    w["w_uq"] = _pad_heads(full["w_uq"], QK_DIM, 0, QK_DIM)
    w["kv_a_norm_g"] = full["kv_a_norm_g"]
    w["w_uk"] = _pad_heads(full["w_ukv"], NOPE + V_DIM, 0, NOPE)
    w["w_uv"] = _pad_heads(full["w_ukv"], NOPE + V_DIM, NOPE, NOPE + V_DIM)
    w["q_norm_g"] = jnp.pad(full["q_norm_g"], (0, HEAD_W - QK_DIM))
    w["k_norm_g"] = jnp.pad(full["k_norm_g"], (0, HEAD_W - QK_DIM))
    wa = full["w_attn_out"].reshape(HEADS, V_DIM, D_MODEL)
    w["w_attn_out"] = jnp.pad(wa, ((0, 0), (0, HEAD_W - V_DIM), (0, 0))).reshape(HEADS * HEAD_W, D_MODEL)
    w["lb"] = lb
    w["hgrn_norm_g"] = full["hgrn_norm_g"]
    w["w_hgrn_out"] = full["w_hgrn_out"]
    w["w_out"] = full["w_out"]
    w["norm2_g"] = full["norm2_g"]
    w["w_ff1"] = full["w_ff1"]
    w["w_ff2"] = full["w_ff2"]
    return w


def _matrix_grads_to_original(g):
    o = {name: g[name] for name in ("w_conv_out", "w_hgrn_out", "w_out", "w_ff1", "w_ff2")}
    o["w_in"] = _unpad_w_in(g["w_in"])
    o["w_in_shards"] = _w_in_grad_shards(g["w_in"])
    o["w_uq"] = _unpad_heads(g["w_uq"], QK_DIM).reshape(Q_RANK, HEADS * QK_DIM)
    guk = _unpad_heads(g["w_uk"], NOPE)
    guv = _unpad_heads(g["w_uv"], V_DIM)
    o["w_ukv"] = jnp.concatenate([guk, guv], axis=-1).reshape(KV_RANK, HEADS * (NOPE + V_DIM))
    o["w_attn_out"] = g["w_attn_out"].reshape(HEADS, HEAD_W, D_MODEL)[:, :V_DIM].reshape(HEADS * V_DIM, D_MODEL)
    return o


def _vector_grads_to_original(g):
    o = {"conv_w": g["conv_w"]}
    for name in ("norm1_g", "conv_b", "conv_ln_g", "conv_ln_b", "q_a_norm_g", "kv_a_norm_g", "hgrn_norm_g", "norm2_g", "lb"):
        o[name] = g[name].reshape(-1)
    o["q_norm_g"] = g["q_norm_g"].reshape(-1)[:QK_DIM]
    o["k_norm_g"] = g["k_norm_g"].reshape(-1)[:QK_DIM]
    return o


def _lower_bounds(logits):
    p = jax.nn.softmax(logits.astype(F32), axis=0)
    return jnp.cumsum(p, axis=0) - p[0:1]


def _run_step(x, target, meta, lb_logits, layer_weights, layer_done, layer_mid=None, layer_matrices=None):
    seq = x.shape[0]
    T = ROW0 + seq
    assert T % 128 == 0
    tabs = _rope_tables(T)
    lbs, lb_vjp = jax.vjp(_lower_bounds, lb_logits)
    xp = jnp.concatenate([jnp.zeros((PAD_FRONT, D_MODEL), F32), meta.astype(F32), x], axis=0)
    tp = jnp.concatenate([jnp.zeros((ROW0, D_MODEL), F32), target], axis=0)
    ws, svs = [], []
    for l in range(DEPTH):
        full, xp = layer_weights(l, xp)
        w = _layer_weights(full, lbs[l])
        xp, sv = _layer_fwd(xp, w, tabs, T, l)
        ws.append(w)
        svs.append(sv)
    dx, sq = _loss_head(xp, tp, T)
    loss = 0.5 * jnp.sum(sq) * (1.0 / D_MODEL)
    dlb = [None] * DEPTH
    for l in reversed(range(DEPTH)):
        mid = None if layer_mid is None else functools.partial(layer_mid, l)
        mats = {}

        def matrices(g, norm_g, l=l, mats=mats):
            mats.update(_matrix_grads_to_original(g))
            return norm_g if layer_matrices is None else layer_matrices(l, mats, norm_g)

        dx, g = _layer_bwd(dx, ws[l], svs[l], tabs, T, l, mid, matrices)
        g = {**_vector_grads_to_original(g), **mats}
        dlb[l] = g.pop("lb")
        dx = layer_done(l, g, dx)
    return loss, dx[ROW0:], dx[PAD_FRONT:ROW0], lb_vjp(jnp.stack(dlb))[0]


def _local_step(x, target, full):
    per_layer = [None] * DEPTH

    def done(l, g, dx):
        per_layer[l] = g
        return dx

    loss, gx, gmeta, glb = _run_step(
        x, target, full["meta"], full["hgrn_lb_logits"],
        lambda l, xp: ({k: v[l] for k, v in full.items() if k != "meta"}, xp), done)
    grads = {k: jnp.stack([per_layer[l][k] for l in range(DEPTH)]) for k in per_layer[0]}
    grads["hgrn_lb_logits"] = glb
    grads["meta"] = gmeta
    return loss, gx, grads


def _mesh_pos():
    return lax.axis_index("x"), lax.axis_index("y"), lax.axis_index("c")


N_COPY = N_DEV - 1


def _all_gather(arrs, name):
    n = len(arrs)

    def body(*refs):
        x_refs, out_refs = refs[:n], refs[n:2 * n]
        send_sems, recv_sems, local_sems = refs[2 * n:]
        x, y, c = _mesh_pos()
        me, sibling = (x, y, c), (x, y, 1 - c)
        chips = [(1 - x, y), (x, 1 - y), (1 - x, 1 - y)]

        def slot(a, px, py, pc):
            return out_refs[a].at[4 * px + 2 * py + pc]

        def copy(a, k, block, to, own=False):
            return pltpu.make_async_remote_copy(
                src_ref=x_refs[a] if own else slot(a, *block), dst_ref=slot(a, *block),
                send_sem=send_sems.at[a * N_COPY + k], recv_sem=recv_sems.at[a * N_COPY + k],
                device_id=to, device_id_type=MESH)

        mine = [pltpu.make_async_copy(x_refs[a], slot(a, *me), local_sems.at[a]) for a in range(n)]
        for cp in mine:
            cp.start()
        first = []
        for a in range(n):
            first.append(copy(a, 0, me, sibling, own=True))
            first += [copy(a, 1 + j, me, (*chip, c), own=True) for j, chip in enumerate(chips)]
        for cp in first:
            cp.start()
        passed = []
        for j, chip in enumerate(chips):
            for a in range(n):
                copy(a, 1 + j, (*chip, c), me).wait_recv()
                cp = copy(a, 4 + j, (*chip, c), sibling)
                cp.start()
                passed.append(cp)
        for a in range(n):
            copy(a, 0, sibling, me).wait_recv()
            for j, chip in enumerate(chips):
                copy(a, 4 + j, (*chip, 1 - c), me).wait_recv()
        for cp in first + passed:
            cp.wait_send()
        for cp in mine:
            cp.wait()

    anyspec = pl.BlockSpec(memory_space=pl.ANY)
    return pl.pallas_call(
        body, name=name, out_shape=[jax.ShapeDtypeStruct((N_DEV,) + a.shape, a.dtype) for a in arrs],
        in_specs=[anyspec] * n, out_specs=[anyspec] * n,
        scratch_shapes=[pltpu.SemaphoreType.DMA((n * N_COPY,)), pltpu.SemaphoreType.DMA((n * N_COPY,)),
                        pltpu.SemaphoreType.DMA((n,))],
    )(*arrs)


def _exchange(arrs, name):
    n = len(arrs)

    def body(*refs):
        s_refs, r_refs = refs[:n], refs[n:2 * n]
        send_sems, recv_sems, local_sems = refs[2 * n:]
        x, y, c = _mesh_pos()
        me = 4 * x + 2 * y + c
        local = [pltpu.make_async_copy(s_refs[a].at[me], r_refs[a].at[me], local_sems.at[a]) for a in range(n)]
        for cp in local:
            cp.start()
        sends, recvs = [], []
        for rel in range(1, N_DEV):
            px = 1 - x if rel & 4 else x
            py = 1 - y if rel & 2 else y
            pc = 1 - c if rel & 1 else c
            p = 4 * px + 2 * py + pc
            for a in range(n):
                k = a * N_COPY + rel - 1
                sends.append(pltpu.make_async_remote_copy(
                    src_ref=s_refs[a].at[p], dst_ref=r_refs[a].at[me], send_sem=send_sems.at[k],
                    recv_sem=recv_sems.at[k], device_id=(px, py, pc), device_id_type=MESH))
                recvs.append(pltpu.make_async_remote_copy(
                    src_ref=s_refs[a].at[me], dst_ref=r_refs[a].at[p], send_sem=send_sems.at[k],
                    recv_sem=recv_sems.at[k], device_id=(px, py, pc), device_id_type=MESH))
        for cp in sends:
            cp.start()
        for cp in recvs:
            cp.wait_recv()
        for cp in sends:
            cp.wait_send()
        for cp in local:
            cp.wait()

    anyspec = pl.BlockSpec(memory_space=pl.ANY)
    return pl.pallas_call(
        body, name=name, out_shape=[jax.ShapeDtypeStruct(a.shape, a.dtype) for a in arrs],
        in_specs=[anyspec] * n, out_specs=[anyspec] * n,
        scratch_shapes=[pltpu.SemaphoreType.DMA((n * N_COPY,)), pltpu.SemaphoreType.DMA((n * N_COPY,)),
                        pltpu.SemaphoreType.DMA((n,))],
    )(*arrs)


_HBM = pl.BlockSpec(memory_space=pltpu.HBM)
_SEM = pl.BlockSpec(memory_space=pltpu.SEMAPHORE)
_EFFECT = pltpu.SideEffectType.DATAFLOW_SIDE_EFFECTING


def _peers(x, y, c):
    out = []
    for rel in range(1, N_DEV):
        px = 1 - x if rel & 4 else x
        py = 1 - y if rel & 2 else y
        pc = 1 - c if rel & 1 else c
        out.append((rel, (px, py, pc), 4 * px + 2 * py + pc))
    return out


def _split_copies(src_refs, land_refs, send_sems, recv_sems, gather):
    x, y, c = _mesh_pos()
    me = 4 * x + 2 * y + c
    out = []
    for a, (src, land) in enumerate(zip(src_refs, land_refs)):
        for rel, peer, p in _peers(x, y, c):
            k = a * N_COPY + rel - 1
            mk = lambda s, d: pltpu.make_async_remote_copy(
                src_ref=s, dst_ref=d, send_sem=send_sems.at[k], recv_sem=recv_sems.at[k],
                device_id=peer, device_id_type=MESH)
            mine = src if gather else src.at[p]
            out.append((mk(mine, land.at[me]), mk(mine, land.at[p])))
    return out


def _copy_start(srcs, gather, name, collective_id):
    n = len(srcs)
    lands = [lax.empty(((N_DEV,) + s.shape) if gather else s.shape, s.dtype) for s in srcs]

    def body(*refs):
        src_refs, land_refs = refs[:n], refs[n:2 * n]
        send_sems, recv_sems = refs[2 * n], refs[2 * n + 1]
        token = refs[-1]
        x, y, c = _mesh_pos()
        barrier = pltpu.get_barrier_semaphore()
        for _, peer, _ in _peers(x, y, c):
            pl.semaphore_signal(barrier, inc=1, device_id=peer, device_id_type=MESH)
        pl.semaphore_wait(barrier, N_COPY)
        for out_copy, _ in _split_copies(src_refs, land_refs, send_sems, recv_sems, gather):
            out_copy.start()
        token[...] = jnp.zeros_like(token)

    hbm = lambda a: pltpu.HBM(a.shape, a.dtype)
    res = pl.pallas_call(
        body, name=name,
        out_shape=(pltpu.SemaphoreType.DMA((n * N_COPY,)), pltpu.SemaphoreType.DMA((n * N_COPY,)),
                   *[hbm(s) for s in srcs], *[hbm(z) for z in lands], jax.ShapeDtypeStruct((8, 128), F32)),
        in_specs=[_HBM] * (2 * n), out_specs=(_SEM, _SEM, *([_HBM] * (2 * n)), pl.BlockSpec(memory_space=pltpu.VMEM)),
        input_output_aliases={i: 2 + i for i in range(2 * n)},
        compiler_params=pltpu.CompilerParams(has_side_effects=_EFFECT, collective_id=collective_id),
    )(*[pltpu.with_memory_space_constraint(s, pltpu.HBM) for s in srcs],
      *[pltpu.with_memory_space_constraint(z, pltpu.HBM) for z in lands])
    return res[0], res[1], list(res[2:2 + n]), list(res[2 + n:2 + 2 * n]), res[-1]


def _copy_wait(send_sems, recv_sems, srcs, lands, after, gather, name):
    n = len(srcs)

    def body(*refs):
        src_refs, land_refs = refs[:n], refs[n:2 * n]
        s_sems, r_sems = refs[2 * n], refs[2 * n + 1]
        for out_copy, in_copy in _split_copies(src_refs, land_refs, s_sems, r_sems, gather):
            out_copy.wait_send()
            in_copy.wait_recv()

    hbm = lambda a: pltpu.HBM(a.shape, a.dtype)
    res = pl.pallas_call(
        body, name=name, out_shape=(*[hbm(s) for s in srcs], *[hbm(z) for z in lands]),
        in_specs=[_HBM] * (2 * n) + [_SEM, _SEM, pl.BlockSpec(memory_space=pl.ANY)], out_specs=tuple([_HBM] * (2 * n)),
        input_output_aliases={i: i for i in range(2 * n)},
        compiler_params=pltpu.CompilerParams(has_side_effects=_EFFECT),
    )(*srcs, *lands, send_sems, recv_sems, after)
    return list(res[:n]), list(res[n:])


def _sum_parts(parts, name):
    P, R, W = parts.shape

    def body(p_ref, o_ref):
        g = p_ref[0].astype(F32)
        for i in range(1, P):
            g = g + p_ref[i].astype(F32)
        o_ref[...] = g

    return pl.pallas_call(body, name=name, out_shape=jax.ShapeDtypeStruct((R, W), F32))(parts)


def _adamw_body(p_ref, w_ref, m_ref, v_ref, g_ref, d_ref, nm_ref, nv_ref):
    g = p_ref[0].astype(F32)
    for i in range(1, p_ref.shape[0]):
        g = g + p_ref[i].astype(F32)
    _adamw_apply(g, w_ref, m_ref, v_ref, g_ref, d_ref, nm_ref, nv_ref)


def _adamw_apply(g, w_ref, m_ref, v_ref, g_ref, d_ref, nm_ref, nv_ref):
    m_new = ADAM_B1 * m_ref[...] + (1.0 - ADAM_B1) * g
    v_new = ADAM_B2 * v_ref[...] + (1.0 - ADAM_B2) * jnp.square(g)
    m_hat = m_new / (1.0 - ADAM_B1 ** ADAM_STEP)
    v_hat = v_new / (1.0 - ADAM_B2 ** ADAM_STEP)
    g_ref[...] = g
    d_ref[...] = -ADAM_LR * (m_hat / (jnp.sqrt(v_hat) + ADAM_EPS) + ADAM_WD * w_ref[...])
    nm_ref[...] = m_new
    nv_ref[...] = v_new


def _adamw(parts, w, m, v, name):
    P, R, W = parts.shape
    tr = _pick(R, (368, 192, 64, 16, 8))
    spec = pl.BlockSpec((tr, W), lambda i: (i, 0))
    return pl.pallas_call(
        functools.partial(_adamw_body), name=name, grid=(R // tr,),
        in_specs=[pl.BlockSpec((P, tr, W), lambda i: (0, i, 0)), spec, spec, spec], out_specs=[spec] * 4,
        out_shape=[jax.ShapeDtypeStruct((R, W), F32)] * 4,
        compiler_params=_cparams(("parallel",)),
    )(parts, w, m, v)


def _adamw_layers(parts, w, m, v, name):
    P, B, C_ = parts[0].shape
    tb = _pick(B, (256, 128))
    nb = B // tb

    def body(*refs):
        p_refs, rest = refs[:DEPTH], refs[DEPTH:]
        a = pl.program_id(0)
        for l in range(DEPTH):
            @pl.when(a == l)
            def _():
                _adamw_body(p_refs[l], *[r.at[0] for r in rest])

    spec = pl.BlockSpec((1, tb, C_), lambda a, i: (a, i, 0))

    def part_spec(l):
        return pl.BlockSpec((P, tb, C_), lambda a, i: (0, jnp.where(a == l, i, jnp.where(a < l, 0, nb - 1)), 0))

    return pl.pallas_call(
        body, name=name, grid=(DEPTH, nb),
        in_specs=[part_spec(l) for l in range(DEPTH)] + [spec, spec, spec], out_specs=[spec] * 4,
        out_shape=[jax.ShapeDtypeStruct((DEPTH, B, C_), F32)] * 4,
        compiler_params=_cparams(("arbitrary", "arbitrary")),
    )(*parts, w, m, v)


VEC_GROUPS = (("norm1_g", "norm2_g"), ("conv_b", "conv_ln_g", "conv_ln_b", "hgrn_lb_logits", "hgrn_norm_g"),
              ("q_a_norm_g",), ("kv_a_norm_g",), ("q_norm_g", "k_norm_g"))
SMALL_NAMES = tuple(n for grp in VEC_GROUPS for n in grp) + ("meta", "conv_w")


def _adamw_small(own, lands, wts, mom, var, name):
    n_in = len(own)

    def body(*refs):
        own_r, land_r = refs[:n_in], refs[n_in:2 * n_in]
        rest = iter(refs[2 * n_in:])
        wmv = {n: (next(rest), next(rest), next(rest)) for n in SMALL_NAMES}
        outs = {n: (next(rest), next(rest), next(rest), next(rest)) for n in SMALL_NAMES}
        loss_ref = next(rest)
        x, y, c = _mesh_pos()
        me = 4 * x + 2 * y + c

        def total(k):
            acc = None
            for s in range(N_DEV):
                v = jnp.where(me == s, own_r[k][...], land_r[k][s])
                acc = v if acc is None else acc + v
            return acc

        for k, grp in enumerate(VEC_GROUPS):
            tot = total(k)
            for j, n in enumerate(grp):
                _adamw_apply(tot[DEPTH * j:DEPTH * (j + 1)], *wmv[n], *outs[n])
        loss_ref[...] = total(len(VEC_GROUPS))
        _adamw_apply(total(n_in - 2), *wmv["meta"], *outs["meta"])
        _adamw_apply(total(n_in - 1), *wmv["conv_w"], *outs["conv_w"])

    args = list(own) + list(lands) + [d[n] for n in SMALL_NAMES for d in (wts, mom, var)]
    out_shape = [jax.ShapeDtypeStruct(wts[n].shape, F32) for n in SMALL_NAMES for _ in range(4)]
    res = pl.pallas_call(body, name=name, out_shape=out_shape + [jax.ShapeDtypeStruct((1, 128), F32)])(*args)
    out = {}
    for i, n in enumerate(SMALL_NAMES):
        for j, kind in enumerate(("grad_", "delta_", "new_m_", "new_v_")):
            out[kind + n] = res[4 * i + j]
    return out, res[-1]


PACK_W = 1024
BIG = (("w_in", (DEPTH, D_MODEL, N_IN // N_DEV), 2), ("w_conv_out", (DEPTH, CONV_DIM, D_MODEL // N_DEV), 2),
       ("w_uq", (DEPTH, Q_RANK, HEADS * QK_DIM // N_DEV), 2), ("w_ukv", (DEPTH, KV_RANK, HEADS * (NOPE + V_DIM) // N_DEV), 2),
       ("w_attn_out", (DEPTH, HEADS * V_DIM, D_MODEL // N_DEV), 2), ("w_hgrn_out", (DEPTH, 512, D_MODEL // N_DEV), 2),
       ("w_out", (DEPTH, D_MODEL // N_DEV, D_MODEL), 1), ("w_ff1", (DEPTH, D_MODEL, D_FF // N_DEV), 2),
       ("w_ff2", (DEPTH, D_FF // N_DEV, D_MODEL), 1))
SMALL_SHARDED = (("meta", (N_META, D_MODEL // N_DEV), 1), ("conv_w", (DEPTH, CONV_K, CONV_DIM // N_DEV), 2))
REPLICATED = (("norm1_g", (DEPTH, D_MODEL)), ("conv_b", (DEPTH, CONV_DIM)), ("conv_ln_g", (DEPTH, CONV_DIM)),
              ("conv_ln_b", (DEPTH, CONV_DIM)), ("q_a_norm_g", (DEPTH, Q_RANK)), ("kv_a_norm_g", (DEPTH, KV_RANK)),
              ("q_norm_g", (DEPTH, QK_DIM)), ("k_norm_g", (DEPTH, QK_DIM)), ("hgrn_lb_logits", (DEPTH, 512)),
              ("hgrn_norm_g", (DEPTH, 512)), ("norm2_g", (DEPTH, D_MODEL)))
WEIGHT_ORDER = ("meta", "norm1_g", "w_in", "conv_w", "conv_b", "conv_ln_g", "conv_ln_b", "w_conv_out", "q_a_norm_g", "w_uq",
                "kv_a_norm_g", "w_ukv", "q_norm_g", "k_norm_g", "w_attn_out", "hgrn_lb_logits", "hgrn_norm_g", "w_hgrn_out",
                "w_out", "norm2_g", "w_ff1", "w_ff2")


def _rows_for(n_elems, mult):
    rows = -(-n_elems // PACK_W)
    return -(-rows // mult) * mult


def _pack(arrays, dtype, mult, lead=()):
    nl = len(lead)
    flat = jnp.concatenate([a.reshape(lead + (-1,)).astype(dtype) for a in arrays], axis=nl)
    rows = _rows_for(flat.shape[nl], mult)
    flat = jnp.pad(flat, [(0, 0)] * nl + [(0, rows * PACK_W - flat.shape[nl])])
    return flat.reshape(lead + (rows, PACK_W))


def _unpack(pack, shapes, lead=()):
    nl = len(lead)
    flat = pack.reshape(lead + (-1,))
    out, off = [], 0
    for shp in shapes:
        n = int(np.prod(shp))
        out.append(lax.slice_in_dim(flat, off, off + n, axis=nl).reshape(lead + tuple(shp)))
        off += n
    return out


def _join_shards(g, axis):
    g = jnp.moveaxis(g, 0, axis)
    shp = g.shape
    return g.reshape(shp[:axis] + (shp[axis] * shp[axis + 1],) + shp[axis + 2:])


def _cut_shards(a, axis):
    shp = a.shape
    a = a.reshape(shp[:axis] + (N_DEV, shp[axis] // N_DEV) + shp[axis + 1:])
    return jnp.moveaxis(a, axis, 0)


def kernel(x, meta, norm1_g, w_in, conv_w, conv_b, conv_ln_g, conv_ln_b, w_conv_out, q_a_norm_g, w_uq, kv_a_norm_g, w_ukv, q_norm_g, k_norm_g, w_attn_out, hgrn_lb_logits, hgrn_norm_g, w_hgrn_out, w_out, norm2_g, w_ff1, w_ff2, loss_target, m_meta, m_norm1_g, m_w_in, m_conv_w, m_conv_b, m_conv_ln_g, m_conv_ln_b, m_w_conv_out, m_q_a_norm_g, m_w_uq, m_kv_a_norm_g, m_w_ukv, m_q_norm_g, m_k_norm_g, m_w_attn_out, m_hgrn_lb_logits, m_hgrn_norm_g, m_w_hgrn_out, m_w_out, m_norm2_g, m_w_ff1, m_w_ff2, v_meta, v_norm1_g, v_w_in, v_conv_w, v_conv_b, v_conv_ln_g, v_conv_ln_b, v_w_conv_out, v_q_a_norm_g, v_w_uq, v_kv_a_norm_g, v_w_ukv, v_q_norm_g, v_k_norm_g, v_w_attn_out, v_hgrn_lb_logits, v_hgrn_norm_g, v_w_hgrn_out, v_w_out, v_norm2_g, v_w_ff1, v_w_ff2):
    args = dict(locals())
    wts = {n: args[n] for n in WEIGHT_ORDER}
    mom = {n: args["m_" + n] for n in WEIGHT_ORDER}
    var = {n: args["v_" + n] for n in WEIGHT_ORDER}
    xi, yi, ci = _mesh_pos()
    me = 4 * xi + 2 * yi + ci

    shard = lambda l: [wts[n][l].astype(BF16) for n, _, _ in BIG]
    gathered = _all_gather(shard(0) + [_pack([wts[n] for n, _, _ in SMALL_SHARDED], F32, 8)], "gather_layer0")
    small = dict(zip([n for n, _, _ in SMALL_SHARDED],
                     [_join_shards(g, axis) for (_, _, axis), g in
                      zip(SMALL_SHARDED, _unpack(gathered[-1], [s for _, s, _ in SMALL_SHARDED], (N_DEV,)))]))
    pending = _copy_start(shard(1), True, "gather_layer1_start", 5)

    def layer_weights(l, xp):
        full = {n: wts[n][l] for n, _ in REPLICATED}
        if l == 0:
            mats = gathered[:-1]
            full["norm1_g"] = full["norm1_g"] + pending[4][0, 0]
        else:
            own, lands = _copy_wait(pending[0], pending[1], pending[2], pending[3], xp, True, "gather_layer1_wait")
            mats = [lax.dynamic_update_index_in_dim(z, s, me, 0) for z, s in zip(lands, own)]
        full["conv_w"] = small["conv_w"][l]
        for (n, _, axis), g in zip(BIG, mats):
            if n == "w_in":
                full["w_in_padded"] = _w_in_from_shards(g)
            else:
                full[n] = _join_shards(g, axis - 1)
        return full, xp

    big_names = [n for n, _, _ in BIG]
    early = [n for n in big_names if n in ("w_out", "w_ff1", "w_ff2")]
    late = [n for n in big_names if n not in early]
    cut = lambda g, names: [(g[n + "_shards"] if n + "_shards" in g else _cut_shards(g[n], axis - 1)).astype(BF16)
                            for n, _, axis in BIG if n in names]
    layer_grads = [None] * DEPTH
    flight = {}

    def layer_mid(l, g, w_out):
        if l == 0:
            flight["l0_early"] = _copy_start(cut(g, early), False, "scatter_layer0_early_start", 7)
            w_out = w_out + flight["l0_early"][4][0, 0].astype(w_out.dtype)
        return w_out

    def layer_done(l, g, dx):
        layer_grads[l] = g
        if l == 1:
            flight["l1"] = _copy_start(cut(g, big_names), False, "scatter_l1_start", 6)
            dx = dx + flight["l1"][4][0, 0]
        return dx

    def layer_matrices(l, mats, norm_g):
        if l == 0:
            flight["l0_late"] = _copy_start(cut(mats, late), False, "scatter_l0_late_start", 8)
            norm_g = norm_g + flight["l0_late"][4][0, 0]
        return norm_g

    loss, grad_x, g_meta, g_lb = _run_step(x[0], loss_target[0], small["meta"], wts["hgrn_lb_logits"],
                                           layer_weights, layer_done, layer_mid, layer_matrices)

    grads = {k: jnp.stack([layer_grads[l][k] for l in range(DEPTH)]) for k in layer_grads[0]
             if k not in big_names and not k.endswith("_shards")}
    grads["hgrn_lb_logits"] = g_lb
    own = [jnp.concatenate([grads[n] for n in grp], axis=0) for grp in VEC_GROUPS]
    own.append(jnp.broadcast_to(loss.reshape(1, 1), (1, 128)))
    flight["small"] = _copy_start(own, True, "gather_small_grads_start", 9)
    cuts = [_cut_shards(g_meta, 1), _cut_shards(grads["conv_w"], 2)]
    flight["small_x"] = _copy_start(cuts, False, "scatter_small_grads_start", 10)
    started = flight["small_x"][4]

    def arrive(key, names, after):
        s_sems, r_sems, sent, lands, _ = flight[key]
        sent, lands = _copy_wait(s_sems, r_sems, sent, lands, after, False, f"scatter_{key}_wait")
        return {n: lax.dynamic_update_index_in_dim(z, lax.dynamic_index_in_dim(s, me, 0, keepdims=False), me, 0)
                for n, z, s in zip(names, lands, sent)}

    out = {}

    def update(names, recv0, recv1):
        for n in names:
            res4 = _adamw_layers([recv0[n], recv1[n]], wts[n], mom[n], var[n], "adamw_" + n)
            for kind, a in zip(("grad_", "delta_", "new_m_", "new_v_"), res4):
                out[kind + n] = a

    recv1 = arrive("l1", big_names, started)
    recv0 = arrive("l0_early", early, started)
    update(early, recv0, recv1)

    s_sems, r_sems, sent, lands, _ = flight["small"]
    updated = lax.optimization_barrier(tuple(out["grad_" + n] for n in early))
    own, lands = _copy_wait(s_sems, r_sems, sent, lands, updated[0], True, "gather_small_grads_wait")
    s_sems, r_sems, sent, lands_x, _ = flight["small_x"]
    sent, lands_x = _copy_wait(s_sems, r_sems, sent, lands_x, updated[0], False, "scatter_small_grads_wait")
    own += [lax.dynamic_index_in_dim(s, me, 0, keepdims=False) for s in sent]
    small_out, loss = _adamw_small(own, lands + lands_x, wts, mom, var, "adamw_small")
    out.update(small_out)
    loss = loss[0, 0]

    recv0 = arrive("l0_late", late, small_out["grad_norm1_g"])
    update(late, recv0, recv1)

    res = [loss, grad_x[None]]
    for kind in ("grad_", "delta_", "new_m_", "new_v_"):
        res += [out[kind + n] for n in WEIGHT_ORDER]
    return tuple(res)
```

```python
import functools

import numpy as np
import jax
import jax.numpy as jnp
from jax import lax
from jax.experimental import pallas as pl
from jax.experimental.pallas import tpu as pltpu

F32 = jnp.float32
BF16 = jnp.bfloat16

D_MODEL = 1024
DEPTH = 2
N_META = 16
PAD_FRONT = 112
ROW0 = PAD_FRONT + N_META
EPS = 1e-6
GATE_CLAMP = 1.0 - 1e-6
CONV_DIM = 512
CONV_K = 31
HEADS = 8
Q_RANK = 256
KV_RANK = 128
NOPE = 64
ROPE = 32
V_DIM = 64
QK_DIM = NOPE + ROPE
HEAD_W = 128
ROPE_BASE = 10000.0
HG_HEADS = 4
HG_DK = 128
HG_DV = 128
HG_CHUNK = 64
D_FF = 4096
N_IN = 6560
C_CONV_A, C_CONV_G, C_GATE, C_CQ, C_CKV, C_KR, C_HQ, C_HF, C_HI, C_HG = (
    0, 512, 1024, 4096, 4352, 4480, 4608, 5120, 5632, 6144)
N_IN_P = 6656
O_CQ, O_KR, O_HQ, O_GATE = 1024, 1408, 1440, 3488
KR_LANE = NOPE

ADAM_LR = 0.001
ADAM_B1 = 0.9
ADAM_B2 = 0.999
ADAM_EPS = 1e-08
ADAM_WD = 0.01
ADAM_STEP = 10

N_DEV = 8
VMEM_LIMIT = 56 * 1024 * 1024
MESH = pl.DeviceIdType.MESH


def _pick(n, cands):
    for c in cands:
        if n % c == 0:
            return c
    raise ValueError(f"no tile for {n}")


def _cparams(sem, **kw):
    return pltpu.CompilerParams(dimension_semantics=sem, vmem_limit_bytes=VMEM_LIMIT, **kw)


def _relu2(v):
    return jnp.square(jnp.maximum(v, 0.0))


def _mm(a, b, *, ta=False, tb=False, out_dtype=F32, res=None, a_fn=None, epi=None, name):
    M, K = (a.shape[1], a.shape[0]) if ta else a.shape
    N = b.shape[0] if tb else b.shape[1]
    assert (b.shape[1] if tb else b.shape[0]) == K, (a.shape, b.shape, ta, tb)
    tm = _pick(M, (1056, 1024, 512, 384, 256, 128, 96))
    tn = _pick(N, (1664, 1024, 512, 384, 256, 128))
    tk = _pick(K, (1664, 1056, 1024, 512, 384, 256, 128, 96) if ta else (1664, 1408, 1024, 512, 384, 256, 128))
    nk = K // tk
    dims = (((0 if ta else 1,), (1 if tb else 0,)), ((), ()))
    extras = ([res] if res is not None else []) + ([epi[0]] if epi is not None else [])

    def body(*refs):
        a_ref, b_ref = refs[0], refs[1]
        r_ref = refs[2] if res is not None else None
        e_ref = refs[2 + (res is not None)] if epi is not None else None
        o_ref = refs[2 + len(extras)]
        acc = refs[-1] if nk > 1 else None
        k = pl.program_id(2)
        av = a_ref[...]
        if a_fn is not None:
            av = a_fn(av.astype(F32))
        p = lax.dot_general(av.astype(BF16), b_ref[...].astype(BF16), dims, preferred_element_type=F32)

        def finish(total):
            if e_ref is not None:
                total = epi[1](total, e_ref[...].astype(F32))
            if r_ref is not None:
                total = total + r_ref[...].astype(F32)
            o_ref[...] = total.astype(o_ref.dtype)

        if nk == 1:
            finish(p)
        else:
            @pl.when(k == 0)
            def _():
                acc[...] = p

            @pl.when(k > 0)
            def _():
                acc[...] += p

            @pl.when(k == nk - 1)
            def _():
                finish(acc[...])

    a_spec = pl.BlockSpec((tk, tm), lambda i, j, k: (k, i)) if ta else pl.BlockSpec((tm, tk), lambda i, j, k: (i, k))
    b_spec = pl.BlockSpec((tn, tk), lambda i, j, k: (j, k)) if tb else pl.BlockSpec((tk, tn), lambda i, j, k: (k, j))
    o_spec = pl.BlockSpec((tm, tn), lambda i, j, k: (i, j))
    in_specs = [a_spec, b_spec] + [o_spec] * len(extras)
    args = (a, b) + tuple(extras)
    return pl.pallas_call(
        body, name=name, grid=(M // tm, N // tn, nk), in_specs=in_specs, out_specs=o_spec,
        out_shape=jax.ShapeDtypeStruct((M, N), out_dtype),
        scratch_shapes=[pltpu.VMEM((tm, tn), F32)] if nk > 1 else [],
        compiler_params=_cparams(("parallel", "parallel", "arbitrary")),
    )(*args)


class Row:
    def __init__(self, arr, width=None, col=0, piece=None):
        self.arr = arr
        self.width = arr.shape[1] if width is None else width
        assert col % self.width == 0
        self.blk = col // self.width
        self.piece = self.width if piece is None else piece

    def spec(self, tm):
        blk = self.blk
        return pl.BlockSpec((tm, self.width), lambda i: (i, blk))


def _split(v, piece):
    w = v.shape[-1]
    if piece == w:
        return v
    return [v[:, j * piece:(j + 1) * piece] for j in range(w // piece)]


def _store(ref, val, dtype=None):
    if isinstance(val, (list, tuple)):
        piece = val[0].shape[-1]
        for j, p in enumerate(val):
            ref[:, j * piece:(j + 1) * piece] = p.astype(ref.dtype)
    else:
        ref[...] = val.astype(ref.dtype)


def _row_tile(T):
    return _pick(T, (384, 352, 192, 128))


def _param2d(p):
    return p.reshape(1, -1).astype(F32)


def _rowwise(fn, T, rows, params, outs, name, transposed=False):
    tm = _row_tile(T)
    nr, npar = len(rows), len(params)
    par = [(_param2d(p), piece) for p, piece in params]

    def body(*refs):
        rid = pl.program_id(0) * tm + lax.broadcasted_iota(jnp.int32, (tm, 1), 0)
        rv = [_split(refs[n][...].astype(F32), rows[n].piece) for n in range(nr)]
        pv = [_split(refs[nr + n][...], par[n][1]) for n in range(npar)]
        res = fn(rid, rv, pv)
        for n, val in enumerate(res):
            _store(refs[nr + npar + n], val)
        if transposed:
            refs[-1][...] = res[0].T.astype(refs[-1].dtype)

    out_specs = [pl.BlockSpec((tm, w), lambda i: (i, 0)) for w, _ in outs]
    out_shape = [jax.ShapeDtypeStruct((T, w), dt) for w, dt in outs]
    if transposed:
        out_specs.append(pl.BlockSpec((outs[0][0], tm), lambda i: (0, i)))
        out_shape.append(jax.ShapeDtypeStruct((outs[0][0], T), outs[0][1]))
    return pl.pallas_call(
        body, name=name, grid=(T // tm,),
        in_specs=[r.spec(tm) for r in rows] + [pl.BlockSpec(p.shape, lambda i: (0, 0)) for p, _ in par],
        out_specs=out_specs, out_shape=out_shape,
        compiler_params=_cparams(("parallel",)),
    )(*[r.arr for r in rows], *[p for p, _ in par])


def _rowwise_bwd(fn, T, rows, params, cts, drow, name, add=None):
    tm = _row_tile(T)
    nr, npar, nct = len(rows), len(params), len(cts)
    par = [(_param2d(p), piece) for p, piece in params]
    didx = sorted(drow)
    has_add = add is not None

    def body(*refs):
        i = pl.program_id(0)
        rid = i * tm + lax.broadcasted_iota(jnp.int32, (tm, 1), 0)
        rv = [_split(refs[n][...].astype(F32), rows[n].piece) for n in range(nr)]
        pv = [_split(refs[nr + n][...], par[n][1]) for n in range(npar)]
        cv = [_split(refs[nr + npar + n][...].astype(F32), cts[n].piece) for n in range(nct)]
        base = nr + npar + nct + (1 if has_add else 0)
        d_refs = refs[base:base + len(didx)]
        p_refs = refs[base + len(didx):]

        def g(dvals, pvals):
            full = list(rv)
            for n, v in zip(didx, dvals):
                full[n] = v
            return fn(rid, full, pvals)

        _, vjp = jax.vjp(g, [rv[n] for n in didx], pv)
        d_rows, d_pars = vjp(cv)
        for slot, n in enumerate(didx):
            val = d_rows[slot]
            if has_add and add[0] == n:
                assert not isinstance(val, (list, tuple))
                val = val + refs[nr + npar + nct][...].astype(F32)
            _store(d_refs[slot], val)

        @pl.when(i == 0)
        def _():
            for r in p_refs:
                r[...] = jnp.zeros_like(r)

        for r, val in zip(p_refs, d_pars):
            if isinstance(val, (list, tuple)):
                piece = val[0].shape[-1]
                for j, p in enumerate(val):
                    r[:, j * piece:(j + 1) * piece] += p
            else:
                r[...] += val

    in_specs = ([r.spec(tm) for r in rows] + [pl.BlockSpec(p.shape, lambda i: (0, 0)) for p, _ in par]
                + [c.spec(tm) for c in cts])
    args = [r.arr for r in rows] + [p for p, _ in par] + [c.arr for c in cts]
    if has_add:
        in_specs.append(pl.BlockSpec((tm, rows[add[0]].width), lambda i: (i, 0)))
        args.append(add[1])
    out_specs = ([pl.BlockSpec((tm, rows[n].width), lambda i: (i, 0)) for n in didx]
                 + [pl.BlockSpec(p.shape, lambda i: (0, 0)) for p, _ in par])
    out_shape = ([jax.ShapeDtypeStruct((T, rows[n].width), drow[n]) for n in didx]
                 + [jax.ShapeDtypeStruct(p.shape, F32) for p, _ in par])
    res = pl.pallas_call(
        body, name=name, grid=(T // tm,), in_specs=in_specs, out_specs=out_specs, out_shape=out_shape,
        compiler_params=_cparams(("arbitrary",)),
    )(*args)
    return list(res[:len(didx)]), list(res[len(didx):])


def _f_rms(rid, rv, pv):
    x, g = rv[0], pv[0]
    return [x * lax.rsqrt(jnp.mean(x * x, axis=-1, keepdims=True) + EPS) * g]


def _f_glu(rid, rv, pv):
    a, gt = rv
    return [a * jax.nn.sigmoid(gt) * (rid >= PAD_FRONT).astype(F32)]


def _f_lnsilu(rid, rv, pv):
    x = rv[0]
    g, b = pv
    mu = jnp.mean(x, axis=-1, keepdims=True)
    xc = x - mu
    y = xc * lax.rsqrt(jnp.mean(xc * xc, axis=-1, keepdims=True) + EPS) * g + b
    return [y * jax.nn.sigmoid(y)]


@functools.partial(jax.custom_vjp, nondiff_argnums=(1,))
def _lane_roll(x, shift):
    return pltpu.roll(x, shift, 1)


def _lane_roll_fwd(x, shift):
    return pltpu.roll(x, shift, 1), None


def _lane_roll_bwd(shift, _, g):
    return (pltpu.roll(g, (HEAD_W - shift) % HEAD_W, 1),)


_lane_roll.defvjp(_lane_roll_fwd, _lane_roll_bwd)


def _head_norm_rope(xh, g, c, s1, s2):
    y = xh * lax.rsqrt(jnp.sum(xh * xh, axis=-1, keepdims=True) * (1.0 / QK_DIM) + EPS) * g
    half = ROPE // 2
    return y * c + _lane_roll(y, HEAD_W - half) * s1 + _lane_roll(y, half) * s2


def _f_qrope(rid, rv, pv):
    q, c, s1, s2 = rv
    return [[_head_norm_rope(qh, pv[0], c, s1, s2) * ATT_SCALE for qh in q]]


def _f_krope(rid, rv, pv):
    k, kr, c, s1, s2 = rv
    return [[_head_norm_rope(kh + kr, pv[0], c, s1, s2) for kh in k]]


def _f_hgrn_prep(rid, rv, pv):
    hf, hi = rv
    m = (rid >= PAD_FRONT).astype(F32)
    kk = (1.0 - pv[0]) * jax.nn.sigmoid(-hf) * m
    lf = jnp.log1p(-jnp.minimum(kk, GATE_CLAMP))
    vv = hi * jax.nn.sigmoid(hi) * m
    return [kk, lf, vv]


def _f_hgrn_out(rid, rv, pv):
    o, hg = rv
    ng = pv[0]
    out = []
    for oh, gh, nh in zip(o, hg, ng):
        y = oh * lax.rsqrt(jnp.mean(oh * oh, axis=-1, keepdims=True) + EPS) * nh
        out.append(y * (gh * jax.nn.sigmoid(gh)))
    return [out]


def _f_mix(rid, rv, pv):
    g0, g1, g2, ya, yb, yc = rv
    return [jax.nn.sigmoid(g0) * ya + jax.nn.sigmoid(g1) * yb + jax.nn.sigmoid(g2) * yc]


def _f_relu2(rid, rv, pv):
    return [jnp.square(jax.nn.relu(rv[0]))]


def _loss_head(x2, tgt, T):
    tm = _row_tile(T)

    def body(x_ref, t_ref, dx_ref, l_ref):
        i = pl.program_id(0)
        rid = i * tm + lax.broadcasted_iota(jnp.int32, (tm, 1), 0)
        diff = (x_ref[...] - t_ref[...]) * (rid >= ROW0).astype(F32)
        dx_ref[...] = diff * (1.0 / D_MODEL)

        @pl.when(i == 0)
        def _():
            l_ref[...] = jnp.zeros_like(l_ref)

        l_ref[...] += jnp.sum(diff * diff, axis=0, keepdims=True)

    spec = pl.BlockSpec((tm, D_MODEL), lambda i: (i, 0))
    return pl.pallas_call(
        body, name="loss_head", grid=(T // tm,), in_specs=[spec, spec],
        out_specs=[spec, pl.BlockSpec((1, D_MODEL), lambda i: (0, 0))],
        out_shape=[jax.ShapeDtypeStruct((T, D_MODEL), F32), jax.ShapeDtypeStruct((1, D_MODEL), F32)],
        compiler_params=_cparams(("arbitrary",)),
    )(x2, tgt)


HALO = 32


CONV_ROWS = 64


def _conv_lanes():
    return [slice(c, c + 128) for c in range(0, CONV_DIM, 128)]


def _conv_tile(T):
    return _pick(T, (384, 128))


def _conv_fwd(h, w, b, T, name):
    tr = _conv_tile(T)
    ratio = tr // HALO
    wp = jnp.zeros((HALO, CONV_DIM), F32).at[:CONV_K].set(w)

    def body(m_ref, h_ref, w_ref, b_ref, o_ref, win):
        i = pl.program_id(0)
        win[0:HALO, :] = h_ref[...] * (i > 0).astype(F32)
        win[HALO:, :] = m_ref[...]
        for cs in _conv_lanes():
            wv, bv = w_ref[:, cs], b_ref[:, cs]
            for r0 in range(0, tr, CONV_ROWS):
                acc = jnp.broadcast_to(bv, (CONV_ROWS, 128))
                for k in range(CONV_K):
                    acc = acc + wv[k:k + 1] * win[pl.ds(HALO - (CONV_K - 1) + k + r0, CONV_ROWS), cs]
                o_ref[r0:r0 + CONV_ROWS, cs] = acc

    return pl.pallas_call(
        body, name=name, grid=(T // tr,),
        in_specs=[pl.BlockSpec((tr, CONV_DIM), lambda i: (i, 0)),
                  pl.BlockSpec((HALO, CONV_DIM), lambda i: (jnp.maximum(i * ratio - 1, 0), 0)),
                  pl.BlockSpec((HALO, CONV_DIM), lambda i: (0, 0)),
                  pl.BlockSpec((1, CONV_DIM), lambda i: (0, 0))],
        out_specs=pl.BlockSpec((tr, CONV_DIM), lambda i: (i, 0)),
        out_shape=jax.ShapeDtypeStruct((T, CONV_DIM), F32),
        scratch_shapes=[pltpu.VMEM((tr + HALO, CONV_DIM), F32)],
        compiler_params=_cparams(("parallel",)),
    )(h, h, wp, _param2d(b))


def _conv_bwd(h, w, dy, T, name):
    tr = _conv_tile(T)
    ratio = tr // HALO
    n_t = T // tr
    last_halo = T // HALO - 1
    wp = jnp.zeros((HALO, CONV_DIM), F32).at[:CONV_K].set(w)

    def body(hm_ref, hh_ref, dm_ref, dh_ref, w_ref, dx_ref, dw_ref, db_ref, hwin, dwin):
        i = pl.program_id(0)
        hwin[0:HALO, :] = hh_ref[...] * (i > 0).astype(F32)
        hwin[HALO:, :] = hm_ref[...]
        dwin[0:tr, :] = dm_ref[...]
        dwin[tr:, :] = dh_ref[...] * (i < n_t - 1).astype(F32)

        @pl.when(i == 0)
        def _():
            dw_ref[...] = jnp.zeros_like(dw_ref)
            db_ref[...] = jnp.zeros_like(db_ref)

        db_ref[...] += jnp.sum(dm_ref[...], axis=0, keepdims=True)
        fold = lambda a: functools.reduce(jnp.add, [a[r:r + 8] for r in range(0, CONV_ROWS, 8)])
        for cs in _conv_lanes():
            wv = w_ref[:, cs]
            dws = [jnp.zeros((8, 128), F32) for _ in range(CONV_K)]
            for r0 in range(0, tr, CONV_ROWS):
                acc = jnp.zeros((CONV_ROWS, 128), F32)
                for k in range(CONV_K):
                    acc = acc + wv[k:k + 1] * dwin[pl.ds(CONV_K - 1 - k + r0, CONV_ROWS), cs]
                dx_ref[r0:r0 + CONV_ROWS, cs] = acc
                dy_t = dm_ref[r0:r0 + CONV_ROWS, cs]
                for k in range(CONV_K):
                    dws[k] = dws[k] + fold(dy_t * hwin[pl.ds(HALO - (CONV_K - 1) + k + r0, CONV_ROWS), cs])
            for k in range(CONV_K):
                dw_ref[k:k + 1, cs] += jnp.sum(dws[k], axis=0, keepdims=True)

    main = pl.BlockSpec((tr, CONV_DIM), lambda i: (i, 0))
    return pl.pallas_call(
        body, name=name, grid=(n_t,),
        in_specs=[main,
                  pl.BlockSpec((HALO, CONV_DIM), lambda i: (jnp.maximum(i * ratio - 1, 0), 0)),
                  main,
                  pl.BlockSpec((HALO, CONV_DIM), lambda i: (jnp.minimum((i + 1) * ratio, last_halo), 0)),
                  pl.BlockSpec((HALO, CONV_DIM), lambda i: (0, 0))],
        out_specs=[main, pl.BlockSpec((HALO, CONV_DIM), lambda i: (0, 0)), pl.BlockSpec((1, CONV_DIM), lambda i: (0, 0))],
        out_shape=[jax.ShapeDtypeStruct((T, CONV_DIM), F32), jax.ShapeDtypeStruct((HALO, CONV_DIM), F32),
                   jax.ShapeDtypeStruct((1, CONV_DIM), F32)],
        scratch_shapes=[pltpu.VMEM((tr + HALO, CONV_DIM), F32), pltpu.VMEM((tr + HALO, CONV_DIM), F32)],
        compiler_params=_cparams(("arbitrary",)),
    )(h, h, dy, dy, wp)


NEG = -1e30
ATT_SCALE = QK_DIM ** -0.5
_NT = (((1,), (1,)), ((), ()))
_TN = (((0,), (0,)), ((), ()))


def _att_blk(T):
    return _pick(T, (384, 128))


def _att_mask(i, j, blk):
    kpos = j * blk + lax.broadcasted_iota(jnp.int32, (blk, blk), 0)
    qpos = i * blk + lax.broadcasted_iota(jnp.int32, (blk, blk), 1)
    return (kpos <= qpos) & (kpos >= PAD_FRONT)


def _t32(a):
    return a.astype(F32).T.astype(BF16)


def _attn_fwd(q, k, v, T, name):
    blk = _att_blk(T)
    nq = T // blk

    def body(q_ref, k_ref, v_ref, o_ref, lse_ref, vt):
        i = pl.program_id(1)

        @pl.when(i == 0)
        def _():
            def tr(j, c):
                vt[j] = _t32(v_ref[pl.ds(pl.multiple_of(j * blk, blk), blk), :])
                return c

            lax.fori_loop(0, nq, tr, 0)

        qb = q_ref[...]

        def step(js, carry, masked):
            m, l, acc = carry
            ss = []
            for j in js:
                kb = k_ref[pl.ds(pl.multiple_of(j * blk, blk), blk), :]
                s = lax.dot_general(kb, qb, _NT, preferred_element_type=F32)
                ss.append(jnp.where(_att_mask(i, j, blk), s, NEG) if masked else s)
            m_new = m
            for s in ss:
                m_new = jnp.maximum(m_new, jnp.max(s, axis=0, keepdims=True))
            alpha = jnp.exp(m - m_new)
            l = alpha * l
            acc = alpha * acc
            for j, s in zip(js, ss):
                p = jnp.exp(s - m_new)
                l = l + jnp.sum(p, axis=0, keepdims=True)
                acc = acc + jnp.dot(vt[j], p.astype(BF16), preferred_element_type=F32)
            return m_new, l, acc

        init = (jnp.full((1, blk), NEG, F32), jnp.zeros((1, blk), F32), jnp.zeros((HEAD_W, blk), F32))
        later = jnp.minimum(i, 1)
        carry = lax.fori_loop(0, 1 - later, lambda t, c: step([i], c, True), init)
        carry = lax.fori_loop(0, later, lambda t, c: step([i, 0], c, True), carry)
        n_free = jnp.maximum(i - 1, 0)
        n4 = n_free // 4
        carry = lax.fori_loop(0, n4, lambda t, c: step([1 + 4 * t + d for d in range(4)], c, False), carry)
        rest = n_free - 4 * n4
        carry = lax.fori_loop(0, rest // 2, lambda t, c: step([i - rest, i - rest + 1], c, False), carry)
        m, l, acc = lax.fori_loop(0, rest % 2, lambda t, c: step([i - 1], c, False), carry)
        o_ref[...] = (acc / l).T.astype(o_ref.dtype)
        lse_ref[0, 0] = m + jnp.log(l)

    full = pl.BlockSpec((T, HEAD_W), lambda h, i: (0, h))
    return pl.pallas_call(
        body, name=name, grid=(HEADS, nq),
        in_specs=[pl.BlockSpec((blk, HEAD_W), lambda h, i: (i, h)), full, full],
        out_specs=[pl.BlockSpec((blk, HEAD_W), lambda h, i: (i, h)),
                   pl.BlockSpec((1, 1, 1, blk), lambda h, i: (h, i, 0, 0))],
        out_shape=[jax.ShapeDtypeStruct((T, HEADS * HEAD_W), BF16), jax.ShapeDtypeStruct((HEADS, nq, 1, blk), F32)],
        scratch_shapes=[pltpu.VMEM((nq, HEAD_W, blk), BF16)],
        compiler_params=_cparams(("parallel", "arbitrary")),
    )(q, k, v)


def _attn_bwd(q, k, v, o, lse, do, T, name):
    blk = _att_blk(T)
    nq = T // blk

    def body(q_ref, k_ref, v_ref, o_ref, lse_ref, do_ref, dq_ref, dk_ref, dv_ref, delta, dqt, dk_acc, dv_acc):
        j = pl.program_id(1)

        @pl.when(j == 0)
        def _():
            dqt[...] = jnp.zeros_like(dqt)

            def dstep(i, c):
                r0 = pl.multiple_of(i * blk, blk)
                prod = do_ref[pl.ds(r0, blk), :].astype(F32) * o_ref[pl.ds(r0, blk), :].astype(F32)
                delta[i] = jnp.sum(prod.T, axis=0, keepdims=True)
                return c

            lax.fori_loop(0, nq, dstep, 0)

        kb = k_ref[...]
        vb = v_ref[...]
        kbt = _t32(kb)
        dk_acc[...] = jnp.zeros_like(dk_acc)
        dv_acc[...] = jnp.zeros_like(dv_acc)

        def step(qs, masked):
            dvs, dks = [], []
            for i in qs:
                r0 = pl.multiple_of(i * blk, blk)
                qb = q_ref[pl.ds(r0, blk), :]
                dob = do_ref[pl.ds(r0, blk), :]
                s = lax.dot_general(kb, qb, _NT, preferred_element_type=F32)
                p = jnp.exp(s - lse_ref[0, i])
                if masked:
                    p = jnp.where(_att_mask(i, j, blk), p, 0.0)
                dvs.append(jnp.dot(p.astype(BF16), dob, preferred_element_type=F32))
                dp = lax.dot_general(vb, dob, _NT, preferred_element_type=F32)
                ds = (p * (dp - delta[i])).astype(BF16)
                dks.append(jnp.dot(ds, qb, preferred_element_type=F32))
                dqt[i] += jnp.dot(kbt, ds, preferred_element_type=F32)
            dv_acc[...] += functools.reduce(jnp.add, dvs)
            dk_acc[...] += functools.reduce(jnp.add, dks)

        def loop(lo, masked):
            n = nq - lo
            n3 = n // 3
            rest = n - 3 * n3

            def triple(t, c):
                step([lo + 3 * t + d for d in range(3)], masked)
                return c

            def pair(t, c):
                step([nq - 2, nq - 1], masked)
                return c

            def last(t, c):
                step([nq - 1], masked)
                return c

            lax.fori_loop(0, n3, triple, 0)
            lax.fori_loop(0, rest // 2, pair, 0)
            lax.fori_loop(0, rest % 2, last, 0)

        @pl.when(j == 0)
        def _():
            loop(0, True)

        @pl.when(j > 0)
        def _():
            step([j], True)
            loop(j + 1, False)

        dk_ref[...] = dk_acc[...].astype(dk_ref.dtype)
        dv_ref[...] = dv_acc[...].astype(dv_ref.dtype)

        @pl.when(j == nq - 1)
        def _():
            def wstep(i, c):
                dq_ref[pl.ds(pl.multiple_of(i * blk, blk), blk), :] = dqt[i].T
                return c

            lax.fori_loop(0, nq, wstep, 0)

    full = pl.BlockSpec((T, HEAD_W), lambda h, j: (0, h))
    kblk = pl.BlockSpec((blk, HEAD_W), lambda h, j: (j, h))
    wide = (T, HEADS * HEAD_W)
    return pl.pallas_call(
        body, name=name, grid=(HEADS, nq),
        in_specs=[full, kblk, kblk, full, pl.BlockSpec((1, nq, 1, blk), lambda h, j: (h, 0, 0, 0)), full],
        out_specs=[full, kblk, kblk],
        out_shape=[jax.ShapeDtypeStruct(wide, F32), jax.ShapeDtypeStruct(wide, BF16), jax.ShapeDtypeStruct(wide, BF16)],
        scratch_shapes=[pltpu.VMEM((nq, 1, blk), F32), pltpu.VMEM((nq, HEAD_W, blk), F32),
                        pltpu.VMEM((blk, HEAD_W), F32), pltpu.VMEM((blk, HEAD_W), F32)],
        compiler_params=_cparams(("parallel", "arbitrary")),
    )(q, k, v, o, lse, do)


HG_NB = 6
C = HG_CHUNK
_HI = lax.Precision.HIGHEST


def _tri(lower):
    r = lax.broadcasted_iota(jnp.int32, (C, C), 0)
    c = lax.broadcasted_iota(jnp.int32, (C, C), 1)
    return ((c <= r) if lower else (c >= r)).astype(F32)


HG_SUB = 8
N_SUB = C // HG_SUB


def _hg_split_decay(b, I, rid):
    lo = I * HG_SUB
    r = b[lo:lo + 1]
    eq = jnp.exp(b[lo:lo + HG_SUB] - r)
    ek = jnp.where(rid < lo, jnp.exp(jnp.minimum(r - b, 0.0)), 0.0)
    return eq, ek


def _hg_intra_fwd(q, k, v, b):
    rid = lax.broadcasted_iota(jnp.int32, (C, 1), 0)
    tid = lax.broadcasted_iota(jnp.int32, (HG_SUB, 1), 0)
    a_rows = [jnp.zeros((HG_SUB, C), F32)]
    blocks = []
    for I in range(N_SUB):
        lo = I * HG_SUB
        q_i, b_i = q[lo:lo + HG_SUB], b[lo:lo + HG_SUB]
        if I > 0:
            eq, ek = _hg_split_decay(b, I, rid)
            a_rows.append(lax.dot_general((q_i * eq).astype(BF16), (k * ek).astype(BF16), _NT,
                                          preferred_element_type=F32))
        o_i = jnp.zeros((HG_SUB, HG_DV), F32)
        for s in range(HG_SUB):
            r = lo + s
            e = jnp.exp(jnp.minimum(b_i - b[r:r + 1], 0.0))
            a = jnp.sum(q_i * k[r:r + 1] * e, axis=-1, keepdims=True)
            o_i = o_i + jnp.where(tid >= s, a, 0.0) * v[r:r + 1]
        blocks.append(o_i)
    a_off = jnp.concatenate(a_rows, axis=0).astype(BF16)
    return jnp.dot(a_off, v.astype(BF16), preferred_element_type=F32) + jnp.concatenate(blocks, axis=0)


def _hg_intra_bwd(q, k, v, b, do, dk_s, dv_s):
    rid = lax.broadcasted_iota(jnp.int32, (C, 1), 0)
    tid = lax.broadcasted_iota(jnp.int32, (HG_SUB, 1), 0)
    da_all = lax.dot_general(do.astype(BF16), v.astype(BF16), _NT, preferred_element_type=F32)
    a_rows = [jnp.zeros((HG_SUB, C), F32)]
    dq_blocks = []
    dk = jnp.zeros((C, HG_DK), F32)
    for I in range(N_SUB):
        lo = I * HG_SUB
        q_i, b_i, do_i = q[lo:lo + HG_SUB], b[lo:lo + HG_SUB], do[lo:lo + HG_SUB]
        dq_i = jnp.zeros((HG_SUB, HG_DK), F32)
        if I > 0:
            eq, ek = _hg_split_decay(b, I, rid)
            qs, ks = (q_i * eq).astype(BF16), (k * ek).astype(BF16)
            a_rows.append(lax.dot_general(qs, ks, _NT, preferred_element_type=F32))
            da = da_all[lo:lo + HG_SUB].astype(BF16)
            dq_i = jnp.dot(da, ks, preferred_element_type=F32) * eq
            dk = dk + lax.dot_general(da, qs, _TN, preferred_element_type=F32) * ek
        for s in range(HG_SUB):
            r = lo + s
            e = jnp.where(tid >= s, jnp.exp(jnp.minimum(b_i - b[r:r + 1], 0.0)), 0.0)
            a = jnp.sum(q_i * k[r:r + 1] * e, axis=-1, keepdims=True)
            g = jnp.sum(do_i * v[r:r + 1], axis=-1, keepdims=True) * e
            dq_i = dq_i + g * k[r:r + 1]
            dk_s[r:r + 1, :] = jnp.sum(g * q_i, axis=0, keepdims=True)
            dv_s[r:r + 1, :] = jnp.sum(a * do_i, axis=0, keepdims=True)
        dq_blocks.append(dq_i)
    a_off = jnp.concatenate(a_rows, axis=0).astype(BF16)
    dv = lax.dot_general(a_off, do.astype(BF16), _TN, preferred_element_type=F32)
    return jnp.concatenate(dq_blocks, axis=0), dk + dk_s[...], dv + dv_s[...]


def _hgrn_fwd(u, kk, lf, vv, T, name):
    nb = _pick(T // C, (HG_NB, 3, 2, 1))
    rows = nb * C
    qblk = C_HQ // HG_DK

    def body(q_ref, k_ref, lf_ref, v_ref, o_ref, st_ref, st):
        @pl.when(pl.program_id(1) == 0)
        def _():
            st[...] = jnp.zeros_like(st)

        lower = _tri(True)
        for n in range(nb):
            sl = slice(n * C, (n + 1) * C)
            q, k, v = q_ref[sl, :].astype(F32), k_ref[sl, :], v_ref[sl, :]
            b = jnp.dot(lower, lf_ref[sl, :], precision=_HI, preferred_element_type=F32)
            s_t = st[...]
            st_ref[0, n] = s_t
            qe = (q * jnp.exp(b)).astype(BF16)
            o = lax.dot_general(qe, s_t.astype(BF16), _NT, preferred_element_type=F32)
            o_ref[sl, :] = o + _hg_intra_fwd(q, k, v, b)
            bl = b[C - 1:C, :]
            kd = (k * jnp.exp(bl - b)).astype(BF16)
            st[...] = s_t * jnp.exp(bl) + lax.dot_general(v.astype(BF16), kd, _TN, preferred_element_type=F32)

    col = lambda off: pl.BlockSpec((rows, HG_DK), lambda h, c: (c, h + off))
    return pl.pallas_call(
        body, name=name, grid=(HG_HEADS, T // rows),
        in_specs=[col(qblk), col(0), col(0), col(0)],
        out_specs=[col(0), pl.BlockSpec((1, nb, HG_DV, HG_DK), lambda h, c: (h, c, 0, 0))],
        out_shape=[jax.ShapeDtypeStruct((T, HG_HEADS * HG_DV), F32),
                   jax.ShapeDtypeStruct((HG_HEADS, T // C, HG_DV, HG_DK), F32)],
        scratch_shapes=[pltpu.VMEM((HG_DV, HG_DK), F32)],
        compiler_params=_cparams(("parallel", "arbitrary")),
    )(u, kk, lf, vv)


def _hgrn_bwd(u, kk, lf, vv, states, do, T, name):
    nb = _pick(T // C, (HG_NB, 3, 2, 1))
    rows = nb * C
    n_steps = T // rows
    qblk = C_HQ // HG_DK

    def body(q_ref, k_ref, lf_ref, v_ref, st_ref, do_ref, dq_ref, dk_ref, dlf_ref, dv_ref, dst, dk_s, dv_s):
        @pl.when(pl.program_id(1) == 0)
        def _():
            dst[...] = jnp.zeros_like(dst)

        lower, upper = _tri(True), _tri(False)
        rid = lax.broadcasted_iota(jnp.int32, (C, 1), 0)
        for n in reversed(range(nb)):
            sl = slice(n * C, (n + 1) * C)
            q, k, v, do = q_ref[sl, :].astype(F32), k_ref[sl, :], v_ref[sl, :], do_ref[sl, :]
            b = jnp.dot(lower, lf_ref[sl, :], precision=_HI, preferred_element_type=F32)
            s_t = st_ref[0, n]
            d_new = dst[...]
            eb = jnp.exp(b)
            bl = b[C - 1:C, :]
            ebl = jnp.exp(bl)
            dec = jnp.exp(bl - b)
            qe = q * eb
            kd = k * dec
            do_b = do.astype(BF16)
            dqe = jnp.dot(do_b, s_t.astype(BF16), preferred_element_type=F32)
            dkd = jnp.dot(v.astype(BF16), d_new.astype(BF16), preferred_element_type=F32)
            dv = lax.dot_general(kd.astype(BF16), d_new.astype(BF16), _NT, preferred_element_type=F32)
            dbl = ebl * jnp.sum(d_new * s_t, axis=0, keepdims=True) + jnp.sum(dkd * kd, axis=0, keepdims=True)
            dst[...] = d_new * ebl + lax.dot_general(do_b, qe.astype(BF16), _TN, preferred_element_type=F32)
            dq_in, dk_in, dv_in = _hg_intra_bwd(q, k, v, b, do, dk_s, dv_s)
            dq = dqe * eb + dq_in
            dk = dkd * dec + dk_in
            dv = dv + dv_in
            db = q * dq - k * dk
            db = db + jnp.where(rid == C - 1, dbl, 0.0)
            dq_ref[sl, :] = dq
            dk_ref[sl, :] = dk
            dv_ref[sl, :] = dv
            dlf_ref[sl, :] = jnp.dot(upper, db, precision=_HI, preferred_element_type=F32)

    rev = lambda off: pl.BlockSpec((rows, HG_DK), lambda h, c: (n_steps - 1 - c, h + off))
    return pl.pallas_call(
        body, name=name, grid=(HG_HEADS, n_steps),
        in_specs=[rev(qblk), rev(0), rev(0), rev(0),
                  pl.BlockSpec((1, nb, HG_DV, HG_DK), lambda h, c: (h, n_steps - 1 - c, 0, 0)), rev(0)],
        out_specs=[rev(0)] * 4,
        out_shape=[jax.ShapeDtypeStruct((T, HG_HEADS * HG_DK), F32)] * 4,
        scratch_shapes=[pltpu.VMEM((HG_DV, HG_DK), F32), pltpu.VMEM((C, HG_DK), F32), pltpu.VMEM((C, HG_DV), F32)],
        compiler_params=_cparams(("parallel", "arbitrary")),
    )(u, kk, lf, vv, states, do)


def _rope_tables(T):
    half = ROPE // 2
    inv_freq = (ROPE_BASE ** (-np.arange(half, dtype=np.float32) / half)).astype(np.float32)
    row = lambda lo, hi, val: np.concatenate([np.zeros(lo, np.float32), np.asarray(val, np.float32) * np.ones(hi - lo, np.float32),
                                              np.zeros(HEAD_W - hi, np.float32)])[None, :]
    freq = row(NOPE, NOPE + half, inv_freq) + row(NOPE + half, NOPE + ROPE, inv_freq)
    pos = lax.broadcasted_iota(jnp.int32, (T, HEAD_W), 0).astype(F32) - float(PAD_FRONT)
    ang = pos * freq
    cos, sin = jnp.cos(ang), jnp.sin(ang)
    c = cos * row(NOPE, NOPE + ROPE, 1.0) + row(0, NOPE, 1.0)
    s1 = sin * row(NOPE, NOPE + half, -1.0)
    s2 = sin * row(NOPE + half, NOPE + ROPE, 1.0)
    return c, s1, s2


def _layer_fwd(x, w, tabs, T, l):
    c, s1, s2 = tabs
    n = lambda s: f"l{l}_{s}"
    sv = {"x": x}
    h, sv["h_t"] = _rowwise(_f_rms, T, [Row(x)], [(w["norm1_g"], D_MODEL)], [(D_MODEL, BF16)], n("norm1"), True)
    u = _mm(h, w["w_in"], out_dtype=BF16, name=n("in_proj"))
    sv.update(h=h, u=u)
    hglu = _rowwise(_f_glu, T, [Row(u, 512, C_CONV_A), Row(u, 512, C_CONV_G)], [], [(CONV_DIM, F32)], n("glu"))[0]
    cv = _conv_fwd(hglu, w["conv_w"], w["conv_b"], T, n("conv"))
    hc = _rowwise(_f_lnsilu, T, [Row(cv)], [(w["conv_ln_g"], CONV_DIM), (w["conv_ln_b"], CONV_DIM)],
                  [(CONV_DIM, BF16)], n("conv_ln"))[0]
    y_a = _mm(hc, w["w_conv_out"], out_dtype=BF16, name=n("conv_out"))
    sv.update(hglu=hglu, cv=cv, hc=hc, y_a=y_a)
    cqn = _rowwise(_f_rms, T, [Row(u, Q_RANK, C_CQ)], [(w["q_a_norm_g"], Q_RANK)], [(Q_RANK, BF16)], n("q_a_norm"))[0]
    ckvn = _rowwise(_f_rms, T, [Row(u, KV_RANK, C_CKV)], [(w["kv_a_norm_g"], KV_RANK)], [(KV_RANK, BF16)], n("kv_a_norm"))[0]
    q_raw = _mm(cqn, w["w_uq"], out_dtype=BF16, name=n("uq"))
    k_raw = _mm(ckvn, w["w_uk"], out_dtype=BF16, name=n("uk"))
    v = _mm(ckvn, w["w_uv"], out_dtype=BF16, name=n("uv"))
    tab_rows = [Row(c), Row(s1), Row(s2)]
    q = _rowwise(_f_qrope, T, [Row(q_raw, piece=HEAD_W)] + tab_rows, [(w["q_norm_g"], HEAD_W)],
                 [(HEADS * HEAD_W, BF16)], n("q_rope"))[0]
    k = _rowwise(_f_krope, T, [Row(k_raw, piece=HEAD_W), Row(u, HEAD_W, C_KR)] + tab_rows, [(w["k_norm_g"], HEAD_W)],
                 [(HEADS * HEAD_W, BF16)], n("k_rope"))[0]
    o, lse = _attn_fwd(q, k, v, T, n("attn"))
    y_b = _mm(o, w["w_attn_out"], out_dtype=BF16, name=n("attn_out"))
    sv.update(cqn=cqn, ckvn=ckvn, q_raw=q_raw, k_raw=k_raw, v=v, q=q, k=k, o=o, lse=lse, y_b=y_b)
    kk, lf, vv = _rowwise(_f_hgrn_prep, T, [Row(u, 512, C_HF), Row(u, 512, C_HI)], [(w["lb"], 512)],
                          [(512, F32)] * 3, n("hgrn_prep"))
    o_h, states = _hgrn_fwd(u, kk, lf, vv, T, n("hgrn"))
    oh = _rowwise(_f_hgrn_out, T, [Row(o_h, piece=HG_DV), Row(u, 512, C_HG, piece=HG_DV)], [(w["hgrn_norm_g"], HG_DV)],
                  [(512, BF16)], n("hgrn_out_norm"))[0]
    y_c = _mm(oh, w["w_hgrn_out"], out_dtype=BF16, name=n("hgrn_out"))
    sv.update(kk=kk, lf=lf, vv=vv, o_h=o_h, states=states, oh=oh, y_c=y_c)
    gate_rows = [Row(u, D_MODEL, C_GATE + g * D_MODEL) for g in range(3)]
    mix = _rowwise(_f_mix, T, gate_rows + [Row(y_a), Row(y_b), Row(y_c)], [], [(D_MODEL, BF16)], n("mix"))[0]
    x1 = _mm(mix, w["w_out"], res=x, name=n("out_proj"))
    h2, sv["h2_t"] = _rowwise(_f_rms, T, [Row(x1)], [(w["norm2_g"], D_MODEL)], [(D_MODEL, BF16)], n("norm2"), True)
    f = _mm(h2, w["w_ff1"], out_dtype=BF16, name=n("ff1"))
    x2 = _mm(f, w["w_ff2"], res=x1, a_fn=_relu2, name=n("ff2"))
    sv.update(mix=mix, x1=x1, h2=h2, f=f)
    return x2, sv


def _layer_bwd(dx2, w, sv, tabs, T, l, mid=None, matrices=None):
    c, s1, s2 = tabs
    n = lambda s: f"l{l}_b_{s}"
    u = sv["u"]
    g = {}
    g["w_ff2"] = _mm(sv["f"], dx2, ta=True, a_fn=_relu2, out_dtype=BF16, name=n("dw_ff2"))
    df = _mm(dx2, w["w_ff2"], tb=True, out_dtype=BF16, name=n("d_f"),
             epi=(sv["f"], lambda d, fv: d * (2.0 * jnp.maximum(fv, 0.0))))
    g["w_ff1"] = _mm(sv["h2_t"], df, out_dtype=BF16, name=n("dw_ff1"))
    dh2 = _mm(df, w["w_ff1"], tb=True, name=n("d_h2"))
    (dx1,), (g["norm2_g"],) = _rowwise_bwd(_f_rms, T, [Row(sv["x1"])], [(w["norm2_g"], D_MODEL)], [Row(dh2)],
                                           {0: F32}, n("norm2"), add=(0, dx2))
    g["w_out"] = _mm(sv["mix"], dx1, ta=True, out_dtype=BF16, name=n("dw_out"))
    w_out = w["w_out"] if mid is None else mid(g, w["w_out"])
    dmix = _mm(dx1, w_out, tb=True, out_dtype=BF16, name=n("d_mix"))
    gate_rows = [Row(u, D_MODEL, C_GATE + i * D_MODEL) for i in range(3)]
    (dg0, dg1, dg2, dy_a, dy_b, dy_c), _ = _rowwise_bwd(
        _f_mix, T, gate_rows + [Row(sv["y_a"]), Row(sv["y_b"]), Row(sv["y_c"])], [], [Row(dmix)],
        {0: BF16, 1: BF16, 2: BF16, 3: BF16, 4: BF16, 5: BF16}, n("mix"))
    g["w_hgrn_out"] = _mm(sv["oh"], dy_c, ta=True, out_dtype=BF16, name=n("dw_hgrn_out"))
    doh = _mm(dy_c, w["w_hgrn_out"], tb=True, out_dtype=BF16, name=n("d_oh"))
    (do_h, dhg), (g["hgrn_norm_g"],) = _rowwise_bwd(
        _f_hgrn_out, T, [Row(sv["o_h"], piece=HG_DV), Row(u, 512, C_HG, piece=HG_DV)], [(w["hgrn_norm_g"], HG_DV)],
        [Row(doh, piece=HG_DV)], {0: F32, 1: BF16}, n("hgrn_out_norm"))
    dhq, dkk, dlf, dvv = _hgrn_bwd(u, sv["kk"], sv["lf"], sv["vv"], sv["states"], do_h, T, n("hgrn"))
    (dhf, dhi), (g["lb"],) = _rowwise_bwd(
        _f_hgrn_prep, T, [Row(u, 512, C_HF), Row(u, 512, C_HI)], [(w["lb"], 512)],
        [Row(dkk), Row(dlf), Row(dvv)], {0: BF16, 1: BF16}, n("hgrn_prep"))
    g["w_attn_out"] = _mm(sv["o"], dy_b, ta=True, out_dtype=BF16, name=n("dw_attn_out"))
    do = _mm(dy_b, w["w_attn_out"], tb=True, out_dtype=BF16, name=n("d_o"))
    dq, dk, dv = _attn_bwd(sv["q"], sv["k"], sv["v"], sv["o"], sv["lse"], do, T, n("attn"))
    tab_rows = [Row(c), Row(s1), Row(s2)]
    (dq_raw,), (g["q_norm_g"],) = _rowwise_bwd(
        _f_qrope, T, [Row(sv["q_raw"], piece=HEAD_W)] + tab_rows, [(w["q_norm_g"], HEAD_W)],
        [Row(dq, piece=HEAD_W)], {0: BF16}, n("q_rope"))
    (dk_raw, dkr), (g["k_norm_g"],) = _rowwise_bwd(
        _f_krope, T, [Row(sv["k_raw"], piece=HEAD_W), Row(u, HEAD_W, C_KR)] + tab_rows, [(w["k_norm_g"], HEAD_W)],
        [Row(dk, piece=HEAD_W)], {0: BF16, 1: BF16}, n("k_rope"))
    g["w_uq"] = _mm(sv["cqn"], dq_raw, ta=True, out_dtype=BF16, name=n("dw_uq"))
    g["w_uk"] = _mm(sv["ckvn"], dk_raw, ta=True, out_dtype=BF16, name=n("dw_uk"))
    g["w_uv"] = _mm(sv["ckvn"], dv, ta=True, out_dtype=BF16, name=n("dw_uv"))
    dcqn = _mm(dq_raw, w["w_uq"], tb=True, out_dtype=BF16, name=n("d_cqn"))
    dckvn = _mm(dk_raw, w["w_uk"], tb=True, name=n("d_ckvn_k"))
    dckvn = _mm(dv, w["w_uv"], tb=True, res=dckvn, out_dtype=BF16, name=n("d_ckvn_v"))
    (dcq,), (g["q_a_norm_g"],) = _rowwise_bwd(_f_rms, T, [Row(u, Q_RANK, C_CQ)], [(w["q_a_norm_g"], Q_RANK)],
                                              [Row(dcqn)], {0: BF16}, n("q_a_norm"))
    (dckv,), (g["kv_a_norm_g"],) = _rowwise_bwd(_f_rms, T, [Row(u, KV_RANK, C_CKV)], [(w["kv_a_norm_g"], KV_RANK)],
                                                [Row(dckvn)], {0: BF16}, n("kv_a_norm"))
    g["w_conv_out"] = _mm(sv["hc"], dy_a, ta=True, out_dtype=BF16, name=n("dw_conv_out"))
    dhc = _mm(dy_a, w["w_conv_out"], tb=True, out_dtype=BF16, name=n("d_hc"))
    (dcv,), (g["conv_ln_g"], g["conv_ln_b"]) = _rowwise_bwd(
        _f_lnsilu, T, [Row(sv["cv"])], [(w["conv_ln_g"], CONV_DIM), (w["conv_ln_b"], CONV_DIM)], [Row(dhc)],
        {0: F32}, n("conv_ln"))
    dhglu, dconv_w, g["conv_b"] = _conv_bwd(sv["hglu"], w["conv_w"], dcv, T, n("conv"))
    g["conv_w"] = dconv_w[:CONV_K]
    (dua, dug), _ = _rowwise_bwd(_f_glu, T, [Row(u, 512, C_CONV_A), Row(u, 512, C_CONV_G)], [], [Row(dhglu)],
                                 {0: BF16, 1: BF16}, n("glu"))
    du = jnp.concatenate([dua, dug, dg0, dg1, dg2, dcq, dckv, dkr, dhq.astype(BF16), dhf, dhi, dhg], axis=1)
    small = ("w_uq", "w_uk", "w_uv", "w_attn_out", "w_hgrn_out", "w_conv_out")
    du, *done = lax.optimization_barrier((du, *[g[k] for k in small]))
    g.update(zip(small, done))
    g["w_in"] = _mm(sv["h_t"], du, out_dtype=BF16, name=n("dw_in"))
    norm_g = w["norm1_g"] if matrices is None else matrices(g, w["norm1_g"])
    du, norm_g = lax.optimization_barrier((du, norm_g))
    dh = _mm(du, w["w_in"], tb=True, name=n("d_h"))
    (dx,), (g["norm1_g"],) = _rowwise_bwd(_f_rms, T, [Row(sv["x"])], [(norm_g, D_MODEL)], [Row(dh)],
                                          {0: F32}, n("norm1"), add=(0, dx1))
    return dx, g


def _pad_w_in(w_in):
    z = lambda k: jnp.zeros((w_in.shape[0], k), w_in.dtype)
    return jnp.concatenate([w_in[:, :O_CQ], w_in[:, O_GATE:], w_in[:, O_CQ:O_KR], z(KR_LANE), w_in[:, O_KR:O_HQ],
                            z(HEAD_W - KR_LANE - ROPE), w_in[:, O_HQ:O_GATE]], axis=1)


def _unpad_w_in(g):
    return jnp.concatenate([g[:, :C_GATE], g[:, C_CQ:C_KR], g[:, C_KR + KR_LANE:C_KR + KR_LANE + ROPE],
                            g[:, C_HQ:], g[:, C_GATE:C_CQ]], axis=1)


_W_IN_RUNS = ((0, 0, O_CQ), (O_CQ, C_CQ, O_KR - O_CQ), (O_KR, C_KR + KR_LANE, ROPE), (O_HQ, C_HQ, O_GATE - O_HQ),
              (O_GATE, C_GATE, N_IN - O_GATE))


def _w_in_from_shards(g8):
    per = N_IN // N_DEV
    pieces, at = [], 0
    for o0, p0, n in sorted(_W_IN_RUNS, key=lambda r: r[1]):
        if p0 > at:
            pieces.append(jnp.zeros((g8.shape[1], p0 - at), g8.dtype))
        for j in range(o0 // per, (o0 + n - 1) // per + 1):
            lo, hi = max(o0, j * per), min(o0 + n, (j + 1) * per)
            pieces.append(g8[j][:, lo - j * per:hi - j * per])
        at = p0 + n
    if at < N_IN_P:
        pieces.append(jnp.zeros((g8.shape[1], N_IN_P - at), g8.dtype))
    return jnp.concatenate(pieces, axis=1)


def _w_in_grad_shards(g):
    per = N_IN // N_DEV
    shards = []
    for j in range(N_DEV):
        lo, hi = j * per, (j + 1) * per
        pieces = [g[:, p0 + max(lo, o0) - o0:p0 + min(hi, o0 + n) - o0]
                  for o0, p0, n in _W_IN_RUNS if max(lo, o0) < min(hi, o0 + n)]
        shards.append(jnp.concatenate(pieces, axis=1) if len(pieces) > 1 else pieces[0])
    return jnp.stack(shards)


def _pad_heads(wm, per_head, lo, hi):
    lead = wm.shape[:-1]
    wh = wm.reshape(lead + (HEADS, per_head))[..., lo:hi]
    pad = [(0, 0)] * len(lead) + [(0, 0), (0, HEAD_W - (hi - lo))]
    return jnp.pad(wh, pad).reshape(lead + (HEADS * HEAD_W,))


def _unpad_heads(gm, width):
    lead = gm.shape[:-1]
    return gm.reshape(lead + (HEADS, HEAD_W))[..., :width]


def _layer_weights(full, lb):
    w = {}
    w["norm1_g"] = full["norm1_g"]
    w["w_in"] = full["w_in_padded"] if "w_in_padded" in full else _pad_w_in(full["w_in"])
    w["conv_w"] = full["conv_w"]
    w["conv_b"] = full["conv_b"]
    w["conv_ln_g"] = full["conv_ln_g"]
    w["conv_ln_b"] = full["conv_ln_b"]
    w["w_conv_out"] = full["w_conv_out"]
    w["q_a_norm_g"] = full["q_a_norm_g"]
    w["w_uq"] = _pad_heads(full["w_uq"], QK_DIM, 0, QK_DIM)
    w["kv_a_norm_g"] = full["kv_a_norm_g"]
    w["w_uk"] = _pad_heads(full["w_ukv"], NOPE + V_DIM, 0, NOPE)
    w["w_uv"] = _pad_heads(full["w_ukv"], NOPE + V_DIM, NOPE, NOPE + V_DIM)
    w["q_norm_g"] = jnp.pad(full["q_norm_g"], (0, HEAD_W - QK_DIM))
    w["k_norm_g"] = jnp.pad(full["k_norm_g"], (0, HEAD_W - QK_DIM))
    wa = full["w_attn_out"].reshape(HEADS, V_DIM, D_MODEL)
    w["w_attn_out"] = jnp.pad(wa, ((0, 0), (0, HEAD_W - V_DIM), (0, 0))).reshape(HEADS * HEAD_W, D_MODEL)
    w["lb"] = lb
    w["hgrn_norm_g"] = full["hgrn_norm_g"]
    w["w_hgrn_out"] = full["w_hgrn_out"]
    w["w_out"] = full["w_out"]
    w["norm2_g"] = full["norm2_g"]
    w["w_ff1"] = full["w_ff1"]
    w["w_ff2"] = full["w_ff2"]
    return w


def _matrix_grads_to_original(g):
    o = {name: g[name] for name in ("w_conv_out", "w_hgrn_out", "w_out", "w_ff1", "w_ff2")}
    o["w_in"] = _unpad_w_in(g["w_in"])
    o["w_in_shards"] = _w_in_grad_shards(g["w_in"])
    o["w_uq"] = _unpad_heads(g["w_uq"], QK_DIM).reshape(Q_RANK, HEADS * QK_DIM)
    guk = _unpad_heads(g["w_uk"], NOPE)
    guv = _unpad_heads(g["w_uv"], V_DIM)
    o["w_ukv"] = jnp.concatenate([guk, guv], axis=-1).reshape(KV_RANK, HEADS * (NOPE + V_DIM))
    o["w_attn_out"] = g["w_attn_out"].reshape(HEADS, HEAD_W, D_MODEL)[:, :V_DIM].reshape(HEADS * V_DIM, D_MODEL)
    return o


def _vector_grads_to_original(g):
    o = {"conv_w": g["conv_w"]}
    for name in ("norm1_g", "conv_b", "conv_ln_g", "conv_ln_b", "q_a_norm_g", "kv_a_norm_g", "hgrn_norm_g", "norm2_g", "lb"):
        o[name] = g[name].reshape(-1)
    o["q_norm_g"] = g["q_norm_g"].reshape(-1)[:QK_DIM]
    o["k_norm_g"] = g["k_norm_g"].reshape(-1)[:QK_DIM]
    return o


def _lower_bounds(logits):
    p = jax.nn.softmax(logits.astype(F32), axis=0)
    return jnp.cumsum(p, axis=0) - p[0:1]


def _run_step(x, target, meta, lb_logits, layer_weights, layer_done, layer_mid=None, layer_matrices=None):
    seq = x.shape[0]
    T = ROW0 + seq
    assert T % 128 == 0
    tabs = _rope_tables(T)
    lbs, lb_vjp = jax.vjp(_lower_bounds, lb_logits)
    xp = jnp.concatenate([jnp.zeros((PAD_FRONT, D_MODEL), F32), meta.astype(F32), x], axis=0)
    tp = jnp.concatenate([jnp.zeros((ROW0, D_MODEL), F32), target], axis=0)
    ws, svs = [], []
    for l in range(DEPTH):
        full, xp = layer_weights(l, xp)
        w = _layer_weights(full, lbs[l])
        xp, sv = _layer_fwd(xp, w, tabs, T, l)
        ws.append(w)
        svs.append(sv)
    dx, sq = _loss_head(xp, tp, T)
    loss = 0.5 * jnp.sum(sq) * (1.0 / D_MODEL)
    dlb = [None] * DEPTH
    for l in reversed(range(DEPTH)):
        mid = None if layer_mid is None else functools.partial(layer_mid, l)
        mats = {}

        def matrices(g, norm_g, l=l, mats=mats):
            mats.update(_matrix_grads_to_original(g))
            return norm_g if layer_matrices is None else layer_matrices(l, mats, norm_g)

        dx, g = _layer_bwd(dx, ws[l], svs[l], tabs, T, l, mid, matrices)
        g = {**_vector_grads_to_original(g), **mats}
        dlb[l] = g.pop("lb")
        dx = layer_done(l, g, dx)
    return loss, dx[ROW0:], dx[PAD_FRONT:ROW0], lb_vjp(jnp.stack(dlb))[0]


def _local_step(x, target, full):
    per_layer = [None] * DEPTH

    def done(l, g, dx):
        per_layer[l] = g
        return dx

    loss, gx, gmeta, glb = _run_step(
        x, target, full["meta"], full["hgrn_lb_logits"],
        lambda l, xp: ({k: v[l] for k, v in full.items() if k != "meta"}, xp), done)
    grads = {k: jnp.stack([per_layer[l][k] for l in range(DEPTH)]) for k in per_layer[0]}
    grads["hgrn_lb_logits"] = glb
    grads["meta"] = gmeta
    return loss, gx, grads


def _mesh_pos():
    return lax.axis_index("x"), lax.axis_index("y"), lax.axis_index("c")


N_COPY = N_DEV - 1


def _all_gather(arrs, name):
    n = len(arrs)

    def body(*refs):
        x_refs, out_refs = refs[:n], refs[n:2 * n]
        send_sems, recv_sems, local_sems = refs[2 * n:]
        x, y, c = _mesh_pos()
        me, sibling = (x, y, c), (x, y, 1 - c)
        chips = [(1 - x, y), (x, 1 - y), (1 - x, 1 - y)]

        def slot(a, px, py, pc):
            return out_refs[a].at[4 * px + 2 * py + pc]

        def copy(a, k, block, to, own=False):
            return pltpu.make_async_remote_copy(
                src_ref=x_refs[a] if own else slot(a, *block), dst_ref=slot(a, *block),
                send_sem=send_sems.at[a * N_COPY + k], recv_sem=recv_sems.at[a * N_COPY + k],
                device_id=to, device_id_type=MESH)

        mine = [pltpu.make_async_copy(x_refs[a], slot(a, *me), local_sems.at[a]) for a in range(n)]
        for cp in mine:
            cp.start()
        first = []
        for a in range(n):
            first.append(copy(a, 0, me, sibling, own=True))
            first += [copy(a, 1 + j, me, (*chip, c), own=True) for j, chip in enumerate(chips)]
        for cp in first:
            cp.start()
        passed = []
        for j, chip in enumerate(chips):
            for a in range(n):
                copy(a, 1 + j, (*chip, c), me).wait_recv()
                cp = copy(a, 4 + j, (*chip, c), sibling)
                cp.start()
                passed.append(cp)
        for a in range(n):
            copy(a, 0, sibling, me).wait_recv()
            for j, chip in enumerate(chips):
                copy(a, 4 + j, (*chip, 1 - c), me).wait_recv()
        for cp in first + passed:
            cp.wait_send()
        for cp in mine:
            cp.wait()

    anyspec = pl.BlockSpec(memory_space=pl.ANY)
    return pl.pallas_call(
        body, name=name, out_shape=[jax.ShapeDtypeStruct((N_DEV,) + a.shape, a.dtype) for a in arrs],
        in_specs=[anyspec] * n, out_specs=[anyspec] * n,
        scratch_shapes=[pltpu.SemaphoreType.DMA((n * N_COPY,)), pltpu.SemaphoreType.DMA((n * N_COPY,)),
                        pltpu.SemaphoreType.DMA((n,))],
    )(*arrs)


def _exchange(arrs, name):
    n = len(arrs)

    def body(*refs):
        s_refs, r_refs = refs[:n], refs[n:2 * n]
        send_sems, recv_sems, local_sems = refs[2 * n:]
        x, y, c = _mesh_pos()
        me = 4 * x + 2 * y + c
        local = [pltpu.make_async_copy(s_refs[a].at[me], r_refs[a].at[me], local_sems.at[a]) for a in range(n)]
        for cp in local:
            cp.start()
        sends, recvs = [], []
        for rel in range(1, N_DEV):
            px = 1 - x if rel & 4 else x
            py = 1 - y if rel & 2 else y
            pc = 1 - c if rel & 1 else c
            p = 4 * px + 2 * py + pc
            for a in range(n):
                k = a * N_COPY + rel - 1
                sends.append(pltpu.make_async_remote_copy(
                    src_ref=s_refs[a].at[p], dst_ref=r_refs[a].at[me], send_sem=send_sems.at[k],
                    recv_sem=recv_sems.at[k], device_id=(px, py, pc), device_id_type=MESH))
                recvs.append(pltpu.make_async_remote_copy(
                    src_ref=s_refs[a].at[me], dst_ref=r_refs[a].at[p], send_sem=send_sems.at[k],
                    recv_sem=recv_sems.at[k], device_id=(px, py, pc), device_id_type=MESH))
        for cp in sends:
            cp.start()
        for cp in recvs:
            cp.wait_recv()
        for cp in sends:
            cp.wait_send()
        for cp in local:
            cp.wait()

    anyspec = pl.BlockSpec(memory_space=pl.ANY)
    return pl.pallas_call(
        body, name=name, out_shape=[jax.ShapeDtypeStruct(a.shape, a.dtype) for a in arrs],
        in_specs=[anyspec] * n, out_specs=[anyspec] * n,
        scratch_shapes=[pltpu.SemaphoreType.DMA((n * N_COPY,)), pltpu.SemaphoreType.DMA((n * N_COPY,)),
                        pltpu.SemaphoreType.DMA((n,))],
    )(*arrs)


_HBM = pl.BlockSpec(memory_space=pltpu.HBM)
_SEM = pl.BlockSpec(memory_space=pltpu.SEMAPHORE)
_EFFECT = pltpu.SideEffectType.DATAFLOW_SIDE_EFFECTING


def _peers(x, y, c):
    out = []
    for rel in range(1, N_DEV):
        px = 1 - x if rel & 4 else x
        py = 1 - y if rel & 2 else y
        pc = 1 - c if rel & 1 else c
        out.append((rel, (px, py, pc), 4 * px + 2 * py + pc))
    return out


def _split_copies(src_refs, land_refs, send_sems, recv_sems, gather):
    x, y, c = _mesh_pos()
    me = 4 * x + 2 * y + c
    out = []
    for a, (src, land) in enumerate(zip(src_refs, land_refs)):
        for rel, peer, p in _peers(x, y, c):
            k = a * N_COPY + rel - 1
            mk = lambda s, d: pltpu.make_async_remote_copy(
                src_ref=s, dst_ref=d, send_sem=send_sems.at[k], recv_sem=recv_sems.at[k],
                device_id=peer, device_id_type=MESH)
            mine = src if gather else src.at[p]
            out.append((mk(mine, land.at[me]), mk(mine, land.at[p])))
    return out


def _copy_start(srcs, gather, name, collective_id):
    n = len(srcs)
    lands = [lax.empty(((N_DEV,) + s.shape) if gather else s.shape, s.dtype) for s in srcs]

    def body(*refs):
        src_refs, land_refs = refs[:n], refs[n:2 * n]
        send_sems, recv_sems = refs[2 * n], refs[2 * n + 1]
        token = refs[-1]
        x, y, c = _mesh_pos()
        barrier = pltpu.get_barrier_semaphore()
        for _, peer, _ in _peers(x, y, c):
            pl.semaphore_signal(barrier, inc=1, device_id=peer, device_id_type=MESH)
        pl.semaphore_wait(barrier, N_COPY)
        for out_copy, _ in _split_copies(src_refs, land_refs, send_sems, recv_sems, gather):
            out_copy.start()
        token[...] = jnp.zeros_like(token)

    hbm = lambda a: pltpu.HBM(a.shape, a.dtype)
    res = pl.pallas_call(
        body, name=name,
        out_shape=(pltpu.SemaphoreType.DMA((n * N_COPY,)), pltpu.SemaphoreType.DMA((n * N_COPY,)),
                   *[hbm(s) for s in srcs], *[hbm(z) for z in lands], jax.ShapeDtypeStruct((8, 128), F32)),
        in_specs=[_HBM] * (2 * n), out_specs=(_SEM, _SEM, *([_HBM] * (2 * n)), pl.BlockSpec(memory_space=pltpu.VMEM)),
        input_output_aliases={i: 2 + i for i in range(2 * n)},
        compiler_params=pltpu.CompilerParams(has_side_effects=_EFFECT, collective_id=collective_id),
    )(*[pltpu.with_memory_space_constraint(s, pltpu.HBM) for s in srcs],
      *[pltpu.with_memory_space_constraint(z, pltpu.HBM) for z in lands])
    return res[0], res[1], list(res[2:2 + n]), list(res[2 + n:2 + 2 * n]), res[-1]


def _after(a, token):
    return lax.optimization_barrier((a, token))[0]


def _copy_wait(send_sems, recv_sems, srcs, lands, after, gather, name):
    n = len(srcs)

    def body(*refs):
        src_refs, land_refs = refs[:n], refs[n:2 * n]
        s_sems, r_sems = refs[2 * n], refs[2 * n + 1]
        for out_copy, in_copy in _split_copies(src_refs, land_refs, s_sems, r_sems, gather):
            out_copy.wait_send()
            in_copy.wait_recv()

    hbm = lambda a: pltpu.HBM(a.shape, a.dtype)
    res = pl.pallas_call(
        body, name=name, out_shape=(*[hbm(s) for s in srcs], *[hbm(z) for z in lands]),
        in_specs=[_HBM] * (2 * n) + [_SEM, _SEM, pl.BlockSpec(memory_space=pl.ANY)], out_specs=tuple([_HBM] * (2 * n)),
        input_output_aliases={i: i for i in range(2 * n)},
        compiler_params=pltpu.CompilerParams(has_side_effects=_EFFECT),
    )(*srcs, *lands, send_sems, recv_sems, after)
    return list(res[:n]), list(res[n:])


def _sum_parts(parts, name):
    P, R, W = parts.shape

    def body(p_ref, o_ref):
        g = p_ref[0].astype(F32)
        for i in range(1, P):
            g = g + p_ref[i].astype(F32)
        o_ref[...] = g

    return pl.pallas_call(body, name=name, out_shape=jax.ShapeDtypeStruct((R, W), F32))(parts)


def _adamw_body(p_ref, w_ref, m_ref, v_ref, g_ref, d_ref, nm_ref, nv_ref):
    g = p_ref[0].astype(F32)
    for i in range(1, p_ref.shape[0]):
        g = g + p_ref[i].astype(F32)
    _adamw_apply(g, w_ref, m_ref, v_ref, g_ref, d_ref, nm_ref, nv_ref)


def _adamw_apply(g, w_ref, m_ref, v_ref, g_ref, d_ref, nm_ref, nv_ref):
    m_new = ADAM_B1 * m_ref[...] + (1.0 - ADAM_B1) * g
    v_new = ADAM_B2 * v_ref[...] + (1.0 - ADAM_B2) * jnp.square(g)
    m_hat = m_new / (1.0 - ADAM_B1 ** ADAM_STEP)
    v_hat = v_new / (1.0 - ADAM_B2 ** ADAM_STEP)
    g_ref[...] = g
    d_ref[...] = -ADAM_LR * (m_hat / (jnp.sqrt(v_hat) + ADAM_EPS) + ADAM_WD * w_ref[...])
    nm_ref[...] = m_new
    nv_ref[...] = v_new


def _adamw(parts, w, m, v, name):
    P, R, W = parts.shape
    tr = _pick(R, (368, 192, 64, 16, 8))
    spec = pl.BlockSpec((tr, W), lambda i: (i, 0))
    return pl.pallas_call(
        functools.partial(_adamw_body), name=name, grid=(R // tr,),
        in_specs=[pl.BlockSpec((P, tr, W), lambda i: (0, i, 0)), spec, spec, spec], out_specs=[spec] * 4,
        out_shape=[jax.ShapeDtypeStruct((R, W), F32)] * 4,
        compiler_params=_cparams(("parallel",)),
    )(parts, w, m, v)


def _adamw_layers(parts, w, m, v, name):
    P, B, C_ = parts[0].shape
    tb = _pick(B, (256, 128))
    nb = B // tb

    def body(*refs):
        p_refs, rest = refs[:DEPTH], refs[DEPTH:]
        a = pl.program_id(0)
        for l in range(DEPTH):
            @pl.when(a == l)
            def _():
                _adamw_body(p_refs[l], *[r.at[0] for r in rest])

    spec = pl.BlockSpec((1, tb, C_), lambda a, i: (a, i, 0))

    def part_spec(l):
        return pl.BlockSpec((P, tb, C_), lambda a, i: (0, jnp.where(a == l, i, jnp.where(a < l, 0, nb - 1)), 0))

    return pl.pallas_call(
        body, name=name, grid=(DEPTH, nb),
        in_specs=[part_spec(l) for l in range(DEPTH)] + [spec, spec, spec], out_specs=[spec] * 4,
        out_shape=[jax.ShapeDtypeStruct((DEPTH, B, C_), F32)] * 4,
        compiler_params=_cparams(("arbitrary", "arbitrary")),
    )(*parts, w, m, v)


VEC_GROUPS = (("norm1_g", "norm2_g"), ("conv_b", "conv_ln_g", "conv_ln_b", "hgrn_lb_logits", "hgrn_norm_g"),
              ("q_a_norm_g",), ("kv_a_norm_g",), ("q_norm_g", "k_norm_g"))
SMALL_NAMES = tuple(n for grp in VEC_GROUPS for n in grp) + ("meta", "conv_w")


def _adamw_small(own, lands, wts, mom, var, name):
    n_in = len(own)

    def body(*refs):
        own_r, land_r = refs[:n_in], refs[n_in:2 * n_in]
        rest = iter(refs[2 * n_in:])
        wmv = {n: (next(rest), next(rest), next(rest)) for n in SMALL_NAMES}
        outs = {n: (next(rest), next(rest), next(rest), next(rest)) for n in SMALL_NAMES}
        loss_ref = next(rest)
        x, y, c = _mesh_pos()
        me = 4 * x + 2 * y + c

        def total(k):
            acc = None
            for s in range(N_DEV):
                v = jnp.where(me == s, own_r[k][...], land_r[k][s])
                acc = v if acc is None else acc + v
            return acc

        for k, grp in enumerate(VEC_GROUPS):
            tot = total(k)
            for j, n in enumerate(grp):
                _adamw_apply(tot[DEPTH * j:DEPTH * (j + 1)], *wmv[n], *outs[n])
        loss_ref[...] = total(len(VEC_GROUPS))
        _adamw_apply(total(n_in - 2), *wmv["meta"], *outs["meta"])
        _adamw_apply(total(n_in - 1), *wmv["conv_w"], *outs["conv_w"])

    args = list(own) + list(lands) + [d[n] for n in SMALL_NAMES for d in (wts, mom, var)]
    out_shape = [jax.ShapeDtypeStruct(wts[n].shape, F32) for n in SMALL_NAMES for _ in range(4)]
    res = pl.pallas_call(body, name=name, out_shape=out_shape + [jax.ShapeDtypeStruct((1, 128), F32)])(*args)
    out = {}
    for i, n in enumerate(SMALL_NAMES):
        for j, kind in enumerate(("grad_", "delta_", "new_m_", "new_v_")):
            out[kind + n] = res[4 * i + j]
    return out, res[-1]


PACK_W = 1024
BIG = (("w_in", (DEPTH, D_MODEL, N_IN // N_DEV), 2), ("w_conv_out", (DEPTH, CONV_DIM, D_MODEL // N_DEV), 2),
       ("w_uq", (DEPTH, Q_RANK, HEADS * QK_DIM // N_DEV), 2), ("w_ukv", (DEPTH, KV_RANK, HEADS * (NOPE + V_DIM) // N_DEV), 2),
       ("w_attn_out", (DEPTH, HEADS * V_DIM, D_MODEL // N_DEV), 2), ("w_hgrn_out", (DEPTH, 512, D_MODEL // N_DEV), 2),
       ("w_out", (DEPTH, D_MODEL // N_DEV, D_MODEL), 1), ("w_ff1", (DEPTH, D_MODEL, D_FF // N_DEV), 2),
       ("w_ff2", (DEPTH, D_FF // N_DEV, D_MODEL), 1))
SMALL_SHARDED = (("meta", (N_META, D_MODEL // N_DEV), 1), ("conv_w", (DEPTH, CONV_K, CONV_DIM // N_DEV), 2))
REPLICATED = (("norm1_g", (DEPTH, D_MODEL)), ("conv_b", (DEPTH, CONV_DIM)), ("conv_ln_g", (DEPTH, CONV_DIM)),
              ("conv_ln_b", (DEPTH, CONV_DIM)), ("q_a_norm_g", (DEPTH, Q_RANK)), ("kv_a_norm_g", (DEPTH, KV_RANK)),
              ("q_norm_g", (DEPTH, QK_DIM)), ("k_norm_g", (DEPTH, QK_DIM)), ("hgrn_lb_logits", (DEPTH, 512)),
              ("hgrn_norm_g", (DEPTH, 512)), ("norm2_g", (DEPTH, D_MODEL)))
WEIGHT_ORDER = ("meta", "norm1_g", "w_in", "conv_w", "conv_b", "conv_ln_g", "conv_ln_b", "w_conv_out", "q_a_norm_g", "w_uq",
                "kv_a_norm_g", "w_ukv", "q_norm_g", "k_norm_g", "w_attn_out", "hgrn_lb_logits", "hgrn_norm_g", "w_hgrn_out",
                "w_out", "norm2_g", "w_ff1", "w_ff2")


def _rows_for(n_elems, mult):
    rows = -(-n_elems // PACK_W)
    return -(-rows // mult) * mult


def _pack(arrays, dtype, mult, lead=()):
    nl = len(lead)
    flat = jnp.concatenate([a.reshape(lead + (-1,)).astype(dtype) for a in arrays], axis=nl)
    rows = _rows_for(flat.shape[nl], mult)
    flat = jnp.pad(flat, [(0, 0)] * nl + [(0, rows * PACK_W - flat.shape[nl])])
    return flat.reshape(lead + (rows, PACK_W))


def _unpack(pack, shapes, lead=()):
    nl = len(lead)
    flat = pack.reshape(lead + (-1,))
    out, off = [], 0
    for shp in shapes:
        n = int(np.prod(shp))
        out.append(lax.slice_in_dim(flat, off, off + n, axis=nl).reshape(lead + tuple(shp)))
        off += n
    return out


def _join_shards(g, axis):
    g = jnp.moveaxis(g, 0, axis)
    shp = g.shape
    return g.reshape(shp[:axis] + (shp[axis] * shp[axis + 1],) + shp[axis + 2:])


def _cut_shards(a, axis):
    shp = a.shape
    a = a.reshape(shp[:axis] + (N_DEV, shp[axis] // N_DEV) + shp[axis + 1:])
    return jnp.moveaxis(a, axis, 0)


def kernel(x, meta, norm1_g, w_in, conv_w, conv_b, conv_ln_g, conv_ln_b, w_conv_out, q_a_norm_g, w_uq, kv_a_norm_g, w_ukv, q_norm_g, k_norm_g, w_attn_out, hgrn_lb_logits, hgrn_norm_g, w_hgrn_out, w_out, norm2_g, w_ff1, w_ff2, loss_target, m_meta, m_norm1_g, m_w_in, m_conv_w, m_conv_b, m_conv_ln_g, m_conv_ln_b, m_w_conv_out, m_q_a_norm_g, m_w_uq, m_kv_a_norm_g, m_w_ukv, m_q_norm_g, m_k_norm_g, m_w_attn_out, m_hgrn_lb_logits, m_hgrn_norm_g, m_w_hgrn_out, m_w_out, m_norm2_g, m_w_ff1, m_w_ff2, v_meta, v_norm1_g, v_w_in, v_conv_w, v_conv_b, v_conv_ln_g, v_conv_ln_b, v_w_conv_out, v_q_a_norm_g, v_w_uq, v_kv_a_norm_g, v_w_ukv, v_q_norm_g, v_k_norm_g, v_w_attn_out, v_hgrn_lb_logits, v_hgrn_norm_g, v_w_hgrn_out, v_w_out, v_norm2_g, v_w_ff1, v_w_ff2):
    args = dict(locals())
    wts = {n: args[n] for n in WEIGHT_ORDER}
    mom = {n: args["m_" + n] for n in WEIGHT_ORDER}
    var = {n: args["v_" + n] for n in WEIGHT_ORDER}
    xi, yi, ci = _mesh_pos()
    me = 4 * xi + 2 * yi + ci

    shard = lambda l: [wts[n][l].astype(BF16) for n, _, _ in BIG]
    gathered = _all_gather(shard(0) + [_pack([wts[n] for n, _, _ in SMALL_SHARDED], F32, 8)], "gather_layer0")
    small = dict(zip([n for n, _, _ in SMALL_SHARDED],
                     [_join_shards(g, axis) for (_, _, axis), g in
                      zip(SMALL_SHARDED, _unpack(gathered[-1], [s for _, s, _ in SMALL_SHARDED], (N_DEV,)))]))
    pending = _copy_start(shard(1), True, "gather_layer1_start", 5)

    def layer_weights(l, xp):
        full = {n: wts[n][l] for n, _ in REPLICATED}
        if l == 0:
            mats = gathered[:-1]
            full["norm1_g"] = _after(full["norm1_g"], pending[4])
        else:
            own, lands = _copy_wait(pending[0], pending[1], pending[2], pending[3], xp, True, "gather_layer1_wait")
            mats = [lax.dynamic_update_index_in_dim(z, s, me, 0) for z, s in zip(lands, own)]
        full["conv_w"] = small["conv_w"][l]
        for (n, _, axis), g in zip(BIG, mats):
            if n == "w_in":
                full["w_in_padded"] = _w_in_from_shards(g)
            else:
                full[n] = _join_shards(g, axis - 1)
        return full, xp

    big_names = [n for n, _, _ in BIG]
    early = [n for n in big_names if n in ("w_out", "w_ff1", "w_ff2")]
    late = [n for n in big_names if n not in early]
    cut = lambda g, names: [(g[n + "_shards"] if n + "_shards" in g else _cut_shards(g[n], axis - 1)).astype(BF16)
                            for n, _, axis in BIG if n in names]
    layer_grads = [None] * DEPTH
    flight = {}

    def layer_mid(l, g, w_out):
        if l == 0:
            flight["l0_early"] = _copy_start(cut(g, early), False, "scatter_layer0_early_start", 7)
            w_out = _after(w_out, flight["l0_early"][4])
        return w_out

    def layer_done(l, g, dx):
        layer_grads[l] = g
        if l == 1:
            flight["l1"] = _copy_start(cut(g, big_names), False, "scatter_l1_start", 6)
            dx = _after(dx, flight["l1"][4])
        return dx

    def layer_matrices(l, mats, norm_g):
        if l == 0:
            flight["l0_late"] = _copy_start(cut(mats, late), False, "scatter_l0_late_start", 8)
            norm_g = _after(norm_g, flight["l0_late"][4])
        return norm_g

    loss, grad_x, g_meta, g_lb = _run_step(x[0], loss_target[0], small["meta"], wts["hgrn_lb_logits"],
                                           layer_weights, layer_done, layer_mid, layer_matrices)

    grads = {k: jnp.stack([layer_grads[l][k] for l in range(DEPTH)]) for k in layer_grads[0]
             if k not in big_names and not k.endswith("_shards")}
    grads["hgrn_lb_logits"] = g_lb
    own = [jnp.concatenate([grads[n] for n in grp], axis=0) for grp in VEC_GROUPS]
    own.append(jnp.broadcast_to(loss.reshape(1, 1), (1, 128)))
    flight["small"] = _copy_start(own, True, "gather_small_grads_start", 9)
    cuts = [_cut_shards(g_meta, 1), _cut_shards(grads["conv_w"], 2)]
    flight["small_x"] = _copy_start(cuts, False, "scatter_small_grads_start", 10)
    started = flight["small_x"][4]

    def arrive(key, names, after):
        s_sems, r_sems, sent, lands, _ = flight[key]
        sent, lands = _copy_wait(s_sems, r_sems, sent, lands, after, False, f"scatter_{key}_wait")
        return {n: lax.dynamic_update_index_in_dim(z, lax.dynamic_index_in_dim(s, me, 0, keepdims=False), me, 0)
                for n, z, s in zip(names, lands, sent)}

    out = {}

    def update(names, recv0, recv1):
        for n in names:
            res4 = _adamw_layers([recv0[n], recv1[n]], wts[n], mom[n], var[n], "adamw_" + n)
            for kind, a in zip(("grad_", "delta_", "new_m_", "new_v_"), res4):
                out[kind + n] = a

    recv1 = arrive("l1", big_names, started)
    recv0 = arrive("l0_early", early, started)
    update(early, recv0, recv1)

    s_sems, r_sems, sent, lands, _ = flight["small"]
    updated = lax.optimization_barrier(tuple(out["grad_" + n] for n in early))
    own, lands = _copy_wait(s_sems, r_sems, sent, lands, updated[0], True, "gather_small_grads_wait")
    s_sems, r_sems, sent, lands_x, _ = flight["small_x"]
    sent, lands_x = _copy_wait(s_sems, r_sems, sent, lands_x, updated[0], False, "scatter_small_grads_wait")
    own += [lax.dynamic_index_in_dim(s, me, 0, keepdims=False) for s in sent]
    small_out, loss = _adamw_small(own, lands + lands_x, wts, mom, var, "adamw_small")
    out.update(small_out)
    loss = loss[0, 0]

    recv0 = arrive("l0_late", late, small_out["grad_norm1_g"])
    update(late, recv0, recv1)

    res = [loss, grad_x[None]]
    for kind in ("grad_", "delta_", "new_m_", "new_v_"):
        res += [out[kind + n] for n in WEIGHT_ORDER]
    return tuple(res)
```

```python
import functools

import numpy as np
import jax
import jax.numpy as jnp
from jax import lax
from jax.experimental import pallas as pl
from jax.experimental.pallas import tpu as pltpu

F32 = jnp.float32
BF16 = jnp.bfloat16

D_MODEL = 1024
DEPTH = 2
N_META = 16
PAD_FRONT = 112
ROW0 = PAD_FRONT + N_META
EPS = 1e-6
GATE_CLAMP = 1.0 - 1e-6
CONV_DIM = 512
CONV_K = 31
HEADS = 8
Q_RANK = 256
KV_RANK = 128
NOPE = 64
ROPE = 32
V_DIM = 64
QK_DIM = NOPE + ROPE
HEAD_W = 128
ROPE_BASE = 10000.0
HG_HEADS = 4
HG_DK = 128
HG_DV = 128
HG_CHUNK = 64
D_FF = 4096
N_IN = 6560
C_CONV_A, C_CONV_G, C_GATE, C_CQ, C_CKV, C_KR, C_HQ, C_HF, C_HI, C_HG = (
    0, 512, 1024, 4096, 4352, 4480, 4608, 5120, 5632, 6144)
N_IN_P = 6656
O_CQ, O_KR, O_HQ, O_GATE = 1024, 1408, 1440, 3488
KR_LANE = NOPE

ADAM_LR = 0.001
ADAM_B1 = 0.9
ADAM_B2 = 0.999
ADAM_EPS = 1e-08
ADAM_WD = 0.01
ADAM_STEP = 10

N_DEV = 8
VMEM_LIMIT = 56 * 1024 * 1024
MESH = pl.DeviceIdType.MESH


def _pick(n, cands):
    for c in cands:
        if n % c == 0:
            return c
    raise ValueError(f"no tile for {n}")


def _cparams(sem, **kw):
    return pltpu.CompilerParams(dimension_semantics=sem, vmem_limit_bytes=VMEM_LIMIT, **kw)


def _relu2(v):
    return jnp.square(jnp.maximum(v, 0.0))


def _mm(a, b, *, ta=False, tb=False, out_dtype=F32, res=None, a_fn=None, epi=None, name):
    M, K = (a.shape[1], a.shape[0]) if ta else a.shape
    N = b.shape[0] if tb else b.shape[1]
    assert (b.shape[1] if tb else b.shape[0]) == K, (a.shape, b.shape, ta, tb)
    tm = _pick(M, (1056, 1024, 512, 384, 256, 128, 96))
    tn = _pick(N, (1664, 1024, 512, 384, 256, 128))
    tk = _pick(K, (1664, 1056, 1024, 512, 384, 256, 128, 96) if ta else (1664, 1408, 1024, 512, 384, 256, 128))
    nk = K // tk
    dims = (((0 if ta else 1,), (1 if tb else 0,)), ((), ()))
    extras = ([res] if res is not None else []) + ([epi[0]] if epi is not None else [])

    def body(*refs):
        a_ref, b_ref = refs[0], refs[1]
        r_ref = refs[2] if res is not None else None
        e_ref = refs[2 + (res is not None)] if epi is not None else None
        o_ref = refs[2 + len(extras)]
        acc = refs[-1] if nk > 1 else None
        k = pl.program_id(2)
        av = a_ref[...]
        if a_fn is not None:
            av = a_fn(av.astype(F32))
        p = lax.dot_general(av.astype(BF16), b_ref[...].astype(BF16), dims, preferred_element_type=F32)

        def finish(total):
            if e_ref is not None:
                total = epi[1](total, e_ref[...].astype(F32))
            if r_ref is not None:
                total = total + r_ref[...].astype(F32)
            o_ref[...] = total.astype(o_ref.dtype)

        if nk == 1:
            finish(p)
        else:
            @pl.when(k == 0)
            def _():
                acc[...] = p

            @pl.when(k > 0)
            def _():
                acc[...] += p

            @pl.when(k == nk - 1)
            def _():
                finish(acc[...])

    a_spec = pl.BlockSpec((tk, tm), lambda i, j, k: (k, i)) if ta else pl.BlockSpec((tm, tk), lambda i, j, k: (i, k))
    b_spec = pl.BlockSpec((tn, tk), lambda i, j, k: (j, k)) if tb else pl.BlockSpec((tk, tn), lambda i, j, k: (k, j))
    o_spec = pl.BlockSpec((tm, tn), lambda i, j, k: (i, j))
    in_specs = [a_spec, b_spec] + [o_spec] * len(extras)
    args = (a, b) + tuple(extras)
    return pl.pallas_call(
        body, name=name, grid=(M // tm, N // tn, nk), in_specs=in_specs, out_specs=o_spec,
        out_shape=jax.ShapeDtypeStruct((M, N), out_dtype),
        scratch_shapes=[pltpu.VMEM((tm, tn), F32)] if nk > 1 else [],
        compiler_params=_cparams(("parallel", "parallel", "arbitrary")),
    )(*args)


class Row:
    def __init__(self, arr, width=None, col=0, piece=None):
        self.arr = arr
        self.width = arr.shape[1] if width is None else width
        assert col % self.width == 0
        self.blk = col // self.width
        self.piece = self.width if piece is None else piece

    def spec(self, tm):
        blk = self.blk
        return pl.BlockSpec((tm, self.width), lambda i: (i, blk))


def _split(v, piece):
    w = v.shape[-1]
    if piece == w:
        return v
    return [v[:, j * piece:(j + 1) * piece] for j in range(w // piece)]


def _store(ref, val, dtype=None):
    if isinstance(val, (list, tuple)):
        piece = val[0].shape[-1]
        for j, p in enumerate(val):
            ref[:, j * piece:(j + 1) * piece] = p.astype(ref.dtype)
    else:
        ref[...] = val.astype(ref.dtype)


def _row_tile(T):
    return _pick(T, (384, 352, 192, 128))


def _param2d(p):
    return p.reshape(1, -1).astype(F32)


def _rowwise(fn, T, rows, params, outs, name, transposed=False):
    tm = _row_tile(T)
    nr, npar = len(rows), len(params)
    par = [(_param2d(p), piece) for p, piece in params]

    def body(*refs):
        rid = pl.program_id(0) * tm + lax.broadcasted_iota(jnp.int32, (tm, 1), 0)
        rv = [_split(refs[n][...].astype(F32), rows[n].piece) for n in range(nr)]
        pv = [_split(refs[nr + n][...], par[n][1]) for n in range(npar)]
        res = fn(rid, rv, pv)
        for n, val in enumerate(res):
            _store(refs[nr + npar + n], val)
        if transposed:
            refs[-1][...] = res[0].T.astype(refs[-1].dtype)

    out_specs = [pl.BlockSpec((tm, w), lambda i: (i, 0)) for w, _ in outs]
    out_shape = [jax.ShapeDtypeStruct((T, w), dt) for w, dt in outs]
    if transposed:
        out_specs.append(pl.BlockSpec((outs[0][0], tm), lambda i: (0, i)))
        out_shape.append(jax.ShapeDtypeStruct((outs[0][0], T), outs[0][1]))
    return pl.pallas_call(
        body, name=name, grid=(T // tm,),
        in_specs=[r.spec(tm) for r in rows] + [pl.BlockSpec(p.shape, lambda i: (0, 0)) for p, _ in par],
        out_specs=out_specs, out_shape=out_shape,
        compiler_params=_cparams(("parallel",)),
    )(*[r.arr for r in rows], *[p for p, _ in par])


def _rowwise_bwd(fn, T, rows, params, cts, drow, name, add=None):
    tm = _row_tile(T)
    nr, npar, nct = len(rows), len(params), len(cts)
    par = [(_param2d(p), piece) for p, piece in params]
    didx = sorted(drow)
    has_add = add is not None

    def body(*refs):
        i = pl.program_id(0)
        rid = i * tm + lax.broadcasted_iota(jnp.int32, (tm, 1), 0)
        rv = [_split(refs[n][...].astype(F32), rows[n].piece) for n in range(nr)]
        pv = [_split(refs[nr + n][...], par[n][1]) for n in range(npar)]
        cv = [_split(refs[nr + npar + n][...].astype(F32), cts[n].piece) for n in range(nct)]
        base = nr + npar + nct + (1 if has_add else 0)
        d_refs = refs[base:base + len(didx)]
        p_refs = refs[base + len(didx):]

        def g(dvals, pvals):
            full = list(rv)
            for n, v in zip(didx, dvals):
                full[n] = v
            return fn(rid, full, pvals)

        _, vjp = jax.vjp(g, [rv[n] for n in didx], pv)
        d_rows, d_pars = vjp(cv)
        for slot, n in enumerate(didx):
            val = d_rows[slot]
            if has_add and add[0] == n:
                assert not isinstance(val, (list, tuple))
                val = val + refs[nr + npar + nct][...].astype(F32)
            _store(d_refs[slot], val)

        @pl.when(i == 0)
        def _():
            for r in p_refs:
                r[...] = jnp.zeros_like(r)

        for r, val in zip(p_refs, d_pars):
            if isinstance(val, (list, tuple)):
                piece = val[0].shape[-1]
                for j, p in enumerate(val):
                    r[:, j * piece:(j + 1) * piece] += p
            else:
                r[...] += val

    in_specs = ([r.spec(tm) for r in rows] + [pl.BlockSpec(p.shape, lambda i: (0, 0)) for p, _ in par]
                + [c.spec(tm) for c in cts])
    args = [r.arr for r in rows] + [p for p, _ in par] + [c.arr for c in cts]
    if has_add:
        in_specs.append(pl.BlockSpec((tm, rows[add[0]].width), lambda i: (i, 0)))
        args.append(add[1])
    out_specs = ([pl.BlockSpec((tm, rows[n].width), lambda i: (i, 0)) for n in didx]
                 + [pl.BlockSpec(p.shape, lambda i: (0, 0)) for p, _ in par])
    out_shape = ([jax.ShapeDtypeStruct((T, rows[n].width), drow[n]) for n in didx]
                 + [jax.ShapeDtypeStruct(p.shape, F32) for p, _ in par])
    res = pl.pallas_call(
        body, name=name, grid=(T // tm,), in_specs=in_specs, out_specs=out_specs, out_shape=out_shape,
        compiler_params=_cparams(("arbitrary",)),
    )(*args)
    return list(res[:len(didx)]), list(res[len(didx):])


def _f_rms(rid, rv, pv):
    x, g = rv[0], pv[0]
    return [x * lax.rsqrt(jnp.mean(x * x, axis=-1, keepdims=True) + EPS) * g]


def _f_glu(rid, rv, pv):
    a, gt = rv
    return [a * jax.nn.sigmoid(gt) * (rid >= PAD_FRONT).astype(F32)]


def _f_lnsilu(rid, rv, pv):
    x = rv[0]
    g, b = pv
    mu = jnp.mean(x, axis=-1, keepdims=True)
    xc = x - mu
    y = xc * lax.rsqrt(jnp.mean(xc * xc, axis=-1, keepdims=True) + EPS) * g + b
    return [y * jax.nn.sigmoid(y)]


@functools.partial(jax.custom_vjp, nondiff_argnums=(1,))
def _lane_roll(x, shift):
    return pltpu.roll(x, shift, 1)


def _lane_roll_fwd(x, shift):
    return pltpu.roll(x, shift, 1), None


def _lane_roll_bwd(shift, _, g):
    return (pltpu.roll(g, (HEAD_W - shift) % HEAD_W, 1),)


_lane_roll.defvjp(_lane_roll_fwd, _lane_roll_bwd)


def _head_norm_rope(xh, g, c, s1, s2):
    y = xh * lax.rsqrt(jnp.sum(xh * xh, axis=-1, keepdims=True) * (1.0 / QK_DIM) + EPS) * g
    half = ROPE // 2
    return y * c + _lane_roll(y, HEAD_W - half) * s1 + _lane_roll(y, half) * s2


def _f_qrope(rid, rv, pv):
    q, c, s1, s2 = rv
    return [[_head_norm_rope(qh, pv[0], c, s1, s2) * ATT_SCALE for qh in q]]


def _f_krope(rid, rv, pv):
    k, kr, c, s1, s2 = rv
    return [[_head_norm_rope(kh + kr, pv[0], c, s1, s2) for kh in k]]


def _f_hgrn_prep(rid, rv, pv):
    hf, hi = rv
    m = (rid >= PAD_FRONT).astype(F32)
    kk = (1.0 - pv[0]) * jax.nn.sigmoid(-hf) * m
    lf = jnp.log1p(-jnp.minimum(kk, GATE_CLAMP))
    vv = hi * jax.nn.sigmoid(hi) * m
    return [kk, lf, vv]


def _f_hgrn_out(rid, rv, pv):
    o, hg = rv
    ng = pv[0]
    out = []
    for oh, gh, nh in zip(o, hg, ng):
        y = oh * lax.rsqrt(jnp.mean(oh * oh, axis=-1, keepdims=True) + EPS) * nh
        out.append(y * (gh * jax.nn.sigmoid(gh)))
    return [out]


def _f_mix(rid, rv, pv):
    g0, g1, g2, ya, yb, yc = rv
    return [jax.nn.sigmoid(g0) * ya + jax.nn.sigmoid(g1) * yb + jax.nn.sigmoid(g2) * yc]


def _f_relu2(rid, rv, pv):
    return [jnp.square(jax.nn.relu(rv[0]))]


def _loss_head(x2, tgt, T):
    tm = _row_tile(T)

    def body(x_ref, t_ref, dx_ref, l_ref):
        i = pl.program_id(0)
        rid = i * tm + lax.broadcasted_iota(jnp.int32, (tm, 1), 0)
        diff = (x_ref[...] - t_ref[...]) * (rid >= ROW0).astype(F32)
        dx_ref[...] = diff * (1.0 / D_MODEL)

        @pl.when(i == 0)
        def _():
            l_ref[...] = jnp.zeros_like(l_ref)

        l_ref[...] += jnp.sum(diff * diff, axis=0, keepdims=True)

    spec = pl.BlockSpec((tm, D_MODEL), lambda i: (i, 0))
    return pl.pallas_call(
        body, name="loss_head", grid=(T // tm,), in_specs=[spec, spec],
        out_specs=[spec, pl.BlockSpec((1, D_MODEL), lambda i: (0, 0))],
        out_shape=[jax.ShapeDtypeStruct((T, D_MODEL), F32), jax.ShapeDtypeStruct((1, D_MODEL), F32)],
        compiler_params=_cparams(("arbitrary",)),
    )(x2, tgt)


HALO = 32


CONV_ROWS = 64


def _conv_lanes():
    return [slice(c, c + 128) for c in range(0, CONV_DIM, 128)]


def _conv_tile(T):
    return _pick(T, (384, 128))


def _conv_fwd(h, w, b, T, name):
    tr = _conv_tile(T)
    ratio = tr // HALO
    wp = jnp.zeros((HALO, CONV_DIM), F32).at[:CONV_K].set(w)

    def body(m_ref, h_ref, w_ref, b_ref, o_ref, win):
        i = pl.program_id(0)
        win[0:HALO, :] = h_ref[...] * (i > 0).astype(F32)
        win[HALO:, :] = m_ref[...]
        for cs in _conv_lanes():
            wv, bv = w_ref[:, cs], b_ref[:, cs]
            for r0 in range(0, tr, CONV_ROWS):
                acc = jnp.broadcast_to(bv, (CONV_ROWS, 128))
                for k in range(CONV_K):
                    acc = acc + wv[k:k + 1] * win[pl.ds(HALO - (CONV_K - 1) + k + r0, CONV_ROWS), cs]
                o_ref[r0:r0 + CONV_ROWS, cs] = acc

    return pl.pallas_call(
        body, name=name, grid=(T // tr,),
        in_specs=[pl.BlockSpec((tr, CONV_DIM), lambda i: (i, 0)),
                  pl.BlockSpec((HALO, CONV_DIM), lambda i: (jnp.maximum(i * ratio - 1, 0), 0)),
                  pl.BlockSpec((HALO, CONV_DIM), lambda i: (0, 0)),
                  pl.BlockSpec((1, CONV_DIM), lambda i: (0, 0))],
        out_specs=pl.BlockSpec((tr, CONV_DIM), lambda i: (i, 0)),
        out_shape=jax.ShapeDtypeStruct((T, CONV_DIM), F32),
        scratch_shapes=[pltpu.VMEM((tr + HALO, CONV_DIM), F32)],
        compiler_params=_cparams(("parallel",)),
    )(h, h, wp, _param2d(b))


def _conv_bwd(h, w, dy, T, name):
    tr = _conv_tile(T)
    ratio = tr // HALO
    n_t = T // tr
    last_halo = T // HALO - 1
    wp = jnp.zeros((HALO, CONV_DIM), F32).at[:CONV_K].set(w)

    def body(hm_ref, hh_ref, dm_ref, dh_ref, w_ref, dx_ref, dw_ref, db_ref, hwin, dwin):
        i = pl.program_id(0)
        hwin[0:HALO, :] = hh_ref[...] * (i > 0).astype(F32)
        hwin[HALO:, :] = hm_ref[...]
        dwin[0:tr, :] = dm_ref[...]
        dwin[tr:, :] = dh_ref[...] * (i < n_t - 1).astype(F32)

        @pl.when(i == 0)
        def _():
            dw_ref[...] = jnp.zeros_like(dw_ref)
            db_ref[...] = jnp.zeros_like(db_ref)

        db_ref[...] += jnp.sum(dm_ref[...], axis=0, keepdims=True)
        fold = lambda a: functools.reduce(jnp.add, [a[r:r + 8] for r in range(0, CONV_ROWS, 8)])
        for cs in _conv_lanes():
            wv = w_ref[:, cs]
            dws = [jnp.zeros((8, 128), F32) for _ in range(CONV_K)]
            for r0 in range(0, tr, CONV_ROWS):
                acc = jnp.zeros((CONV_ROWS, 128), F32)
                for k in range(CONV_K):
                    acc = acc + wv[k:k + 1] * dwin[pl.ds(CONV_K - 1 - k + r0, CONV_ROWS), cs]
                dx_ref[r0:r0 + CONV_ROWS, cs] = acc
                dy_t = dm_ref[r0:r0 + CONV_ROWS, cs]
                for k in range(CONV_K):
                    dws[k] = dws[k] + fold(dy_t * hwin[pl.ds(HALO - (CONV_K - 1) + k + r0, CONV_ROWS), cs])
            for k in range(CONV_K):
                dw_ref[k:k + 1, cs] += jnp.sum(dws[k], axis=0, keepdims=True)

    main = pl.BlockSpec((tr, CONV_DIM), lambda i: (i, 0))
    return pl.pallas_call(
        body, name=name, grid=(n_t,),
        in_specs=[main,
                  pl.BlockSpec((HALO, CONV_DIM), lambda i: (jnp.maximum(i * ratio - 1, 0), 0)),
                  main,
                  pl.BlockSpec((HALO, CONV_DIM), lambda i: (jnp.minimum((i + 1) * ratio, last_halo), 0)),
                  pl.BlockSpec((HALO, CONV_DIM), lambda i: (0, 0))],
        out_specs=[main, pl.BlockSpec((HALO, CONV_DIM), lambda i: (0, 0)), pl.BlockSpec((1, CONV_DIM), lambda i: (0, 0))],
        out_shape=[jax.ShapeDtypeStruct((T, CONV_DIM), F32), jax.ShapeDtypeStruct((HALO, CONV_DIM), F32),
                   jax.ShapeDtypeStruct((1, CONV_DIM), F32)],
        scratch_shapes=[pltpu.VMEM((tr + HALO, CONV_DIM), F32), pltpu.VMEM((tr + HALO, CONV_DIM), F32)],
        compiler_params=_cparams(("arbitrary",)),
    )(h, h, dy, dy, wp)


NEG = -1e30
ATT_SCALE = QK_DIM ** -0.5
_NT = (((1,), (1,)), ((), ()))
_TN = (((0,), (0,)), ((), ()))


def _att_blk(T):
    return _pick(T, (384, 128))


def _att_mask(i, j, blk):
    kpos = j * blk + lax.broadcasted_iota(jnp.int32, (blk, blk), 0)
    qpos = i * blk + lax.broadcasted_iota(jnp.int32, (blk, blk), 1)
    return (kpos <= qpos) & (kpos >= PAD_FRONT)


def _t32(a):
    return a.astype(F32).T.astype(BF16)


def _attn_fwd(q, k, v, T, name):
    blk = _att_blk(T)
    nq = T // blk

    def body(q_ref, k_ref, v_ref, o_ref, lse_ref, vt):
        i = pl.program_id(1)

        @pl.when(i == 0)
        def _():
            def tr(j, c):
                vt[j] = _t32(v_ref[pl.ds(pl.multiple_of(j * blk, blk), blk), :])
                return c

            lax.fori_loop(0, nq, tr, 0)

        qb = q_ref[...]

        def step(js, carry, masked):
            m, l, acc = carry
            ss = []
            for j in js:
                kb = k_ref[pl.ds(pl.multiple_of(j * blk, blk), blk), :]
                s = lax.dot_general(kb, qb, _NT, preferred_element_type=F32)
                ss.append(jnp.where(_att_mask(i, j, blk), s, NEG) if masked else s)
            m_new = m
            for s in ss:
                m_new = jnp.maximum(m_new, jnp.max(s, axis=0, keepdims=True))
            alpha = jnp.exp(m - m_new)
            l = alpha * l
            acc = alpha * acc
            for j, s in zip(js, ss):
                p = jnp.exp(s - m_new)
                l = l + jnp.sum(p, axis=0, keepdims=True)
                acc = acc + jnp.dot(vt[j], p.astype(BF16), preferred_element_type=F32)
            return m_new, l, acc

        init = (jnp.full((1, blk), NEG, F32), jnp.zeros((1, blk), F32), jnp.zeros((HEAD_W, blk), F32))
        later = jnp.minimum(i, 1)
        carry = lax.fori_loop(0, 1 - later, lambda t, c: step([i], c, True), init)
        carry = lax.fori_loop(0, later, lambda t, c: step([i, 0], c, True), carry)
        n_free = jnp.maximum(i - 1, 0)
        n4 = n_free // 4
        carry = lax.fori_loop(0, n4, lambda t, c: step([1 + 4 * t + d for d in range(4)], c, False), carry)
        rest = n_free - 4 * n4
        carry = lax.fori_loop(0, rest // 2, lambda t, c: step([i - rest, i - rest + 1], c, False), carry)
        m, l, acc = lax.fori_loop(0, rest % 2, lambda t, c: step([i - 1], c, False), carry)
        o_ref[...] = (acc / l).T.astype(o_ref.dtype)
        lse_ref[0, 0] = m + jnp.log(l)

    full = pl.BlockSpec((T, HEAD_W), lambda h, i: (0, h))
    return pl.pallas_call(
        body, name=name, grid=(HEADS, nq),
        in_specs=[pl.BlockSpec((blk, HEAD_W), lambda h, i: (i, h)), full, full],
        out_specs=[pl.BlockSpec((blk, HEAD_W), lambda h, i: (i, h)),
                   pl.BlockSpec((1, 1, 1, blk), lambda h, i: (h, i, 0, 0))],
        out_shape=[jax.ShapeDtypeStruct((T, HEADS * HEAD_W), BF16), jax.ShapeDtypeStruct((HEADS, nq, 1, blk), F32)],
        scratch_shapes=[pltpu.VMEM((nq, HEAD_W, blk), BF16)],
        compiler_params=_cparams(("parallel", "arbitrary")),
    )(q, k, v)


def _attn_bwd(q, k, v, o, lse, do, T, name):
    blk = _att_blk(T)
    nq = T // blk

    def body(q_ref, k_ref, v_ref, o_ref, lse_ref, do_ref, dq_ref, dk_ref, dv_ref, delta, dqt, dk_acc, dv_acc):
        j = pl.program_id(1)

        @pl.when(j == 0)
        def _():
            dqt[...] = jnp.zeros_like(dqt)

            def dstep(i, c):
                r0 = pl.multiple_of(i * blk, blk)
                prod = do_ref[pl.ds(r0, blk), :].astype(F32) * o_ref[pl.ds(r0, blk), :].astype(F32)
                delta[i] = jnp.sum(prod.T, axis=0, keepdims=True)
                return c

            lax.fori_loop(0, nq, dstep, 0)

        kb = k_ref[...]
        vb = v_ref[...]
        kbt = _t32(kb)
        dk_acc[...] = jnp.zeros_like(dk_acc)
        dv_acc[...] = jnp.zeros_like(dv_acc)

        def step(qs, masked):
            dvs, dks = [], []
            for i in qs:
                r0 = pl.multiple_of(i * blk, blk)
                qb = q_ref[pl.ds(r0, blk), :]
                dob = do_ref[pl.ds(r0, blk), :]
                s = lax.dot_general(kb, qb, _NT, preferred_element_type=F32)
                p = jnp.exp(s - lse_ref[0, i])
                if masked:
                    p = jnp.where(_att_mask(i, j, blk), p, 0.0)
                dvs.append(jnp.dot(p.astype(BF16), dob, preferred_element_type=F32))
                dp = lax.dot_general(vb, dob, _NT, preferred_element_type=F32)
                ds = (p * (dp - delta[i])).astype(BF16)
                dks.append(jnp.dot(ds, qb, preferred_element_type=F32))
                dqt[i] += jnp.dot(kbt, ds, preferred_element_type=F32)
            dv_acc[...] += functools.reduce(jnp.add, dvs)
            dk_acc[...] += functools.reduce(jnp.add, dks)

        def loop(lo, masked):
            n = nq - lo
            n3 = n // 3
            rest = n - 3 * n3

            def triple(t, c):
                step([lo + 3 * t + d for d in range(3)], masked)
                return c

            def pair(t, c):
                step([nq - 2, nq - 1], masked)
                return c

            def last(t, c):
                step([nq - 1], masked)
                return c

            lax.fori_loop(0, n3, triple, 0)
            lax.fori_loop(0, rest // 2, pair, 0)
            lax.fori_loop(0, rest % 2, last, 0)

        @pl.when(j == 0)
        def _():
            loop(0, True)

        @pl.when(j > 0)
        def _():
            step([j], True)
            loop(j + 1, False)

        dk_ref[...] = dk_acc[...].astype(dk_ref.dtype)
        dv_ref[...] = dv_acc[...].astype(dv_ref.dtype)

        @pl.when(j == nq - 1)
        def _():
            def wstep(i, c):
                dq_ref[pl.ds(pl.multiple_of(i * blk, blk), blk), :] = dqt[i].T
                return c

            lax.fori_loop(0, nq, wstep, 0)

    full = pl.BlockSpec((T, HEAD_W), lambda h, j: (0, h))
    kblk = pl.BlockSpec((blk, HEAD_W), lambda h, j: (j, h))
    wide = (T, HEADS * HEAD_W)
    return pl.pallas_call(
        body, name=name, grid=(HEADS, nq),
        in_specs=[full, kblk, kblk, full, pl.BlockSpec((1, nq, 1, blk), lambda h, j: (h, 0, 0, 0)), full],
        out_specs=[full, kblk, kblk],
        out_shape=[jax.ShapeDtypeStruct(wide, F32), jax.ShapeDtypeStruct(wide, BF16), jax.ShapeDtypeStruct(wide, BF16)],
        scratch_shapes=[pltpu.VMEM((nq, 1, blk), F32), pltpu.VMEM((nq, HEAD_W, blk), F32),
                        pltpu.VMEM((blk, HEAD_W), F32), pltpu.VMEM((blk, HEAD_W), F32)],
        compiler_params=_cparams(("parallel", "arbitrary")),
    )(q, k, v, o, lse, do)


HG_NB = 6
C = HG_CHUNK
_HI = lax.Precision.HIGHEST


def _tri(lower):
    r = lax.broadcasted_iota(jnp.int32, (C, C), 0)
    c = lax.broadcasted_iota(jnp.int32, (C, C), 1)
    return ((c <= r) if lower else (c >= r)).astype(F32)


HG_SUB = 8
N_SUB = C // HG_SUB


def _hg_split_decay(b, I, rid):
    lo = I * HG_SUB
    r = b[lo:lo + 1]
    eq = jnp.exp(b[lo:lo + HG_SUB] - r)
    ek = jnp.where(rid < lo, jnp.exp(jnp.minimum(r - b, 0.0)), 0.0)
    return eq, ek


def _hg_intra_fwd(q, k, v, b):
    rid = lax.broadcasted_iota(jnp.int32, (C, 1), 0)
    tid = lax.broadcasted_iota(jnp.int32, (HG_SUB, 1), 0)
    a_rows = [jnp.zeros((HG_SUB, C), F32)]
    blocks = []
    for I in range(N_SUB):
        lo = I * HG_SUB
        q_i, b_i = q[lo:lo + HG_SUB], b[lo:lo + HG_SUB]
        if I > 0:
            eq, ek = _hg_split_decay(b, I, rid)
            a_rows.append(lax.dot_general((q_i * eq).astype(BF16), (k * ek).astype(BF16), _NT,
                                          preferred_element_type=F32))
        o_i = jnp.zeros((HG_SUB, HG_DV), F32)
        for s in range(HG_SUB):
            r = lo + s
            e = jnp.exp(jnp.minimum(b_i - b[r:r + 1], 0.0))
            a = jnp.sum(q_i * k[r:r + 1] * e, axis=-1, keepdims=True)
            o_i = o_i + jnp.where(tid >= s, a, 0.0) * v[r:r + 1]
        blocks.append(o_i)
    a_off = jnp.concatenate(a_rows, axis=0).astype(BF16)
    return jnp.dot(a_off, v.astype(BF16), preferred_element_type=F32) + jnp.concatenate(blocks, axis=0)


def _hg_intra_bwd(q, k, v, b, do, dk_s, dv_s):
    rid = lax.broadcasted_iota(jnp.int32, (C, 1), 0)
    tid = lax.broadcasted_iota(jnp.int32, (HG_SUB, 1), 0)
    da_all = lax.dot_general(do.astype(BF16), v.astype(BF16), _NT, preferred_element_type=F32)
    a_rows = [jnp.zeros((HG_SUB, C), F32)]
    dq_blocks = []
    dk = jnp.zeros((C, HG_DK), F32)
    for I in range(N_SUB):
        lo = I * HG_SUB
        q_i, b_i, do_i = q[lo:lo + HG_SUB], b[lo:lo + HG_SUB], do[lo:lo + HG_SUB]
        dq_i = jnp.zeros((HG_SUB, HG_DK), F32)
        if I > 0:
            eq, ek = _hg_split_decay(b, I, rid)
            qs, ks = (q_i * eq).astype(BF16), (k * ek).astype(BF16)
            a_rows.append(lax.dot_general(qs, ks, _NT, preferred_element_type=F32))
            da = da_all[lo:lo + HG_SUB].astype(BF16)
            dq_i = jnp.dot(da, ks, preferred_element_type=F32) * eq
            dk = dk + lax.dot_general(da, qs, _TN, preferred_element_type=F32) * ek
        for s in range(HG_SUB):
            r = lo + s
            e = jnp.where(tid >= s, jnp.exp(jnp.minimum(b_i - b[r:r + 1], 0.0)), 0.0)
            a = jnp.sum(q_i * k[r:r + 1] * e, axis=-1, keepdims=True)
            g = jnp.sum(do_i * v[r:r + 1], axis=-1, keepdims=True) * e
            dq_i = dq_i + g * k[r:r + 1]
            dk_s[r:r + 1, :] = jnp.sum(g * q_i, axis=0, keepdims=True)
            dv_s[r:r + 1, :] = jnp.sum(a * do_i, axis=0, keepdims=True)
        dq_blocks.append(dq_i)
    a_off = jnp.concatenate(a_rows, axis=0).astype(BF16)
    dv = lax.dot_general(a_off, do.astype(BF16), _TN, preferred_element_type=F32)
    return jnp.concatenate(dq_blocks, axis=0), dk + dk_s[...], dv + dv_s[...]


def _hgrn_fwd(u, kk, lf, vv, T, name):
    nb = _pick(T // C, (HG_NB, 3, 2, 1))
    rows = nb * C
    qblk = C_HQ // HG_DK

    def body(q_ref, k_ref, lf_ref, v_ref, o_ref, st_ref, st):
        @pl.when(pl.program_id(1) == 0)
        def _():
            st[...] = jnp.zeros_like(st)

        lower = _tri(True)
        for n in range(nb):
            sl = slice(n * C, (n + 1) * C)
            q, k, v = q_ref[sl, :].astype(F32), k_ref[sl, :], v_ref[sl, :]
            b = jnp.dot(lower, lf_ref[sl, :], precision=_HI, preferred_element_type=F32)
            s_t = st[...]
            st_ref[0, n] = s_t
            qe = (q * jnp.exp(b)).astype(BF16)
            o = lax.dot_general(qe, s_t.astype(BF16), _NT, preferred_element_type=F32)
            o_ref[sl, :] = o + _hg_intra_fwd(q, k, v, b)
            bl = b[C - 1:C, :]
            kd = (k * jnp.exp(bl - b)).astype(BF16)
            st[...] = s_t * jnp.exp(bl) + lax.dot_general(v.astype(BF16), kd, _TN, preferred_element_type=F32)

    col = lambda off: pl.BlockSpec((rows, HG_DK), lambda h, c: (c, h + off))
    return pl.pallas_call(
        body, name=name, grid=(HG_HEADS, T // rows),
        in_specs=[col(qblk), col(0), col(0), col(0)],
        out_specs=[col(0), pl.BlockSpec((1, nb, HG_DV, HG_DK), lambda h, c: (h, c, 0, 0))],
        out_shape=[jax.ShapeDtypeStruct((T, HG_HEADS * HG_DV), F32),
                   jax.ShapeDtypeStruct((HG_HEADS, T // C, HG_DV, HG_DK), F32)],
        scratch_shapes=[pltpu.VMEM((HG_DV, HG_DK), F32)],
        compiler_params=_cparams(("parallel", "arbitrary")),
    )(u, kk, lf, vv)


def _hgrn_bwd(u, kk, lf, vv, states, do, T, name):
    nb = _pick(T // C, (HG_NB, 3, 2, 1))
    rows = nb * C
    n_steps = T // rows
    qblk = C_HQ // HG_DK

    def body(q_ref, k_ref, lf_ref, v_ref, st_ref, do_ref, dq_ref, dk_ref, dlf_ref, dv_ref, dst, dk_s, dv_s):
        @pl.when(pl.program_id(1) == 0)
        def _():
            dst[...] = jnp.zeros_like(dst)

        lower, upper = _tri(True), _tri(False)
        rid = lax.broadcasted_iota(jnp.int32, (C, 1), 0)
        for n in reversed(range(nb)):
            sl = slice(n * C, (n + 1) * C)
            q, k, v, do = q_ref[sl, :].astype(F32), k_ref[sl, :], v_ref[sl, :], do_ref[sl, :]
            b = jnp.dot(lower, lf_ref[sl, :], precision=_HI, preferred_element_type=F32)
            s_t = st_ref[0, n]
            d_new = dst[...]
            eb = jnp.exp(b)
            bl = b[C - 1:C, :]
            ebl = jnp.exp(bl)
            dec = jnp.exp(bl - b)
            qe = q * eb
            kd = k * dec
            do_b = do.astype(BF16)
            dqe = jnp.dot(do_b, s_t.astype(BF16), preferred_element_type=F32)
            dkd = jnp.dot(v.astype(BF16), d_new.astype(BF16), preferred_element_type=F32)
            dv = lax.dot_general(kd.astype(BF16), d_new.astype(BF16), _NT, preferred_element_type=F32)
            dbl = ebl * jnp.sum(d_new * s_t, axis=0, keepdims=True) + jnp.sum(dkd * kd, axis=0, keepdims=True)
            dst[...] = d_new * ebl + lax.dot_general(do_b, qe.astype(BF16), _TN, preferred_element_type=F32)
            dq_in, dk_in, dv_in = _hg_intra_bwd(q, k, v, b, do, dk_s, dv_s)
            dq = dqe * eb + dq_in
            dk = dkd * dec + dk_in
            dv = dv + dv_in
            db = q * dq - k * dk
            db = db + jnp.where(rid == C - 1, dbl, 0.0)
            dq_ref[sl, :] = dq
            dk_ref[sl, :] = dk
            dv_ref[sl, :] = dv
            dlf_ref[sl, :] = jnp.dot(upper, db, precision=_HI, preferred_element_type=F32)

    rev = lambda off: pl.BlockSpec((rows, HG_DK), lambda h, c: (n_steps - 1 - c, h + off))
    return pl.pallas_call(
        body, name=name, grid=(HG_HEADS, n_steps),
        in_specs=[rev(qblk), rev(0), rev(0), rev(0),
                  pl.BlockSpec((1, nb, HG_DV, HG_DK), lambda h, c: (h, n_steps - 1 - c, 0, 0)), rev(0)],
        out_specs=[rev(0)] * 4,
        out_shape=[jax.ShapeDtypeStruct((T, HG_HEADS * HG_DK), F32)] * 4,
        scratch_shapes=[pltpu.VMEM((HG_DV, HG_DK), F32), pltpu.VMEM((C, HG_DK), F32), pltpu.VMEM((C, HG_DV), F32)],
        compiler_params=_cparams(("parallel", "arbitrary")),
    )(u, kk, lf, vv, states, do)


def _rope_tables(T):
    half = ROPE // 2
    inv_freq = (ROPE_BASE ** (-np.arange(half, dtype=np.float32) / half)).astype(np.float32)
    row = lambda lo, hi, val: np.concatenate([np.zeros(lo, np.float32), np.asarray(val, np.float32) * np.ones(hi - lo, np.float32),
                                              np.zeros(HEAD_W - hi, np.float32)])[None, :]
    freq = row(NOPE, NOPE + half, inv_freq) + row(NOPE + half, NOPE + ROPE, inv_freq)
    pos = lax.broadcasted_iota(jnp.int32, (T, HEAD_W), 0).astype(F32) - float(PAD_FRONT)
    ang = pos * freq
    cos, sin = jnp.cos(ang), jnp.sin(ang)
    c = cos * row(NOPE, NOPE + ROPE, 1.0) + row(0, NOPE, 1.0)
    s1 = sin * row(NOPE, NOPE + half, -1.0)
    s2 = sin * row(NOPE + half, NOPE + ROPE, 1.0)
    return c, s1, s2


def _layer_fwd(x, w, tabs, T, l):
    c, s1, s2 = tabs
    n = lambda s: f"l{l}_{s}"
    sv = {"x": x}
    h, sv["h_t"] = _rowwise(_f_rms, T, [Row(x)], [(w["norm1_g"], D_MODEL)], [(D_MODEL, BF16)], n("norm1"), True)
    u = _mm(h, w["w_in"], out_dtype=BF16, name=n("in_proj"))
    sv.update(h=h, u=u)
    hglu = _rowwise(_f_glu, T, [Row(u, 512, C_CONV_A), Row(u, 512, C_CONV_G)], [], [(CONV_DIM, F32)], n("glu"))[0]
    cv = _conv_fwd(hglu, w["conv_w"], w["conv_b"], T, n("conv"))
    hc = _rowwise(_f_lnsilu, T, [Row(cv)], [(w["conv_ln_g"], CONV_DIM), (w["conv_ln_b"], CONV_DIM)],
                  [(CONV_DIM, BF16)], n("conv_ln"))[0]
    y_a = _mm(hc, w["w_conv_out"], out_dtype=BF16, name=n("conv_out"))
    sv.update(hglu=hglu, cv=cv, hc=hc, y_a=y_a)
    cqn = _rowwise(_f_rms, T, [Row(u, Q_RANK, C_CQ)], [(w["q_a_norm_g"], Q_RANK)], [(Q_RANK, BF16)], n("q_a_norm"))[0]
    ckvn = _rowwise(_f_rms, T, [Row(u, KV_RANK, C_CKV)], [(w["kv_a_norm_g"], KV_RANK)], [(KV_RANK, BF16)], n("kv_a_norm"))[0]
    q_raw = _mm(cqn, w["w_uq"], out_dtype=BF16, name=n("uq"))
    k_raw = _mm(ckvn, w["w_uk"], out_dtype=BF16, name=n("uk"))
    v = _mm(ckvn, w["w_uv"], out_dtype=BF16, name=n("uv"))
    tab_rows = [Row(c), Row(s1), Row(s2)]
    q = _rowwise(_f_qrope, T, [Row(q_raw, piece=HEAD_W)] + tab_rows, [(w["q_norm_g"], HEAD_W)],
                 [(HEADS * HEAD_W, BF16)], n("q_rope"))[0]
    k = _rowwise(_f_krope, T, [Row(k_raw, piece=HEAD_W), Row(u, HEAD_W, C_KR)] + tab_rows, [(w["k_norm_g"], HEAD_W)],
                 [(HEADS * HEAD_W, BF16)], n("k_rope"))[0]
    o, lse = _attn_fwd(q, k, v, T, n("attn"))
    y_b = _mm(o, w["w_attn_out"], out_dtype=BF16, name=n("attn_out"))
    sv.update(cqn=cqn, ckvn=ckvn, q_raw=q_raw, k_raw=k_raw, v=v, q=q, k=k, o=o, lse=lse, y_b=y_b)
    kk, lf, vv = _rowwise(_f_hgrn_prep, T, [Row(u, 512, C_HF), Row(u, 512, C_HI)], [(w["lb"], 512)],
                          [(512, F32)] * 3, n("hgrn_prep"))
    o_h, states = _hgrn_fwd(u, kk, lf, vv, T, n("hgrn"))
    oh = _rowwise(_f_hgrn_out, T, [Row(o_h, piece=HG_DV), Row(u, 512, C_HG, piece=HG_DV)], [(w["hgrn_norm_g"], HG_DV)],
                  [(512, BF16)], n("hgrn_out_norm"))[0]
    y_c = _mm(oh, w["w_hgrn_out"], out_dtype=BF16, name=n("hgrn_out"))
    sv.update(kk=kk, lf=lf, vv=vv, o_h=o_h, states=states, oh=oh, y_c=y_c)
    gate_rows = [Row(u, D_MODEL, C_GATE + g * D_MODEL) for g in range(3)]
    mix = _rowwise(_f_mix, T, gate_rows + [Row(y_a), Row(y_b), Row(y_c)], [], [(D_MODEL, BF16)], n("mix"))[0]
    x1 = _mm(mix, w["w_out"], res=x, name=n("out_proj"))
    h2, sv["h2_t"] = _rowwise(_f_rms, T, [Row(x1)], [(w["norm2_g"], D_MODEL)], [(D_MODEL, BF16)], n("norm2"), True)
    f = _mm(h2, w["w_ff1"], out_dtype=BF16, name=n("ff1"))
    x2 = _mm(f, w["w_ff2"], res=x1, a_fn=_relu2, name=n("ff2"))
    sv.update(mix=mix, x1=x1, h2=h2, f=f)
    return x2, sv


def _layer_bwd(dx2, w, sv, tabs, T, l, mid=None, matrices=None):
    c, s1, s2 = tabs
    n = lambda s: f"l{l}_b_{s}"
    u = sv["u"]
    g = {}
    g["w_ff2"] = _mm(sv["f"], dx2, ta=True, a_fn=_relu2, out_dtype=BF16, name=n("dw_ff2"))
    df = _mm(dx2, w["w_ff2"], tb=True, out_dtype=BF16, name=n("d_f"),
             epi=(sv["f"], lambda d, fv: d * (2.0 * jnp.maximum(fv, 0.0))))
    g["w_ff1"] = _mm(sv["h2_t"], df, out_dtype=BF16, name=n("dw_ff1"))
    dh2 = _mm(df, w["w_ff1"], tb=True, name=n("d_h2"))
    (dx1,), (g["norm2_g"],) = _rowwise_bwd(_f_rms, T, [Row(sv["x1"])], [(w["norm2_g"], D_MODEL)], [Row(dh2)],
                                           {0: F32}, n("norm2"), add=(0, dx2))
    g["w_out"] = _mm(sv["mix"], dx1, ta=True, out_dtype=BF16, name=n("dw_out"))
    w_out = w["w_out"] if mid is None else mid(g, w["w_out"])
    dmix = _mm(dx1, w_out, tb=True, out_dtype=BF16, name=n("d_mix"))
    gate_rows = [Row(u, D_MODEL, C_GATE + i * D_MODEL) for i in range(3)]
    (dg0, dg1, dg2, dy_a, dy_b, dy_c), _ = _rowwise_bwd(
        _f_mix, T, gate_rows + [Row(sv["y_a"]), Row(sv["y_b"]), Row(sv["y_c"])], [], [Row(dmix)],
        {0: BF16, 1: BF16, 2: BF16, 3: BF16, 4: BF16, 5: BF16}, n("mix"))
    g["w_hgrn_out"] = _mm(sv["oh"], dy_c, ta=True, out_dtype=BF16, name=n("dw_hgrn_out"))
    doh = _mm(dy_c, w["w_hgrn_out"], tb=True, out_dtype=BF16, name=n("d_oh"))
    (do_h, dhg), (g["hgrn_norm_g"],) = _rowwise_bwd(
        _f_hgrn_out, T, [Row(sv["o_h"], piece=HG_DV), Row(u, 512, C_HG, piece=HG_DV)], [(w["hgrn_norm_g"], HG_DV)],
        [Row(doh, piece=HG_DV)], {0: F32, 1: BF16}, n("hgrn_out_norm"))
    dhq, dkk, dlf, dvv = _hgrn_bwd(u, sv["kk"], sv["lf"], sv["vv"], sv["states"], do_h, T, n("hgrn"))
    (dhf, dhi), (g["lb"],) = _rowwise_bwd(
        _f_hgrn_prep, T, [Row(u, 512, C_HF), Row(u, 512, C_HI)], [(w["lb"], 512)],
        [Row(dkk), Row(dlf), Row(dvv)], {0: BF16, 1: BF16}, n("hgrn_prep"))
    g["w_attn_out"] = _mm(sv["o"], dy_b, ta=True, out_dtype=BF16, name=n("dw_attn_out"))
    do = _mm(dy_b, w["w_attn_out"], tb=True, out_dtype=BF16, name=n("d_o"))
    dq, dk, dv = _attn_bwd(sv["q"], sv["k"], sv["v"], sv["o"], sv["lse"], do, T, n("attn"))
    tab_rows = [Row(c), Row(s1), Row(s2)]
    (dq_raw,), (g["q_norm_g"],) = _rowwise_bwd(
        _f_qrope, T, [Row(sv["q_raw"], piece=HEAD_W)] + tab_rows, [(w["q_norm_g"], HEAD_W)],
        [Row(dq, piece=HEAD_W)], {0: BF16}, n("q_rope"))
    (dk_raw, dkr), (g["k_norm_g"],) = _rowwise_bwd(
        _f_krope, T, [Row(sv["k_raw"], piece=HEAD_W), Row(u, HEAD_W, C_KR)] + tab_rows, [(w["k_norm_g"], HEAD_W)],
        [Row(dk, piece=HEAD_W)], {0: BF16, 1: BF16}, n("k_rope"))
    g["w_uq"] = _mm(sv["cqn"], dq_raw, ta=True, out_dtype=BF16, name=n("dw_uq"))
    g["w_uk"] = _mm(sv["ckvn"], dk_raw, ta=True, out_dtype=BF16, name=n("dw_uk"))
    g["w_uv"] = _mm(sv["ckvn"], dv, ta=True, out_dtype=BF16, name=n("dw_uv"))
    dcqn = _mm(dq_raw, w["w_uq"], tb=True, out_dtype=BF16, name=n("d_cqn"))
    dckvn = _mm(dk_raw, w["w_uk"], tb=True, name=n("d_ckvn_k"))
    dckvn = _mm(dv, w["w_uv"], tb=True, res=dckvn, out_dtype=BF16, name=n("d_ckvn_v"))
    (dcq,), (g["q_a_norm_g"],) = _rowwise_bwd(_f_rms, T, [Row(u, Q_RANK, C_CQ)], [(w["q_a_norm_g"], Q_RANK)],
                                              [Row(dcqn)], {0: BF16}, n("q_a_norm"))
    (dckv,), (g["kv_a_norm_g"],) = _rowwise_bwd(_f_rms, T, [Row(u, KV_RANK, C_CKV)], [(w["kv_a_norm_g"], KV_RANK)],
                                                [Row(dckvn)], {0: BF16}, n("kv_a_norm"))
    g["w_conv_out"] = _mm(sv["hc"], dy_a, ta=True, out_dtype=BF16, name=n("dw_conv_out"))
    dhc = _mm(dy_a, w["w_conv_out"], tb=True, out_dtype=BF16, name=n("d_hc"))
    (dcv,), (g["conv_ln_g"], g["conv_ln_b"]) = _rowwise_bwd(
        _f_lnsilu, T, [Row(sv["cv"])], [(w["conv_ln_g"], CONV_DIM), (w["conv_ln_b"], CONV_DIM)], [Row(dhc)],
        {0: F32}, n("conv_ln"))
    dhglu, dconv_w, g["conv_b"] = _conv_bwd(sv["hglu"], w["conv_w"], dcv, T, n("conv"))
    g["conv_w"] = dconv_w[:CONV_K]
    (dua, dug), _ = _rowwise_bwd(_f_glu, T, [Row(u, 512, C_CONV_A), Row(u, 512, C_CONV_G)], [], [Row(dhglu)],
                                 {0: BF16, 1: BF16}, n("glu"))
    du = jnp.concatenate([dua, dug, dg0, dg1, dg2, dcq, dckv, dkr, dhq.astype(BF16), dhf, dhi, dhg], axis=1)
    small = ("w_uq", "w_uk", "w_uv", "w_attn_out", "w_hgrn_out", "w_conv_out")
    du, *done = lax.optimization_barrier((du, *[g[k] for k in small]))
    g.update(zip(small, done))
    g["w_in"] = _mm(sv["h_t"], du, out_dtype=BF16, name=n("dw_in"))
    norm_g = w["norm1_g"] if matrices is None else matrices(g, w["norm1_g"])
    du, norm_g = lax.optimization_barrier((du, norm_g))
    dh = _mm(du, w["w_in"], tb=True, name=n("d_h"))
    (dx,), (g["norm1_g"],) = _rowwise_bwd(_f_rms, T, [Row(sv["x"])], [(norm_g, D_MODEL)], [Row(dh)],
                                          {0: F32}, n("norm1"), add=(0, dx1))
    return dx, g


def _pad_w_in(w_in):
    z = lambda k: jnp.zeros((w_in.shape[0], k), w_in.dtype)
    return jnp.concatenate([w_in[:, :O_CQ], w_in[:, O_GATE:], w_in[:, O_CQ:O_KR], z(KR_LANE), w_in[:, O_KR:O_HQ],
                            z(HEAD_W - KR_LANE - ROPE), w_in[:, O_HQ:O_GATE]], axis=1)


def _unpad_w_in(g):
    return jnp.concatenate([g[:, :C_GATE], g[:, C_CQ:C_KR], g[:, C_KR + KR_LANE:C_KR + KR_LANE + ROPE],
                            g[:, C_HQ:], g[:, C_GATE:C_CQ]], axis=1)


_W_IN_RUNS = ((0, 0, O_CQ), (O_CQ, C_CQ, O_KR - O_CQ), (O_KR, C_KR + KR_LANE, ROPE), (O_HQ, C_HQ, O_GATE - O_HQ),
              (O_GATE, C_GATE, N_IN - O_GATE))


def _w_in_from_shards(g8):
    per = N_IN // N_DEV
    pieces, at = [], 0
    for o0, p0, n in sorted(_W_IN_RUNS, key=lambda r: r[1]):
        if p0 > at:
            pieces.append(jnp.zeros((g8.shape[1], p0 - at), g8.dtype))
        for j in range(o0 // per, (o0 + n - 1) // per + 1):
            lo, hi = max(o0, j * per), min(o0 + n, (j + 1) * per)
            pieces.append(g8[j][:, lo - j * per:hi - j * per])
        at = p0 + n
    if at < N_IN_P:
        pieces.append(jnp.zeros((g8.shape[1], N_IN_P - at), g8.dtype))
    return jnp.concatenate(pieces, axis=1)


def _w_in_grad_shards(g):
    per = N_IN // N_DEV
    shards = []
    for j in range(N_DEV):
        lo, hi = j * per, (j + 1) * per
        pieces = [g[:, p0 + max(lo, o0) - o0:p0 + min(hi, o0 + n) - o0]
                  for o0, p0, n in _W_IN_RUNS if max(lo, o0) < min(hi, o0 + n)]
        shards.append(jnp.concatenate(pieces, axis=1) if len(pieces) > 1 else pieces[0])
    return jnp.stack(shards)


def _pad_heads(wm, per_head, lo, hi):
    lead = wm.shape[:-1]
    wh = wm.reshape(lead + (HEADS, per_head))[..., lo:hi]
    pad = [(0, 0)] * len(lead) + [(0, 0), (0, HEAD_W - (hi - lo))]
    return jnp.pad(wh, pad).reshape(lead + (HEADS * HEAD_W,))


def _unpad_heads(gm, width):
    lead = gm.shape[:-1]
    return gm.reshape(lead + (HEADS, HEAD_W))[..., :width]


def _layer_weights(full, lb):
    w = {}
    w["norm1_g"] = full["norm1_g"]
    w["w_in"] = full["w_in_padded"] if "w_in_padded" in full else _pad_w_in(full["w_in"])
    w["conv_w"] = full["conv_w"]
    w["conv_b"] = full["conv_b"]
    w["conv_ln_g"] = full["conv_ln_g"]
    w["conv_ln_b"] = full["conv_ln_b"]
    w["w_conv_out"] = full["w_conv_out"]
    w["q_a_norm_g"] = full["q_a_norm_g"]
    w["w_uq"] = _pad_heads(full["w_uq"], QK_DIM, 0, QK_DIM)
    w["kv_a_norm_g"] = full["kv_a_norm_g"]
    w["w_uk"] = _pad_heads(full["w_ukv"], NOPE + V_DIM, 0, NOPE)
    w["w_uv"] = _pad_heads(full["w_ukv"], NOPE + V_DIM, NOPE, NOPE + V_DIM)
    w["q_norm_g"] = jnp.pad(full["q_norm_g"], (0, HEAD_W - QK_DIM))
    w["k_norm_g"] = jnp.pad(full["k_norm_g"], (0, HEAD_W - QK_DIM))
    wa = full["w_attn_out"].reshape(HEADS, V_DIM, D_MODEL)
    w["w_attn_out"] = jnp.pad(wa, ((0, 0), (0, HEAD_W - V_DIM), (0, 0))).reshape(HEADS * HEAD_W, D_MODEL)
    w["lb"] = lb
    w["hgrn_norm_g"] = full["hgrn_norm_g"]
    w["w_hgrn_out"] = full["w_hgrn_out"]
    w["w_out"] = full["w_out"]
    w["norm2_g"] = full["norm2_g"]
    w["w_ff1"] = full["w_ff1"]
    w["w_ff2"] = full["w_ff2"]
    return w


def _matrix_grads_to_original(g):
    o = {name: g[name] for name in ("w_conv_out", "w_hgrn_out", "w_out", "w_ff1", "w_ff2")}
    o["w_in"] = _unpad_w_in(g["w_in"])
    o["w_in_shards"] = _w_in_grad_shards(g["w_in"])
    o["w_uq"] = _unpad_heads(g["w_uq"], QK_DIM).reshape(Q_RANK, HEADS * QK_DIM)
    guk = _unpad_heads(g["w_uk"], NOPE)
    guv = _unpad_heads(g["w_uv"], V_DIM)
    o["w_ukv"] = jnp.concatenate([guk, guv], axis=-1).reshape(KV_RANK, HEADS * (NOPE + V_DIM))
    o["w_attn_out"] = g["w_attn_out"].reshape(HEADS, HEAD_W, D_MODEL)[:, :V_DIM].reshape(HEADS * V_DIM, D_MODEL)
    return o


def _vector_grads_to_original(g):
    o = {"conv_w": g["conv_w"]}
    for name in ("norm1_g", "conv_b", "conv_ln_g", "conv_ln_b", "q_a_norm_g", "kv_a_norm_g", "hgrn_norm_g", "norm2_g", "lb"):
        o[name] = g[name].reshape(-1)
    o["q_norm_g"] = g["q_norm_g"].reshape(-1)[:QK_DIM]
    o["k_norm_g"] = g["k_norm_g"].reshape(-1)[:QK_DIM]
    return o


def _lower_bounds(logits):
    p = jax.nn.softmax(logits.astype(F32), axis=0)
    return jnp.cumsum(p, axis=0) - p[0:1]


def _run_step(x, target, meta, lb_logits, layer_weights, layer_done, layer_mid=None, layer_matrices=None):
    seq = x.shape[0]
    T = ROW0 + seq
    assert T % 128 == 0
    tabs = _rope_tables(T)
    lbs, lb_vjp = jax.vjp(_lower_bounds, lb_logits)
    xp = jnp.concatenate([jnp.zeros((PAD_FRONT, D_MODEL), F32), meta.astype(F32), x], axis=0)
    tp = jnp.concatenate([jnp.zeros((ROW0, D_MODEL), F32), target], axis=0)
    ws, svs = [], []
    for l in range(DEPTH):
        full, xp = layer_weights(l, xp)
        w = _layer_weights(full, lbs[l])
        xp, sv = _layer_fwd(xp, w, tabs, T, l)
        ws.append(w)
        svs.append(sv)
    dx, sq = _loss_head(xp, tp, T)
    loss = 0.5 * jnp.sum(sq) * (1.0 / D_MODEL)
    dlb = [None] * DEPTH
    for l in reversed(range(DEPTH)):
        mid = None if layer_mid is None else functools.partial(layer_mid, l)
        mats = {}

        def matrices(g, norm_g, l=l, mats=mats):
            mats.update(_matrix_grads_to_original(g))
            return norm_g if layer_matrices is None else layer_matrices(l, mats, norm_g)

        dx, g = _layer_bwd(dx, ws[l], svs[l], tabs, T, l, mid, matrices)
        g = {**_vector_grads_to_original(g), **mats}
        dlb[l] = g.pop("lb")
        dx = layer_done(l, g, dx)
    return loss, dx[ROW0:], dx[PAD_FRONT:ROW0], lb_vjp(jnp.stack(dlb))[0]


def _local_step(x, target, full):
    per_layer = [None] * DEPTH

    def done(l, g, dx):
        per_layer[l] = g
        return dx

    loss, gx, gmeta, glb = _run_step(
        x, target, full["meta"], full["hgrn_lb_logits"],
        lambda l, xp: ({k: v[l] for k, v in full.items() if k != "meta"}, xp), done)
    grads = {k: jnp.stack([per_layer[l][k] for l in range(DEPTH)]) for k in per_layer[0]}
    grads["hgrn_lb_logits"] = glb
    grads["meta"] = gmeta
    return loss, gx, grads


def _mesh_pos():
    return lax.axis_index("x"), lax.axis_index("y"), lax.axis_index("c")


N_COPY = N_DEV - 1


def _all_gather(arrs, name):
    n = len(arrs)

    def body(*refs):
        x_refs, out_refs = refs[:n], refs[n:2 * n]
        send_sems, recv_sems, local_sems = refs[2 * n:]
        x, y, c = _mesh_pos()
        me, sibling = (x, y, c), (x, y, 1 - c)
        chips = [(1 - x, y), (x, 1 - y), (1 - x, 1 - y)]

        def slot(a, px, py, pc):
            return out_refs[a].at[4 * px + 2 * py + pc]

        def copy(a, k, block, to, own=False):
            return pltpu.make_async_remote_copy(
                src_ref=x_refs[a] if own else slot(a, *block), dst_ref=slot(a, *block),
                send_sem=send_sems.at[a * N_COPY + k], recv_sem=recv_sems.at[a * N_COPY + k],
                device_id=to, device_id_type=MESH)

        mine = [pltpu.make_async_copy(x_refs[a], slot(a, *me), local_sems.at[a]) for a in range(n)]
        for cp in mine:
            cp.start()
        first = []
        for a in range(n):
            first.append(copy(a, 0, me, sibling, own=True))
            first += [copy(a, 1 + j, me, (*chip, c), own=True) for j, chip in enumerate(chips)]
        for cp in first:
            cp.start()
        passed = []
        for j, chip in enumerate(chips):
            for a in range(n):
                copy(a, 1 + j, (*chip, c), me).wait_recv()
                cp = copy(a, 4 + j, (*chip, c), sibling)
                cp.start()
                passed.append(cp)
        for a in range(n):
            copy(a, 0, sibling, me).wait_recv()
            for j, chip in enumerate(chips):
                copy(a, 4 + j, (*chip, 1 - c), me).wait_recv()
        for cp in first + passed:
            cp.wait_send()
        for cp in mine:
            cp.wait()

    anyspec = pl.BlockSpec(memory_space=pl.ANY)
    return pl.pallas_call(
        body, name=name, out_shape=[jax.ShapeDtypeStruct((N_DEV,) + a.shape, a.dtype) for a in arrs],
        in_specs=[anyspec] * n, out_specs=[anyspec] * n,
        scratch_shapes=[pltpu.SemaphoreType.DMA((n * N_COPY,)), pltpu.SemaphoreType.DMA((n * N_COPY,)),
                        pltpu.SemaphoreType.DMA((n,))],
    )(*arrs)


def _exchange(arrs, name):
    n = len(arrs)

    def body(*refs):
        s_refs, r_refs = refs[:n], refs[n:2 * n]
        send_sems, recv_sems, local_sems = refs[2 * n:]
        x, y, c = _mesh_pos()
        me = 4 * x + 2 * y + c
        local = [pltpu.make_async_copy(s_refs[a].at[me], r_refs[a].at[me], local_sems.at[a]) for a in range(n)]
        for cp in local:
            cp.start()
        sends, recvs = [], []
        for rel in range(1, N_DEV):
            px = 1 - x if rel & 4 else x
            py = 1 - y if rel & 2 else y
            pc = 1 - c if rel & 1 else c
            p = 4 * px + 2 * py + pc
            for a in range(n):
                k = a * N_COPY + rel - 1
                sends.append(pltpu.make_async_remote_copy(
                    src_ref=s_refs[a].at[p], dst_ref=r_refs[a].at[me], send_sem=send_sems.at[k],
                    recv_sem=recv_sems.at[k], device_id=(px, py, pc), device_id_type=MESH))
                recvs.append(pltpu.make_async_remote_copy(
                    src_ref=s_refs[a].at[me], dst_ref=r_refs[a].at[p], send_sem=send_sems.at[k],
                    recv_sem=recv_sems.at[k], device_id=(px, py, pc), device_id_type=MESH))
        for cp in sends:
            cp.start()
        for cp in recvs:
            cp.wait_recv()
        for cp in sends:
            cp.wait_send()
        for cp in local:
            cp.wait()

    anyspec = pl.BlockSpec(memory_space=pl.ANY)
    return pl.pallas_call(
        body, name=name, out_shape=[jax.ShapeDtypeStruct(a.shape, a.dtype) for a in arrs],
        in_specs=[anyspec] * n, out_specs=[anyspec] * n,
        scratch_shapes=[pltpu.SemaphoreType.DMA((n * N_COPY,)), pltpu.SemaphoreType.DMA((n * N_COPY,)),
                        pltpu.SemaphoreType.DMA((n,))],
    )(*arrs)


_HBM = pl.BlockSpec(memory_space=pltpu.HBM)
_SEM = pl.BlockSpec(memory_space=pltpu.SEMAPHORE)
_EFFECT = pltpu.SideEffectType.DATAFLOW_SIDE_EFFECTING


def _peers(x, y, c):
    out = []
    for rel in range(1, N_DEV):
        px = 1 - x if rel & 4 else x
        py = 1 - y if rel & 2 else y
        pc = 1 - c if rel & 1 else c
        out.append((rel, (px, py, pc), 4 * px + 2 * py + pc))
    return out


def _split_copies(src_refs, land_refs, send_sems, recv_sems, gather):
    x, y, c = _mesh_pos()
    me = 4 * x + 2 * y + c
    out = []
    for a, (src, land) in enumerate(zip(src_refs, land_refs)):
        for rel, peer, p in _peers(x, y, c):
            k = a * N_COPY + rel - 1
            mk = lambda s, d: pltpu.make_async_remote_copy(
                src_ref=s, dst_ref=d, send_sem=send_sems.at[k], recv_sem=recv_sems.at[k],
                device_id=peer, device_id_type=MESH)
            mine = src if gather else src.at[p]
            out.append((mk(mine, land.at[me]), mk(mine, land.at[p])))
    return out


def _copy_start(srcs, gather, name, collective_id):
    n = len(srcs)
    lands = [lax.empty(((N_DEV,) + s.shape) if gather else s.shape, s.dtype) for s in srcs]

    def body(*refs):
        src_refs, land_refs = refs[:n], refs[n:2 * n]
        send_sems, recv_sems = refs[2 * n], refs[2 * n + 1]
        token = refs[-1]
        x, y, c = _mesh_pos()
        barrier = pltpu.get_barrier_semaphore()
        for _, peer, _ in _peers(x, y, c):
            pl.semaphore_signal(barrier, inc=1, device_id=peer, device_id_type=MESH)
        pl.semaphore_wait(barrier, N_COPY)
        for out_copy, _ in _split_copies(src_refs, land_refs, send_sems, recv_sems, gather):
            out_copy.start()
        token[...] = jnp.zeros_like(token)

    hbm = lambda a: pltpu.HBM(a.shape, a.dtype)
    res = pl.pallas_call(
        body, name=name,
        out_shape=(pltpu.SemaphoreType.DMA((n * N_COPY,)), pltpu.SemaphoreType.DMA((n * N_COPY,)),
                   *[hbm(s) for s in srcs], *[hbm(z) for z in lands], jax.ShapeDtypeStruct((8, 128), F32)),
        in_specs=[_HBM] * (2 * n), out_specs=(_SEM, _SEM, *([_HBM] * (2 * n)), pl.BlockSpec(memory_space=pltpu.VMEM)),
        input_output_aliases={i: 2 + i for i in range(2 * n)},
        compiler_params=pltpu.CompilerParams(has_side_effects=_EFFECT, collective_id=collective_id),
    )(*[pltpu.with_memory_space_constraint(s, pltpu.HBM) for s in srcs],
      *[pltpu.with_memory_space_constraint(z, pltpu.HBM) for z in lands])
    return res[0], res[1], list(res[2:2 + n]), list(res[2 + n:2 + 2 * n]), res[-1]


def _after(a, token):
    return a + token[0, 0].astype(a.dtype)


def _copy_wait(send_sems, recv_sems, srcs, lands, after, gather, name):
    n = len(srcs)

    def body(*refs):
        src_refs, land_refs = refs[:n], refs[n:2 * n]
        s_sems, r_sems = refs[2 * n], refs[2 * n + 1]
        for out_copy, in_copy in _split_copies(src_refs, land_refs, s_sems, r_sems, gather):
            out_copy.wait_send()
            in_copy.wait_recv()

    hbm = lambda a: pltpu.HBM(a.shape, a.dtype)
    res = pl.pallas_call(
        body, name=name, out_shape=(*[hbm(s) for s in srcs], *[hbm(z) for z in lands]),
        in_specs=[_HBM] * (2 * n) + [_SEM, _SEM, pl.BlockSpec(memory_space=pl.ANY)], out_specs=tuple([_HBM] * (2 * n)),
        input_output_aliases={i: i for i in range(2 * n)},
        compiler_params=pltpu.CompilerParams(has_side_effects=_EFFECT),
    )(*srcs, *lands, send_sems, recv_sems, after)
    return list(res[:n]), list(res[n:])


def _sum_parts(parts, name):
    P, R, W = parts.shape

    def body(p_ref, o_ref):
        g = p_ref[0].astype(F32)
        for i in range(1, P):
            g = g + p_ref[i].astype(F32)
        o_ref[...] = g

    return pl.pallas_call(body, name=name, out_shape=jax.ShapeDtypeStruct((R, W), F32))(parts)


def _adamw_body(p_ref, w_ref, m_ref, v_ref, g_ref, d_ref, nm_ref, nv_ref):
    g = p_ref[0].astype(F32)
    for i in range(1, p_ref.shape[0]):
        g = g + p_ref[i].astype(F32)
    _adamw_apply(g, w_ref, m_ref, v_ref, g_ref, d_ref, nm_ref, nv_ref)


def _adamw_apply(g, w_ref, m_ref, v_ref, g_ref, d_ref, nm_ref, nv_ref):
    m_new = ADAM_B1 * m_ref[...] + (1.0 - ADAM_B1) * g
    v_new = ADAM_B2 * v_ref[...] + (1.0 - ADAM_B2) * jnp.square(g)
    m_hat = m_new / (1.0 - ADAM_B1 ** ADAM_STEP)
    v_hat = v_new / (1.0 - ADAM_B2 ** ADAM_STEP)
    g_ref[...] = g
    d_ref[...] = -ADAM_LR * (m_hat / (jnp.sqrt(v_hat) + ADAM_EPS) + ADAM_WD * w_ref[...])
    nm_ref[...] = m_new
    nv_ref[...] = v_new


def _adamw(parts, w, m, v, name):
    P, R, W = parts.shape
    tr = _pick(R, (368, 192, 64, 16, 8))
    spec = pl.BlockSpec((tr, W), lambda i: (i, 0))
    return pl.pallas_call(
        functools.partial(_adamw_body), name=name, grid=(R // tr,),
        in_specs=[pl.BlockSpec((P, tr, W), lambda i: (0, i, 0)), spec, spec, spec], out_specs=[spec] * 4,
        out_shape=[jax.ShapeDtypeStruct((R, W), F32)] * 4,
        compiler_params=_cparams(("parallel",)),
    )(parts, w, m, v)


def _adamw_layers(parts, w, m, v, name):
    P, B, C_ = parts[0].shape
    tb = _pick(B, (256, 128))
    nb = B // tb

    def body(*refs):
        p_refs, rest = refs[:DEPTH], refs[DEPTH:]
        a = pl.program_id(0)
        for l in range(DEPTH):
            @pl.when(a == l)
            def _():
                _adamw_body(p_refs[l], *[r.at[0] for r in rest])

    spec = pl.BlockSpec((1, tb, C_), lambda a, i: (a, i, 0))

    def part_spec(l):
        return pl.BlockSpec((P, tb, C_), lambda a, i: (0, jnp.where(a == l, i, jnp.where(a < l, 0, nb - 1)), 0))

    return pl.pallas_call(
        body, name=name, grid=(DEPTH, nb),
        in_specs=[part_spec(l) for l in range(DEPTH)] + [spec, spec, spec], out_specs=[spec] * 4,
        out_shape=[jax.ShapeDtypeStruct((DEPTH, B, C_), F32)] * 4,
        compiler_params=_cparams(("arbitrary", "arbitrary")),
    )(*parts, w, m, v)


VEC_GROUPS = (("norm1_g", "norm2_g"), ("conv_b", "conv_ln_g", "conv_ln_b", "hgrn_lb_logits", "hgrn_norm_g"),
              ("q_a_norm_g",), ("kv_a_norm_g",), ("q_norm_g", "k_norm_g"))
SMALL_NAMES = tuple(n for grp in VEC_GROUPS for n in grp) + ("meta", "conv_w")


def _adamw_small(own, lands, wts, mom, var, name):
    n_in = len(own)

    def body(*refs):
        own_r, land_r = refs[:n_in], refs[n_in:2 * n_in]
        rest = iter(refs[2 * n_in:])
        wmv = {n: (next(rest), next(rest), next(rest)) for n in SMALL_NAMES}
        outs = {n: (next(rest), next(rest), next(rest), next(rest)) for n in SMALL_NAMES}
        loss_ref = next(rest)
        x, y, c = _mesh_pos()
        me = 4 * x + 2 * y + c

        def total(k):
            acc = None
            for s in range(N_DEV):
                v = jnp.where(me == s, own_r[k][...], land_r[k][s])
                acc = v if acc is None else acc + v
            return acc

        for k, grp in enumerate(VEC_GROUPS):
            tot = total(k)
            for j, n in enumerate(grp):
                _adamw_apply(tot[DEPTH * j:DEPTH * (j + 1)], *wmv[n], *outs[n])
        loss_ref[...] = total(len(VEC_GROUPS))
        _adamw_apply(total(n_in - 2), *wmv["meta"], *outs["meta"])
        _adamw_apply(total(n_in - 1), *wmv["conv_w"], *outs["conv_w"])

    args = list(own) + list(lands) + [d[n] for n in SMALL_NAMES for d in (wts, mom, var)]
    out_shape = [jax.ShapeDtypeStruct(wts[n].shape, F32) for n in SMALL_NAMES for _ in range(4)]
    res = pl.pallas_call(body, name=name, out_shape=out_shape + [jax.ShapeDtypeStruct((1, 128), F32)])(*args)
    out = {}
    for i, n in enumerate(SMALL_NAMES):
        for j, kind in enumerate(("grad_", "delta_", "new_m_", "new_v_")):
            out[kind + n] = res[4 * i + j]
    return out, res[-1]


PACK_W = 1024
BIG = (("w_in", (DEPTH, D_MODEL, N_IN // N_DEV), 2), ("w_conv_out", (DEPTH, CONV_DIM, D_MODEL // N_DEV), 2),
       ("w_uq", (DEPTH, Q_RANK, HEADS * QK_DIM // N_DEV), 2), ("w_ukv", (DEPTH, KV_RANK, HEADS * (NOPE + V_DIM) // N_DEV), 2),
       ("w_attn_out", (DEPTH, HEADS * V_DIM, D_MODEL // N_DEV), 2), ("w_hgrn_out", (DEPTH, 512, D_MODEL // N_DEV), 2),
       ("w_out", (DEPTH, D_MODEL // N_DEV, D_MODEL), 1), ("w_ff1", (DEPTH, D_MODEL, D_FF // N_DEV), 2),
       ("w_ff2", (DEPTH, D_FF // N_DEV, D_MODEL), 1))
SMALL_SHARDED = (("meta", (N_META, D_MODEL // N_DEV), 1), ("conv_w", (DEPTH, CONV_K, CONV_DIM // N_DEV), 2))
REPLICATED = (("norm1_g", (DEPTH, D_MODEL)), ("conv_b", (DEPTH, CONV_DIM)), ("conv_ln_g", (DEPTH, CONV_DIM)),
              ("conv_ln_b", (DEPTH, CONV_DIM)), ("q_a_norm_g", (DEPTH, Q_RANK)), ("kv_a_norm_g", (DEPTH, KV_RANK)),
              ("q_norm_g", (DEPTH, QK_DIM)), ("k_norm_g", (DEPTH, QK_DIM)), ("hgrn_lb_logits", (DEPTH, 512)),
              ("hgrn_norm_g", (DEPTH, 512)), ("norm2_g", (DEPTH, D_MODEL)))
WEIGHT_ORDER = ("meta", "norm1_g", "w_in", "conv_w", "conv_b", "conv_ln_g", "conv_ln_b", "w_conv_out", "q_a_norm_g", "w_uq",
                "kv_a_norm_g", "w_ukv", "q_norm_g", "k_norm_g", "w_attn_out", "hgrn_lb_logits", "hgrn_norm_g", "w_hgrn_out",
                "w_out", "norm2_g", "w_ff1", "w_ff2")


def _rows_for(n_elems, mult):
    rows = -(-n_elems // PACK_W)
    return -(-rows // mult) * mult


def _pack(arrays, dtype, mult, lead=()):
    nl = len(lead)
    flat = jnp.concatenate([a.reshape(lead + (-1,)).astype(dtype) for a in arrays], axis=nl)
    rows = _rows_for(flat.shape[nl], mult)
    flat = jnp.pad(flat, [(0, 0)] * nl + [(0, rows * PACK_W - flat.shape[nl])])
    return flat.reshape(lead + (rows, PACK_W))


def _unpack(pack, shapes, lead=()):
    nl = len(lead)
    flat = pack.reshape(lead + (-1,))
    out, off = [], 0
    for shp in shapes:
        n = int(np.prod(shp))
        out.append(lax.slice_in_dim(flat, off, off + n, axis=nl).reshape(lead + tuple(shp)))
        off += n
    return out


def _join_shards(g, axis):
    g = jnp.moveaxis(g, 0, axis)
    shp = g.shape
    return g.reshape(shp[:axis] + (shp[axis] * shp[axis + 1],) + shp[axis + 2:])


def _cut_shards(a, axis):
    shp = a.shape
    a = a.reshape(shp[:axis] + (N_DEV, shp[axis] // N_DEV) + shp[axis + 1:])
    return jnp.moveaxis(a, axis, 0)


def kernel(x, meta, norm1_g, w_in, conv_w, conv_b, conv_ln_g, conv_ln_b, w_conv_out, q_a_norm_g, w_uq, kv_a_norm_g, w_ukv, q_norm_g, k_norm_g, w_attn_out, hgrn_lb_logits, hgrn_norm_g, w_hgrn_out, w_out, norm2_g, w_ff1, w_ff2, loss_target, m_meta, m_norm1_g, m_w_in, m_conv_w, m_conv_b, m_conv_ln_g, m_conv_ln_b, m_w_conv_out, m_q_a_norm_g, m_w_uq, m_kv_a_norm_g, m_w_ukv, m_q_norm_g, m_k_norm_g, m_w_attn_out, m_hgrn_lb_logits, m_hgrn_norm_g, m_w_hgrn_out, m_w_out, m_norm2_g, m_w_ff1, m_w_ff2, v_meta, v_norm1_g, v_w_in, v_conv_w, v_conv_b, v_conv_ln_g, v_conv_ln_b, v_w_conv_out, v_q_a_norm_g, v_w_uq, v_kv_a_norm_g, v_w_ukv, v_q_norm_g, v_k_norm_g, v_w_attn_out, v_hgrn_lb_logits, v_hgrn_norm_g, v_w_hgrn_out, v_w_out, v_norm2_g, v_w_ff1, v_w_ff2):
    args = dict(locals())
    wts = {n: args[n] for n in WEIGHT_ORDER}
    mom = {n: args["m_" + n] for n in WEIGHT_ORDER}
    var = {n: args["v_" + n] for n in WEIGHT_ORDER}
    xi, yi, ci = _mesh_pos()
    me = 4 * xi + 2 * yi + ci

    shard = lambda l: [wts[n][l].astype(BF16) for n, _, _ in BIG]
    gathered = _all_gather(shard(0) + [_pack([wts[n] for n, _, _ in SMALL_SHARDED], F32, 8)], "gather_layer0")
    small = dict(zip([n for n, _, _ in SMALL_SHARDED],
                     [_join_shards(g, axis) for (_, _, axis), g in
                      zip(SMALL_SHARDED, _unpack(gathered[-1], [s for _, s, _ in SMALL_SHARDED], (N_DEV,)))]))
    pending = _copy_start(shard(1), True, "gather_layer1_start", 5)

    def layer_weights(l, xp):
        full = {n: wts[n][l] for n, _ in REPLICATED}
        if l == 0:
            mats = gathered[:-1]
            full["norm1_g"] = _after(full["norm1_g"], pending[4])
        else:
            own, lands = _copy_wait(pending[0], pending[1], pending[2], pending[3], xp, True, "gather_layer1_wait")
            mats = [lax.dynamic_update_index_in_dim(z, s, me, 0) for z, s in zip(lands, own)]
        full["conv_w"] = small["conv_w"][l]
        for (n, _, axis), g in zip(BIG, mats):
            if n == "w_in":
                full["w_in_padded"] = _w_in_from_shards(g)
            else:
                full[n] = _join_shards(g, axis - 1)
        return full, xp

    big_names = [n for n, _, _ in BIG]
    early = [n for n in big_names if n in ("w_out", "w_ff1", "w_ff2")]
    late = [n for n in big_names if n not in early]
    cut = lambda g, names: [(g[n + "_shards"] if n + "_shards" in g else _cut_shards(g[n], axis - 1)).astype(BF16)
                            for n, _, axis in BIG if n in names]
    layer_grads = [None] * DEPTH
    flight = {}

    def layer_mid(l, g, w_out):
        if l == 0:
            flight["l0_early"] = _copy_start(cut(g, early), False, "scatter_layer0_early_start", 7)
            w_out = _after(w_out, flight["l0_early"][4])
        return w_out

    def layer_done(l, g, dx):
        layer_grads[l] = g
        if l == 1:
            flight["l1"] = _copy_start(cut(g, big_names), False, "scatter_l1_start", 6)
            dx = _after(dx, flight["l1"][4])
        return dx

    def layer_matrices(l, mats, norm_g):
        if l == 0:
            flight["l0_late"] = _copy_start(cut(mats, late), False, "scatter_l0_late_start", 8)
            norm_g = _after(norm_g, flight["l0_late"][4])
        return norm_g

    loss, grad_x, g_meta, g_lb = _run_step(x[0], loss_target[0], small["meta"], wts["hgrn_lb_logits"],
                                           layer_weights, layer_done, layer_mid, layer_matrices)

    grads = {k: jnp.stack([layer_grads[l][k] for l in range(DEPTH)]) for k in layer_grads[0]
             if k not in big_names and not k.endswith("_shards")}
    grads["hgrn_lb_logits"] = g_lb
    own = [jnp.concatenate([grads[n] for n in grp], axis=0) for grp in VEC_GROUPS]
    own.append(jnp.broadcast_to(loss.reshape(1, 1), (1, 128)))
    flight["small"] = _copy_start(own, True, "gather_small_grads_start", 9)
    cuts = [_cut_shards(g_meta, 1), _cut_shards(grads["conv_w"], 2)]
    flight["small_x"] = _copy_start(cuts, False, "scatter_small_grads_start", 10)
    started = flight["small_x"][4]

    def arrive(key, names, after):
        s_sems, r_sems, sent, lands, _ = flight[key]
        sent, lands = _copy_wait(s_sems, r_sems, sent, lands, after, False, f"scatter_{key}_wait")
        return {n: lax.dynamic_update_index_in_dim(z, lax.dynamic_index_in_dim(s, me, 0, keepdims=False), me, 0)
                for n, z, s in zip(names, lands, sent)}

    out = {}

    def update(names, recv0, recv1):
        for n in names:
            res4 = _adamw_layers([recv0[n], recv1[n]], wts[n], mom[n], var[n], "adamw_" + n)
            for kind, a in zip(("grad_", "delta_", "new_m_", "new_v_"), res4):
                out[kind + n] = a

    recv1 = arrive("l1", big_names, started)
    recv0 = arrive("l0_early", early, started)
    update(early, recv0, recv1)

    s_sems, r_sems, sent, lands, _ = flight["small"]
    updated = lax.optimization_barrier(tuple(out["grad_" + n] for n in early))
    own, lands = _copy_wait(s_sems, r_sems, sent, lands, updated[0], True, "gather_small_grads_wait")
    s_sems, r_sems, sent, lands_x, _ = flight["small_x"]
    sent, lands_x = _copy_wait(s_sems, r_sems, sent, lands_x, updated[0], False, "scatter_small_grads_wait")
    own += [lax.dynamic_index_in_dim(s, me, 0, keepdims=False) for s in sent]
    small_out, loss = _adamw_small(own, lands + lands_x, wts, mom, var, "adamw_small")
    out.update(small_out)
    loss = loss[0, 0]

    recv0 = arrive("l0_late", late, small_out["grad_norm1_g"])
    update(late, recv0, recv1)

    res = [loss, grad_x[None]]
    for kind in ("grad_", "delta_", "new_m_", "new_v_"):
        res += [out[kind + n] for n in WEIGHT_ORDER]
    return tuple(res)
```

```python
import functools

import numpy as np
import jax
import jax.numpy as jnp
from jax import lax
from jax.experimental import pallas as pl
from jax.experimental.pallas import tpu as pltpu

F32 = jnp.float32
BF16 = jnp.bfloat16

D_MODEL = 1024
DEPTH = 2
N_META = 16
PAD_FRONT = 112
ROW0 = PAD_FRONT + N_META
EPS = 1e-6
GATE_CLAMP = 1.0 - 1e-6
CONV_DIM = 512
CONV_K = 31
HEADS = 8
Q_RANK = 256
KV_RANK = 128
NOPE = 64
ROPE = 32
V_DIM = 64
QK_DIM = NOPE + ROPE
HEAD_W = 128
ROPE_BASE = 10000.0
HG_HEADS = 4
HG_DK = 128
HG_DV = 128
HG_CHUNK = 64
D_FF = 4096
N_IN = 6560
C_CONV_A, C_CONV_G, C_GATE, C_CQ, C_CKV, C_KR, C_HQ, C_HF, C_HI, C_HG = (
    0, 512, 1024, 4096, 4352, 4480, 4608, 5120, 5632, 6144)
N_IN_P = 6656
O_CQ, O_KR, O_HQ, O_GATE = 1024, 1408, 1440, 3488
KR_LANE = NOPE

ADAM_LR = 0.001
ADAM_B1 = 0.9
ADAM_B2 = 0.999
ADAM_EPS = 1e-08
ADAM_WD = 0.01
ADAM_STEP = 10

N_DEV = 8
VMEM_LIMIT = 56 * 1024 * 1024
MESH = pl.DeviceIdType.MESH


def _pick(n, cands):
    for c in cands:
        if n % c == 0:
            return c
    raise ValueError(f"no tile for {n}")


def _cparams(sem, **kw):
    return pltpu.CompilerParams(dimension_semantics=sem, vmem_limit_bytes=VMEM_LIMIT, **kw)


def _relu2(v):
    return jnp.square(jnp.maximum(v, 0.0))


def _mm(a, b, *, ta=False, tb=False, out_dtype=F32, res=None, a_fn=None, epi=None, name):
    M, K = (a.shape[1], a.shape[0]) if ta else a.shape
    N = b.shape[0] if tb else b.shape[1]
    assert (b.shape[1] if tb else b.shape[0]) == K, (a.shape, b.shape, ta, tb)
    tm = _pick(M, (1056, 1024, 512, 384, 256, 128, 96))
    tn = _pick(N, (1664, 1024, 512, 384, 256, 128))
    tk = _pick(K, (1664, 1056, 1024, 512, 384, 256, 128, 96) if ta else (1664, 1408, 1024, 512, 384, 256, 128))
    nk = K // tk
    dims = (((0 if ta else 1,), (1 if tb else 0,)), ((), ()))
    extras = ([res] if res is not None else []) + ([epi[0]] if epi is not None else [])

    def body(*refs):
        a_ref, b_ref = refs[0], refs[1]
        r_ref = refs[2] if res is not None else None
        e_ref = refs[2 + (res is not None)] if epi is not None else None
        o_ref = refs[2 + len(extras)]
        acc = refs[-1] if nk > 1 else None
        k = pl.program_id(2)
        av = a_ref[...]
        if a_fn is not None:
            av = a_fn(av.astype(F32))
        p = lax.dot_general(av.astype(BF16), b_ref[...].astype(BF16), dims, preferred_element_type=F32)

        def finish(total):
            if e_ref is not None:
                total = epi[1](total, e_ref[...].astype(F32))
            if r_ref is not None:
                total = total + r_ref[...].astype(F32)
            o_ref[...] = total.astype(o_ref.dtype)

        if nk == 1:
            finish(p)
        else:
            @pl.when(k == 0)
            def _():
                acc[...] = p

            @pl.when(k > 0)
            def _():
                acc[...] += p

            @pl.when(k == nk - 1)
            def _():
                finish(acc[...])

    a_spec = pl.BlockSpec((tk, tm), lambda i, j, k: (k, i)) if ta else pl.BlockSpec((tm, tk), lambda i, j, k: (i, k))
    b_spec = pl.BlockSpec((tn, tk), lambda i, j, k: (j, k)) if tb else pl.BlockSpec((tk, tn), lambda i, j, k: (k, j))
    o_spec = pl.BlockSpec((tm, tn), lambda i, j, k: (i, j))
    in_specs = [a_spec, b_spec] + [o_spec] * len(extras)
    args = (a, b) + tuple(extras)
    return pl.pallas_call(
        body, name=name, grid=(M // tm, N // tn, nk), in_specs=in_specs, out_specs=o_spec,
        out_shape=jax.ShapeDtypeStruct((M, N), out_dtype),
        scratch_shapes=[pltpu.VMEM((tm, tn), F32)] if nk > 1 else [],
        compiler_params=_cparams(("parallel", "parallel", "arbitrary")),
    )(*args)


class Row:
    def __init__(self, arr, width=None, col=0, piece=None):
        self.arr = arr
        self.width = arr.shape[1] if width is None else width
        assert col % self.width == 0
        self.blk = col // self.width
        self.piece = self.width if piece is None else piece

    def spec(self, tm):
        blk = self.blk
        return pl.BlockSpec((tm, self.width), lambda i: (i, blk))


def _split(v, piece):
    w = v.shape[-1]
    if piece == w:
        return v
    return [v[:, j * piece:(j + 1) * piece] for j in range(w // piece)]


def _store(ref, val, dtype=None):
    if isinstance(val, (list, tuple)):
        piece = val[0].shape[-1]
        for j, p in enumerate(val):
            ref[:, j * piece:(j + 1) * piece] = p.astype(ref.dtype)
    else:
        ref[...] = val.astype(ref.dtype)


def _row_tile(T):
    return _pick(T, (384, 352, 192, 128))


def _param2d(p):
    return p.reshape(1, -1).astype(F32)


def _rowwise(fn, T, rows, params, outs, name, transposed=False):
    tm = _row_tile(T)
    nr, npar = len(rows), len(params)
    par = [(_param2d(p), piece) for p, piece in params]

    def body(*refs):
        rid = pl.program_id(0) * tm + lax.broadcasted_iota(jnp.int32, (tm, 1), 0)
        rv = [_split(refs[n][...].astype(F32), rows[n].piece) for n in range(nr)]
        pv = [_split(refs[nr + n][...], par[n][1]) for n in range(npar)]
        res = fn(rid, rv, pv)
        for n, val in enumerate(res):
            _store(refs[nr + npar + n], val)
        if transposed:
            refs[-1][...] = res[0].T.astype(refs[-1].dtype)

    out_specs = [pl.BlockSpec((tm, w), lambda i: (i, 0)) for w, _ in outs]
    out_shape = [jax.ShapeDtypeStruct((T, w), dt) for w, dt in outs]
    if transposed:
        out_specs.append(pl.BlockSpec((outs[0][0], tm), lambda i: (0, i)))
        out_shape.append(jax.ShapeDtypeStruct((outs[0][0], T), outs[0][1]))
    return pl.pallas_call(
        body, name=name, grid=(T // tm,),
        in_specs=[r.spec(tm) for r in rows] + [pl.BlockSpec(p.shape, lambda i: (0, 0)) for p, _ in par],
        out_specs=out_specs, out_shape=out_shape,
        compiler_params=_cparams(("parallel",)),
    )(*[r.arr for r in rows], *[p for p, _ in par])


def _rowwise_bwd(fn, T, rows, params, cts, drow, name, add=None):
    tm = _row_tile(T)
    nr, npar, nct = len(rows), len(params), len(cts)
    par = [(_param2d(p), piece) for p, piece in params]
    didx = sorted(drow)
    has_add = add is not None

    def body(*refs):
        i = pl.program_id(0)
        rid = i * tm + lax.broadcasted_iota(jnp.int32, (tm, 1), 0)
        rv = [_split(refs[n][...].astype(F32), rows[n].piece) for n in range(nr)]
        pv = [_split(refs[nr + n][...], par[n][1]) for n in range(npar)]
        cv = [_split(refs[nr + npar + n][...].astype(F32), cts[n].piece) for n in range(nct)]
        base = nr + npar + nct + (1 if has_add else 0)
        d_refs = refs[base:base + len(didx)]
        p_refs = refs[base + len(didx):]

        def g(dvals, pvals):
            full = list(rv)
            for n, v in zip(didx, dvals):
                full[n] = v
            return fn(rid, full, pvals)

        _, vjp = jax.vjp(g, [rv[n] for n in didx], pv)
        d_rows, d_pars = vjp(cv)
        for slot, n in enumerate(didx):
            val = d_rows[slot]
            if has_add and add[0] == n:
                assert not isinstance(val, (list, tuple))
                val = val + refs[nr + npar + nct][...].astype(F32)
            _store(d_refs[slot], val)

        @pl.when(i == 0)
        def _():
            for r in p_refs:
                r[...] = jnp.zeros_like(r)

        for r, val in zip(p_refs, d_pars):
            if isinstance(val, (list, tuple)):
                piece = val[0].shape[-1]
                for j, p in enumerate(val):
                    r[:, j * piece:(j + 1) * piece] += p
            else:
                r[...] += val

    in_specs = ([r.spec(tm) for r in rows] + [pl.BlockSpec(p.shape, lambda i: (0, 0)) for p, _ in par]
                + [c.spec(tm) for c in cts])
    args = [r.arr for r in rows] + [p for p, _ in par] + [c.arr for c in cts]
    if has_add:
        in_specs.append(pl.BlockSpec((tm, rows[add[0]].width), lambda i: (i, 0)))
        args.append(add[1])
    out_specs = ([pl.BlockSpec((tm, rows[n].width), lambda i: (i, 0)) for n in didx]
                 + [pl.BlockSpec(p.shape, lambda i: (0, 0)) for p, _ in par])
    out_shape = ([jax.ShapeDtypeStruct((T, rows[n].width), drow[n]) for n in didx]
                 + [jax.ShapeDtypeStruct(p.shape, F32) for p, _ in par])
    res = pl.pallas_call(
        body, name=name, grid=(T // tm,), in_specs=in_specs, out_specs=out_specs, out_shape=out_shape,
        compiler_params=_cparams(("arbitrary",)),
    )(*args)
    return list(res[:len(didx)]), list(res[len(didx):])


def _f_rms(rid, rv, pv):
    x, g = rv[0], pv[0]
    return [x * lax.rsqrt(jnp.mean(x * x, axis=-1, keepdims=True) + EPS) * g]


def _f_glu(rid, rv, pv):
    a, gt = rv
    return [a * jax.nn.sigmoid(gt) * (rid >= PAD_FRONT).astype(F32)]


def _f_lnsilu(rid, rv, pv):
    x = rv[0]
    g, b = pv
    mu = jnp.mean(x, axis=-1, keepdims=True)
    xc = x - mu
    y = xc * lax.rsqrt(jnp.mean(xc * xc, axis=-1, keepdims=True) + EPS) * g + b
    return [y * jax.nn.sigmoid(y)]


@functools.partial(jax.custom_vjp, nondiff_argnums=(1,))
def _lane_roll(x, shift):
    return pltpu.roll(x, shift, 1)


def _lane_roll_fwd(x, shift):
    return pltpu.roll(x, shift, 1), None


def _lane_roll_bwd(shift, _, g):
    return (pltpu.roll(g, (HEAD_W - shift) % HEAD_W, 1),)


_lane_roll.defvjp(_lane_roll_fwd, _lane_roll_bwd)


def _head_norm_rope(xh, g, c, s1, s2):
    y = xh * lax.rsqrt(jnp.sum(xh * xh, axis=-1, keepdims=True) * (1.0 / QK_DIM) + EPS) * g
    half = ROPE // 2
    return y * c + _lane_roll(y, HEAD_W - half) * s1 + _lane_roll(y, half) * s2


def _f_qrope(rid, rv, pv):
    q, c, s1, s2 = rv
    return [[_head_norm_rope(qh, pv[0], c, s1, s2) * ATT_SCALE for qh in q]]


def _f_krope(rid, rv, pv):
    k, kr, c, s1, s2 = rv
    return [[_head_norm_rope(kh + kr, pv[0], c, s1, s2) for kh in k]]


def _f_hgrn_prep(rid, rv, pv):
    hf, hi = rv
    m = (rid >= PAD_FRONT).astype(F32)
    kk = (1.0 - pv[0]) * jax.nn.sigmoid(-hf) * m
    lf = jnp.log1p(-jnp.minimum(kk, GATE_CLAMP))
    vv = hi * jax.nn.sigmoid(hi) * m
    return [kk, lf, vv]


def _f_hgrn_out(rid, rv, pv):
    o, hg = rv
    ng = pv[0]
    out = []
    for oh, gh, nh in zip(o, hg, ng):
        y = oh * lax.rsqrt(jnp.mean(oh * oh, axis=-1, keepdims=True) + EPS) * nh
        out.append(y * (gh * jax.nn.sigmoid(gh)))
    return [out]


def _f_mix(rid, rv, pv):
    g0, g1, g2, ya, yb, yc = rv
    return [jax.nn.sigmoid(g0) * ya + jax.nn.sigmoid(g1) * yb + jax.nn.sigmoid(g2) * yc]


def _f_relu2(rid, rv, pv):
    return [jnp.square(jax.nn.relu(rv[0]))]


def _loss_head(x2, tgt, T):
    tm = _row_tile(T)

    def body(x_ref, t_ref, dx_ref, l_ref):
        i = pl.program_id(0)
        rid = i * tm + lax.broadcasted_iota(jnp.int32, (tm, 1), 0)
        diff = (x_ref[...] - t_ref[...]) * (rid >= ROW0).astype(F32)
        dx_ref[...] = diff * (1.0 / D_MODEL)

        @pl.when(i == 0)
        def _():
            l_ref[...] = jnp.zeros_like(l_ref)

        l_ref[...] += jnp.sum(diff * diff, axis=0, keepdims=True)

    spec = pl.BlockSpec((tm, D_MODEL), lambda i: (i, 0))
    return pl.pallas_call(
        body, name="loss_head", grid=(T // tm,), in_specs=[spec, spec],
        out_specs=[spec, pl.BlockSpec((1, D_MODEL), lambda i: (0, 0))],
        out_shape=[jax.ShapeDtypeStruct((T, D_MODEL), F32), jax.ShapeDtypeStruct((1, D_MODEL), F32)],
        compiler_params=_cparams(("arbitrary",)),
    )(x2, tgt)


HALO = 32


CONV_ROWS = 64


def _conv_lanes():
    return [slice(c, c + 128) for c in range(0, CONV_DIM, 128)]


def _conv_tile(T):
    return _pick(T, (384, 128))


def _conv_fwd(h, w, b, T, name):
    tr = _conv_tile(T)
    ratio = tr // HALO
    wp = jnp.zeros((HALO, CONV_DIM), F32).at[:CONV_K].set(w)

    def body(m_ref, h_ref, w_ref, b_ref, o_ref, win):
        i = pl.program_id(0)
        win[0:HALO, :] = h_ref[...] * (i > 0).astype(F32)
        win[HALO:, :] = m_ref[...]
        for cs in _conv_lanes():
            wv, bv = w_ref[:, cs], b_ref[:, cs]
            for r0 in range(0, tr, CONV_ROWS):
                acc = jnp.broadcast_to(bv, (CONV_ROWS, 128))
                for k in range(CONV_K):
                    acc = acc + wv[k:k + 1] * win[pl.ds(HALO - (CONV_K - 1) + k + r0, CONV_ROWS), cs]
                o_ref[r0:r0 + CONV_ROWS, cs] = acc

    return pl.pallas_call(
        body, name=name, grid=(T // tr,),
        in_specs=[pl.BlockSpec((tr, CONV_DIM), lambda i: (i, 0)),
                  pl.BlockSpec((HALO, CONV_DIM), lambda i: (jnp.maximum(i * ratio - 1, 0), 0)),
                  pl.BlockSpec((HALO, CONV_DIM), lambda i: (0, 0)),
                  pl.BlockSpec((1, CONV_DIM), lambda i: (0, 0))],
        out_specs=pl.BlockSpec((tr, CONV_DIM), lambda i: (i, 0)),
        out_shape=jax.ShapeDtypeStruct((T, CONV_DIM), F32),
        scratch_shapes=[pltpu.VMEM((tr + HALO, CONV_DIM), F32)],
        compiler_params=_cparams(("parallel",)),
    )(h, h, wp, _param2d(b))


def _conv_bwd(h, w, dy, T, name):
    tr = _conv_tile(T)
    ratio = tr // HALO
    n_t = T // tr
    last_halo = T // HALO - 1
    wp = jnp.zeros((HALO, CONV_DIM), F32).at[:CONV_K].set(w)

    def body(hm_ref, hh_ref, dm_ref, dh_ref, w_ref, dx_ref, dw_ref, db_ref, hwin, dwin):
        i = pl.program_id(0)
        hwin[0:HALO, :] = hh_ref[...] * (i > 0).astype(F32)
        hwin[HALO:, :] = hm_ref[...]
        dwin[0:tr, :] = dm_ref[...]
        dwin[tr:, :] = dh_ref[...] * (i < n_t - 1).astype(F32)

        @pl.when(i == 0)
        def _():
            dw_ref[...] = jnp.zeros_like(dw_ref)
            db_ref[...] = jnp.zeros_like(db_ref)

        db_ref[...] += jnp.sum(dm_ref[...], axis=0, keepdims=True)
        fold = lambda a: functools.reduce(jnp.add, [a[r:r + 8] for r in range(0, CONV_ROWS, 8)])
        for cs in _conv_lanes():
            wv = w_ref[:, cs]
            dws = [jnp.zeros((8, 128), F32) for _ in range(CONV_K)]
            for r0 in range(0, tr, CONV_ROWS):
                acc = jnp.zeros((CONV_ROWS, 128), F32)
                for k in range(CONV_K):
                    acc = acc + wv[k:k + 1] * dwin[pl.ds(CONV_K - 1 - k + r0, CONV_ROWS), cs]
                dx_ref[r0:r0 + CONV_ROWS, cs] = acc
                dy_t = dm_ref[r0:r0 + CONV_ROWS, cs]
                for k in range(CONV_K):
                    dws[k] = dws[k] + fold(dy_t * hwin[pl.ds(HALO - (CONV_K - 1) + k + r0, CONV_ROWS), cs])
            for k in range(CONV_K):
                dw_ref[k:k + 1, cs] += jnp.sum(dws[k], axis=0, keepdims=True)

    main = pl.BlockSpec((tr, CONV_DIM), lambda i: (i, 0))
    return pl.pallas_call(
        body, name=name, grid=(n_t,),
        in_specs=[main,
                  pl.BlockSpec((HALO, CONV_DIM), lambda i: (jnp.maximum(i * ratio - 1, 0), 0)),
                  main,
                  pl.BlockSpec((HALO, CONV_DIM), lambda i: (jnp.minimum((i + 1) * ratio, last_halo), 0)),
                  pl.BlockSpec((HALO, CONV_DIM), lambda i: (0, 0))],
        out_specs=[main, pl.BlockSpec((HALO, CONV_DIM), lambda i: (0, 0)), pl.BlockSpec((1, CONV_DIM), lambda i: (0, 0))],
        out_shape=[jax.ShapeDtypeStruct((T, CONV_DIM), F32), jax.ShapeDtypeStruct((HALO, CONV_DIM), F32),
                   jax.ShapeDtypeStruct((1, CONV_DIM), F32)],
        scratch_shapes=[pltpu.VMEM((tr + HALO, CONV_DIM), F32), pltpu.VMEM((tr + HALO, CONV_DIM), F32)],
        compiler_params=_cparams(("arbitrary",)),
    )(h, h, dy, dy, wp)


NEG = -1e30
ATT_SCALE = QK_DIM ** -0.5
_NT = (((1,), (1,)), ((), ()))
_TN = (((0,), (0,)), ((), ()))


def _att_blk(T):
    return _pick(T, (384, 128))


def _att_mask(i, j, blk):
    kpos = j * blk + lax.broadcasted_iota(jnp.int32, (blk, blk), 0)
    qpos = i * blk + lax.broadcasted_iota(jnp.int32, (blk, blk), 1)
    return (kpos <= qpos) & (kpos >= PAD_FRONT)


def _t32(a):
    return a.astype(F32).T.astype(BF16)


def _attn_fwd(q, k, v, T, name):
    blk = _att_blk(T)
    nq = T // blk

    def body(q_ref, k_ref, v_ref, o_ref, lse_ref, vt):
        i = pl.program_id(1)

        @pl.when(i == 0)
        def _():
            def tr(j, c):
                vt[j] = _t32(v_ref[pl.ds(pl.multiple_of(j * blk, blk), blk), :])
                return c

            lax.fori_loop(0, nq, tr, 0)

        qb = q_ref[...]

        def step(js, carry, masked):
            m, l, acc = carry
            ss = []
            for j in js:
                kb = k_ref[pl.ds(pl.multiple_of(j * blk, blk), blk), :]
                s = lax.dot_general(kb, qb, _NT, preferred_element_type=F32)
                ss.append(jnp.where(_att_mask(i, j, blk), s, NEG) if masked else s)
            m_new = m
            for s in ss:
                m_new = jnp.maximum(m_new, jnp.max(s, axis=0, keepdims=True))
            alpha = jnp.exp(m - m_new)
            l = alpha * l
            acc = alpha * acc
            for j, s in zip(js, ss):
                p = jnp.exp(s - m_new)
                l = l + jnp.sum(p, axis=0, keepdims=True)
                acc = acc + jnp.dot(vt[j], p.astype(BF16), preferred_element_type=F32)
            return m_new, l, acc

        init = (jnp.full((1, blk), NEG, F32), jnp.zeros((1, blk), F32), jnp.zeros((HEAD_W, blk), F32))
        later = jnp.minimum(i, 1)
        carry = lax.fori_loop(0, 1 - later, lambda t, c: step([i], c, True), init)
        carry = lax.fori_loop(0, later, lambda t, c: step([i, 0], c, True), carry)
        n_free = jnp.maximum(i - 1, 0)
        n4 = n_free // 4
        carry = lax.fori_loop(0, n4, lambda t, c: step([1 + 4 * t + d for d in range(4)], c, False), carry)
        rest = n_free - 4 * n4
        carry = lax.fori_loop(0, rest // 2, lambda t, c: step([i - rest, i - rest + 1], c, False), carry)
        m, l, acc = lax.fori_loop(0, rest % 2, lambda t, c: step([i - 1], c, False), carry)
        o_ref[...] = (acc / l).T.astype(o_ref.dtype)
        lse_ref[0, 0] = m + jnp.log(l)

    full = pl.BlockSpec((T, HEAD_W), lambda h, i: (0, h))
    return pl.pallas_call(
        body, name=name, grid=(HEADS, nq),
        in_specs=[pl.BlockSpec((blk, HEAD_W), lambda h, i: (i, h)), full, full],
        out_specs=[pl.BlockSpec((blk, HEAD_W), lambda h, i: (i, h)),
                   pl.BlockSpec((1, 1, 1, blk), lambda h, i: (h, i, 0, 0))],
        out_shape=[jax.ShapeDtypeStruct((T, HEADS * HEAD_W), BF16), jax.ShapeDtypeStruct((HEADS, nq, 1, blk), F32)],
        scratch_shapes=[pltpu.VMEM((nq, HEAD_W, blk), BF16)],
        compiler_params=_cparams(("parallel", "arbitrary")),
    )(q, k, v)


def _attn_bwd(q, k, v, o, lse, do, T, name):
    blk = _att_blk(T)
    nq = T // blk

    def body(q_ref, k_ref, v_ref, o_ref, lse_ref, do_ref, dq_ref, dk_ref, dv_ref, delta, dqt, dk_acc, dv_acc):
        j = pl.program_id(1)

        @pl.when(j == 0)
        def _():
            dqt[...] = jnp.zeros_like(dqt)

            def dstep(i, c):
                r0 = pl.multiple_of(i * blk, blk)
                prod = do_ref[pl.ds(r0, blk), :].astype(F32) * o_ref[pl.ds(r0, blk), :].astype(F32)
                delta[i] = jnp.sum(prod.T, axis=0, keepdims=True)
                return c

            lax.fori_loop(0, nq, dstep, 0)

        kb = k_ref[...]
        vb = v_ref[...]
        kbt = _t32(kb)
        dk_acc[...] = jnp.zeros_like(dk_acc)
        dv_acc[...] = jnp.zeros_like(dv_acc)

        def step(qs, masked):
            dvs, dks = [], []
            for i in qs:
                r0 = pl.multiple_of(i * blk, blk)
                qb = q_ref[pl.ds(r0, blk), :]
                dob = do_ref[pl.ds(r0, blk), :]
                s = lax.dot_general(kb, qb, _NT, preferred_element_type=F32)
                p = jnp.exp(s - lse_ref[0, i])
                if masked:
                    p = jnp.where(_att_mask(i, j, blk), p, 0.0)
                dvs.append(jnp.dot(p.astype(BF16), dob, preferred_element_type=F32))
                dp = lax.dot_general(vb, dob, _NT, preferred_element_type=F32)
                ds = (p * (dp - delta[i])).astype(BF16)
                dks.append(jnp.dot(ds, qb, preferred_element_type=F32))
                dqt[i] += jnp.dot(kbt, ds, preferred_element_type=F32)
            dv_acc[...] += functools.reduce(jnp.add, dvs)
            dk_acc[...] += functools.reduce(jnp.add, dks)

        def loop(lo, masked):
            n = nq - lo
            n3 = n // 3
            rest = n - 3 * n3

            def triple(t, c):
                step([lo + 3 * t + d for d in range(3)], masked)
                return c

            def pair(t, c):
                step([nq - 2, nq - 1], masked)
                return c

            def last(t, c):
                step([nq - 1], masked)
                return c

            lax.fori_loop(0, n3, triple, 0)
            lax.fori_loop(0, rest // 2, pair, 0)
            lax.fori_loop(0, rest % 2, last, 0)

        @pl.when(j == 0)
        def _():
            loop(0, True)

        @pl.when(j > 0)
        def _():
            step([j], True)
            loop(j + 1, False)

        dk_ref[...] = dk_acc[...].astype(dk_ref.dtype)
        dv_ref[...] = dv_acc[...].astype(dv_ref.dtype)

        @pl.when(j == nq - 1)
        def _():
            def wstep(i, c):
                dq_ref[pl.ds(pl.multiple_of(i * blk, blk), blk), :] = dqt[i].T
                return c

            lax.fori_loop(0, nq, wstep, 0)

    full = pl.BlockSpec((T, HEAD_W), lambda h, j: (0, h))
    kblk = pl.BlockSpec((blk, HEAD_W), lambda h, j: (j, h))
    wide = (T, HEADS * HEAD_W)
    return pl.pallas_call(
        body, name=name, grid=(HEADS, nq),
        in_specs=[full, kblk, kblk, full, pl.BlockSpec((1, nq, 1, blk), lambda h, j: (h, 0, 0, 0)), full],
        out_specs=[full, kblk, kblk],
        out_shape=[jax.ShapeDtypeStruct(wide, F32), jax.ShapeDtypeStruct(wide, BF16), jax.ShapeDtypeStruct(wide, BF16)],
        scratch_shapes=[pltpu.VMEM((nq, 1, blk), F32), pltpu.VMEM((nq, HEAD_W, blk), F32),
                        pltpu.VMEM((blk, HEAD_W), F32), pltpu.VMEM((blk, HEAD_W), F32)],
        compiler_params=_cparams(("parallel", "arbitrary")),
    )(q, k, v, o, lse, do)


HG_NB = 6
C = HG_CHUNK
_HI = lax.Precision.HIGHEST


def _tri(lower):
    r = lax.broadcasted_iota(jnp.int32, (C, C), 0)
    c = lax.broadcasted_iota(jnp.int32, (C, C), 1)
    return ((c <= r) if lower else (c >= r)).astype(F32)


HG_SUB = 8
N_SUB = C // HG_SUB


def _hg_split_decay(b, I, rid):
    lo = I * HG_SUB
    r = b[lo:lo + 1]
    eq = jnp.exp(b[lo:lo + HG_SUB] - r)
    ek = jnp.where(rid < lo, jnp.exp(jnp.minimum(r - b, 0.0)), 0.0)
    return eq, ek


def _hg_intra_fwd(q, k, v, b):
    rid = lax.broadcasted_iota(jnp.int32, (C, 1), 0)
    tid = lax.broadcasted_iota(jnp.int32, (HG_SUB, 1), 0)
    a_rows = [jnp.zeros((HG_SUB, C), F32)]
    blocks = []
    for I in range(N_SUB):
        lo = I * HG_SUB
        q_i, b_i = q[lo:lo + HG_SUB], b[lo:lo + HG_SUB]
        if I > 0:
            eq, ek = _hg_split_decay(b, I, rid)
            a_rows.append(lax.dot_general((q_i * eq).astype(BF16), (k * ek).astype(BF16), _NT,
                                          preferred_element_type=F32))
        o_i = jnp.zeros((HG_SUB, HG_DV), F32)
        for s in range(HG_SUB):
            r = lo + s
            e = jnp.exp(jnp.minimum(b_i - b[r:r + 1], 0.0))
            a = jnp.sum(q_i * k[r:r + 1] * e, axis=-1, keepdims=True)
            o_i = o_i + jnp.where(tid >= s, a, 0.0) * v[r:r + 1]
        blocks.append(o_i)
    a_off = jnp.concatenate(a_rows, axis=0).astype(BF16)
    return jnp.dot(a_off, v.astype(BF16), preferred_element_type=F32) + jnp.concatenate(blocks, axis=0)


def _hg_intra_bwd(q, k, v, b, do, dk_s, dv_s):
    rid = lax.broadcasted_iota(jnp.int32, (C, 1), 0)
    tid = lax.broadcasted_iota(jnp.int32, (HG_SUB, 1), 0)
    da_all = lax.dot_general(do.astype(BF16), v.astype(BF16), _NT, preferred_element_type=F32)
    a_rows = [jnp.zeros((HG_SUB, C), F32)]
    dq_blocks = []
    dk = jnp.zeros((C, HG_DK), F32)
    for I in range(N_SUB):
        lo = I * HG_SUB
        q_i, b_i, do_i = q[lo:lo + HG_SUB], b[lo:lo + HG_SUB], do[lo:lo + HG_SUB]
        dq_i = jnp.zeros((HG_SUB, HG_DK), F32)
        if I > 0:
            eq, ek = _hg_split_decay(b, I, rid)
            qs, ks = (q_i * eq).astype(BF16), (k * ek).astype(BF16)
            a_rows.append(lax.dot_general(qs, ks, _NT, preferred_element_type=F32))
            da = da_all[lo:lo + HG_SUB].astype(BF16)
            dq_i = jnp.dot(da, ks, preferred_element_type=F32) * eq
            dk = dk + lax.dot_general(da, qs, _TN, preferred_element_type=F32) * ek
        for s in range(HG_SUB):
            r = lo + s
            e = jnp.where(tid >= s, jnp.exp(jnp.minimum(b_i - b[r:r + 1], 0.0)), 0.0)
            a = jnp.sum(q_i * k[r:r + 1] * e, axis=-1, keepdims=True)
            g = jnp.sum(do_i * v[r:r + 1], axis=-1, keepdims=True) * e
            dq_i = dq_i + g * k[r:r + 1]
            dk_s[r:r + 1, :] = jnp.sum(g * q_i, axis=0, keepdims=True)
            dv_s[r:r + 1, :] = jnp.sum(a * do_i, axis=0, keepdims=True)
        dq_blocks.append(dq_i)
    a_off = jnp.concatenate(a_rows, axis=0).astype(BF16)
    dv = lax.dot_general(a_off, do.astype(BF16), _TN, preferred_element_type=F32)
    return jnp.concatenate(dq_blocks, axis=0), dk + dk_s[...], dv + dv_s[...]


def _hgrn_fwd(u, kk, lf, vv, T, name):
    nb = _pick(T // C, (HG_NB, 3, 2, 1))
    rows = nb * C
    qblk = C_HQ // HG_DK

    def body(q_ref, k_ref, lf_ref, v_ref, o_ref, st_ref, st):
        @pl.when(pl.program_id(1) == 0)
        def _():
            st[...] = jnp.zeros_like(st)

        lower = _tri(True)
        for n in range(nb):
            sl = slice(n * C, (n + 1) * C)
            q, k, v = q_ref[sl, :].astype(F32), k_ref[sl, :], v_ref[sl, :]
            b = jnp.dot(lower, lf_ref[sl, :], precision=_HI, preferred_element_type=F32)
            s_t = st[...]
            st_ref[0, n] = s_t
            qe = (q * jnp.exp(b)).astype(BF16)
            o = lax.dot_general(qe, s_t.astype(BF16), _NT, preferred_element_type=F32)
            o_ref[sl, :] = o + _hg_intra_fwd(q, k, v, b)
            bl = b[C - 1:C, :]
            kd = (k * jnp.exp(bl - b)).astype(BF16)
            st[...] = s_t * jnp.exp(bl) + lax.dot_general(v.astype(BF16), kd, _TN, preferred_element_type=F32)

    col = lambda off: pl.BlockSpec((rows, HG_DK), lambda h, c: (c, h + off))
    return pl.pallas_call(
        body, name=name, grid=(HG_HEADS, T // rows),
        in_specs=[col(qblk), col(0), col(0), col(0)],
        out_specs=[col(0), pl.BlockSpec((1, nb, HG_DV, HG_DK), lambda h, c: (h, c, 0, 0))],
        out_shape=[jax.ShapeDtypeStruct((T, HG_HEADS * HG_DV), F32),
                   jax.ShapeDtypeStruct((HG_HEADS, T // C, HG_DV, HG_DK), F32)],
        scratch_shapes=[pltpu.VMEM((HG_DV, HG_DK), F32)],
        compiler_params=_cparams(("parallel", "arbitrary")),
    )(u, kk, lf, vv)


def _hgrn_bwd(u, kk, lf, vv, states, do, T, name):
    nb = _pick(T // C, (HG_NB, 3, 2, 1))
    rows = nb * C
    n_steps = T // rows
    qblk = C_HQ // HG_DK

    def body(q_ref, k_ref, lf_ref, v_ref, st_ref, do_ref, dq_ref, dk_ref, dlf_ref, dv_ref, dst, dk_s, dv_s):
        @pl.when(pl.program_id(1) == 0)
        def _():
            dst[...] = jnp.zeros_like(dst)

        lower, upper = _tri(True), _tri(False)
        rid = lax.broadcasted_iota(jnp.int32, (C, 1), 0)
        for n in reversed(range(nb)):
            sl = slice(n * C, (n + 1) * C)
            q, k, v, do = q_ref[sl, :].astype(F32), k_ref[sl, :], v_ref[sl, :], do_ref[sl, :]
            b = jnp.dot(lower, lf_ref[sl, :], precision=_HI, preferred_element_type=F32)
            s_t = st_ref[0, n]
            d_new = dst[...]
            eb = jnp.exp(b)
            bl = b[C - 1:C, :]
            ebl = jnp.exp(bl)
            dec = jnp.exp(bl - b)
            qe = q * eb
            kd = k * dec
            do_b = do.astype(BF16)
            dqe = jnp.dot(do_b, s_t.astype(BF16), preferred_element_type=F32)
            dkd = jnp.dot(v.astype(BF16), d_new.astype(BF16), preferred_element_type=F32)
            dv = lax.dot_general(kd.astype(BF16), d_new.astype(BF16), _NT, preferred_element_type=F32)
            dbl = ebl * jnp.sum(d_new * s_t, axis=0, keepdims=True) + jnp.sum(dkd * kd, axis=0, keepdims=True)
            dst[...] = d_new * ebl + lax.dot_general(do_b, qe.astype(BF16), _TN, preferred_element_type=F32)
            dq_in, dk_in, dv_in = _hg_intra_bwd(q, k, v, b, do, dk_s, dv_s)
            dq = dqe * eb + dq_in
            dk = dkd * dec + dk_in
            dv = dv + dv_in
            db = q * dq - k * dk
            db = db + jnp.where(rid == C - 1, dbl, 0.0)
            dq_ref[sl, :] = dq
            dk_ref[sl, :] = dk
            dv_ref[sl, :] = dv
            dlf_ref[sl, :] = jnp.dot(upper, db, precision=_HI, preferred_element_type=F32)

    rev = lambda off: pl.BlockSpec((rows, HG_DK), lambda h, c: (n_steps - 1 - c, h + off))
    return pl.pallas_call(
        body, name=name, grid=(HG_HEADS, n_steps),
        in_specs=[rev(qblk), rev(0), rev(0), rev(0),
                  pl.BlockSpec((1, nb, HG_DV, HG_DK), lambda h, c: (h, n_steps - 1 - c, 0, 0)), rev(0)],
        out_specs=[rev(0)] * 4,
        out_shape=[jax.ShapeDtypeStruct((T, HG_HEADS * HG_DK), F32)] * 4,
        scratch_shapes=[pltpu.VMEM((HG_DV, HG_DK), F32), pltpu.VMEM((C, HG_DK), F32), pltpu.VMEM((C, HG_DV), F32)],
        compiler_params=_cparams(("parallel", "arbitrary")),
    )(u, kk, lf, vv, states, do)


def _rope_tables(T):
    half = ROPE // 2
    inv_freq = (ROPE_BASE ** (-np.arange(half, dtype=np.float32) / half)).astype(np.float32)
    row = lambda lo, hi, val: np.concatenate([np.zeros(lo, np.float32), np.asarray(val, np.float32) * np.ones(hi - lo, np.float32),
                                              np.zeros(HEAD_W - hi, np.float32)])[None, :]
    freq = row(NOPE, NOPE + half, inv_freq) + row(NOPE + half, NOPE + ROPE, inv_freq)
    pos = lax.broadcasted_iota(jnp.int32, (T, HEAD_W), 0).astype(F32) - float(PAD_FRONT)
    ang = pos * freq
    cos, sin = jnp.cos(ang), jnp.sin(ang)
    c = cos * row(NOPE, NOPE + ROPE, 1.0) + row(0, NOPE, 1.0)
    s1 = sin * row(NOPE, NOPE + half, -1.0)
    s2 = sin * row(NOPE + half, NOPE + ROPE, 1.0)
    return c, s1, s2


def _layer_fwd(x, w, tabs, T, l, late=None):
    c, s1, s2 = tabs
    n = lambda s: f"l{l}_{s}"
    sv = {"x": x}
    h, sv["h_t"] = _rowwise(_f_rms, T, [Row(x)], [(w["norm1_g"], D_MODEL)], [(D_MODEL, BF16)], n("norm1"), True)
    u = _mm(h, w["w_in"], out_dtype=BF16, name=n("in_proj"))
    sv.update(h=h, u=u)
    if late is not None:
        w.update(late(u))
    hglu = _rowwise(_f_glu, T, [Row(u, 512, C_CONV_A), Row(u, 512, C_CONV_G)], [], [(CONV_DIM, F32)], n("glu"))[0]
    cv = _conv_fwd(hglu, w["conv_w"], w["conv_b"], T, n("conv"))
    hc = _rowwise(_f_lnsilu, T, [Row(cv)], [(w["conv_ln_g"], CONV_DIM), (w["conv_ln_b"], CONV_DIM)],
                  [(CONV_DIM, BF16)], n("conv_ln"))[0]
    y_a = _mm(hc, w["w_conv_out"], out_dtype=BF16, name=n("conv_out"))
    sv.update(hglu=hglu, cv=cv, hc=hc, y_a=y_a)
    cqn = _rowwise(_f_rms, T, [Row(u, Q_RANK, C_CQ)], [(w["q_a_norm_g"], Q_RANK)], [(Q_RANK, BF16)], n("q_a_norm"))[0]
    ckvn = _rowwise(_f_rms, T, [Row(u, KV_RANK, C_CKV)], [(w["kv_a_norm_g"], KV_RANK)], [(KV_RANK, BF16)], n("kv_a_norm"))[0]
    q_raw = _mm(cqn, w["w_uq"], out_dtype=BF16, name=n("uq"))
    k_raw = _mm(ckvn, w["w_uk"], out_dtype=BF16, name=n("uk"))
    v = _mm(ckvn, w["w_uv"], out_dtype=BF16, name=n("uv"))
    tab_rows = [Row(c), Row(s1), Row(s2)]
    q = _rowwise(_f_qrope, T, [Row(q_raw, piece=HEAD_W)] + tab_rows, [(w["q_norm_g"], HEAD_W)],
                 [(HEADS * HEAD_W, BF16)], n("q_rope"))[0]
    k = _rowwise(_f_krope, T, [Row(k_raw, piece=HEAD_W), Row(u, HEAD_W, C_KR)] + tab_rows, [(w["k_norm_g"], HEAD_W)],
                 [(HEADS * HEAD_W, BF16)], n("k_rope"))[0]
    o, lse = _attn_fwd(q, k, v, T, n("attn"))
    y_b = _mm(o, w["w_attn_out"], out_dtype=BF16, name=n("attn_out"))
    sv.update(cqn=cqn, ckvn=ckvn, q_raw=q_raw, k_raw=k_raw, v=v, q=q, k=k, o=o, lse=lse, y_b=y_b)
    kk, lf, vv = _rowwise(_f_hgrn_prep, T, [Row(u, 512, C_HF), Row(u, 512, C_HI)], [(w["lb"], 512)],
                          [(512, F32)] * 3, n("hgrn_prep"))
    o_h, states = _hgrn_fwd(u, kk, lf, vv, T, n("hgrn"))
    oh = _rowwise(_f_hgrn_out, T, [Row(o_h, piece=HG_DV), Row(u, 512, C_HG, piece=HG_DV)], [(w["hgrn_norm_g"], HG_DV)],
                  [(512, BF16)], n("hgrn_out_norm"))[0]
    y_c = _mm(oh, w["w_hgrn_out"], out_dtype=BF16, name=n("hgrn_out"))
    sv.update(kk=kk, lf=lf, vv=vv, o_h=o_h, states=states, oh=oh, y_c=y_c)
    gate_rows = [Row(u, D_MODEL, C_GATE + g * D_MODEL) for g in range(3)]
    mix = _rowwise(_f_mix, T, gate_rows + [Row(y_a), Row(y_b), Row(y_c)], [], [(D_MODEL, BF16)], n("mix"))[0]
    x1 = _mm(mix, w["w_out"], res=x, name=n("out_proj"))
    h2, sv["h2_t"] = _rowwise(_f_rms, T, [Row(x1)], [(w["norm2_g"], D_MODEL)], [(D_MODEL, BF16)], n("norm2"), True)
    f = _mm(h2, w["w_ff1"], out_dtype=BF16, name=n("ff1"))
    x2 = _mm(f, w["w_ff2"], res=x1, a_fn=_relu2, name=n("ff2"))
    sv.update(mix=mix, x1=x1, h2=h2, f=f)
    return x2, sv


def _layer_bwd(dx2, w, sv, tabs, T, l, mid=None, matrices=None):
    c, s1, s2 = tabs
    n = lambda s: f"l{l}_b_{s}"
    u = sv["u"]
    g = {}
    g["w_ff2"] = _mm(sv["f"], dx2, ta=True, a_fn=_relu2, out_dtype=BF16, name=n("dw_ff2"))
    df = _mm(dx2, w["w_ff2"], tb=True, out_dtype=BF16, name=n("d_f"),
             epi=(sv["f"], lambda d, fv: d * (2.0 * jnp.maximum(fv, 0.0))))
    g["w_ff1"] = _mm(sv["h2_t"], df, out_dtype=BF16, name=n("dw_ff1"))
    dh2 = _mm(df, w["w_ff1"], tb=True, name=n("d_h2"))
    (dx1,), (g["norm2_g"],) = _rowwise_bwd(_f_rms, T, [Row(sv["x1"])], [(w["norm2_g"], D_MODEL)], [Row(dh2)],
                                           {0: F32}, n("norm2"), add=(0, dx2))
    g["w_out"] = _mm(sv["mix"], dx1, ta=True, out_dtype=BF16, name=n("dw_out"))
    w_out = w["w_out"] if mid is None else mid(g, w["w_out"])
    dmix = _mm(dx1, w_out, tb=True, out_dtype=BF16, name=n("d_mix"))
    gate_rows = [Row(u, D_MODEL, C_GATE + i * D_MODEL) for i in range(3)]
    (dg0, dg1, dg2, dy_a, dy_b, dy_c), _ = _rowwise_bwd(
        _f_mix, T, gate_rows + [Row(sv["y_a"]), Row(sv["y_b"]), Row(sv["y_c"])], [], [Row(dmix)],
        {0: BF16, 1: BF16, 2: BF16, 3: BF16, 4: BF16, 5: BF16}, n("mix"))
    g["w_hgrn_out"] = _mm(sv["oh"], dy_c, ta=True, out_dtype=BF16, name=n("dw_hgrn_out"))
    doh = _mm(dy_c, w["w_hgrn_out"], tb=True, out_dtype=BF16, name=n("d_oh"))
    (do_h, dhg), (g["hgrn_norm_g"],) = _rowwise_bwd(
        _f_hgrn_out, T, [Row(sv["o_h"], piece=HG_DV), Row(u, 512, C_HG, piece=HG_DV)], [(w["hgrn_norm_g"], HG_DV)],
        [Row(doh, piece=HG_DV)], {0: F32, 1: BF16}, n("hgrn_out_norm"))
    dhq, dkk, dlf, dvv = _hgrn_bwd(u, sv["kk"], sv["lf"], sv["vv"], sv["states"], do_h, T, n("hgrn"))
    (dhf, dhi), (g["lb"],) = _rowwise_bwd(
        _f_hgrn_prep, T, [Row(u, 512, C_HF), Row(u, 512, C_HI)], [(w["lb"], 512)],
        [Row(dkk), Row(dlf), Row(dvv)], {0: BF16, 1: BF16}, n("hgrn_prep"))
    g["w_attn_out"] = _mm(sv["o"], dy_b, ta=True, out_dtype=BF16, name=n("dw_attn_out"))
    do = _mm(dy_b, w["w_attn_out"], tb=True, out_dtype=BF16, name=n("d_o"))
    dq, dk, dv = _attn_bwd(sv["q"], sv["k"], sv["v"], sv["o"], sv["lse"], do, T, n("attn"))
    tab_rows = [Row(c), Row(s1), Row(s2)]
    (dq_raw,), (g["q_norm_g"],) = _rowwise_bwd(
        _f_qrope, T, [Row(sv["q_raw"], piece=HEAD_W)] + tab_rows, [(w["q_norm_g"], HEAD_W)],
        [Row(dq, piece=HEAD_W)], {0: BF16}, n("q_rope"))
    (dk_raw, dkr), (g["k_norm_g"],) = _rowwise_bwd(
        _f_krope, T, [Row(sv["k_raw"], piece=HEAD_W), Row(u, HEAD_W, C_KR)] + tab_rows, [(w["k_norm_g"], HEAD_W)],
        [Row(dk, piece=HEAD_W)], {0: BF16, 1: BF16}, n("k_rope"))
    g["w_uq"] = _mm(sv["cqn"], dq_raw, ta=True, out_dtype=BF16, name=n("dw_uq"))
    g["w_uk"] = _mm(sv["ckvn"], dk_raw, ta=True, out_dtype=BF16, name=n("dw_uk"))
    g["w_uv"] = _mm(sv["ckvn"], dv, ta=True, out_dtype=BF16, name=n("dw_uv"))
    dcqn = _mm(dq_raw, w["w_uq"], tb=True, out_dtype=BF16, name=n("d_cqn"))
    dckvn = _mm(dk_raw, w["w_uk"], tb=True, name=n("d_ckvn_k"))
    dckvn = _mm(dv, w["w_uv"], tb=True, res=dckvn, out_dtype=BF16, name=n("d_ckvn_v"))
    (dcq,), (g["q_a_norm_g"],) = _rowwise_bwd(_f_rms, T, [Row(u, Q_RANK, C_CQ)], [(w["q_a_norm_g"], Q_RANK)],
                                              [Row(dcqn)], {0: BF16}, n("q_a_norm"))
    (dckv,), (g["kv_a_norm_g"],) = _rowwise_bwd(_f_rms, T, [Row(u, KV_RANK, C_CKV)], [(w["kv_a_norm_g"], KV_RANK)],
                                                [Row(dckvn)], {0: BF16}, n("kv_a_norm"))
    g["w_conv_out"] = _mm(sv["hc"], dy_a, ta=True, out_dtype=BF16, name=n("dw_conv_out"))
    dhc = _mm(dy_a, w["w_conv_out"], tb=True, out_dtype=BF16, name=n("d_hc"))
    (dcv,), (g["conv_ln_g"], g["conv_ln_b"]) = _rowwise_bwd(
        _f_lnsilu, T, [Row(sv["cv"])], [(w["conv_ln_g"], CONV_DIM), (w["conv_ln_b"], CONV_DIM)], [Row(dhc)],
        {0: F32}, n("conv_ln"))
    dhglu, dconv_w, g["conv_b"] = _conv_bwd(sv["hglu"], w["conv_w"], dcv, T, n("conv"))
    g["conv_w"] = dconv_w[:CONV_K]
    (dua, dug), _ = _rowwise_bwd(_f_glu, T, [Row(u, 512, C_CONV_A), Row(u, 512, C_CONV_G)], [], [Row(dhglu)],
                                 {0: BF16, 1: BF16}, n("glu"))
    du = jnp.concatenate([dua, dug, dg0, dg1, dg2, dcq, dckv, dkr, dhq.astype(BF16), dhf, dhi, dhg], axis=1)
    small = ("w_uq", "w_uk", "w_uv", "w_attn_out", "w_hgrn_out", "w_conv_out")
    du, *done = lax.optimization_barrier((du, *[g[k] for k in small]))
    g.update(zip(small, done))
    g["w_in"] = _mm(sv["h_t"], du, out_dtype=BF16, name=n("dw_in"))
    norm_g = w["norm1_g"] if matrices is None else matrices(g, w["norm1_g"])
    du, norm_g = lax.optimization_barrier((du, norm_g))
    dh = _mm(du, w["w_in"], tb=True, name=n("d_h"))
    (dx,), (g["norm1_g"],) = _rowwise_bwd(_f_rms, T, [Row(sv["x"])], [(norm_g, D_MODEL)], [Row(dh)],
                                          {0: F32}, n("norm1"), add=(0, dx1))
    return dx, g


def _pad_w_in(w_in):
    z = lambda k: jnp.zeros((w_in.shape[0], k), w_in.dtype)
    return jnp.concatenate([w_in[:, :O_CQ], w_in[:, O_GATE:], w_in[:, O_CQ:O_KR], z(KR_LANE), w_in[:, O_KR:O_HQ],
                            z(HEAD_W - KR_LANE - ROPE), w_in[:, O_HQ:O_GATE]], axis=1)


def _unpad_w_in(g):
    return jnp.concatenate([g[:, :C_GATE], g[:, C_CQ:C_KR], g[:, C_KR + KR_LANE:C_KR + KR_LANE + ROPE],
                            g[:, C_HQ:], g[:, C_GATE:C_CQ]], axis=1)


_W_IN_RUNS = ((0, 0, O_CQ), (O_CQ, C_CQ, O_KR - O_CQ), (O_KR, C_KR + KR_LANE, ROPE), (O_HQ, C_HQ, O_GATE - O_HQ),
              (O_GATE, C_GATE, N_IN - O_GATE))


def _w_in_from_shards(g8):
    per = N_IN // N_DEV
    pieces, at = [], 0
    for o0, p0, n in sorted(_W_IN_RUNS, key=lambda r: r[1]):
        if p0 > at:
            pieces.append(jnp.zeros((g8.shape[1], p0 - at), g8.dtype))
        for j in range(o0 // per, (o0 + n - 1) // per + 1):
            lo, hi = max(o0, j * per), min(o0 + n, (j + 1) * per)
            pieces.append(g8[j][:, lo - j * per:hi - j * per])
        at = p0 + n
    if at < N_IN_P:
        pieces.append(jnp.zeros((g8.shape[1], N_IN_P - at), g8.dtype))
    return jnp.concatenate(pieces, axis=1)


def _w_in_grad_shards(g):
    per = N_IN // N_DEV
    shards = []
    for j in range(N_DEV):
        lo, hi = j * per, (j + 1) * per
        pieces = [g[:, p0 + max(lo, o0) - o0:p0 + min(hi, o0 + n) - o0]
                  for o0, p0, n in _W_IN_RUNS if max(lo, o0) < min(hi, o0 + n)]
        shards.append(jnp.concatenate(pieces, axis=1) if len(pieces) > 1 else pieces[0])
    return jnp.stack(shards)


def _pad_heads(wm, per_head, lo, hi):
    lead = wm.shape[:-1]
    wh = wm.reshape(lead + (HEADS, per_head))[..., lo:hi]
    pad = [(0, 0)] * len(lead) + [(0, 0), (0, HEAD_W - (hi - lo))]
    return jnp.pad(wh, pad).reshape(lead + (HEADS * HEAD_W,))


def _unpad_heads(gm, width):
    lead = gm.shape[:-1]
    return gm.reshape(lead + (HEADS, HEAD_W))[..., :width]


def _layer_weights(full, lb=None):
    w = {}
    for name in ("norm1_g", "conv_w", "conv_b", "conv_ln_g", "conv_ln_b", "q_a_norm_g", "kv_a_norm_g", "hgrn_norm_g",
                 "norm2_g", "w_conv_out", "w_hgrn_out", "w_out", "w_ff1", "w_ff2"):
        if name in full:
            w[name] = full[name]
    if lb is not None:
        w["lb"] = lb
    if "w_in_padded" in full:
        w["w_in"] = full["w_in_padded"]
    elif "w_in" in full:
        w["w_in"] = _pad_w_in(full["w_in"])
    if "w_uq" in full:
        w["w_uq"] = _pad_heads(full["w_uq"], QK_DIM, 0, QK_DIM)
    if "w_ukv" in full:
        w["w_uk"] = _pad_heads(full["w_ukv"], NOPE + V_DIM, 0, NOPE)
        w["w_uv"] = _pad_heads(full["w_ukv"], NOPE + V_DIM, NOPE, NOPE + V_DIM)
    for name in ("q_norm_g", "k_norm_g"):
        if name in full:
            w[name] = jnp.pad(full[name], (0, HEAD_W - QK_DIM))
    if "w_attn_out" in full:
        wa = full["w_attn_out"].reshape(HEADS, V_DIM, D_MODEL)
        w["w_attn_out"] = jnp.pad(wa, ((0, 0), (0, HEAD_W - V_DIM), (0, 0))).reshape(HEADS * HEAD_W, D_MODEL)
    return w


def _matrix_grads_to_original(g):
    o = {name: g[name] for name in ("w_conv_out", "w_hgrn_out", "w_out", "w_ff1", "w_ff2")}
    o["w_in"] = _unpad_w_in(g["w_in"])
    o["w_in_shards"] = _w_in_grad_shards(g["w_in"])
    o["w_uq"] = _unpad_heads(g["w_uq"], QK_DIM).reshape(Q_RANK, HEADS * QK_DIM)
    guk = _unpad_heads(g["w_uk"], NOPE)
    guv = _unpad_heads(g["w_uv"], V_DIM)
    o["w_ukv"] = jnp.concatenate([guk, guv], axis=-1).reshape(KV_RANK, HEADS * (NOPE + V_DIM))
    o["w_attn_out"] = g["w_attn_out"].reshape(HEADS, HEAD_W, D_MODEL)[:, :V_DIM].reshape(HEADS * V_DIM, D_MODEL)
    return o


def _vector_grads_to_original(g):
    o = {"conv_w": g["conv_w"]}
    for name in ("norm1_g", "conv_b", "conv_ln_g", "conv_ln_b", "q_a_norm_g", "kv_a_norm_g", "hgrn_norm_g", "norm2_g", "lb"):
        o[name] = g[name].reshape(-1)
    o["q_norm_g"] = g["q_norm_g"].reshape(-1)[:QK_DIM]
    o["k_norm_g"] = g["k_norm_g"].reshape(-1)[:QK_DIM]
    return o


def _lower_bounds(logits):
    p = jax.nn.softmax(logits.astype(F32), axis=0)
    return jnp.cumsum(p, axis=0) - p[0:1]


def _run_step(x, target, meta, lb_logits, layer_weights, layer_done, layer_mid=None, layer_matrices=None):
    seq = x.shape[0]
    T = ROW0 + seq
    assert T % 128 == 0
    tabs = _rope_tables(T)
    lbs, lb_vjp = jax.vjp(_lower_bounds, lb_logits)
    xp = jnp.concatenate([jnp.zeros((PAD_FRONT, D_MODEL), F32), meta.astype(F32), x], axis=0)
    tp = jnp.concatenate([jnp.zeros((ROW0, D_MODEL), F32), target], axis=0)
    ws, svs = [], []
    for l in range(DEPTH):
        full, xp = layer_weights(l, xp)
        late = full.pop("late", None)
        w = _layer_weights(full, lbs[l])
        xp, sv = _layer_fwd(xp, w, tabs, T, l, late)
        ws.append(w)
        svs.append(sv)
    dx, sq = _loss_head(xp, tp, T)
    loss = 0.5 * jnp.sum(sq) * (1.0 / D_MODEL)
    dlb = [None] * DEPTH
    for l in reversed(range(DEPTH)):
        mid = None if layer_mid is None else functools.partial(layer_mid, l)
        mats = {}

        def matrices(g, norm_g, l=l, mats=mats):
            mats.update(_matrix_grads_to_original(g))
            return norm_g if layer_matrices is None else layer_matrices(l, mats, norm_g)

        dx, g = _layer_bwd(dx, ws[l], svs[l], tabs, T, l, mid, matrices)
        g = {**_vector_grads_to_original(g), **mats}
        dlb[l] = g.pop("lb")
        dx = layer_done(l, g, dx)
    return loss, dx[ROW0:], dx[PAD_FRONT:ROW0], lb_vjp(jnp.stack(dlb))[0]


def _local_step(x, target, full):
    per_layer = [None] * DEPTH

    def done(l, g, dx):
        per_layer[l] = g
        return dx

    loss, gx, gmeta, glb = _run_step(
        x, target, full["meta"], full["hgrn_lb_logits"],
        lambda l, xp: ({k: v[l] for k, v in full.items() if k != "meta"}, xp), done)
    grads = {k: jnp.stack([per_layer[l][k] for l in range(DEPTH)]) for k in per_layer[0]}
    grads["hgrn_lb_logits"] = glb
    grads["meta"] = gmeta
    return loss, gx, grads


def _mesh_pos():
    return lax.axis_index("x"), lax.axis_index("y"), lax.axis_index("c")


N_COPY = N_DEV - 1


def _all_gather(arrs, name):
    n = len(arrs)

    def body(*refs):
        x_refs, out_refs = refs[:n], refs[n:2 * n]
        send_sems, recv_sems, local_sems = refs[2 * n:]
        x, y, c = _mesh_pos()
        me, sibling = (x, y, c), (x, y, 1 - c)
        chips = [(1 - x, y), (x, 1 - y), (1 - x, 1 - y)]

        def slot(a, px, py, pc):
            return out_refs[a].at[4 * px + 2 * py + pc]

        def copy(a, k, block, to, own=False):
            return pltpu.make_async_remote_copy(
                src_ref=x_refs[a] if own else slot(a, *block), dst_ref=slot(a, *block),
                send_sem=send_sems.at[a * N_COPY + k], recv_sem=recv_sems.at[a * N_COPY + k],
                device_id=to, device_id_type=MESH)

        mine = [pltpu.make_async_copy(x_refs[a], slot(a, *me), local_sems.at[a]) for a in range(n)]
        for cp in mine:
            cp.start()
        first = []
        for a in range(n):
            first.append(copy(a, 0, me, sibling, own=True))
            first += [copy(a, 1 + j, me, (*chip, c), own=True) for j, chip in enumerate(chips)]
        for cp in first:
            cp.start()
        passed = []
        for j, chip in enumerate(chips):
            for a in range(n):
                copy(a, 1 + j, (*chip, c), me).wait_recv()
                cp = copy(a, 4 + j, (*chip, c), sibling)
                cp.start()
                passed.append(cp)
        for a in range(n):
            copy(a, 0, sibling, me).wait_recv()
            for j, chip in enumerate(chips):
                copy(a, 4 + j, (*chip, 1 - c), me).wait_recv()
        for cp in first + passed:
            cp.wait_send()
        for cp in mine:
            cp.wait()

    anyspec = pl.BlockSpec(memory_space=pl.ANY)
    return pl.pallas_call(
        body, name=name, out_shape=[jax.ShapeDtypeStruct((N_DEV,) + a.shape, a.dtype) for a in arrs],
        in_specs=[anyspec] * n, out_specs=[anyspec] * n,
        scratch_shapes=[pltpu.SemaphoreType.DMA((n * N_COPY,)), pltpu.SemaphoreType.DMA((n * N_COPY,)),
                        pltpu.SemaphoreType.DMA((n,))],
    )(*arrs)


def _exchange(arrs, name):
    n = len(arrs)

    def body(*refs):
        s_refs, r_refs = refs[:n], refs[n:2 * n]
        send_sems, recv_sems, local_sems = refs[2 * n:]
        x, y, c = _mesh_pos()
        me = 4 * x + 2 * y + c
        local = [pltpu.make_async_copy(s_refs[a].at[me], r_refs[a].at[me], local_sems.at[a]) for a in range(n)]
        for cp in local:
            cp.start()
        sends, recvs = [], []
        for rel in range(1, N_DEV):
            px = 1 - x if rel & 4 else x
            py = 1 - y if rel & 2 else y
            pc = 1 - c if rel & 1 else c
            p = 4 * px + 2 * py + pc
            for a in range(n):
                k = a * N_COPY + rel - 1
                sends.append(pltpu.make_async_remote_copy(
                    src_ref=s_refs[a].at[p], dst_ref=r_refs[a].at[me], send_sem=send_sems.at[k],
                    recv_sem=recv_sems.at[k], device_id=(px, py, pc), device_id_type=MESH))
                recvs.append(pltpu.make_async_remote_copy(
                    src_ref=s_refs[a].at[me], dst_ref=r_refs[a].at[p], send_sem=send_sems.at[k],
                    recv_sem=recv_sems.at[k], device_id=(px, py, pc), device_id_type=MESH))
        for cp in sends:
            cp.start()
        for cp in recvs:
            cp.wait_recv()
        for cp in sends:
            cp.wait_send()
        for cp in local:
            cp.wait()

    anyspec = pl.BlockSpec(memory_space=pl.ANY)
    return pl.pallas_call(
        body, name=name, out_shape=[jax.ShapeDtypeStruct(a.shape, a.dtype) for a in arrs],
        in_specs=[anyspec] * n, out_specs=[anyspec] * n,
        scratch_shapes=[pltpu.SemaphoreType.DMA((n * N_COPY,)), pltpu.SemaphoreType.DMA((n * N_COPY,)),
                        pltpu.SemaphoreType.DMA((n,))],
    )(*arrs)


_HBM = pl.BlockSpec(memory_space=pltpu.HBM)
_SEM = pl.BlockSpec(memory_space=pltpu.SEMAPHORE)
_EFFECT = pltpu.SideEffectType.DATAFLOW_SIDE_EFFECTING


def _peers(x, y, c):
    out = []
    for rel in range(1, N_DEV):
        px = 1 - x if rel & 4 else x
        py = 1 - y if rel & 2 else y
        pc = 1 - c if rel & 1 else c
        out.append((rel, (px, py, pc), 4 * px + 2 * py + pc))
    return out


ALL_RELS = tuple(range(1, N_DEV))
NEAR_RELS = (1, 2, 4, 6)


def _split_copies(src_refs, land_refs, send_sems, recv_sems, gather, rels=ALL_RELS):
    x, y, c = _mesh_pos()
    me = 4 * x + 2 * y + c
    out = []
    for a, (src, land) in enumerate(zip(src_refs, land_refs)):
        for rel, peer, p in _peers(x, y, c):
            if rel not in rels:
                continue
            k = a * N_COPY + rel - 1
            mk = lambda s, d: pltpu.make_async_remote_copy(
                src_ref=s, dst_ref=d, send_sem=send_sems.at[k], recv_sem=recv_sems.at[k],
                device_id=peer, device_id_type=MESH)
            mine = src if gather else src.at[p]
            out.append((mk(mine, land.at[me]), mk(mine, land.at[p])))
    return out


def _copy_start(srcs, gather, name, collective_id, rels=ALL_RELS):
    n = len(srcs)
    lands = [lax.empty(((N_DEV,) + s.shape) if gather else s.shape, s.dtype) for s in srcs]

    def body(*refs):
        src_refs, land_refs = refs[:n], refs[n:2 * n]
        send_sems, recv_sems = refs[2 * n], refs[2 * n + 1]
        token = refs[-1]
        x, y, c = _mesh_pos()
        barrier = pltpu.get_barrier_semaphore()
        for rel, peer, _ in _peers(x, y, c):
            if rel in rels:
                pl.semaphore_signal(barrier, inc=1, device_id=peer, device_id_type=MESH)
        pl.semaphore_wait(barrier, len(rels))
        for out_copy, _ in _split_copies(src_refs, land_refs, send_sems, recv_sems, gather, rels):
            out_copy.start()
        token[...] = jnp.zeros_like(token)

    hbm = lambda a: pltpu.HBM(a.shape, a.dtype)
    res = pl.pallas_call(
        body, name=name,
        out_shape=(pltpu.SemaphoreType.DMA((n * N_COPY,)), pltpu.SemaphoreType.DMA((n * N_COPY,)),
                   *[hbm(s) for s in srcs], *[hbm(z) for z in lands], jax.ShapeDtypeStruct((8, 128), F32)),
        in_specs=[_HBM] * (2 * n), out_specs=(_SEM, _SEM, *([_HBM] * (2 * n)), pl.BlockSpec(memory_space=pltpu.VMEM)),
        input_output_aliases={i: 2 + i for i in range(2 * n)},
        compiler_params=pltpu.CompilerParams(has_side_effects=_EFFECT, collective_id=collective_id),
    )(*[pltpu.with_memory_space_constraint(s, pltpu.HBM) for s in srcs],
      *[pltpu.with_memory_space_constraint(z, pltpu.HBM) for z in lands])
    return res[0], res[1], list(res[2:2 + n]), list(res[2 + n:2 + 2 * n]), res[-1]


def _after(a, token):
    return a + token[0, 0].astype(a.dtype)


def _forward_to_sibling(lands, name):
    n = len(lands)

    def body(*refs):
        land_refs = refs[n:2 * n]
        send_sems, recv_sems = refs[2 * n], refs[2 * n + 1]
        x, y, c = _mesh_pos()
        chips = [(1 - x, y), (x, 1 - y), (1 - x, 1 - y)]
        sends, recvs = [], []
        for a, land in enumerate(land_refs):
            for j, (px, py) in enumerate(chips):
                k = a * len(chips) + j
                mk = lambda slot: pltpu.make_async_remote_copy(
                    src_ref=land.at[slot], dst_ref=land.at[slot], send_sem=send_sems.at[k], recv_sem=recv_sems.at[k],
                    device_id=(x, y, 1 - c), device_id_type=MESH)
                sends.append(mk(4 * px + 2 * py + c))
                recvs.append(mk(4 * px + 2 * py + 1 - c))
        for cp in sends:
            cp.start()
        for cp in recvs:
            cp.wait_recv()
        for cp in sends:
            cp.wait_send()

    anyspec = pl.BlockSpec(memory_space=pl.ANY)
    return list(pl.pallas_call(
        body, name=name, out_shape=[jax.ShapeDtypeStruct(z.shape, z.dtype) for z in lands],
        in_specs=[anyspec] * n, out_specs=[anyspec] * n, input_output_aliases={i: i for i in range(n)},
        scratch_shapes=[pltpu.SemaphoreType.DMA((3 * n,)), pltpu.SemaphoreType.DMA((3 * n,))],
    )(*lands))


def _copy_wait(send_sems, recv_sems, srcs, lands, after, gather, name, rels=ALL_RELS):
    n = len(srcs)

    def body(*refs):
        src_refs, land_refs = refs[:n], refs[n:2 * n]
        s_sems, r_sems = refs[2 * n], refs[2 * n + 1]
        for out_copy, in_copy in _split_copies(src_refs, land_refs, s_sems, r_sems, gather, rels):
            out_copy.wait_send()
            in_copy.wait_recv()

    hbm = lambda a: pltpu.HBM(a.shape, a.dtype)
    res = pl.pallas_call(
        body, name=name, out_shape=(*[hbm(s) for s in srcs], *[hbm(z) for z in lands]),
        in_specs=[_HBM] * (2 * n) + [_SEM, _SEM, pl.BlockSpec(memory_space=pl.ANY)], out_specs=tuple([_HBM] * (2 * n)),
        input_output_aliases={i: i for i in range(2 * n)},
        compiler_params=pltpu.CompilerParams(has_side_effects=_EFFECT),
    )(*srcs, *lands, send_sems, recv_sems, after)
    return list(res[:n]), list(res[n:])


def _sum_parts(parts, name):
    P, R, W = parts.shape

    def body(p_ref, o_ref):
        g = p_ref[0].astype(F32)
        for i in range(1, P):
            g = g + p_ref[i].astype(F32)
        o_ref[...] = g

    return pl.pallas_call(body, name=name, out_shape=jax.ShapeDtypeStruct((R, W), F32))(parts)


def _adamw_body(p_ref, w_ref, m_ref, v_ref, g_ref, d_ref, nm_ref, nv_ref):
    g = p_ref[0].astype(F32)
    for i in range(1, p_ref.shape[0]):
        g = g + p_ref[i].astype(F32)
    _adamw_apply(g, w_ref, m_ref, v_ref, g_ref, d_ref, nm_ref, nv_ref)


def _adamw_apply(g, w_ref, m_ref, v_ref, g_ref, d_ref, nm_ref, nv_ref):
    m_new = ADAM_B1 * m_ref[...] + (1.0 - ADAM_B1) * g
    v_new = ADAM_B2 * v_ref[...] + (1.0 - ADAM_B2) * jnp.square(g)
    m_hat = m_new / (1.0 - ADAM_B1 ** ADAM_STEP)
    v_hat = v_new / (1.0 - ADAM_B2 ** ADAM_STEP)
    g_ref[...] = g
    d_ref[...] = -ADAM_LR * (m_hat / (jnp.sqrt(v_hat) + ADAM_EPS) + ADAM_WD * w_ref[...])
    nm_ref[...] = m_new
    nv_ref[...] = v_new


def _adamw(parts, w, m, v, name):
    P, R, W = parts.shape
    tr = _pick(R, (368, 192, 64, 16, 8))
    spec = pl.BlockSpec((tr, W), lambda i: (i, 0))
    return pl.pallas_call(
        functools.partial(_adamw_body), name=name, grid=(R // tr,),
        in_specs=[pl.BlockSpec((P, tr, W), lambda i: (0, i, 0)), spec, spec, spec], out_specs=[spec] * 4,
        out_shape=[jax.ShapeDtypeStruct((R, W), F32)] * 4,
        compiler_params=_cparams(("parallel",)),
    )(parts, w, m, v)


def _adamw_layers(parts, w, m, v, name):
    P, B, C_ = parts[0].shape
    tb = _pick(B, (256, 128))
    nb = B // tb

    def body(*refs):
        p_refs, rest = refs[:DEPTH], refs[DEPTH:]
        a = pl.program_id(0)
        for l in range(DEPTH):
            @pl.when(a == l)
            def _():
                _adamw_body(p_refs[l], *[r.at[0] for r in rest])

    spec = pl.BlockSpec((1, tb, C_), lambda a, i: (a, i, 0))

    def part_spec(l):
        return pl.BlockSpec((P, tb, C_), lambda a, i: (0, jnp.where(a == l, i, jnp.where(a < l, 0, nb - 1)), 0))

    return pl.pallas_call(
        body, name=name, grid=(DEPTH, nb),
        in_specs=[part_spec(l) for l in range(DEPTH)] + [spec, spec, spec], out_specs=[spec] * 4,
        out_shape=[jax.ShapeDtypeStruct((DEPTH, B, C_), F32)] * 4,
        compiler_params=_cparams(("arbitrary", "arbitrary")),
    )(*parts, w, m, v)


VEC_GROUPS = (("norm1_g", "norm2_g"), ("conv_b", "conv_ln_g", "conv_ln_b", "hgrn_lb_logits", "hgrn_norm_g"),
              ("q_a_norm_g",), ("kv_a_norm_g",), ("q_norm_g", "k_norm_g"))
SMALL_NAMES = tuple(n for grp in VEC_GROUPS for n in grp) + ("meta", "conv_w")


def _adamw_small(own, lands, wts, mom, var, name):
    n_in = len(own)

    def body(*refs):
        own_r, land_r = refs[:n_in], refs[n_in:2 * n_in]
        rest = iter(refs[2 * n_in:])
        wmv = {n: (next(rest), next(rest), next(rest)) for n in SMALL_NAMES}
        outs = {n: (next(rest), next(rest), next(rest), next(rest)) for n in SMALL_NAMES}
        loss_ref = next(rest)
        x, y, c = _mesh_pos()
        me = 4 * x + 2 * y + c

        def total(k):
            acc = None
            for s in range(N_DEV):
                v = jnp.where(me == s, own_r[k][...], land_r[k][s])
                acc = v if acc is None else acc + v
            return acc

        for k, grp in enumerate(VEC_GROUPS):
            tot = total(k)
            for j, n in enumerate(grp):
                _adamw_apply(tot[DEPTH * j:DEPTH * (j + 1)], *wmv[n], *outs[n])
        loss_ref[...] = total(len(VEC_GROUPS))
        _adamw_apply(total(n_in - 2), *wmv["meta"], *outs["meta"])
        _adamw_apply(total(n_in - 1), *wmv["conv_w"], *outs["conv_w"])

    args = list(own) + list(lands) + [d[n] for n in SMALL_NAMES for d in (wts, mom, var)]
    out_shape = [jax.ShapeDtypeStruct(wts[n].shape, F32) for n in SMALL_NAMES for _ in range(4)]
    res = pl.pallas_call(body, name=name, out_shape=out_shape + [jax.ShapeDtypeStruct((1, 128), F32)])(*args)
    out = {}
    for i, n in enumerate(SMALL_NAMES):
        for j, kind in enumerate(("grad_", "delta_", "new_m_", "new_v_")):
            out[kind + n] = res[4 * i + j]
    return out, res[-1]


PACK_W = 1024
BIG = (("w_in", (DEPTH, D_MODEL, N_IN // N_DEV), 2), ("w_conv_out", (DEPTH, CONV_DIM, D_MODEL // N_DEV), 2),
       ("w_uq", (DEPTH, Q_RANK, HEADS * QK_DIM // N_DEV), 2), ("w_ukv", (DEPTH, KV_RANK, HEADS * (NOPE + V_DIM) // N_DEV), 2),
       ("w_attn_out", (DEPTH, HEADS * V_DIM, D_MODEL // N_DEV), 2), ("w_hgrn_out", (DEPTH, 512, D_MODEL // N_DEV), 2),
       ("w_out", (DEPTH, D_MODEL // N_DEV, D_MODEL), 1), ("w_ff1", (DEPTH, D_MODEL, D_FF // N_DEV), 2),
       ("w_ff2", (DEPTH, D_FF // N_DEV, D_MODEL), 1))
SMALL_SHARDED = (("meta", (N_META, D_MODEL // N_DEV), 1), ("conv_w", (DEPTH, CONV_K, CONV_DIM // N_DEV), 2))
REPLICATED = (("norm1_g", (DEPTH, D_MODEL)), ("conv_b", (DEPTH, CONV_DIM)), ("conv_ln_g", (DEPTH, CONV_DIM)),
              ("conv_ln_b", (DEPTH, CONV_DIM)), ("q_a_norm_g", (DEPTH, Q_RANK)), ("kv_a_norm_g", (DEPTH, KV_RANK)),
              ("q_norm_g", (DEPTH, QK_DIM)), ("k_norm_g", (DEPTH, QK_DIM)), ("hgrn_lb_logits", (DEPTH, 512)),
              ("hgrn_norm_g", (DEPTH, 512)), ("norm2_g", (DEPTH, D_MODEL)))
WEIGHT_ORDER = ("meta", "norm1_g", "w_in", "conv_w", "conv_b", "conv_ln_g", "conv_ln_b", "w_conv_out", "q_a_norm_g", "w_uq",
                "kv_a_norm_g", "w_ukv", "q_norm_g", "k_norm_g", "w_attn_out", "hgrn_lb_logits", "hgrn_norm_g", "w_hgrn_out",
                "w_out", "norm2_g", "w_ff1", "w_ff2")


def _rows_for(n_elems, mult):
    rows = -(-n_elems // PACK_W)
    return -(-rows // mult) * mult


def _pack(arrays, dtype, mult, lead=()):
    nl = len(lead)
    flat = jnp.concatenate([a.reshape(lead + (-1,)).astype(dtype) for a in arrays], axis=nl)
    rows = _rows_for(flat.shape[nl], mult)
    flat = jnp.pad(flat, [(0, 0)] * nl + [(0, rows * PACK_W - flat.shape[nl])])
    return flat.reshape(lead + (rows, PACK_W))


def _unpack(pack, shapes, lead=()):
    nl = len(lead)
    flat = pack.reshape(lead + (-1,))
    out, off = [], 0
    for shp in shapes:
        n = int(np.prod(shp))
        out.append(lax.slice_in_dim(flat, off, off + n, axis=nl).reshape(lead + tuple(shp)))
        off += n
    return out


def _join_shards(g, axis):
    g = jnp.moveaxis(g, 0, axis)
    shp = g.shape
    return g.reshape(shp[:axis] + (shp[axis] * shp[axis + 1],) + shp[axis + 2:])


def _cut_shards(a, axis):
    shp = a.shape
    a = a.reshape(shp[:axis] + (N_DEV, shp[axis] // N_DEV) + shp[axis + 1:])
    return jnp.moveaxis(a, axis, 0)


def kernel(x, meta, norm1_g, w_in, conv_w, conv_b, conv_ln_g, conv_ln_b, w_conv_out, q_a_norm_g, w_uq, kv_a_norm_g, w_ukv, q_norm_g, k_norm_g, w_attn_out, hgrn_lb_logits, hgrn_norm_g, w_hgrn_out, w_out, norm2_g, w_ff1, w_ff2, loss_target, m_meta, m_norm1_g, m_w_in, m_conv_w, m_conv_b, m_conv_ln_g, m_conv_ln_b, m_w_conv_out, m_q_a_norm_g, m_w_uq, m_kv_a_norm_g, m_w_ukv, m_q_norm_g, m_k_norm_g, m_w_attn_out, m_hgrn_lb_logits, m_hgrn_norm_g, m_w_hgrn_out, m_w_out, m_norm2_g, m_w_ff1, m_w_ff2, v_meta, v_norm1_g, v_w_in, v_conv_w, v_conv_b, v_conv_ln_g, v_conv_ln_b, v_w_conv_out, v_q_a_norm_g, v_w_uq, v_kv_a_norm_g, v_w_ukv, v_q_norm_g, v_k_norm_g, v_w_attn_out, v_hgrn_lb_logits, v_hgrn_norm_g, v_w_hgrn_out, v_w_out, v_norm2_g, v_w_ff1, v_w_ff2):
    args = dict(locals())
    wts = {n: args[n] for n in WEIGHT_ORDER}
    mom = {n: args["m_" + n] for n in WEIGHT_ORDER}
    var = {n: args["v_" + n] for n in WEIGHT_ORDER}
    xi, yi, ci = _mesh_pos()
    me = 4 * xi + 2 * yi + ci

    shard = lambda l: [wts[n][l].astype(BF16) for n, _, _ in BIG]
    assert BIG[0][0] == "w_in"
    gathered = _all_gather(shard(0)[:1] + [_pack([wts[n] for n, _, _ in SMALL_SHARDED], F32, 8)], "gather_layer0")
    small = dict(zip([n for n, _, _ in SMALL_SHARDED],
                     [_join_shards(g, axis) for (_, _, axis), g in
                      zip(SMALL_SHARDED, _unpack(gathered[-1], [s for _, s, _ in SMALL_SHARDED], (N_DEV,)))]))
    rest0 = _copy_start(shard(0)[1:], True, "gather_rest0_start", 11, NEAR_RELS)
    pending = []

    def joined(mats, names_axes):
        full = {}
        for (n, _, axis), g in zip(names_axes, mats):
            if n == "w_in":
                full["w_in_padded"] = _w_in_from_shards(g)
            else:
                full[n] = _join_shards(g, axis - 1)
        return full

    def rest_of_layer0(u):
        own, lands = _copy_wait(rest0[0], rest0[1], rest0[2], rest0[3], u, True, "gather_rest0_wait", NEAR_RELS)
        lands = _forward_to_sibling(lands, "gather_rest0_forward")
        full = joined([lax.dynamic_update_index_in_dim(z, s, me, 0) for z, s in zip(lands, own)], BIG[1:])
        pending.append(_copy_start(shard(1), True, "gather_layer1_start", 5))
        full["w_conv_out"] = _after(full["w_conv_out"], pending[0][4])
        return _layer_weights(full)

    def layer_weights(l, xp):
        full = {n: wts[n][l] for n, _ in REPLICATED}
        full["conv_w"] = small["conv_w"][l]
        if l == 0:
            full.update(joined(gathered[:1], BIG[:1]))
            full["norm1_g"] = _after(full["norm1_g"], rest0[4])
            full["late"] = rest_of_layer0
        else:
            s_sems, r_sems, sent, lands, _ = pending[0]
            own, lands = _copy_wait(s_sems, r_sems, sent, lands, xp, True, "gather_layer1_wait")
            full.update(joined([lax.dynamic_update_index_in_dim(z, s, me, 0) for z, s in zip(lands, own)], BIG))
        return full, xp

    big_names = [n for n, _, _ in BIG]
    early = [n for n in big_names if n in ("w_out", "w_ff1", "w_ff2")]
    late = [n for n in big_names if n not in early]
    cut = lambda g, names: [(g[n + "_shards"] if n + "_shards" in g else _cut_shards(g[n], axis - 1)).astype(BF16)
                            for n, _, axis in BIG if n in names]
    layer_grads = [None] * DEPTH
    flight = {}

    def layer_mid(l, g, w_out):
        if l == 0:
            flight["l0_early"] = _copy_start(cut(g, early), False, "scatter_layer0_early_start", 7)
            w_out = _after(w_out, flight["l0_early"][4])
        return w_out

    def layer_done(l, g, dx):
        layer_grads[l] = g
        if l == 1:
            flight["l1"] = _copy_start(cut(g, big_names), False, "scatter_l1_start", 6)
            dx = _after(dx, flight["l1"][4])
        return dx

    def layer_matrices(l, mats, norm_g):
        if l == 0:
            flight["l0_late"] = _copy_start(cut(mats, late), False, "scatter_l0_late_start", 8)
            norm_g = _after(norm_g, flight["l0_late"][4])
        return norm_g

    loss, grad_x, g_meta, g_lb = _run_step(x[0], loss_target[0], small["meta"], wts["hgrn_lb_logits"],
                                           layer_weights, layer_done, layer_mid, layer_matrices)

    grads = {k: jnp.stack([layer_grads[l][k] for l in range(DEPTH)]) for k in layer_grads[0]
             if k not in big_names and not k.endswith("_shards")}
    grads["hgrn_lb_logits"] = g_lb
    own = [jnp.concatenate([grads[n] for n in grp], axis=0) for grp in VEC_GROUPS]
    own.append(jnp.broadcast_to(loss.reshape(1, 1), (1, 128)))
    flight["small"] = _copy_start(own, True, "gather_small_grads_start", 9)
    cuts = [_cut_shards(g_meta, 1), _cut_shards(grads["conv_w"], 2)]
    flight["small_x"] = _copy_start(cuts, False, "scatter_small_grads_start", 10)
    started = flight["small_x"][4]

    def arrive(key, names, after):
        s_sems, r_sems, sent, lands, _ = flight[key]
        sent, lands = _copy_wait(s_sems, r_sems, sent, lands, after, False, f"scatter_{key}_wait")
        return {n: lax.dynamic_update_index_in_dim(z, lax.dynamic_index_in_dim(s, me, 0, keepdims=False), me, 0)
                for n, z, s in zip(names, lands, sent)}

    out = {}

    def update(names, recv0, recv1):
        for n in names:
            res4 = _adamw_layers([recv0[n], recv1[n]], wts[n], mom[n], var[n], "adamw_" + n)
            for kind, a in zip(("grad_", "delta_", "new_m_", "new_v_"), res4):
                out[kind + n] = a

    recv1 = arrive("l1", big_names, started)
    recv0 = arrive("l0_early", early, started)
    update(early, recv0, recv1)

    s_sems, r_sems, sent, lands, _ = flight["small"]
    updated = lax.optimization_barrier(tuple(out["grad_" + n] for n in early))
    own, lands = _copy_wait(s_sems, r_sems, sent, lands, updated[0], True, "gather_small_grads_wait")
    s_sems, r_sems, sent, lands_x, _ = flight["small_x"]
    sent, lands_x = _copy_wait(s_sems, r_sems, sent, lands_x, updated[0], False, "scatter_small_grads_wait")
    own += [lax.dynamic_index_in_dim(s, me, 0, keepdims=False) for s in sent]
    small_out, loss = _adamw_small(own, lands + lands_x, wts, mom, var, "adamw_small")
    out.update(small_out)
    loss = loss[0, 0]

    recv0 = arrive("l0_late", late, small_out["grad_norm1_g"])
    update(late, recv0, recv1)

    res = [loss, grad_x[None]]
    for kind in ("grad_", "delta_", "new_m_", "new_v_"):
        res += [out[kind + n] for n in WEIGHT_ORDER]
    return tuple(res)
```

```python
import functools

import numpy as np
import jax
import jax.numpy as jnp
from jax import lax
from jax.experimental import pallas as pl
from jax.experimental.pallas import tpu as pltpu

F32 = jnp.float32
BF16 = jnp.bfloat16

D_MODEL = 1024
DEPTH = 2
N_META = 16
PAD_FRONT = 112
ROW0 = PAD_FRONT + N_META
EPS = 1e-6
GATE_CLAMP = 1.0 - 1e-6
CONV_DIM = 512
CONV_K = 31
HEADS = 8
Q_RANK = 256
KV_RANK = 128
NOPE = 64
ROPE = 32
V_DIM = 64
QK_DIM = NOPE + ROPE
HEAD_W = 128
ROPE_BASE = 10000.0
HG_HEADS = 4
HG_DK = 128
HG_DV = 128
HG_CHUNK = 64
D_FF = 4096
N_IN = 6560
C_CONV_A, C_CONV_G, C_GATE, C_CQ, C_CKV, C_KR, C_HQ, C_HF, C_HI, C_HG = (
    0, 512, 1024, 4096, 4352, 4480, 4608, 5120, 5632, 6144)
N_IN_P = 6656
O_CQ, O_KR, O_HQ, O_GATE = 1024, 1408, 1440, 3488
KR_LANE = NOPE

ADAM_LR = 0.001
ADAM_B1 = 0.9
ADAM_B2 = 0.999
ADAM_EPS = 1e-08
ADAM_WD = 0.01
ADAM_STEP = 10

N_DEV = 8
VMEM_LIMIT = 56 * 1024 * 1024
MESH = pl.DeviceIdType.MESH


def _pick(n, cands):
    for c in cands:
        if n % c == 0:
            return c
    raise ValueError(f"no tile for {n}")


def _cparams(sem, **kw):
    return pltpu.CompilerParams(dimension_semantics=sem, vmem_limit_bytes=VMEM_LIMIT, **kw)


def _relu2(v):
    return jnp.square(jnp.maximum(v, 0.0))


def _mm(a, b, *, ta=False, tb=False, out_dtype=F32, res=None, a_fn=None, epi=None, name):
    M, K = (a.shape[1], a.shape[0]) if ta else a.shape
    N = b.shape[0] if tb else b.shape[1]
    assert (b.shape[1] if tb else b.shape[0]) == K, (a.shape, b.shape, ta, tb)
    tm = _pick(M, (1056, 1024, 512, 384, 256, 128, 96))
    tn = _pick(N, (1664, 1024, 512, 384, 256, 128))
    tk = _pick(K, (1664, 1056, 1024, 512, 384, 256, 128, 96) if ta else (1664, 1408, 1024, 512, 384, 256, 128))
    nk = K // tk
    dims = (((0 if ta else 1,), (1 if tb else 0,)), ((), ()))
    extras = ([res] if res is not None else []) + ([epi[0]] if epi is not None else [])

    def body(*refs):
        a_ref, b_ref = refs[0], refs[1]
        r_ref = refs[2] if res is not None else None
        e_ref = refs[2 + (res is not None)] if epi is not None else None
        o_ref = refs[2 + len(extras)]
        acc = refs[-1] if nk > 1 else None
        k = pl.program_id(2)
        av = a_ref[...]
        if a_fn is not None:
            av = a_fn(av.astype(F32))
        p = lax.dot_general(av.astype(BF16), b_ref[...].astype(BF16), dims, preferred_element_type=F32)

        def finish(total):
            if e_ref is not None:
                total = epi[1](total, e_ref[...].astype(F32))
            if r_ref is not None:
                total = total + r_ref[...].astype(F32)
            o_ref[...] = total.astype(o_ref.dtype)

        if nk == 1:
            finish(p)
        else:
            @pl.when(k == 0)
            def _():
                acc[...] = p

            @pl.when(k > 0)
            def _():
                acc[...] += p

            @pl.when(k == nk - 1)
            def _():
                finish(acc[...])

    a_spec = pl.BlockSpec((tk, tm), lambda i, j, k: (k, i)) if ta else pl.BlockSpec((tm, tk), lambda i, j, k: (i, k))
    b_spec = pl.BlockSpec((tn, tk), lambda i, j, k: (j, k)) if tb else pl.BlockSpec((tk, tn), lambda i, j, k: (k, j))
    o_spec = pl.BlockSpec((tm, tn), lambda i, j, k: (i, j))
    in_specs = [a_spec, b_spec] + [o_spec] * len(extras)
    args = (a, b) + tuple(extras)
    return pl.pallas_call(
        body, name=name, grid=(M // tm, N // tn, nk), in_specs=in_specs, out_specs=o_spec,
        out_shape=jax.ShapeDtypeStruct((M, N), out_dtype),
        scratch_shapes=[pltpu.VMEM((tm, tn), F32)] if nk > 1 else [],
        compiler_params=_cparams(("parallel", "parallel", "arbitrary")),
    )(*args)


class Row:
    def __init__(self, arr, width=None, col=0, piece=None):
        self.arr = arr
        self.width = arr.shape[1] if width is None else width
        assert col % self.width == 0
        self.blk = col // self.width
        self.piece = self.width if piece is None else piece

    def spec(self, tm):
        blk = self.blk
        return pl.BlockSpec((tm, self.width), lambda i: (i, blk))


def _split(v, piece):
    w = v.shape[-1]
    if piece == w:
        return v
    return [v[:, j * piece:(j + 1) * piece] for j in range(w // piece)]


def _store(ref, val, dtype=None):
    if isinstance(val, (list, tuple)):
        piece = val[0].shape[-1]
        for j, p in enumerate(val):
            ref[:, j * piece:(j + 1) * piece] = p.astype(ref.dtype)
    else:
        ref[...] = val.astype(ref.dtype)


def _row_tile(T):
    return _pick(T, (384, 352, 192, 128))


def _param2d(p):
    return p.reshape(1, -1).astype(F32)


def _rowwise(fn, T, rows, params, outs, name, transposed=False):
    tm = _row_tile(T)
    nr, npar = len(rows), len(params)
    par = [(_param2d(p), piece) for p, piece in params]

    def body(*refs):
        rid = pl.program_id(0) * tm + lax.broadcasted_iota(jnp.int32, (tm, 1), 0)
        rv = [_split(refs[n][...].astype(F32), rows[n].piece) for n in range(nr)]
        pv = [_split(refs[nr + n][...], par[n][1]) for n in range(npar)]
        res = fn(rid, rv, pv)
        for n, val in enumerate(res):
            _store(refs[nr + npar + n], val)
        if transposed:
            refs[-1][...] = res[0].T.astype(refs[-1].dtype)

    out_specs = [pl.BlockSpec((tm, w), lambda i: (i, 0)) for w, _ in outs]
    out_shape = [jax.ShapeDtypeStruct((T, w), dt) for w, dt in outs]
    if transposed:
        out_specs.append(pl.BlockSpec((outs[0][0], tm), lambda i: (0, i)))
        out_shape.append(jax.ShapeDtypeStruct((outs[0][0], T), outs[0][1]))
    return pl.pallas_call(
        body, name=name, grid=(T // tm,),
        in_specs=[r.spec(tm) for r in rows] + [pl.BlockSpec(p.shape, lambda i: (0, 0)) for p, _ in par],
        out_specs=out_specs, out_shape=out_shape,
        compiler_params=_cparams(("parallel",)),
    )(*[r.arr for r in rows], *[p for p, _ in par])


def _rowwise_bwd(fn, T, rows, params, cts, drow, name, add=None):
    tm = _row_tile(T)
    nr, npar, nct = len(rows), len(params), len(cts)
    par = [(_param2d(p), piece) for p, piece in params]
    didx = sorted(drow)
    has_add = add is not None

    def body(*refs):
        i = pl.program_id(0)
        rid = i * tm + lax.broadcasted_iota(jnp.int32, (tm, 1), 0)
        rv = [_split(refs[n][...].astype(F32), rows[n].piece) for n in range(nr)]
        pv = [_split(refs[nr + n][...], par[n][1]) for n in range(npar)]
        cv = [_split(refs[nr + npar + n][...].astype(F32), cts[n].piece) for n in range(nct)]
        base = nr + npar + nct + (1 if has_add else 0)
        d_refs = refs[base:base + len(didx)]
        p_refs = refs[base + len(didx):]

        def g(dvals, pvals):
            full = list(rv)
            for n, v in zip(didx, dvals):
                full[n] = v
            return fn(rid, full, pvals)

        _, vjp = jax.vjp(g, [rv[n] for n in didx], pv)
        d_rows, d_pars = vjp(cv)
        for slot, n in enumerate(didx):
            val = d_rows[slot]
            if has_add and add[0] == n:
                assert not isinstance(val, (list, tuple))
                val = val + refs[nr + npar + nct][...].astype(F32)
            _store(d_refs[slot], val)

        @pl.when(i == 0)
        def _():
            for r in p_refs:
                r[...] = jnp.zeros_like(r)

        for r, val in zip(p_refs, d_pars):
            if isinstance(val, (list, tuple)):
                piece = val[0].shape[-1]
                for j, p in enumerate(val):
                    r[:, j * piece:(j + 1) * piece] += p
            else:
                r[...] += val

    in_specs = ([r.spec(tm) for r in rows] + [pl.BlockSpec(p.shape, lambda i: (0, 0)) for p, _ in par]
                + [c.spec(tm) for c in cts])
    args = [r.arr for r in rows] + [p for p, _ in par] + [c.arr for c in cts]
    if has_add:
        in_specs.append(pl.BlockSpec((tm, rows[add[0]].width), lambda i: (i, 0)))
        args.append(add[1])
    out_specs = ([pl.BlockSpec((tm, rows[n].width), lambda i: (i, 0)) for n in didx]
                 + [pl.BlockSpec(p.shape, lambda i: (0, 0)) for p, _ in par])
    out_shape = ([jax.ShapeDtypeStruct((T, rows[n].width), drow[n]) for n in didx]
                 + [jax.ShapeDtypeStruct(p.shape, F32) for p, _ in par])
    res = pl.pallas_call(
        body, name=name, grid=(T // tm,), in_specs=in_specs, out_specs=out_specs, out_shape=out_shape,
        compiler_params=_cparams(("arbitrary",)),
    )(*args)
    return list(res[:len(didx)]), list(res[len(didx):])


def _f_rms(rid, rv, pv):
    x, g = rv[0], pv[0]
    return [x * lax.rsqrt(jnp.mean(x * x, axis=-1, keepdims=True) + EPS) * g]


def _f_glu(rid, rv, pv):
    a, gt = rv
    return [a * jax.nn.sigmoid(gt) * (rid >= PAD_FRONT).astype(F32)]


def _f_lnsilu(rid, rv, pv):
    x = rv[0]
    g, b = pv
    mu = jnp.mean(x, axis=-1, keepdims=True)
    xc = x - mu
    y = xc * lax.rsqrt(jnp.mean(xc * xc, axis=-1, keepdims=True) + EPS) * g + b
    return [y * jax.nn.sigmoid(y)]


@functools.partial(jax.custom_vjp, nondiff_argnums=(1,))
def _lane_roll(x, shift):
    return pltpu.roll(x, shift, 1)


def _lane_roll_fwd(x, shift):
    return pltpu.roll(x, shift, 1), None


def _lane_roll_bwd(shift, _, g):
    return (pltpu.roll(g, (HEAD_W - shift) % HEAD_W, 1),)


_lane_roll.defvjp(_lane_roll_fwd, _lane_roll_bwd)


def _head_norm_rope(xh, g, c, s1, s2):
    y = xh * lax.rsqrt(jnp.sum(xh * xh, axis=-1, keepdims=True) * (1.0 / QK_DIM) + EPS) * g
    half = ROPE // 2
    return y * c + _lane_roll(y, HEAD_W - half) * s1 + _lane_roll(y, half) * s2


def _f_qrope(rid, rv, pv):
    q, c, s1, s2 = rv
    return [[_head_norm_rope(qh, pv[0], c, s1, s2) * ATT_SCALE for qh in q]]


def _f_krope(rid, rv, pv):
    k, kr, c, s1, s2 = rv
    return [[_head_norm_rope(kh + kr, pv[0], c, s1, s2) for kh in k]]


def _f_hgrn_prep(rid, rv, pv):
    hf, hi = rv
    m = (rid >= PAD_FRONT).astype(F32)
    kk = (1.0 - pv[0]) * jax.nn.sigmoid(-hf) * m
    lf = jnp.log1p(-jnp.minimum(kk, GATE_CLAMP))
    vv = hi * jax.nn.sigmoid(hi) * m
    return [kk, lf, vv]


def _f_hgrn_out(rid, rv, pv):
    o, hg = rv
    ng = pv[0]
    out = []
    for oh, gh, nh in zip(o, hg, ng):
        y = oh * lax.rsqrt(jnp.mean(oh * oh, axis=-1, keepdims=True) + EPS) * nh
        out.append(y * (gh * jax.nn.sigmoid(gh)))
    return [out]


def _f_mix(rid, rv, pv):
    g0, g1, g2, ya, yb, yc = rv
    return [jax.nn.sigmoid(g0) * ya + jax.nn.sigmoid(g1) * yb + jax.nn.sigmoid(g2) * yc]


def _f_relu2(rid, rv, pv):
    return [jnp.square(jax.nn.relu(rv[0]))]


def _loss_head(x2, tgt, T):
    tm = _row_tile(T)

    def body(x_ref, t_ref, dx_ref, l_ref):
        i = pl.program_id(0)
        rid = i * tm + lax.broadcasted_iota(jnp.int32, (tm, 1), 0)
        diff = (x_ref[...] - t_ref[...]) * (rid >= ROW0).astype(F32)
        dx_ref[...] = diff * (1.0 / D_MODEL)

        @pl.when(i == 0)
        def _():
            l_ref[...] = jnp.zeros_like(l_ref)

        l_ref[...] += jnp.sum(diff * diff, axis=0, keepdims=True)

    spec = pl.BlockSpec((tm, D_MODEL), lambda i: (i, 0))
    return pl.pallas_call(
        body, name="loss_head", grid=(T // tm,), in_specs=[spec, spec],
        out_specs=[spec, pl.BlockSpec((1, D_MODEL), lambda i: (0, 0))],
        out_shape=[jax.ShapeDtypeStruct((T, D_MODEL), F32), jax.ShapeDtypeStruct((1, D_MODEL), F32)],
        compiler_params=_cparams(("arbitrary",)),
    )(x2, tgt)


HALO = 32


CONV_ROWS = 64


def _conv_lanes():
    return [slice(c, c + 128) for c in range(0, CONV_DIM, 128)]


def _conv_tile(T):
    return _pick(T, (384, 128))


def _conv_fwd(h, w, b, T, name):
    tr = _conv_tile(T)
    ratio = tr // HALO
    wp = jnp.zeros((HALO, CONV_DIM), F32).at[:CONV_K].set(w)

    def body(m_ref, h_ref, w_ref, b_ref, o_ref, win):
        i = pl.program_id(0)
        win[0:HALO, :] = h_ref[...] * (i > 0).astype(F32)
        win[HALO:, :] = m_ref[...]
        for cs in _conv_lanes():
            wv, bv = w_ref[:, cs], b_ref[:, cs]
            for r0 in range(0, tr, CONV_ROWS):
                acc = jnp.broadcast_to(bv, (CONV_ROWS, 128))
                for k in range(CONV_K):
                    acc = acc + wv[k:k + 1] * win[pl.ds(HALO - (CONV_K - 1) + k + r0, CONV_ROWS), cs]
                o_ref[r0:r0 + CONV_ROWS, cs] = acc

    return pl.pallas_call(
        body, name=name, grid=(T // tr,),
        in_specs=[pl.BlockSpec((tr, CONV_DIM), lambda i: (i, 0)),
                  pl.BlockSpec((HALO, CONV_DIM), lambda i: (jnp.maximum(i * ratio - 1, 0), 0)),
                  pl.BlockSpec((HALO, CONV_DIM), lambda i: (0, 0)),
                  pl.BlockSpec((1, CONV_DIM), lambda i: (0, 0))],
        out_specs=pl.BlockSpec((tr, CONV_DIM), lambda i: (i, 0)),
        out_shape=jax.ShapeDtypeStruct((T, CONV_DIM), F32),
        scratch_shapes=[pltpu.VMEM((tr + HALO, CONV_DIM), F32)],
        compiler_params=_cparams(("parallel",)),
    )(h, h, wp, _param2d(b))


def _conv_bwd(h, w, dy, T, name):
    tr = _conv_tile(T)
    ratio = tr // HALO
    n_t = T // tr
    last_halo = T // HALO - 1
    wp = jnp.zeros((HALO, CONV_DIM), F32).at[:CONV_K].set(w)

    def body(hm_ref, hh_ref, dm_ref, dh_ref, w_ref, dx_ref, dw_ref, db_ref, hwin, dwin):
        i = pl.program_id(0)
        hwin[0:HALO, :] = hh_ref[...] * (i > 0).astype(F32)
        hwin[HALO:, :] = hm_ref[...]
        dwin[0:tr, :] = dm_ref[...]
        dwin[tr:, :] = dh_ref[...] * (i < n_t - 1).astype(F32)

        @pl.when(i == 0)
        def _():
            dw_ref[...] = jnp.zeros_like(dw_ref)
            db_ref[...] = jnp.zeros_like(db_ref)

        db_ref[...] += jnp.sum(dm_ref[...], axis=0, keepdims=True)
        fold = lambda a: functools.reduce(jnp.add, [a[r:r + 8] for r in range(0, CONV_ROWS, 8)])
        for cs in _conv_lanes():
            wv = w_ref[:, cs]
            dws = [jnp.zeros((8, 128), F32) for _ in range(CONV_K)]
            for r0 in range(0, tr, CONV_ROWS):
                acc = jnp.zeros((CONV_ROWS, 128), F32)
                for k in range(CONV_K):
                    acc = acc + wv[k:k + 1] * dwin[pl.ds(CONV_K - 1 - k + r0, CONV_ROWS), cs]
                dx_ref[r0:r0 + CONV_ROWS, cs] = acc
                dy_t = dm_ref[r0:r0 + CONV_ROWS, cs]
                for k in range(CONV_K):
                    dws[k] = dws[k] + fold(dy_t * hwin[pl.ds(HALO - (CONV_K - 1) + k + r0, CONV_ROWS), cs])
            for k in range(CONV_K):
                dw_ref[k:k + 1, cs] += jnp.sum(dws[k], axis=0, keepdims=True)

    main = pl.BlockSpec((tr, CONV_DIM), lambda i: (i, 0))
    return pl.pallas_call(
        body, name=name, grid=(n_t,),
        in_specs=[main,
                  pl.BlockSpec((HALO, CONV_DIM), lambda i: (jnp.maximum(i * ratio - 1, 0), 0)),
                  main,
                  pl.BlockSpec((HALO, CONV_DIM), lambda i: (jnp.minimum((i + 1) * ratio, last_halo), 0)),
                  pl.BlockSpec((HALO, CONV_DIM), lambda i: (0, 0))],
        out_specs=[main, pl.BlockSpec((HALO, CONV_DIM), lambda i: (0, 0)), pl.BlockSpec((1, CONV_DIM), lambda i: (0, 0))],
        out_shape=[jax.ShapeDtypeStruct((T, CONV_DIM), F32), jax.ShapeDtypeStruct((HALO, CONV_DIM), F32),
                   jax.ShapeDtypeStruct((1, CONV_DIM), F32)],
        scratch_shapes=[pltpu.VMEM((tr + HALO, CONV_DIM), F32), pltpu.VMEM((tr + HALO, CONV_DIM), F32)],
        compiler_params=_cparams(("arbitrary",)),
    )(h, h, dy, dy, wp)


NEG = -1e30
ATT_SCALE = QK_DIM ** -0.5
_NT = (((1,), (1,)), ((), ()))
_TN = (((0,), (0,)), ((), ()))


def _att_blk(T):
    return _pick(T, (384, 128))


def _att_mask(i, j, blk):
    kpos = j * blk + lax.broadcasted_iota(jnp.int32, (blk, blk), 0)
    qpos = i * blk + lax.broadcasted_iota(jnp.int32, (blk, blk), 1)
    return (kpos <= qpos) & (kpos >= PAD_FRONT)


def _t32(a):
    return a.astype(F32).T.astype(BF16)


def _attn_fwd(q, k, v, T, name):
    blk = _att_blk(T)
    nq = T // blk

    def body(q_ref, k_ref, v_ref, o_ref, lse_ref, vt):
        i = pl.program_id(1)

        @pl.when(i == 0)
        def _():
            def tr(j, c):
                vt[j] = _t32(v_ref[pl.ds(pl.multiple_of(j * blk, blk), blk), :])
                return c

            lax.fori_loop(0, nq, tr, 0)

        qb = q_ref[...]

        def step(js, carry, masked):
            m, l, acc = carry
            ss = []
            for j in js:
                kb = k_ref[pl.ds(pl.multiple_of(j * blk, blk), blk), :]
                s = lax.dot_general(kb, qb, _NT, preferred_element_type=F32)
                ss.append(jnp.where(_att_mask(i, j, blk), s, NEG) if masked else s)
            m_new = m
            for s in ss:
                m_new = jnp.maximum(m_new, jnp.max(s, axis=0, keepdims=True))
            alpha = jnp.exp(m - m_new)
            l = alpha * l
            acc = alpha * acc
            for j, s in zip(js, ss):
                p = jnp.exp(s - m_new)
                l = l + jnp.sum(p, axis=0, keepdims=True)
                acc = acc + jnp.dot(vt[j], p.astype(BF16), preferred_element_type=F32)
            return m_new, l, acc

        init = (jnp.full((1, blk), NEG, F32), jnp.zeros((1, blk), F32), jnp.zeros((HEAD_W, blk), F32))
        later = jnp.minimum(i, 1)
        carry = lax.fori_loop(0, 1 - later, lambda t, c: step([i], c, True), init)
        carry = lax.fori_loop(0, later, lambda t, c: step([i, 0], c, True), carry)
        n_free = jnp.maximum(i - 1, 0)
        n4 = n_free // 4
        carry = lax.fori_loop(0, n4, lambda t, c: step([1 + 4 * t + d for d in range(4)], c, False), carry)
        rest = n_free - 4 * n4
        carry = lax.fori_loop(0, rest // 2, lambda t, c: step([i - rest, i - rest + 1], c, False), carry)
        m, l, acc = lax.fori_loop(0, rest % 2, lambda t, c: step([i - 1], c, False), carry)
        o_ref[...] = (acc / l).T.astype(o_ref.dtype)
        lse_ref[0, 0] = m + jnp.log(l)

    full = pl.BlockSpec((T, HEAD_W), lambda h, i: (0, h))
    return pl.pallas_call(
        body, name=name, grid=(HEADS, nq),
        in_specs=[pl.BlockSpec((blk, HEAD_W), lambda h, i: (i, h)), full, full],
        out_specs=[pl.BlockSpec((blk, HEAD_W), lambda h, i: (i, h)),
                   pl.BlockSpec((1, 1, 1, blk), lambda h, i: (h, i, 0, 0))],
        out_shape=[jax.ShapeDtypeStruct((T, HEADS * HEAD_W), BF16), jax.ShapeDtypeStruct((HEADS, nq, 1, blk), F32)],
        scratch_shapes=[pltpu.VMEM((nq, HEAD_W, blk), BF16)],
        compiler_params=_cparams(("parallel", "arbitrary")),
    )(q, k, v)


def _attn_bwd(q, k, v, o, lse, do, T, name):
    blk = _att_blk(T)
    nq = T // blk

    def body(q_ref, k_ref, v_ref, o_ref, lse_ref, do_ref, dq_ref, dk_ref, dv_ref, delta, dqt, dk_acc, dv_acc):
        j = pl.program_id(1)

        @pl.when(j == 0)
        def _():
            dqt[...] = jnp.zeros_like(dqt)

            def dstep(i, c):
                r0 = pl.multiple_of(i * blk, blk)
                prod = do_ref[pl.ds(r0, blk), :].astype(F32) * o_ref[pl.ds(r0, blk), :].astype(F32)
                delta[i] = jnp.sum(prod.T, axis=0, keepdims=True)
                return c

            lax.fori_loop(0, nq, dstep, 0)

        kb = k_ref[...]
        vb = v_ref[...]
        kbt = _t32(kb)
        dk_acc[...] = jnp.zeros_like(dk_acc)
        dv_acc[...] = jnp.zeros_like(dv_acc)

        def step(qs, masked):
            dvs, dks = [], []
            for i in qs:
                r0 = pl.multiple_of(i * blk, blk)
                qb = q_ref[pl.ds(r0, blk), :]
                dob = do_ref[pl.ds(r0, blk), :]
                s = lax.dot_general(kb, qb, _NT, preferred_element_type=F32)
                p = jnp.exp(s - lse_ref[0, i])
                if masked:
                    p = jnp.where(_att_mask(i, j, blk), p, 0.0)
                dvs.append(jnp.dot(p.astype(BF16), dob, preferred_element_type=F32))
                dp = lax.dot_general(vb, dob, _NT, preferred_element_type=F32)
                ds = (p * (dp - delta[i])).astype(BF16)
                dks.append(jnp.dot(ds, qb, preferred_element_type=F32))
                dqt[i] += jnp.dot(kbt, ds, preferred_element_type=F32)
            dv_acc[...] += functools.reduce(jnp.add, dvs)
            dk_acc[...] += functools.reduce(jnp.add, dks)

        def loop(lo, masked):
            n = nq - lo
            n3 = n // 3
            rest = n - 3 * n3

            def triple(t, c):
                step([lo + 3 * t + d for d in range(3)], masked)
                return c

            def pair(t, c):
                step([nq - 2, nq - 1], masked)
                return c

            def last(t, c):
                step([nq - 1], masked)
                return c

            lax.fori_loop(0, n3, triple, 0)
            lax.fori_loop(0, rest // 2, pair, 0)
            lax.fori_loop(0, rest % 2, last, 0)

        @pl.when(j == 0)
        def _():
            loop(0, True)

        @pl.when(j > 0)
        def _():
            step([j], True)
            loop(j + 1, False)

        dk_ref[...] = dk_acc[...].astype(dk_ref.dtype)
        dv_ref[...] = dv_acc[...].astype(dv_ref.dtype)

        @pl.when(j == nq - 1)
        def _():
            def wstep(i, c):
                dq_ref[pl.ds(pl.multiple_of(i * blk, blk), blk), :] = dqt[i].T
                return c

            lax.fori_loop(0, nq, wstep, 0)

    full = pl.BlockSpec((T, HEAD_W), lambda h, j: (0, h))
    kblk = pl.BlockSpec((blk, HEAD_W), lambda h, j: (j, h))
    wide = (T, HEADS * HEAD_W)
    return pl.pallas_call(
        body, name=name, grid=(HEADS, nq),
        in_specs=[full, kblk, kblk, full, pl.BlockSpec((1, nq, 1, blk), lambda h, j: (h, 0, 0, 0)), full],
        out_specs=[full, kblk, kblk],
        out_shape=[jax.ShapeDtypeStruct(wide, F32), jax.ShapeDtypeStruct(wide, BF16), jax.ShapeDtypeStruct(wide, BF16)],
        scratch_shapes=[pltpu.VMEM((nq, 1, blk), F32), pltpu.VMEM((nq, HEAD_W, blk), F32),
                        pltpu.VMEM((blk, HEAD_W), F32), pltpu.VMEM((blk, HEAD_W), F32)],
        compiler_params=_cparams(("parallel", "arbitrary")),
    )(q, k, v, o, lse, do)


HG_NB = 6
C = HG_CHUNK
_HI = lax.Precision.HIGHEST


def _tri(lower):
    r = lax.broadcasted_iota(jnp.int32, (C, C), 0)
    c = lax.broadcasted_iota(jnp.int32, (C, C), 1)
    return ((c <= r) if lower else (c >= r)).astype(F32)


HG_SUB = 8
N_SUB = C // HG_SUB


def _hg_split_decay(b, I, rid):
    lo = I * HG_SUB
    r = b[lo:lo + 1]
    eq = jnp.exp(b[lo:lo + HG_SUB] - r)
    ek = jnp.concatenate([jnp.exp(jnp.minimum(r - b[:lo], 0.0)), jnp.zeros((C - lo, HG_DK), F32)], axis=0)
    return eq, ek


def _hg_intra_fwd(q, k, v, b):
    rid = lax.broadcasted_iota(jnp.int32, (C, 1), 0)
    tid = lax.broadcasted_iota(jnp.int32, (HG_SUB, 1), 0)
    a_rows = [jnp.zeros((HG_SUB, C), F32)]
    blocks = []
    for I in range(N_SUB):
        lo = I * HG_SUB
        q_i, b_i = q[lo:lo + HG_SUB], b[lo:lo + HG_SUB]
        if I > 0:
            eq, ek = _hg_split_decay(b, I, rid)
            a_rows.append(lax.dot_general((q_i * eq).astype(BF16), (k * ek).astype(BF16), _NT,
                                          preferred_element_type=F32))
        o_i = jnp.zeros((HG_SUB, HG_DV), F32)
        for s in range(HG_SUB):
            r = lo + s
            e = jnp.exp(jnp.minimum(b_i - b[r:r + 1], 0.0))
            a = jnp.sum(q_i * k[r:r + 1] * e, axis=-1, keepdims=True)
            o_i = o_i + jnp.where(tid >= s, a, 0.0) * v[r:r + 1]
        blocks.append(o_i)
    a_off = jnp.concatenate(a_rows, axis=0).astype(BF16)
    return jnp.dot(a_off, v.astype(BF16), preferred_element_type=F32) + jnp.concatenate(blocks, axis=0)


def _hg_intra_bwd(q, k, v, b, do, dk_s, dv_s):
    rid = lax.broadcasted_iota(jnp.int32, (C, 1), 0)
    tid = lax.broadcasted_iota(jnp.int32, (HG_SUB, 1), 0)
    da_all = lax.dot_general(do.astype(BF16), v.astype(BF16), _NT, preferred_element_type=F32)
    a_rows = [jnp.zeros((HG_SUB, C), F32)]
    dq_blocks = []
    dk = jnp.zeros((C, HG_DK), F32)
    for I in range(N_SUB):
        lo = I * HG_SUB
        q_i, b_i, do_i = q[lo:lo + HG_SUB], b[lo:lo + HG_SUB], do[lo:lo + HG_SUB]
        dq_i = jnp.zeros((HG_SUB, HG_DK), F32)
        if I > 0:
            eq, ek = _hg_split_decay(b, I, rid)
            qs, ks = (q_i * eq).astype(BF16), (k * ek).astype(BF16)
            a_rows.append(lax.dot_general(qs, ks, _NT, preferred_element_type=F32))
            da = da_all[lo:lo + HG_SUB].astype(BF16)
            dq_i = jnp.dot(da, ks, preferred_element_type=F32) * eq
            dk = dk + lax.dot_general(da, qs, _TN, preferred_element_type=F32) * ek
        for s in range(HG_SUB):
            r = lo + s
            e = jnp.where(tid >= s, jnp.exp(jnp.minimum(b_i - b[r:r + 1], 0.0)), 0.0)
            a = jnp.sum(q_i * k[r:r + 1] * e, axis=-1, keepdims=True)
            g = jnp.sum(do_i * v[r:r + 1], axis=-1, keepdims=True) * e
            dq_i = dq_i + g * k[r:r + 1]
            dk_s[r:r + 1, :] = jnp.sum(g * q_i, axis=0, keepdims=True)
            dv_s[r:r + 1, :] = jnp.sum(a * do_i, axis=0, keepdims=True)
        dq_blocks.append(dq_i)
    a_off = jnp.concatenate(a_rows, axis=0).astype(BF16)
    dv = lax.dot_general(a_off, do.astype(BF16), _TN, preferred_element_type=F32)
    return jnp.concatenate(dq_blocks, axis=0), dk + dk_s[...], dv + dv_s[...]


def _hgrn_fwd(u, kk, lf, vv, T, name):
    nb = _pick(T // C, (HG_NB, 3, 2, 1))
    rows = nb * C
    qblk = C_HQ // HG_DK

    def body(q_ref, k_ref, lf_ref, v_ref, o_ref, st_ref, st):
        @pl.when(pl.program_id(1) == 0)
        def _():
            st[...] = jnp.zeros_like(st)

        lower = _tri(True)
        for n in range(nb):
            sl = slice(n * C, (n + 1) * C)
            q, k, v = q_ref[sl, :].astype(F32), k_ref[sl, :], v_ref[sl, :]
            b = jnp.dot(lower, lf_ref[sl, :], precision=_HI, preferred_element_type=F32)
            s_t = st[...]
            st_ref[0, n] = s_t
            qe = (q * jnp.exp(b)).astype(BF16)
            o = lax.dot_general(qe, s_t.astype(BF16), _NT, preferred_element_type=F32)
            o_ref[sl, :] = o + _hg_intra_fwd(q, k, v, b)
            bl = b[C - 1:C, :]
            kd = (k * jnp.exp(bl - b)).astype(BF16)
            st[...] = s_t * jnp.exp(bl) + lax.dot_general(v.astype(BF16), kd, _TN, preferred_element_type=F32)

    col = lambda off: pl.BlockSpec((rows, HG_DK), lambda h, c: (c, h + off))
    return pl.pallas_call(
        body, name=name, grid=(HG_HEADS, T // rows),
        in_specs=[col(qblk), col(0), col(0), col(0)],
        out_specs=[col(0), pl.BlockSpec((1, nb, HG_DV, HG_DK), lambda h, c: (h, c, 0, 0))],
        out_shape=[jax.ShapeDtypeStruct((T, HG_HEADS * HG_DV), F32),
                   jax.ShapeDtypeStruct((HG_HEADS, T // C, HG_DV, HG_DK), F32)],
        scratch_shapes=[pltpu.VMEM((HG_DV, HG_DK), F32)],
        compiler_params=_cparams(("parallel", "arbitrary")),
    )(u, kk, lf, vv)


def _hgrn_bwd(u, kk, lf, vv, states, do, T, name):
    nb = _pick(T // C, (HG_NB, 3, 2, 1))
    rows = nb * C
    n_steps = T // rows
    qblk = C_HQ // HG_DK

    def body(q_ref, k_ref, lf_ref, v_ref, st_ref, do_ref, dq_ref, dk_ref, dlf_ref, dv_ref, dst, dk_s, dv_s):
        @pl.when(pl.program_id(1) == 0)
        def _():
            dst[...] = jnp.zeros_like(dst)

        lower, upper = _tri(True), _tri(False)
        rid = lax.broadcasted_iota(jnp.int32, (C, 1), 0)
        for n in reversed(range(nb)):
            sl = slice(n * C, (n + 1) * C)
            q, k, v, do = q_ref[sl, :].astype(F32), k_ref[sl, :], v_ref[sl, :], do_ref[sl, :]
            b = jnp.dot(lower, lf_ref[sl, :], precision=_HI, preferred_element_type=F32)
            s_t = st_ref[0, n]
            d_new = dst[...]
            eb = jnp.exp(b)
            bl = b[C - 1:C, :]
            ebl = jnp.exp(bl)
            dec = jnp.exp(bl - b)
            qe = q * eb
            kd = k * dec
            do_b = do.astype(BF16)
            dqe = jnp.dot(do_b, s_t.astype(BF16), preferred_element_type=F32)
            dkd = jnp.dot(v.astype(BF16), d_new.astype(BF16), preferred_element_type=F32)
            dv = lax.dot_general(kd.astype(BF16), d_new.astype(BF16), _NT, preferred_element_type=F32)
            dbl = ebl * jnp.sum(d_new * s_t, axis=0, keepdims=True) + jnp.sum(dkd * kd, axis=0, keepdims=True)
            dst[...] = d_new * ebl + lax.dot_general(do_b, qe.astype(BF16), _TN, preferred_element_type=F32)
            dq_in, dk_in, dv_in = _hg_intra_bwd(q, k, v, b, do, dk_s, dv_s)
            dq = dqe * eb + dq_in
            dk = dkd * dec + dk_in
            dv = dv + dv_in
            db = q * dq - k * dk
            db = db + jnp.where(rid == C - 1, dbl, 0.0)
            dq_ref[sl, :] = dq
            dk_ref[sl, :] = dk
            dv_ref[sl, :] = dv
            dlf_ref[sl, :] = jnp.dot(upper, db, precision=_HI, preferred_element_type=F32)

    rev = lambda off: pl.BlockSpec((rows, HG_DK), lambda h, c: (n_steps - 1 - c, h + off))
    return pl.pallas_call(
        body, name=name, grid=(HG_HEADS, n_steps),
        in_specs=[rev(qblk), rev(0), rev(0), rev(0),
                  pl.BlockSpec((1, nb, HG_DV, HG_DK), lambda h, c: (h, n_steps - 1 - c, 0, 0)), rev(0)],
        out_specs=[rev(0)] * 4,
        out_shape=[jax.ShapeDtypeStruct((T, HG_HEADS * HG_DK), F32)] * 4,
        scratch_shapes=[pltpu.VMEM((HG_DV, HG_DK), F32), pltpu.VMEM((C, HG_DK), F32), pltpu.VMEM((C, HG_DV), F32)],
        compiler_params=_cparams(("parallel", "arbitrary")),
    )(u, kk, lf, vv, states, do)


def _rope_tables(T):
    half = ROPE // 2
    inv_freq = (ROPE_BASE ** (-np.arange(half, dtype=np.float32) / half)).astype(np.float32)
    row = lambda lo, hi, val: np.concatenate([np.zeros(lo, np.float32), np.asarray(val, np.float32) * np.ones(hi - lo, np.float32),
                                              np.zeros(HEAD_W - hi, np.float32)])[None, :]
    freq = row(NOPE, NOPE + half, inv_freq) + row(NOPE + half, NOPE + ROPE, inv_freq)
    pos = lax.broadcasted_iota(jnp.int32, (T, HEAD_W), 0).astype(F32) - float(PAD_FRONT)
    ang = pos * freq
    cos, sin = jnp.cos(ang), jnp.sin(ang)
    c = cos * row(NOPE, NOPE + ROPE, 1.0) + row(0, NOPE, 1.0)
    s1 = sin * row(NOPE, NOPE + half, -1.0)
    s2 = sin * row(NOPE + half, NOPE + ROPE, 1.0)
    return c, s1, s2


def _layer_fwd(x, w, tabs, T, l, late=None):
    c, s1, s2 = tabs
    n = lambda s: f"l{l}_{s}"
    sv = {"x": x}
    h, sv["h_t"] = _rowwise(_f_rms, T, [Row(x)], [(w["norm1_g"], D_MODEL)], [(D_MODEL, BF16)], n("norm1"), True)
    u = _mm(h, w["w_in"], out_dtype=BF16, name=n("in_proj"))
    sv.update(h=h, u=u)
    if late is not None:
        w.update(late(u))
    hglu = _rowwise(_f_glu, T, [Row(u, 512, C_CONV_A), Row(u, 512, C_CONV_G)], [], [(CONV_DIM, F32)], n("glu"))[0]
    cv = _conv_fwd(hglu, w["conv_w"], w["conv_b"], T, n("conv"))
    hc = _rowwise(_f_lnsilu, T, [Row(cv)], [(w["conv_ln_g"], CONV_DIM), (w["conv_ln_b"], CONV_DIM)],
                  [(CONV_DIM, BF16)], n("conv_ln"))[0]
    y_a = _mm(hc, w["w_conv_out"], out_dtype=BF16, name=n("conv_out"))
    sv.update(hglu=hglu, cv=cv, hc=hc, y_a=y_a)
    cqn = _rowwise(_f_rms, T, [Row(u, Q_RANK, C_CQ)], [(w["q_a_norm_g"], Q_RANK)], [(Q_RANK, BF16)], n("q_a_norm"))[0]
    ckvn = _rowwise(_f_rms, T, [Row(u, KV_RANK, C_CKV)], [(w["kv_a_norm_g"], KV_RANK)], [(KV_RANK, BF16)], n("kv_a_norm"))[0]
    q_raw = _mm(cqn, w["w_uq"], out_dtype=BF16, name=n("uq"))
    k_raw = _mm(ckvn, w["w_uk"], out_dtype=BF16, name=n("uk"))
    v = _mm(ckvn, w["w_uv"], out_dtype=BF16, name=n("uv"))
    tab_rows = [Row(c), Row(s1), Row(s2)]
    q = _rowwise(_f_qrope, T, [Row(q_raw, piece=HEAD_W)] + tab_rows, [(w["q_norm_g"], HEAD_W)],
                 [(HEADS * HEAD_W, BF16)], n("q_rope"))[0]
    k = _rowwise(_f_krope, T, [Row(k_raw, piece=HEAD_W), Row(u, HEAD_W, C_KR)] + tab_rows, [(w["k_norm_g"], HEAD_W)],
                 [(HEADS * HEAD_W, BF16)], n("k_rope"))[0]
    o, lse = _attn_fwd(q, k, v, T, n("attn"))
    y_b = _mm(o, w["w_attn_out"], out_dtype=BF16, name=n("attn_out"))
    sv.update(cqn=cqn, ckvn=ckvn, q_raw=q_raw, k_raw=k_raw, v=v, q=q, k=k, o=o, lse=lse, y_b=y_b)
    kk, lf, vv = _rowwise(_f_hgrn_prep, T, [Row(u, 512, C_HF), Row(u, 512, C_HI)], [(w["lb"], 512)],
                          [(512, F32)] * 3, n("hgrn_prep"))
    o_h, states = _hgrn_fwd(u, kk, lf, vv, T, n("hgrn"))
    oh = _rowwise(_f_hgrn_out, T, [Row(o_h, piece=HG_DV), Row(u, 512, C_HG, piece=HG_DV)], [(w["hgrn_norm_g"], HG_DV)],
                  [(512, BF16)], n("hgrn_out_norm"))[0]
    y_c = _mm(oh, w["w_hgrn_out"], out_dtype=BF16, name=n("hgrn_out"))
    sv.update(kk=kk, lf=lf, vv=vv, o_h=o_h, states=states, oh=oh, y_c=y_c)
    gate_rows = [Row(u, D_MODEL, C_GATE + g * D_MODEL) for g in range(3)]
    mix = _rowwise(_f_mix, T, gate_rows + [Row(y_a), Row(y_b), Row(y_c)], [], [(D_MODEL, BF16)], n("mix"))[0]
    x1 = _mm(mix, w["w_out"], res=x, name=n("out_proj"))
    h2, sv["h2_t"] = _rowwise(_f_rms, T, [Row(x1)], [(w["norm2_g"], D_MODEL)], [(D_MODEL, BF16)], n("norm2"), True)
    f = _mm(h2, w["w_ff1"], out_dtype=BF16, name=n("ff1"))
    x2 = _mm(f, w["w_ff2"], res=x1, a_fn=_relu2, name=n("ff2"))
    sv.update(mix=mix, x1=x1, h2=h2, f=f)
    return x2, sv


def _layer_bwd(dx2, w, sv, tabs, T, l, mid=None, matrices=None):
    c, s1, s2 = tabs
    n = lambda s: f"l{l}_b_{s}"
    u = sv["u"]
    g = {}
    g["w_ff2"] = _mm(sv["f"], dx2, ta=True, a_fn=_relu2, out_dtype=BF16, name=n("dw_ff2"))
    df = _mm(dx2, w["w_ff2"], tb=True, out_dtype=BF16, name=n("d_f"),
             epi=(sv["f"], lambda d, fv: d * (2.0 * jnp.maximum(fv, 0.0))))
    g["w_ff1"] = _mm(sv["h2_t"], df, out_dtype=BF16, name=n("dw_ff1"))
    dh2 = _mm(df, w["w_ff1"], tb=True, name=n("d_h2"))
    (dx1,), (g["norm2_g"],) = _rowwise_bwd(_f_rms, T, [Row(sv["x1"])], [(w["norm2_g"], D_MODEL)], [Row(dh2)],
                                           {0: F32}, n("norm2"), add=(0, dx2))
    g["w_out"] = _mm(sv["mix"], dx1, ta=True, out_dtype=BF16, name=n("dw_out"))
    w_out = w["w_out"] if mid is None else mid(g, w["w_out"])
    dmix = _mm(dx1, w_out, tb=True, out_dtype=BF16, name=n("d_mix"))
    gate_rows = [Row(u, D_MODEL, C_GATE + i * D_MODEL) for i in range(3)]
    (dg0, dg1, dg2, dy_a, dy_b, dy_c), _ = _rowwise_bwd(
        _f_mix, T, gate_rows + [Row(sv["y_a"]), Row(sv["y_b"]), Row(sv["y_c"])], [], [Row(dmix)],
        {0: BF16, 1: BF16, 2: BF16, 3: BF16, 4: BF16, 5: BF16}, n("mix"))
    g["w_hgrn_out"] = _mm(sv["oh"], dy_c, ta=True, out_dtype=BF16, name=n("dw_hgrn_out"))
    doh = _mm(dy_c, w["w_hgrn_out"], tb=True, out_dtype=BF16, name=n("d_oh"))
    (do_h, dhg), (g["hgrn_norm_g"],) = _rowwise_bwd(
        _f_hgrn_out, T, [Row(sv["o_h"], piece=HG_DV), Row(u, 512, C_HG, piece=HG_DV)], [(w["hgrn_norm_g"], HG_DV)],
        [Row(doh, piece=HG_DV)], {0: F32, 1: BF16}, n("hgrn_out_norm"))
    dhq, dkk, dlf, dvv = _hgrn_bwd(u, sv["kk"], sv["lf"], sv["vv"], sv["states"], do_h, T, n("hgrn"))
    (dhf, dhi), (g["lb"],) = _rowwise_bwd(
        _f_hgrn_prep, T, [Row(u, 512, C_HF), Row(u, 512, C_HI)], [(w["lb"], 512)],
        [Row(dkk), Row(dlf), Row(dvv)], {0: BF16, 1: BF16}, n("hgrn_prep"))
    g["w_attn_out"] = _mm(sv["o"], dy_b, ta=True, out_dtype=BF16, name=n("dw_attn_out"))
    do = _mm(dy_b, w["w_attn_out"], tb=True, out_dtype=BF16, name=n("d_o"))
    dq, dk, dv = _attn_bwd(sv["q"], sv["k"], sv["v"], sv["o"], sv["lse"], do, T, n("attn"))
    tab_rows = [Row(c), Row(s1), Row(s2)]
    (dq_raw,), (g["q_norm_g"],) = _rowwise_bwd(
        _f_qrope, T, [Row(sv["q_raw"], piece=HEAD_W)] + tab_rows, [(w["q_norm_g"], HEAD_W)],
        [Row(dq, piece=HEAD_W)], {0: BF16}, n("q_rope"))
    (dk_raw, dkr), (g["k_norm_g"],) = _rowwise_bwd(
        _f_krope, T, [Row(sv["k_raw"], piece=HEAD_W), Row(u, HEAD_W, C_KR)] + tab_rows, [(w["k_norm_g"], HEAD_W)],
        [Row(dk, piece=HEAD_W)], {0: BF16, 1: BF16}, n("k_rope"))
    g["w_uq"] = _mm(sv["cqn"], dq_raw, ta=True, out_dtype=BF16, name=n("dw_uq"))
    g["w_uk"] = _mm(sv["ckvn"], dk_raw, ta=True, out_dtype=BF16, name=n("dw_uk"))
    g["w_uv"] = _mm(sv["ckvn"], dv, ta=True, out_dtype=BF16, name=n("dw_uv"))
    dcqn = _mm(dq_raw, w["w_uq"], tb=True, out_dtype=BF16, name=n("d_cqn"))
    dckvn = _mm(dk_raw, w["w_uk"], tb=True, name=n("d_ckvn_k"))
    dckvn = _mm(dv, w["w_uv"], tb=True, res=dckvn, out_dtype=BF16, name=n("d_ckvn_v"))
    (dcq,), (g["q_a_norm_g"],) = _rowwise_bwd(_f_rms, T, [Row(u, Q_RANK, C_CQ)], [(w["q_a_norm_g"], Q_RANK)],
                                              [Row(dcqn)], {0: BF16}, n("q_a_norm"))
    (dckv,), (g["kv_a_norm_g"],) = _rowwise_bwd(_f_rms, T, [Row(u, KV_RANK, C_CKV)], [(w["kv_a_norm_g"], KV_RANK)],
                                                [Row(dckvn)], {0: BF16}, n("kv_a_norm"))
    g["w_conv_out"] = _mm(sv["hc"], dy_a, ta=True, out_dtype=BF16, name=n("dw_conv_out"))
    dhc = _mm(dy_a, w["w_conv_out"], tb=True, out_dtype=BF16, name=n("d_hc"))
    (dcv,), (g["conv_ln_g"], g["conv_ln_b"]) = _rowwise_bwd(
        _f_lnsilu, T, [Row(sv["cv"])], [(w["conv_ln_g"], CONV_DIM), (w["conv_ln_b"], CONV_DIM)], [Row(dhc)],
        {0: F32}, n("conv_ln"))
    dhglu, dconv_w, g["conv_b"] = _conv_bwd(sv["hglu"], w["conv_w"], dcv, T, n("conv"))
    g["conv_w"] = dconv_w[:CONV_K]
    (dua, dug), _ = _rowwise_bwd(_f_glu, T, [Row(u, 512, C_CONV_A), Row(u, 512, C_CONV_G)], [], [Row(dhglu)],
                                 {0: BF16, 1: BF16}, n("glu"))
    du = jnp.concatenate([dua, dug, dg0, dg1, dg2, dcq, dckv, dkr, dhq.astype(BF16), dhf, dhi, dhg], axis=1)
    small = ("w_uq", "w_uk", "w_uv", "w_attn_out", "w_hgrn_out", "w_conv_out")
    du, *done = lax.optimization_barrier((du, *[g[k] for k in small]))
    g.update(zip(small, done))
    g["w_in"] = _mm(sv["h_t"], du, out_dtype=BF16, name=n("dw_in"))
    norm_g = w["norm1_g"] if matrices is None else matrices(g, w["norm1_g"])
    du, norm_g = lax.optimization_barrier((du, norm_g))
    dh = _mm(du, w["w_in"], tb=True, name=n("d_h"))
    (dx,), (g["norm1_g"],) = _rowwise_bwd(_f_rms, T, [Row(sv["x"])], [(norm_g, D_MODEL)], [Row(dh)],
                                          {0: F32}, n("norm1"), add=(0, dx1))
    return dx, g


def _pad_w_in(w_in):
    z = lambda k: jnp.zeros((w_in.shape[0], k), w_in.dtype)
    return jnp.concatenate([w_in[:, :O_CQ], w_in[:, O_GATE:], w_in[:, O_CQ:O_KR], z(KR_LANE), w_in[:, O_KR:O_HQ],
                            z(HEAD_W - KR_LANE - ROPE), w_in[:, O_HQ:O_GATE]], axis=1)


def _unpad_w_in(g):
    return jnp.concatenate([g[:, :C_GATE], g[:, C_CQ:C_KR], g[:, C_KR + KR_LANE:C_KR + KR_LANE + ROPE],
                            g[:, C_HQ:], g[:, C_GATE:C_CQ]], axis=1)


_W_IN_RUNS = ((0, 0, O_CQ), (O_CQ, C_CQ, O_KR - O_CQ), (O_KR, C_KR + KR_LANE, ROPE), (O_HQ, C_HQ, O_GATE - O_HQ),
              (O_GATE, C_GATE, N_IN - O_GATE))


def _w_in_from_shards(g8):
    per = N_IN // N_DEV
    pieces, at = [], 0
    for o0, p0, n in sorted(_W_IN_RUNS, key=lambda r: r[1]):
        if p0 > at:
            pieces.append(jnp.zeros((g8.shape[1], p0 - at), g8.dtype))
        for j in range(o0 // per, (o0 + n - 1) // per + 1):
            lo, hi = max(o0, j * per), min(o0 + n, (j + 1) * per)
            pieces.append(g8[j][:, lo - j * per:hi - j * per])
        at = p0 + n
    if at < N_IN_P:
        pieces.append(jnp.zeros((g8.shape[1], N_IN_P - at), g8.dtype))
    return jnp.concatenate(pieces, axis=1)


def _w_in_grad_shards(g):
    per = N_IN // N_DEV
    shards = []
    for j in range(N_DEV):
        lo, hi = j * per, (j + 1) * per
        pieces = [g[:, p0 + max(lo, o0) - o0:p0 + min(hi, o0 + n) - o0]
                  for o0, p0, n in _W_IN_RUNS if max(lo, o0) < min(hi, o0 + n)]
        shards.append(jnp.concatenate(pieces, axis=1) if len(pieces) > 1 else pieces[0])
    return jnp.stack(shards)


def _pad_heads(wm, per_head, lo, hi):
    lead = wm.shape[:-1]
    wh = wm.reshape(lead + (HEADS, per_head))[..., lo:hi]
    pad = [(0, 0)] * len(lead) + [(0, 0), (0, HEAD_W - (hi - lo))]
    return jnp.pad(wh, pad).reshape(lead + (HEADS * HEAD_W,))


def _unpad_heads(gm, width):
    lead = gm.shape[:-1]
    return gm.reshape(lead + (HEADS, HEAD_W))[..., :width]


def _layer_weights(full, lb=None):
    w = {}
    for name in ("norm1_g", "conv_w", "conv_b", "conv_ln_g", "conv_ln_b", "q_a_norm_g", "kv_a_norm_g", "hgrn_norm_g",
                 "norm2_g", "w_conv_out", "w_hgrn_out", "w_out", "w_ff1", "w_ff2"):
        if name in full:
            w[name] = full[name]
    if lb is not None:
        w["lb"] = lb
    if "w_in_padded" in full:
        w["w_in"] = full["w_in_padded"]
    elif "w_in" in full:
        w["w_in"] = _pad_w_in(full["w_in"])
    if "w_uq" in full:
        w["w_uq"] = _pad_heads(full["w_uq"], QK_DIM, 0, QK_DIM)
    if "w_ukv" in full:
        w["w_uk"] = _pad_heads(full["w_ukv"], NOPE + V_DIM, 0, NOPE)
        w["w_uv"] = _pad_heads(full["w_ukv"], NOPE + V_DIM, NOPE, NOPE + V_DIM)
    for name in ("q_norm_g", "k_norm_g"):
        if name in full:
            w[name] = jnp.pad(full[name], (0, HEAD_W - QK_DIM))
    if "w_attn_out" in full:
        wa = full["w_attn_out"].reshape(HEADS, V_DIM, D_MODEL)
        w["w_attn_out"] = jnp.pad(wa, ((0, 0), (0, HEAD_W - V_DIM), (0, 0))).reshape(HEADS * HEAD_W, D_MODEL)
    return w


def _matrix_grads_to_original(g):
    o = {name: g[name] for name in ("w_conv_out", "w_hgrn_out", "w_out", "w_ff1", "w_ff2")}
    o["w_in"] = _unpad_w_in(g["w_in"])
    o["w_in_shards"] = _w_in_grad_shards(g["w_in"])
    o["w_uq"] = _unpad_heads(g["w_uq"], QK_DIM).reshape(Q_RANK, HEADS * QK_DIM)
    guk = _unpad_heads(g["w_uk"], NOPE)
    guv = _unpad_heads(g["w_uv"], V_DIM)
    o["w_ukv"] = jnp.concatenate([guk, guv], axis=-1).reshape(KV_RANK, HEADS * (NOPE + V_DIM))
    o["w_attn_out"] = g["w_attn_out"].reshape(HEADS, HEAD_W, D_MODEL)[:, :V_DIM].reshape(HEADS * V_DIM, D_MODEL)
    return o


def _vector_grads_to_original(g):
    o = {"conv_w": g["conv_w"]}
    for name in ("norm1_g", "conv_b", "conv_ln_g", "conv_ln_b", "q_a_norm_g", "kv_a_norm_g", "hgrn_norm_g", "norm2_g", "lb"):
        o[name] = g[name].reshape(-1)
    o["q_norm_g"] = g["q_norm_g"].reshape(-1)[:QK_DIM]
    o["k_norm_g"] = g["k_norm_g"].reshape(-1)[:QK_DIM]
    return o


def _lower_bounds(logits):
    p = jax.nn.softmax(logits.astype(F32), axis=0)
    return jnp.cumsum(p, axis=0) - p[0:1]


def _run_step(x, target, meta, lb_logits, layer_weights, layer_done, layer_mid=None, layer_matrices=None):
    seq = x.shape[0]
    T = ROW0 + seq
    assert T % 128 == 0
    tabs = _rope_tables(T)
    lbs, lb_vjp = jax.vjp(_lower_bounds, lb_logits)
    xp = jnp.concatenate([jnp.zeros((PAD_FRONT, D_MODEL), F32), meta.astype(F32), x], axis=0)
    tp = jnp.concatenate([jnp.zeros((ROW0, D_MODEL), F32), target], axis=0)
    ws, svs = [], []
    for l in range(DEPTH):
        full, xp = layer_weights(l, xp)
        late = full.pop("late", None)
        w = _layer_weights(full, lbs[l])
        xp, sv = _layer_fwd(xp, w, tabs, T, l, late)
        ws.append(w)
        svs.append(sv)
    dx, sq = _loss_head(xp, tp, T)
    loss = 0.5 * jnp.sum(sq) * (1.0 / D_MODEL)
    dlb = [None] * DEPTH
    for l in reversed(range(DEPTH)):
        mid = None if layer_mid is None else functools.partial(layer_mid, l)
        mats = {}

        def matrices(g, norm_g, l=l, mats=mats):
            mats.update(_matrix_grads_to_original(g))
            return norm_g if layer_matrices is None else layer_matrices(l, mats, norm_g)

        dx, g = _layer_bwd(dx, ws[l], svs[l], tabs, T, l, mid, matrices)
        g = {**_vector_grads_to_original(g), **mats}
        dlb[l] = g.pop("lb")
        dx = layer_done(l, g, dx)
    return loss, dx[ROW0:], dx[PAD_FRONT:ROW0], lb_vjp(jnp.stack(dlb))[0]


def _local_step(x, target, full):
    per_layer = [None] * DEPTH

    def done(l, g, dx):
        per_layer[l] = g
        return dx

    loss, gx, gmeta, glb = _run_step(
        x, target, full["meta"], full["hgrn_lb_logits"],
        lambda l, xp: ({k: v[l] for k, v in full.items() if k != "meta"}, xp), done)
    grads = {k: jnp.stack([per_layer[l][k] for l in range(DEPTH)]) for k in per_layer[0]}
    grads["hgrn_lb_logits"] = glb
    grads["meta"] = gmeta
    return loss, gx, grads


def _mesh_pos():
    return lax.axis_index("x"), lax.axis_index("y"), lax.axis_index("c")


N_COPY = N_DEV - 1


def _all_gather(arrs, name):
    n = len(arrs)

    def body(*refs):
        x_refs, out_refs = refs[:n], refs[n:2 * n]
        send_sems, recv_sems, local_sems = refs[2 * n:]
        x, y, c = _mesh_pos()
        me, sibling = (x, y, c), (x, y, 1 - c)
        chips = [(1 - x, y), (x, 1 - y), (1 - x, 1 - y)]

        def slot(a, px, py, pc):
            return out_refs[a].at[4 * px + 2 * py + pc]

        def copy(a, k, block, to, own=False):
            return pltpu.make_async_remote_copy(
                src_ref=x_refs[a] if own else slot(a, *block), dst_ref=slot(a, *block),
                send_sem=send_sems.at[a * N_COPY + k], recv_sem=recv_sems.at[a * N_COPY + k],
                device_id=to, device_id_type=MESH)

        mine = [pltpu.make_async_copy(x_refs[a], slot(a, *me), local_sems.at[a]) for a in range(n)]
        for cp in mine:
            cp.start()
        first = []
        for a in range(n):
            first.append(copy(a, 0, me, sibling, own=True))
            first += [copy(a, 1 + j, me, (*chip, c), own=True) for j, chip in enumerate(chips)]
        for cp in first:
            cp.start()
        passed = []
        for j, chip in enumerate(chips):
            for a in range(n):
                copy(a, 1 + j, (*chip, c), me).wait_recv()
                cp = copy(a, 4 + j, (*chip, c), sibling)
                cp.start()
                passed.append(cp)
        for a in range(n):
            copy(a, 0, sibling, me).wait_recv()
            for j, chip in enumerate(chips):
                copy(a, 4 + j, (*chip, 1 - c), me).wait_recv()
        for cp in first + passed:
            cp.wait_send()
        for cp in mine:
            cp.wait()

    anyspec = pl.BlockSpec(memory_space=pl.ANY)
    return pl.pallas_call(
        body, name=name, out_shape=[jax.ShapeDtypeStruct((N_DEV,) + a.shape, a.dtype) for a in arrs],
        in_specs=[anyspec] * n, out_specs=[anyspec] * n,
        scratch_shapes=[pltpu.SemaphoreType.DMA((n * N_COPY,)), pltpu.SemaphoreType.DMA((n * N_COPY,)),
                        pltpu.SemaphoreType.DMA((n,))],
    )(*arrs)


def _exchange(arrs, name):
    n = len(arrs)

    def body(*refs):
        s_refs, r_refs = refs[:n], refs[n:2 * n]
        send_sems, recv_sems, local_sems = refs[2 * n:]
        x, y, c = _mesh_pos()
        me = 4 * x + 2 * y + c
        local = [pltpu.make_async_copy(s_refs[a].at[me], r_refs[a].at[me], local_sems.at[a]) for a in range(n)]
        for cp in local:
            cp.start()
        sends, recvs = [], []
        for rel in range(1, N_DEV):
            px = 1 - x if rel & 4 else x
            py = 1 - y if rel & 2 else y
            pc = 1 - c if rel & 1 else c
            p = 4 * px + 2 * py + pc
            for a in range(n):
                k = a * N_COPY + rel - 1
                sends.append(pltpu.make_async_remote_copy(
                    src_ref=s_refs[a].at[p], dst_ref=r_refs[a].at[me], send_sem=send_sems.at[k],
                    recv_sem=recv_sems.at[k], device_id=(px, py, pc), device_id_type=MESH))
                recvs.append(pltpu.make_async_remote_copy(
                    src_ref=s_refs[a].at[me], dst_ref=r_refs[a].at[p], send_sem=send_sems.at[k],
                    recv_sem=recv_sems.at[k], device_id=(px, py, pc), device_id_type=MESH))
        for cp in sends:
            cp.start()
        for cp in recvs:
            cp.wait_recv()
        for cp in sends:
            cp.wait_send()
        for cp in local:
            cp.wait()

    anyspec = pl.BlockSpec(memory_space=pl.ANY)
    return pl.pallas_call(
        body, name=name, out_shape=[jax.ShapeDtypeStruct(a.shape, a.dtype) for a in arrs],
        in_specs=[anyspec] * n, out_specs=[anyspec] * n,
        scratch_shapes=[pltpu.SemaphoreType.DMA((n * N_COPY,)), pltpu.SemaphoreType.DMA((n * N_COPY,)),
                        pltpu.SemaphoreType.DMA((n,))],
    )(*arrs)


_HBM = pl.BlockSpec(memory_space=pltpu.HBM)
_SEM = pl.BlockSpec(memory_space=pltpu.SEMAPHORE)
_EFFECT = pltpu.SideEffectType.DATAFLOW_SIDE_EFFECTING


def _peers(x, y, c):
    out = []
    for rel in range(1, N_DEV):
        px = 1 - x if rel & 4 else x
        py = 1 - y if rel & 2 else y
        pc = 1 - c if rel & 1 else c
        out.append((rel, (px, py, pc), 4 * px + 2 * py + pc))
    return out


ALL_RELS = tuple(range(1, N_DEV))
NEAR_RELS = (1, 2, 4, 6)


def _split_copies(src_refs, land_refs, send_sems, recv_sems, gather, rels=ALL_RELS):
    x, y, c = _mesh_pos()
    me = 4 * x + 2 * y + c
    out = []
    for a, (src, land) in enumerate(zip(src_refs, land_refs)):
        for rel, peer, p in _peers(x, y, c):
            if rel not in rels:
                continue
            k = a * N_COPY + rel - 1
            mk = lambda s, d: pltpu.make_async_remote_copy(
                src_ref=s, dst_ref=d, send_sem=send_sems.at[k], recv_sem=recv_sems.at[k],
                device_id=peer, device_id_type=MESH)
            mine = src if gather else src.at[p]
            out.append((mk(mine, land.at[me]), mk(mine, land.at[p])))
    return out


def _copy_start(srcs, gather, name, collective_id, rels=ALL_RELS):
    n = len(srcs)
    lands = [lax.empty(((N_DEV,) + s.shape) if gather else s.shape, s.dtype) for s in srcs]

    def body(*refs):
        src_refs, land_refs = refs[:n], refs[n:2 * n]
        send_sems, recv_sems = refs[2 * n], refs[2 * n + 1]
        token = refs[-1]
        x, y, c = _mesh_pos()
        barrier = pltpu.get_barrier_semaphore()
        for rel, peer, _ in _peers(x, y, c):
            if rel in rels:
                pl.semaphore_signal(barrier, inc=1, device_id=peer, device_id_type=MESH)
        pl.semaphore_wait(barrier, len(rels))
        for out_copy, _ in _split_copies(src_refs, land_refs, send_sems, recv_sems, gather, rels):
            out_copy.start()
        token[...] = jnp.zeros_like(token)

    hbm = lambda a: pltpu.HBM(a.shape, a.dtype)
    res = pl.pallas_call(
        body, name=name,
        out_shape=(pltpu.SemaphoreType.DMA((n * N_COPY,)), pltpu.SemaphoreType.DMA((n * N_COPY,)),
                   *[hbm(s) for s in srcs], *[hbm(z) for z in lands], jax.ShapeDtypeStruct((8, 128), F32)),
        in_specs=[_HBM] * (2 * n), out_specs=(_SEM, _SEM, *([_HBM] * (2 * n)), pl.BlockSpec(memory_space=pltpu.VMEM)),
        input_output_aliases={i: 2 + i for i in range(2 * n)},
        compiler_params=pltpu.CompilerParams(has_side_effects=_EFFECT, collective_id=collective_id),
    )(*[pltpu.with_memory_space_constraint(s, pltpu.HBM) for s in srcs],
      *[pltpu.with_memory_space_constraint(z, pltpu.HBM) for z in lands])
    return res[0], res[1], list(res[2:2 + n]), list(res[2 + n:2 + 2 * n]), res[-1]


def _after(a, token):
    return a + token[0, 0].astype(a.dtype)


def _forward_to_sibling(lands, name):
    n = len(lands)

    def body(*refs):
        land_refs = refs[n:2 * n]
        send_sems, recv_sems = refs[2 * n], refs[2 * n + 1]
        x, y, c = _mesh_pos()
        chips = [(1 - x, y), (x, 1 - y), (1 - x, 1 - y)]
        sends, recvs = [], []
        for a, land in enumerate(land_refs):
            for j, (px, py) in enumerate(chips):
                k = a * len(chips) + j
                mk = lambda slot: pltpu.make_async_remote_copy(
                    src_ref=land.at[slot], dst_ref=land.at[slot], send_sem=send_sems.at[k], recv_sem=recv_sems.at[k],
                    device_id=(x, y, 1 - c), device_id_type=MESH)
                sends.append(mk(4 * px + 2 * py + c))
                recvs.append(mk(4 * px + 2 * py + 1 - c))
        for cp in sends:
            cp.start()
        for cp in recvs:
            cp.wait_recv()
        for cp in sends:
            cp.wait_send()

    anyspec = pl.BlockSpec(memory_space=pl.ANY)
    return list(pl.pallas_call(
        body, name=name, out_shape=[jax.ShapeDtypeStruct(z.shape, z.dtype) for z in lands],
        in_specs=[anyspec] * n, out_specs=[anyspec] * n, input_output_aliases={i: i for i in range(n)},
        scratch_shapes=[pltpu.SemaphoreType.DMA((3 * n,)), pltpu.SemaphoreType.DMA((3 * n,))],
    )(*lands))


def _copy_wait(send_sems, recv_sems, srcs, lands, after, gather, name, rels=ALL_RELS):
    n = len(srcs)

    def body(*refs):
        src_refs, land_refs = refs[:n], refs[n:2 * n]
        s_sems, r_sems = refs[2 * n], refs[2 * n + 1]
        for out_copy, in_copy in _split_copies(src_refs, land_refs, s_sems, r_sems, gather, rels):
            out_copy.wait_send()
            in_copy.wait_recv()

    hbm = lambda a: pltpu.HBM(a.shape, a.dtype)
    res = pl.pallas_call(
        body, name=name, out_shape=(*[hbm(s) for s in srcs], *[hbm(z) for z in lands]),
        in_specs=[_HBM] * (2 * n) + [_SEM, _SEM, pl.BlockSpec(memory_space=pl.ANY)], out_specs=tuple([_HBM] * (2 * n)),
        input_output_aliases={i: i for i in range(2 * n)},
        compiler_params=pltpu.CompilerParams(has_side_effects=_EFFECT),
    )(*srcs, *lands, send_sems, recv_sems, after)
    return list(res[:n]), list(res[n:])


def _sum_parts(parts, name):
    P, R, W = parts.shape

    def body(p_ref, o_ref):
        g = p_ref[0].astype(F32)
        for i in range(1, P):
            g = g + p_ref[i].astype(F32)
        o_ref[...] = g

    return pl.pallas_call(body, name=name, out_shape=jax.ShapeDtypeStruct((R, W), F32))(parts)


def _adamw_body(p_ref, w_ref, m_ref, v_ref, g_ref, d_ref, nm_ref, nv_ref):
    g = p_ref[0].astype(F32)
    for i in range(1, p_ref.shape[0]):
        g = g + p_ref[i].astype(F32)
    _adamw_apply(g, w_ref, m_ref, v_ref, g_ref, d_ref, nm_ref, nv_ref)


def _adamw_apply(g, w_ref, m_ref, v_ref, g_ref, d_ref, nm_ref, nv_ref):
    m_new = ADAM_B1 * m_ref[...] + (1.0 - ADAM_B1) * g
    v_new = ADAM_B2 * v_ref[...] + (1.0 - ADAM_B2) * jnp.square(g)
    m_hat = m_new / (1.0 - ADAM_B1 ** ADAM_STEP)
    v_hat = v_new / (1.0 - ADAM_B2 ** ADAM_STEP)
    g_ref[...] = g
    d_ref[...] = -ADAM_LR * (m_hat / (jnp.sqrt(v_hat) + ADAM_EPS) + ADAM_WD * w_ref[...])
    nm_ref[...] = m_new
    nv_ref[...] = v_new


def _adamw(parts, w, m, v, name):
    P, R, W = parts.shape
    tr = _pick(R, (368, 192, 64, 16, 8))
    spec = pl.BlockSpec((tr, W), lambda i: (i, 0))
    return pl.pallas_call(
        functools.partial(_adamw_body), name=name, grid=(R // tr,),
        in_specs=[pl.BlockSpec((P, tr, W), lambda i: (0, i, 0)), spec, spec, spec], out_specs=[spec] * 4,
        out_shape=[jax.ShapeDtypeStruct((R, W), F32)] * 4,
        compiler_params=_cparams(("parallel",)),
    )(parts, w, m, v)


def _adamw_layers(parts, w, m, v, name):
    P, B, C_ = parts[0].shape
    tb = _pick(B, (256, 128))
    nb = B // tb

    def body(*refs):
        p_refs, rest = refs[:DEPTH], refs[DEPTH:]
        a = pl.program_id(0)
        for l in range(DEPTH):
            @pl.when(a == l)
            def _():
                _adamw_body(p_refs[l], *[r.at[0] for r in rest])

    spec = pl.BlockSpec((1, tb, C_), lambda a, i: (a, i, 0))

    def part_spec(l):
        return pl.BlockSpec((P, tb, C_), lambda a, i: (0, jnp.where(a == l, i, jnp.where(a < l, 0, nb - 1)), 0))

    return pl.pallas_call(
        body, name=name, grid=(DEPTH, nb),
        in_specs=[part_spec(l) for l in range(DEPTH)] + [spec, spec, spec], out_specs=[spec] * 4,
        out_shape=[jax.ShapeDtypeStruct((DEPTH, B, C_), F32)] * 4,
        compiler_params=_cparams(("arbitrary", "arbitrary")),
    )(*parts, w, m, v)


VEC_GROUPS = (("norm1_g", "norm2_g"), ("conv_b", "conv_ln_g", "conv_ln_b", "hgrn_lb_logits", "hgrn_norm_g"),
              ("q_a_norm_g",), ("kv_a_norm_g",), ("q_norm_g", "k_norm_g"))
SMALL_NAMES = tuple(n for grp in VEC_GROUPS for n in grp) + ("meta", "conv_w")


def _adamw_small(own, lands, wts, mom, var, name):
    n_in = len(own)

    def body(*refs):
        own_r, land_r = refs[:n_in], refs[n_in:2 * n_in]
        rest = iter(refs[2 * n_in:])
        wmv = {n: (next(rest), next(rest), next(rest)) for n in SMALL_NAMES}
        outs = {n: (next(rest), next(rest), next(rest), next(rest)) for n in SMALL_NAMES}
        loss_ref = next(rest)
        x, y, c = _mesh_pos()
        me = 4 * x + 2 * y + c

        def total(k):
            acc = None
            for s in range(N_DEV):
                v = jnp.where(me == s, own_r[k][...], land_r[k][s])
                acc = v if acc is None else acc + v
            return acc

        for k, grp in enumerate(VEC_GROUPS):
            tot = total(k)
            for j, n in enumerate(grp):
                _adamw_apply(tot[DEPTH * j:DEPTH * (j + 1)], *wmv[n], *outs[n])
        loss_ref[...] = total(len(VEC_GROUPS))
        _adamw_apply(total(n_in - 2), *wmv["meta"], *outs["meta"])
        _adamw_apply(total(n_in - 1), *wmv["conv_w"], *outs["conv_w"])

    args = list(own) + list(lands) + [d[n] for n in SMALL_NAMES for d in (wts, mom, var)]
    out_shape = [jax.ShapeDtypeStruct(wts[n].shape, F32) for n in SMALL_NAMES for _ in range(4)]
    res = pl.pallas_call(body, name=name, out_shape=out_shape + [jax.ShapeDtypeStruct((1, 128), F32)])(*args)
    out = {}
    for i, n in enumerate(SMALL_NAMES):
        for j, kind in enumerate(("grad_", "delta_", "new_m_", "new_v_")):
            out[kind + n] = res[4 * i + j]
    return out, res[-1]


PACK_W = 1024
BIG = (("w_in", (DEPTH, D_MODEL, N_IN // N_DEV), 2), ("w_conv_out", (DEPTH, CONV_DIM, D_MODEL // N_DEV), 2),
       ("w_uq", (DEPTH, Q_RANK, HEADS * QK_DIM // N_DEV), 2), ("w_ukv", (DEPTH, KV_RANK, HEADS * (NOPE + V_DIM) // N_DEV), 2),
       ("w_attn_out", (DEPTH, HEADS * V_DIM, D_MODEL // N_DEV), 2), ("w_hgrn_out", (DEPTH, 512, D_MODEL // N_DEV), 2),
       ("w_out", (DEPTH, D_MODEL // N_DEV, D_MODEL), 1), ("w_ff1", (DEPTH, D_MODEL, D_FF // N_DEV), 2),
       ("w_ff2", (DEPTH, D_FF // N_DEV, D_MODEL), 1))
SMALL_SHARDED = (("meta", (N_META, D_MODEL // N_DEV), 1), ("conv_w", (DEPTH, CONV_K, CONV_DIM // N_DEV), 2))
REPLICATED = (("norm1_g", (DEPTH, D_MODEL)), ("conv_b", (DEPTH, CONV_DIM)), ("conv_ln_g", (DEPTH, CONV_DIM)),
              ("conv_ln_b", (DEPTH, CONV_DIM)), ("q_a_norm_g", (DEPTH, Q_RANK)), ("kv_a_norm_g", (DEPTH, KV_RANK)),
              ("q_norm_g", (DEPTH, QK_DIM)), ("k_norm_g", (DEPTH, QK_DIM)), ("hgrn_lb_logits", (DEPTH, 512)),
              ("hgrn_norm_g", (DEPTH, 512)), ("norm2_g", (DEPTH, D_MODEL)))
WEIGHT_ORDER = ("meta", "norm1_g", "w_in", "conv_w", "conv_b", "conv_ln_g", "conv_ln_b", "w_conv_out", "q_a_norm_g", "w_uq",
                "kv_a_norm_g", "w_ukv", "q_norm_g", "k_norm_g", "w_attn_out", "hgrn_lb_logits", "hgrn_norm_g", "w_hgrn_out",
                "w_out", "norm2_g", "w_ff1", "w_ff2")


def _rows_for(n_elems, mult):
    rows = -(-n_elems // PACK_W)
    return -(-rows // mult) * mult


def _pack(arrays, dtype, mult, lead=()):
    nl = len(lead)
    flat = jnp.concatenate([a.reshape(lead + (-1,)).astype(dtype) for a in arrays], axis=nl)
    rows = _rows_for(flat.shape[nl], mult)
    flat = jnp.pad(flat, [(0, 0)] * nl + [(0, rows * PACK_W - flat.shape[nl])])
    return flat.reshape(lead + (rows, PACK_W))


def _unpack(pack, shapes, lead=()):
    nl = len(lead)
    flat = pack.reshape(lead + (-1,))
    out, off = [], 0
    for shp in shapes:
        n = int(np.prod(shp))
        out.append(lax.slice_in_dim(flat, off, off + n, axis=nl).reshape(lead + tuple(shp)))
        off += n
    return out


def _join_shards(g, axis):
    g = jnp.moveaxis(g, 0, axis)
    shp = g.shape
    return g.reshape(shp[:axis] + (shp[axis] * shp[axis + 1],) + shp[axis + 2:])


def _cut_shards(a, axis):
    shp = a.shape
    a = a.reshape(shp[:axis] + (N_DEV, shp[axis] // N_DEV) + shp[axis + 1:])
    return jnp.moveaxis(a, axis, 0)


def kernel(x, meta, norm1_g, w_in, conv_w, conv_b, conv_ln_g, conv_ln_b, w_conv_out, q_a_norm_g, w_uq, kv_a_norm_g, w_ukv, q_norm_g, k_norm_g, w_attn_out, hgrn_lb_logits, hgrn_norm_g, w_hgrn_out, w_out, norm2_g, w_ff1, w_ff2, loss_target, m_meta, m_norm1_g, m_w_in, m_conv_w, m_conv_b, m_conv_ln_g, m_conv_ln_b, m_w_conv_out, m_q_a_norm_g, m_w_uq, m_kv_a_norm_g, m_w_ukv, m_q_norm_g, m_k_norm_g, m_w_attn_out, m_hgrn_lb_logits, m_hgrn_norm_g, m_w_hgrn_out, m_w_out, m_norm2_g, m_w_ff1, m_w_ff2, v_meta, v_norm1_g, v_w_in, v_conv_w, v_conv_b, v_conv_ln_g, v_conv_ln_b, v_w_conv_out, v_q_a_norm_g, v_w_uq, v_kv_a_norm_g, v_w_ukv, v_q_norm_g, v_k_norm_g, v_w_attn_out, v_hgrn_lb_logits, v_hgrn_norm_g, v_w_hgrn_out, v_w_out, v_norm2_g, v_w_ff1, v_w_ff2):
    args = dict(locals())
    wts = {n: args[n] for n in WEIGHT_ORDER}
    mom = {n: args["m_" + n] for n in WEIGHT_ORDER}
    var = {n: args["v_" + n] for n in WEIGHT_ORDER}
    xi, yi, ci = _mesh_pos()
    me = 4 * xi + 2 * yi + ci

    shard = lambda l: [wts[n][l].astype(BF16) for n, _, _ in BIG]
    assert BIG[0][0] == "w_in"
    gathered = _all_gather(shard(0)[:1] + [_pack([wts[n] for n, _, _ in SMALL_SHARDED], F32, 8)], "gather_layer0")
    small = dict(zip([n for n, _, _ in SMALL_SHARDED],
                     [_join_shards(g, axis) for (_, _, axis), g in
                      zip(SMALL_SHARDED, _unpack(gathered[-1], [s for _, s, _ in SMALL_SHARDED], (N_DEV,)))]))
    rest0 = _copy_start(shard(0)[1:], True, "gather_rest0_start", 11, NEAR_RELS)
    pending = []

    def joined(mats, names_axes):
        full = {}
        for (n, _, axis), g in zip(names_axes, mats):
            if n == "w_in":
                full["w_in_padded"] = _w_in_from_shards(g)
            else:
                full[n] = _join_shards(g, axis - 1)
        return full

    def rest_of_layer0(u):
        own, lands = _copy_wait(rest0[0], rest0[1], rest0[2], rest0[3], u, True, "gather_rest0_wait", NEAR_RELS)
        lands = _forward_to_sibling(lands, "gather_rest0_forward")
        full = joined([lax.dynamic_update_index_in_dim(z, s, me, 0) for z, s in zip(lands, own)], BIG[1:])
        pending.append(_copy_start(shard(1), True, "gather_layer1_start", 5))
        full["w_conv_out"] = _after(full["w_conv_out"], pending[0][4])
        return _layer_weights(full)

    def layer_weights(l, xp):
        full = {n: wts[n][l] for n, _ in REPLICATED}
        full["conv_w"] = small["conv_w"][l]
        if l == 0:
            full.update(joined(gathered[:1], BIG[:1]))
            full["norm1_g"] = _after(full["norm1_g"], rest0[4])
            full["late"] = rest_of_layer0
        else:
            s_sems, r_sems, sent, lands, _ = pending[0]
            own, lands = _copy_wait(s_sems, r_sems, sent, lands, xp, True, "gather_layer1_wait")
            full.update(joined([lax.dynamic_update_index_in_dim(z, s, me, 0) for z, s in zip(lands, own)], BIG))
        return full, xp

    big_names = [n for n, _, _ in BIG]
    early = [n for n in big_names if n in ("w_out", "w_ff1", "w_ff2")]
    late = [n for n in big_names if n not in early]
    cut = lambda g, names: [(g[n + "_shards"] if n + "_shards" in g else _cut_shards(g[n], axis - 1)).astype(BF16)
                            for n, _, axis in BIG if n in names]
    layer_grads = [None] * DEPTH
    flight = {}

    def layer_mid(l, g, w_out):
        if l == 0:
            flight["l0_early"] = _copy_start(cut(g, early), False, "scatter_layer0_early_start", 7)
            w_out = _after(w_out, flight["l0_early"][4])
        return w_out

    def layer_done(l, g, dx):
        layer_grads[l] = g
        if l == 1:
            flight["l1"] = _copy_start(cut(g, big_names), False, "scatter_l1_start", 6)
            dx = _after(dx, flight["l1"][4])
        return dx

    def layer_matrices(l, mats, norm_g):
        if l == 0:
            flight["l0_late"] = _copy_start(cut(mats, late), False, "scatter_l0_late_start", 8)
            norm_g = _after(norm_g, flight["l0_late"][4])
        return norm_g

    loss, grad_x, g_meta, g_lb = _run_step(x[0], loss_target[0], small["meta"], wts["hgrn_lb_logits"],
                                           layer_weights, layer_done, layer_mid, layer_matrices)

    grads = {k: jnp.stack([layer_grads[l][k] for l in range(DEPTH)]) for k in layer_grads[0]
             if k not in big_names and not k.endswith("_shards")}
    grads["hgrn_lb_logits"] = g_lb
    own = [jnp.concatenate([grads[n] for n in grp], axis=0) for grp in VEC_GROUPS]
    own.append(jnp.broadcast_to(loss.reshape(1, 1), (1, 128)))
    flight["small"] = _copy_start(own, True, "gather_small_grads_start", 9)
    cuts = [_cut_shards(g_meta, 1), _cut_shards(grads["conv_w"], 2)]
    flight["small_x"] = _copy_start(cuts, False, "scatter_small_grads_start", 10)
    started = flight["small_x"][4]

    def arrive(key, names, after):
        s_sems, r_sems, sent, lands, _ = flight[key]
        sent, lands = _copy_wait(s_sems, r_sems, sent, lands, after, False, f"scatter_{key}_wait")
        return {n: lax.dynamic_update_index_in_dim(z, lax.dynamic_index_in_dim(s, me, 0, keepdims=False), me, 0)
                for n, z, s in zip(names, lands, sent)}

    out = {}

    def update(names, recv0, recv1):
        for n in names:
            res4 = _adamw_layers([recv0[n], recv1[n]], wts[n], mom[n], var[n], "adamw_" + n)
            for kind, a in zip(("grad_", "delta_", "new_m_", "new_v_"), res4):
                out[kind + n] = a

    recv1 = arrive("l1", big_names, started)
    recv0 = arrive("l0_early", early, started)
    update(early, recv0, recv1)

    s_sems, r_sems, sent, lands, _ = flight["small"]
    updated = lax.optimization_barrier(tuple(out["grad_" + n] for n in early))
    own, lands = _copy_wait(s_sems, r_sems, sent, lands, updated[0], True, "gather_small_grads_wait")
    s_sems, r_sems, sent, lands_x, _ = flight["small_x"]
    sent, lands_x = _copy_wait(s_sems, r_sems, sent, lands_x, updated[0], False, "scatter_small_grads_wait")
    own += [lax.dynamic_index_in_dim(s, me, 0, keepdims=False) for s in sent]
    small_out, loss = _adamw_small(own, lands + lands_x, wts, mom, var, "adamw_small")
    out.update(small_out)
    loss = loss[0, 0]

    recv0 = arrive("l0_late", late, small_out["grad_norm1_g"])
    update(late, recv0, recv1)

    res = [loss, grad_x[None]]
    for kind in ("grad_", "delta_", "new_m_", "new_v_"):
        res += [out[kind + n] for n in WEIGHT_ORDER]
    return tuple(res)
```

```python
import functools

import numpy as np
import jax
import jax.numpy as jnp
from jax import lax
from jax.experimental import pallas as pl
from jax.experimental.pallas import tpu as pltpu

F32 = jnp.float32
BF16 = jnp.bfloat16

D_MODEL = 1024
DEPTH = 2
N_META = 16
PAD_FRONT = 112
ROW0 = PAD_FRONT + N_META
EPS = 1e-6
GATE_CLAMP = 1.0 - 1e-6
CONV_DIM = 512
CONV_K = 31
HEADS = 8
Q_RANK = 256
KV_RANK = 128
NOPE = 64
ROPE = 32
V_DIM = 64
QK_DIM = NOPE + ROPE
HEAD_W = 128
ROPE_BASE = 10000.0
HG_HEADS = 4
HG_DK = 128
HG_DV = 128
HG_CHUNK = 64
D_FF = 4096
N_IN = 6560
C_CONV_A, C_CONV_G, C_GATE, C_CQ, C_CKV, C_KR, C_HQ, C_HF, C_HI, C_HG = (
    0, 512, 1024, 4096, 4352, 4480, 4608, 5120, 5632, 6144)
N_IN_P = 6656
O_CQ, O_KR, O_HQ, O_GATE = 1024, 1408, 1440, 3488
KR_LANE = NOPE

ADAM_LR = 0.001
ADAM_B1 = 0.9
ADAM_B2 = 0.999
ADAM_EPS = 1e-08
ADAM_WD = 0.01
ADAM_STEP = 10

N_DEV = 8
VMEM_LIMIT = 56 * 1024 * 1024
MESH = pl.DeviceIdType.MESH


def _pick(n, cands):
    for c in cands:
        if n % c == 0:
            return c
    raise ValueError(f"no tile for {n}")


def _cparams(sem, **kw):
    return pltpu.CompilerParams(dimension_semantics=sem, vmem_limit_bytes=VMEM_LIMIT, **kw)


def _relu2(v):
    return jnp.square(jnp.maximum(v, 0.0))


def _mm(a, b, *, ta=False, tb=False, out_dtype=F32, res=None, a_fn=None, epi=None, name):
    M, K = (a.shape[1], a.shape[0]) if ta else a.shape
    N = b.shape[0] if tb else b.shape[1]
    assert (b.shape[1] if tb else b.shape[0]) == K, (a.shape, b.shape, ta, tb)
    tm = _pick(M, (1056, 1024, 512, 384, 256, 128, 96))
    tn = _pick(N, (1664, 1024, 512, 384, 256, 128))
    tk = _pick(K, (1664, 1056, 1024, 512, 384, 256, 128, 96) if ta else (1664, 1408, 1024, 512, 384, 256, 128))
    nk = K // tk
    dims = (((0 if ta else 1,), (1 if tb else 0,)), ((), ()))
    extras = ([res] if res is not None else []) + ([epi[0]] if epi is not None else [])

    def body(*refs):
        a_ref, b_ref = refs[0], refs[1]
        r_ref = refs[2] if res is not None else None
        e_ref = refs[2 + (res is not None)] if epi is not None else None
        o_ref = refs[2 + len(extras)]
        acc = refs[-1] if nk > 1 else None
        k = pl.program_id(2)
        av = a_ref[...]
        if a_fn is not None:
            av = a_fn(av.astype(F32))
        p = lax.dot_general(av.astype(BF16), b_ref[...].astype(BF16), dims, preferred_element_type=F32)

        def finish(total):
            if e_ref is not None:
                total = epi[1](total, e_ref[...].astype(F32))
            if r_ref is not None:
                total = total + r_ref[...].astype(F32)
            o_ref[...] = total.astype(o_ref.dtype)

        if nk == 1:
            finish(p)
        else:
            @pl.when(k == 0)
            def _():
                acc[...] = p

            @pl.when(k > 0)
            def _():
                acc[...] += p

            @pl.when(k == nk - 1)
            def _():
                finish(acc[...])

    a_spec = pl.BlockSpec((tk, tm), lambda i, j, k: (k, i)) if ta else pl.BlockSpec((tm, tk), lambda i, j, k: (i, k))
    b_spec = pl.BlockSpec((tn, tk), lambda i, j, k: (j, k)) if tb else pl.BlockSpec((tk, tn), lambda i, j, k: (k, j))
    o_spec = pl.BlockSpec((tm, tn), lambda i, j, k: (i, j))
    in_specs = [a_spec, b_spec] + [o_spec] * len(extras)
    args = (a, b) + tuple(extras)
    return pl.pallas_call(
        body, name=name, grid=(M // tm, N // tn, nk), in_specs=in_specs, out_specs=o_spec,
        out_shape=jax.ShapeDtypeStruct((M, N), out_dtype),
        scratch_shapes=[pltpu.VMEM((tm, tn), F32)] if nk > 1 else [],
        compiler_params=_cparams(("parallel", "parallel", "arbitrary")),
    )(*args)


class Row:
    def __init__(self, arr, width=None, col=0, piece=None):
        self.arr = arr
        self.width = arr.shape[1] if width is None else width
        assert col % self.width == 0
        self.blk = col // self.width
        self.piece = self.width if piece is None else piece

    def spec(self, tm):
        blk = self.blk
        return pl.BlockSpec((tm, self.width), lambda i: (i, blk))


def _split(v, piece):
    w = v.shape[-1]
    if piece == w:
        return v
    return [v[:, j * piece:(j + 1) * piece] for j in range(w // piece)]


def _store(ref, val, dtype=None):
    if isinstance(val, (list, tuple)):
        piece = val[0].shape[-1]
        for j, p in enumerate(val):
            ref[:, j * piece:(j + 1) * piece] = p.astype(ref.dtype)
    else:
        ref[...] = val.astype(ref.dtype)


def _row_tile(T):
    return _pick(T, (384, 352, 192, 128))


def _param2d(p):
    return p.reshape(1, -1).astype(F32)


def _rowwise(fn, T, rows, params, outs, name, transposed=False):
    tm = _row_tile(T)
    nr, npar = len(rows), len(params)
    par = [(_param2d(p), piece) for p, piece in params]

    def body(*refs):
        rid = pl.program_id(0) * tm + lax.broadcasted_iota(jnp.int32, (tm, 1), 0)
        rv = [_split(refs[n][...].astype(F32), rows[n].piece) for n in range(nr)]
        pv = [_split(refs[nr + n][...], par[n][1]) for n in range(npar)]
        res = fn(rid, rv, pv)
        for n, val in enumerate(res):
            _store(refs[nr + npar + n], val)
        if transposed:
            refs[-1][...] = res[0].T.astype(refs[-1].dtype)

    out_specs = [pl.BlockSpec((tm, w), lambda i: (i, 0)) for w, _ in outs]
    out_shape = [jax.ShapeDtypeStruct((T, w), dt) for w, dt in outs]
    if transposed:
        out_specs.append(pl.BlockSpec((outs[0][0], tm), lambda i: (0, i)))
        out_shape.append(jax.ShapeDtypeStruct((outs[0][0], T), outs[0][1]))
    return pl.pallas_call(
        body, name=name, grid=(T // tm,),
        in_specs=[r.spec(tm) for r in rows] + [pl.BlockSpec(p.shape, lambda i: (0, 0)) for p, _ in par],
        out_specs=out_specs, out_shape=out_shape,
        compiler_params=_cparams(("parallel",)),
    )(*[r.arr for r in rows], *[p for p, _ in par])


def _rowwise_bwd(fn, T, rows, params, cts, drow, name, add=None):
    tm = _row_tile(T)
    nr, npar, nct = len(rows), len(params), len(cts)
    par = [(_param2d(p), piece) for p, piece in params]
    didx = sorted(drow)
    has_add = add is not None

    def body(*refs):
        i = pl.program_id(0)
        rid = i * tm + lax.broadcasted_iota(jnp.int32, (tm, 1), 0)
        rv = [_split(refs[n][...].astype(F32), rows[n].piece) for n in range(nr)]
        pv = [_split(refs[nr + n][...], par[n][1]) for n in range(npar)]
        cv = [_split(refs[nr + npar + n][...].astype(F32), cts[n].piece) for n in range(nct)]
        base = nr + npar + nct + (1 if has_add else 0)
        d_refs = refs[base:base + len(didx)]
        p_refs = refs[base + len(didx):]

        def g(dvals, pvals):
            full = list(rv)
            for n, v in zip(didx, dvals):
                full[n] = v
            return fn(rid, full, pvals)

        _, vjp = jax.vjp(g, [rv[n] for n in didx], pv)
        d_rows, d_pars = vjp(cv)
        for slot, n in enumerate(didx):
            val = d_rows[slot]
            if has_add and add[0] == n:
                assert not isinstance(val, (list, tuple))
                val = val + refs[nr + npar + nct][...].astype(F32)
            _store(d_refs[slot], val)

        @pl.when(i == 0)
        def _():
            for r in p_refs:
                r[...] = jnp.zeros_like(r)

        for r, val in zip(p_refs, d_pars):
            if isinstance(val, (list, tuple)):
                piece = val[0].shape[-1]
                for j, p in enumerate(val):
                    r[:, j * piece:(j + 1) * piece] += p
            else:
                r[...] += val

    in_specs = ([r.spec(tm) for r in rows] + [pl.BlockSpec(p.shape, lambda i: (0, 0)) for p, _ in par]
                + [c.spec(tm) for c in cts])
    args = [r.arr for r in rows] + [p for p, _ in par] + [c.arr for c in cts]
    if has_add:
        in_specs.append(pl.BlockSpec((tm, rows[add[0]].width), lambda i: (i, 0)))
        args.append(add[1])
    out_specs = ([pl.BlockSpec((tm, rows[n].width), lambda i: (i, 0)) for n in didx]
                 + [pl.BlockSpec(p.shape, lambda i: (0, 0)) for p, _ in par])
    out_shape = ([jax.ShapeDtypeStruct((T, rows[n].width), drow[n]) for n in didx]
                 + [jax.ShapeDtypeStruct(p.shape, F32) for p, _ in par])
    res = pl.pallas_call(
        body, name=name, grid=(T // tm,), in_specs=in_specs, out_specs=out_specs, out_shape=out_shape,
        compiler_params=_cparams(("arbitrary",)),
    )(*args)
    return list(res[:len(didx)]), list(res[len(didx):])


def _f_rms(rid, rv, pv):
    x, g = rv[0], pv[0]
    return [x * lax.rsqrt(jnp.mean(x * x, axis=-1, keepdims=True) + EPS) * g]


def _f_glu(rid, rv, pv):
    a, gt = rv
    return [a * jax.nn.sigmoid(gt) * (rid >= PAD_FRONT).astype(F32)]


def _f_lnsilu(rid, rv, pv):
    x = rv[0]
    g, b = pv
    mu = jnp.mean(x, axis=-1, keepdims=True)
    xc = x - mu
    y = xc * lax.rsqrt(jnp.mean(xc * xc, axis=-1, keepdims=True) + EPS) * g + b
    return [y * jax.nn.sigmoid(y)]


@functools.partial(jax.custom_vjp, nondiff_argnums=(1,))
def _lane_roll(x, shift):
    return pltpu.roll(x, shift, 1)


def _lane_roll_fwd(x, shift):
    return pltpu.roll(x, shift, 1), None


def _lane_roll_bwd(shift, _, g):
    return (pltpu.roll(g, (HEAD_W - shift) % HEAD_W, 1),)


_lane_roll.defvjp(_lane_roll_fwd, _lane_roll_bwd)


def _head_norm_rope(xh, g, c, s1, s2):
    y = xh * lax.rsqrt(jnp.sum(xh * xh, axis=-1, keepdims=True) * (1.0 / QK_DIM) + EPS) * g
    half = ROPE // 2
    return y * c + _lane_roll(y, HEAD_W - half) * s1 + _lane_roll(y, half) * s2


def _f_qrope(rid, rv, pv):
    q, c, s1, s2 = rv
    return [[_head_norm_rope(qh, pv[0], c, s1, s2) * ATT_SCALE for qh in q]]


def _f_krope(rid, rv, pv):
    k, kr, c, s1, s2 = rv
    return [[_head_norm_rope(kh + kr, pv[0], c, s1, s2) for kh in k]]


def _f_hgrn_prep(rid, rv, pv):
    hf, hi = rv
    m = (rid >= PAD_FRONT).astype(F32)
    kk = (1.0 - pv[0]) * jax.nn.sigmoid(-hf) * m
    lf = jnp.log1p(-jnp.minimum(kk, GATE_CLAMP))
    vv = hi * jax.nn.sigmoid(hi) * m
    return [kk, lf, vv]


def _f_hgrn_out(rid, rv, pv):
    o, hg = rv
    ng = pv[0]
    out = []
    for oh, gh, nh in zip(o, hg, ng):
        y = oh * lax.rsqrt(jnp.mean(oh * oh, axis=-1, keepdims=True) + EPS) * nh
        out.append(y * (gh * jax.nn.sigmoid(gh)))
    return [out]


def _f_mix(rid, rv, pv):
    g0, g1, g2, ya, yb, yc = rv
    return [jax.nn.sigmoid(g0) * ya + jax.nn.sigmoid(g1) * yb + jax.nn.sigmoid(g2) * yc]


def _f_relu2(rid, rv, pv):
    return [jnp.square(jax.nn.relu(rv[0]))]


def _loss_head(x2, tgt, T):
    tm = _row_tile(T)

    def body(x_ref, t_ref, dx_ref, l_ref):
        i = pl.program_id(0)
        rid = i * tm + lax.broadcasted_iota(jnp.int32, (tm, 1), 0)
        diff = (x_ref[...] - t_ref[...]) * (rid >= ROW0).astype(F32)
        dx_ref[...] = diff * (1.0 / D_MODEL)

        @pl.when(i == 0)
        def _():
            l_ref[...] = jnp.zeros_like(l_ref)

        l_ref[...] += jnp.sum(diff * diff, axis=0, keepdims=True)

    spec = pl.BlockSpec((tm, D_MODEL), lambda i: (i, 0))
    return pl.pallas_call(
        body, name="loss_head", grid=(T // tm,), in_specs=[spec, spec],
        out_specs=[spec, pl.BlockSpec((1, D_MODEL), lambda i: (0, 0))],
        out_shape=[jax.ShapeDtypeStruct((T, D_MODEL), F32), jax.ShapeDtypeStruct((1, D_MODEL), F32)],
        compiler_params=_cparams(("arbitrary",)),
    )(x2, tgt)


HALO = 32


CONV_ROWS = 64


def _conv_lanes():
    return [slice(c, c + 128) for c in range(0, CONV_DIM, 128)]


def _conv_tile(T):
    return _pick(T, (384, 128))


def _conv_fwd(h, w, b, T, name):
    tr = _conv_tile(T)
    ratio = tr // HALO
    wp = jnp.zeros((HALO, CONV_DIM), F32).at[:CONV_K].set(w)

    def body(m_ref, h_ref, w_ref, b_ref, o_ref, win):
        i = pl.program_id(0)
        win[0:HALO, :] = h_ref[...] * (i > 0).astype(F32)
        win[HALO:, :] = m_ref[...]
        for cs in _conv_lanes():
            wv, bv = w_ref[:, cs], b_ref[:, cs]
            for r0 in range(0, tr, CONV_ROWS):
                acc = jnp.broadcast_to(bv, (CONV_ROWS, 128))
                for k in range(CONV_K):
                    acc = acc + wv[k:k + 1] * win[pl.ds(HALO - (CONV_K - 1) + k + r0, CONV_ROWS), cs]
                o_ref[r0:r0 + CONV_ROWS, cs] = acc

    return pl.pallas_call(
        body, name=name, grid=(T // tr,),
        in_specs=[pl.BlockSpec((tr, CONV_DIM), lambda i: (i, 0)),
                  pl.BlockSpec((HALO, CONV_DIM), lambda i: (jnp.maximum(i * ratio - 1, 0), 0)),
                  pl.BlockSpec((HALO, CONV_DIM), lambda i: (0, 0)),
                  pl.BlockSpec((1, CONV_DIM), lambda i: (0, 0))],
        out_specs=pl.BlockSpec((tr, CONV_DIM), lambda i: (i, 0)),
        out_shape=jax.ShapeDtypeStruct((T, CONV_DIM), F32),
        scratch_shapes=[pltpu.VMEM((tr + HALO, CONV_DIM), F32)],
        compiler_params=_cparams(("parallel",)),
    )(h, h, wp, _param2d(b))


def _conv_bwd(h, w, dy, T, name):
    tr = _conv_tile(T)
    ratio = tr // HALO
    n_t = T // tr
    last_halo = T // HALO - 1
    wp = jnp.zeros((HALO, CONV_DIM), F32).at[:CONV_K].set(w)

    def body(hm_ref, hh_ref, dm_ref, dh_ref, w_ref, dx_ref, dw_ref, db_ref, hwin, dwin):
        i = pl.program_id(0)
        hwin[0:HALO, :] = hh_ref[...] * (i > 0).astype(F32)
        hwin[HALO:, :] = hm_ref[...]
        dwin[0:tr, :] = dm_ref[...]
        dwin[tr:, :] = dh_ref[...] * (i < n_t - 1).astype(F32)

        @pl.when(i == 0)
        def _():
            dw_ref[...] = jnp.zeros_like(dw_ref)
            db_ref[...] = jnp.zeros_like(db_ref)

        db_ref[...] += jnp.sum(dm_ref[...], axis=0, keepdims=True)
        fold = lambda a: functools.reduce(jnp.add, [a[r:r + 8] for r in range(0, CONV_ROWS, 8)])
        for cs in _conv_lanes():
            wv = w_ref[:, cs]
            dws = [jnp.zeros((8, 128), F32) for _ in range(CONV_K)]
            for r0 in range(0, tr, CONV_ROWS):
                acc = jnp.zeros((CONV_ROWS, 128), F32)
                for k in range(CONV_K):
                    acc = acc + wv[k:k + 1] * dwin[pl.ds(CONV_K - 1 - k + r0, CONV_ROWS), cs]
                dx_ref[r0:r0 + CONV_ROWS, cs] = acc
                dy_t = dm_ref[r0:r0 + CONV_ROWS, cs]
                for k in range(CONV_K):
                    dws[k] = dws[k] + fold(dy_t * hwin[pl.ds(HALO - (CONV_K - 1) + k + r0, CONV_ROWS), cs])
            for k in range(CONV_K):
                dw_ref[k:k + 1, cs] += jnp.sum(dws[k], axis=0, keepdims=True)

    main = pl.BlockSpec((tr, CONV_DIM), lambda i: (i, 0))
    return pl.pallas_call(
        body, name=name, grid=(n_t,),
        in_specs=[main,
                  pl.BlockSpec((HALO, CONV_DIM), lambda i: (jnp.maximum(i * ratio - 1, 0), 0)),
                  main,
                  pl.BlockSpec((HALO, CONV_DIM), lambda i: (jnp.minimum((i + 1) * ratio, last_halo), 0)),
                  pl.BlockSpec((HALO, CONV_DIM), lambda i: (0, 0))],
        out_specs=[main, pl.BlockSpec((HALO, CONV_DIM), lambda i: (0, 0)), pl.BlockSpec((1, CONV_DIM), lambda i: (0, 0))],
        out_shape=[jax.ShapeDtypeStruct((T, CONV_DIM), F32), jax.ShapeDtypeStruct((HALO, CONV_DIM), F32),
                   jax.ShapeDtypeStruct((1, CONV_DIM), F32)],
        scratch_shapes=[pltpu.VMEM((tr + HALO, CONV_DIM), F32), pltpu.VMEM((tr + HALO, CONV_DIM), F32)],
        compiler_params=_cparams(("arbitrary",)),
    )(h, h, dy, dy, wp)


NEG = -1e30
ATT_SCALE = QK_DIM ** -0.5
_NT = (((1,), (1,)), ((), ()))
_TN = (((0,), (0,)), ((), ()))


def _att_blk(T):
    return _pick(T, (384, 128))


def _att_mask(i, j, blk):
    kpos = j * blk + lax.broadcasted_iota(jnp.int32, (blk, blk), 0)
    qpos = i * blk + lax.broadcasted_iota(jnp.int32, (blk, blk), 1)
    return (kpos <= qpos) & (kpos >= PAD_FRONT)


def _t32(a):
    return a.astype(F32).T.astype(BF16)


def _attn_fwd(q, k, v, T, name):
    blk = _att_blk(T)
    nq = T // blk

    def body(q_ref, k_ref, v_ref, o_ref, lse_ref, vt):
        i = pl.program_id(1)

        @pl.when(i == 0)
        def _():
            def tr(j, c):
                vt[j] = _t32(v_ref[pl.ds(pl.multiple_of(j * blk, blk), blk), :])
                return c

            lax.fori_loop(0, nq, tr, 0)

        qb = q_ref[...]

        def step(js, carry, masked):
            m, l, acc = carry
            ss = []
            for j in js:
                kb = k_ref[pl.ds(pl.multiple_of(j * blk, blk), blk), :]
                s = lax.dot_general(kb, qb, _NT, preferred_element_type=F32)
                ss.append(jnp.where(_att_mask(i, j, blk), s, NEG) if masked else s)
            m_new = m
            for s in ss:
                m_new = jnp.maximum(m_new, jnp.max(s, axis=0, keepdims=True))
            alpha = jnp.exp(m - m_new)
            l = alpha * l
            acc = alpha * acc
            for j, s in zip(js, ss):
                p = jnp.exp(s - m_new)
                l = l + jnp.sum(p, axis=0, keepdims=True)
                acc = acc + jnp.dot(vt[j], p.astype(BF16), preferred_element_type=F32)
            return m_new, l, acc

        init = (jnp.full((1, blk), NEG, F32), jnp.zeros((1, blk), F32), jnp.zeros((HEAD_W, blk), F32))
        later = jnp.minimum(i, 1)
        carry = lax.fori_loop(0, 1 - later, lambda t, c: step([i], c, True), init)
        carry = lax.fori_loop(0, later, lambda t, c: step([i, 0], c, True), carry)
        n_free = jnp.maximum(i - 1, 0)
        n4 = n_free // 4
        carry = lax.fori_loop(0, n4, lambda t, c: step([1 + 4 * t + d for d in range(4)], c, False), carry)
        rest = n_free - 4 * n4
        carry = lax.fori_loop(0, rest // 2, lambda t, c: step([i - rest, i - rest + 1], c, False), carry)
        m, l, acc = lax.fori_loop(0, rest % 2, lambda t, c: step([i - 1], c, False), carry)
        o_ref[...] = (acc / l).T.astype(o_ref.dtype)
        lse_ref[0, 0] = m + jnp.log(l)

    full = pl.BlockSpec((T, HEAD_W), lambda h, i: (0, h))
    return pl.pallas_call(
        body, name=name, grid=(HEADS, nq),
        in_specs=[pl.BlockSpec((blk, HEAD_W), lambda h, i: (i, h)), full, full],
        out_specs=[pl.BlockSpec((blk, HEAD_W), lambda h, i: (i, h)),
                   pl.BlockSpec((1, 1, 1, blk), lambda h, i: (h, i, 0, 0))],
        out_shape=[jax.ShapeDtypeStruct((T, HEADS * HEAD_W), BF16), jax.ShapeDtypeStruct((HEADS, nq, 1, blk), F32)],
        scratch_shapes=[pltpu.VMEM((nq, HEAD_W, blk), BF16)],
        compiler_params=_cparams(("parallel", "arbitrary")),
    )(q, k, v)


def _attn_bwd(q, k, v, o, lse, do, T, name):
    blk = _att_blk(T)
    nq = T // blk

    def body(q_ref, k_ref, v_ref, o_ref, lse_ref, do_ref, dq_ref, dk_ref, dv_ref, delta, dqt, dk_acc, dv_acc):
        j = pl.program_id(1)

        @pl.when(j == 0)
        def _():
            dqt[...] = jnp.zeros_like(dqt)

            def dstep(i, c):
                r0 = pl.multiple_of(i * blk, blk)
                prod = do_ref[pl.ds(r0, blk), :].astype(F32) * o_ref[pl.ds(r0, blk), :].astype(F32)
                delta[i] = jnp.sum(prod.T, axis=0, keepdims=True)
                return c

            lax.fori_loop(0, nq, dstep, 0)

        kb = k_ref[...]
        vb = v_ref[...]
        kbt = _t32(kb)
        dk_acc[...] = jnp.zeros_like(dk_acc)
        dv_acc[...] = jnp.zeros_like(dv_acc)

        def step(qs, masked):
            dvs, dks = [], []
            for i in qs:
                r0 = pl.multiple_of(i * blk, blk)
                qb = q_ref[pl.ds(r0, blk), :]
                dob = do_ref[pl.ds(r0, blk), :]
                s = lax.dot_general(kb, qb, _NT, preferred_element_type=F32)
                p = jnp.exp(s - lse_ref[0, i])
                if masked:
                    p = jnp.where(_att_mask(i, j, blk), p, 0.0)
                dvs.append(jnp.dot(p.astype(BF16), dob, preferred_element_type=F32))
                dp = lax.dot_general(vb, dob, _NT, preferred_element_type=F32)
                ds = (p * (dp - delta[i])).astype(BF16)
                dks.append(jnp.dot(ds, qb, preferred_element_type=F32))
                dqt[i] += jnp.dot(kbt, ds, preferred_element_type=F32)
            dv_acc[...] += functools.reduce(jnp.add, dvs)
            dk_acc[...] += functools.reduce(jnp.add, dks)

        def loop(lo, masked):
            n = nq - lo
            n3 = n // 3
            rest = n - 3 * n3

            def triple(t, c):
                step([lo + 3 * t + d for d in range(3)], masked)
                return c

            def pair(t, c):
                step([nq - 2, nq - 1], masked)
                return c

            def last(t, c):
                step([nq - 1], masked)
                return c

            lax.fori_loop(0, n3, triple, 0)
            lax.fori_loop(0, rest // 2, pair, 0)
            lax.fori_loop(0, rest % 2, last, 0)

        @pl.when(j == 0)
        def _():
            loop(0, True)

        @pl.when(j > 0)
        def _():
            step([j], True)
            loop(j + 1, False)

        dk_ref[...] = dk_acc[...].astype(dk_ref.dtype)
        dv_ref[...] = dv_acc[...].astype(dv_ref.dtype)

        @pl.when(j == nq - 1)
        def _():
            def wstep(i, c):
                dq_ref[pl.ds(pl.multiple_of(i * blk, blk), blk), :] = dqt[i].T
                return c

            lax.fori_loop(0, nq, wstep, 0)

    full = pl.BlockSpec((T, HEAD_W), lambda h, j: (0, h))
    kblk = pl.BlockSpec((blk, HEAD_W), lambda h, j: (j, h))
    wide = (T, HEADS * HEAD_W)
    return pl.pallas_call(
        body, name=name, grid=(HEADS, nq),
        in_specs=[full, kblk, kblk, full, pl.BlockSpec((1, nq, 1, blk), lambda h, j: (h, 0, 0, 0)), full],
        out_specs=[full, kblk, kblk],
        out_shape=[jax.ShapeDtypeStruct(wide, F32), jax.ShapeDtypeStruct(wide, BF16), jax.ShapeDtypeStruct(wide, BF16)],
        scratch_shapes=[pltpu.VMEM((nq, 1, blk), F32), pltpu.VMEM((nq, HEAD_W, blk), F32),
                        pltpu.VMEM((blk, HEAD_W), F32), pltpu.VMEM((blk, HEAD_W), F32)],
        compiler_params=_cparams(("parallel", "arbitrary")),
    )(q, k, v, o, lse, do)


HG_NB = 6
C = HG_CHUNK
_HI = lax.Precision.HIGHEST


def _tri(lower):
    r = lax.broadcasted_iota(jnp.int32, (C, C), 0)
    c = lax.broadcasted_iota(jnp.int32, (C, C), 1)
    return ((c <= r) if lower else (c >= r)).astype(F32)


HG_SUB = 8
N_SUB = C // HG_SUB


def _hg_split_decay(b, I, rid):
    lo = I * HG_SUB
    r = b[lo:lo + 1]
    eq = jnp.exp(b[lo:lo + HG_SUB] - r)
    ek = jnp.concatenate([jnp.exp(jnp.minimum(r - b[:lo], 0.0)), jnp.zeros((C - lo, HG_DK), F32)], axis=0)
    return eq, ek


def _hg_intra_fwd(q, k, v, b):
    rid = lax.broadcasted_iota(jnp.int32, (C, 1), 0)
    tid = lax.broadcasted_iota(jnp.int32, (HG_SUB, 1), 0)
    a_rows = [jnp.zeros((HG_SUB, C), F32)]
    blocks = []
    for I in range(N_SUB):
        lo = I * HG_SUB
        q_i, b_i = q[lo:lo + HG_SUB], b[lo:lo + HG_SUB]
        if I > 0:
            eq, ek = _hg_split_decay(b, I, rid)
            a_rows.append(lax.dot_general((q_i * eq).astype(BF16), (k * ek).astype(BF16), _NT,
                                          preferred_element_type=F32))
        o_i = jnp.zeros((HG_SUB, HG_DV), F32)
        for s in range(HG_SUB):
            r = lo + s
            e = jnp.exp(jnp.minimum(b_i - b[r:r + 1], 0.0))
            a = jnp.sum(q_i * k[r:r + 1] * e, axis=-1, keepdims=True)
            o_i = o_i + jnp.where(tid >= s, a, 0.0) * v[r:r + 1]
        blocks.append(o_i)
    a_off = jnp.concatenate(a_rows, axis=0).astype(BF16)
    return jnp.dot(a_off, v.astype(BF16), preferred_element_type=F32) + jnp.concatenate(blocks, axis=0)


def _hg_intra_bwd(q, k, v, b, do, dk_s, dv_s):
    rid = lax.broadcasted_iota(jnp.int32, (C, 1), 0)
    tid = lax.broadcasted_iota(jnp.int32, (HG_SUB, 1), 0)
    da_all = lax.dot_general(do.astype(BF16), v.astype(BF16), _NT, preferred_element_type=F32)
    a_rows = [jnp.zeros((HG_SUB, C), F32)]
    dq_blocks = []
    dk = jnp.zeros((C, HG_DK), F32)
    for I in range(N_SUB):
        lo = I * HG_SUB
        q_i, b_i, do_i = q[lo:lo + HG_SUB], b[lo:lo + HG_SUB], do[lo:lo + HG_SUB]
        dq_i = jnp.zeros((HG_SUB, HG_DK), F32)
        if I > 0:
            eq, ek = _hg_split_decay(b, I, rid)
            qs, ks = (q_i * eq).astype(BF16), (k * ek).astype(BF16)
            a_rows.append(lax.dot_general(qs, ks, _NT, preferred_element_type=F32))
            da = da_all[lo:lo + HG_SUB].astype(BF16)
            dq_i = jnp.dot(da, ks, preferred_element_type=F32) * eq
            dk = dk + lax.dot_general(da, qs, _TN, preferred_element_type=F32) * ek
        for s in range(HG_SUB):
            r = lo + s
            e = jnp.where(tid >= s, jnp.exp(jnp.minimum(b_i - b[r:r + 1], 0.0)), 0.0)
            a = jnp.sum(q_i * k[r:r + 1] * e, axis=-1, keepdims=True)
            g = jnp.sum(do_i * v[r:r + 1], axis=-1, keepdims=True) * e
            dq_i = dq_i + g * k[r:r + 1]
            dk_s[r:r + 1, :] = jnp.sum(g * q_i, axis=0, keepdims=True)
            dv_s[r:r + 1, :] = jnp.sum(a * do_i, axis=0, keepdims=True)
        dq_blocks.append(dq_i)
    a_off = jnp.concatenate(a_rows, axis=0).astype(BF16)
    dv = lax.dot_general(a_off, do.astype(BF16), _TN, preferred_element_type=F32)
    return jnp.concatenate(dq_blocks, axis=0), dk + dk_s[...], dv + dv_s[...]


def _hgrn_fwd(u, kk, lf, vv, T, name):
    nb = _pick(T // C, (HG_NB, 3, 2, 1))
    rows = nb * C
    qblk = C_HQ // HG_DK

    def body(q_ref, k_ref, lf_ref, v_ref, o_ref, st_ref, st):
        @pl.when(pl.program_id(1) == 0)
        def _():
            st[...] = jnp.zeros_like(st)

        lower = _tri(True)
        for n in range(nb):
            sl = slice(n * C, (n + 1) * C)
            q, k, v = q_ref[sl, :].astype(F32), k_ref[sl, :], v_ref[sl, :]
            b = jnp.dot(lower, lf_ref[sl, :], precision=_HI, preferred_element_type=F32)
            s_t = st[...]
            st_ref[0, n] = s_t
            qe = (q * jnp.exp(b)).astype(BF16)
            o = lax.dot_general(qe, s_t.astype(BF16), _NT, preferred_element_type=F32)
            o_ref[sl, :] = o + _hg_intra_fwd(q, k, v, b)
            bl = b[C - 1:C, :]
            kd = (k * jnp.exp(bl - b)).astype(BF16)
            st[...] = s_t * jnp.exp(bl) + lax.dot_general(v.astype(BF16), kd, _TN, preferred_element_type=F32)

    col = lambda off: pl.BlockSpec((rows, HG_DK), lambda h, c: (c, h + off))
    return pl.pallas_call(
        body, name=name, grid=(HG_HEADS, T // rows),
        in_specs=[col(qblk), col(0), col(0), col(0)],
        out_specs=[col(0), pl.BlockSpec((1, nb, HG_DV, HG_DK), lambda h, c: (h, c, 0, 0))],
        out_shape=[jax.ShapeDtypeStruct((T, HG_HEADS * HG_DV), F32),
                   jax.ShapeDtypeStruct((HG_HEADS, T // C, HG_DV, HG_DK), F32)],
        scratch_shapes=[pltpu.VMEM((HG_DV, HG_DK), F32)],
        compiler_params=_cparams(("parallel", "arbitrary")),
    )(u, kk, lf, vv)


def _hgrn_bwd(u, kk, lf, vv, states, do, T, name):
    nb = _pick(T // C, (HG_NB, 3, 2, 1))
    rows = nb * C
    n_steps = T // rows
    qblk = C_HQ // HG_DK

    def body(q_ref, k_ref, lf_ref, v_ref, st_ref, do_ref, dq_ref, dk_ref, dlf_ref, dv_ref, dst, dk_s, dv_s):
        @pl.when(pl.program_id(1) == 0)
        def _():
            dst[...] = jnp.zeros_like(dst)

        lower, upper = _tri(True), _tri(False)
        rid = lax.broadcasted_iota(jnp.int32, (C, 1), 0)
        for n in reversed(range(nb)):
            sl = slice(n * C, (n + 1) * C)
            q, k, v, do = q_ref[sl, :].astype(F32), k_ref[sl, :], v_ref[sl, :], do_ref[sl, :]
            b = jnp.dot(lower, lf_ref[sl, :], precision=_HI, preferred_element_type=F32)
            s_t = st_ref[0, n]
            d_new = dst[...]
            eb = jnp.exp(b)
            bl = b[C - 1:C, :]
            ebl = jnp.exp(bl)
            dec = jnp.exp(bl - b)
            qe = q * eb
            kd = k * dec
            do_b = do.astype(BF16)
            dqe = jnp.dot(do_b, s_t.astype(BF16), preferred_element_type=F32)
            dkd = jnp.dot(v.astype(BF16), d_new.astype(BF16), preferred_element_type=F32)
            dv = lax.dot_general(kd.astype(BF16), d_new.astype(BF16), _NT, preferred_element_type=F32)
            dbl = ebl * jnp.sum(d_new * s_t, axis=0, keepdims=True) + jnp.sum(dkd * kd, axis=0, keepdims=True)
            dst[...] = d_new * ebl + lax.dot_general(do_b, qe.astype(BF16), _TN, preferred_element_type=F32)
            dq_in, dk_in, dv_in = _hg_intra_bwd(q, k, v, b, do, dk_s, dv_s)
            dq = dqe * eb + dq_in
            dk = dkd * dec + dk_in
            dv = dv + dv_in
            db = q * dq - k * dk
            db = db + jnp.where(rid == C - 1, dbl, 0.0)
            dq_ref[sl, :] = dq
            dk_ref[sl, :] = dk
            dv_ref[sl, :] = dv
            dlf_ref[sl, :] = jnp.dot(upper, db, precision=_HI, preferred_element_type=F32)

    rev = lambda off: pl.BlockSpec((rows, HG_DK), lambda h, c: (n_steps - 1 - c, h + off))
    return pl.pallas_call(
        body, name=name, grid=(HG_HEADS, n_steps),
        in_specs=[rev(qblk), rev(0), rev(0), rev(0),
                  pl.BlockSpec((1, nb, HG_DV, HG_DK), lambda h, c: (h, n_steps - 1 - c, 0, 0)), rev(0)],
        out_specs=[rev(0)] * 4,
        out_shape=[jax.ShapeDtypeStruct((T, HG_HEADS * HG_DK), F32)] * 4,
        scratch_shapes=[pltpu.VMEM((HG_DV, HG_DK), F32), pltpu.VMEM((C, HG_DK), F32), pltpu.VMEM((C, HG_DV), F32)],
        compiler_params=_cparams(("parallel", "arbitrary")),
    )(u, kk, lf, vv, states, do)


def _rope_tables(T):
    half = ROPE // 2
    inv_freq = (ROPE_BASE ** (-np.arange(half, dtype=np.float32) / half)).astype(np.float32)
    row = lambda lo, hi, val: np.concatenate([np.zeros(lo, np.float32), np.asarray(val, np.float32) * np.ones(hi - lo, np.float32),
                                              np.zeros(HEAD_W - hi, np.float32)])[None, :]
    freq = row(NOPE, NOPE + half, inv_freq) + row(NOPE + half, NOPE + ROPE, inv_freq)
    pos = lax.broadcasted_iota(jnp.int32, (T, HEAD_W), 0).astype(F32) - float(PAD_FRONT)
    ang = pos * freq
    cos, sin = jnp.cos(ang), jnp.sin(ang)
    c = cos * row(NOPE, NOPE + ROPE, 1.0) + row(0, NOPE, 1.0)
    s1 = sin * row(NOPE, NOPE + half, -1.0)
    s2 = sin * row(NOPE + half, NOPE + ROPE, 1.0)
    return c, s1, s2


def _layer_fwd(x, w, tabs, T, l, late=None):
    c, s1, s2 = tabs
    n = lambda s: f"l{l}_{s}"
    sv = {"x": x}
    h, sv["h_t"] = _rowwise(_f_rms, T, [Row(x)], [(w["norm1_g"], D_MODEL)], [(D_MODEL, BF16)], n("norm1"), True)
    u = _mm(h, w["w_in"], out_dtype=BF16, name=n("in_proj"))
    sv.update(h=h, u=u)
    if late is not None:
        w.update(late(u))
    hglu = _rowwise(_f_glu, T, [Row(u, 512, C_CONV_A), Row(u, 512, C_CONV_G)], [], [(CONV_DIM, F32)], n("glu"))[0]
    cv = _conv_fwd(hglu, w["conv_w"], w["conv_b"], T, n("conv"))
    hc = _rowwise(_f_lnsilu, T, [Row(cv)], [(w["conv_ln_g"], CONV_DIM), (w["conv_ln_b"], CONV_DIM)],
                  [(CONV_DIM, BF16)], n("conv_ln"))[0]
    y_a = _mm(hc, w["w_conv_out"], out_dtype=BF16, name=n("conv_out"))
    sv.update(hglu=hglu, cv=cv, hc=hc, y_a=y_a)
    cqn = _rowwise(_f_rms, T, [Row(u, Q_RANK, C_CQ)], [(w["q_a_norm_g"], Q_RANK)], [(Q_RANK, BF16)], n("q_a_norm"))[0]
    ckvn = _rowwise(_f_rms, T, [Row(u, KV_RANK, C_CKV)], [(w["kv_a_norm_g"], KV_RANK)], [(KV_RANK, BF16)], n("kv_a_norm"))[0]
    q_raw = _mm(cqn, w["w_uq"], out_dtype=BF16, name=n("uq"))
    k_raw = _mm(ckvn, w["w_uk"], out_dtype=BF16, name=n("uk"))
    v = _mm(ckvn, w["w_uv"], out_dtype=BF16, name=n("uv"))
    tab_rows = [Row(c), Row(s1), Row(s2)]
    q = _rowwise(_f_qrope, T, [Row(q_raw, piece=HEAD_W)] + tab_rows, [(w["q_norm_g"], HEAD_W)],
                 [(HEADS * HEAD_W, BF16)], n("q_rope"))[0]
    k = _rowwise(_f_krope, T, [Row(k_raw, piece=HEAD_W), Row(u, HEAD_W, C_KR)] + tab_rows, [(w["k_norm_g"], HEAD_W)],
                 [(HEADS * HEAD_W, BF16)], n("k_rope"))[0]
    o, lse = _attn_fwd(q, k, v, T, n("attn"))
    y_b = _mm(o, w["w_attn_out"], out_dtype=BF16, name=n("attn_out"))
    sv.update(cqn=cqn, ckvn=ckvn, q_raw=q_raw, k_raw=k_raw, v=v, q=q, k=k, o=o, lse=lse, y_b=y_b)
    kk, lf, vv = _rowwise(_f_hgrn_prep, T, [Row(u, 512, C_HF), Row(u, 512, C_HI)], [(w["lb"], 512)],
                          [(512, F32)] * 3, n("hgrn_prep"))
    o_h, states = _hgrn_fwd(u, kk, lf, vv, T, n("hgrn"))
    oh = _rowwise(_f_hgrn_out, T, [Row(o_h, piece=HG_DV), Row(u, 512, C_HG, piece=HG_DV)], [(w["hgrn_norm_g"], HG_DV)],
                  [(512, BF16)], n("hgrn_out_norm"))[0]
    y_c = _mm(oh, w["w_hgrn_out"], out_dtype=BF16, name=n("hgrn_out"))
    sv.update(kk=kk, lf=lf, vv=vv, o_h=o_h, states=states, oh=oh, y_c=y_c)
    gate_rows = [Row(u, D_MODEL, C_GATE + g * D_MODEL) for g in range(3)]
    mix, sv["mix_t"] = _rowwise(_f_mix, T, gate_rows + [Row(y_a), Row(y_b), Row(y_c)], [], [(D_MODEL, BF16)], n("mix"), True)
    x1 = _mm(mix, w["w_out"], res=x, name=n("out_proj"))
    h2, sv["h2_t"] = _rowwise(_f_rms, T, [Row(x1)], [(w["norm2_g"], D_MODEL)], [(D_MODEL, BF16)], n("norm2"), True)
    f = _mm(h2, w["w_ff1"], out_dtype=BF16, name=n("ff1"))
    x2 = _mm(f, w["w_ff2"], res=x1, a_fn=_relu2, name=n("ff2"))
    sv.update(mix=mix, x1=x1, h2=h2, f=f)
    return x2, sv


def _layer_bwd(dx2, w, sv, tabs, T, l, mid=None, matrices=None):
    c, s1, s2 = tabs
    n = lambda s: f"l{l}_b_{s}"
    u = sv["u"]
    g = {}
    g["w_ff2"] = _mm(sv["f"], dx2, ta=True, a_fn=_relu2, out_dtype=BF16, name=n("dw_ff2"))
    df = _mm(dx2, w["w_ff2"], tb=True, out_dtype=BF16, name=n("d_f"),
             epi=(sv["f"], lambda d, fv: d * (2.0 * jnp.maximum(fv, 0.0))))
    g["w_ff1"] = _mm(sv["h2_t"], df, out_dtype=BF16, name=n("dw_ff1"))
    dh2 = _mm(df, w["w_ff1"], tb=True, name=n("d_h2"))
    (dx1,), (g["norm2_g"],) = _rowwise_bwd(_f_rms, T, [Row(sv["x1"])], [(w["norm2_g"], D_MODEL)], [Row(dh2)],
                                           {0: F32}, n("norm2"), add=(0, dx2))
    g["w_out"] = _mm(sv["mix_t"], dx1, out_dtype=BF16, name=n("dw_out"))
    w_out = w["w_out"] if mid is None else mid(g, w["w_out"])
    dmix = _mm(dx1, w_out, tb=True, out_dtype=BF16, name=n("d_mix"))
    gate_rows = [Row(u, D_MODEL, C_GATE + i * D_MODEL) for i in range(3)]
    (dg0, dg1, dg2, dy_a, dy_b, dy_c), _ = _rowwise_bwd(
        _f_mix, T, gate_rows + [Row(sv["y_a"]), Row(sv["y_b"]), Row(sv["y_c"])], [], [Row(dmix)],
        {0: BF16, 1: BF16, 2: BF16, 3: BF16, 4: BF16, 5: BF16}, n("mix"))
    g["w_hgrn_out"] = _mm(sv["oh"], dy_c, ta=True, out_dtype=BF16, name=n("dw_hgrn_out"))
    doh = _mm(dy_c, w["w_hgrn_out"], tb=True, out_dtype=BF16, name=n("d_oh"))
    (do_h, dhg), (g["hgrn_norm_g"],) = _rowwise_bwd(
        _f_hgrn_out, T, [Row(sv["o_h"], piece=HG_DV), Row(u, 512, C_HG, piece=HG_DV)], [(w["hgrn_norm_g"], HG_DV)],
        [Row(doh, piece=HG_DV)], {0: F32, 1: BF16}, n("hgrn_out_norm"))
    dhq, dkk, dlf, dvv = _hgrn_bwd(u, sv["kk"], sv["lf"], sv["vv"], sv["states"], do_h, T, n("hgrn"))
    (dhf, dhi), (g["lb"],) = _rowwise_bwd(
        _f_hgrn_prep, T, [Row(u, 512, C_HF), Row(u, 512, C_HI)], [(w["lb"], 512)],
        [Row(dkk), Row(dlf), Row(dvv)], {0: BF16, 1: BF16}, n("hgrn_prep"))
    g["w_attn_out"] = _mm(sv["o"], dy_b, ta=True, out_dtype=BF16, name=n("dw_attn_out"))
    do = _mm(dy_b, w["w_attn_out"], tb=True, out_dtype=BF16, name=n("d_o"))
    dq, dk, dv = _attn_bwd(sv["q"], sv["k"], sv["v"], sv["o"], sv["lse"], do, T, n("attn"))
    tab_rows = [Row(c), Row(s1), Row(s2)]
    (dq_raw,), (g["q_norm_g"],) = _rowwise_bwd(
        _f_qrope, T, [Row(sv["q_raw"], piece=HEAD_W)] + tab_rows, [(w["q_norm_g"], HEAD_W)],
        [Row(dq, piece=HEAD_W)], {0: BF16}, n("q_rope"))
    (dk_raw, dkr), (g["k_norm_g"],) = _rowwise_bwd(
        _f_krope, T, [Row(sv["k_raw"], piece=HEAD_W), Row(u, HEAD_W, C_KR)] + tab_rows, [(w["k_norm_g"], HEAD_W)],
        [Row(dk, piece=HEAD_W)], {0: BF16, 1: BF16}, n("k_rope"))
    g["w_uq"] = _mm(sv["cqn"], dq_raw, ta=True, out_dtype=BF16, name=n("dw_uq"))
    g["w_uk"] = _mm(sv["ckvn"], dk_raw, ta=True, out_dtype=BF16, name=n("dw_uk"))
    g["w_uv"] = _mm(sv["ckvn"], dv, ta=True, out_dtype=BF16, name=n("dw_uv"))
    dcqn = _mm(dq_raw, w["w_uq"], tb=True, out_dtype=BF16, name=n("d_cqn"))
    dckvn = _mm(dk_raw, w["w_uk"], tb=True, name=n("d_ckvn_k"))
    dckvn = _mm(dv, w["w_uv"], tb=True, res=dckvn, out_dtype=BF16, name=n("d_ckvn_v"))
    (dcq,), (g["q_a_norm_g"],) = _rowwise_bwd(_f_rms, T, [Row(u, Q_RANK, C_CQ)], [(w["q_a_norm_g"], Q_RANK)],
                                              [Row(dcqn)], {0: BF16}, n("q_a_norm"))
    (dckv,), (g["kv_a_norm_g"],) = _rowwise_bwd(_f_rms, T, [Row(u, KV_RANK, C_CKV)], [(w["kv_a_norm_g"], KV_RANK)],
                                                [Row(dckvn)], {0: BF16}, n("kv_a_norm"))
    g["w_conv_out"] = _mm(sv["hc"], dy_a, ta=True, out_dtype=BF16, name=n("dw_conv_out"))
    dhc = _mm(dy_a, w["w_conv_out"], tb=True, out_dtype=BF16, name=n("d_hc"))
    (dcv,), (g["conv_ln_g"], g["conv_ln_b"]) = _rowwise_bwd(
        _f_lnsilu, T, [Row(sv["cv"])], [(w["conv_ln_g"], CONV_DIM), (w["conv_ln_b"], CONV_DIM)], [Row(dhc)],
        {0: F32}, n("conv_ln"))
    dhglu, dconv_w, g["conv_b"] = _conv_bwd(sv["hglu"], w["conv_w"], dcv, T, n("conv"))
    g["conv_w"] = dconv_w[:CONV_K]
    (dua, dug), _ = _rowwise_bwd(_f_glu, T, [Row(u, 512, C_CONV_A), Row(u, 512, C_CONV_G)], [], [Row(dhglu)],
                                 {0: BF16, 1: BF16}, n("glu"))
    du = jnp.concatenate([dua, dug, dg0, dg1, dg2, dcq, dckv, dkr, dhq.astype(BF16), dhf, dhi, dhg], axis=1)
    small = ("w_uq", "w_uk", "w_uv", "w_attn_out", "w_hgrn_out", "w_conv_out")
    du, *done = lax.optimization_barrier((du, *[g[k] for k in small]))
    g.update(zip(small, done))
    g["w_in"] = _mm(sv["h_t"], du, out_dtype=BF16, name=n("dw_in"))
    norm_g = w["norm1_g"] if matrices is None else matrices(g, w["norm1_g"])
    du, norm_g = lax.optimization_barrier((du, norm_g))
    dh = _mm(du, w["w_in"], tb=True, name=n("d_h"))
    (dx,), (g["norm1_g"],) = _rowwise_bwd(_f_rms, T, [Row(sv["x"])], [(norm_g, D_MODEL)], [Row(dh)],
                                          {0: F32}, n("norm1"), add=(0, dx1))
    return dx, g


def _pad_w_in(w_in):
    z = lambda k: jnp.zeros((w_in.shape[0], k), w_in.dtype)
    return jnp.concatenate([w_in[:, :O_CQ], w_in[:, O_GATE:], w_in[:, O_CQ:O_KR], z(KR_LANE), w_in[:, O_KR:O_HQ],
                            z(HEAD_W - KR_LANE - ROPE), w_in[:, O_HQ:O_GATE]], axis=1)


def _unpad_w_in(g):
    return jnp.concatenate([g[:, :C_GATE], g[:, C_CQ:C_KR], g[:, C_KR + KR_LANE:C_KR + KR_LANE + ROPE],
                            g[:, C_HQ:], g[:, C_GATE:C_CQ]], axis=1)


_W_IN_RUNS = ((0, 0, O_CQ), (O_CQ, C_CQ, O_KR - O_CQ), (O_KR, C_KR + KR_LANE, ROPE), (O_HQ, C_HQ, O_GATE - O_HQ),
              (O_GATE, C_GATE, N_IN - O_GATE))


def _w_in_from_shards(g8):
    per = N_IN // N_DEV
    pieces, at = [], 0
    for o0, p0, n in sorted(_W_IN_RUNS, key=lambda r: r[1]):
        if p0 > at:
            pieces.append(jnp.zeros((g8.shape[1], p0 - at), g8.dtype))
        for j in range(o0 // per, (o0 + n - 1) // per + 1):
            lo, hi = max(o0, j * per), min(o0 + n, (j + 1) * per)
            pieces.append(g8[j][:, lo - j * per:hi - j * per])
        at = p0 + n
    if at < N_IN_P:
        pieces.append(jnp.zeros((g8.shape[1], N_IN_P - at), g8.dtype))
    return jnp.concatenate(pieces, axis=1)


def _w_in_grad_shards(g):
    per = N_IN // N_DEV
    shards = []
    for j in range(N_DEV):
        lo, hi = j * per, (j + 1) * per
        pieces = [g[:, p0 + max(lo, o0) - o0:p0 + min(hi, o0 + n) - o0]
                  for o0, p0, n in _W_IN_RUNS if max(lo, o0) < min(hi, o0 + n)]
        shards.append(jnp.concatenate(pieces, axis=1) if len(pieces) > 1 else pieces[0])
    return jnp.stack(shards)


def _pad_heads(wm, per_head, lo, hi):
    lead = wm.shape[:-1]
    wh = wm.reshape(lead + (HEADS, per_head))[..., lo:hi]
    pad = [(0, 0)] * len(lead) + [(0, 0), (0, HEAD_W - (hi - lo))]
    return jnp.pad(wh, pad).reshape(lead + (HEADS * HEAD_W,))


def _unpad_heads(gm, width):
    lead = gm.shape[:-1]
    return gm.reshape(lead + (HEADS, HEAD_W))[..., :width]


def _layer_weights(full, lb=None):
    w = {}
    for name in ("norm1_g", "conv_w", "conv_b", "conv_ln_g", "conv_ln_b", "q_a_norm_g", "kv_a_norm_g", "hgrn_norm_g",
                 "norm2_g", "w_conv_out", "w_hgrn_out", "w_out", "w_ff1", "w_ff2"):
        if name in full:
            w[name] = full[name]
    if lb is not None:
        w["lb"] = lb
    if "w_in_padded" in full:
        w["w_in"] = full["w_in_padded"]
    elif "w_in" in full:
        w["w_in"] = _pad_w_in(full["w_in"])
    if "w_uq" in full:
        w["w_uq"] = _pad_heads(full["w_uq"], QK_DIM, 0, QK_DIM)
    if "w_ukv" in full:
        w["w_uk"] = _pad_heads(full["w_ukv"], NOPE + V_DIM, 0, NOPE)
        w["w_uv"] = _pad_heads(full["w_ukv"], NOPE + V_DIM, NOPE, NOPE + V_DIM)
    for name in ("q_norm_g", "k_norm_g"):
        if name in full:
            w[name] = jnp.pad(full[name], (0, HEAD_W - QK_DIM))
    if "w_attn_out" in full:
        wa = full["w_attn_out"].reshape(HEADS, V_DIM, D_MODEL)
        w["w_attn_out"] = jnp.pad(wa, ((0, 0), (0, HEAD_W - V_DIM), (0, 0))).reshape(HEADS * HEAD_W, D_MODEL)
    return w


def _matrix_grads_to_original(g):
    o = {name: g[name] for name in ("w_conv_out", "w_hgrn_out", "w_out", "w_ff1", "w_ff2")}
    o["w_in"] = _unpad_w_in(g["w_in"])
    o["w_in_shards"] = _w_in_grad_shards(g["w_in"])
    o["w_uq"] = _unpad_heads(g["w_uq"], QK_DIM).reshape(Q_RANK, HEADS * QK_DIM)
    guk = _unpad_heads(g["w_uk"], NOPE)
    guv = _unpad_heads(g["w_uv"], V_DIM)
    o["w_ukv"] = jnp.concatenate([guk, guv], axis=-1).reshape(KV_RANK, HEADS * (NOPE + V_DIM))
    o["w_attn_out"] = g["w_attn_out"].reshape(HEADS, HEAD_W, D_MODEL)[:, :V_DIM].reshape(HEADS * V_DIM, D_MODEL)
    return o


def _vector_grads_to_original(g):
    o = {"conv_w": g["conv_w"]}
    for name in ("norm1_g", "conv_b", "conv_ln_g", "conv_ln_b", "q_a_norm_g", "kv_a_norm_g", "hgrn_norm_g", "norm2_g", "lb"):
        o[name] = g[name].reshape(-1)
    o["q_norm_g"] = g["q_norm_g"].reshape(-1)[:QK_DIM]
    o["k_norm_g"] = g["k_norm_g"].reshape(-1)[:QK_DIM]
    return o


def _lower_bounds(logits):
    p = jax.nn.softmax(logits.astype(F32), axis=0)
    return jnp.cumsum(p, axis=0) - p[0:1]


def _run_step(x, target, meta, lb_logits, layer_weights, layer_done, layer_mid=None, layer_matrices=None):
    seq = x.shape[0]
    T = ROW0 + seq
    assert T % 128 == 0
    tabs = _rope_tables(T)
    lbs, lb_vjp = jax.vjp(_lower_bounds, lb_logits)
    xp = jnp.concatenate([jnp.zeros((PAD_FRONT, D_MODEL), F32), meta.astype(F32), x], axis=0)
    tp = jnp.concatenate([jnp.zeros((ROW0, D_MODEL), F32), target], axis=0)
    ws, svs = [], []
    for l in range(DEPTH):
        full, xp = layer_weights(l, xp)
        late = full.pop("late", None)
        w = _layer_weights(full, lbs[l])
        xp, sv = _layer_fwd(xp, w, tabs, T, l, late)
        ws.append(w)
        svs.append(sv)
    dx, sq = _loss_head(xp, tp, T)
    loss = 0.5 * jnp.sum(sq) * (1.0 / D_MODEL)
    dlb = [None] * DEPTH
    for l in reversed(range(DEPTH)):
        mid = None if layer_mid is None else functools.partial(layer_mid, l)
        mats = {}

        def matrices(g, norm_g, l=l, mats=mats):
            mats.update(_matrix_grads_to_original(g))
            return norm_g if layer_matrices is None else layer_matrices(l, mats, norm_g)

        dx, g = _layer_bwd(dx, ws[l], svs[l], tabs, T, l, mid, matrices)
        g = {**_vector_grads_to_original(g), **mats}
        dlb[l] = g.pop("lb")
        dx = layer_done(l, g, dx, ws[l - 1] if l > 0 else None)
    return loss, dx[ROW0:], dx[PAD_FRONT:ROW0], lb_vjp(jnp.stack(dlb))[0]


def _local_step(x, target, full):
    per_layer = [None] * DEPTH

    def done(l, g, dx, below):
        per_layer[l] = g
        return dx

    loss, gx, gmeta, glb = _run_step(
        x, target, full["meta"], full["hgrn_lb_logits"],
        lambda l, xp: ({k: v[l] for k, v in full.items() if k != "meta"}, xp), done)
    grads = {k: jnp.stack([per_layer[l][k] for l in range(DEPTH)]) for k in per_layer[0]}
    grads["hgrn_lb_logits"] = glb
    grads["meta"] = gmeta
    return loss, gx, grads


def _mesh_pos():
    return lax.axis_index("x"), lax.axis_index("y"), lax.axis_index("c")


N_COPY = N_DEV - 1


def _all_gather(arrs, name):
    n = len(arrs)

    def body(*refs):
        x_refs, out_refs = refs[:n], refs[n:2 * n]
        send_sems, recv_sems, local_sems = refs[2 * n:]
        x, y, c = _mesh_pos()
        me, sibling = (x, y, c), (x, y, 1 - c)
        chips = [(1 - x, y), (x, 1 - y), (1 - x, 1 - y)]

        def slot(a, px, py, pc):
            return out_refs[a].at[4 * px + 2 * py + pc]

        def copy(a, k, block, to, own=False):
            return pltpu.make_async_remote_copy(
                src_ref=x_refs[a] if own else slot(a, *block), dst_ref=slot(a, *block),
                send_sem=send_sems.at[a * N_COPY + k], recv_sem=recv_sems.at[a * N_COPY + k],
                device_id=to, device_id_type=MESH)

        mine = [pltpu.make_async_copy(x_refs[a], slot(a, *me), local_sems.at[a]) for a in range(n)]
        for cp in mine:
            cp.start()
        first = []
        for a in range(n):
            first.append(copy(a, 0, me, sibling, own=True))
            first += [copy(a, 1 + j, me, (*chip, c), own=True) for j, chip in enumerate(chips)]
        for cp in first:
            cp.start()
        passed = []
        for j, chip in enumerate(chips):
            for a in range(n):
                copy(a, 1 + j, (*chip, c), me).wait_recv()
                cp = copy(a, 4 + j, (*chip, c), sibling)
                cp.start()
                passed.append(cp)
        for a in range(n):
            copy(a, 0, sibling, me).wait_recv()
            for j, chip in enumerate(chips):
                copy(a, 4 + j, (*chip, 1 - c), me).wait_recv()
        for cp in first + passed:
            cp.wait_send()
        for cp in mine:
            cp.wait()

    anyspec = pl.BlockSpec(memory_space=pl.ANY)
    return pl.pallas_call(
        body, name=name, out_shape=[jax.ShapeDtypeStruct((N_DEV,) + a.shape, a.dtype) for a in arrs],
        in_specs=[anyspec] * n, out_specs=[anyspec] * n,
        scratch_shapes=[pltpu.SemaphoreType.DMA((n * N_COPY,)), pltpu.SemaphoreType.DMA((n * N_COPY,)),
                        pltpu.SemaphoreType.DMA((n,))],
    )(*arrs)


def _exchange(arrs, name):
    n = len(arrs)

    def body(*refs):
        s_refs, r_refs = refs[:n], refs[n:2 * n]
        send_sems, recv_sems, local_sems = refs[2 * n:]
        x, y, c = _mesh_pos()
        me = 4 * x + 2 * y + c
        local = [pltpu.make_async_copy(s_refs[a].at[me], r_refs[a].at[me], local_sems.at[a]) for a in range(n)]
        for cp in local:
            cp.start()
        sends, recvs = [], []
        for rel in range(1, N_DEV):
            px = 1 - x if rel & 4 else x
            py = 1 - y if rel & 2 else y
            pc = 1 - c if rel & 1 else c
            p = 4 * px + 2 * py + pc
            for a in range(n):
                k = a * N_COPY + rel - 1
                sends.append(pltpu.make_async_remote_copy(
                    src_ref=s_refs[a].at[p], dst_ref=r_refs[a].at[me], send_sem=send_sems.at[k],
                    recv_sem=recv_sems.at[k], device_id=(px, py, pc), device_id_type=MESH))
                recvs.append(pltpu.make_async_remote_copy(
                    src_ref=s_refs[a].at[me], dst_ref=r_refs[a].at[p], send_sem=send_sems.at[k],
                    recv_sem=recv_sems.at[k], device_id=(px, py, pc), device_id_type=MESH))
        for cp in sends:
            cp.start()
        for cp in recvs:
            cp.wait_recv()
        for cp in sends:
            cp.wait_send()
        for cp in local:
            cp.wait()

    anyspec = pl.BlockSpec(memory_space=pl.ANY)
    return pl.pallas_call(
        body, name=name, out_shape=[jax.ShapeDtypeStruct(a.shape, a.dtype) for a in arrs],
        in_specs=[anyspec] * n, out_specs=[anyspec] * n,
        scratch_shapes=[pltpu.SemaphoreType.DMA((n * N_COPY,)), pltpu.SemaphoreType.DMA((n * N_COPY,)),
                        pltpu.SemaphoreType.DMA((n,))],
    )(*arrs)


_HBM = pl.BlockSpec(memory_space=pltpu.HBM)
_SEM = pl.BlockSpec(memory_space=pltpu.SEMAPHORE)
_EFFECT = pltpu.SideEffectType.DATAFLOW_SIDE_EFFECTING


def _peers(x, y, c):
    out = []
    for rel in range(1, N_DEV):
        px = 1 - x if rel & 4 else x
        py = 1 - y if rel & 2 else y
        pc = 1 - c if rel & 1 else c
        out.append((rel, (px, py, pc), 4 * px + 2 * py + pc))
    return out


ALL_RELS = tuple(range(1, N_DEV))
NEAR_RELS = (1, 2, 4, 6)


def _split_copies(src_refs, land_refs, send_sems, recv_sems, gather, rels=ALL_RELS):
    x, y, c = _mesh_pos()
    me = 4 * x + 2 * y + c
    out = []
    for a, (src, land) in enumerate(zip(src_refs, land_refs)):
        for rel, peer, p in _peers(x, y, c):
            if rel not in rels:
                continue
            k = a * N_COPY + rel - 1
            mk = lambda s, d: pltpu.make_async_remote_copy(
                src_ref=s, dst_ref=d, send_sem=send_sems.at[k], recv_sem=recv_sems.at[k],
                device_id=peer, device_id_type=MESH)
            mine = src if gather else src.at[p]
            out.append((mk(mine, land.at[me]), mk(mine, land.at[p])))
    return out


def _copy_start(srcs, gather, name, collective_id, rels=ALL_RELS):
    n = len(srcs)
    lands = [lax.empty(((N_DEV,) + s.shape) if gather else s.shape, s.dtype) for s in srcs]

    def body(*refs):
        src_refs, land_refs = refs[:n], refs[n:2 * n]
        send_sems, recv_sems = refs[2 * n], refs[2 * n + 1]
        token = refs[-1]
        x, y, c = _mesh_pos()
        barrier = pltpu.get_barrier_semaphore()
        for rel, peer, _ in _peers(x, y, c):
            if rel in rels:
                pl.semaphore_signal(barrier, inc=1, device_id=peer, device_id_type=MESH)
        pl.semaphore_wait(barrier, len(rels))
        for out_copy, _ in _split_copies(src_refs, land_refs, send_sems, recv_sems, gather, rels):
            out_copy.start()
        token[...] = jnp.zeros_like(token)

    hbm = lambda a: pltpu.HBM(a.shape, a.dtype)
    res = pl.pallas_call(
        body, name=name,
        out_shape=(pltpu.SemaphoreType.DMA((n * N_COPY,)), pltpu.SemaphoreType.DMA((n * N_COPY,)),
                   *[hbm(s) for s in srcs], *[hbm(z) for z in lands], jax.ShapeDtypeStruct((8, 128), F32)),
        in_specs=[_HBM] * (2 * n), out_specs=(_SEM, _SEM, *([_HBM] * (2 * n)), pl.BlockSpec(memory_space=pltpu.VMEM)),
        input_output_aliases={i: 2 + i for i in range(2 * n)},
        compiler_params=pltpu.CompilerParams(has_side_effects=_EFFECT, collective_id=collective_id),
    )(*[pltpu.with_memory_space_constraint(s, pltpu.HBM) for s in srcs],
      *[pltpu.with_memory_space_constraint(z, pltpu.HBM) for z in lands])
    return res[0], res[1], list(res[2:2 + n]), list(res[2 + n:2 + 2 * n]), res[-1]


def _after(a, token):
    return a + token[0, 0].astype(a.dtype)


def _forward_to_sibling(lands, name):
    n = len(lands)

    def body(*refs):
        land_refs = refs[n:2 * n]
        send_sems, recv_sems = refs[2 * n], refs[2 * n + 1]
        x, y, c = _mesh_pos()
        chips = [(1 - x, y), (x, 1 - y), (1 - x, 1 - y)]
        sends, recvs = [], []
        for a, land in enumerate(land_refs):
            for j, (px, py) in enumerate(chips):
                k = a * len(chips) + j
                mk = lambda slot: pltpu.make_async_remote_copy(
                    src_ref=land.at[slot], dst_ref=land.at[slot], send_sem=send_sems.at[k], recv_sem=recv_sems.at[k],
                    device_id=(x, y, 1 - c), device_id_type=MESH)
                sends.append(mk(4 * px + 2 * py + c))
                recvs.append(mk(4 * px + 2 * py + 1 - c))
        for cp in sends:
            cp.start()
        for cp in recvs:
            cp.wait_recv()
        for cp in sends:
            cp.wait_send()

    anyspec = pl.BlockSpec(memory_space=pl.ANY)
    return list(pl.pallas_call(
        body, name=name, out_shape=[jax.ShapeDtypeStruct(z.shape, z.dtype) for z in lands],
        in_specs=[anyspec] * n, out_specs=[anyspec] * n, input_output_aliases={i: i for i in range(n)},
        scratch_shapes=[pltpu.SemaphoreType.DMA((3 * n,)), pltpu.SemaphoreType.DMA((3 * n,))],
    )(*lands))


def _copy_wait(send_sems, recv_sems, srcs, lands, after, gather, name, rels=ALL_RELS):
    n = len(srcs)

    def body(*refs):
        src_refs, land_refs = refs[:n], refs[n:2 * n]
        s_sems, r_sems = refs[2 * n], refs[2 * n + 1]
        for out_copy, in_copy in _split_copies(src_refs, land_refs, s_sems, r_sems, gather, rels):
            out_copy.wait_send()
            in_copy.wait_recv()

    hbm = lambda a: pltpu.HBM(a.shape, a.dtype)
    res = pl.pallas_call(
        body, name=name, out_shape=(*[hbm(s) for s in srcs], *[hbm(z) for z in lands]),
        in_specs=[_HBM] * (2 * n) + [_SEM, _SEM, pl.BlockSpec(memory_space=pl.ANY)], out_specs=tuple([_HBM] * (2 * n)),
        input_output_aliases={i: i for i in range(2 * n)},
        compiler_params=pltpu.CompilerParams(has_side_effects=_EFFECT),
    )(*srcs, *lands, send_sems, recv_sems, after)
    return list(res[:n]), list(res[n:])


def _sum_parts(parts, name):
    P, R, W = parts.shape

    def body(p_ref, o_ref):
        g = p_ref[0].astype(F32)
        for i in range(1, P):
            g = g + p_ref[i].astype(F32)
        o_ref[...] = g

    return pl.pallas_call(body, name=name, out_shape=jax.ShapeDtypeStruct((R, W), F32))(parts)


def _adamw_body(p_ref, w_ref, m_ref, v_ref, g_ref, d_ref, nm_ref, nv_ref):
    g = p_ref[0].astype(F32)
    for i in range(1, p_ref.shape[0]):
        g = g + p_ref[i].astype(F32)
    _adamw_apply(g, w_ref, m_ref, v_ref, g_ref, d_ref, nm_ref, nv_ref)


def _adamw_apply(g, w_ref, m_ref, v_ref, g_ref, d_ref, nm_ref, nv_ref):
    m_new = ADAM_B1 * m_ref[...] + (1.0 - ADAM_B1) * g
    v_new = ADAM_B2 * v_ref[...] + (1.0 - ADAM_B2) * jnp.square(g)
    m_hat = m_new / (1.0 - ADAM_B1 ** ADAM_STEP)
    v_hat = v_new / (1.0 - ADAM_B2 ** ADAM_STEP)
    g_ref[...] = g
    d_ref[...] = -ADAM_LR * (m_hat / (jnp.sqrt(v_hat) + ADAM_EPS) + ADAM_WD * w_ref[...])
    nm_ref[...] = m_new
    nv_ref[...] = v_new


def _adamw(parts, w, m, v, name):
    P, R, W = parts.shape
    tr = _pick(R, (368, 192, 64, 16, 8))
    spec = pl.BlockSpec((tr, W), lambda i: (i, 0))
    return pl.pallas_call(
        functools.partial(_adamw_body), name=name, grid=(R // tr,),
        in_specs=[pl.BlockSpec((P, tr, W), lambda i: (0, i, 0)), spec, spec, spec], out_specs=[spec] * 4,
        out_shape=[jax.ShapeDtypeStruct((R, W), F32)] * 4,
        compiler_params=_cparams(("parallel",)),
    )(parts, w, m, v)


def _adamw_layers(parts, w, m, v, name):
    P, B, C_ = parts[0].shape
    tb = _pick(B, (256, 128))
    nb = B // tb

    def body(*refs):
        p_refs, rest = refs[:DEPTH], refs[DEPTH:]
        a = pl.program_id(0)
        for l in range(DEPTH):
            @pl.when(a == l)
            def _():
                _adamw_body(p_refs[l], *[r.at[0] for r in rest])

    spec = pl.BlockSpec((1, tb, C_), lambda a, i: (a, i, 0))

    def part_spec(l):
        return pl.BlockSpec((P, tb, C_), lambda a, i: (0, jnp.where(a == l, i, jnp.where(a < l, 0, nb - 1)), 0))

    return pl.pallas_call(
        body, name=name, grid=(DEPTH, nb),
        in_specs=[part_spec(l) for l in range(DEPTH)] + [spec, spec, spec], out_specs=[spec] * 4,
        out_shape=[jax.ShapeDtypeStruct((DEPTH, B, C_), F32)] * 4,
        compiler_params=_cparams(("arbitrary", "arbitrary")),
    )(*parts, w, m, v)


VEC_GROUPS = (("norm1_g", "norm2_g"), ("conv_b", "conv_ln_g", "conv_ln_b", "hgrn_lb_logits", "hgrn_norm_g"),
              ("q_a_norm_g",), ("kv_a_norm_g",), ("q_norm_g", "k_norm_g"))
SMALL_NAMES = tuple(n for grp in VEC_GROUPS for n in grp) + ("meta", "conv_w")


def _adamw_small(own, lands, wts, mom, var, name):
    n_in = len(own)

    def body(*refs):
        own_r, land_r = refs[:n_in], refs[n_in:2 * n_in]
        rest = iter(refs[2 * n_in:])
        wmv = {n: (next(rest), next(rest), next(rest)) for n in SMALL_NAMES}
        outs = {n: (next(rest), next(rest), next(rest), next(rest)) for n in SMALL_NAMES}
        loss_ref = next(rest)
        x, y, c = _mesh_pos()
        me = 4 * x + 2 * y + c

        def total(k):
            acc = None
            for s in range(N_DEV):
                v = jnp.where(me == s, own_r[k][...], land_r[k][s])
                acc = v if acc is None else acc + v
            return acc

        for k, grp in enumerate(VEC_GROUPS):
            tot = total(k)
            for j, n in enumerate(grp):
                _adamw_apply(tot[DEPTH * j:DEPTH * (j + 1)], *wmv[n], *outs[n])
        loss_ref[...] = total(len(VEC_GROUPS))
        _adamw_apply(total(n_in - 2), *wmv["meta"], *outs["meta"])
        _adamw_apply(total(n_in - 1), *wmv["conv_w"], *outs["conv_w"])

    args = list(own) + list(lands) + [d[n] for n in SMALL_NAMES for d in (wts, mom, var)]
    out_shape = [jax.ShapeDtypeStruct(wts[n].shape, F32) for n in SMALL_NAMES for _ in range(4)]
    res = pl.pallas_call(body, name=name, out_shape=out_shape + [jax.ShapeDtypeStruct((1, 128), F32)])(*args)
    out = {}
    for i, n in enumerate(SMALL_NAMES):
        for j, kind in enumerate(("grad_", "delta_", "new_m_", "new_v_")):
            out[kind + n] = res[4 * i + j]
    return out, res[-1]


PACK_W = 1024
BIG = (("w_in", (DEPTH, D_MODEL, N_IN // N_DEV), 2), ("w_conv_out", (DEPTH, CONV_DIM, D_MODEL // N_DEV), 2),
       ("w_uq", (DEPTH, Q_RANK, HEADS * QK_DIM // N_DEV), 2), ("w_ukv", (DEPTH, KV_RANK, HEADS * (NOPE + V_DIM) // N_DEV), 2),
       ("w_attn_out", (DEPTH, HEADS * V_DIM, D_MODEL // N_DEV), 2), ("w_hgrn_out", (DEPTH, 512, D_MODEL // N_DEV), 2),
       ("w_out", (DEPTH, D_MODEL // N_DEV, D_MODEL), 1), ("w_ff1", (DEPTH, D_MODEL, D_FF // N_DEV), 2),
       ("w_ff2", (DEPTH, D_FF // N_DEV, D_MODEL), 1))
SMALL_SHARDED = (("meta", (N_META, D_MODEL // N_DEV), 1), ("conv_w", (DEPTH, CONV_K, CONV_DIM // N_DEV), 2))
REPLICATED = (("norm1_g", (DEPTH, D_MODEL)), ("conv_b", (DEPTH, CONV_DIM)), ("conv_ln_g", (DEPTH, CONV_DIM)),
              ("conv_ln_b", (DEPTH, CONV_DIM)), ("q_a_norm_g", (DEPTH, Q_RANK)), ("kv_a_norm_g", (DEPTH, KV_RANK)),
              ("q_norm_g", (DEPTH, QK_DIM)), ("k_norm_g", (DEPTH, QK_DIM)), ("hgrn_lb_logits", (DEPTH, 512)),
              ("hgrn_norm_g", (DEPTH, 512)), ("norm2_g", (DEPTH, D_MODEL)))
WEIGHT_ORDER = ("meta", "norm1_g", "w_in", "conv_w", "conv_b", "conv_ln_g", "conv_ln_b", "w_conv_out", "q_a_norm_g", "w_uq",
                "kv_a_norm_g", "w_ukv", "q_norm_g", "k_norm_g", "w_attn_out", "hgrn_lb_logits", "hgrn_norm_g", "w_hgrn_out",
                "w_out", "norm2_g", "w_ff1", "w_ff2")


def _rows_for(n_elems, mult):
    rows = -(-n_elems // PACK_W)
    return -(-rows // mult) * mult


def _pack(arrays, dtype, mult, lead=()):
    nl = len(lead)
    flat = jnp.concatenate([a.reshape(lead + (-1,)).astype(dtype) for a in arrays], axis=nl)
    rows = _rows_for(flat.shape[nl], mult)
    flat = jnp.pad(flat, [(0, 0)] * nl + [(0, rows * PACK_W - flat.shape[nl])])
    return flat.reshape(lead + (rows, PACK_W))


def _unpack(pack, shapes, lead=()):
    nl = len(lead)
    flat = pack.reshape(lead + (-1,))
    out, off = [], 0
    for shp in shapes:
        n = int(np.prod(shp))
        out.append(lax.slice_in_dim(flat, off, off + n, axis=nl).reshape(lead + tuple(shp)))
        off += n
    return out


def _join_shards(g, axis):
    g = jnp.moveaxis(g, 0, axis)
    shp = g.shape
    return g.reshape(shp[:axis] + (shp[axis] * shp[axis + 1],) + shp[axis + 2:])


def _cut_shards(a, axis):
    shp = a.shape
    a = a.reshape(shp[:axis] + (N_DEV, shp[axis] // N_DEV) + shp[axis + 1:])
    return jnp.moveaxis(a, axis, 0)


def kernel(x, meta, norm1_g, w_in, conv_w, conv_b, conv_ln_g, conv_ln_b, w_conv_out, q_a_norm_g, w_uq, kv_a_norm_g, w_ukv, q_norm_g, k_norm_g, w_attn_out, hgrn_lb_logits, hgrn_norm_g, w_hgrn_out, w_out, norm2_g, w_ff1, w_ff2, loss_target, m_meta, m_norm1_g, m_w_in, m_conv_w, m_conv_b, m_conv_ln_g, m_conv_ln_b, m_w_conv_out, m_q_a_norm_g, m_w_uq, m_kv_a_norm_g, m_w_ukv, m_q_norm_g, m_k_norm_g, m_w_attn_out, m_hgrn_lb_logits, m_hgrn_norm_g, m_w_hgrn_out, m_w_out, m_norm2_g, m_w_ff1, m_w_ff2, v_meta, v_norm1_g, v_w_in, v_conv_w, v_conv_b, v_conv_ln_g, v_conv_ln_b, v_w_conv_out, v_q_a_norm_g, v_w_uq, v_kv_a_norm_g, v_w_ukv, v_q_norm_g, v_k_norm_g, v_w_attn_out, v_hgrn_lb_logits, v_hgrn_norm_g, v_w_hgrn_out, v_w_out, v_norm2_g, v_w_ff1, v_w_ff2):
    args = dict(locals())
    wts = {n: args[n] for n in WEIGHT_ORDER}
    mom = {n: args["m_" + n] for n in WEIGHT_ORDER}
    var = {n: args["v_" + n] for n in WEIGHT_ORDER}
    xi, yi, ci = _mesh_pos()
    me = 4 * xi + 2 * yi + ci

    shard = lambda l: [wts[n][l].astype(BF16) for n, _, _ in BIG]
    assert BIG[0][0] == "w_in"
    gathered = _all_gather(shard(0)[:1] + [_pack([wts[n] for n, _, _ in SMALL_SHARDED], F32, 8)], "gather_layer0")
    small = dict(zip([n for n, _, _ in SMALL_SHARDED],
                     [_join_shards(g, axis) for (_, _, axis), g in
                      zip(SMALL_SHARDED, _unpack(gathered[-1], [s for _, s, _ in SMALL_SHARDED], (N_DEV,)))]))
    rest0 = _copy_start(shard(0)[1:], True, "gather_rest0_start", 11, NEAR_RELS)
    pending = []

    def joined(mats, names_axes):
        full = {}
        for (n, _, axis), g in zip(names_axes, mats):
            if n == "w_in":
                full["w_in_padded"] = _w_in_from_shards(g)
            else:
                full[n] = _join_shards(g, axis - 1)
        return full

    def rest_of_layer0(u):
        own, lands = _copy_wait(rest0[0], rest0[1], rest0[2], rest0[3], u, True, "gather_rest0_wait", NEAR_RELS)
        lands = _forward_to_sibling(lands, "gather_rest0_forward")
        full = joined([lax.dynamic_update_index_in_dim(z, s, me, 0) for z, s in zip(lands, own)], BIG[1:])
        pending.append(_copy_start(shard(1), True, "gather_layer1_start", 5))
        full["w_conv_out"] = _after(full["w_conv_out"], pending[0][4])
        return _layer_weights(full)

    def layer_weights(l, xp):
        full = {n: wts[n][l] for n, _ in REPLICATED}
        full["conv_w"] = small["conv_w"][l]
        if l == 0:
            full.update(joined(gathered[:1], BIG[:1]))
            full["norm1_g"] = _after(full["norm1_g"], rest0[4])
            full["late"] = rest_of_layer0
        else:
            s_sems, r_sems, sent, lands, _ = pending[0]
            own, lands = _copy_wait(s_sems, r_sems, sent, lands, xp, True, "gather_layer1_wait")
            full.update(joined([lax.dynamic_update_index_in_dim(z, s, me, 0) for z, s in zip(lands, own)], BIG))
        return full, xp

    big_names = [n for n, _, _ in BIG]
    early = [n for n in big_names if n in ("w_out", "w_ff1", "w_ff2")]
    late = [n for n in big_names if n not in early]
    cut = lambda g, names: [(g[n + "_shards"] if n + "_shards" in g else _cut_shards(g[n], axis - 1)).astype(BF16)
                            for n, _, axis in BIG if n in names]
    layer_grads = [None] * DEPTH
    flight = {}

    def layer_mid(l, g, w_out):
        if l == 0:
            flight["l0_early"] = _copy_start(cut(g, early), False, "scatter_layer0_early_start", 7)
            w_out = _after(w_out, flight["l0_early"][4])
        return w_out

    def layer_done(l, g, dx, below):
        layer_grads[l] = g
        if l == 1:
            flight["l1"] = _copy_start(cut(g, big_names), False, "scatter_l1_start", 6)
            below["norm2_g"] = _after(below["norm2_g"], flight["l1"][4])
        return dx

    def layer_matrices(l, mats, norm_g):
        if l == 0:
            flight["l0_late"] = _copy_start(cut(mats, late), False, "scatter_l0_late_start", 8)
            norm_g = _after(norm_g, flight["l0_late"][4])
        return norm_g

    loss, grad_x, g_meta, g_lb = _run_step(x[0], loss_target[0], small["meta"], wts["hgrn_lb_logits"],
                                           layer_weights, layer_done, layer_mid, layer_matrices)

    grads = {k: jnp.stack([layer_grads[l][k] for l in range(DEPTH)]) for k in layer_grads[0]
             if k not in big_names and not k.endswith("_shards")}
    grads["hgrn_lb_logits"] = g_lb
    own = [jnp.concatenate([grads[n] for n in grp], axis=0) for grp in VEC_GROUPS]
    own.append(jnp.broadcast_to(loss.reshape(1, 1), (1, 128)))
    flight["small"] = _copy_start(own, True, "gather_small_grads_start", 9)
    cuts = [_cut_shards(g_meta, 1), _cut_shards(grads["conv_w"], 2)]
    flight["small_x"] = _copy_start(cuts, False, "scatter_small_grads_start", 10)
    started = flight["small_x"][4]

    def arrive(key, names, after):
        s_sems, r_sems, sent, lands, _ = flight[key]
        sent, lands = _copy_wait(s_sems, r_sems, sent, lands, after, False, f"scatter_{key}_wait")
        return {n: lax.dynamic_update_index_in_dim(z, lax.dynamic_index_in_dim(s, me, 0, keepdims=False), me, 0)
                for n, z, s in zip(names, lands, sent)}

    out = {}

    def update(names, recv0, recv1):
        for n in names:
            res4 = _adamw_layers([recv0[n], recv1[n]], wts[n], mom[n], var[n], "adamw_" + n)
            for kind, a in zip(("grad_", "delta_", "new_m_", "new_v_"), res4):
                out[kind + n] = a

    recv1 = arrive("l1", big_names, started)
    recv0 = arrive("l0_early", early, started)
    update(early, recv0, recv1)

    s_sems, r_sems, sent, lands, _ = flight["small"]
    updated = lax.optimization_barrier(tuple(out["grad_" + n] for n in early))
    own, lands = _copy_wait(s_sems, r_sems, sent, lands, updated[0], True, "gather_small_grads_wait")
    s_sems, r_sems, sent, lands_x, _ = flight["small_x"]
    sent, lands_x = _copy_wait(s_sems, r_sems, sent, lands_x, updated[0], False, "scatter_small_grads_wait")
    own += [lax.dynamic_index_in_dim(s, me, 0, keepdims=False) for s in sent]
    small_out, loss = _adamw_small(own, lands + lands_x, wts, mom, var, "adamw_small")
    out.update(small_out)
    loss = loss[0, 0]

    recv0 = arrive("l0_late", late, small_out["grad_norm1_g"])
    update(late, recv0, recv1)

    res = [loss, grad_x[None]]
    for kind in ("grad_", "delta_", "new_m_", "new_v_"):
        res += [out[kind + n] for n in WEIGHT_ORDER]
    return tuple(res)
```

```python
import functools

import numpy as np
import jax
import jax.numpy as jnp
from jax import lax
from jax.experimental import pallas as pl
from jax.experimental.pallas import tpu as pltpu

F32 = jnp.float32
BF16 = jnp.bfloat16

D_MODEL = 1024
DEPTH = 2
N_META = 16
PAD_FRONT = 112
ROW0 = PAD_FRONT + N_META
EPS = 1e-6
GATE_CLAMP = 1.0 - 1e-6
CONV_DIM = 512
CONV_K = 31
HEADS = 8
Q_RANK = 256
KV_RANK = 128
NOPE = 64
ROPE = 32
V_DIM = 64
QK_DIM = NOPE + ROPE
HEAD_W = 128
ROPE_BASE = 10000.0
HG_HEADS = 4
HG_DK = 128
HG_DV = 128
HG_CHUNK = 64
D_FF = 4096
N_IN = 6560
C_CONV_A, C_CONV_G, C_GATE, C_CQ, C_CKV, C_KR, C_HQ, C_HF, C_HI, C_HG = (
    0, 512, 1024, 4096, 4352, 4480, 4608, 5120, 5632, 6144)
N_IN_P = 6656
O_CQ, O_KR, O_HQ, O_GATE = 1024, 1408, 1440, 3488
KR_LANE = NOPE

ADAM_LR = 0.001
ADAM_B1 = 0.9
ADAM_B2 = 0.999
ADAM_EPS = 1e-08
ADAM_WD = 0.01
ADAM_STEP = 10

N_DEV = 8
VMEM_LIMIT = 56 * 1024 * 1024
MESH = pl.DeviceIdType.MESH


def _pick(n, cands):
    for c in cands:
        if n % c == 0:
            return c
    raise ValueError(f"no tile for {n}")


def _cparams(sem, **kw):
    return pltpu.CompilerParams(dimension_semantics=sem, vmem_limit_bytes=VMEM_LIMIT, **kw)


def _relu2(v):
    return jnp.square(jnp.maximum(v, 0.0))


def _mm(a, b, *, ta=False, tb=False, out_dtype=F32, res=None, a_fn=None, epi=None, name):
    M, K = (a.shape[1], a.shape[0]) if ta else a.shape
    N = b.shape[0] if tb else b.shape[1]
    assert (b.shape[1] if tb else b.shape[0]) == K, (a.shape, b.shape, ta, tb)
    tm = _pick(M, (1056, 1024, 512, 384, 256, 128, 96))
    tn = _pick(N, (1664, 1024, 512, 384, 256, 128))
    tk = _pick(K, (1664, 1056, 1024, 512, 384, 256, 128, 96) if ta else (1664, 1408, 1024, 512, 384, 256, 128))
    nk = K // tk
    dims = (((0 if ta else 1,), (1 if tb else 0,)), ((), ()))
    extras = ([res] if res is not None else []) + ([epi[0]] if epi is not None else [])

    def body(*refs):
        a_ref, b_ref = refs[0], refs[1]
        r_ref = refs[2] if res is not None else None
        e_ref = refs[2 + (res is not None)] if epi is not None else None
        o_ref = refs[2 + len(extras)]
        acc = refs[-1] if nk > 1 else None
        k = pl.program_id(2)
        av = a_ref[...]
        if a_fn is not None:
            av = a_fn(av.astype(F32))
        p = lax.dot_general(av.astype(BF16), b_ref[...].astype(BF16), dims, preferred_element_type=F32)

        def finish(total):
            if e_ref is not None:
                total = epi[1](total, e_ref[...].astype(F32))
            if r_ref is not None:
                total = total + r_ref[...].astype(F32)
            o_ref[...] = total.astype(o_ref.dtype)

        if nk == 1:
            finish(p)
        else:
            @pl.when(k == 0)
            def _():
                acc[...] = p

            @pl.when(k > 0)
            def _():
                acc[...] += p

            @pl.when(k == nk - 1)
            def _():
                finish(acc[...])

    a_spec = pl.BlockSpec((tk, tm), lambda i, j, k: (k, i)) if ta else pl.BlockSpec((tm, tk), lambda i, j, k: (i, k))
    b_spec = pl.BlockSpec((tn, tk), lambda i, j, k: (j, k)) if tb else pl.BlockSpec((tk, tn), lambda i, j, k: (k, j))
    o_spec = pl.BlockSpec((tm, tn), lambda i, j, k: (i, j))
    in_specs = [a_spec, b_spec] + [o_spec] * len(extras)
    args = (a, b) + tuple(extras)
    return pl.pallas_call(
        body, name=name, grid=(M // tm, N // tn, nk), in_specs=in_specs, out_specs=o_spec,
        out_shape=jax.ShapeDtypeStruct((M, N), out_dtype),
        scratch_shapes=[pltpu.VMEM((tm, tn), F32)] if nk > 1 else [],
        compiler_params=_cparams(("parallel", "parallel", "arbitrary")),
    )(*args)


class Row:
    def __init__(self, arr, width=None, col=0, piece=None):
        self.arr = arr
        self.width = arr.shape[1] if width is None else width
        assert col % self.width == 0
        self.blk = col // self.width
        self.piece = self.width if piece is None else piece

    def spec(self, tm):
        blk = self.blk
        return pl.BlockSpec((tm, self.width), lambda i: (i, blk))


def _split(v, piece):
    w = v.shape[-1]
    if piece == w:
        return v
    return [v[:, j * piece:(j + 1) * piece] for j in range(w // piece)]


def _store(ref, val, dtype=None):
    if isinstance(val, (list, tuple)):
        piece = val[0].shape[-1]
        for j, p in enumerate(val):
            ref[:, j * piece:(j + 1) * piece] = p.astype(ref.dtype)
    else:
        ref[...] = val.astype(ref.dtype)


ROW_CHUNK = 32


def _row_tile(T):
    return _pick(T, (384, 352, 192, 128))


def _param2d(p):
    return p.reshape(1, -1).astype(F32)


def _rowwise(fn, T, rows, params, outs, name, transposed=False):
    tm = _row_tile(T)
    nr, npar = len(rows), len(params)
    par = [(_param2d(p), piece) for p, piece in params]

    def body(*refs):
        pv = [_split(refs[nr + n][...], par[n][1]) for n in range(npar)]

        def chunk(t, carry):
            r0 = pl.multiple_of(t * ROW_CHUNK, ROW_CHUNK)
            rid = pl.program_id(0) * tm + r0 + lax.broadcasted_iota(jnp.int32, (ROW_CHUNK, 1), 0)
            rv = [_split(refs[n][pl.ds(r0, ROW_CHUNK), :].astype(F32), rows[n].piece) for n in range(nr)]
            for n, val in enumerate(fn(rid, rv, pv)):
                ref = refs[nr + npar + n]
                pieces = val if isinstance(val, (list, tuple)) else [val]
                width = pieces[0].shape[-1]
                for j, p in enumerate(pieces):
                    ref[pl.ds(r0, ROW_CHUNK), j * width:(j + 1) * width] = p.astype(ref.dtype)
            return carry

        lax.fori_loop(0, tm // ROW_CHUNK, chunk, 0)
        if transposed:
            refs[-1][...] = refs[nr + npar][...].astype(F32).T.astype(refs[-1].dtype)

    out_specs = [pl.BlockSpec((tm, w), lambda i: (i, 0)) for w, _ in outs]
    out_shape = [jax.ShapeDtypeStruct((T, w), dt) for w, dt in outs]
    if transposed:
        out_specs.append(pl.BlockSpec((outs[0][0], tm), lambda i: (0, i)))
        out_shape.append(jax.ShapeDtypeStruct((outs[0][0], T), outs[0][1]))
    return pl.pallas_call(
        body, name=name, grid=(T // tm,),
        in_specs=[r.spec(tm) for r in rows] + [pl.BlockSpec(p.shape, lambda i: (0, 0)) for p, _ in par],
        out_specs=out_specs, out_shape=out_shape,
        compiler_params=_cparams(("parallel",)),
    )(*[r.arr for r in rows], *[p for p, _ in par])


def _rowwise_bwd(fn, T, rows, params, cts, drow, name, add=None):
    tm = _row_tile(T)
    nr, npar, nct = len(rows), len(params), len(cts)
    par = [(_param2d(p), piece) for p, piece in params]
    didx = sorted(drow)
    has_add = add is not None

    def body(*refs):
        i = pl.program_id(0)
        rid = i * tm + lax.broadcasted_iota(jnp.int32, (tm, 1), 0)
        rv = [_split(refs[n][...].astype(F32), rows[n].piece) for n in range(nr)]
        pv = [_split(refs[nr + n][...], par[n][1]) for n in range(npar)]
        cv = [_split(refs[nr + npar + n][...].astype(F32), cts[n].piece) for n in range(nct)]
        base = nr + npar + nct + (1 if has_add else 0)
        d_refs = refs[base:base + len(didx)]
        p_refs = refs[base + len(didx):]

        def g(dvals, pvals):
            full = list(rv)
            for n, v in zip(didx, dvals):
                full[n] = v
            return fn(rid, full, pvals)

        _, vjp = jax.vjp(g, [rv[n] for n in didx], pv)
        d_rows, d_pars = vjp(cv)
        for slot, n in enumerate(didx):
            val = d_rows[slot]
            if has_add and add[0] == n:
                assert not isinstance(val, (list, tuple))
                val = val + refs[nr + npar + nct][...].astype(F32)
            _store(d_refs[slot], val)

        @pl.when(i == 0)
        def _():
            for r in p_refs:
                r[...] = jnp.zeros_like(r)

        for r, val in zip(p_refs, d_pars):
            if isinstance(val, (list, tuple)):
                piece = val[0].shape[-1]
                for j, p in enumerate(val):
                    r[:, j * piece:(j + 1) * piece] += p
            else:
                r[...] += val

    in_specs = ([r.spec(tm) for r in rows] + [pl.BlockSpec(p.shape, lambda i: (0, 0)) for p, _ in par]
                + [c.spec(tm) for c in cts])
    args = [r.arr for r in rows] + [p for p, _ in par] + [c.arr for c in cts]
    if has_add:
        in_specs.append(pl.BlockSpec((tm, rows[add[0]].width), lambda i: (i, 0)))
        args.append(add[1])
    out_specs = ([pl.BlockSpec((tm, rows[n].width), lambda i: (i, 0)) for n in didx]
                 + [pl.BlockSpec(p.shape, lambda i: (0, 0)) for p, _ in par])
    out_shape = ([jax.ShapeDtypeStruct((T, rows[n].width), drow[n]) for n in didx]
                 + [jax.ShapeDtypeStruct(p.shape, F32) for p, _ in par])
    res = pl.pallas_call(
        body, name=name, grid=(T // tm,), in_specs=in_specs, out_specs=out_specs, out_shape=out_shape,
        compiler_params=_cparams(("arbitrary",)),
    )(*args)
    return list(res[:len(didx)]), list(res[len(didx):])


def _f_rms(rid, rv, pv):
    x, g = rv[0], pv[0]
    return [x * lax.rsqrt(jnp.mean(x * x, axis=-1, keepdims=True) + EPS) * g]


def _f_glu(rid, rv, pv):
    a, gt = rv
    return [a * jax.nn.sigmoid(gt) * (rid >= PAD_FRONT).astype(F32)]


def _f_lnsilu(rid, rv, pv):
    x = rv[0]
    g, b = pv
    mu = jnp.mean(x, axis=-1, keepdims=True)
    xc = x - mu
    y = xc * lax.rsqrt(jnp.mean(xc * xc, axis=-1, keepdims=True) + EPS) * g + b
    return [y * jax.nn.sigmoid(y)]


@functools.partial(jax.custom_vjp, nondiff_argnums=(1,))
def _lane_roll(x, shift):
    return pltpu.roll(x, shift, 1)


def _lane_roll_fwd(x, shift):
    return pltpu.roll(x, shift, 1), None


def _lane_roll_bwd(shift, _, g):
    return (pltpu.roll(g, (HEAD_W - shift) % HEAD_W, 1),)


_lane_roll.defvjp(_lane_roll_fwd, _lane_roll_bwd)


def _head_norm_rope(xh, g, c, s1, s2):
    y = xh * lax.rsqrt(jnp.sum(xh * xh, axis=-1, keepdims=True) * (1.0 / QK_DIM) + EPS) * g
    half = ROPE // 2
    return y * c + _lane_roll(y, HEAD_W - half) * s1 + _lane_roll(y, half) * s2


def _f_qrope(rid, rv, pv):
    q, c, s1, s2 = rv
    return [[_head_norm_rope(qh, pv[0], c, s1, s2) * ATT_SCALE for qh in q]]


def _f_krope(rid, rv, pv):
    k, kr, c, s1, s2 = rv
    return [[_head_norm_rope(kh + kr, pv[0], c, s1, s2) for kh in k]]


def _f_hgrn_prep(rid, rv, pv):
    hf, hi = rv
    m = (rid >= PAD_FRONT).astype(F32)
    kk = (1.0 - pv[0]) * jax.nn.sigmoid(-hf) * m
    lf = jnp.log1p(-jnp.minimum(kk, GATE_CLAMP))
    vv = hi * jax.nn.sigmoid(hi) * m
    return [kk, lf, vv]


def _f_hgrn_out(rid, rv, pv):
    o, hg = rv
    ng = pv[0]
    out = []
    for oh, gh, nh in zip(o, hg, ng):
        y = oh * lax.rsqrt(jnp.mean(oh * oh, axis=-1, keepdims=True) + EPS) * nh
        out.append(y * (gh * jax.nn.sigmoid(gh)))
    return [out]


def _f_mix(rid, rv, pv):
    g0, g1, g2, ya, yb, yc = rv
    return [jax.nn.sigmoid(g0) * ya + jax.nn.sigmoid(g1) * yb + jax.nn.sigmoid(g2) * yc]


def _f_relu2(rid, rv, pv):
    return [jnp.square(jax.nn.relu(rv[0]))]


def _loss_head(x2, tgt, T):
    tm = _row_tile(T)

    def body(x_ref, t_ref, dx_ref, l_ref):
        i = pl.program_id(0)
        rid = i * tm + lax.broadcasted_iota(jnp.int32, (tm, 1), 0)
        diff = (x_ref[...] - t_ref[...]) * (rid >= ROW0).astype(F32)
        dx_ref[...] = diff * (1.0 / D_MODEL)

        @pl.when(i == 0)
        def _():
            l_ref[...] = jnp.zeros_like(l_ref)

        l_ref[...] += jnp.sum(diff * diff, axis=0, keepdims=True)

    spec = pl.BlockSpec((tm, D_MODEL), lambda i: (i, 0))
    return pl.pallas_call(
        body, name="loss_head", grid=(T // tm,), in_specs=[spec, spec],
        out_specs=[spec, pl.BlockSpec((1, D_MODEL), lambda i: (0, 0))],
        out_shape=[jax.ShapeDtypeStruct((T, D_MODEL), F32), jax.ShapeDtypeStruct((1, D_MODEL), F32)],
        compiler_params=_cparams(("arbitrary",)),
    )(x2, tgt)


HALO = 32


CONV_ROWS = 64


def _conv_lanes():
    return [slice(c, c + 128) for c in range(0, CONV_DIM, 128)]


def _conv_tile(T):
    return _pick(T, (384, 128))


def _conv_fwd(h, w, b, T, name):
    tr = _conv_tile(T)
    ratio = tr // HALO
    wp = jnp.zeros((HALO, CONV_DIM), F32).at[:CONV_K].set(w)

    def body(m_ref, h_ref, w_ref, b_ref, o_ref, win):
        i = pl.program_id(0)
        win[0:HALO, :] = h_ref[...] * (i > 0).astype(F32)
        win[HALO:, :] = m_ref[...]
        for cs in _conv_lanes():
            wv, bv = w_ref[:, cs], b_ref[:, cs]
            for r0 in range(0, tr, CONV_ROWS):
                acc = jnp.broadcast_to(bv, (CONV_ROWS, 128))
                for k in range(CONV_K):
                    acc = acc + wv[k:k + 1] * win[pl.ds(HALO - (CONV_K - 1) + k + r0, CONV_ROWS), cs]
                o_ref[r0:r0 + CONV_ROWS, cs] = acc

    return pl.pallas_call(
        body, name=name, grid=(T // tr,),
        in_specs=[pl.BlockSpec((tr, CONV_DIM), lambda i: (i, 0)),
                  pl.BlockSpec((HALO, CONV_DIM), lambda i: (jnp.maximum(i * ratio - 1, 0), 0)),
                  pl.BlockSpec((HALO, CONV_DIM), lambda i: (0, 0)),
                  pl.BlockSpec((1, CONV_DIM), lambda i: (0, 0))],
        out_specs=pl.BlockSpec((tr, CONV_DIM), lambda i: (i, 0)),
        out_shape=jax.ShapeDtypeStruct((T, CONV_DIM), F32),
        scratch_shapes=[pltpu.VMEM((tr + HALO, CONV_DIM), F32)],
        compiler_params=_cparams(("parallel",)),
    )(h, h, wp, _param2d(b))


def _conv_bwd(h, w, dy, T, name):
    tr = _conv_tile(T)
    ratio = tr // HALO
    n_t = T // tr
    last_halo = T // HALO - 1
    wp = jnp.zeros((HALO, CONV_DIM), F32).at[:CONV_K].set(w)

    def body(hm_ref, hh_ref, dm_ref, dh_ref, w_ref, dx_ref, dw_ref, db_ref, hwin, dwin):
        i = pl.program_id(0)
        hwin[0:HALO, :] = hh_ref[...] * (i > 0).astype(F32)
        hwin[HALO:, :] = hm_ref[...]
        dwin[0:tr, :] = dm_ref[...]
        dwin[tr:, :] = dh_ref[...] * (i < n_t - 1).astype(F32)

        @pl.when(i == 0)
        def _():
            dw_ref[...] = jnp.zeros_like(dw_ref)
            db_ref[...] = jnp.zeros_like(db_ref)

        db_ref[...] += jnp.sum(dm_ref[...], axis=0, keepdims=True)
        fold = lambda a: functools.reduce(jnp.add, [a[r:r + 8] for r in range(0, CONV_ROWS, 8)])
        for cs in _conv_lanes():
            wv = w_ref[:, cs]
            dws = [jnp.zeros((8, 128), F32) for _ in range(CONV_K)]
            for r0 in range(0, tr, CONV_ROWS):
                acc = jnp.zeros((CONV_ROWS, 128), F32)
                for k in range(CONV_K):
                    acc = acc + wv[k:k + 1] * dwin[pl.ds(CONV_K - 1 - k + r0, CONV_ROWS), cs]
                dx_ref[r0:r0 + CONV_ROWS, cs] = acc
                dy_t = dm_ref[r0:r0 + CONV_ROWS, cs]
                for k in range(CONV_K):
                    dws[k] = dws[k] + fold(dy_t * hwin[pl.ds(HALO - (CONV_K - 1) + k + r0, CONV_ROWS), cs])
            for k in range(CONV_K):
                dw_ref[k:k + 1, cs] += jnp.sum(dws[k], axis=0, keepdims=True)

    main = pl.BlockSpec((tr, CONV_DIM), lambda i: (i, 0))
    return pl.pallas_call(
        body, name=name, grid=(n_t,),
        in_specs=[main,
                  pl.BlockSpec((HALO, CONV_DIM), lambda i: (jnp.maximum(i * ratio - 1, 0), 0)),
                  main,
                  pl.BlockSpec((HALO, CONV_DIM), lambda i: (jnp.minimum((i + 1) * ratio, last_halo), 0)),
                  pl.BlockSpec((HALO, CONV_DIM), lambda i: (0, 0))],
        out_specs=[main, pl.BlockSpec((HALO, CONV_DIM), lambda i: (0, 0)), pl.BlockSpec((1, CONV_DIM), lambda i: (0, 0))],
        out_shape=[jax.ShapeDtypeStruct((T, CONV_DIM), F32), jax.ShapeDtypeStruct((HALO, CONV_DIM), F32),
                   jax.ShapeDtypeStruct((1, CONV_DIM), F32)],
        scratch_shapes=[pltpu.VMEM((tr + HALO, CONV_DIM), F32), pltpu.VMEM((tr + HALO, CONV_DIM), F32)],
        compiler_params=_cparams(("arbitrary",)),
    )(h, h, dy, dy, wp)


NEG = -1e30
ATT_SCALE = QK_DIM ** -0.5
_NT = (((1,), (1,)), ((), ()))
_TN = (((0,), (0,)), ((), ()))


def _att_blk(T):
    return _pick(T, (384, 128))


def _att_mask(i, j, blk):
    kpos = j * blk + lax.broadcasted_iota(jnp.int32, (blk, blk), 0)
    qpos = i * blk + lax.broadcasted_iota(jnp.int32, (blk, blk), 1)
    return (kpos <= qpos) & (kpos >= PAD_FRONT)


def _t32(a):
    return a.astype(F32).T.astype(BF16)


def _attn_fwd(q, k, v, T, name):
    blk = _att_blk(T)
    nq = T // blk

    def body(q_ref, k_ref, v_ref, o_ref, lse_ref, vt):
        i = pl.program_id(1)

        @pl.when(i == 0)
        def _():
            def tr(j, c):
                vt[j] = _t32(v_ref[pl.ds(pl.multiple_of(j * blk, blk), blk), :])
                return c

            lax.fori_loop(0, nq, tr, 0)

        qb = q_ref[...]

        def step(js, carry, masked):
            m, l, acc = carry
            ss = []
            for j in js:
                kb = k_ref[pl.ds(pl.multiple_of(j * blk, blk), blk), :]
                s = lax.dot_general(kb, qb, _NT, preferred_element_type=F32)
                ss.append(jnp.where(_att_mask(i, j, blk), s, NEG) if masked else s)
            m_new = m
            for s in ss:
                m_new = jnp.maximum(m_new, jnp.max(s, axis=0, keepdims=True))
            alpha = jnp.exp(m - m_new)
            l = alpha * l
            acc = alpha * acc
            for j, s in zip(js, ss):
                p = jnp.exp(s - m_new)
                l = l + jnp.sum(p, axis=0, keepdims=True)
                acc = acc + jnp.dot(vt[j], p.astype(BF16), preferred_element_type=F32)
            return m_new, l, acc

        init = (jnp.full((1, blk), NEG, F32), jnp.zeros((1, blk), F32), jnp.zeros((HEAD_W, blk), F32))
        later = jnp.minimum(i, 1)
        carry = lax.fori_loop(0, 1 - later, lambda t, c: step([i], c, True), init)
        carry = lax.fori_loop(0, later, lambda t, c: step([i, 0], c, True), carry)
        n_free = jnp.maximum(i - 1, 0)
        n4 = n_free // 4
        carry = lax.fori_loop(0, n4, lambda t, c: step([1 + 4 * t + d for d in range(4)], c, False), carry)
        rest = n_free - 4 * n4
        carry = lax.fori_loop(0, rest // 2, lambda t, c: step([i - rest, i - rest + 1], c, False), carry)
        m, l, acc = lax.fori_loop(0, rest % 2, lambda t, c: step([i - 1], c, False), carry)
        o_ref[...] = (acc / l).T.astype(o_ref.dtype)
        lse_ref[0, 0] = m + jnp.log(l)

    full = pl.BlockSpec((T, HEAD_W), lambda h, i: (0, h))
    return pl.pallas_call(
        body, name=name, grid=(HEADS, nq),
        in_specs=[pl.BlockSpec((blk, HEAD_W), lambda h, i: (i, h)), full, full],
        out_specs=[pl.BlockSpec((blk, HEAD_W), lambda h, i: (i, h)),
                   pl.BlockSpec((1, 1, 1, blk), lambda h, i: (h, i, 0, 0))],
        out_shape=[jax.ShapeDtypeStruct((T, HEADS * HEAD_W), BF16), jax.ShapeDtypeStruct((HEADS, nq, 1, blk), F32)],
        scratch_shapes=[pltpu.VMEM((nq, HEAD_W, blk), BF16)],
        compiler_params=_cparams(("parallel", "arbitrary")),
    )(q, k, v)


def _attn_bwd(q, k, v, o, lse, do, T, name):
    blk = _att_blk(T)
    nq = T // blk

    def body(q_ref, k_ref, v_ref, o_ref, lse_ref, do_ref, dq_ref, dk_ref, dv_ref, delta, dqt, dk_acc, dv_acc):
        j = pl.program_id(1)

        @pl.when(j == 0)
        def _():
            dqt[...] = jnp.zeros_like(dqt)

            def dstep(i, c):
                r0 = pl.multiple_of(i * blk, blk)
                prod = do_ref[pl.ds(r0, blk), :].astype(F32) * o_ref[pl.ds(r0, blk), :].astype(F32)
                delta[i] = jnp.sum(prod.T, axis=0, keepdims=True)
                return c

            lax.fori_loop(0, nq, dstep, 0)

        kb = k_ref[...]
        vb = v_ref[...]
        kbt = _t32(kb)
        dk_acc[...] = jnp.zeros_like(dk_acc)
        dv_acc[...] = jnp.zeros_like(dv_acc)

        def step(qs, masked):
            dvs, dks = [], []
            for i in qs:
                r0 = pl.multiple_of(i * blk, blk)
                qb = q_ref[pl.ds(r0, blk), :]
                dob = do_ref[pl.ds(r0, blk), :]
                s = lax.dot_general(kb, qb, _NT, preferred_element_type=F32)
                p = jnp.exp(s - lse_ref[0, i])
                if masked:
                    p = jnp.where(_att_mask(i, j, blk), p, 0.0)
                dvs.append(jnp.dot(p.astype(BF16), dob, preferred_element_type=F32))
                dp = lax.dot_general(vb, dob, _NT, preferred_element_type=F32)
                ds = (p * (dp - delta[i])).astype(BF16)
                dks.append(jnp.dot(ds, qb, preferred_element_type=F32))
                dqt[i] += jnp.dot(kbt, ds, preferred_element_type=F32)
            dv_acc[...] += functools.reduce(jnp.add, dvs)
            dk_acc[...] += functools.reduce(jnp.add, dks)

        def loop(lo, masked):
            n = nq - lo
            n3 = n // 3
            rest = n - 3 * n3

            def triple(t, c):
                step([lo + 3 * t + d for d in range(3)], masked)
                return c

            def pair(t, c):
                step([nq - 2, nq - 1], masked)
                return c

            def last(t, c):
                step([nq - 1], masked)
                return c

            lax.fori_loop(0, n3, triple, 0)
            lax.fori_loop(0, rest // 2, pair, 0)
            lax.fori_loop(0, rest % 2, last, 0)

        @pl.when(j == 0)
        def _():
            loop(0, True)

        @pl.when(j > 0)
        def _():
            step([j], True)
            loop(j + 1, False)

        dk_ref[...] = dk_acc[...].astype(dk_ref.dtype)
        dv_ref[...] = dv_acc[...].astype(dv_ref.dtype)

        @pl.when(j == nq - 1)
        def _():
            def wstep(i, c):
                dq_ref[pl.ds(pl.multiple_of(i * blk, blk), blk), :] = dqt[i].T
                return c

            lax.fori_loop(0, nq, wstep, 0)

    full = pl.BlockSpec((T, HEAD_W), lambda h, j: (0, h))
    kblk = pl.BlockSpec((blk, HEAD_W), lambda h, j: (j, h))
    wide = (T, HEADS * HEAD_W)
    return pl.pallas_call(
        body, name=name, grid=(HEADS, nq),
        in_specs=[full, kblk, kblk, full, pl.BlockSpec((1, nq, 1, blk), lambda h, j: (h, 0, 0, 0)), full],
        out_specs=[full, kblk, kblk],
        out_shape=[jax.ShapeDtypeStruct(wide, F32), jax.ShapeDtypeStruct(wide, BF16), jax.ShapeDtypeStruct(wide, BF16)],
        scratch_shapes=[pltpu.VMEM((nq, 1, blk), F32), pltpu.VMEM((nq, HEAD_W, blk), F32),
                        pltpu.VMEM((blk, HEAD_W), F32), pltpu.VMEM((blk, HEAD_W), F32)],
        compiler_params=_cparams(("parallel", "arbitrary")),
    )(q, k, v, o, lse, do)


HG_NB = 6
C = HG_CHUNK
_HI = lax.Precision.HIGHEST


def _tri(lower):
    r = lax.broadcasted_iota(jnp.int32, (C, C), 0)
    c = lax.broadcasted_iota(jnp.int32, (C, C), 1)
    return ((c <= r) if lower else (c >= r)).astype(F32)


HG_SUB = 8
N_SUB = C // HG_SUB


def _hg_split_decay(b, I, rid):
    lo = I * HG_SUB
    r = b[lo:lo + 1]
    eq = jnp.exp(b[lo:lo + HG_SUB] - r)
    ek = jnp.concatenate([jnp.exp(jnp.minimum(r - b[:lo], 0.0)), jnp.zeros((C - lo, HG_DK), F32)], axis=0)
    return eq, ek


def _hg_intra_fwd(q, k, v, b):
    rid = lax.broadcasted_iota(jnp.int32, (C, 1), 0)
    tid = lax.broadcasted_iota(jnp.int32, (HG_SUB, 1), 0)
    a_rows = [jnp.zeros((HG_SUB, C), F32)]
    blocks = []
    for I in range(N_SUB):
        lo = I * HG_SUB
        q_i, b_i = q[lo:lo + HG_SUB], b[lo:lo + HG_SUB]
        if I > 0:
            eq, ek = _hg_split_decay(b, I, rid)
            a_rows.append(lax.dot_general((q_i * eq).astype(BF16), (k * ek).astype(BF16), _NT,
                                          preferred_element_type=F32))
        o_i = jnp.zeros((HG_SUB, HG_DV), F32)
        for s in range(HG_SUB):
            r = lo + s
            e = jnp.exp(jnp.minimum(b_i - b[r:r + 1], 0.0))
            a = jnp.sum(q_i * k[r:r + 1] * e, axis=-1, keepdims=True)
            o_i = o_i + jnp.where(tid >= s, a, 0.0) * v[r:r + 1]
        blocks.append(o_i)
    a_off = jnp.concatenate(a_rows, axis=0).astype(BF16)
    return jnp.dot(a_off, v.astype(BF16), preferred_element_type=F32) + jnp.concatenate(blocks, axis=0)


def _hg_intra_bwd(q, k, v, b, do, dk_s, dv_s):
    rid = lax.broadcasted_iota(jnp.int32, (C, 1), 0)
    tid = lax.broadcasted_iota(jnp.int32, (HG_SUB, 1), 0)
    da_all = lax.dot_general(do.astype(BF16), v.astype(BF16), _NT, preferred_element_type=F32)
    a_rows = [jnp.zeros((HG_SUB, C), F32)]
    dq_blocks = []
    dk = jnp.zeros((C, HG_DK), F32)
    for I in range(N_SUB):
        lo = I * HG_SUB
        q_i, b_i, do_i = q[lo:lo + HG_SUB], b[lo:lo + HG_SUB], do[lo:lo + HG_SUB]
        dq_i = jnp.zeros((HG_SUB, HG_DK), F32)
        if I > 0:
            eq, ek = _hg_split_decay(b, I, rid)
            qs, ks = (q_i * eq).astype(BF16), (k * ek).astype(BF16)
            a_rows.append(lax.dot_general(qs, ks, _NT, preferred_element_type=F32))
            da = da_all[lo:lo + HG_SUB].astype(BF16)
            dq_i = jnp.dot(da, ks, preferred_element_type=F32) * eq
            dk = dk + lax.dot_general(da, qs, _TN, preferred_element_type=F32) * ek
        for s in range(HG_SUB):
            r = lo + s
            e = jnp.where(tid >= s, jnp.exp(jnp.minimum(b_i - b[r:r + 1], 0.0)), 0.0)
            a = jnp.sum(q_i * k[r:r + 1] * e, axis=-1, keepdims=True)
            g = jnp.sum(do_i * v[r:r + 1], axis=-1, keepdims=True) * e
            dq_i = dq_i + g * k[r:r + 1]
            dk_s[r:r + 1, :] = jnp.sum(g * q_i, axis=0, keepdims=True)
            dv_s[r:r + 1, :] = jnp.sum(a * do_i, axis=0, keepdims=True)
        dq_blocks.append(dq_i)
    a_off = jnp.concatenate(a_rows, axis=0).astype(BF16)
    dv = lax.dot_general(a_off, do.astype(BF16), _TN, preferred_element_type=F32)
    return jnp.concatenate(dq_blocks, axis=0), dk + dk_s[...], dv + dv_s[...]


def _hgrn_fwd(u, kk, lf, vv, T, name):
    nb = _pick(T // C, (HG_NB, 3, 2, 1))
    rows = nb * C
    qblk = C_HQ // HG_DK

    def body(q_ref, k_ref, lf_ref, v_ref, o_ref, st_ref, st):
        @pl.when(pl.program_id(1) == 0)
        def _():
            st[...] = jnp.zeros_like(st)

        lower = _tri(True)
        for n in range(nb):
            sl = slice(n * C, (n + 1) * C)
            q, k, v = q_ref[sl, :].astype(F32), k_ref[sl, :], v_ref[sl, :]
            b = jnp.dot(lower, lf_ref[sl, :], precision=_HI, preferred_element_type=F32)
            s_t = st[...]
            st_ref[0, n] = s_t
            qe = (q * jnp.exp(b)).astype(BF16)
            o = lax.dot_general(qe, s_t.astype(BF16), _NT, preferred_element_type=F32)
            o_ref[sl, :] = o + _hg_intra_fwd(q, k, v, b)
            bl = b[C - 1:C, :]
            kd = (k * jnp.exp(bl - b)).astype(BF16)
            st[...] = s_t * jnp.exp(bl) + lax.dot_general(v.astype(BF16), kd, _TN, preferred_element_type=F32)

    col = lambda off: pl.BlockSpec((rows, HG_DK), lambda h, c: (c, h + off))
    return pl.pallas_call(
        body, name=name, grid=(HG_HEADS, T // rows),
        in_specs=[col(qblk), col(0), col(0), col(0)],
        out_specs=[col(0), pl.BlockSpec((1, nb, HG_DV, HG_DK), lambda h, c: (h, c, 0, 0))],
        out_shape=[jax.ShapeDtypeStruct((T, HG_HEADS * HG_DV), F32),
                   jax.ShapeDtypeStruct((HG_HEADS, T // C, HG_DV, HG_DK), F32)],
        scratch_shapes=[pltpu.VMEM((HG_DV, HG_DK), F32)],
        compiler_params=_cparams(("parallel", "arbitrary")),
    )(u, kk, lf, vv)


def _hgrn_bwd(u, kk, lf, vv, states, do, T, name):
    nb = _pick(T // C, (HG_NB, 3, 2, 1))
    rows = nb * C
    n_steps = T // rows
    qblk = C_HQ // HG_DK

    def body(q_ref, k_ref, lf_ref, v_ref, st_ref, do_ref, dq_ref, dk_ref, dlf_ref, dv_ref, dst, dk_s, dv_s):
        @pl.when(pl.program_id(1) == 0)
        def _():
            dst[...] = jnp.zeros_like(dst)

        lower, upper = _tri(True), _tri(False)
        rid = lax.broadcasted_iota(jnp.int32, (C, 1), 0)
        for n in reversed(range(nb)):
            sl = slice(n * C, (n + 1) * C)
            q, k, v, do = q_ref[sl, :].astype(F32), k_ref[sl, :], v_ref[sl, :], do_ref[sl, :]
            b = jnp.dot(lower, lf_ref[sl, :], precision=_HI, preferred_element_type=F32)
            s_t = st_ref[0, n]
            d_new = dst[...]
            eb = jnp.exp(b)
            bl = b[C - 1:C, :]
            ebl = jnp.exp(bl)
            dec = jnp.exp(bl - b)
            qe = q * eb
            kd = k * dec
            do_b = do.astype(BF16)
            dqe = jnp.dot(do_b, s_t.astype(BF16), preferred_element_type=F32)
            dkd = jnp.dot(v.astype(BF16), d_new.astype(BF16), preferred_element_type=F32)
            dv = lax.dot_general(kd.astype(BF16), d_new.astype(BF16), _NT, preferred_element_type=F32)
            dbl = ebl * jnp.sum(d_new * s_t, axis=0, keepdims=True) + jnp.sum(dkd * kd, axis=0, keepdims=True)
            dst[...] = d_new * ebl + lax.dot_general(do_b, qe.astype(BF16), _TN, preferred_element_type=F32)
            dq_in, dk_in, dv_in = _hg_intra_bwd(q, k, v, b, do, dk_s, dv_s)
            dq = dqe * eb + dq_in
            dk = dkd * dec + dk_in
            dv = dv + dv_in
            db = q * dq - k * dk
            db = db + jnp.where(rid == C - 1, dbl, 0.0)
            dq_ref[sl, :] = dq
            dk_ref[sl, :] = dk
            dv_ref[sl, :] = dv
            dlf_ref[sl, :] = jnp.dot(upper, db, precision=_HI, preferred_element_type=F32)

    rev = lambda off: pl.BlockSpec((rows, HG_DK), lambda h, c: (n_steps - 1 - c, h + off))
    return pl.pallas_call(
        body, name=name, grid=(HG_HEADS, n_steps),
        in_specs=[rev(qblk), rev(0), rev(0), rev(0),
                  pl.BlockSpec((1, nb, HG_DV, HG_DK), lambda h, c: (h, n_steps - 1 - c, 0, 0)), rev(0)],
        out_specs=[rev(0)] * 4,
        out_shape=[jax.ShapeDtypeStruct((T, HG_HEADS * HG_DK), F32)] * 4,
        scratch_shapes=[pltpu.VMEM((HG_DV, HG_DK), F32), pltpu.VMEM((C, HG_DK), F32), pltpu.VMEM((C, HG_DV), F32)],
        compiler_params=_cparams(("parallel", "arbitrary")),
    )(u, kk, lf, vv, states, do)


def _rope_tables(T):
    half = ROPE // 2
    inv_freq = (ROPE_BASE ** (-np.arange(half, dtype=np.float32) / half)).astype(np.float32)
    row = lambda lo, hi, val: np.concatenate([np.zeros(lo, np.float32), np.asarray(val, np.float32) * np.ones(hi - lo, np.float32),
                                              np.zeros(HEAD_W - hi, np.float32)])[None, :]
    freq = row(NOPE, NOPE + half, inv_freq) + row(NOPE + half, NOPE + ROPE, inv_freq)
    pos = lax.broadcasted_iota(jnp.int32, (T, HEAD_W), 0).astype(F32) - float(PAD_FRONT)
    ang = pos * freq
    cos, sin = jnp.cos(ang), jnp.sin(ang)
    c = cos * row(NOPE, NOPE + ROPE, 1.0) + row(0, NOPE, 1.0)
    s1 = sin * row(NOPE, NOPE + half, -1.0)
    s2 = sin * row(NOPE + half, NOPE + ROPE, 1.0)
    return c, s1, s2


def _layer_fwd(x, w, tabs, T, l, late=None):
    c, s1, s2 = tabs
    n = lambda s: f"l{l}_{s}"
    sv = {"x": x}
    h, sv["h_t"] = _rowwise(_f_rms, T, [Row(x)], [(w["norm1_g"], D_MODEL)], [(D_MODEL, BF16)], n("norm1"), True)
    u = _mm(h, w["w_in"], out_dtype=BF16, name=n("in_proj"))
    sv.update(h=h, u=u)
    if late is not None:
        w.update(late(u))
    hglu = _rowwise(_f_glu, T, [Row(u, 512, C_CONV_A), Row(u, 512, C_CONV_G)], [], [(CONV_DIM, F32)], n("glu"))[0]
    cv = _conv_fwd(hglu, w["conv_w"], w["conv_b"], T, n("conv"))
    hc = _rowwise(_f_lnsilu, T, [Row(cv)], [(w["conv_ln_g"], CONV_DIM), (w["conv_ln_b"], CONV_DIM)],
                  [(CONV_DIM, BF16)], n("conv_ln"))[0]
    y_a = _mm(hc, w["w_conv_out"], out_dtype=BF16, name=n("conv_out"))
    sv.update(hglu=hglu, cv=cv, hc=hc, y_a=y_a)
    cqn = _rowwise(_f_rms, T, [Row(u, Q_RANK, C_CQ)], [(w["q_a_norm_g"], Q_RANK)], [(Q_RANK, BF16)], n("q_a_norm"))[0]
    ckvn = _rowwise(_f_rms, T, [Row(u, KV_RANK, C_CKV)], [(w["kv_a_norm_g"], KV_RANK)], [(KV_RANK, BF16)], n("kv_a_norm"))[0]
    q_raw = _mm(cqn, w["w_uq"], out_dtype=BF16, name=n("uq"))
    k_raw = _mm(ckvn, w["w_uk"], out_dtype=BF16, name=n("uk"))
    v = _mm(ckvn, w["w_uv"], out_dtype=BF16, name=n("uv"))
    tab_rows = [Row(c), Row(s1), Row(s2)]
    q = _rowwise(_f_qrope, T, [Row(q_raw, piece=HEAD_W)] + tab_rows, [(w["q_norm_g"], HEAD_W)],
                 [(HEADS * HEAD_W, BF16)], n("q_rope"))[0]
    k = _rowwise(_f_krope, T, [Row(k_raw, piece=HEAD_W), Row(u, HEAD_W, C_KR)] + tab_rows, [(w["k_norm_g"], HEAD_W)],
                 [(HEADS * HEAD_W, BF16)], n("k_rope"))[0]
    o, lse = _attn_fwd(q, k, v, T, n("attn"))
    y_b = _mm(o, w["w_attn_out"], out_dtype=BF16, name=n("attn_out"))
    sv.update(cqn=cqn, ckvn=ckvn, q_raw=q_raw, k_raw=k_raw, v=v, q=q, k=k, o=o, lse=lse, y_b=y_b)
    kk, lf, vv = _rowwise(_f_hgrn_prep, T, [Row(u, 512, C_HF), Row(u, 512, C_HI)], [(w["lb"], 512)],
                          [(512, F32)] * 3, n("hgrn_prep"))
    o_h, states = _hgrn_fwd(u, kk, lf, vv, T, n("hgrn"))
    oh = _rowwise(_f_hgrn_out, T, [Row(o_h, piece=HG_DV), Row(u, 512, C_HG, piece=HG_DV)], [(w["hgrn_norm_g"], HG_DV)],
                  [(512, BF16)], n("hgrn_out_norm"))[0]
    y_c = _mm(oh, w["w_hgrn_out"], out_dtype=BF16, name=n("hgrn_out"))
    sv.update(kk=kk, lf=lf, vv=vv, o_h=o_h, states=states, oh=oh, y_c=y_c)
    gate_rows = [Row(u, D_MODEL, C_GATE + g * D_MODEL) for g in range(3)]
    mix, sv["mix_t"] = _rowwise(_f_mix, T, gate_rows + [Row(y_a), Row(y_b), Row(y_c)], [], [(D_MODEL, BF16)], n("mix"), True)
    x1 = _mm(mix, w["w_out"], res=x, name=n("out_proj"))
    h2, sv["h2_t"] = _rowwise(_f_rms, T, [Row(x1)], [(w["norm2_g"], D_MODEL)], [(D_MODEL, BF16)], n("norm2"), True)
    f = _mm(h2, w["w_ff1"], out_dtype=BF16, name=n("ff1"))
    x2 = _mm(f, w["w_ff2"], res=x1, a_fn=_relu2, name=n("ff2"))
    sv.update(mix=mix, x1=x1, h2=h2, f=f)
    return x2, sv


def _layer_bwd(dx2, w, sv, tabs, T, l, mid=None, matrices=None):
    c, s1, s2 = tabs
    n = lambda s: f"l{l}_b_{s}"
    u = sv["u"]
    g = {}
    g["w_ff2"] = _mm(sv["f"], dx2, ta=True, a_fn=_relu2, out_dtype=BF16, name=n("dw_ff2"))
    df = _mm(dx2, w["w_ff2"], tb=True, out_dtype=BF16, name=n("d_f"),
             epi=(sv["f"], lambda d, fv: d * (2.0 * jnp.maximum(fv, 0.0))))
    g["w_ff1"] = _mm(sv["h2_t"], df, out_dtype=BF16, name=n("dw_ff1"))
    dh2 = _mm(df, w["w_ff1"], tb=True, name=n("d_h2"))
    (dx1,), (g["norm2_g"],) = _rowwise_bwd(_f_rms, T, [Row(sv["x1"])], [(w["norm2_g"], D_MODEL)], [Row(dh2)],
                                           {0: F32}, n("norm2"), add=(0, dx2))
    g["w_out"] = _mm(sv["mix_t"], dx1, out_dtype=BF16, name=n("dw_out"))
    w_out = w["w_out"] if mid is None else mid(g, w["w_out"])
    dmix = _mm(dx1, w_out, tb=True, out_dtype=BF16, name=n("d_mix"))
    gate_rows = [Row(u, D_MODEL, C_GATE + i * D_MODEL) for i in range(3)]
    (dg0, dg1, dg2, dy_a, dy_b, dy_c), _ = _rowwise_bwd(
        _f_mix, T, gate_rows + [Row(sv["y_a"]), Row(sv["y_b"]), Row(sv["y_c"])], [], [Row(dmix)],
        {0: BF16, 1: BF16, 2: BF16, 3: BF16, 4: BF16, 5: BF16}, n("mix"))
    g["w_hgrn_out"] = _mm(sv["oh"], dy_c, ta=True, out_dtype=BF16, name=n("dw_hgrn_out"))
    doh = _mm(dy_c, w["w_hgrn_out"], tb=True, out_dtype=BF16, name=n("d_oh"))
    (do_h, dhg), (g["hgrn_norm_g"],) = _rowwise_bwd(
        _f_hgrn_out, T, [Row(sv["o_h"], piece=HG_DV), Row(u, 512, C_HG, piece=HG_DV)], [(w["hgrn_norm_g"], HG_DV)],
        [Row(doh, piece=HG_DV)], {0: F32, 1: BF16}, n("hgrn_out_norm"))
    dhq, dkk, dlf, dvv = _hgrn_bwd(u, sv["kk"], sv["lf"], sv["vv"], sv["states"], do_h, T, n("hgrn"))
    (dhf, dhi), (g["lb"],) = _rowwise_bwd(
        _f_hgrn_prep, T, [Row(u, 512, C_HF), Row(u, 512, C_HI)], [(w["lb"], 512)],
        [Row(dkk), Row(dlf), Row(dvv)], {0: BF16, 1: BF16}, n("hgrn_prep"))
    g["w_attn_out"] = _mm(sv["o"], dy_b, ta=True, out_dtype=BF16, name=n("dw_attn_out"))
    do = _mm(dy_b, w["w_attn_out"], tb=True, out_dtype=BF16, name=n("d_o"))
    dq, dk, dv = _attn_bwd(sv["q"], sv["k"], sv["v"], sv["o"], sv["lse"], do, T, n("attn"))
    tab_rows = [Row(c), Row(s1), Row(s2)]
    (dq_raw,), (g["q_norm_g"],) = _rowwise_bwd(
        _f_qrope, T, [Row(sv["q_raw"], piece=HEAD_W)] + tab_rows, [(w["q_norm_g"], HEAD_W)],
        [Row(dq, piece=HEAD_W)], {0: BF16}, n("q_rope"))
    (dk_raw, dkr), (g["k_norm_g"],) = _rowwise_bwd(
        _f_krope, T, [Row(sv["k_raw"], piece=HEAD_W), Row(u, HEAD_W, C_KR)] + tab_rows, [(w["k_norm_g"], HEAD_W)],
        [Row(dk, piece=HEAD_W)], {0: BF16, 1: BF16}, n("k_rope"))
    g["w_uq"] = _mm(sv["cqn"], dq_raw, ta=True, out_dtype=BF16, name=n("dw_uq"))
    g["w_uk"] = _mm(sv["ckvn"], dk_raw, ta=True, out_dtype=BF16, name=n("dw_uk"))
    g["w_uv"] = _mm(sv["ckvn"], dv, ta=True, out_dtype=BF16, name=n("dw_uv"))
    dcqn = _mm(dq_raw, w["w_uq"], tb=True, out_dtype=BF16, name=n("d_cqn"))
    dckvn = _mm(dk_raw, w["w_uk"], tb=True, name=n("d_ckvn_k"))
    dckvn = _mm(dv, w["w_uv"], tb=True, res=dckvn, out_dtype=BF16, name=n("d_ckvn_v"))
    (dcq,), (g["q_a_norm_g"],) = _rowwise_bwd(_f_rms, T, [Row(u, Q_RANK, C_CQ)], [(w["q_a_norm_g"], Q_RANK)],
                                              [Row(dcqn)], {0: BF16}, n("q_a_norm"))
    (dckv,), (g["kv_a_norm_g"],) = _rowwise_bwd(_f_rms, T, [Row(u, KV_RANK, C_CKV)], [(w["kv_a_norm_g"], KV_RANK)],
                                                [Row(dckvn)], {0: BF16}, n("kv_a_norm"))
    g["w_conv_out"] = _mm(sv["hc"], dy_a, ta=True, out_dtype=BF16, name=n("dw_conv_out"))
    dhc = _mm(dy_a, w["w_conv_out"], tb=True, out_dtype=BF16, name=n("d_hc"))
    (dcv,), (g["conv_ln_g"], g["conv_ln_b"]) = _rowwise_bwd(
        _f_lnsilu, T, [Row(sv["cv"])], [(w["conv_ln_g"], CONV_DIM), (w["conv_ln_b"], CONV_DIM)], [Row(dhc)],
        {0: F32}, n("conv_ln"))
    dhglu, dconv_w, g["conv_b"] = _conv_bwd(sv["hglu"], w["conv_w"], dcv, T, n("conv"))
    g["conv_w"] = dconv_w[:CONV_K]
    (dua, dug), _ = _rowwise_bwd(_f_glu, T, [Row(u, 512, C_CONV_A), Row(u, 512, C_CONV_G)], [], [Row(dhglu)],
                                 {0: BF16, 1: BF16}, n("glu"))
    du = jnp.concatenate([dua, dug, dg0, dg1, dg2, dcq, dckv, dkr, dhq.astype(BF16), dhf, dhi, dhg], axis=1)
    small = ("w_uq", "w_uk", "w_uv", "w_attn_out", "w_hgrn_out", "w_conv_out")
    du, *done = lax.optimization_barrier((du, *[g[k] for k in small]))
    g.update(zip(small, done))
    g["w_in"] = _mm(sv["h_t"], du, out_dtype=BF16, name=n("dw_in"))
    norm_g = w["norm1_g"] if matrices is None else matrices(g, w["norm1_g"])
    du, norm_g = lax.optimization_barrier((du, norm_g))
    dh = _mm(du, w["w_in"], tb=True, name=n("d_h"))
    (dx,), (g["norm1_g"],) = _rowwise_bwd(_f_rms, T, [Row(sv["x"])], [(norm_g, D_MODEL)], [Row(dh)],
                                          {0: F32}, n("norm1"), add=(0, dx1))
    return dx, g


def _pad_w_in(w_in):
    z = lambda k: jnp.zeros((w_in.shape[0], k), w_in.dtype)
    return jnp.concatenate([w_in[:, :O_CQ], w_in[:, O_GATE:], w_in[:, O_CQ:O_KR], z(KR_LANE), w_in[:, O_KR:O_HQ],
                            z(HEAD_W - KR_LANE - ROPE), w_in[:, O_HQ:O_GATE]], axis=1)


def _unpad_w_in(g):
    return jnp.concatenate([g[:, :C_GATE], g[:, C_CQ:C_KR], g[:, C_KR + KR_LANE:C_KR + KR_LANE + ROPE],
                            g[:, C_HQ:], g[:, C_GATE:C_CQ]], axis=1)


_W_IN_RUNS = ((0, 0, O_CQ), (O_CQ, C_CQ, O_KR - O_CQ), (O_KR, C_KR + KR_LANE, ROPE), (O_HQ, C_HQ, O_GATE - O_HQ),
              (O_GATE, C_GATE, N_IN - O_GATE))


def _w_in_from_shards(g8):
    per = N_IN // N_DEV
    pieces, at = [], 0
    for o0, p0, n in sorted(_W_IN_RUNS, key=lambda r: r[1]):
        if p0 > at:
            pieces.append(jnp.zeros((g8.shape[1], p0 - at), g8.dtype))
        for j in range(o0 // per, (o0 + n - 1) // per + 1):
            lo, hi = max(o0, j * per), min(o0 + n, (j + 1) * per)
            pieces.append(g8[j][:, lo - j * per:hi - j * per])
        at = p0 + n
    if at < N_IN_P:
        pieces.append(jnp.zeros((g8.shape[1], N_IN_P - at), g8.dtype))
    return jnp.concatenate(pieces, axis=1)


def _w_in_grad_shards(g):
    per = N_IN // N_DEV
    shards = []
    for j in range(N_DEV):
        lo, hi = j * per, (j + 1) * per
        pieces = [g[:, p0 + max(lo, o0) - o0:p0 + min(hi, o0 + n) - o0]
                  for o0, p0, n in _W_IN_RUNS if max(lo, o0) < min(hi, o0 + n)]
        shards.append(jnp.concatenate(pieces, axis=1) if len(pieces) > 1 else pieces[0])
    return jnp.stack(shards)


def _pad_heads(wm, per_head, lo, hi):
    lead = wm.shape[:-1]
    wh = wm.reshape(lead + (HEADS, per_head))[..., lo:hi]
    pad = [(0, 0)] * len(lead) + [(0, 0), (0, HEAD_W - (hi - lo))]
    return jnp.pad(wh, pad).reshape(lead + (HEADS * HEAD_W,))


def _unpad_heads(gm, width):
    lead = gm.shape[:-1]
    return gm.reshape(lead + (HEADS, HEAD_W))[..., :width]


def _layer_weights(full, lb=None):
    w = {}
    for name in ("norm1_g", "conv_w", "conv_b", "conv_ln_g", "conv_ln_b", "q_a_norm_g", "kv_a_norm_g", "hgrn_norm_g",
                 "norm2_g", "w_conv_out", "w_hgrn_out", "w_out", "w_ff1", "w_ff2"):
        if name in full:
            w[name] = full[name]
    if lb is not None:
        w["lb"] = lb
    if "w_in_padded" in full:
        w["w_in"] = full["w_in_padded"]
    elif "w_in" in full:
        w["w_in"] = _pad_w_in(full["w_in"])
    if "w_uq" in full:
        w["w_uq"] = _pad_heads(full["w_uq"], QK_DIM, 0, QK_DIM)
    if "w_ukv" in full:
        w["w_uk"] = _pad_heads(full["w_ukv"], NOPE + V_DIM, 0, NOPE)
        w["w_uv"] = _pad_heads(full["w_ukv"], NOPE + V_DIM, NOPE, NOPE + V_DIM)
    for name in ("q_norm_g", "k_norm_g"):
        if name in full:
            w[name] = jnp.pad(full[name], (0, HEAD_W - QK_DIM))
    if "w_attn_out" in full:
        wa = full["w_attn_out"].reshape(HEADS, V_DIM, D_MODEL)
        w["w_attn_out"] = jnp.pad(wa, ((0, 0), (0, HEAD_W - V_DIM), (0, 0))).reshape(HEADS * HEAD_W, D_MODEL)
    return w


def _matrix_grads_to_original(g):
    o = {name: g[name] for name in ("w_conv_out", "w_hgrn_out", "w_out", "w_ff1", "w_ff2")}
    o["w_in"] = _unpad_w_in(g["w_in"])
    o["w_in_shards"] = _w_in_grad_shards(g["w_in"])
    o["w_uq"] = _unpad_heads(g["w_uq"], QK_DIM).reshape(Q_RANK, HEADS * QK_DIM)
    guk = _unpad_heads(g["w_uk"], NOPE)
    guv = _unpad_heads(g["w_uv"], V_DIM)
    o["w_ukv"] = jnp.concatenate([guk, guv], axis=-1).reshape(KV_RANK, HEADS * (NOPE + V_DIM))
    o["w_attn_out"] = g["w_attn_out"].reshape(HEADS, HEAD_W, D_MODEL)[:, :V_DIM].reshape(HEADS * V_DIM, D_MODEL)
    return o


def _vector_grads_to_original(g):
    o = {"conv_w": g["conv_w"]}
    for name in ("norm1_g", "conv_b", "conv_ln_g", "conv_ln_b", "q_a_norm_g", "kv_a_norm_g", "hgrn_norm_g", "norm2_g", "lb"):
        o[name] = g[name].reshape(-1)
    o["q_norm_g"] = g["q_norm_g"].reshape(-1)[:QK_DIM]
    o["k_norm_g"] = g["k_norm_g"].reshape(-1)[:QK_DIM]
    return o


def _lower_bounds(logits):
    p = jax.nn.softmax(logits.astype(F32), axis=0)
    return jnp.cumsum(p, axis=0) - p[0:1]


def _run_step(x, target, meta, lb_logits, layer_weights, layer_done, layer_mid=None, layer_matrices=None):
    seq = x.shape[0]
    T = ROW0 + seq
    assert T % 128 == 0
    tabs = _rope_tables(T)
    lbs, lb_vjp = jax.vjp(_lower_bounds, lb_logits)
    xp = jnp.concatenate([jnp.zeros((PAD_FRONT, D_MODEL), F32), meta.astype(F32), x], axis=0)
    tp = jnp.concatenate([jnp.zeros((ROW0, D_MODEL), F32), target], axis=0)
    ws, svs = [], []
    for l in range(DEPTH):
        full, xp = layer_weights(l, xp)
        late = full.pop("late", None)
        w = _layer_weights(full, lbs[l])
        xp, sv = _layer_fwd(xp, w, tabs, T, l, late)
        ws.append(w)
        svs.append(sv)
    dx, sq = _loss_head(xp, tp, T)
    loss = 0.5 * jnp.sum(sq) * (1.0 / D_MODEL)
    dlb = [None] * DEPTH
    for l in reversed(range(DEPTH)):
        mid = None if layer_mid is None else functools.partial(layer_mid, l)
        mats = {}

        def matrices(g, norm_g, l=l, mats=mats):
            mats.update(_matrix_grads_to_original(g))
            return norm_g if layer_matrices is None else layer_matrices(l, mats, norm_g)

        dx, g = _layer_bwd(dx, ws[l], svs[l], tabs, T, l, mid, matrices)
        g = {**_vector_grads_to_original(g), **mats}
        dlb[l] = g.pop("lb")
        dx = layer_done(l, g, dx, ws[l - 1] if l > 0 else None)
    return loss, dx[ROW0:], dx[PAD_FRONT:ROW0], lb_vjp(jnp.stack(dlb))[0]


def _local_step(x, target, full):
    per_layer = [None] * DEPTH

    def done(l, g, dx, below):
        per_layer[l] = g
        return dx

    loss, gx, gmeta, glb = _run_step(
        x, target, full["meta"], full["hgrn_lb_logits"],
        lambda l, xp: ({k: v[l] for k, v in full.items() if k != "meta"}, xp), done)
    grads = {k: jnp.stack([per_layer[l][k] for l in range(DEPTH)]) for k in per_layer[0]}
    grads["hgrn_lb_logits"] = glb
    grads["meta"] = gmeta
    return loss, gx, grads


def _mesh_pos():
    return lax.axis_index("x"), lax.axis_index("y"), lax.axis_index("c")


N_COPY = N_DEV - 1


def _all_gather(arrs, name):
    n = len(arrs)

    def body(*refs):
        x_refs, out_refs = refs[:n], refs[n:2 * n]
        send_sems, recv_sems, local_sems = refs[2 * n:]
        x, y, c = _mesh_pos()
        me, sibling = (x, y, c), (x, y, 1 - c)
        chips = [(1 - x, y), (x, 1 - y), (1 - x, 1 - y)]

        def slot(a, px, py, pc):
            return out_refs[a].at[4 * px + 2 * py + pc]

        def copy(a, k, block, to, own=False):
            return pltpu.make_async_remote_copy(
                src_ref=x_refs[a] if own else slot(a, *block), dst_ref=slot(a, *block),
                send_sem=send_sems.at[a * N_COPY + k], recv_sem=recv_sems.at[a * N_COPY + k],
                device_id=to, device_id_type=MESH)

        mine = [pltpu.make_async_copy(x_refs[a], slot(a, *me), local_sems.at[a]) for a in range(n)]
        for cp in mine:
            cp.start()
        first = []
        for a in range(n):
            first.append(copy(a, 0, me, sibling, own=True))
            first += [copy(a, 1 + j, me, (*chip, c), own=True) for j, chip in enumerate(chips)]
        for cp in first:
            cp.start()
        passed = []
        for j, chip in enumerate(chips):
            for a in range(n):
                copy(a, 1 + j, (*chip, c), me).wait_recv()
                cp = copy(a, 4 + j, (*chip, c), sibling)
                cp.start()
                passed.append(cp)
        for a in range(n):
            copy(a, 0, sibling, me).wait_recv()
            for j, chip in enumerate(chips):
                copy(a, 4 + j, (*chip, 1 - c), me).wait_recv()
        for cp in first + passed:
            cp.wait_send()
        for cp in mine:
            cp.wait()

    anyspec = pl.BlockSpec(memory_space=pl.ANY)
    return pl.pallas_call(
        body, name=name, out_shape=[jax.ShapeDtypeStruct((N_DEV,) + a.shape, a.dtype) for a in arrs],
        in_specs=[anyspec] * n, out_specs=[anyspec] * n,
        scratch_shapes=[pltpu.SemaphoreType.DMA((n * N_COPY,)), pltpu.SemaphoreType.DMA((n * N_COPY,)),
                        pltpu.SemaphoreType.DMA((n,))],
    )(*arrs)


def _exchange(arrs, name):
    n = len(arrs)

    def body(*refs):
        s_refs, r_refs = refs[:n], refs[n:2 * n]
        send_sems, recv_sems, local_sems = refs[2 * n:]
        x, y, c = _mesh_pos()
        me = 4 * x + 2 * y + c
        local = [pltpu.make_async_copy(s_refs[a].at[me], r_refs[a].at[me], local_sems.at[a]) for a in range(n)]
        for cp in local:
            cp.start()
        sends, recvs = [], []
        for rel in range(1, N_DEV):
            px = 1 - x if rel & 4 else x
            py = 1 - y if rel & 2 else y
            pc = 1 - c if rel & 1 else c
            p = 4 * px + 2 * py + pc
            for a in range(n):
                k = a * N_COPY + rel - 1
                sends.append(pltpu.make_async_remote_copy(
                    src_ref=s_refs[a].at[p], dst_ref=r_refs[a].at[me], send_sem=send_sems.at[k],
                    recv_sem=recv_sems.at[k], device_id=(px, py, pc), device_id_type=MESH))
                recvs.append(pltpu.make_async_remote_copy(
                    src_ref=s_refs[a].at[me], dst_ref=r_refs[a].at[p], send_sem=send_sems.at[k],
                    recv_sem=recv_sems.at[k], device_id=(px, py, pc), device_id_type=MESH))
        for cp in sends:
            cp.start()
        for cp in recvs:
            cp.wait_recv()
        for cp in sends:
            cp.wait_send()
        for cp in local:
            cp.wait()

    anyspec = pl.BlockSpec(memory_space=pl.ANY)
    return pl.pallas_call(
        body, name=name, out_shape=[jax.ShapeDtypeStruct(a.shape, a.dtype) for a in arrs],
        in_specs=[anyspec] * n, out_specs=[anyspec] * n,
        scratch_shapes=[pltpu.SemaphoreType.DMA((n * N_COPY,)), pltpu.SemaphoreType.DMA((n * N_COPY,)),
                        pltpu.SemaphoreType.DMA((n,))],
    )(*arrs)


_HBM = pl.BlockSpec(memory_space=pltpu.HBM)
_SEM = pl.BlockSpec(memory_space=pltpu.SEMAPHORE)
_EFFECT = pltpu.SideEffectType.DATAFLOW_SIDE_EFFECTING


def _peers(x, y, c):
    out = []
    for rel in range(1, N_DEV):
        px = 1 - x if rel & 4 else x
        py = 1 - y if rel & 2 else y
        pc = 1 - c if rel & 1 else c
        out.append((rel, (px, py, pc), 4 * px + 2 * py + pc))
    return out


ALL_RELS = tuple(range(1, N_DEV))
NEAR_RELS = (1, 2, 4, 6)


def _split_copies(src_refs, land_refs, send_sems, recv_sems, gather, rels=ALL_RELS):
    x, y, c = _mesh_pos()
    me = 4 * x + 2 * y + c
    out = []
    for a, (src, land) in enumerate(zip(src_refs, land_refs)):
        for rel, peer, p in _peers(x, y, c):
            if rel not in rels:
                continue
            k = a * N_COPY + rel - 1
            mk = lambda s, d: pltpu.make_async_remote_copy(
                src_ref=s, dst_ref=d, send_sem=send_sems.at[k], recv_sem=recv_sems.at[k],
                device_id=peer, device_id_type=MESH)
            mine = src if gather else src.at[p]
            out.append((mk(mine, land.at[me]), mk(mine, land.at[p])))
    return out


def _copy_start(srcs, gather, name, collective_id, rels=ALL_RELS):
    n = len(srcs)
    lands = [lax.empty(((N_DEV,) + s.shape) if gather else s.shape, s.dtype) for s in srcs]

    def body(*refs):
        src_refs, land_refs = refs[:n], refs[n:2 * n]
        send_sems, recv_sems = refs[2 * n], refs[2 * n + 1]
        token = refs[-1]
        x, y, c = _mesh_pos()
        barrier = pltpu.get_barrier_semaphore()
        for rel, peer, _ in _peers(x, y, c):
            if rel in rels:
                pl.semaphore_signal(barrier, inc=1, device_id=peer, device_id_type=MESH)
        pl.semaphore_wait(barrier, len(rels))
        for out_copy, _ in _split_copies(src_refs, land_refs, send_sems, recv_sems, gather, rels):
            out_copy.start()
        token[...] = jnp.zeros_like(token)

    hbm = lambda a: pltpu.HBM(a.shape, a.dtype)
    res = pl.pallas_call(
        body, name=name,
        out_shape=(pltpu.SemaphoreType.DMA((n * N_COPY,)), pltpu.SemaphoreType.DMA((n * N_COPY,)),
                   *[hbm(s) for s in srcs], *[hbm(z) for z in lands], jax.ShapeDtypeStruct((8, 128), F32)),
        in_specs=[_HBM] * (2 * n), out_specs=(_SEM, _SEM, *([_HBM] * (2 * n)), pl.BlockSpec(memory_space=pltpu.VMEM)),
        input_output_aliases={i: 2 + i for i in range(2 * n)},
        compiler_params=pltpu.CompilerParams(has_side_effects=_EFFECT, collective_id=collective_id),
    )(*[pltpu.with_memory_space_constraint(s, pltpu.HBM) for s in srcs],
      *[pltpu.with_memory_space_constraint(z, pltpu.HBM) for z in lands])
    return res[0], res[1], list(res[2:2 + n]), list(res[2 + n:2 + 2 * n]), res[-1]


def _after(a, token):
    return a + token[0, 0].astype(a.dtype)


def _forward_to_sibling(lands, name):
    n = len(lands)

    def body(*refs):
        land_refs = refs[n:2 * n]
        send_sems, recv_sems = refs[2 * n], refs[2 * n + 1]
        x, y, c = _mesh_pos()
        chips = [(1 - x, y), (x, 1 - y), (1 - x, 1 - y)]
        sends, recvs = [], []
        for a, land in enumerate(land_refs):
            for j, (px, py) in enumerate(chips):
                k = a * len(chips) + j
                mk = lambda slot: pltpu.make_async_remote_copy(
                    src_ref=land.at[slot], dst_ref=land.at[slot], send_sem=send_sems.at[k], recv_sem=recv_sems.at[k],
                    device_id=(x, y, 1 - c), device_id_type=MESH)
                sends.append(mk(4 * px + 2 * py + c))
                recvs.append(mk(4 * px + 2 * py + 1 - c))
        for cp in sends:
            cp.start()
        for cp in recvs:
            cp.wait_recv()
        for cp in sends:
            cp.wait_send()

    anyspec = pl.BlockSpec(memory_space=pl.ANY)
    return list(pl.pallas_call(
        body, name=name, out_shape=[jax.ShapeDtypeStruct(z.shape, z.dtype) for z in lands],
        in_specs=[anyspec] * n, out_specs=[anyspec] * n, input_output_aliases={i: i for i in range(n)},
        scratch_shapes=[pltpu.SemaphoreType.DMA((3 * n,)), pltpu.SemaphoreType.DMA((3 * n,))],
    )(*lands))


def _copy_wait(send_sems, recv_sems, srcs, lands, after, gather, name, rels=ALL_RELS):
    n = len(srcs)

    def body(*refs):
        src_refs, land_refs = refs[:n], refs[n:2 * n]
        s_sems, r_sems = refs[2 * n], refs[2 * n + 1]
        for out_copy, in_copy in _split_copies(src_refs, land_refs, s_sems, r_sems, gather, rels):
            out_copy.wait_send()
            in_copy.wait_recv()

    hbm = lambda a: pltpu.HBM(a.shape, a.dtype)
    res = pl.pallas_call(
        body, name=name, out_shape=(*[hbm(s) for s in srcs], *[hbm(z) for z in lands]),
        in_specs=[_HBM] * (2 * n) + [_SEM, _SEM, pl.BlockSpec(memory_space=pl.ANY)], out_specs=tuple([_HBM] * (2 * n)),
        input_output_aliases={i: i for i in range(2 * n)},
        compiler_params=pltpu.CompilerParams(has_side_effects=_EFFECT),
    )(*srcs, *lands, send_sems, recv_sems, after)
    return list(res[:n]), list(res[n:])


def _sum_parts(parts, name):
    P, R, W = parts.shape

    def body(p_ref, o_ref):
        g = p_ref[0].astype(F32)
        for i in range(1, P):
            g = g + p_ref[i].astype(F32)
        o_ref[...] = g

    return pl.pallas_call(body, name=name, out_shape=jax.ShapeDtypeStruct((R, W), F32))(parts)


def _adamw_body(p_ref, w_ref, m_ref, v_ref, g_ref, d_ref, nm_ref, nv_ref):
    g = p_ref[0].astype(F32)
    for i in range(1, p_ref.shape[0]):
        g = g + p_ref[i].astype(F32)
    _adamw_apply(g, w_ref, m_ref, v_ref, g_ref, d_ref, nm_ref, nv_ref)


def _adamw_apply(g, w_ref, m_ref, v_ref, g_ref, d_ref, nm_ref, nv_ref):
    m_new = ADAM_B1 * m_ref[...] + (1.0 - ADAM_B1) * g
    v_new = ADAM_B2 * v_ref[...] + (1.0 - ADAM_B2) * jnp.square(g)
    m_hat = m_new / (1.0 - ADAM_B1 ** ADAM_STEP)
    v_hat = v_new / (1.0 - ADAM_B2 ** ADAM_STEP)
    g_ref[...] = g
    d_ref[...] = -ADAM_LR * (m_hat / (jnp.sqrt(v_hat) + ADAM_EPS) + ADAM_WD * w_ref[...])
    nm_ref[...] = m_new
    nv_ref[...] = v_new


def _adamw(parts, w, m, v, name):
    P, R, W = parts.shape
    tr = _pick(R, (368, 192, 64, 16, 8))
    spec = pl.BlockSpec((tr, W), lambda i: (i, 0))
    return pl.pallas_call(
        functools.partial(_adamw_body), name=name, grid=(R // tr,),
        in_specs=[pl.BlockSpec((P, tr, W), lambda i: (0, i, 0)), spec, spec, spec], out_specs=[spec] * 4,
        out_shape=[jax.ShapeDtypeStruct((R, W), F32)] * 4,
        compiler_params=_cparams(("parallel",)),
    )(parts, w, m, v)


def _adamw_layers(parts, w, m, v, name):
    P, B, C_ = parts[0].shape
    tb = _pick(B, (256, 128))
    nb = B // tb

    def body(*refs):
        p_refs, rest = refs[:DEPTH], refs[DEPTH:]
        a = pl.program_id(0)
        for l in range(DEPTH):
            @pl.when(a == l)
            def _():
                _adamw_body(p_refs[l], *[r.at[0] for r in rest])

    spec = pl.BlockSpec((1, tb, C_), lambda a, i: (a, i, 0))

    def part_spec(l):
        return pl.BlockSpec((P, tb, C_), lambda a, i: (0, jnp.where(a == l, i, jnp.where(a < l, 0, nb - 1)), 0))

    return pl.pallas_call(
        body, name=name, grid=(DEPTH, nb),
        in_specs=[part_spec(l) for l in range(DEPTH)] + [spec, spec, spec], out_specs=[spec] * 4,
        out_shape=[jax.ShapeDtypeStruct((DEPTH, B, C_), F32)] * 4,
        compiler_params=_cparams(("arbitrary", "arbitrary")),
    )(*parts, w, m, v)


VEC_GROUPS = (("norm1_g", "norm2_g"), ("conv_b", "conv_ln_g", "conv_ln_b", "hgrn_lb_logits", "hgrn_norm_g"),
              ("q_a_norm_g",), ("kv_a_norm_g",), ("q_norm_g", "k_norm_g"))
SMALL_NAMES = tuple(n for grp in VEC_GROUPS for n in grp) + ("meta", "conv_w")


def _adamw_small(own, lands, wts, mom, var, name):
    n_in = len(own)

    def body(*refs):
        own_r, land_r = refs[:n_in], refs[n_in:2 * n_in]
        rest = iter(refs[2 * n_in:])
        wmv = {n: (next(rest), next(rest), next(rest)) for n in SMALL_NAMES}
        outs = {n: (next(rest), next(rest), next(rest), next(rest)) for n in SMALL_NAMES}
        loss_ref = next(rest)
        x, y, c = _mesh_pos()
        me = 4 * x + 2 * y + c

        def total(k):
            acc = None
            for s in range(N_DEV):
                v = jnp.where(me == s, own_r[k][...], land_r[k][s])
                acc = v if acc is None else acc + v
            return acc

        for k, grp in enumerate(VEC_GROUPS):
            tot = total(k)
            for j, n in enumerate(grp):
                _adamw_apply(tot[DEPTH * j:DEPTH * (j + 1)], *wmv[n], *outs[n])
        loss_ref[...] = total(len(VEC_GROUPS))
        _adamw_apply(total(n_in - 2), *wmv["meta"], *outs["meta"])
        _adamw_apply(total(n_in - 1), *wmv["conv_w"], *outs["conv_w"])

    args = list(own) + list(lands) + [d[n] for n in SMALL_NAMES for d in (wts, mom, var)]
    out_shape = [jax.ShapeDtypeStruct(wts[n].shape, F32) for n in SMALL_NAMES for _ in range(4)]
    res = pl.pallas_call(body, name=name, out_shape=out_shape + [jax.ShapeDtypeStruct((1, 128), F32)])(*args)
    out = {}
    for i, n in enumerate(SMALL_NAMES):
        for j, kind in enumerate(("grad_", "delta_", "new_m_", "new_v_")):
            out[kind + n] = res[4 * i + j]
    return out, res[-1]


PACK_W = 1024
BIG = (("w_in", (DEPTH, D_MODEL, N_IN // N_DEV), 2), ("w_conv_out", (DEPTH, CONV_DIM, D_MODEL // N_DEV), 2),
       ("w_uq", (DEPTH, Q_RANK, HEADS * QK_DIM // N_DEV), 2), ("w_ukv", (DEPTH, KV_RANK, HEADS * (NOPE + V_DIM) // N_DEV), 2),
       ("w_attn_out", (DEPTH, HEADS * V_DIM, D_MODEL // N_DEV), 2), ("w_hgrn_out", (DEPTH, 512, D_MODEL // N_DEV), 2),
       ("w_out", (DEPTH, D_MODEL // N_DEV, D_MODEL), 1), ("w_ff1", (DEPTH, D_MODEL, D_FF // N_DEV), 2),
       ("w_ff2", (DEPTH, D_FF // N_DEV, D_MODEL), 1))
SMALL_SHARDED = (("meta", (N_META, D_MODEL // N_DEV), 1), ("conv_w", (DEPTH, CONV_K, CONV_DIM // N_DEV), 2))
REPLICATED = (("norm1_g", (DEPTH, D_MODEL)), ("conv_b", (DEPTH, CONV_DIM)), ("conv_ln_g", (DEPTH, CONV_DIM)),
              ("conv_ln_b", (DEPTH, CONV_DIM)), ("q_a_norm_g", (DEPTH, Q_RANK)), ("kv_a_norm_g", (DEPTH, KV_RANK)),
              ("q_norm_g", (DEPTH, QK_DIM)), ("k_norm_g", (DEPTH, QK_DIM)), ("hgrn_lb_logits", (DEPTH, 512)),
              ("hgrn_norm_g", (DEPTH, 512)), ("norm2_g", (DEPTH, D_MODEL)))
WEIGHT_ORDER = ("meta", "norm1_g", "w_in", "conv_w", "conv_b", "conv_ln_g", "conv_ln_b", "w_conv_out", "q_a_norm_g", "w_uq",
                "kv_a_norm_g", "w_ukv", "q_norm_g", "k_norm_g", "w_attn_out", "hgrn_lb_logits", "hgrn_norm_g", "w_hgrn_out",
                "w_out", "norm2_g", "w_ff1", "w_ff2")


def _rows_for(n_elems, mult):
    rows = -(-n_elems // PACK_W)
    return -(-rows // mult) * mult


def _pack(arrays, dtype, mult, lead=()):
    nl = len(lead)
    flat = jnp.concatenate([a.reshape(lead + (-1,)).astype(dtype) for a in arrays], axis=nl)
    rows = _rows_for(flat.shape[nl], mult)
    flat = jnp.pad(flat, [(0, 0)] * nl + [(0, rows * PACK_W - flat.shape[nl])])
    return flat.reshape(lead + (rows, PACK_W))


def _unpack(pack, shapes, lead=()):
    nl = len(lead)
    flat = pack.reshape(lead + (-1,))
    out, off = [], 0
    for shp in shapes:
        n = int(np.prod(shp))
        out.append(lax.slice_in_dim(flat, off, off + n, axis=nl).reshape(lead + tuple(shp)))
        off += n
    return out


def _join_shards(g, axis):
    g = jnp.moveaxis(g, 0, axis)
    shp = g.shape
    return g.reshape(shp[:axis] + (shp[axis] * shp[axis + 1],) + shp[axis + 2:])


def _cut_shards(a, axis):
    shp = a.shape
    a = a.reshape(shp[:axis] + (N_DEV, shp[axis] // N_DEV) + shp[axis + 1:])
    return jnp.moveaxis(a, axis, 0)


def kernel(x, meta, norm1_g, w_in, conv_w, conv_b, conv_ln_g, conv_ln_b, w_conv_out, q_a_norm_g, w_uq, kv_a_norm_g, w_ukv, q_norm_g, k_norm_g, w_attn_out, hgrn_lb_logits, hgrn_norm_g, w_hgrn_out, w_out, norm2_g, w_ff1, w_ff2, loss_target, m_meta, m_norm1_g, m_w_in, m_conv_w, m_conv_b, m_conv_ln_g, m_conv_ln_b, m_w_conv_out, m_q_a_norm_g, m_w_uq, m_kv_a_norm_g, m_w_ukv, m_q_norm_g, m_k_norm_g, m_w_attn_out, m_hgrn_lb_logits, m_hgrn_norm_g, m_w_hgrn_out, m_w_out, m_norm2_g, m_w_ff1, m_w_ff2, v_meta, v_norm1_g, v_w_in, v_conv_w, v_conv_b, v_conv_ln_g, v_conv_ln_b, v_w_conv_out, v_q_a_norm_g, v_w_uq, v_kv_a_norm_g, v_w_ukv, v_q_norm_g, v_k_norm_g, v_w_attn_out, v_hgrn_lb_logits, v_hgrn_norm_g, v_w_hgrn_out, v_w_out, v_norm2_g, v_w_ff1, v_w_ff2):
    args = dict(locals())
    wts = {n: args[n] for n in WEIGHT_ORDER}
    mom = {n: args["m_" + n] for n in WEIGHT_ORDER}
    var = {n: args["v_" + n] for n in WEIGHT_ORDER}
    xi, yi, ci = _mesh_pos()
    me = 4 * xi + 2 * yi + ci

    shard = lambda l: [wts[n][l].astype(BF16) for n, _, _ in BIG]
    assert BIG[0][0] == "w_in"
    gathered = _all_gather(shard(0)[:1] + [_pack([wts[n] for n, _, _ in SMALL_SHARDED], F32, 8)], "gather_layer0")
    small = dict(zip([n for n, _, _ in SMALL_SHARDED],
                     [_join_shards(g, axis) for (_, _, axis), g in
                      zip(SMALL_SHARDED, _unpack(gathered[-1], [s for _, s, _ in SMALL_SHARDED], (N_DEV,)))]))
    rest0 = _copy_start(shard(0)[1:], True, "gather_rest0_start", 11, NEAR_RELS)
    pending = []

    def joined(mats, names_axes):
        full = {}
        for (n, _, axis), g in zip(names_axes, mats):
            if n == "w_in":
                full["w_in_padded"] = _w_in_from_shards(g)
            else:
                full[n] = _join_shards(g, axis - 1)
        return full

    def rest_of_layer0(u):
        own, lands = _copy_wait(rest0[0], rest0[1], rest0[2], rest0[3], u, True, "gather_rest0_wait", NEAR_RELS)
        lands = _forward_to_sibling(lands, "gather_rest0_forward")
        full = joined([lax.dynamic_update_index_in_dim(z, s, me, 0) for z, s in zip(lands, own)], BIG[1:])
        pending.append(_copy_start(shard(1), True, "gather_layer1_start", 5))
        full["w_conv_out"] = _after(full["w_conv_out"], pending[0][4])
        return _layer_weights(full)

    def layer_weights(l, xp):
        full = {n: wts[n][l] for n, _ in REPLICATED}
        full["conv_w"] = small["conv_w"][l]
        if l == 0:
            full.update(joined(gathered[:1], BIG[:1]))
            full["norm1_g"] = _after(full["norm1_g"], rest0[4])
            full["late"] = rest_of_layer0
        else:
            s_sems, r_sems, sent, lands, _ = pending[0]
            own, lands = _copy_wait(s_sems, r_sems, sent, lands, xp, True, "gather_layer1_wait")
            full.update(joined([lax.dynamic_update_index_in_dim(z, s, me, 0) for z, s in zip(lands, own)], BIG))
        return full, xp

    big_names = [n for n, _, _ in BIG]
    early = [n for n in big_names if n in ("w_out", "w_ff1", "w_ff2")]
    late = [n for n in big_names if n not in early]
    cut = lambda g, names: [(g[n + "_shards"] if n + "_shards" in g else _cut_shards(g[n], axis - 1)).astype(BF16)
                            for n, _, axis in BIG if n in names]
    layer_grads = [None] * DEPTH
    flight = {}

    def layer_mid(l, g, w_out):
        if l == 0:
            flight["l0_early"] = _copy_start(cut(g, early), False, "scatter_layer0_early_start", 7)
            w_out = _after(w_out, flight["l0_early"][4])
        return w_out

    def layer_done(l, g, dx, below):
        layer_grads[l] = g
        if l == 1:
            flight["l1"] = _copy_start(cut(g, big_names), False, "scatter_l1_start", 6)
            below["norm2_g"] = _after(below["norm2_g"], flight["l1"][4])
        return dx

    def layer_matrices(l, mats, norm_g):
        if l == 0:
            flight["l0_late"] = _copy_start(cut(mats, late), False, "scatter_l0_late_start", 8)
            norm_g = _after(norm_g, flight["l0_late"][4])
        return norm_g

    loss, grad_x, g_meta, g_lb = _run_step(x[0], loss_target[0], small["meta"], wts["hgrn_lb_logits"],
                                           layer_weights, layer_done, layer_mid, layer_matrices)

    grads = {k: jnp.stack([layer_grads[l][k] for l in range(DEPTH)]) for k in layer_grads[0]
             if k not in big_names and not k.endswith("_shards")}
    grads["hgrn_lb_logits"] = g_lb
    own = [jnp.concatenate([grads[n] for n in grp], axis=0) for grp in VEC_GROUPS]
    own.append(jnp.broadcast_to(loss.reshape(1, 1), (1, 128)))
    flight["small"] = _copy_start(own, True, "gather_small_grads_start", 9)
    cuts = [_cut_shards(g_meta, 1), _cut_shards(grads["conv_w"], 2)]
    flight["small_x"] = _copy_start(cuts, False, "scatter_small_grads_start", 10)
    started = flight["small_x"][4]

    def arrive(key, names, after):
        s_sems, r_sems, sent, lands, _ = flight[key]
        sent, lands = _copy_wait(s_sems, r_sems, sent, lands, after, False, f"scatter_{key}_wait")
        return {n: lax.dynamic_update_index_in_dim(z, lax.dynamic_index_in_dim(s, me, 0, keepdims=False), me, 0)
                for n, z, s in zip(names, lands, sent)}

    out = {}

    def update(names, recv0, recv1):
        for n in names:
            res4 = _adamw_layers([recv0[n], recv1[n]], wts[n], mom[n], var[n], "adamw_" + n)
            for kind, a in zip(("grad_", "delta_", "new_m_", "new_v_"), res4):
                out[kind + n] = a

    recv1 = arrive("l1", big_names, started)
    recv0 = arrive("l0_early", early, started)
    update(early, recv0, recv1)

    s_sems, r_sems, sent, lands, _ = flight["small"]
    updated = lax.optimization_barrier(tuple(out["grad_" + n] for n in early))
    own, lands = _copy_wait(s_sems, r_sems, sent, lands, updated[0], True, "gather_small_grads_wait")
    s_sems, r_sems, sent, lands_x, _ = flight["small_x"]
    sent, lands_x = _copy_wait(s_sems, r_sems, sent, lands_x, updated[0], False, "scatter_small_grads_wait")
    own += [lax.dynamic_index_in_dim(s, me, 0, keepdims=False) for s in sent]
    small_out, loss = _adamw_small(own, lands + lands_x, wts, mom, var, "adamw_small")
    out.update(small_out)
    loss = loss[0, 0]

    recv0 = arrive("l0_late", late, small_out["grad_norm1_g"])
    update(late, recv0, recv1)

    res = [loss, grad_x[None]]
    for kind in ("grad_", "delta_", "new_m_", "new_v_"):
        res += [out[kind + n] for n in WEIGHT_ORDER]
    return tuple(res)
```
